```python
import jax, jax.numpy as jnp
from jax import lax
import numpy as np

D_MODEL = 1024
BATCH = 8
SEQ = 8192
DEPTH = 1

HEAD_DIM = 64
N_FOX_HEADS = 8
N_SB_HEADS = 8
FOX_WIDTH = N_FOX_HEADS * HEAD_DIM
SB_WIDTH = N_SB_HEADS * HEAD_DIM
MIX_WIDTH = FOX_WIDTH + SB_WIDTH
IN_COLS = 3 * FOX_WIDTH + 3 * SB_WIDTH + N_FOX_HEADS
Q_BLOCK = 128
N_MEM = 256
N_MEM_HEADS = 4
MEM_HEAD_DIM = D_MODEL // N_MEM_HEADS
D_FF = ((8 * D_MODEL // 3 + 255) // 256) * 256
CONV_WIDTH = 3
EPS = 1e-6

kernel_name = "hybrid_fox_stickbreak_memxattn_convffn"


def _rmsnorm(x, g):
    xf = x.astype(jnp.float32)
    y = xf * lax.rsqrt(jnp.mean(xf * xf, axis=-1, keepdims=True) + EPS)
    return (y * g.astype(jnp.float32)).astype(x.dtype)


def _heads(t, n):
    b, s, w = t.shape
    return t.reshape(b, s, n, w // n).transpose(0, 2, 1, 3)


def _merge(t):
    b, n, s, d = t.shape
    return t.transpose(0, 2, 1, 3).reshape(b, s, n * d)


def _sweep(fn, q, *per_query):
    b, h, s, d = q.shape
    n = s // Q_BLOCK
    qb = q.reshape(b, h, n, Q_BLOCK, d).transpose(2, 0, 1, 3, 4)
    extras = [e.reshape(b, h, n, Q_BLOCK).transpose(2, 0, 1, 3) for e in per_query]
    out = lax.map(lambda a: fn(*a), (jnp.arange(n), qb, *extras))
    return out.transpose(1, 2, 0, 3, 4).reshape(b, h, s, d)


def _fox_attention(q, k, v, log_f):
    s_len = q.shape[2]
    scale = HEAD_DIM ** -0.5
    c = jnp.cumsum(log_f.astype(jnp.float32), axis=-1)
    kpos = jnp.arange(s_len)

    def block(i, qi, ci):
        qpos = i * Q_BLOCK + jnp.arange(Q_BLOCK)
        logits = jnp.einsum('bhqd,bhkd->bhqk', qi, k, preferred_element_type=jnp.float32) * scale
        logits = logits + ci[..., None] - c[:, :, None, :]
        logits = jnp.where(kpos[None, :] <= qpos[:, None], logits, -jnp.inf)
        p = jax.nn.softmax(logits, axis=-1)
        return jnp.einsum('bhqk,bhkd->bhqd', p.astype(v.dtype), v)

    return _sweep(block, q, c)


def _stick_breaking_attention(q, k, v):
    s_len = q.shape[2]
    scale = HEAD_DIM ** -0.5
    kpos = jnp.arange(s_len)

    def block(i, qi):
        qpos = i * Q_BLOCK + jnp.arange(Q_BLOCK)
        z = jnp.einsum('bhqd,bhkd->bhqk', qi, k, preferred_element_type=jnp.float32) * scale
        strict = kpos[None, :] < qpos[:, None]
        log_keep = jnp.where(strict, jax.nn.log_sigmoid(-z), 0.0)
        after = lax.cumsum(log_keep, axis=3, reverse=True) - log_keep
        a = jnp.where(strict, jnp.exp(jax.nn.log_sigmoid(z) + after), 0.0)
        return jnp.einsum('bhqk,bhkd->bhqd', a.astype(v.dtype), v)

    return _sweep(block, q)


def _memory_cross_attention(h, m, w_q, w_kv, w_o):
    q = _heads(jnp.einsum('bsd,dc->bsc', h, w_q), N_MEM_HEADS)
    kv = jnp.einsum('bmd,dc->bmc', m, w_kv)
    k, v = jnp.split(kv, 2, axis=-1)
    k, v = _heads(k, N_MEM_HEADS), _heads(v, N_MEM_HEADS)
    logits = jnp.einsum('bhqd,bhkd->bhqk', q, k, preferred_element_type=jnp.float32) * (MEM_HEAD_DIM ** -0.5)
    p = jax.nn.softmax(logits, axis=-1)
    o = jnp.einsum('bhqk,bhkd->bhqd', p.astype(v.dtype), v)
    return jnp.einsum('bsc,cd->bsd', _merge(o), w_o)


def _causal_dwconv(u, w, b):
    s_len = u.shape[1]
    up = jnp.pad(u, ((0, 0), (CONV_WIDTH - 1, 0), (0, 0)))
    y = b
    for i in range(CONV_WIDTH):
        y = y + up[:, i:i + s_len] * w[i]
    return y


def _fwd_setup_inputs(seed: int = 0) -> dict:
    key = jax.random.key(seed)
    ks = jax.random.split(key, 24)
    f32 = jnp.float32

    def nrm(k, shape, fan_in):
        return jax.random.normal(k, shape, f32) * (fan_in ** -0.5)

    def gain(k, shape):
        return 1.0 + 0.02 * jax.random.normal(k, shape, f32)

    return {
        "x": jax.random.normal(ks[0], (BATCH, SEQ, D_MODEL), f32),
        "mem": jax.random.normal(ks[1], (BATCH, N_MEM, D_MODEL), f32),
        "attn_norm_g": gain(ks[2], (DEPTH, D_MODEL)),
        "w_in": nrm(ks[3], (DEPTH, D_MODEL, IN_COLS), D_MODEL),
        "b_forget": jnp.linspace(1.0, 6.0, N_FOX_HEADS, dtype=f32)[None, :]
                    + 0.1 * jax.random.normal(ks[4], (DEPTH, N_FOX_HEADS), f32),
        "fox_out_g": gain(ks[5], (DEPTH, FOX_WIDTH)),
        "sb_out_g": gain(ks[6], (DEPTH, SB_WIDTH)),
        "w_out": nrm(ks[7], (DEPTH, MIX_WIDTH, D_MODEL), MIX_WIDTH),
        "xattn_norm_g": gain(ks[8], (DEPTH, D_MODEL)),
        "mem_norm_g": gain(ks[9], (DEPTH, D_MODEL)),
        "w_mq": nrm(ks[10], (DEPTH, D_MODEL, D_MODEL), D_MODEL),
        "w_mkv": nrm(ks[11], (DEPTH, D_MODEL, 2 * D_MODEL), D_MODEL),
        "w_mo": nrm(ks[12], (DEPTH, D_MODEL, D_MODEL), D_MODEL),
        "ffn_norm_g": gain(ks[13], (DEPTH, D_MODEL)),
        "w_up": nrm(ks[14], (DEPTH, D_MODEL, 2 * D_FF), D_MODEL),
        "conv_w": nrm(ks[15], (DEPTH, CONV_WIDTH, 2 * D_FF), CONV_WIDTH),
        "conv_b": 0.02 * jax.random.normal(ks[16], (DEPTH, 2 * D_FF), f32),
        "w_down": nrm(ks[17], (DEPTH, D_FF, D_MODEL), D_FF),
        "final_norm_g": gain(ks[18], (D_MODEL,)),
    }


def _fwd_reference(x, mem, attn_norm_g, w_in, b_forget, fox_out_g, sb_out_g, w_out,
              xattn_norm_g, mem_norm_g, w_mq, w_mkv, w_mo,
              ffn_norm_g, w_up, conv_w, conv_b, w_down, final_norm_g):
    split_at = np.cumsum([FOX_WIDTH, FOX_WIDTH, FOX_WIDTH, SB_WIDTH, SB_WIDTH, SB_WIDTH]).tolist()
    for l in range(DEPTH):
        h = _rmsnorm(x, attn_norm_g[l])
        proj = jnp.einsum('bsd,dc->bsc', h, w_in[l])
        fq, fk, fv, sq, sk, sv, f_logit = jnp.split(proj, split_at, axis=-1)
        log_f = jax.nn.log_sigmoid(f_logit.astype(jnp.float32) + b_forget[l].astype(jnp.float32))
        log_f = log_f.transpose(0, 2, 1)
        fox_o = _merge(_fox_attention(_heads(fq, N_FOX_HEADS), _heads(fk, N_FOX_HEADS),
                                      _heads(fv, N_FOX_HEADS), log_f))
        sb_o = _merge(_stick_breaking_attention(_heads(sq, N_SB_HEADS), _heads(sk, N_SB_HEADS),
                                                _heads(sv, N_SB_HEADS)))
        mixed = jnp.concatenate([_rmsnorm(fox_o, fox_out_g[l]), _rmsnorm(sb_o, sb_out_g[l])], axis=-1)
        x = x + jnp.einsum('bsc,cd->bsd', mixed, w_out[l])

        h = _rmsnorm(x, xattn_norm_g[l])
        m = _rmsnorm(mem, mem_norm_g[l])
        x = x + _memory_cross_attention(h, m, w_mq[l], w_mkv[l], w_mo[l])

        h = _rmsnorm(x, ffn_norm_g[l])
        u = jnp.einsum('bsd,df->bsf', h, w_up[l])
        u = _causal_dwconv(u, conv_w[l], conv_b[l])
        gate, val = jnp.split(u, 2, axis=-1)
        x = x + jnp.einsum('bsf,fd->bsd', jax.nn.silu(gate) * val, w_down[l])
    return _rmsnorm(x, final_norm_g)


import jax as _jax
import jax.numpy as _jnp

TWIN_FORMAT = 'train_step'
FWD_PARAMS = ['x', 'mem', 'attn_norm_g', 'w_in', 'b_forget', 'fox_out_g', 'sb_out_g', 'w_out', 'xattn_norm_g', 'mem_norm_g', 'w_mq', 'w_mkv', 'w_mo', 'ffn_norm_g', 'w_up', 'conv_w', 'conv_b', 'w_down', 'final_norm_g']
TWIN_WEIGHTS = ['attn_norm_g', 'w_in', 'b_forget', 'fox_out_g', 'sb_out_g', 'w_out', 'xattn_norm_g', 'mem_norm_g', 'w_mq', 'w_mkv', 'w_mo', 'ffn_norm_g', 'w_up', 'conv_w', 'conv_b', 'w_down', 'final_norm_g']
TWIN_DIFF_INPUT = 'x'
TWIN_INPUTS = ['x', 'mem', 'attn_norm_g', 'w_in', 'b_forget', 'fox_out_g', 'sb_out_g', 'w_out', 'xattn_norm_g', 'mem_norm_g', 'w_mq', 'w_mkv', 'w_mo', 'ffn_norm_g', 'w_up', 'conv_w', 'conv_b', 'w_down', 'final_norm_g', 'loss_target', 'm_attn_norm_g', 'm_w_in', 'm_b_forget', 'm_fox_out_g', 'm_sb_out_g', 'm_w_out', 'm_xattn_norm_g', 'm_mem_norm_g', 'm_w_mq', 'm_w_mkv', 'm_w_mo', 'm_ffn_norm_g', 'm_w_up', 'm_conv_w', 'm_conv_b', 'm_w_down', 'm_final_norm_g', 'v_attn_norm_g', 'v_w_in', 'v_b_forget', 'v_fox_out_g', 'v_sb_out_g', 'v_w_out', 'v_xattn_norm_g', 'v_mem_norm_g', 'v_w_mq', 'v_w_mkv', 'v_w_mo', 'v_ffn_norm_g', 'v_w_up', 'v_conv_w', 'v_conv_b', 'v_w_down', 'v_final_norm_g']
TWIN_OUTPUTS = ['loss', 'grad_x', 'grad_attn_norm_g', 'grad_w_in', 'grad_b_forget', 'grad_fox_out_g', 'grad_sb_out_g', 'grad_w_out', 'grad_xattn_norm_g', 'grad_mem_norm_g', 'grad_w_mq', 'grad_w_mkv', 'grad_w_mo', 'grad_ffn_norm_g', 'grad_w_up', 'grad_conv_w', 'grad_conv_b', 'grad_w_down', 'grad_final_norm_g', 'delta_attn_norm_g', 'delta_w_in', 'delta_b_forget', 'delta_fox_out_g', 'delta_sb_out_g', 'delta_w_out', 'delta_xattn_norm_g', 'delta_mem_norm_g', 'delta_w_mq', 'delta_w_mkv', 'delta_w_mo', 'delta_ffn_norm_g', 'delta_w_up', 'delta_conv_w', 'delta_conv_b', 'delta_w_down', 'delta_final_norm_g', 'new_m_attn_norm_g', 'new_m_w_in', 'new_m_b_forget', 'new_m_fox_out_g', 'new_m_sb_out_g', 'new_m_w_out', 'new_m_xattn_norm_g', 'new_m_mem_norm_g', 'new_m_w_mq', 'new_m_w_mkv', 'new_m_w_mo', 'new_m_ffn_norm_g', 'new_m_w_up', 'new_m_conv_w', 'new_m_conv_b', 'new_m_w_down', 'new_m_final_norm_g', 'new_v_attn_norm_g', 'new_v_w_in', 'new_v_b_forget', 'new_v_fox_out_g', 'new_v_sb_out_g', 'new_v_w_out', 'new_v_xattn_norm_g', 'new_v_mem_norm_g', 'new_v_w_mq', 'new_v_w_mkv', 'new_v_w_mo', 'new_v_ffn_norm_g', 'new_v_w_up', 'new_v_conv_w', 'new_v_conv_b', 'new_v_w_down', 'new_v_final_norm_g']
TWIN_LEAF_KINDS = {'loss': 'loss', 'grad_x': 'grad_x', 'grad_attn_norm_g': 'grad_w', 'grad_w_in': 'grad_w', 'grad_b_forget': 'grad_w', 'grad_fox_out_g': 'grad_w', 'grad_sb_out_g': 'grad_w', 'grad_w_out': 'grad_w', 'grad_xattn_norm_g': 'grad_w', 'grad_mem_norm_g': 'grad_w', 'grad_w_mq': 'grad_w', 'grad_w_mkv': 'grad_w', 'grad_w_mo': 'grad_w', 'grad_ffn_norm_g': 'grad_w', 'grad_w_up': 'grad_w', 'grad_conv_w': 'grad_w', 'grad_conv_b': 'grad_w', 'grad_w_down': 'grad_w', 'grad_final_norm_g': 'grad_w', 'delta_attn_norm_g': 'delta_w', 'delta_w_in': 'delta_w', 'delta_b_forget': 'delta_w', 'delta_fox_out_g': 'delta_w', 'delta_sb_out_g': 'delta_w', 'delta_w_out': 'delta_w', 'delta_xattn_norm_g': 'delta_w', 'delta_mem_norm_g': 'delta_w', 'delta_w_mq': 'delta_w', 'delta_w_mkv': 'delta_w', 'delta_w_mo': 'delta_w', 'delta_ffn_norm_g': 'delta_w', 'delta_w_up': 'delta_w', 'delta_conv_w': 'delta_w', 'delta_conv_b': 'delta_w', 'delta_w_down': 'delta_w', 'delta_final_norm_g': 'delta_w', 'new_m_attn_norm_g': 'new_m', 'new_m_w_in': 'new_m', 'new_m_b_forget': 'new_m', 'new_m_fox_out_g': 'new_m', 'new_m_sb_out_g': 'new_m', 'new_m_w_out': 'new_m', 'new_m_xattn_norm_g': 'new_m', 'new_m_mem_norm_g': 'new_m', 'new_m_w_mq': 'new_m', 'new_m_w_mkv': 'new_m', 'new_m_w_mo': 'new_m', 'new_m_ffn_norm_g': 'new_m', 'new_m_w_up': 'new_m', 'new_m_conv_w': 'new_m', 'new_m_conv_b': 'new_m', 'new_m_w_down': 'new_m', 'new_m_final_norm_g': 'new_m', 'new_v_attn_norm_g': 'new_v', 'new_v_w_in': 'new_v', 'new_v_b_forget': 'new_v', 'new_v_fox_out_g': 'new_v', 'new_v_sb_out_g': 'new_v', 'new_v_w_out': 'new_v', 'new_v_xattn_norm_g': 'new_v', 'new_v_mem_norm_g': 'new_v', 'new_v_w_mq': 'new_v', 'new_v_w_mkv': 'new_v', 'new_v_w_mo': 'new_v', 'new_v_ffn_norm_g': 'new_v', 'new_v_w_up': 'new_v', 'new_v_conv_w': 'new_v', 'new_v_conv_b': 'new_v', 'new_v_w_down': 'new_v', 'new_v_final_norm_g': 'new_v'}


def _forward(args):
    return _fwd_reference(*[args[k] for k in FWD_PARAMS])


def _output_shape():
    def fwd():
        inp = _fwd_setup_inputs(0)
        return _fwd_reference(*[inp[k] for k in FWD_PARAMS])
    out = _jax.eval_shape(fwd)
    return out.shape, out.dtype

N_MICROBATCH = 1
ADAM_LR = 0.001
ADAM_B1 = 0.9
ADAM_B2 = 0.999
ADAM_EPS = 1e-08
ADAM_WD = 0.01
ADAM_STEP = 10
PER_EXAMPLE_BATCH_AXIS = {'x': 0, 'mem': 0, 'loss_target': 0}
SHARED_INPUTS = []
_WEIGHT_DTYPES = {'attn_norm_g': _jnp.float32, 'w_in': _jnp.float32, 'b_forget': _jnp.float32, 'fox_out_g': _jnp.float32, 'sb_out_g': _jnp.float32, 'w_out': _jnp.float32, 'xattn_norm_g': _jnp.float32, 'mem_norm_g': _jnp.float32, 'w_mq': _jnp.float32, 'w_mkv': _jnp.float32, 'w_mo': _jnp.float32, 'ffn_norm_g': _jnp.float32, 'w_up': _jnp.float32, 'conv_w': _jnp.float32, 'conv_b': _jnp.float32, 'w_down': _jnp.float32, 'final_norm_g': _jnp.float32}
MOMENT_SCALE = {'attn_norm_g': 2.825582e-01, 'w_in': 1.498349e-01, 'b_forget': 2.624450e+00, 'fox_out_g': 2.305416e-01, 'sb_out_g': 1.957681e-01, 'w_out': 1.930168e-01, 'xattn_norm_g': 2.018592e-02, 'mem_norm_g': 3.098938e-02, 'w_mq': 1.988853e-02, 'w_mkv': 2.003081e-02, 'w_mo': 2.031934e-02, 'ffn_norm_g': 1.509050e-01, 'w_up': 6.130788e-02, 'conv_w': 6.256054e-02, 'conv_b': 6.135867e-02, 'w_down': 1.012209e-01, 'final_norm_g': 6.405889e+01}


def _to_microbatches(a, axis):
    t = _jnp.moveaxis(a, axis, 0)
    t = t.reshape((N_MICROBATCH, t.shape[0] // N_MICROBATCH) + t.shape[1:])
    return _jnp.moveaxis(t, 1, axis + 1)


def setup_inputs(seed: int = 0) -> dict:
    inp = _fwd_setup_inputs(seed)
    key = _jax.random.fold_in(_jax.random.key(seed), 7919)
    shape, _ = _output_shape()
    out = dict(inp)
    out["loss_target"] = _jax.random.normal(_jax.random.fold_in(key, 0), shape, _jnp.float32)
    for i, name in enumerate(TWIN_WEIGHTS):
        w = inp[name].astype(_jnp.float32)
        if MOMENT_SCALE is None:
            s = _jnp.sqrt(_jnp.mean(_jnp.square(w)) + 1e-30)
        else:
            s = MOMENT_SCALE[name]
        km, kv = _jax.random.split(_jax.random.fold_in(key, i + 1))
        out[name] = w
        out["m_" + name] = s * _jax.random.normal(km, w.shape, _jnp.float32)
        out["v_" + name] = (s * s) * _jax.random.uniform(kv, w.shape, _jnp.float32, 0.5, 1.5)
    if N_MICROBATCH > 1:
        for name, axis in PER_EXAMPLE_BATCH_AXIS.items():
            out[name] = _to_microbatches(out[name], axis)
    return {'x': out['x'], 'mem': out['mem'], 'attn_norm_g': out['attn_norm_g'], 'w_in': out['w_in'], 'b_forget': out['b_forget'], 'fox_out_g': out['fox_out_g'], 'sb_out_g': out['sb_out_g'], 'w_out': out['w_out'], 'xattn_norm_g': out['xattn_norm_g'], 'mem_norm_g': out['mem_norm_g'], 'w_mq': out['w_mq'], 'w_mkv': out['w_mkv'], 'w_mo': out['w_mo'], 'ffn_norm_g': out['ffn_norm_g'], 'w_up': out['w_up'], 'conv_w': out['conv_w'], 'conv_b': out['conv_b'], 'w_down': out['w_down'], 'final_norm_g': out['final_norm_g'], 'loss_target': out['loss_target'], 'm_attn_norm_g': out['m_attn_norm_g'], 'm_w_in': out['m_w_in'], 'm_b_forget': out['m_b_forget'], 'm_fox_out_g': out['m_fox_out_g'], 'm_sb_out_g': out['m_sb_out_g'], 'm_w_out': out['m_w_out'], 'm_xattn_norm_g': out['m_xattn_norm_g'], 'm_mem_norm_g': out['m_mem_norm_g'], 'm_w_mq': out['m_w_mq'], 'm_w_mkv': out['m_w_mkv'], 'm_w_mo': out['m_w_mo'], 'm_ffn_norm_g': out['m_ffn_norm_g'], 'm_w_up': out['m_w_up'], 'm_conv_w': out['m_conv_w'], 'm_conv_b': out['m_conv_b'], 'm_w_down': out['m_w_down'], 'm_final_norm_g': out['m_final_norm_g'], 'v_attn_norm_g': out['v_attn_norm_g'], 'v_w_in': out['v_w_in'], 'v_b_forget': out['v_b_forget'], 'v_fox_out_g': out['v_fox_out_g'], 'v_sb_out_g': out['v_sb_out_g'], 'v_w_out': out['v_w_out'], 'v_xattn_norm_g': out['v_xattn_norm_g'], 'v_mem_norm_g': out['v_mem_norm_g'], 'v_w_mq': out['v_w_mq'], 'v_w_mkv': out['v_w_mkv'], 'v_w_mo': out['v_w_mo'], 'v_ffn_norm_g': out['v_ffn_norm_g'], 'v_w_up': out['v_w_up'], 'v_conv_w': out['v_conv_w'], 'v_conv_b': out['v_conv_b'], 'v_w_down': out['v_w_down'], 'v_final_norm_g': out['v_final_norm_g']}


def _loss(weights, diff, rest, loss_target):
    with _jax.named_scope("forward"):
        args = {**rest, TWIN_DIFF_INPUT: diff, **{k: w.astype(_WEIGHT_DTYPES[k]) for k, w in weights.items()}}
        y = _forward(args)
    with _jax.named_scope("loss_head"):
        err = _jnp.square(y.astype(_jnp.float32) - loss_target)
        return 0.5 * _jnp.sum(_jnp.mean(err, axis=-1)) if err.ndim else 0.5 * err


def _adamw(w, g, m, v):
    m = ADAM_B1 * m + (1.0 - ADAM_B1) * g
    v = ADAM_B2 * v + (1.0 - ADAM_B2) * _jnp.square(g)
    m_hat = m / (1.0 - ADAM_B1 ** ADAM_STEP)
    v_hat = v / (1.0 - ADAM_B2 ** ADAM_STEP)
    delta = -ADAM_LR * (m_hat / (_jnp.sqrt(v_hat) + ADAM_EPS) + ADAM_WD * w)
    return delta, m, v


def reference(x, mem, attn_norm_g, w_in, b_forget, fox_out_g, sb_out_g, w_out, xattn_norm_g, mem_norm_g, w_mq, w_mkv, w_mo, ffn_norm_g, w_up, conv_w, conv_b, w_down, final_norm_g, loss_target, m_attn_norm_g, m_w_in, m_b_forget, m_fox_out_g, m_sb_out_g, m_w_out, m_xattn_norm_g, m_mem_norm_g, m_w_mq, m_w_mkv, m_w_mo, m_ffn_norm_g, m_w_up, m_conv_w, m_conv_b, m_w_down, m_final_norm_g, v_attn_norm_g, v_w_in, v_b_forget, v_fox_out_g, v_sb_out_g, v_w_out, v_xattn_norm_g, v_mem_norm_g, v_w_mq, v_w_mkv, v_w_mo, v_ffn_norm_g, v_w_up, v_conv_w, v_conv_b, v_w_down, v_final_norm_g):
    given = dict(x=x, mem=mem, attn_norm_g=attn_norm_g, w_in=w_in, b_forget=b_forget, fox_out_g=fox_out_g, sb_out_g=sb_out_g, w_out=w_out, xattn_norm_g=xattn_norm_g, mem_norm_g=mem_norm_g, w_mq=w_mq, w_mkv=w_mkv, w_mo=w_mo, ffn_norm_g=ffn_norm_g, w_up=w_up, conv_w=conv_w, conv_b=conv_b, w_down=w_down, final_norm_g=final_norm_g, loss_target=loss_target, m_attn_norm_g=m_attn_norm_g, m_w_in=m_w_in, m_b_forget=m_b_forget, m_fox_out_g=m_fox_out_g, m_sb_out_g=m_sb_out_g, m_w_out=m_w_out, m_xattn_norm_g=m_xattn_norm_g, m_mem_norm_g=m_mem_norm_g, m_w_mq=m_w_mq, m_w_mkv=m_w_mkv, m_w_mo=m_w_mo, m_ffn_norm_g=m_ffn_norm_g, m_w_up=m_w_up, m_conv_w=m_conv_w, m_conv_b=m_conv_b, m_w_down=m_w_down, m_final_norm_g=m_final_norm_g, v_attn_norm_g=v_attn_norm_g, v_w_in=v_w_in, v_b_forget=v_b_forget, v_fox_out_g=v_fox_out_g, v_sb_out_g=v_sb_out_g, v_w_out=v_w_out, v_xattn_norm_g=v_xattn_norm_g, v_mem_norm_g=v_mem_norm_g, v_w_mq=v_w_mq, v_w_mkv=v_w_mkv, v_w_mo=v_w_mo, v_ffn_norm_g=v_ffn_norm_g, v_w_up=v_w_up, v_conv_w=v_conv_w, v_conv_b=v_conv_b, v_w_down=v_w_down, v_final_norm_g=v_final_norm_g)
    weights = {n: given[n] for n in TWIN_WEIGHTS}
    shared = {n: given[n] for n in SHARED_INPUTS}
    per_example = {n: given[n] for n in ['x', 'mem']}
    grad_fn = _jax.value_and_grad(_loss, argnums=(0, 1))

    def one_microbatch(ex, loss_target):
        ex = dict(ex)
        diff = ex.pop(TWIN_DIFF_INPUT)
        return grad_fn(weights, diff, {**shared, **ex}, loss_target)

    if N_MICROBATCH == 1:
        loss, (grad_w, grad_x) = one_microbatch(per_example, given["loss_target"])
    else:
        def body(carry, xs):
            loss_sum, grad_sum = carry
            l_k, (gw_k, gx_k) = one_microbatch(xs[0], xs[1])
            with _jax.named_scope("update"):
                return (loss_sum + l_k, _jax.tree.map(_jnp.add, grad_sum, gw_k)), gx_k

        init = (_jnp.zeros((), _jnp.float32), _jax.tree.map(_jnp.zeros_like, weights))
        (loss, grad_w), grad_x = _jax.lax.scan(body, init, (per_example, given["loss_target"]))
    with _jax.named_scope("update"):
        delta_w, new_m, new_v = {}, {}, {}
        for n in TWIN_WEIGHTS:
            delta_w[n], new_m[n], new_v[n] = _adamw(weights[n], grad_w[n], given["m_" + n], given["v_" + n])
    return (loss, grad_x, *[grad_w[n] for n in TWIN_WEIGHTS], *[delta_w[n] for n in TWIN_WEIGHTS],
            *[new_m[n] for n in TWIN_WEIGHTS], *[new_v[n] for n in TWIN_WEIGHTS])
```

```python
import functools

import numpy as np
import jax
import jax.numpy as jnp
from jax import lax
from jax.experimental import pallas as pl
from jax.experimental.pallas import tpu as pltpu

F32 = jnp.float32
CDT = jnp.bfloat16
MESH = pl.DeviceIdType.MESH

D = 1024
HD = 64
NH = 8
GW = NH * HD
NQKV = 6 * GW
IN_COLS = NQKV + NH
IN_PAD = NQKV + 128
NMH = 4
MHD = D // NMH
DFF = 2816
EPS = 1e-6
ATT_SCALE = HD ** -0.5
MEM_SCALE = MHD ** -0.5
NEG = -1e30

LR, B1, B2, AEPS, WD, STEP = 0.001, 0.9, 0.999, 1e-08, 0.01, 10
BC1 = 1.0 - B1 ** STEP
BC2 = 1.0 - B2 ** STEP

T_ATT = 512
W_SB = 256
VMEM_LIMIT = 52 * 2 ** 20

N_CHIP = 4
BIG = (("w_in", (D, IN_COLS // N_CHIP), 1), ("w_out", (D // N_CHIP, D), 0), ("w_mq", (D // N_CHIP, D), 0),
       ("w_mkv", (D, 2 * D // N_CHIP), 1), ("w_mo", (D // N_CHIP, D), 0), ("w_up", (D, 2 * DFF // N_CHIP), 1),
       ("conv_w", (3, 2 * DFF // N_CHIP), 1), ("w_down", (DFF // N_CHIP, D), 0))
BIG_SIZES = tuple(int(np.prod(s)) for _, s, _ in BIG)
P_BIG = sum(BIG_SIZES)
ROWS_F = 33 * 1024
assert ROWS_F * 128 >= P_BIG
HALF_F = ROWS_F // 2
ADAM_ROWS = 1536
GATHER_SIZES = tuple(2 * n if name == "conv_w" else n for (name, _, _), n in zip(BIG, BIG_SIZES))
ROWS_G = -(-sum(GATHER_SIZES) // 4096) * 32
HALF_G = ROWS_G // 2
SMALL = (("attn_norm_g", 1024), ("b_forget", 8), ("fox_out_g", 512), ("sb_out_g", 512), ("xattn_norm_g", 1024),
         ("mem_norm_g", 1024), ("ffn_norm_g", 1024), ("conv_b", 2 * DFF), ("final_norm_g", 1024))
P_SMALL = sum(n for _, n in SMALL) + 1
ROWS_S = -(-P_SMALL // 1024) * 8


def _params(sem=None, vmem=VMEM_LIMIT):
    return pltpu.CompilerParams(dimension_semantics=sem, vmem_limit_bytes=vmem)


def _tile(n, pref, mult):
    t = (min(pref, n) // mult) * mult
    while t >= mult:
        if n % t == 0:
            return t
        t -= mult
    return n


def _dot(a, b):
    return jnp.dot(a, b, preferred_element_type=F32)


def _dot_nt(a, b):
    return lax.dot_general(a, b, (((1,), (1,)), ((), ())), preferred_element_type=F32)


def _dot_tn(a, b):
    return lax.dot_general(a, b, (((0,), (0,)), ((), ())), preferred_element_type=F32)


def _split3(x):
    h1 = x.astype(CDT)
    r1 = x - h1.astype(F32)
    h2 = r1.astype(CDT)
    h3 = (r1 - h2.astype(F32)).astype(CDT)
    return h1, h2, h3


def _split2(x):
    h1 = x.astype(CDT)
    return h1, (x - h1.astype(F32)).astype(CDT)


def _rms_bwd(dh, x, g):
    r = lax.rsqrt(jnp.mean(x * x, axis=-1, keepdims=True) + EPS)
    xn = x * r
    dg = jnp.sum(dh * xn, axis=0, keepdims=True)
    dhg = dh * g
    dx = r * (dhg - xn * jnp.mean(dhg * xn, axis=-1, keepdims=True))
    return dx, dg


def _mm_nn(a, b, out_dtype, name, *, tm=1024, tn=512, residual=None, halves=False):
    M, K = a.shape
    N = b.shape[1]
    tm = _tile(M, tm, 16)
    tn = _tile(N // 2 if halves else N, tn, 128)
    nj = N // tn

    def body(*refs):
        a_ref, b_ref = refs[0], refs[1]
        o_ref = refs[-1]
        acc = _dot(a_ref[...].astype(CDT), b_ref[...].astype(CDT))
        if residual is not None:
            acc = acc + refs[2][...]
        o_ref[...] = acc.astype(o_ref.dtype)

    in_specs = [pl.BlockSpec((tm, K), lambda i, j: (i, 0)), pl.BlockSpec((K, tn), lambda i, j: (0, j))]
    ops = [a, b]
    if residual is not None:
        in_specs.append(pl.BlockSpec((tm, tn), lambda i, j: (i, j)))
        ops.append(residual)
    if halves:
        njh = nj // 2
        out_shape = jax.ShapeDtypeStruct((2, M, N // 2), out_dtype)
        out_spec = pl.BlockSpec((None, tm, tn), lambda i, j: (j // njh, i, j % njh))
    else:
        out_shape = jax.ShapeDtypeStruct((M, N), out_dtype)
        out_spec = pl.BlockSpec((tm, tn), lambda i, j: (i, j))
    return pl.pallas_call(body, name=name, grid=(M // tm, nj), in_specs=in_specs, out_specs=out_spec,
                          out_shape=out_shape, compiler_params=_params(("parallel", "parallel")))(*ops)


def _mm_tn(a, b, name, *, tka=512, tn=1024, ts=512, b_halves=False):
    S, Ka = a.shape
    N = 2 * b.shape[2] if b_halves else b.shape[1]
    tka = _tile(Ka, tka, 128)
    tn = _tile(N // 2 if b_halves else N, tn, 128)
    ts = _tile(S, ts, 16)
    nn = N // tn

    def body(a_ref, b_ref, o_ref):
        @pl.when(pl.program_id(2) == 0)
        def _():
            o_ref[...] = jnp.zeros_like(o_ref)
        o_ref[...] += _dot_tn(a_ref[...].astype(CDT), b_ref[...].astype(CDT))

    if b_halves:
        nnh = nn // 2
        b_spec = pl.BlockSpec((None, ts, tn), lambda i, j, s: (j // nnh, s, j % nnh))
    else:
        b_spec = pl.BlockSpec((ts, tn), lambda i, j, s: (s, j))
    return pl.pallas_call(
        body, name=name, grid=(Ka // tka, nn, S // ts),
        in_specs=[pl.BlockSpec((ts, tka), lambda i, j, s: (s, i)), b_spec],
        out_specs=pl.BlockSpec((tka, tn), lambda i, j, s: (i, j)),
        out_shape=jax.ShapeDtypeStruct((Ka, N), F32),
        compiler_params=_params(("parallel", "parallel", "arbitrary")))(a, b)


def _mm_nt(a, b, name, *, tm=512, tn=None, tk=None, a_halves=False, out_dtype=F32,
           epilogue=None, extra=(), extra_specs=(), out_shape=None, out_specs=None):
    if a_halves:
        M, K = a.shape[1], 2 * a.shape[2]
    else:
        M, K = a.shape
    N = b.shape[0]
    tm = _tile(M, tm, 16)
    tn = N if (epilogue is not None or tn is None) else _tile(N, tn, 128)
    tk = K if tk is None else _tile(K // 2 if a_halves else K, tk, 128)
    nk = K // tk
    n_extra = len(extra)

    def body(*refs):
        a_ref, b_ref = refs[0], refs[1]
        extra_refs = refs[2:2 + n_extra]
        out_refs = refs[2 + n_extra:-1]
        acc_ref = refs[-1]
        k = pl.program_id(2)

        @pl.when(k == 0)
        def _():
            acc_ref[...] = jnp.zeros_like(acc_ref)
        acc_ref[...] += _dot_nt(a_ref[...].astype(CDT), b_ref[...].astype(CDT))

        @pl.when(k == nk - 1)
        def _():
            if epilogue is None:
                out_refs[0][...] = acc_ref[...].astype(out_refs[0].dtype)
            else:
                epilogue(acc_ref[...], pl.program_id(0), extra_refs, out_refs)

    if a_halves:
        nkh = nk // 2
        a_spec = pl.BlockSpec((None, tm, tk), lambda i, j, k: (k // nkh, i, k % nkh))
    else:
        a_spec = pl.BlockSpec((tm, tk), lambda i, j, k: (i, k))
    if epilogue is None:
        out_shape = jax.ShapeDtypeStruct((M, N), out_dtype)
        out_specs = pl.BlockSpec((tm, tn), lambda i, j, k: (i, j))
        sem = ("parallel", "parallel", "arbitrary")
    else:
        sem = ("arbitrary", "arbitrary", "arbitrary")
    return pl.pallas_call(
        body, name=name, grid=(M // tm, N // tn, nk),
        in_specs=[a_spec, pl.BlockSpec((tn, tk), lambda i, j, k: (j, k)), *extra_specs],
        out_specs=out_specs, out_shape=out_shape,
        scratch_shapes=[pltpu.VMEM((tm, tn), F32)],
        compiler_params=_params(sem))(a, b, *extra)


def _mm_nt_rmsbwd(a, b, x, g, dres, name, *, tm=512, tk=None, a_halves=False):
    M = x.shape[0]
    tm = _tile(M, tm, 16)

    def epilogue(acc, i, extra_refs, out_refs):
        x_ref, g_ref, r_ref = extra_refs
        dx_ref, dg_ref = out_refs
        dx, dg = _rms_bwd(acc, x_ref[...], g_ref[...])
        dx_ref[...] = r_ref[...] + dx

        @pl.when(i == 0)
        def _():
            dg_ref[...] = jnp.zeros_like(dg_ref)
        dg_ref[...] += dg

    row = pl.BlockSpec((tm, D), lambda i, j, k: (i, 0))
    vec = pl.BlockSpec((1, D), lambda i, j, k: (0, 0))
    return _mm_nt(a, b, name, tm=tm, tk=tk, a_halves=a_halves, epilogue=epilogue,
                  extra=(x, g, dres), extra_specs=(row, vec, row),
                  out_shape=(jax.ShapeDtypeStruct((M, D), F32), jax.ShapeDtypeStruct((1, D), F32)),
                  out_specs=(row, vec))


def _rms_cast(x, g, name, *, tm=512):
    M, W = x.shape
    tm = _tile(M, tm, 16)

    def body(x_ref, g_ref, o_ref):
        xf = x_ref[...]
        r = lax.rsqrt(jnp.mean(xf * xf, axis=-1, keepdims=True) + EPS)
        o_ref[...] = (xf * r * g_ref[...]).astype(o_ref.dtype)

    return pl.pallas_call(body, name=name, grid=(M // tm,),
                          in_specs=[pl.BlockSpec((tm, W), lambda i: (i, 0)), pl.BlockSpec((1, W), lambda i: (0, 0))],
                          out_specs=pl.BlockSpec((tm, W), lambda i: (i, 0)),
                          out_shape=jax.ShapeDtypeStruct((M, W), CDT),
                          compiler_params=_params(("parallel",)))(x, g)


def _tri(n, lower):
    r = lax.broadcasted_iota(jnp.int32, (n, n), 0)
    c = lax.broadcasted_iota(jnp.int32, (n, n), 1)
    return (c <= r if lower else c >= r).astype(CDT)


def _gate_fwd(fl, b, name, *, tm=512):
    S = fl.shape[0]
    tm = _tile(S, tm, 16)

    def body(f_ref, b_ref, c_ref, carry):
        @pl.when(pl.program_id(0) == 0)
        def _():
            carry[...] = jnp.zeros_like(carry)
        z = f_ref[...] + b_ref[...]
        lf = jnp.minimum(z, 0.0) - jnp.log(1.0 + jnp.exp(-jnp.abs(z)))
        tri = _tri(tm, True)
        cum = sum(_dot(tri, p) for p in _split3(lf)) + carry[...]
        c_ref[...] = cum
        carry[...] = cum[tm - 1:tm, :]

    return pl.pallas_call(body, name=name, grid=(S // tm,),
                          in_specs=[pl.BlockSpec((tm, 128), lambda i: (i, 0)), pl.BlockSpec((1, 128), lambda i: (0, 0))],
                          out_specs=pl.BlockSpec((tm, 128), lambda i: (i, 0)),
                          out_shape=jax.ShapeDtypeStruct((S, 128), F32),
                          scratch_shapes=[pltpu.VMEM((1, 128), F32)],
                          compiler_params=_params(("arbitrary",)))(fl, b)


def _gate_bwd(dc, fl, b, name, *, tm=512):
    S = fl.shape[0]
    tm = _tile(S, tm, 16)
    nb = S // tm

    def body(dc_ref, f_ref, b_ref, df_ref, db_ref, carry):
        @pl.when(pl.program_id(0) == 0)
        def _():
            carry[...] = jnp.zeros_like(carry)
            db_ref[...] = jnp.zeros_like(db_ref)
        tri = _tri(tm, False)
        suf = sum(_dot(tri, p) for p in _split3(dc_ref[...])) + carry[...]
        carry[...] = suf[0:1, :]
        df = suf * jax.nn.sigmoid(-(f_ref[...] + b_ref[...]))
        df_ref[...] = df
        db_ref[...] += jnp.sum(df, axis=0, keepdims=True)

    rev = pl.BlockSpec((tm, 128), lambda i: (nb - 1 - i, 0))
    vec = pl.BlockSpec((1, 128), lambda i: (0, 0))
    return pl.pallas_call(body, name=name, grid=(nb,), in_specs=[rev, rev, vec], out_specs=(rev, vec),
                          out_shape=(jax.ShapeDtypeStruct((S, 128), F32), jax.ShapeDtypeStruct((1, 128), F32)),
                          scratch_shapes=[pltpu.VMEM((1, 128), F32)],
                          compiler_params=_params(("arbitrary",)))(dc, fl, b)


def _out_proj(fo, so, gf, gs, w_out, x0, name, *, tm=512):
    S = fo.shape[0]
    tm = _tile(S, tm, 16)

    def body(fo_ref, so_ref, gf_ref, gs_ref, w_ref, x_ref, x1_ref, mx_ref):
        for ref, g_ref, lo in ((fo_ref, gf_ref, 0), (so_ref, gs_ref, GW)):
            o = ref[...]
            r = lax.rsqrt(jnp.mean(o * o, axis=-1, keepdims=True) + EPS)
            mx_ref[:, lo:lo + GW] = (o * r * g_ref[...]).astype(CDT)
        x1_ref[...] = x_ref[...] + _dot(mx_ref[...], w_ref[...])

    half = pl.BlockSpec((tm, GW), lambda i: (i, 0))
    gvec = pl.BlockSpec((1, GW), lambda i: (0, 0))
    row = pl.BlockSpec((tm, D), lambda i: (i, 0))
    return pl.pallas_call(body, name=name, grid=(S // tm,),
                          in_specs=[half, half, gvec, gvec, pl.BlockSpec((D, D), lambda i: (0, 0)), row],
                          out_specs=(row, row),
                          out_shape=(jax.ShapeDtypeStruct((S, D), F32), jax.ShapeDtypeStruct((S, D), CDT)),
                          compiler_params=_params(("parallel",)))(fo, so, gf, gs, w_out, x0)


def _out_proj_bwd(dx1, w_out, fo, so, gf, gs, name, *, tm=512):
    S = fo.shape[0]
    tm = _tile(S, tm, 16)

    def epilogue(acc, i, extra_refs, out_refs):
        fo_ref, so_ref, gf_ref, gs_ref = extra_refs
        dfo_ref, dso_ref, dgf_ref, dgs_ref = out_refs

        @pl.when(i == 0)
        def _():
            dgf_ref[...] = jnp.zeros_like(dgf_ref)
            dgs_ref[...] = jnp.zeros_like(dgs_ref)
        for lo, o_ref, g_ref, do_ref, dg_ref in ((0, fo_ref, gf_ref, dfo_ref, dgf_ref), (GW, so_ref, gs_ref, dso_ref, dgs_ref)):
            dx, dg = _rms_bwd(acc[:, lo:lo + GW], o_ref[...], g_ref[...])
            do_ref[...] = dx.astype(do_ref.dtype)
            dg_ref[...] += dg

    half = pl.BlockSpec((tm, GW), lambda i, j, k: (i, 0))
    gvec = pl.BlockSpec((1, GW), lambda i, j, k: (0, 0))
    return _mm_nt(dx1, w_out, name, tm=tm, epilogue=epilogue, extra=(fo, so, gf, gs),
                  extra_specs=(half, half, gvec, gvec),
                  out_shape=(jax.ShapeDtypeStruct((S, GW), CDT), jax.ShapeDtypeStruct((S, GW), CDT),
                             jax.ShapeDtypeStruct((1, GW), F32), jax.ShapeDtypeStruct((1, GW), F32)),
                  out_specs=(half, half, gvec, gvec))


def _loss_bwd(x3, tgt, g, name, *, tm=512):
    S = x3.shape[0]
    tm = _tile(S, tm, 16)

    def body(x_ref, t_ref, g_ref, dx_ref, loss_ref, dg_ref):
        @pl.when(pl.program_id(0) == 0)
        def _():
            loss_ref[...] = jnp.zeros_like(loss_ref)
            dg_ref[...] = jnp.zeros_like(dg_ref)
        x = x_ref[...]
        gv = g_ref[...]
        r = lax.rsqrt(jnp.mean(x * x, axis=-1, keepdims=True) + EPS)
        xn = x * r
        err = xn * gv - t_ref[...]
        loss_ref[...] += jnp.full(loss_ref.shape, 0.5 * jnp.sum(jnp.mean(err * err, axis=-1, keepdims=True)), F32)
        dy = err * (1.0 / D)
        dg_ref[...] += jnp.sum(dy * xn, axis=0, keepdims=True)
        dyg = dy * gv
        dx_ref[...] = r * (dyg - xn * jnp.mean(dyg * xn, axis=-1, keepdims=True))

    row = pl.BlockSpec((tm, D), lambda i: (i, 0))
    vec = pl.BlockSpec((1, D), lambda i: (0, 0))
    dx3, loss, dg = pl.pallas_call(
        body, name=name, grid=(S // tm,), in_specs=[row, row, vec],
        out_specs=(row, pl.BlockSpec((1, 128), lambda i: (0, 0)), vec),
        out_shape=(jax.ShapeDtypeStruct((S, D), F32), jax.ShapeDtypeStruct((1, 128), F32), jax.ShapeDtypeStruct((1, D), F32)),
        compiler_params=_params(("arbitrary",)))(x3, tgt, g)
    return loss, dx3, dg


def _pairs(nq):
    qi = [i for i in range(nq) for _ in range(i + 1)]
    kj = [kb for i in range(nq) for kb in range(i, -1, -1)]
    return jnp.asarray(np.asarray(qi, np.int32)), jnp.asarray(np.asarray(kj, np.int32))


def _att_specs(T):
    qblk = pl.BlockSpec((1, T, HD), lambda h, n, qi, kj: (h, qi[n], 0))
    kblk = pl.BlockSpec((1, T, HD), lambda h, n, qi, kj: (h, kj[n], 0))
    qcol = pl.BlockSpec((1, T, 1), lambda h, n, qi, kj: (h, qi[n], 0))
    krow = pl.BlockSpec((1, 1, T), lambda h, n, qi, kj: (h, 0, kj[n]))
    return qblk, kblk, qcol, krow


def _causal(T, Tk, i, kb, off, strict):
    row = lax.broadcasted_iota(jnp.int32, (T, Tk), 0) + i * T
    col = lax.broadcasted_iota(jnp.int32, (T, Tk), 1) + (kb * T + off)
    return col < row if strict else col <= row


def _fox_fwd(q, k, v, cq, ck, name):
    S = q.shape[1]
    T = min(T_ATT, S)
    qi, kj = _pairs(S // T)
    qblk, kblk, qcol, krow = _att_specs(T)

    def body(qi_ref, kj_ref, q_ref, k_ref, v_ref, cq_ref, ck_ref, o_ref, lse_ref, m_s, l_s, acc_s):
        n = pl.program_id(1)
        i, kb = qi_ref[n], kj_ref[n]

        @pl.when(kb == i)
        def _():
            m_s[...] = jnp.full_like(m_s, NEG)
            l_s[...] = jnp.zeros_like(l_s)
            acc_s[...] = jnp.zeros_like(acc_s)
        s = _dot_nt(q_ref[0] * ATT_SCALE, k_ref[0]) + cq_ref[0] - ck_ref[0]
        s = jnp.where(_causal(T, T, i, kb, 0, False), s, NEG)
        m_new = jnp.maximum(m_s[...], jnp.max(s, axis=-1, keepdims=True))
        alpha = jnp.exp(m_s[...] - m_new)
        p = jnp.exp(s - m_new)
        l_s[...] = alpha * l_s[...] + jnp.sum(p, axis=-1, keepdims=True)
        acc_s[...] = alpha * acc_s[...] + _dot(p.astype(CDT), v_ref[0])
        m_s[...] = m_new

        @pl.when(kb == 0)
        def _():
            o_ref[0] = acc_s[...] / l_s[...]
            lse_ref[0] = m_s[...] + jnp.log(l_s[...])

    grid_spec = pltpu.PrefetchScalarGridSpec(
        num_scalar_prefetch=2, grid=(NH, int(qi.shape[0])),
        in_specs=[qblk, kblk, kblk, qcol, krow],
        out_specs=(pl.BlockSpec((1, T, HD), lambda h, n, qi, kj: (h, qi[n], 0)), qcol),
        scratch_shapes=[pltpu.VMEM((T, 1), F32), pltpu.VMEM((T, 1), F32), pltpu.VMEM((T, HD), F32)])
    return pl.pallas_call(body, name=name, grid_spec=grid_spec,
                          out_shape=(jax.ShapeDtypeStruct((NH, S, HD), F32), jax.ShapeDtypeStruct((NH, S, 1), F32)),
                          compiler_params=_params(("parallel", "arbitrary")))(qi, kj, q, k, v, cq, ck)


def _fox_bwd(q, k, v, cq, ck, o, do, lse, name):
    S = q.shape[1]
    T = min(T_ATT, S)
    qi, kj = _pairs(S // T)
    qblk, kblk, qcol, krow = _att_specs(T)

    def body(qi_ref, kj_ref, q_ref, k_ref, v_ref, cq_ref, ck_ref, o_ref, do_ref, lse_ref,
             dq_ref, dk_ref, dv_ref, dck_ref, dcq_ref, dq_s, dl_s, dcq_s):
        n = pl.program_id(1)
        i, kb = qi_ref[n], kj_ref[n]

        @pl.when(n == 0)
        def _():
            dk_ref[...] = jnp.zeros_like(dk_ref)
            dv_ref[...] = jnp.zeros_like(dv_ref)
            dck_ref[...] = jnp.zeros_like(dck_ref)

        @pl.when(kb == i)
        def _():
            dq_s[...] = jnp.zeros_like(dq_s)
            dcq_s[...] = jnp.zeros_like(dcq_s)
            dl_s[...] = jnp.sum(do_ref[0].astype(F32) * o_ref[0], axis=-1, keepdims=True)
        qs = q_ref[0] * ATT_SCALE
        do = do_ref[0]
        s = _dot_nt(qs, k_ref[0]) + cq_ref[0] - ck_ref[0]
        p = jnp.where(_causal(T, T, i, kb, 0, False), jnp.exp(s - lse_ref[0]), 0.0)
        ds = p * (_dot_nt(do, v_ref[0]) - dl_s[...])
        dsb = ds.astype(CDT)
        dq_s[...] += _dot(dsb, k_ref[0])
        rows = pl.ds(pl.multiple_of(kb * T, T), T)
        dk_ref[0, rows, :] += _dot_tn(dsb, qs)
        dv_ref[0, rows, :] += _dot_tn(p.astype(CDT), do)
        dck_ref[0, :, rows] += -jnp.sum(ds, axis=0, keepdims=True)
        dcq_s[...] += jnp.sum(ds, axis=-1, keepdims=True)

        @pl.when(kb == 0)
        def _():
            dq_ref[0] = (dq_s[...] * ATT_SCALE).astype(dq_ref.dtype)
            dcq_ref[0] = dcq_s[...]

    whole = pl.BlockSpec((1, S, HD), lambda h, n, qi, kj: (h, 0, 0))
    grid_spec = pltpu.PrefetchScalarGridSpec(
        num_scalar_prefetch=2, grid=(NH, int(qi.shape[0])),
        in_specs=[qblk, kblk, kblk, qcol, krow, qblk, qblk, qcol],
        out_specs=(qblk, whole, whole, pl.BlockSpec((1, 1, S), lambda h, n, qi, kj: (h, 0, 0)), qcol),
        scratch_shapes=[pltpu.VMEM((T, HD), F32), pltpu.VMEM((T, 1), F32), pltpu.VMEM((T, 1), F32)])
    return pl.pallas_call(body, name=name, grid_spec=grid_spec,
                          out_shape=(jax.ShapeDtypeStruct((NH, S, HD), CDT), jax.ShapeDtypeStruct((NH, S, HD), F32),
                                     jax.ShapeDtypeStruct((NH, S, HD), F32), jax.ShapeDtypeStruct((NH, 1, S), F32),
                                     jax.ShapeDtypeStruct((NH, S, 1), F32)),
                          compiler_params=_params(("parallel", "arbitrary")))(qi, kj, q, k, v, cq, ck, o, do, lse)


def _sb_scores(qs, ksub, mask, run):
    W = ksub.shape[0]
    z = _dot_nt(qs, ksub)
    sp = jnp.maximum(z, 0.0) + jnp.log(1.0 + jnp.exp(-jnp.abs(z)))
    lk = jnp.where(mask, -sp, 0.0)
    r = lax.broadcasted_iota(jnp.int32, (W, W), 0)
    c = lax.broadcasted_iota(jnp.int32, (W, W), 1)
    later = (r > c).astype(CDT)
    hi, lo = _split2(lk)
    excl = _dot(hi, later) + _dot(lo, later)
    return jnp.exp(z - sp), lk, excl + run, excl[:, 0:1] + lk[:, 0:1]


def _sb_fwd(q, k, v, name):
    S = q.shape[1]
    T = min(T_ATT, S)
    W = min(W_SB, T)
    qi, kj = _pairs(S // T)
    qblk, kblk, _, _ = _att_specs(T)

    def body(qi_ref, kj_ref, q_ref, k_ref, v_ref, o_ref, run_s, acc_s):
        n = pl.program_id(1)
        i, kb = qi_ref[n], kj_ref[n]

        @pl.when(kb == i)
        def _():
            run_s[...] = jnp.zeros_like(run_s)
            acc_s[...] = jnp.zeros_like(acc_s)
        qs = q_ref[0] * ATT_SCALE
        for sub in range(T // W - 1, -1, -1):
            cols = slice(sub * W, (sub + 1) * W)
            mask = _causal(T, W, i, kb, sub * W, True)
            sig, lk, after, tot = _sb_scores(qs, k_ref[0, cols, :], mask, run_s[...])
            a = jnp.where(mask, sig * jnp.exp(after), 0.0)
            hi, lo = _split2(a)
            acc_s[...] += _dot(hi, v_ref[0, cols, :]) + _dot(lo, v_ref[0, cols, :])
            run_s[...] += tot

        @pl.when(kb == 0)
        def _():
            o_ref[0] = acc_s[...]

    grid_spec = pltpu.PrefetchScalarGridSpec(
        num_scalar_prefetch=2, grid=(NH, int(qi.shape[0])), in_specs=[qblk, kblk, kblk], out_specs=qblk,
        scratch_shapes=[pltpu.VMEM((T, 1), F32), pltpu.VMEM((T, HD), F32)])
    return pl.pallas_call(body, name=name, grid_spec=grid_spec, out_shape=jax.ShapeDtypeStruct((NH, S, HD), F32),
                          compiler_params=_params(("parallel", "arbitrary")))(qi, kj, q, k, v)


def _sb_bwd(q, k, v, o, do, name):
    S = q.shape[1]
    T = min(T_ATT, S)
    W = min(W_SB, T)
    qi, kj = _pairs(S // T)
    qblk, kblk, _, _ = _att_specs(T)

    def body(qi_ref, kj_ref, q_ref, k_ref, v_ref, o_ref, do_ref, dq_ref, dk_ref, dv_ref, run_s, suf_s, dq_s, dl_s):
        n = pl.program_id(1)
        i, kb = qi_ref[n], kj_ref[n]

        @pl.when(n == 0)
        def _():
            dk_ref[...] = jnp.zeros_like(dk_ref)
            dv_ref[...] = jnp.zeros_like(dv_ref)

        @pl.when(kb == i)
        def _():
            run_s[...] = jnp.zeros_like(run_s)
            suf_s[...] = jnp.zeros_like(suf_s)
            dq_s[...] = jnp.zeros_like(dq_s)
            dl_s[...] = jnp.sum(do_ref[0].astype(F32) * o_ref[0], axis=-1, keepdims=True)
        qs = q_ref[0] * ATT_SCALE
        do = do_ref[0]
        r = lax.broadcasted_iota(jnp.int32, (W, W), 0)
        c = lax.broadcasted_iota(jnp.int32, (W, W), 1)
        from_here = (r >= c).astype(CDT)
        for sub in range(T // W - 1, -1, -1):
            cols = slice(sub * W, (sub + 1) * W)
            mask = _causal(T, W, i, kb, sub * W, True)
            ksub = k_ref[0, cols, :]
            sig, lk, after, tot = _sb_scores(qs, ksub, mask, run_s[...])
            a = jnp.where(mask, sig * jnp.exp(after), 0.0)
            dl = _dot_nt(do, v_ref[0, cols, :]) * a
            hi, lo = _split2(dl)
            suf = _dot(hi, from_here) + _dot(lo, from_here) + suf_s[...]
            dz = jnp.where(mask, dl * (1.0 - sig) - (dl_s[...] - suf) * sig, 0.0)
            dzb = dz.astype(CDT)
            dq_s[...] += _dot(dzb, ksub)
            rows = pl.ds(pl.multiple_of(kb * T + sub * W, W), W)
            dk_ref[0, rows, :] += _dot_tn(dzb, qs)
            dv_ref[0, rows, :] += _dot_tn(a.astype(CDT), do)
            run_s[...] += tot
            suf_s[...] = suf[:, 0:1]

        @pl.when(kb == 0)
        def _():
            dq_ref[0] = (dq_s[...] * ATT_SCALE).astype(dq_ref.dtype)

    whole = pl.BlockSpec((1, S, HD), lambda h, n, qi, kj: (h, 0, 0))
    grid_spec = pltpu.PrefetchScalarGridSpec(
        num_scalar_prefetch=2, grid=(NH, int(qi.shape[0])), in_specs=[qblk, kblk, kblk, qblk, qblk],
        out_specs=(qblk, whole, whole),
        scratch_shapes=[pltpu.VMEM((T, 1), F32), pltpu.VMEM((T, 1), F32), pltpu.VMEM((T, HD), F32), pltpu.VMEM((T, 1), F32)])
    return pl.pallas_call(body, name=name, grid_spec=grid_spec,
                          out_shape=(jax.ShapeDtypeStruct((NH, S, HD), CDT), jax.ShapeDtypeStruct((NH, S, HD), F32),
                                     jax.ShapeDtypeStruct((NH, S, HD), F32)),
                          compiler_params=_params(("parallel", "arbitrary")))(qi, kj, q, k, v, o, do)


def _mem_probs(q_ref, kv_ref, h):
    cols = slice(h * MHD, (h + 1) * MHD)
    s = _dot_nt(q_ref[:, cols], kv_ref[:, cols]) * MEM_SCALE
    e = jnp.exp(s - jnp.max(s, axis=-1, keepdims=True))
    return e / jnp.sum(e, axis=-1, keepdims=True)


def _xattn_fwd(q, kv, w_mo, x1, name, *, tm=512):
    S = q.shape[0]
    tm = _tile(S, tm, 16)
    nm = kv.shape[0]

    def body(q_ref, kv_ref, w_ref, x_ref, x2_ref, o_ref):
        for h in range(NMH):
            p = _mem_probs(q_ref, kv_ref, h)
            o_ref[:, h * MHD:(h + 1) * MHD] = _dot(p.astype(CDT), kv_ref[:, D + h * MHD:D + (h + 1) * MHD]).astype(CDT)
        x2_ref[...] = x_ref[...] + _dot(o_ref[...], w_ref[...])

    row = pl.BlockSpec((tm, D), lambda i: (i, 0))
    return pl.pallas_call(body, name=name, grid=(S // tm,),
                          in_specs=[row, pl.BlockSpec((nm, 2 * D), lambda i: (0, 0)), pl.BlockSpec((D, D), lambda i: (0, 0)), row],
                          out_specs=(row, row),
                          out_shape=(jax.ShapeDtypeStruct((S, D), F32), jax.ShapeDtypeStruct((S, D), CDT)),
                          compiler_params=_params(("parallel",)))(q, kv, w_mo, x1)


def _xattn_bwd(q, kv, do, name, *, tm=512):
    S = q.shape[0]
    tm = _tile(S, tm, 16)
    nm = kv.shape[0]

    def body(q_ref, kv_ref, do_ref, dq_ref, dkv_ref):
        @pl.when(pl.program_id(0) == 0)
        def _():
            dkv_ref[...] = jnp.zeros_like(dkv_ref)
        for h in range(NMH):
            cols = slice(h * MHD, (h + 1) * MHD)
            vcols = slice(D + h * MHD, D + (h + 1) * MHD)
            p = _mem_probs(q_ref, kv_ref, h)
            doh = do_ref[:, cols]
            dp = _dot_nt(doh, kv_ref[:, vcols])
            ds = (p * (dp - jnp.sum(p * dp, axis=-1, keepdims=True)) * MEM_SCALE).astype(CDT)
            dq_ref[:, cols] = _dot(ds, kv_ref[:, cols]).astype(CDT)
            dkv_ref[:, cols] += _dot_tn(ds, q_ref[:, cols])
            dkv_ref[:, vcols] += _dot_tn(p.astype(CDT), doh)

    row = pl.BlockSpec((tm, D), lambda i: (i, 0))
    kvs = pl.BlockSpec((nm, 2 * D), lambda i: (0, 0))
    return pl.pallas_call(body, name=name, grid=(S // tm,), in_specs=[row, kvs, row], out_specs=(row, kvs),
                          out_shape=(jax.ShapeDtypeStruct((S, D), CDT), jax.ShapeDtypeStruct((nm, 2 * D), F32)),
                          compiler_params=_params(("arbitrary",)))(q, kv, do)


HALO = 16


def _shift_down(u, prev, s):
    rolled = pltpu.roll(u, s, 0)
    r = lax.broadcasted_iota(jnp.int32, u.shape, 0)
    for t in range(s):
        rolled = jnp.where(r == t, prev[HALO - s + t:HALO - s + t + 1, :], rolled)
    return rolled


def _shift_up(u, nxt, s):
    n = u.shape[0]
    rolled = pltpu.roll(u, n - s, 0)
    r = lax.broadcasted_iota(jnp.int32, u.shape, 0)
    for t in range(s):
        rolled = jnp.where(r == n - s + t, nxt[t:t + 1, :], rolled)
    return rolled


def _conv_taps(u_ref, h_ref, first):
    u = u_ref[...].astype(F32)
    prev = jnp.where(first, 0.0, h_ref[...].astype(F32))
    out = []
    for half in range(2):
        out.append((u[half], _shift_down(u[half], prev[half], 1), _shift_down(u[half], prev[half], 2)))
    return out


def _conv_specs(tm, tn, nsb):
    blk = pl.BlockSpec((2, tm, tn), lambda j, i: (0, i, j))
    prev = pl.BlockSpec((2, HALO, tn), lambda j, i: (0, jnp.maximum(i * (tm // HALO) - 1, 0), j))
    nxt = pl.BlockSpec((2, HALO, tn), lambda j, i: (0, jnp.minimum((i + 1) * (tm // HALO), nsb - 1), j))
    w = pl.BlockSpec((2, 3, tn), lambda j, i: (0, 0, j))
    b = pl.BlockSpec((2, 1, tn), lambda j, i: (0, 0, j))
    return blk, prev, nxt, w, b


def _conv_apply(taps, w_ref, b_ref):
    ys = []
    for half in range(2):
        u, u1, u2 = taps[half]
        w = w_ref[half]
        ys.append(b_ref[half] + u2 * w[0:1, :] + u1 * w[1:2, :] + u * w[2:3, :])
    return ys


def _conv_act(u0, cw, cb, name, *, tm=512, tn=256):
    _, S, F = u0.shape
    tm = _tile(S, tm, HALO)
    tn = _tile(F, tn, 128)
    blk, prev, _, w, b = _conv_specs(tm, tn, S // HALO)

    def body(u_ref, h_ref, w_ref, b_ref, a_ref):
        yg, yv = _conv_apply(_conv_taps(u_ref, h_ref, pl.program_id(1) == 0), w_ref, b_ref)
        a_ref[...] = (yg * jax.nn.sigmoid(yg) * yv).astype(a_ref.dtype)

    return pl.pallas_call(body, name=name, grid=(F // tn, S // tm), in_specs=[blk, prev, w, b],
                          out_specs=pl.BlockSpec((tm, tn), lambda j, i: (i, j)),
                          out_shape=jax.ShapeDtypeStruct((S, F), CDT),
                          compiler_params=_params(("parallel", "parallel")))(u0, u0, cw, cb)


def _conv_act_bwd(u0, da, cw, cb, name, *, tm=512, tn=256):
    _, S, F = u0.shape
    tm = _tile(S, tm, HALO)
    tn = _tile(F, tn, 128)
    blk, prev, _, w, b = _conv_specs(tm, tn, S // HALO)

    def body(u_ref, h_ref, da_ref, w_ref, b_ref, du_ref, dwb_ref):
        @pl.when(pl.program_id(1) == 0)
        def _():
            dwb_ref[...] = jnp.zeros_like(dwb_ref)
        taps = _conv_taps(u_ref, h_ref, pl.program_id(1) == 0)
        yg, yv = _conv_apply(taps, w_ref, b_ref)
        sg = jax.nn.sigmoid(yg)
        da = da_ref[...].astype(F32)
        dus = (da * yv * sg * (1.0 + yg * (1.0 - sg)), da * yg * sg)
        for half in range(2):
            du = dus[half]
            du_ref[half] = du.astype(du_ref.dtype)
            u, u1, u2 = taps[half]
            for row, term in enumerate((du * u2, du * u1, du * u, du)):
                dwb_ref[half, row:row + 1, :] += jnp.sum(term, axis=0, keepdims=True)

    return pl.pallas_call(body, name=name, grid=(F // tn, S // tm),
                          in_specs=[blk, prev, pl.BlockSpec((tm, tn), lambda j, i: (i, j)), w, b],
                          out_specs=(blk, pl.BlockSpec((2, 4, tn), lambda j, i: (0, 0, j))),
                          out_shape=(jax.ShapeDtypeStruct((2, S, F), CDT), jax.ShapeDtypeStruct((2, 4, F), F32)),
                          compiler_params=_params(("parallel", "arbitrary")))(u0, u0, da, cw, cb)


def _conv_bwd_input(du, cw, name, *, tm=512, tn=256):
    _, S, F = du.shape
    tm = _tile(S, tm, HALO)
    tn = _tile(F, tn, 128)
    blk, _, nxt, w, _ = _conv_specs(tm, tn, S // HALO)
    ni = S // tm

    def body(d_ref, h_ref, w_ref, o_ref):
        d = d_ref[...].astype(F32)
        nx = jnp.where(pl.program_id(1) == ni - 1, 0.0, h_ref[...].astype(F32))
        for half in range(2):
            wv = w_ref[half]
            y = d[half] * wv[2:3, :] + _shift_up(d[half], nx[half], 1) * wv[1:2, :] + _shift_up(d[half], nx[half], 2) * wv[0:1, :]
            o_ref[half] = y.astype(o_ref.dtype)

    return pl.pallas_call(body, name=name, grid=(F // tn, ni), in_specs=[blk, nxt, w], out_specs=blk,
                          out_shape=jax.ShapeDtypeStruct((2, S, F), CDT),
                          compiler_params=_params(("parallel", "parallel")))(du, du, cw)


ANY = pl.BlockSpec(memory_space=pl.ANY)


def _place():
    return lax.axis_index("x"), lax.axis_index("y"), lax.axis_index("c")


def _other_chips(x, y):
    return ((1 - x, y), (x, 1 - y), (1 - x, 1 - y))


def _gather_weights(wsh):
    rows, half = wsh.shape[0], wsh.shape[0] // 2

    def body(w_ref, out_ref, send_sems, recv_sems, local_sem):
        x, y, c = _place()
        chips = _other_chips(x, y)

        def part(chip, pc):
            return out_ref.at[2 * chip[0] + chip[1], pl.ds(pl.multiple_of(pc * half, 16), half), :]

        def copy(k, chip, pc, to, src=None):
            return pltpu.make_async_remote_copy(src_ref=part(chip, pc) if src is None else src, dst_ref=part(chip, pc),
                                                send_sem=send_sems.at[k], recv_sem=recv_sems.at[k],
                                                device_id=to, device_id_type=MESH)

        mine = pltpu.make_async_copy(w_ref, out_ref.at[2 * x + y], local_sem)
        mine.start()
        my_half = w_ref.at[pl.ds(pl.multiple_of(c * half, 16), half), :]
        first = [copy(j, (x, y), c, (*chip, c), src=my_half) for j, chip in enumerate(chips)]
        for cp in first:
            cp.start()
        passed = [copy(3 + j, chip, c, (x, y, 1 - c)) for j, chip in enumerate(chips)]
        for j, chip in enumerate(chips):
            copy(j, chip, c, (x, y, c)).wait_recv()
            passed[j].start()
        for j, chip in enumerate(chips):
            copy(3 + j, chip, 1 - c, (x, y, c)).wait_recv()
        for cp in first + passed:
            cp.wait_send()
        mine.wait()

    return pl.pallas_call(body, name="gather_weights", in_specs=[ANY], out_specs=ANY,
                          out_shape=jax.ShapeDtypeStruct((N_CHIP, rows, 128), wsh.dtype),
                          scratch_shapes=[pltpu.SemaphoreType.DMA((6,)), pltpu.SemaphoreType.DMA((6,)), pltpu.SemaphoreType.DMA])(wsh)


def _gather_small(v):
    m = v.shape[0]

    def body(v_ref, out_ref, send_sems, recv_sems, local_sem):
        x, y, c = _place()
        me, sibling = (x, y, c), (x, y, 1 - c)
        chips = _other_chips(x, y)

        def rows(px, py, pc):
            return out_ref.at[pl.ds((4 * px + 2 * py + pc) * m, m), :]

        def copy(k, block, to, src=None):
            return pltpu.make_async_remote_copy(src_ref=rows(*block) if src is None else src, dst_ref=rows(*block),
                                                send_sem=send_sems.at[k], recv_sem=recv_sems.at[k],
                                                device_id=to, device_id_type=MESH)

        mine = pltpu.make_async_copy(v_ref, rows(*me), local_sem)
        mine.start()
        first = [copy(0, me, sibling, src=v_ref)]
        first += [copy(1 + j, me, (*chip, c), src=v_ref) for j, chip in enumerate(chips)]
        for cp in first:
            cp.start()
        passed = [copy(4 + j, (*chip, c), sibling) for j, chip in enumerate(chips)]
        for j, chip in enumerate(chips):
            copy(1 + j, (*chip, c), me).wait_recv()
            passed[j].start()
        copy(0, sibling, me).wait_recv()
        for j, chip in enumerate(chips):
            copy(4 + j, (*chip, 1 - c), me).wait_recv()
        for cp in first + passed:
            cp.wait_send()
        mine.wait()

    vm = pl.BlockSpec(memory_space=pltpu.VMEM)
    return pl.pallas_call(body, name="gather_small", in_specs=[vm], out_specs=vm,
                          out_shape=jax.ShapeDtypeStruct((8 * m, 128), v.dtype),
                          scratch_shapes=[pltpu.SemaphoreType.DMA((7,)), pltpu.SemaphoreType.DMA((7,)), pltpu.SemaphoreType.DMA])(v)


def _swap_halves(g):
    n, rows, _ = g.shape
    half = rows // 2

    def body(g_ref, out_ref, send_sem, recv_sem):
        x, y, c = _place()
        src = g_ref.at[:, pl.ds(pl.multiple_of((1 - c) * half, 8), half), :]
        cp = pltpu.make_async_remote_copy(src_ref=src, dst_ref=out_ref, send_sem=send_sem, recv_sem=recv_sem,
                                          device_id=(x, y, 1 - c), device_id_type=MESH)
        cp.start()
        cp.wait()

    return pl.pallas_call(body, name="swap_halves", in_specs=[ANY], out_specs=ANY,
                          out_shape=jax.ShapeDtypeStruct((n, half, 128), g.dtype),
                          scratch_shapes=[pltpu.SemaphoreType.DMA, pltpu.SemaphoreType.DMA])(g)


def _scatter_chips(hsum):
    n, half, _ = hsum.shape

    def body(h_ref, out_ref, send_sems, recv_sems):
        x, y, c = _place()
        cps = []
        for j, chip in enumerate(_other_chips(x, y)):
            cp = pltpu.make_async_remote_copy(src_ref=h_ref.at[2 * chip[0] + chip[1]], dst_ref=out_ref.at[j],
                                              send_sem=send_sems.at[j], recv_sem=recv_sems.at[j],
                                              device_id=(*chip, c), device_id_type=MESH)
            cp.start()
            cps.append(cp)
        for cp in cps:
            cp.wait()

    return pl.pallas_call(body, name="scatter_chips", in_specs=[ANY], out_specs=ANY,
                          out_shape=jax.ShapeDtypeStruct((3, half, 128), hsum.dtype),
                          scratch_shapes=[pltpu.SemaphoreType.DMA((3,)), pltpu.SemaphoreType.DMA((3,))])(hsum)


def _join_halves(gh):
    half = gh.shape[0]

    def body(g_ref, out_ref, send_sem, recv_sem, local_sem):
        x, y, c = _place()
        dst = out_ref.at[pl.ds(pl.multiple_of(c * half, 8), half), :]
        mine = pltpu.make_async_copy(g_ref, dst, local_sem)
        mine.start()
        cp = pltpu.make_async_remote_copy(src_ref=g_ref, dst_ref=dst, send_sem=send_sem, recv_sem=recv_sem,
                                          device_id=(x, y, 1 - c), device_id_type=MESH)
        cp.start()
        cp.wait_send()
        other = out_ref.at[pl.ds(pl.multiple_of((1 - c) * half, 8), half), :]
        pltpu.make_async_remote_copy(src_ref=g_ref, dst_ref=other, send_sem=send_sem, recv_sem=recv_sem,
                                     device_id=(x, y, 1 - c), device_id_type=MESH).wait_recv()
        mine.wait()

    return pl.pallas_call(body, name="join_halves", in_specs=[ANY], out_specs=ANY,
                          out_shape=jax.ShapeDtypeStruct((2 * half, 128), gh.dtype),
                          scratch_shapes=[pltpu.SemaphoreType.DMA, pltpu.SemaphoreType.DMA, pltpu.SemaphoreType.DMA])(gh)


def _add_sibling(g, recv, c_idx, name):
    n, rows, _ = g.shape
    half = rows // 2
    tr = _tile(half, ADAM_ROWS, 8)
    nb = half // tr

    def body(c_ref, g_ref, r_ref, o_ref):
        o_ref[...] = g_ref[...] + r_ref[...]

    grid_spec = pltpu.PrefetchScalarGridSpec(
        num_scalar_prefetch=1, grid=(n, nb),
        in_specs=[pl.BlockSpec((None, tr, 128), lambda k, i, c: (k, c[0] * nb + i, 0)),
                  pl.BlockSpec((None, tr, 128), lambda k, i, c: (k, i, 0))],
        out_specs=pl.BlockSpec((None, tr, 128), lambda k, i, c: (k, i, 0)))
    return pl.pallas_call(body, name=name, grid_spec=grid_spec, out_shape=jax.ShapeDtypeStruct((n, half, 128), F32),
                          compiler_params=_params(("parallel", "parallel")))(c_idx, g, recv)


def _add_chips(hsum, recv, chip_idx, name):
    n, half, _ = hsum.shape
    tr = _tile(half, ADAM_ROWS, 8)

    def body(k_ref, h_ref, r_ref, o_ref):
        o_ref[...] = ((h_ref[...] + r_ref[0]) + r_ref[1]) + r_ref[2]

    grid_spec = pltpu.PrefetchScalarGridSpec(
        num_scalar_prefetch=1, grid=(half // tr,),
        in_specs=[pl.BlockSpec((None, tr, 128), lambda i, k: (k[0], i, 0)),
                  pl.BlockSpec((3, tr, 128), lambda i, k: (0, i, 0))],
        out_specs=pl.BlockSpec((tr, 128), lambda i, k: (i, 0)))
    return pl.pallas_call(body, name=name, grid_spec=grid_spec, out_shape=jax.ShapeDtypeStruct((half, 128), F32),
                          compiler_params=_params(("parallel",)))(chip_idx, hsum, recv)


def _adamw_math(g, w, m, v):
    m2 = B1 * m + (1.0 - B1) * g
    v2 = B2 * v + (1.0 - B2) * (g * g)
    delta = -LR * ((m2 / BC1) / (jnp.sqrt(v2 / BC2) + AEPS) + WD * w)
    return delta, m2, v2


def _adamw(g, w, m, v, name):
    rows = g.shape[0]
    tr = _tile(rows, ADAM_ROWS, 8)

    def body(g_ref, w_ref, m_ref, v_ref, d_ref, m2_ref, v2_ref):
        d_ref[...], m2_ref[...], v2_ref[...] = _adamw_math(g_ref[...], w_ref[...], m_ref[...], v_ref[...])

    blk = pl.BlockSpec((tr, 128), lambda i: (i, 0))
    shp = jax.ShapeDtypeStruct((rows, 128), F32)
    return pl.pallas_call(body, name=name, grid=(rows // tr,), in_specs=[blk] * 4, out_specs=(blk,) * 3,
                          out_shape=(shp,) * 3, compiler_params=_params(("parallel",)))(g, w, m, v)


def _adamw_small(parts, w, m, v, name):
    rows = w.shape[0]

    def body(p_ref, w_ref, m_ref, v_ref, g_ref, d_ref, m2_ref, v2_ref):
        g = p_ref[0]
        for k in range(1, 8):
            g = g + p_ref[k]
        g_ref[...] = g
        d_ref[...], m2_ref[...], v2_ref[...] = _adamw_math(g, w_ref[...], m_ref[...], v_ref[...])

    shp = jax.ShapeDtypeStruct((rows, 128), F32)
    return pl.pallas_call(body, name=name, out_shape=(shp,) * 4)(parts, w, m, v)


def _pack_rows(parts, rows):
    flat = jnp.concatenate([p.reshape(-1) for p in parts])
    return jnp.pad(flat, (0, rows * 128 - flat.shape[0])).reshape(rows, 128)


def _unpack(flat, sizes, shapes):
    out, off = [], 0
    for n, s in zip(sizes, shapes):
        out.append(flat[off:off + n].reshape(s))
        off += n
    return out


def _to_shards(full, shard_shape, axis):
    if axis == 0:
        return full.reshape(N_CHIP, -1)
    r, cs = shard_shape
    return full.reshape(r, N_CHIP, cs).transpose(1, 0, 2).reshape(N_CHIP, -1)


def _from_shards(sh, shard_shape, axis):
    r, cs = shard_shape
    if axis == 0:
        return sh.reshape(N_CHIP * r, cs)
    return sh.reshape(N_CHIP, r, cs).transpose(1, 0, 2).reshape(r, N_CHIP * cs)


def _heads(t, n):
    s = t.shape[0]
    return t.reshape(s, n * NH, HD).transpose(1, 0, 2)


def _merge(t):
    return t.transpose(1, 0, 2).reshape(t.shape[1], GW)


def _local_step(x0, mem, tgt, W, gains):
    S = x0.shape[0]
    w_in = jnp.pad(W["w_in"], ((0, 0), (0, IN_PAD - IN_COLS)))
    b_f = jnp.pad(gains["b_forget"], ((0, 0), (0, 128 - NH)))
    cw = W["conv_w"].reshape(3, 2, DFF).transpose(1, 0, 2)
    cb = gains["conv_b"].reshape(2, 1, DFF)

    h1 = _rms_cast(x0, gains["attn_norm_g"], "norm_attn")
    qkv = _heads(_mm_nn(h1, w_in[:, :NQKV], CDT, "proj_qkv"), 6)
    fl = _mm_nn(h1, w_in[:, NQKV:], F32, "proj_gate")
    cum = _gate_fwd(fl, b_f, "gate_cumsum")
    c_hm = cum[:, :NH].T
    cq, ck = c_hm[:, :, None], c_hm[:, None, :]
    fq, fk, fv, sq, sk, sv = (qkv[n * NH:(n + 1) * NH] for n in range(6))
    fo_h, lse = _fox_fwd(fq, fk, fv, cq, ck, "fox_fwd")
    so_h = _sb_fwd(sq, sk, sv, "sb_fwd")
    fo, so = _merge(fo_h), _merge(so_h)
    x1, mixed = _out_proj(fo, so, gains["fox_out_g"], gains["sb_out_g"], W["w_out"], x0, "out_proj")

    h2 = _rms_cast(x1, gains["xattn_norm_g"], "norm_xattn")
    mn = _rms_cast(mem, gains["mem_norm_g"], "norm_mem")
    mq = _mm_nn(h2, W["w_mq"], CDT, "proj_mq")
    kv = _mm_nn(mn, W["w_mkv"], CDT, "proj_mkv")
    x2, mo = _xattn_fwd(mq, kv, W["w_mo"], x1, "xattn_fwd")

    h3 = _rms_cast(x2, gains["ffn_norm_g"], "norm_ffn")
    u0 = _mm_nn(h3, W["w_up"], CDT, "ffn_up", halves=True)
    act = _conv_act(u0, cw, cb, "conv_act")
    x3 = _mm_nn(act, W["w_down"], F32, "ffn_down", tm=512, residual=x2)
    loss, dx3, dg_final = _loss_bwd(x3, tgt, gains["final_norm_g"].reshape(1, D), "loss")

    gw, gs = {}, {"final_norm_g": dg_final}
    da = _mm_nt(dx3, W["w_down"], "ffn_down_dx", tn=1408, out_dtype=CDT)
    gw["w_down"] = _mm_tn(act, dx3, "ffn_down_dw", tka=1408)
    du, dwb = _conv_act_bwd(u0, da, cw, cb, "conv_act_bwd")
    gw["conv_w"] = dwb[:, :3].transpose(1, 0, 2).reshape(3, 2 * DFF)
    gs["conv_b"] = dwb[:, 3].reshape(1, 2 * DFF)
    du0 = _conv_bwd_input(du, cw, "conv_bwd_input")
    gw["w_up"] = _mm_tn(h3, du0, "ffn_up_dw", tn=1408, b_halves=True)
    dx2, gs["ffn_norm_g"] = _mm_nt_rmsbwd(du0, W["w_up"], x2, gains["ffn_norm_g"], dx3, "ffn_up_dx", tk=1408, a_halves=True)

    dmo = _mm_nt(dx2, W["w_mo"], "mo_dx", tn=512, out_dtype=CDT)
    gw["w_mo"] = _mm_tn(mo, dx2, "mo_dw")
    dmq, dkv = _xattn_bwd(mq, kv, dmo, "xattn_bwd")
    gw["w_mq"] = _mm_tn(h2, dmq, "mq_dw")
    dx1, gs["xattn_norm_g"] = _mm_nt_rmsbwd(dmq, W["w_mq"], x1, gains["xattn_norm_g"], dx2, "mq_dx")
    gw["w_mkv"] = _mm_tn(mn, dkv, "mkv_dw")
    _, gs["mem_norm_g"] = _mm_nt_rmsbwd(dkv, W["w_mkv"], mem, gains["mem_norm_g"], jnp.zeros_like(mem), "mkv_dx")

    gw["w_out"] = _mm_tn(mixed, dx1, "out_dw")
    dfo, dso, gs["fox_out_g"], gs["sb_out_g"] = _out_proj_bwd(dx1, W["w_out"], fo, so, gains["fox_out_g"], gains["sb_out_g"], "out_dx")
    dfo_h, dso_h = _heads(dfo, 1), _heads(dso, 1)
    dfq, dfk, dfv, dck, dcq = _fox_bwd(fq, fk, fv, cq, ck, fo_h, dfo_h, lse, "fox_bwd")
    dsq, dsk, dsv = _sb_bwd(sq, sk, sv, so_h, dso_h, "sb_bwd")
    dc = jnp.pad((dck[:, 0, :] + dcq[:, :, 0]).T, ((0, 0), (0, 128 - NH)))
    dfl, db = _gate_bwd(dc, fl, b_f, "gate_bwd")
    gs["b_forget"] = db[:, :NH]
    dqkv = jnp.concatenate([dfq, dfk.astype(CDT), dfv.astype(CDT), dsq, dsk.astype(CDT), dsv.astype(CDT)], axis=0)
    dproj = jnp.concatenate([dqkv.transpose(1, 0, 2).reshape(S, NQKV), dfl.astype(CDT)], axis=1)
    gw["w_in"] = _mm_tn(h1, dproj, "in_dw", tn=640)[:, :IN_COLS]
    dx0, gs["attn_norm_g"] = _mm_nt_rmsbwd(dproj, w_in, x0, gains["attn_norm_g"], dx1, "in_dx", tk=640)
    return loss, dx0, gw, gs


NAMES = ("attn_norm_g", "w_in", "b_forget", "fox_out_g", "sb_out_g", "w_out", "xattn_norm_g", "mem_norm_g", "w_mq",
         "w_mkv", "w_mo", "ffn_norm_g", "w_up", "conv_w", "conv_b", "w_down", "final_norm_g")


def _step(x, mem, loss_target, w, m, v):
    xi, yi, ci = _place()
    big_shapes = [s for _, s, _ in BIG]

    parts = []
    for name, shape, _ in BIG:
        blk = w[name].reshape(shape)
        parts.append(lax.bitcast_convert_type(blk, CDT) if name == "conv_w" else blk.astype(CDT))
    gathered = _gather_weights(_pack_rows(parts, ROWS_G)).reshape(N_CHIP, ROWS_G * 128)
    W, off = {}, 0
    for (name, shape, axis), n in zip(BIG, GATHER_SIZES):
        sh = gathered[:, off:off + n]
        off += n
        if name == "conv_w":
            sh = lax.bitcast_convert_type(sh.reshape(N_CHIP, n // 2, 2), F32)
        W[name] = _from_shards(sh, shape, axis)
    gains = {name: w[name].reshape(1, -1) for name, _ in SMALL}

    loss, grad_x, gw, gs = _local_step(x[0], mem[0], loss_target[0], W, gains)

    g_flat = jnp.concatenate([_to_shards(gw[name], shape, axis) for name, shape, axis in BIG], axis=1)
    g_flat = jnp.pad(g_flat, ((0, 0), (0, ROWS_F * 128 - P_BIG))).reshape(N_CHIP, ROWS_F, 128)
    pair_sum = _add_sibling(g_flat, _swap_halves(g_flat), jnp.reshape(ci, (1,)).astype(jnp.int32), "add_sibling")
    g_half = _add_chips(pair_sum, _scatter_chips(pair_sum), jnp.reshape(2 * xi + yi, (1,)).astype(jnp.int32), "add_chips")
    g_big = _join_halves(g_half)
    small = jnp.concatenate([gs[name].reshape(-1) for name, _ in SMALL] + [loss[0, :1]])
    small = jnp.pad(small, (0, ROWS_S * 128 - P_SMALL)).reshape(ROWS_S, 128)
    small_parts = _gather_small(small).reshape(8, ROWS_S, 128)

    def flat_big(d, prefix):
        return _pack_rows([d[prefix + name].reshape(shape) for name, shape, _ in BIG], ROWS_F)

    def flat_small(d, prefix):
        return _pack_rows([d[prefix + name] for name, _ in SMALL], ROWS_S)

    allw = {**w, **{"m_" + k: a for k, a in m.items()}, **{"v_" + k: a for k, a in v.items()}}
    d_big, m_big, v_big = _adamw(g_big, flat_big(allw, ""), flat_big(allw, "m_"), flat_big(allw, "v_"), "adamw")
    g_sm, d_sm, m_sm, v_sm = _adamw_small(small_parts, flat_small(allw, ""), flat_small(allw, "m_"), flat_small(allw, "v_"), "adamw_small")

    outs = {}
    for prefix, big, sm in (("grad_", g_big, g_sm), ("delta_", d_big, d_sm), ("new_m_", m_big, m_sm), ("new_v_", v_big, v_sm)):
        for (name, _, _), arr in zip(BIG, _unpack(big.reshape(-1), BIG_SIZES, big_shapes)):
            outs[prefix + name] = arr.reshape(w[name].shape)
        for (name, n), arr in zip(SMALL, _unpack(sm.reshape(-1), [n for _, n in SMALL], [(n,) for _, n in SMALL])):
            outs[prefix + name] = arr.reshape(w[name].shape)
    total_loss = g_sm.reshape(-1)[P_SMALL - 1]
    return (total_loss, grad_x[None], *[outs[p + n] for p in ("grad_", "delta_", "new_m_", "new_v_") for n in NAMES])


def kernel(x, mem, attn_norm_g, w_in, b_forget, fox_out_g, sb_out_g, w_out, xattn_norm_g, mem_norm_g, w_mq, w_mkv, w_mo, ffn_norm_g, w_up, conv_w, conv_b, w_down, final_norm_g, loss_target, m_attn_norm_g, m_w_in, m_b_forget, m_fox_out_g, m_sb_out_g, m_w_out, m_xattn_norm_g, m_mem_norm_g, m_w_mq, m_w_mkv, m_w_mo, m_ffn_norm_g, m_w_up, m_conv_w, m_conv_b, m_w_down, m_final_norm_g, v_attn_norm_g, v_w_in, v_b_forget, v_fox_out_g, v_sb_out_g, v_w_out, v_xattn_norm_g, v_mem_norm_g, v_w_mq, v_w_mkv, v_w_mo, v_ffn_norm_g, v_w_up, v_conv_w, v_conv_b, v_w_down, v_final_norm_g):
    given = dict(locals())
    w = {n: given[n] for n in NAMES}
    m = {n: given["m_" + n] for n in NAMES}
    v = {n: given["v_" + n] for n in NAMES}
    return _step(x, mem, loss_target, w, m, v)
```

```python
import functools

import numpy as np
import jax
import jax.numpy as jnp
from jax import lax
from jax.experimental import pallas as pl
from jax.experimental.pallas import tpu as pltpu

F32 = jnp.float32
CDT = jnp.bfloat16
MESH = pl.DeviceIdType.MESH

D = 1024
HD = 64
NH = 8
GW = NH * HD
NQKV = 6 * GW
IN_COLS = NQKV + NH
IN_PAD = NQKV + 128
NMH = 4
MHD = D // NMH
DFF = 2816
EPS = 1e-6
ATT_SCALE = HD ** -0.5
MEM_SCALE = MHD ** -0.5
NEG = -1e30

LR, B1, B2, AEPS, WD, STEP = 0.001, 0.9, 0.999, 1e-08, 0.01, 10
BC1 = 1.0 - B1 ** STEP
BC2 = 1.0 - B2 ** STEP

T_ATT = 512
W_SB = 256
VMEM_LIMIT = 52 * 2 ** 20

N_CHIP = 4
BIG = (("w_in", (D, IN_COLS // N_CHIP), 1), ("w_out", (D // N_CHIP, D), 0), ("w_mq", (D // N_CHIP, D), 0),
       ("w_mkv", (D, 2 * D // N_CHIP), 1), ("w_mo", (D // N_CHIP, D), 0), ("w_up", (D, 2 * DFF // N_CHIP), 1),
       ("conv_w", (3, 2 * DFF // N_CHIP), 1), ("w_down", (DFF // N_CHIP, D), 0))
BIG_SIZES = tuple(int(np.prod(s)) for _, s, _ in BIG)
P_BIG = sum(BIG_SIZES)
ROWS_F = 33 * 1024
assert ROWS_F * 128 >= P_BIG
HALF_F = ROWS_F // 2
ADAM_ROWS = 1536
GATHER_SIZES = tuple(2 * n if name == "conv_w" else n for (name, _, _), n in zip(BIG, BIG_SIZES))
ROWS_G = -(-sum(GATHER_SIZES) // 4096) * 32
HALF_G = ROWS_G // 2
SMALL = (("attn_norm_g", 1024), ("b_forget", 8), ("fox_out_g", 512), ("sb_out_g", 512), ("xattn_norm_g", 1024),
         ("mem_norm_g", 1024), ("ffn_norm_g", 1024), ("conv_b", 2 * DFF), ("final_norm_g", 1024))
P_SMALL = sum(n for _, n in SMALL) + 1
ROWS_S = -(-P_SMALL // 1024) * 8


def _params(sem=None, vmem=VMEM_LIMIT):
    return pltpu.CompilerParams(dimension_semantics=sem, vmem_limit_bytes=vmem)


def _tile(n, pref, mult):
    t = (min(pref, n) // mult) * mult
    while t >= mult:
        if n % t == 0:
            return t
        t -= mult
    return n


def _dot(a, b):
    return jnp.dot(a, b, preferred_element_type=F32)


def _dot_nt(a, b):
    return lax.dot_general(a, b, (((1,), (1,)), ((), ())), preferred_element_type=F32)


def _dot_tn(a, b):
    return lax.dot_general(a, b, (((0,), (0,)), ((), ())), preferred_element_type=F32)


def _split3(x):
    h1 = x.astype(CDT)
    r1 = x - h1.astype(F32)
    h2 = r1.astype(CDT)
    h3 = (r1 - h2.astype(F32)).astype(CDT)
    return h1, h2, h3


def _split2(x):
    h1 = x.astype(CDT)
    return h1, (x - h1.astype(F32)).astype(CDT)


def _rms_bwd(dh, x, g):
    r = lax.rsqrt(jnp.mean(x * x, axis=-1, keepdims=True) + EPS)
    xn = x * r
    dg = jnp.sum(dh * xn, axis=0, keepdims=True)
    dhg = dh * g
    dx = r * (dhg - xn * jnp.mean(dhg * xn, axis=-1, keepdims=True))
    return dx, dg


def _mm_nn(a, b, out_dtype, name, *, tm=1024, tn=512, residual=None, halves=False):
    M, K = a.shape
    N = b.shape[1]
    tm = _tile(M, tm, 16)
    tn = _tile(N // 2 if halves else N, tn, 128)
    nj = N // tn

    def body(*refs):
        a_ref, b_ref = refs[0], refs[1]
        o_ref = refs[-1]
        acc = _dot(a_ref[...].astype(CDT), b_ref[...].astype(CDT))
        if residual is not None:
            acc = acc + refs[2][...]
        o_ref[...] = acc.astype(o_ref.dtype)

    in_specs = [pl.BlockSpec((tm, K), lambda i, j: (i, 0)), pl.BlockSpec((K, tn), lambda i, j: (0, j))]
    ops = [a, b]
    if residual is not None:
        in_specs.append(pl.BlockSpec((tm, tn), lambda i, j: (i, j)))
        ops.append(residual)
    if halves:
        njh = nj // 2
        out_shape = jax.ShapeDtypeStruct((2, M, N // 2), out_dtype)
        out_spec = pl.BlockSpec((None, tm, tn), lambda i, j: (j // njh, i, j % njh))
    else:
        out_shape = jax.ShapeDtypeStruct((M, N), out_dtype)
        out_spec = pl.BlockSpec((tm, tn), lambda i, j: (i, j))
    return pl.pallas_call(body, name=name, grid=(M // tm, nj), in_specs=in_specs, out_specs=out_spec,
                          out_shape=out_shape, compiler_params=_params(("parallel", "parallel")))(*ops)


def _mm_tn(a, b, name, *, tka=512, tn=1024, ts=512, b_halves=False):
    S, Ka = a.shape
    N = 2 * b.shape[2] if b_halves else b.shape[1]
    tka = _tile(Ka, tka, 128)
    tn = _tile(N // 2 if b_halves else N, tn, 128)
    ts = _tile(S, ts, 16)
    nn = N // tn

    def body(a_ref, b_ref, o_ref):
        @pl.when(pl.program_id(2) == 0)
        def _():
            o_ref[...] = jnp.zeros_like(o_ref)
        o_ref[...] += _dot_tn(a_ref[...].astype(CDT), b_ref[...].astype(CDT))

    if b_halves:
        nnh = nn // 2
        b_spec = pl.BlockSpec((None, ts, tn), lambda i, j, s: (j // nnh, s, j % nnh))
    else:
        b_spec = pl.BlockSpec((ts, tn), lambda i, j, s: (s, j))
    return pl.pallas_call(
        body, name=name, grid=(Ka // tka, nn, S // ts),
        in_specs=[pl.BlockSpec((ts, tka), lambda i, j, s: (s, i)), b_spec],
        out_specs=pl.BlockSpec((tka, tn), lambda i, j, s: (i, j)),
        out_shape=jax.ShapeDtypeStruct((Ka, N), F32),
        compiler_params=_params(("parallel", "parallel", "arbitrary")))(a, b)


def _mm_nt(a, b, name, *, tm=512, tn=None, tk=None, a_halves=False, out_dtype=F32,
           epilogue=None, extra=(), extra_specs=(), out_shape=None, out_specs=None):
    if a_halves:
        M, K = a.shape[1], 2 * a.shape[2]
    else:
        M, K = a.shape
    N = b.shape[0]
    tm = _tile(M, tm, 16)
    tn = N if (epilogue is not None or tn is None) else _tile(N, tn, 128)
    tk = K if tk is None else _tile(K // 2 if a_halves else K, tk, 128)
    nk = K // tk
    n_extra = len(extra)

    def body(*refs):
        a_ref, b_ref = refs[0], refs[1]
        extra_refs = refs[2:2 + n_extra]
        out_refs = refs[2 + n_extra:-1]
        acc_ref = refs[-1]
        k = pl.program_id(2)

        @pl.when(k == 0)
        def _():
            acc_ref[...] = jnp.zeros_like(acc_ref)
        acc_ref[...] += _dot_nt(a_ref[...].astype(CDT), b_ref[...].astype(CDT))

        @pl.when(k == nk - 1)
        def _():
            if epilogue is None:
                out_refs[0][...] = acc_ref[...].astype(out_refs[0].dtype)
            else:
                epilogue(acc_ref[...], pl.program_id(0), extra_refs, out_refs)

    if a_halves:
        nkh = nk // 2
        a_spec = pl.BlockSpec((None, tm, tk), lambda i, j, k: (k // nkh, i, k % nkh))
    else:
        a_spec = pl.BlockSpec((tm, tk), lambda i, j, k: (i, k))
    if epilogue is None:
        out_shape = jax.ShapeDtypeStruct((M, N), out_dtype)
        out_specs = pl.BlockSpec((tm, tn), lambda i, j, k: (i, j))
        sem = ("parallel", "parallel", "arbitrary")
    else:
        sem = ("arbitrary", "arbitrary", "arbitrary")
    return pl.pallas_call(
        body, name=name, grid=(M // tm, N // tn, nk),
        in_specs=[a_spec, pl.BlockSpec((tn, tk), lambda i, j, k: (j, k)), *extra_specs],
        out_specs=out_specs, out_shape=out_shape,
        scratch_shapes=[pltpu.VMEM((tm, tn), F32)],
        compiler_params=_params(sem))(a, b, *extra)


def _mm_nt_rmsbwd(a, b, x, g, dres, name, *, tm=512, tk=None, a_halves=False):
    M = x.shape[0]
    tm = _tile(M, tm, 16)

    def epilogue(acc, i, extra_refs, out_refs):
        x_ref, g_ref, r_ref = extra_refs
        dx_ref, dg_ref = out_refs
        dx, dg = _rms_bwd(acc, x_ref[...], g_ref[...])
        dx_ref[...] = r_ref[...] + dx

        @pl.when(i == 0)
        def _():
            dg_ref[...] = jnp.zeros_like(dg_ref)
        dg_ref[...] += dg

    row = pl.BlockSpec((tm, D), lambda i, j, k: (i, 0))
    vec = pl.BlockSpec((1, D), lambda i, j, k: (0, 0))
    return _mm_nt(a, b, name, tm=tm, tk=tk, a_halves=a_halves, epilogue=epilogue,
                  extra=(x, g, dres), extra_specs=(row, vec, row),
                  out_shape=(jax.ShapeDtypeStruct((M, D), F32), jax.ShapeDtypeStruct((1, D), F32)),
                  out_specs=(row, vec))


def _rms_cast(x, g, name, *, tm=512):
    M, W = x.shape
    tm = _tile(M, tm, 16)

    def body(x_ref, g_ref, o_ref):
        xf = x_ref[...]
        r = lax.rsqrt(jnp.mean(xf * xf, axis=-1, keepdims=True) + EPS)
        o_ref[...] = (xf * r * g_ref[...]).astype(o_ref.dtype)

    return pl.pallas_call(body, name=name, grid=(M // tm,),
                          in_specs=[pl.BlockSpec((tm, W), lambda i: (i, 0)), pl.BlockSpec((1, W), lambda i: (0, 0))],
                          out_specs=pl.BlockSpec((tm, W), lambda i: (i, 0)),
                          out_shape=jax.ShapeDtypeStruct((M, W), CDT),
                          compiler_params=_params(("parallel",)))(x, g)


def _tri(n, lower):
    r = lax.broadcasted_iota(jnp.int32, (n, n), 0)
    c = lax.broadcasted_iota(jnp.int32, (n, n), 1)
    return (c <= r if lower else c >= r).astype(CDT)


def _gate_fwd(fl, b, name, *, tm=512):
    S = fl.shape[0]
    tm = _tile(S, tm, 16)

    def body(f_ref, b_ref, c_ref, carry):
        @pl.when(pl.program_id(0) == 0)
        def _():
            carry[...] = jnp.zeros_like(carry)
        z = f_ref[...] + b_ref[...]
        lf = jnp.minimum(z, 0.0) - jnp.log(1.0 + jnp.exp(-jnp.abs(z)))
        tri = _tri(tm, True)
        cum = sum(_dot(tri, p) for p in _split3(lf)) + carry[...]
        c_ref[...] = cum
        carry[...] = cum[tm - 1:tm, :]

    return pl.pallas_call(body, name=name, grid=(S // tm,),
                          in_specs=[pl.BlockSpec((tm, 128), lambda i: (i, 0)), pl.BlockSpec((1, 128), lambda i: (0, 0))],
                          out_specs=pl.BlockSpec((tm, 128), lambda i: (i, 0)),
                          out_shape=jax.ShapeDtypeStruct((S, 128), F32),
                          scratch_shapes=[pltpu.VMEM((1, 128), F32)],
                          compiler_params=_params(("arbitrary",)))(fl, b)


def _gate_bwd(dc, fl, b, name, *, tm=512):
    S = fl.shape[0]
    tm = _tile(S, tm, 16)
    nb = S // tm

    def body(dc_ref, f_ref, b_ref, df_ref, db_ref, carry):
        @pl.when(pl.program_id(0) == 0)
        def _():
            carry[...] = jnp.zeros_like(carry)
            db_ref[...] = jnp.zeros_like(db_ref)
        tri = _tri(tm, False)
        suf = sum(_dot(tri, p) for p in _split3(dc_ref[...])) + carry[...]
        carry[...] = suf[0:1, :]
        df = suf * jax.nn.sigmoid(-(f_ref[...] + b_ref[...]))
        df_ref[...] = df
        db_ref[...] += jnp.sum(df, axis=0, keepdims=True)

    rev = pl.BlockSpec((tm, 128), lambda i: (nb - 1 - i, 0))
    vec = pl.BlockSpec((1, 128), lambda i: (0, 0))
    return pl.pallas_call(body, name=name, grid=(nb,), in_specs=[rev, rev, vec], out_specs=(rev, vec),
                          out_shape=(jax.ShapeDtypeStruct((S, 128), F32), jax.ShapeDtypeStruct((1, 128), F32)),
                          scratch_shapes=[pltpu.VMEM((1, 128), F32)],
                          compiler_params=_params(("arbitrary",)))(dc, fl, b)


def _out_proj(fo, so, gf, gs, w_out, x0, name, *, tm=512):
    S = fo.shape[0]
    tm = _tile(S, tm, 16)

    def body(fo_ref, so_ref, gf_ref, gs_ref, w_ref, x_ref, x1_ref, mx_ref):
        for ref, g_ref, lo in ((fo_ref, gf_ref, 0), (so_ref, gs_ref, GW)):
            o = ref[...]
            r = lax.rsqrt(jnp.mean(o * o, axis=-1, keepdims=True) + EPS)
            mx_ref[:, lo:lo + GW] = (o * r * g_ref[...]).astype(CDT)
        x1_ref[...] = x_ref[...] + _dot(mx_ref[...], w_ref[...])

    half = pl.BlockSpec((tm, GW), lambda i: (i, 0))
    gvec = pl.BlockSpec((1, GW), lambda i: (0, 0))
    row = pl.BlockSpec((tm, D), lambda i: (i, 0))
    return pl.pallas_call(body, name=name, grid=(S // tm,),
                          in_specs=[half, half, gvec, gvec, pl.BlockSpec((D, D), lambda i: (0, 0)), row],
                          out_specs=(row, row),
                          out_shape=(jax.ShapeDtypeStruct((S, D), F32), jax.ShapeDtypeStruct((S, D), CDT)),
                          compiler_params=_params(("parallel",)))(fo, so, gf, gs, w_out, x0)


def _out_proj_bwd(dx1, w_out, fo, so, gf, gs, name, *, tm=512):
    S = fo.shape[0]
    tm = _tile(S, tm, 16)

    def epilogue(acc, i, extra_refs, out_refs):
        fo_ref, so_ref, gf_ref, gs_ref = extra_refs
        dfo_ref, dso_ref, dgf_ref, dgs_ref = out_refs

        @pl.when(i == 0)
        def _():
            dgf_ref[...] = jnp.zeros_like(dgf_ref)
            dgs_ref[...] = jnp.zeros_like(dgs_ref)
        for lo, o_ref, g_ref, do_ref, dg_ref in ((0, fo_ref, gf_ref, dfo_ref, dgf_ref), (GW, so_ref, gs_ref, dso_ref, dgs_ref)):
            dx, dg = _rms_bwd(acc[:, lo:lo + GW], o_ref[...], g_ref[...])
            do_ref[...] = dx.astype(do_ref.dtype)
            dg_ref[...] += dg

    half = pl.BlockSpec((tm, GW), lambda i, j, k: (i, 0))
    gvec = pl.BlockSpec((1, GW), lambda i, j, k: (0, 0))
    return _mm_nt(dx1, w_out, name, tm=tm, epilogue=epilogue, extra=(fo, so, gf, gs),
                  extra_specs=(half, half, gvec, gvec),
                  out_shape=(jax.ShapeDtypeStruct((S, GW), CDT), jax.ShapeDtypeStruct((S, GW), CDT),
                             jax.ShapeDtypeStruct((1, GW), F32), jax.ShapeDtypeStruct((1, GW), F32)),
                  out_specs=(half, half, gvec, gvec))


def _loss_bwd(x3, tgt, g, name, *, tm=512):
    S = x3.shape[0]
    tm = _tile(S, tm, 16)

    def body(x_ref, t_ref, g_ref, dx_ref, loss_ref, dg_ref):
        @pl.when(pl.program_id(0) == 0)
        def _():
            loss_ref[...] = jnp.zeros_like(loss_ref)
            dg_ref[...] = jnp.zeros_like(dg_ref)
        x = x_ref[...]
        gv = g_ref[...]
        r = lax.rsqrt(jnp.mean(x * x, axis=-1, keepdims=True) + EPS)
        xn = x * r
        err = xn * gv - t_ref[...]
        loss_ref[...] += jnp.full(loss_ref.shape, 0.5 * jnp.sum(jnp.mean(err * err, axis=-1, keepdims=True)), F32)
        dy = err * (1.0 / D)
        dg_ref[...] += jnp.sum(dy * xn, axis=0, keepdims=True)
        dyg = dy * gv
        dx_ref[...] = r * (dyg - xn * jnp.mean(dyg * xn, axis=-1, keepdims=True))

    row = pl.BlockSpec((tm, D), lambda i: (i, 0))
    vec = pl.BlockSpec((1, D), lambda i: (0, 0))
    dx3, loss, dg = pl.pallas_call(
        body, name=name, grid=(S // tm,), in_specs=[row, row, vec],
        out_specs=(row, pl.BlockSpec((1, 128), lambda i: (0, 0)), vec),
        out_shape=(jax.ShapeDtypeStruct((S, D), F32), jax.ShapeDtypeStruct((1, 128), F32), jax.ShapeDtypeStruct((1, D), F32)),
        compiler_params=_params(("arbitrary",)))(x3, tgt, g)
    return loss, dx3, dg


def _pairs(nq, descending=True):
    qi = [i for i in range(nq) for _ in range(i + 1)]
    kj = [kb for i in range(nq) for kb in (range(i, -1, -1) if descending else range(i + 1))]
    return jnp.asarray(np.asarray(qi, np.int32)), jnp.asarray(np.asarray(kj, np.int32))


def _att_specs(T):
    qblk = pl.BlockSpec((1, T, HD), lambda h, n, qi, kj: (h, qi[n], 0))
    kblk = pl.BlockSpec((1, T, HD), lambda h, n, qi, kj: (h, kj[n], 0))
    qcol = pl.BlockSpec((1, T, 1), lambda h, n, qi, kj: (h, qi[n], 0))
    krow = pl.BlockSpec((1, 1, T), lambda h, n, qi, kj: (h, 0, kj[n]))
    return qblk, kblk, qcol, krow


def _causal(T, Tk, off, strict):
    row = lax.broadcasted_iota(jnp.int32, (T, Tk), 0)
    col = lax.broadcasted_iota(jnp.int32, (T, Tk), 1) + off
    return col < row if strict else col <= row


def _diag_or_not(kb, i, step):
    pl.when(kb == i)(functools.partial(step, True))
    pl.when(kb != i)(functools.partial(step, False))


def _fox_fwd(q, k, v, cq, ck, name):
    S = q.shape[1]
    T = min(T_ATT, S)
    qi, kj = _pairs(S // T)
    qblk, kblk, qcol, krow = _att_specs(T)

    def body(qi_ref, kj_ref, q_ref, k_ref, v_ref, cq_ref, ck_ref, o_ref, lse_ref, m_s, l_s, acc_s):
        n = pl.program_id(1)
        i, kb = qi_ref[n], kj_ref[n]

        @pl.when(kb == i)
        def _():
            m_s[...] = jnp.full_like(m_s, NEG)
            l_s[...] = jnp.zeros_like(l_s)
            acc_s[...] = jnp.zeros_like(acc_s)

        def step(masked):
            s = _dot_nt(q_ref[0] * ATT_SCALE, k_ref[0]) + cq_ref[0] - ck_ref[0]
            if masked:
                s = jnp.where(_causal(T, T, 0, False), s, NEG)
            m_new = jnp.maximum(m_s[...], jnp.max(s, axis=-1, keepdims=True))
            alpha = jnp.exp(m_s[...] - m_new)
            p = jnp.exp(s - m_new)
            l_s[...] = alpha * l_s[...] + jnp.sum(p, axis=-1, keepdims=True)
            acc_s[...] = alpha * acc_s[...] + _dot(p.astype(CDT), v_ref[0])
            m_s[...] = m_new

        _diag_or_not(kb, i, step)

        @pl.when(kb == 0)
        def _():
            o_ref[0] = acc_s[...] / l_s[...]
            lse_ref[0] = m_s[...] + jnp.log(l_s[...])

    grid_spec = pltpu.PrefetchScalarGridSpec(
        num_scalar_prefetch=2, grid=(NH, int(qi.shape[0])),
        in_specs=[qblk, kblk, kblk, qcol, krow],
        out_specs=(pl.BlockSpec((1, T, HD), lambda h, n, qi, kj: (h, qi[n], 0)), qcol),
        scratch_shapes=[pltpu.VMEM((T, 1), F32), pltpu.VMEM((T, 1), F32), pltpu.VMEM((T, HD), F32)])
    return pl.pallas_call(body, name=name, grid_spec=grid_spec,
                          out_shape=(jax.ShapeDtypeStruct((NH, S, HD), F32), jax.ShapeDtypeStruct((NH, S, 1), F32)),
                          compiler_params=_params(("parallel", "arbitrary")))(qi, kj, q, k, v, cq, ck)


def _fox_bwd(q, k, v, cq, ck, o, do, lse, name):
    S = q.shape[1]
    T = min(T_ATT, S)
    qi, kj = _pairs(S // T)
    qblk, kblk, qcol, krow = _att_specs(T)

    def body(qi_ref, kj_ref, q_ref, k_ref, v_ref, cq_ref, ck_ref, o_ref, do_ref, lse_ref,
             dq_ref, dk_ref, dv_ref, dck_ref, dcq_ref, dq_s, dl_s, dcq_s):
        n = pl.program_id(1)
        i, kb = qi_ref[n], kj_ref[n]

        @pl.when(n == 0)
        def _():
            dk_ref[...] = jnp.zeros_like(dk_ref)
            dv_ref[...] = jnp.zeros_like(dv_ref)
            dck_ref[...] = jnp.zeros_like(dck_ref)

        @pl.when(kb == i)
        def _():
            dq_s[...] = jnp.zeros_like(dq_s)
            dcq_s[...] = jnp.zeros_like(dcq_s)
            dl_s[...] = jnp.sum(do_ref[0].astype(F32) * o_ref[0], axis=-1, keepdims=True)

        def step(masked):
            qs = q_ref[0] * ATT_SCALE
            do = do_ref[0]
            p = jnp.exp(_dot_nt(qs, k_ref[0]) + cq_ref[0] - ck_ref[0] - lse_ref[0])
            if masked:
                p = jnp.where(_causal(T, T, 0, False), p, 0.0)
            ds = p * (_dot_nt(do, v_ref[0]) - dl_s[...])
            dsb = ds.astype(CDT)
            dq_s[...] += _dot(dsb, k_ref[0])
            rows = pl.ds(pl.multiple_of(kb * T, T), T)
            dk_ref[0, rows, :] += _dot_tn(dsb, qs)
            dv_ref[0, rows, :] += _dot_tn(p.astype(CDT), do)
            dck_ref[0, :, rows] += -jnp.sum(ds, axis=0, keepdims=True)
            dcq_s[...] += jnp.sum(ds, axis=-1, keepdims=True)

        _diag_or_not(kb, i, step)

        @pl.when(kb == 0)
        def _():
            dq_ref[0] = (dq_s[...] * ATT_SCALE).astype(dq_ref.dtype)
            dcq_ref[0] = dcq_s[...]

    whole = pl.BlockSpec((1, S, HD), lambda h, n, qi, kj: (h, 0, 0))
    grid_spec = pltpu.PrefetchScalarGridSpec(
        num_scalar_prefetch=2, grid=(NH, int(qi.shape[0])),
        in_specs=[qblk, kblk, kblk, qcol, krow, qblk, qblk, qcol],
        out_specs=(qblk, whole, whole, pl.BlockSpec((1, 1, S), lambda h, n, qi, kj: (h, 0, 0)), qcol),
        scratch_shapes=[pltpu.VMEM((T, HD), F32), pltpu.VMEM((T, 1), F32), pltpu.VMEM((T, 1), F32)])
    return pl.pallas_call(body, name=name, grid_spec=grid_spec,
                          out_shape=(jax.ShapeDtypeStruct((NH, S, HD), CDT), jax.ShapeDtypeStruct((NH, S, HD), F32),
                                     jax.ShapeDtypeStruct((NH, S, HD), F32), jax.ShapeDtypeStruct((NH, 1, S), F32),
                                     jax.ShapeDtypeStruct((NH, S, 1), F32)),
                          compiler_params=_params(("parallel", "arbitrary")))(qi, kj, q, k, v, cq, ck, o, do, lse)


LOG2E = 1.4426950408889634


def _sb_softplus2(qs, ksub, mask):
    z2 = _dot_nt(qs, ksub) * LOG2E
    sp2 = jnp.maximum(z2, 0.0) + jnp.log2(1.0 + jnp.exp2(-jnp.abs(z2)))
    return z2, sp2 if mask is None else jnp.where(mask, sp2, 0.0)


def _strict_tri(n, upper, value):
    r = lax.broadcasted_iota(jnp.int32, (n, n), 0)
    c = lax.broadcasted_iota(jnp.int32, (n, n), 1)
    return jnp.where(r < c if upper else r > c, value, 0.0).astype(CDT)


def _sb_fwd(q, k, v, name):
    S = q.shape[1]
    T = min(T_ATT, S)
    W = min(W_SB, T)
    qi, kj = _pairs(S // T)
    qblk, kblk, qcol, _ = _att_specs(T)

    def body(qi_ref, kj_ref, q_ref, k_ref, v_ref, o_ref, lt_ref, run_s, acc_s):
        n = pl.program_id(1)
        i, kb = qi_ref[n], kj_ref[n]

        @pl.when(kb == i)
        def _():
            run_s[...] = jnp.zeros_like(run_s)
            acc_s[...] = jnp.zeros_like(acc_s)

        def step(masked):
            qs = q_ref[0] * ATT_SCALE
            neg_later = _strict_tri(W, False, -1.0)
            for sub in range(T // W - 1, -1, -1):
                cols = slice(sub * W, (sub + 1) * W)
                mask = _causal(T, W, sub * W, True) if masked else None
                z2, sp2 = _sb_softplus2(qs, k_ref[0, cols, :], mask)
                excl = _dot(sp2.astype(CDT), neg_later)
                a = jnp.exp2((z2 - sp2) + (excl + run_s[...]))
                if masked:
                    a = jnp.where(mask, a, 0.0)
                acc_s[...] += _dot(a.astype(CDT), v_ref[0, cols, :])
                run_s[...] += excl[:, 0:1] - sp2[:, 0:1]

        _diag_or_not(kb, i, step)

        @pl.when(kb == 0)
        def _():
            o_ref[0] = acc_s[...]
            lt_ref[0] = run_s[...]

    grid_spec = pltpu.PrefetchScalarGridSpec(
        num_scalar_prefetch=2, grid=(NH, int(qi.shape[0])), in_specs=[qblk, kblk, kblk], out_specs=(qblk, qcol),
        scratch_shapes=[pltpu.VMEM((T, 1), F32), pltpu.VMEM((T, HD), F32)])
    return pl.pallas_call(body, name=name, grid_spec=grid_spec,
                          out_shape=(jax.ShapeDtypeStruct((NH, S, HD), F32), jax.ShapeDtypeStruct((NH, S, 1), F32)),
                          compiler_params=_params(("parallel", "arbitrary")))(qi, kj, q, k, v)


def _sb_bwd(q, k, v, do, lt, name):
    S = q.shape[1]
    T = min(T_ATT, S)
    W = min(W_SB, T)
    qi, kj = _pairs(S // T, descending=False)
    qblk, kblk, qcol, _ = _att_specs(T)

    def body(qi_ref, kj_ref, q_ref, k_ref, v_ref, do_ref, lt_ref, dq_ref, dk_ref, dv_ref, passed_s, gsum_s, dq_s):
        n = pl.program_id(1)
        i, kb = qi_ref[n], kj_ref[n]

        @pl.when(n == 0)
        def _():
            dk_ref[...] = jnp.zeros_like(dk_ref)
            dv_ref[...] = jnp.zeros_like(dv_ref)

        @pl.when(kb == 0)
        def _():
            passed_s[...] = jnp.zeros_like(passed_s)
            gsum_s[...] = jnp.zeros_like(gsum_s)
            dq_s[...] = jnp.zeros_like(dq_s)

        def step(masked):
            qs = q_ref[0] * ATT_SCALE
            do = do_ref[0]
            neg_later = _strict_tri(W, False, -1.0)
            earlier = _strict_tri(W, True, 1.0)
            for sub in range(T // W):
                cols = slice(sub * W, (sub + 1) * W)
                mask = _causal(T, W, sub * W, True) if masked else None
                ksub = k_ref[0, cols, :]
                z2, sp2 = _sb_softplus2(qs, ksub, mask)
                excl = _dot(sp2.astype(CDT), neg_later)
                through = passed_s[...] + (excl[:, 0:1] - sp2[:, 0:1])
                t1 = z2 - sp2
                sig = jnp.exp2(t1)
                a = jnp.exp2(t1 + (excl + (lt_ref[0] - through)))
                if masked:
                    a = jnp.where(mask, a, 0.0)
                dl = _dot_nt(do, v_ref[0, cols, :]) * a
                before = _dot(dl.astype(CDT), earlier)
                dz = dl - sig * (dl + (before + gsum_s[...]))
                if masked:
                    dz = jnp.where(mask, dz, 0.0)
                dzb = dz.astype(CDT)
                dq_s[...] += _dot(dzb, ksub)
                rows = pl.ds(pl.multiple_of(kb * T + sub * W, W), W)
                dk_ref[0, rows, :] += _dot_tn(dzb, qs)
                dv_ref[0, rows, :] += _dot_tn(a.astype(CDT), do)
                passed_s[...] = through
                gsum_s[...] += before[:, W - 1:W] + dl[:, W - 1:W]

        _diag_or_not(kb, i, step)

        @pl.when(kb == i)
        def _():
            dq_ref[0] = (dq_s[...] * ATT_SCALE).astype(dq_ref.dtype)

    whole = pl.BlockSpec((1, S, HD), lambda h, n, qi, kj: (h, 0, 0))
    grid_spec = pltpu.PrefetchScalarGridSpec(
        num_scalar_prefetch=2, grid=(NH, int(qi.shape[0])), in_specs=[qblk, kblk, kblk, qblk, qcol],
        out_specs=(qblk, whole, whole),
        scratch_shapes=[pltpu.VMEM((T, 1), F32), pltpu.VMEM((T, 1), F32), pltpu.VMEM((T, HD), F32)])
    return pl.pallas_call(body, name=name, grid_spec=grid_spec,
                          out_shape=(jax.ShapeDtypeStruct((NH, S, HD), CDT), jax.ShapeDtypeStruct((NH, S, HD), F32),
                                     jax.ShapeDtypeStruct((NH, S, HD), F32)),
                          compiler_params=_params(("parallel", "arbitrary")))(qi, kj, q, k, v, do, lt)


def _mem_probs(q_ref, kv_ref, h):
    cols = slice(h * MHD, (h + 1) * MHD)
    s = _dot_nt(q_ref[:, cols], kv_ref[:, cols]) * MEM_SCALE
    e = jnp.exp(s - jnp.max(s, axis=-1, keepdims=True))
    return e / jnp.sum(e, axis=-1, keepdims=True)


def _xattn_fwd(q, kv, w_mo, x1, name, *, tm=512):
    S = q.shape[0]
    tm = _tile(S, tm, 16)
    nm = kv.shape[0]

    def body(q_ref, kv_ref, w_ref, x_ref, x2_ref, o_ref):
        for h in range(NMH):
            p = _mem_probs(q_ref, kv_ref, h)
            o_ref[:, h * MHD:(h + 1) * MHD] = _dot(p.astype(CDT), kv_ref[:, D + h * MHD:D + (h + 1) * MHD]).astype(CDT)
        x2_ref[...] = x_ref[...] + _dot(o_ref[...], w_ref[...])

    row = pl.BlockSpec((tm, D), lambda i: (i, 0))
    return pl.pallas_call(body, name=name, grid=(S // tm,),
                          in_specs=[row, pl.BlockSpec((nm, 2 * D), lambda i: (0, 0)), pl.BlockSpec((D, D), lambda i: (0, 0)), row],
                          out_specs=(row, row),
                          out_shape=(jax.ShapeDtypeStruct((S, D), F32), jax.ShapeDtypeStruct((S, D), CDT)),
                          compiler_params=_params(("parallel",)))(q, kv, w_mo, x1)


def _xattn_bwd(q, kv, do, name, *, tm=512):
    S = q.shape[0]
    tm = _tile(S, tm, 16)
    nm = kv.shape[0]

    def body(q_ref, kv_ref, do_ref, dq_ref, dkv_ref):
        @pl.when(pl.program_id(0) == 0)
        def _():
            dkv_ref[...] = jnp.zeros_like(dkv_ref)
        for h in range(NMH):
            cols = slice(h * MHD, (h + 1) * MHD)
            vcols = slice(D + h * MHD, D + (h + 1) * MHD)
            p = _mem_probs(q_ref, kv_ref, h)
            doh = do_ref[:, cols]
            dp = _dot_nt(doh, kv_ref[:, vcols])
            ds = (p * (dp - jnp.sum(p * dp, axis=-1, keepdims=True)) * MEM_SCALE).astype(CDT)
            dq_ref[:, cols] = _dot(ds, kv_ref[:, cols]).astype(CDT)
            dkv_ref[:, cols] += _dot_tn(ds, q_ref[:, cols])
            dkv_ref[:, vcols] += _dot_tn(p.astype(CDT), doh)

    row = pl.BlockSpec((tm, D), lambda i: (i, 0))
    kvs = pl.BlockSpec((nm, 2 * D), lambda i: (0, 0))
    return pl.pallas_call(body, name=name, grid=(S // tm,), in_specs=[row, kvs, row], out_specs=(row, kvs),
                          out_shape=(jax.ShapeDtypeStruct((S, D), CDT), jax.ShapeDtypeStruct((nm, 2 * D), F32)),
                          compiler_params=_params(("arbitrary",)))(q, kv, do)


HALO = 16


def _shift_down(u, prev, s):
    rolled = pltpu.roll(u, s, 0)
    r = lax.broadcasted_iota(jnp.int32, u.shape, 0)
    for t in range(s):
        rolled = jnp.where(r == t, prev[HALO - s + t:HALO - s + t + 1, :], rolled)
    return rolled


def _shift_up(u, nxt, s):
    n = u.shape[0]
    rolled = pltpu.roll(u, n - s, 0)
    r = lax.broadcasted_iota(jnp.int32, u.shape, 0)
    for t in range(s):
        rolled = jnp.where(r == n - s + t, nxt[t:t + 1, :], rolled)
    return rolled


def _conv_taps(u_ref, h_ref, first):
    u = u_ref[...].astype(F32)
    prev = jnp.where(first, 0.0, h_ref[...].astype(F32))
    out = []
    for half in range(2):
        out.append((u[half], _shift_down(u[half], prev[half], 1), _shift_down(u[half], prev[half], 2)))
    return out


def _conv_specs(tm, tn, nsb):
    blk = pl.BlockSpec((2, tm, tn), lambda j, i: (0, i, j))
    prev = pl.BlockSpec((2, HALO, tn), lambda j, i: (0, jnp.maximum(i * (tm // HALO) - 1, 0), j))
    nxt = pl.BlockSpec((2, HALO, tn), lambda j, i: (0, jnp.minimum((i + 1) * (tm // HALO), nsb - 1), j))
    w = pl.BlockSpec((2, 3, tn), lambda j, i: (0, 0, j))
    b = pl.BlockSpec((2, 1, tn), lambda j, i: (0, 0, j))
    return blk, prev, nxt, w, b


def _conv_apply(taps, w_ref, b_ref):
    ys = []
    for half in range(2):
        u, u1, u2 = taps[half]
        w = w_ref[half]
        ys.append(b_ref[half] + u2 * w[0:1, :] + u1 * w[1:2, :] + u * w[2:3, :])
    return ys


def _conv_act(u0, cw, cb, name, *, tm=512, tn=256):
    _, S, F = u0.shape
    tm = _tile(S, tm, HALO)
    tn = _tile(F, tn, 128)
    blk, prev, _, w, b = _conv_specs(tm, tn, S // HALO)

    def body(u_ref, h_ref, w_ref, b_ref, a_ref):
        yg, yv = _conv_apply(_conv_taps(u_ref, h_ref, pl.program_id(1) == 0), w_ref, b_ref)
        a_ref[...] = (yg * jax.nn.sigmoid(yg) * yv).astype(a_ref.dtype)

    return pl.pallas_call(body, name=name, grid=(F // tn, S // tm), in_specs=[blk, prev, w, b],
                          out_specs=pl.BlockSpec((tm, tn), lambda j, i: (i, j)),
                          out_shape=jax.ShapeDtypeStruct((S, F), CDT),
                          compiler_params=_params(("parallel", "parallel")))(u0, u0, cw, cb)


def _conv_act_bwd(u0, da, cw, cb, name, *, tm=512, tn=256):
    _, S, F = u0.shape
    tm = _tile(S, tm, HALO)
    tn = _tile(F, tn, 128)
    blk, prev, _, w, b = _conv_specs(tm, tn, S // HALO)

    def body(u_ref, h_ref, da_ref, w_ref, b_ref, du_ref, dwb_ref):
        @pl.when(pl.program_id(1) == 0)
        def _():
            dwb_ref[...] = jnp.zeros_like(dwb_ref)
        taps = _conv_taps(u_ref, h_ref, pl.program_id(1) == 0)
        yg, yv = _conv_apply(taps, w_ref, b_ref)
        sg = jax.nn.sigmoid(yg)
        da = da_ref[...].astype(F32)
        dus = (da * yv * sg * (1.0 + yg * (1.0 - sg)), da * yg * sg)
        for half in range(2):
            du = dus[half]
            du_ref[half] = du.astype(du_ref.dtype)
            u, u1, u2 = taps[half]
            for row, term in enumerate((du * u2, du * u1, du * u, du)):
                dwb_ref[half, row:row + 1, :] += jnp.sum(term, axis=0, keepdims=True)

    return pl.pallas_call(body, name=name, grid=(F // tn, S // tm),
                          in_specs=[blk, prev, pl.BlockSpec((tm, tn), lambda j, i: (i, j)), w, b],
                          out_specs=(blk, pl.BlockSpec((2, 4, tn), lambda j, i: (0, 0, j))),
                          out_shape=(jax.ShapeDtypeStruct((2, S, F), CDT), jax.ShapeDtypeStruct((2, 4, F), F32)),
                          compiler_params=_params(("parallel", "arbitrary")))(u0, u0, da, cw, cb)


def _conv_bwd_input(du, cw, name, *, tm=512, tn=256):
    _, S, F = du.shape
    tm = _tile(S, tm, HALO)
    tn = _tile(F, tn, 128)
    blk, _, nxt, w, _ = _conv_specs(tm, tn, S // HALO)
    ni = S // tm

    def body(d_ref, h_ref, w_ref, o_ref):
        d = d_ref[...].astype(F32)
        nx = jnp.where(pl.program_id(1) == ni - 1, 0.0, h_ref[...].astype(F32))
        for half in range(2):
            wv = w_ref[half]
            y = d[half] * wv[2:3, :] + _shift_up(d[half], nx[half], 1) * wv[1:2, :] + _shift_up(d[half], nx[half], 2) * wv[0:1, :]
            o_ref[half] = y.astype(o_ref.dtype)

    return pl.pallas_call(body, name=name, grid=(F // tn, ni), in_specs=[blk, nxt, w], out_specs=blk,
                          out_shape=jax.ShapeDtypeStruct((2, S, F), CDT),
                          compiler_params=_params(("parallel", "parallel")))(du, du, cw)


ANY = pl.BlockSpec(memory_space=pl.ANY)


def _place():
    return lax.axis_index("x"), lax.axis_index("y"), lax.axis_index("c")


def _other_chips(x, y):
    return ((1 - x, y), (x, 1 - y), (1 - x, 1 - y))


def _gather_weights(wsh):
    rows, half = wsh.shape[0], wsh.shape[0] // 2

    def body(w_ref, out_ref, send_sems, recv_sems, local_sem):
        x, y, c = _place()
        chips = _other_chips(x, y)

        def part(chip, pc):
            return out_ref.at[2 * chip[0] + chip[1], pl.ds(pl.multiple_of(pc * half, 16), half), :]

        def copy(k, chip, pc, to, src=None):
            return pltpu.make_async_remote_copy(src_ref=part(chip, pc) if src is None else src, dst_ref=part(chip, pc),
                                                send_sem=send_sems.at[k], recv_sem=recv_sems.at[k],
                                                device_id=to, device_id_type=MESH)

        mine = pltpu.make_async_copy(w_ref, out_ref.at[2 * x + y], local_sem)
        mine.start()
        my_half = w_ref.at[pl.ds(pl.multiple_of(c * half, 16), half), :]
        first = [copy(j, (x, y), c, (*chip, c), src=my_half) for j, chip in enumerate(chips)]
        for cp in first:
            cp.start()
        passed = [copy(3 + j, chip, c, (x, y, 1 - c)) for j, chip in enumerate(chips)]
        for j, chip in enumerate(chips):
            copy(j, chip, c, (x, y, c)).wait_recv()
            passed[j].start()
        for j, chip in enumerate(chips):
            copy(3 + j, chip, 1 - c, (x, y, c)).wait_recv()
        for cp in first + passed:
            cp.wait_send()
        mine.wait()

    return pl.pallas_call(body, name="gather_weights", in_specs=[ANY], out_specs=ANY,
                          out_shape=jax.ShapeDtypeStruct((N_CHIP, rows, 128), wsh.dtype),
                          scratch_shapes=[pltpu.SemaphoreType.DMA((6,)), pltpu.SemaphoreType.DMA((6,)), pltpu.SemaphoreType.DMA])(wsh)


def _gather_small(v):
    m = v.shape[0]

    def body(v_ref, out_ref, send_sems, recv_sems, local_sem):
        x, y, c = _place()
        me, sibling = (x, y, c), (x, y, 1 - c)
        chips = _other_chips(x, y)

        def rows(px, py, pc):
            return out_ref.at[pl.ds((4 * px + 2 * py + pc) * m, m), :]

        def copy(k, block, to, src=None):
            return pltpu.make_async_remote_copy(src_ref=rows(*block) if src is None else src, dst_ref=rows(*block),
                                                send_sem=send_sems.at[k], recv_sem=recv_sems.at[k],
                                                device_id=to, device_id_type=MESH)

        mine = pltpu.make_async_copy(v_ref, rows(*me), local_sem)
        mine.start()
        first = [copy(0, me, sibling, src=v_ref)]
        first += [copy(1 + j, me, (*chip, c), src=v_ref) for j, chip in enumerate(chips)]
        for cp in first:
            cp.start()
        passed = [copy(4 + j, (*chip, c), sibling) for j, chip in enumerate(chips)]
        for j, chip in enumerate(chips):
            copy(1 + j, (*chip, c), me).wait_recv()
            passed[j].start()
        copy(0, sibling, me).wait_recv()
        for j, chip in enumerate(chips):
            copy(4 + j, (*chip, 1 - c), me).wait_recv()
        for cp in first + passed:
            cp.wait_send()
        mine.wait()

    vm = pl.BlockSpec(memory_space=pltpu.VMEM)
    return pl.pallas_call(body, name="gather_small", in_specs=[vm], out_specs=vm,
                          out_shape=jax.ShapeDtypeStruct((8 * m, 128), v.dtype),
                          scratch_shapes=[pltpu.SemaphoreType.DMA((7,)), pltpu.SemaphoreType.DMA((7,)), pltpu.SemaphoreType.DMA])(v)


def _swap_halves(g):
    n, rows, _ = g.shape
    half = rows // 2

    def body(g_ref, out_ref, send_sem, recv_sem):
        x, y, c = _place()
        src = g_ref.at[:, pl.ds(pl.multiple_of((1 - c) * half, 8), half), :]
        cp = pltpu.make_async_remote_copy(src_ref=src, dst_ref=out_ref, send_sem=send_sem, recv_sem=recv_sem,
                                          device_id=(x, y, 1 - c), device_id_type=MESH)
        cp.start()
        cp.wait()

    return pl.pallas_call(body, name="swap_halves", in_specs=[ANY], out_specs=ANY,
                          out_shape=jax.ShapeDtypeStruct((n, half, 128), g.dtype),
                          scratch_shapes=[pltpu.SemaphoreType.DMA, pltpu.SemaphoreType.DMA])(g)


def _scatter_chips(hsum):
    n, half, _ = hsum.shape

    def body(h_ref, out_ref, send_sems, recv_sems):
        x, y, c = _place()
        cps = []
        for j, chip in enumerate(_other_chips(x, y)):
            cp = pltpu.make_async_remote_copy(src_ref=h_ref.at[2 * chip[0] + chip[1]], dst_ref=out_ref.at[j],
                                              send_sem=send_sems.at[j], recv_sem=recv_sems.at[j],
                                              device_id=(*chip, c), device_id_type=MESH)
            cp.start()
            cps.append(cp)
        for cp in cps:
            cp.wait()

    return pl.pallas_call(body, name="scatter_chips", in_specs=[ANY], out_specs=ANY,
                          out_shape=jax.ShapeDtypeStruct((3, half, 128), hsum.dtype),
                          scratch_shapes=[pltpu.SemaphoreType.DMA((3,)), pltpu.SemaphoreType.DMA((3,))])(hsum)


def _join_halves(gh):
    half = gh.shape[0]

    def body(g_ref, out_ref, send_sem, recv_sem, local_sem):
        x, y, c = _place()
        dst = out_ref.at[pl.ds(pl.multiple_of(c * half, 8), half), :]
        mine = pltpu.make_async_copy(g_ref, dst, local_sem)
        mine.start()
        cp = pltpu.make_async_remote_copy(src_ref=g_ref, dst_ref=dst, send_sem=send_sem, recv_sem=recv_sem,
                                          device_id=(x, y, 1 - c), device_id_type=MESH)
        cp.start()
        cp.wait_send()
        other = out_ref.at[pl.ds(pl.multiple_of((1 - c) * half, 8), half), :]
        pltpu.make_async_remote_copy(src_ref=g_ref, dst_ref=other, send_sem=send_sem, recv_sem=recv_sem,
                                     device_id=(x, y, 1 - c), device_id_type=MESH).wait_recv()
        mine.wait()

    return pl.pallas_call(body, name="join_halves", in_specs=[ANY], out_specs=ANY,
                          out_shape=jax.ShapeDtypeStruct((2 * half, 128), gh.dtype),
                          scratch_shapes=[pltpu.SemaphoreType.DMA, pltpu.SemaphoreType.DMA, pltpu.SemaphoreType.DMA])(gh)


def _add_sibling(g, recv, c_idx, name):
    n, rows, _ = g.shape
    half = rows // 2
    tr = _tile(half, ADAM_ROWS, 8)
    nb = half // tr

    def body(c_ref, g_ref, r_ref, o_ref):
        o_ref[...] = g_ref[...] + r_ref[...]

    grid_spec = pltpu.PrefetchScalarGridSpec(
        num_scalar_prefetch=1, grid=(n, nb),
        in_specs=[pl.BlockSpec((None, tr, 128), lambda k, i, c: (k, c[0] * nb + i, 0)),
                  pl.BlockSpec((None, tr, 128), lambda k, i, c: (k, i, 0))],
        out_specs=pl.BlockSpec((None, tr, 128), lambda k, i, c: (k, i, 0)))
    return pl.pallas_call(body, name=name, grid_spec=grid_spec, out_shape=jax.ShapeDtypeStruct((n, half, 128), F32),
                          compiler_params=_params(("parallel", "parallel")))(c_idx, g, recv)


def _add_chips(hsum, recv, chip_idx, name):
    n, half, _ = hsum.shape
    tr = _tile(half, ADAM_ROWS, 8)

    def body(k_ref, h_ref, r_ref, o_ref):
        o_ref[...] = ((h_ref[...] + r_ref[0]) + r_ref[1]) + r_ref[2]

    grid_spec = pltpu.PrefetchScalarGridSpec(
        num_scalar_prefetch=1, grid=(half // tr,),
        in_specs=[pl.BlockSpec((None, tr, 128), lambda i, k: (k[0], i, 0)),
                  pl.BlockSpec((3, tr, 128), lambda i, k: (0, i, 0))],
        out_specs=pl.BlockSpec((tr, 128), lambda i, k: (i, 0)))
    return pl.pallas_call(body, name=name, grid_spec=grid_spec, out_shape=jax.ShapeDtypeStruct((half, 128), F32),
                          compiler_params=_params(("parallel",)))(chip_idx, hsum, recv)


def _adamw_math(g, w, m, v):
    m2 = B1 * m + (1.0 - B1) * g
    v2 = B2 * v + (1.0 - B2) * (g * g)
    delta = -LR * ((m2 / BC1) / (jnp.sqrt(v2 / BC2) + AEPS) + WD * w)
    return delta, m2, v2


def _adamw(g, w, m, v, name):
    rows = g.shape[0]
    tr = _tile(rows, ADAM_ROWS, 8)

    def body(g_ref, w_ref, m_ref, v_ref, d_ref, m2_ref, v2_ref):
        d_ref[...], m2_ref[...], v2_ref[...] = _adamw_math(g_ref[...], w_ref[...], m_ref[...], v_ref[...])

    blk = pl.BlockSpec((tr, 128), lambda i: (i, 0))
    shp = jax.ShapeDtypeStruct((rows, 128), F32)
    return pl.pallas_call(body, name=name, grid=(rows // tr,), in_specs=[blk] * 4, out_specs=(blk,) * 3,
                          out_shape=(shp,) * 3, compiler_params=_params(("parallel",)))(g, w, m, v)


def _adamw_small(parts, w, m, v, name):
    rows = w.shape[0]

    def body(p_ref, w_ref, m_ref, v_ref, g_ref, d_ref, m2_ref, v2_ref):
        g = p_ref[0]
        for k in range(1, 8):
            g = g + p_ref[k]
        g_ref[...] = g
        d_ref[...], m2_ref[...], v2_ref[...] = _adamw_math(g, w_ref[...], m_ref[...], v_ref[...])

    shp = jax.ShapeDtypeStruct((rows, 128), F32)
    return pl.pallas_call(body, name=name, out_shape=(shp,) * 4)(parts, w, m, v)


def _pack_rows(parts, rows):
    flat = jnp.concatenate([p.reshape(-1) for p in parts])
    return jnp.pad(flat, (0, rows * 128 - flat.shape[0])).reshape(rows, 128)


def _unpack(flat, sizes, shapes):
    out, off = [], 0
    for n, s in zip(sizes, shapes):
        out.append(flat[off:off + n].reshape(s))
        off += n
    return out


def _to_shards(full, shard_shape, axis):
    if axis == 0:
        return full.reshape(N_CHIP, -1)
    r, cs = shard_shape
    return full.reshape(r, N_CHIP, cs).transpose(1, 0, 2).reshape(N_CHIP, -1)


def _from_shards(sh, shard_shape, axis):
    r, cs = shard_shape
    if axis == 0:
        return sh.reshape(N_CHIP * r, cs)
    return sh.reshape(N_CHIP, r, cs).transpose(1, 0, 2).reshape(r, N_CHIP * cs)


def _heads(t, n):
    s = t.shape[0]
    return t.reshape(s, n * NH, HD).transpose(1, 0, 2)


def _merge(t):
    return t.transpose(1, 0, 2).reshape(t.shape[1], GW)


def _local_step(x0, mem, tgt, W, gains):
    S = x0.shape[0]
    w_in = jnp.pad(W["w_in"], ((0, 0), (0, IN_PAD - IN_COLS)))
    b_f = jnp.pad(gains["b_forget"], ((0, 0), (0, 128 - NH)))
    cw = W["conv_w"].reshape(3, 2, DFF).transpose(1, 0, 2)
    cb = gains["conv_b"].reshape(2, 1, DFF)

    h1 = _rms_cast(x0, gains["attn_norm_g"], "norm_attn")
    qkv = _heads(_mm_nn(h1, w_in[:, :NQKV], CDT, "proj_qkv"), 6)
    fl = _mm_nn(h1, w_in[:, NQKV:], F32, "proj_gate")
    cum = _gate_fwd(fl, b_f, "gate_cumsum")
    c_hm = cum[:, :NH].T
    cq, ck = c_hm[:, :, None], c_hm[:, None, :]
    fq, fk, fv, sq, sk, sv = (qkv[n * NH:(n + 1) * NH] for n in range(6))
    fo_h, lse = _fox_fwd(fq, fk, fv, cq, ck, "fox_fwd")
    so_h, s_lt = _sb_fwd(sq, sk, sv, "sb_fwd")
    fo, so = _merge(fo_h), _merge(so_h)
    x1, mixed = _out_proj(fo, so, gains["fox_out_g"], gains["sb_out_g"], W["w_out"], x0, "out_proj")

    h2 = _rms_cast(x1, gains["xattn_norm_g"], "norm_xattn")
    mn = _rms_cast(mem, gains["mem_norm_g"], "norm_mem")
    mq = _mm_nn(h2, W["w_mq"], CDT, "proj_mq")
    kv = _mm_nn(mn, W["w_mkv"], CDT, "proj_mkv")
    x2, mo = _xattn_fwd(mq, kv, W["w_mo"], x1, "xattn_fwd")

    h3 = _rms_cast(x2, gains["ffn_norm_g"], "norm_ffn")
    u0 = _mm_nn(h3, W["w_up"], CDT, "ffn_up", halves=True)
    act = _conv_act(u0, cw, cb, "conv_act")
    x3 = _mm_nn(act, W["w_down"], F32, "ffn_down", tm=512, residual=x2)
    loss, dx3, dg_final = _loss_bwd(x3, tgt, gains["final_norm_g"].reshape(1, D), "loss")

    gw, gs = {}, {"final_norm_g": dg_final}
    da = _mm_nt(dx3, W["w_down"], "ffn_down_dx", tn=1408, out_dtype=CDT)
    gw["w_down"] = _mm_tn(act, dx3, "ffn_down_dw", tka=1408)
    du, dwb = _conv_act_bwd(u0, da, cw, cb, "conv_act_bwd")
    gw["conv_w"] = dwb[:, :3].transpose(1, 0, 2).reshape(3, 2 * DFF)
    gs["conv_b"] = dwb[:, 3].reshape(1, 2 * DFF)
    du0 = _conv_bwd_input(du, cw, "conv_bwd_input")
    gw["w_up"] = _mm_tn(h3, du0, "ffn_up_dw", tn=1408, b_halves=True)
    dx2, gs["ffn_norm_g"] = _mm_nt_rmsbwd(du0, W["w_up"], x2, gains["ffn_norm_g"], dx3, "ffn_up_dx", tk=1408, a_halves=True)

    dmo = _mm_nt(dx2, W["w_mo"], "mo_dx", tn=512, out_dtype=CDT)
    gw["w_mo"] = _mm_tn(mo, dx2, "mo_dw")
    dmq, dkv = _xattn_bwd(mq, kv, dmo, "xattn_bwd")
    gw["w_mq"] = _mm_tn(h2, dmq, "mq_dw")
    dx1, gs["xattn_norm_g"] = _mm_nt_rmsbwd(dmq, W["w_mq"], x1, gains["xattn_norm_g"], dx2, "mq_dx")
    gw["w_mkv"] = _mm_tn(mn, dkv, "mkv_dw")
    _, gs["mem_norm_g"] = _mm_nt_rmsbwd(dkv, W["w_mkv"], mem, gains["mem_norm_g"], jnp.zeros_like(mem), "mkv_dx")

    gw["w_out"] = _mm_tn(mixed, dx1, "out_dw")
    dfo, dso, gs["fox_out_g"], gs["sb_out_g"] = _out_proj_bwd(dx1, W["w_out"], fo, so, gains["fox_out_g"], gains["sb_out_g"], "out_dx")
    dfo_h, dso_h = _heads(dfo, 1), _heads(dso, 1)
    dfq, dfk, dfv, dck, dcq = _fox_bwd(fq, fk, fv, cq, ck, fo_h, dfo_h, lse, "fox_bwd")
    dsq, dsk, dsv = _sb_bwd(sq, sk, sv, dso_h, s_lt, "sb_bwd")
    dc = jnp.pad((dck[:, 0, :] + dcq[:, :, 0]).T, ((0, 0), (0, 128 - NH)))
    dfl, db = _gate_bwd(dc, fl, b_f, "gate_bwd")
    gs["b_forget"] = db[:, :NH]
    dqkv = jnp.concatenate([dfq, dfk.astype(CDT), dfv.astype(CDT), dsq, dsk.astype(CDT), dsv.astype(CDT)], axis=0)
    dproj = jnp.concatenate([dqkv.transpose(1, 0, 2).reshape(S, NQKV), dfl.astype(CDT)], axis=1)
    gw["w_in"] = _mm_tn(h1, dproj, "in_dw", tn=640)[:, :IN_COLS]
    dx0, gs["attn_norm_g"] = _mm_nt_rmsbwd(dproj, w_in, x0, gains["attn_norm_g"], dx1, "in_dx", tk=640)
    return loss, dx0, gw, gs


NAMES = ("attn_norm_g", "w_in", "b_forget", "fox_out_g", "sb_out_g", "w_out", "xattn_norm_g", "mem_norm_g", "w_mq",
         "w_mkv", "w_mo", "ffn_norm_g", "w_up", "conv_w", "conv_b", "w_down", "final_norm_g")


def _step(x, mem, loss_target, w, m, v):
    xi, yi, ci = _place()
    big_shapes = [s for _, s, _ in BIG]

    parts = []
    for name, shape, _ in BIG:
        blk = w[name].reshape(shape)
        parts.append(lax.bitcast_convert_type(blk, CDT) if name == "conv_w" else blk.astype(CDT))
    gathered = _gather_weights(_pack_rows(parts, ROWS_G)).reshape(N_CHIP, ROWS_G * 128)
    W, off = {}, 0
    for (name, shape, axis), n in zip(BIG, GATHER_SIZES):
        sh = gathered[:, off:off + n]
        off += n
        if name == "conv_w":
            sh = lax.bitcast_convert_type(sh.reshape(N_CHIP, n // 2, 2), F32)
        W[name] = _from_shards(sh, shape, axis)
    gains = {name: w[name].reshape(1, -1) for name, _ in SMALL}

    loss, grad_x, gw, gs = _local_step(x[0], mem[0], loss_target[0], W, gains)

    g_flat = jnp.concatenate([_to_shards(gw[name], shape, axis) for name, shape, axis in BIG], axis=1)
    g_flat = jnp.pad(g_flat, ((0, 0), (0, ROWS_F * 128 - P_BIG))).reshape(N_CHIP, ROWS_F, 128)
    pair_sum = _add_sibling(g_flat, _swap_halves(g_flat), jnp.reshape(ci, (1,)).astype(jnp.int32), "add_sibling")
    g_half = _add_chips(pair_sum, _scatter_chips(pair_sum), jnp.reshape(2 * xi + yi, (1,)).astype(jnp.int32), "add_chips")
    g_big = _join_halves(g_half)
    small = jnp.concatenate([gs[name].reshape(-1) for name, _ in SMALL] + [loss[0, :1]])
    small = jnp.pad(small, (0, ROWS_S * 128 - P_SMALL)).reshape(ROWS_S, 128)
    small_parts = _gather_small(small).reshape(8, ROWS_S, 128)

    def flat_big(d, prefix):
        return _pack_rows([d[prefix + name].reshape(shape) for name, shape, _ in BIG], ROWS_F)

    def flat_small(d, prefix):
        return _pack_rows([d[prefix + name] for name, _ in SMALL], ROWS_S)

    allw = {**w, **{"m_" + k: a for k, a in m.items()}, **{"v_" + k: a for k, a in v.items()}}
    d_big, m_big, v_big = _adamw(g_big, flat_big(allw, ""), flat_big(allw, "m_"), flat_big(allw, "v_"), "adamw")
    g_sm, d_sm, m_sm, v_sm = _adamw_small(small_parts, flat_small(allw, ""), flat_small(allw, "m_"), flat_small(allw, "v_"), "adamw_small")

    outs = {}
    for prefix, big, sm in (("grad_", g_big, g_sm), ("delta_", d_big, d_sm), ("new_m_", m_big, m_sm), ("new_v_", v_big, v_sm)):
        for (name, _, _), arr in zip(BIG, _unpack(big.reshape(-1), BIG_SIZES, big_shapes)):
            outs[prefix + name] = arr.reshape(w[name].shape)
        for (name, n), arr in zip(SMALL, _unpack(sm.reshape(-1), [n for _, n in SMALL], [(n,) for _, n in SMALL])):
            outs[prefix + name] = arr.reshape(w[name].shape)
    total_loss = g_sm.reshape(-1)[P_SMALL - 1]
    return (total_loss, grad_x[None], *[outs[p + n] for p in ("grad_", "delta_", "new_m_", "new_v_") for n in NAMES])


def kernel(x, mem, attn_norm_g, w_in, b_forget, fox_out_g, sb_out_g, w_out, xattn_norm_g, mem_norm_g, w_mq, w_mkv, w_mo, ffn_norm_g, w_up, conv_w, conv_b, w_down, final_norm_g, loss_target, m_attn_norm_g, m_w_in, m_b_forget, m_fox_out_g, m_sb_out_g, m_w_out, m_xattn_norm_g, m_mem_norm_g, m_w_mq, m_w_mkv, m_w_mo, m_ffn_norm_g, m_w_up, m_conv_w, m_conv_b, m_w_down, m_final_norm_g, v_attn_norm_g, v_w_in, v_b_forget, v_fox_out_g, v_sb_out_g, v_w_out, v_xattn_norm_g, v_mem_norm_g, v_w_mq, v_w_mkv, v_w_mo, v_ffn_norm_g, v_w_up, v_conv_w, v_conv_b, v_w_down, v_final_norm_g):
    given = dict(locals())
    w = {n: given[n] for n in NAMES}
    m = {n: given["m_" + n] for n in NAMES}
    v = {n: given["v_" + n] for n in NAMES}
    return _step(x, mem, loss_target, w, m, v)
```

```python
import functools

import numpy as np
import jax
import jax.numpy as jnp
from jax import lax
from jax.experimental import pallas as pl
from jax.experimental.pallas import tpu as pltpu

F32 = jnp.float32
CDT = jnp.bfloat16
MESH = pl.DeviceIdType.MESH

D = 1024
HD = 64
NH = 8
GW = NH * HD
NQKV = 6 * GW
IN_COLS = NQKV + NH
IN_PAD = NQKV + 128
NMH = 4
MHD = D // NMH
DFF = 2816
EPS = 1e-6
ATT_SCALE = HD ** -0.5
MEM_SCALE = MHD ** -0.5
NEG = -1e30

LR, B1, B2, AEPS, WD, STEP = 0.001, 0.9, 0.999, 1e-08, 0.01, 10
BC1 = 1.0 - B1 ** STEP
BC2 = 1.0 - B2 ** STEP

ATT_TILES = {"fox_fwd": (512, 1024), "fox_bwd": (512, 1024), "sb_fwd": (512, 1024), "sb_bwd": (1024, 1024)}
W_SB = 256
VMEM_LIMIT = 52 * 2 ** 20

N_CHIP = 4
BIG = (("w_in", (D, IN_COLS // N_CHIP), 1), ("w_out", (D // N_CHIP, D), 0), ("w_mq", (D // N_CHIP, D), 0),
       ("w_mkv", (D, 2 * D // N_CHIP), 1), ("w_mo", (D // N_CHIP, D), 0), ("w_up", (D, 2 * DFF // N_CHIP), 1),
       ("conv_w", (3, 2 * DFF // N_CHIP), 1), ("w_down", (DFF // N_CHIP, D), 0))
BIG_SIZES = tuple(int(np.prod(s)) for _, s, _ in BIG)
P_BIG = sum(BIG_SIZES)
ROWS_F = 33 * 1024
assert ROWS_F * 128 >= P_BIG
HALF_F = ROWS_F // 2
ADAM_ROWS = 1536
GATHER_SIZES = tuple(2 * n if name == "conv_w" else n for (name, _, _), n in zip(BIG, BIG_SIZES))
ROWS_G = -(-sum(GATHER_SIZES) // 4096) * 32
HALF_G = ROWS_G // 2
SMALL = (("attn_norm_g", 1024), ("b_forget", 8), ("fox_out_g", 512), ("sb_out_g", 512), ("xattn_norm_g", 1024),
         ("mem_norm_g", 1024), ("ffn_norm_g", 1024), ("conv_b", 2 * DFF), ("final_norm_g", 1024))
P_SMALL = sum(n for _, n in SMALL) + 1
ROWS_S = -(-P_SMALL // 1024) * 8


def _params(sem=None, vmem=VMEM_LIMIT):
    return pltpu.CompilerParams(dimension_semantics=sem, vmem_limit_bytes=vmem)


def _tile(n, pref, mult):
    t = (min(pref, n) // mult) * mult
    while t >= mult:
        if n % t == 0:
            return t
        t -= mult
    return n


def _dot(a, b):
    return jnp.dot(a, b, preferred_element_type=F32)


def _dot_nt(a, b):
    return lax.dot_general(a, b, (((1,), (1,)), ((), ())), preferred_element_type=F32)


def _dot_tn(a, b):
    return lax.dot_general(a, b, (((0,), (0,)), ((), ())), preferred_element_type=F32)


def _split3(x):
    h1 = x.astype(CDT)
    r1 = x - h1.astype(F32)
    h2 = r1.astype(CDT)
    h3 = (r1 - h2.astype(F32)).astype(CDT)
    return h1, h2, h3


def _split2(x):
    h1 = x.astype(CDT)
    return h1, (x - h1.astype(F32)).astype(CDT)


def _rms_bwd(dh, x, g):
    r = lax.rsqrt(jnp.mean(x * x, axis=-1, keepdims=True) + EPS)
    xn = x * r
    dg = jnp.sum(dh * xn, axis=0, keepdims=True)
    dhg = dh * g
    dx = r * (dhg - xn * jnp.mean(dhg * xn, axis=-1, keepdims=True))
    return dx, dg


def _mm_nn(a, b, out_dtype, name, *, tm=1024, tn=512, residual=None, halves=False):
    M, K = a.shape
    N = b.shape[1]
    tm = _tile(M, tm, 16)
    tn = _tile(N // 2 if halves else N, tn, 128)
    nj = N // tn

    def body(*refs):
        a_ref, b_ref = refs[0], refs[1]
        o_ref = refs[-1]
        acc = _dot(a_ref[...].astype(CDT), b_ref[...].astype(CDT))
        if residual is not None:
            acc = acc + refs[2][...]
        o_ref[...] = acc.astype(o_ref.dtype)

    in_specs = [pl.BlockSpec((tm, K), lambda i, j: (i, 0)), pl.BlockSpec((K, tn), lambda i, j: (0, j))]
    ops = [a, b]
    if residual is not None:
        in_specs.append(pl.BlockSpec((tm, tn), lambda i, j: (i, j)))
        ops.append(residual)
    if halves:
        njh = nj // 2
        out_shape = jax.ShapeDtypeStruct((2, M, N // 2), out_dtype)
        out_spec = pl.BlockSpec((None, tm, tn), lambda i, j: (j // njh, i, j % njh))
    else:
        out_shape = jax.ShapeDtypeStruct((M, N), out_dtype)
        out_spec = pl.BlockSpec((tm, tn), lambda i, j: (i, j))
    return pl.pallas_call(body, name=name, grid=(M // tm, nj), in_specs=in_specs, out_specs=out_spec,
                          out_shape=out_shape, compiler_params=_params(("parallel", "parallel")))(*ops)


def _mm_tn(a, b, name, *, tka=512, tn=1024, ts=512, b_halves=False):
    S, Ka = a.shape
    N = 2 * b.shape[2] if b_halves else b.shape[1]
    tka = _tile(Ka, tka, 128)
    tn = _tile(N // 2 if b_halves else N, tn, 128)
    ts = _tile(S, ts, 16)
    nn = N // tn

    def body(a_ref, b_ref, o_ref):
        @pl.when(pl.program_id(2) == 0)
        def _():
            o_ref[...] = jnp.zeros_like(o_ref)
        o_ref[...] += _dot_tn(a_ref[...].astype(CDT), b_ref[...].astype(CDT))

    if b_halves:
        nnh = nn // 2
        b_spec = pl.BlockSpec((None, ts, tn), lambda i, j, s: (j // nnh, s, j % nnh))
    else:
        b_spec = pl.BlockSpec((ts, tn), lambda i, j, s: (s, j))
    return pl.pallas_call(
        body, name=name, grid=(Ka // tka, nn, S // ts),
        in_specs=[pl.BlockSpec((ts, tka), lambda i, j, s: (s, i)), b_spec],
        out_specs=pl.BlockSpec((tka, tn), lambda i, j, s: (i, j)),
        out_shape=jax.ShapeDtypeStruct((Ka, N), F32),
        compiler_params=_params(("parallel", "parallel", "arbitrary")))(a, b)


def _mm_nt(a, b, name, *, tm=512, tn=None, tk=None, a_halves=False, out_dtype=F32,
           epilogue=None, extra=(), extra_specs=(), out_shape=None, out_specs=None):
    if a_halves:
        M, K = a.shape[1], 2 * a.shape[2]
    else:
        M, K = a.shape
    N = b.shape[0]
    tm = _tile(M, tm, 16)
    tn = N if (epilogue is not None or tn is None) else _tile(N, tn, 128)
    tk = K if tk is None else _tile(K // 2 if a_halves else K, tk, 128)
    nk = K // tk
    n_extra = len(extra)

    def body(*refs):
        a_ref, b_ref = refs[0], refs[1]
        extra_refs = refs[2:2 + n_extra]
        out_refs = refs[2 + n_extra:-1]
        acc_ref = refs[-1]
        k = pl.program_id(2)

        @pl.when(k == 0)
        def _():
            acc_ref[...] = jnp.zeros_like(acc_ref)
        acc_ref[...] += _dot_nt(a_ref[...].astype(CDT), b_ref[...].astype(CDT))

        @pl.when(k == nk - 1)
        def _():
            if epilogue is None:
                out_refs[0][...] = acc_ref[...].astype(out_refs[0].dtype)
            else:
                epilogue(acc_ref[...], pl.program_id(0), extra_refs, out_refs)

    if a_halves:
        nkh = nk // 2
        a_spec = pl.BlockSpec((None, tm, tk), lambda i, j, k: (k // nkh, i, k % nkh))
    else:
        a_spec = pl.BlockSpec((tm, tk), lambda i, j, k: (i, k))
    if epilogue is None:
        out_shape = jax.ShapeDtypeStruct((M, N), out_dtype)
        out_specs = pl.BlockSpec((tm, tn), lambda i, j, k: (i, j))
        sem = ("parallel", "parallel", "arbitrary")
    else:
        sem = ("arbitrary", "arbitrary", "arbitrary")
    return pl.pallas_call(
        body, name=name, grid=(M // tm, N // tn, nk),
        in_specs=[a_spec, pl.BlockSpec((tn, tk), lambda i, j, k: (j, k)), *extra_specs],
        out_specs=out_specs, out_shape=out_shape,
        scratch_shapes=[pltpu.VMEM((tm, tn), F32)],
        compiler_params=_params(sem))(a, b, *extra)


def _mm_nt_rmsbwd(a, b, x, g, dres, name, *, tm=512, tk=None, a_halves=False):
    M = x.shape[0]
    tm = _tile(M, tm, 16)

    def epilogue(acc, i, extra_refs, out_refs):
        x_ref, g_ref, r_ref = extra_refs
        dx_ref, dg_ref = out_refs
        dx, dg = _rms_bwd(acc, x_ref[...], g_ref[...])
        dx_ref[...] = r_ref[...] + dx

        @pl.when(i == 0)
        def _():
            dg_ref[...] = jnp.zeros_like(dg_ref)
        dg_ref[...] += dg

    row = pl.BlockSpec((tm, D), lambda i, j, k: (i, 0))
    vec = pl.BlockSpec((1, D), lambda i, j, k: (0, 0))
    return _mm_nt(a, b, name, tm=tm, tk=tk, a_halves=a_halves, epilogue=epilogue,
                  extra=(x, g, dres), extra_specs=(row, vec, row),
                  out_shape=(jax.ShapeDtypeStruct((M, D), F32), jax.ShapeDtypeStruct((1, D), F32)),
                  out_specs=(row, vec))


def _rms_cast(x, g, name, *, tm=512):
    M, W = x.shape
    tm = _tile(M, tm, 16)

    def body(x_ref, g_ref, o_ref):
        xf = x_ref[...]
        r = lax.rsqrt(jnp.mean(xf * xf, axis=-1, keepdims=True) + EPS)
        o_ref[...] = (xf * r * g_ref[...]).astype(o_ref.dtype)

    return pl.pallas_call(body, name=name, grid=(M // tm,),
                          in_specs=[pl.BlockSpec((tm, W), lambda i: (i, 0)), pl.BlockSpec((1, W), lambda i: (0, 0))],
                          out_specs=pl.BlockSpec((tm, W), lambda i: (i, 0)),
                          out_shape=jax.ShapeDtypeStruct((M, W), CDT),
                          compiler_params=_params(("parallel",)))(x, g)


def _tri(n, lower):
    r = lax.broadcasted_iota(jnp.int32, (n, n), 0)
    c = lax.broadcasted_iota(jnp.int32, (n, n), 1)
    return (c <= r if lower else c >= r).astype(CDT)


def _gate_fwd(fl, b, name, *, tm=512):
    S = fl.shape[0]
    tm = _tile(S, tm, 16)

    def body(f_ref, b_ref, c_ref, carry):
        @pl.when(pl.program_id(0) == 0)
        def _():
            carry[...] = jnp.zeros_like(carry)
        z = f_ref[...] + b_ref[...]
        lf = jnp.minimum(z, 0.0) - jnp.log(1.0 + jnp.exp(-jnp.abs(z)))
        tri = _tri(tm, True)
        cum = sum(_dot(tri, p) for p in _split3(lf)) + carry[...]
        c_ref[...] = cum
        carry[...] = cum[tm - 1:tm, :]

    return pl.pallas_call(body, name=name, grid=(S // tm,),
                          in_specs=[pl.BlockSpec((tm, 128), lambda i: (i, 0)), pl.BlockSpec((1, 128), lambda i: (0, 0))],
                          out_specs=pl.BlockSpec((tm, 128), lambda i: (i, 0)),
                          out_shape=jax.ShapeDtypeStruct((S, 128), F32),
                          scratch_shapes=[pltpu.VMEM((1, 128), F32)],
                          compiler_params=_params(("arbitrary",)))(fl, b)


def _gate_bwd(dc, fl, b, name, *, tm=512):
    S = fl.shape[0]
    tm = _tile(S, tm, 16)
    nb = S // tm

    def body(dc_ref, f_ref, b_ref, df_ref, db_ref, carry):
        @pl.when(pl.program_id(0) == 0)
        def _():
            carry[...] = jnp.zeros_like(carry)
            db_ref[...] = jnp.zeros_like(db_ref)
        tri = _tri(tm, False)
        suf = sum(_dot(tri, p) for p in _split3(dc_ref[...])) + carry[...]
        carry[...] = suf[0:1, :]
        df = suf * jax.nn.sigmoid(-(f_ref[...] + b_ref[...]))
        df_ref[...] = df
        db_ref[...] += jnp.sum(df, axis=0, keepdims=True)

    rev = pl.BlockSpec((tm, 128), lambda i: (nb - 1 - i, 0))
    vec = pl.BlockSpec((1, 128), lambda i: (0, 0))
    return pl.pallas_call(body, name=name, grid=(nb,), in_specs=[rev, rev, vec], out_specs=(rev, vec),
                          out_shape=(jax.ShapeDtypeStruct((S, 128), F32), jax.ShapeDtypeStruct((1, 128), F32)),
                          scratch_shapes=[pltpu.VMEM((1, 128), F32)],
                          compiler_params=_params(("arbitrary",)))(dc, fl, b)


def _out_proj(fo, so, gf, gs, w_out, x0, name, *, tm=512):
    S = fo.shape[0]
    tm = _tile(S, tm, 16)

    def body(fo_ref, so_ref, gf_ref, gs_ref, w_ref, x_ref, x1_ref, mx_ref):
        for ref, g_ref, lo in ((fo_ref, gf_ref, 0), (so_ref, gs_ref, GW)):
            o = ref[...]
            r = lax.rsqrt(jnp.mean(o * o, axis=-1, keepdims=True) + EPS)
            mx_ref[:, lo:lo + GW] = (o * r * g_ref[...]).astype(CDT)
        x1_ref[...] = x_ref[...] + _dot(mx_ref[...], w_ref[...])

    half = pl.BlockSpec((tm, GW), lambda i: (i, 0))
    gvec = pl.BlockSpec((1, GW), lambda i: (0, 0))
    row = pl.BlockSpec((tm, D), lambda i: (i, 0))
    return pl.pallas_call(body, name=name, grid=(S // tm,),
                          in_specs=[half, half, gvec, gvec, pl.BlockSpec((D, D), lambda i: (0, 0)), row],
                          out_specs=(row, row),
                          out_shape=(jax.ShapeDtypeStruct((S, D), F32), jax.ShapeDtypeStruct((S, D), CDT)),
                          compiler_params=_params(("parallel",)))(fo, so, gf, gs, w_out, x0)


def _out_proj_bwd(dx1, w_out, fo, so, gf, gs, name, *, tm=512):
    S = fo.shape[0]
    tm = _tile(S, tm, 16)

    def epilogue(acc, i, extra_refs, out_refs):
        fo_ref, so_ref, gf_ref, gs_ref = extra_refs
        dfo_ref, dso_ref, dgf_ref, dgs_ref = out_refs

        @pl.when(i == 0)
        def _():
            dgf_ref[...] = jnp.zeros_like(dgf_ref)
            dgs_ref[...] = jnp.zeros_like(dgs_ref)
        for lo, o_ref, g_ref, do_ref, dg_ref in ((0, fo_ref, gf_ref, dfo_ref, dgf_ref), (GW, so_ref, gs_ref, dso_ref, dgs_ref)):
            dx, dg = _rms_bwd(acc[:, lo:lo + GW], o_ref[...], g_ref[...])
            do_ref[...] = dx.astype(do_ref.dtype)
            dg_ref[...] += dg

    half = pl.BlockSpec((tm, GW), lambda i, j, k: (i, 0))
    gvec = pl.BlockSpec((1, GW), lambda i, j, k: (0, 0))
    return _mm_nt(dx1, w_out, name, tm=tm, epilogue=epilogue, extra=(fo, so, gf, gs),
                  extra_specs=(half, half, gvec, gvec),
                  out_shape=(jax.ShapeDtypeStruct((S, GW), CDT), jax.ShapeDtypeStruct((S, GW), CDT),
                             jax.ShapeDtypeStruct((1, GW), F32), jax.ShapeDtypeStruct((1, GW), F32)),
                  out_specs=(half, half, gvec, gvec))


def _loss_bwd(x3, tgt, g, name, *, tm=512):
    S = x3.shape[0]
    tm = _tile(S, tm, 16)

    def body(x_ref, t_ref, g_ref, dx_ref, loss_ref, dg_ref):
        @pl.when(pl.program_id(0) == 0)
        def _():
            loss_ref[...] = jnp.zeros_like(loss_ref)
            dg_ref[...] = jnp.zeros_like(dg_ref)
        x = x_ref[...]
        gv = g_ref[...]
        r = lax.rsqrt(jnp.mean(x * x, axis=-1, keepdims=True) + EPS)
        xn = x * r
        err = xn * gv - t_ref[...]
        loss_ref[...] += jnp.full(loss_ref.shape, 0.5 * jnp.sum(jnp.mean(err * err, axis=-1, keepdims=True)), F32)
        dy = err * (1.0 / D)
        dg_ref[...] += jnp.sum(dy * xn, axis=0, keepdims=True)
        dyg = dy * gv
        dx_ref[...] = r * (dyg - xn * jnp.mean(dyg * xn, axis=-1, keepdims=True))

    row = pl.BlockSpec((tm, D), lambda i: (i, 0))
    vec = pl.BlockSpec((1, D), lambda i: (0, 0))
    dx3, loss, dg = pl.pallas_call(
        body, name=name, grid=(S // tm,), in_specs=[row, row, vec],
        out_specs=(row, pl.BlockSpec((1, 128), lambda i: (0, 0)), vec),
        out_shape=(jax.ShapeDtypeStruct((S, D), F32), jax.ShapeDtypeStruct((1, 128), F32), jax.ShapeDtypeStruct((1, D), F32)),
        compiler_params=_params(("arbitrary",)))(x3, tgt, g)
    return loss, dx3, dg


MASKED, FIRST, LAST = 1, 2, 4


def _att_tiles(name, S):
    tq, tk = ATT_TILES[name]
    return min(tq, S), min(tk, S)


def _pairs(S, tq, tk, descending=True):
    assert tk % tq == 0 and S % tk == 0
    qi, kj, fl = [], [], []
    for i in range(S // tq):
        last = ((i + 1) * tq - 1) // tk
        order = list(range(last, -1, -1) if descending else range(last + 1))
        for pos, kb in enumerate(order):
            qi.append(i)
            kj.append(kb)
            fl.append((MASKED if (kb + 1) * tk - 1 > i * tq else 0) | (FIRST if pos == 0 else 0) | (LAST if pos == last else 0))
    return tuple(jnp.asarray(np.asarray(a, np.int32)) for a in (qi, kj, fl))


def _att_specs(tq, tk):
    qblk = pl.BlockSpec((1, tq, HD), lambda h, n, qi, kj, fl: (h, qi[n], 0))
    kblk = pl.BlockSpec((1, tk, HD), lambda h, n, qi, kj, fl: (h, kj[n], 0))
    qcol = pl.BlockSpec((1, tq, 1), lambda h, n, qi, kj, fl: (h, qi[n], 0))
    krow = pl.BlockSpec((1, 1, tk), lambda h, n, qi, kj, fl: (h, 0, kj[n]))
    return qblk, kblk, qcol, krow


def _causal(tq, w, ahead, strict):
    diff = lax.broadcasted_iota(jnp.int32, (tq, w), 1) - lax.broadcasted_iota(jnp.int32, (tq, w), 0)
    return diff < ahead if strict else diff <= ahead


def _masked_or_not(flags, step):
    pl.when(flags % 2 == 1)(functools.partial(step, True))
    pl.when(flags % 2 == 0)(functools.partial(step, False))


def _fox_fwd(q, k, v, cq, ck, name):
    S = q.shape[1]
    tq, tk = _att_tiles("fox_fwd", S)
    qi, kj, fl = _pairs(S, tq, tk)
    qblk, kblk, qcol, krow = _att_specs(tq, tk)

    def body(qi_ref, kj_ref, fl_ref, q_ref, k_ref, v_ref, cq_ref, ck_ref, o_ref, lse_ref, m_s, l_s, acc_s):
        n = pl.program_id(1)
        i, kb, flags = qi_ref[n], kj_ref[n], fl_ref[n]

        @pl.when(flags & FIRST != 0)
        def _():
            m_s[...] = jnp.full_like(m_s, NEG)
            l_s[...] = jnp.zeros_like(l_s)
            acc_s[...] = jnp.zeros_like(acc_s)

        def step(masked):
            s = _dot_nt(q_ref[0] * ATT_SCALE, k_ref[0]) + cq_ref[0] - ck_ref[0]
            if masked:
                s = jnp.where(_causal(tq, tk, i * tq - kb * tk, False), s, NEG)
            m_new = jnp.maximum(m_s[...], jnp.max(s, axis=-1, keepdims=True))
            alpha = jnp.exp(m_s[...] - m_new)
            p = jnp.exp(s - m_new)
            l_s[...] = alpha * l_s[...] + jnp.sum(p, axis=-1, keepdims=True)
            acc_s[...] = alpha * acc_s[...] + _dot(p.astype(CDT), v_ref[0])
            m_s[...] = m_new

        _masked_or_not(flags, step)

        @pl.when(flags & LAST != 0)
        def _():
            o_ref[0] = acc_s[...] / l_s[...]
            lse_ref[0] = m_s[...] + jnp.log(l_s[...])

    grid_spec = pltpu.PrefetchScalarGridSpec(
        num_scalar_prefetch=3, grid=(NH, int(qi.shape[0])),
        in_specs=[qblk, kblk, kblk, qcol, krow], out_specs=(qblk, qcol),
        scratch_shapes=[pltpu.VMEM((tq, 1), F32), pltpu.VMEM((tq, 1), F32), pltpu.VMEM((tq, HD), F32)])
    return pl.pallas_call(body, name=name, grid_spec=grid_spec,
                          out_shape=(jax.ShapeDtypeStruct((NH, S, HD), F32), jax.ShapeDtypeStruct((NH, S, 1), F32)),
                          compiler_params=_params(("parallel", "arbitrary")))(qi, kj, fl, q, k, v, cq, ck)


def _fox_bwd(q, k, v, cq, ck, o, do, lse, name):
    S = q.shape[1]
    tq, tk = _att_tiles("fox_bwd", S)
    qi, kj, fl = _pairs(S, tq, tk)
    qblk, kblk, qcol, krow = _att_specs(tq, tk)

    def body(qi_ref, kj_ref, fl_ref, q_ref, k_ref, v_ref, cq_ref, ck_ref, o_ref, do_ref, lse_ref,
             dq_ref, dk_ref, dv_ref, dck_ref, dcq_ref, dq_s, dl_s, dcq_s):
        n = pl.program_id(1)
        i, kb, flags = qi_ref[n], kj_ref[n], fl_ref[n]

        @pl.when(n == 0)
        def _():
            dk_ref[...] = jnp.zeros_like(dk_ref)
            dv_ref[...] = jnp.zeros_like(dv_ref)
            dck_ref[...] = jnp.zeros_like(dck_ref)

        @pl.when(flags & FIRST != 0)
        def _():
            dq_s[...] = jnp.zeros_like(dq_s)
            dcq_s[...] = jnp.zeros_like(dcq_s)
            dl_s[...] = jnp.sum(do_ref[0].astype(F32) * o_ref[0], axis=-1, keepdims=True)

        def step(masked):
            qs = q_ref[0] * ATT_SCALE
            do = do_ref[0]
            p = jnp.exp(_dot_nt(qs, k_ref[0]) + cq_ref[0] - ck_ref[0] - lse_ref[0])
            if masked:
                p = jnp.where(_causal(tq, tk, i * tq - kb * tk, False), p, 0.0)
            ds = p * (_dot_nt(do, v_ref[0]) - dl_s[...])
            dsb = ds.astype(CDT)
            dq_s[...] += _dot(dsb, k_ref[0])
            rows = pl.ds(pl.multiple_of(kb * tk, tk), tk)
            dk_ref[0, rows, :] += _dot_tn(dsb, qs)
            dv_ref[0, rows, :] += _dot_tn(p.astype(CDT), do)
            dck_ref[0, :, rows] += -jnp.sum(ds, axis=0, keepdims=True)
            dcq_s[...] += jnp.sum(ds, axis=-1, keepdims=True)

        _masked_or_not(flags, step)

        @pl.when(flags & LAST != 0)
        def _():
            dq_ref[0] = (dq_s[...] * ATT_SCALE).astype(dq_ref.dtype)
            dcq_ref[0] = dcq_s[...]

    whole = pl.BlockSpec((1, S, HD), lambda h, n, qi, kj, fl: (h, 0, 0))
    grid_spec = pltpu.PrefetchScalarGridSpec(
        num_scalar_prefetch=3, grid=(NH, int(qi.shape[0])),
        in_specs=[qblk, kblk, kblk, qcol, krow, qblk, qblk, qcol],
        out_specs=(qblk, whole, whole, pl.BlockSpec((1, 1, S), lambda h, n, qi, kj, fl: (h, 0, 0)), qcol),
        scratch_shapes=[pltpu.VMEM((tq, HD), F32), pltpu.VMEM((tq, 1), F32), pltpu.VMEM((tq, 1), F32)])
    return pl.pallas_call(body, name=name, grid_spec=grid_spec,
                          out_shape=(jax.ShapeDtypeStruct((NH, S, HD), CDT), jax.ShapeDtypeStruct((NH, S, HD), F32),
                                     jax.ShapeDtypeStruct((NH, S, HD), F32), jax.ShapeDtypeStruct((NH, 1, S), F32),
                                     jax.ShapeDtypeStruct((NH, S, 1), F32)),
                          compiler_params=_params(("parallel", "arbitrary")))(qi, kj, fl, q, k, v, cq, ck, o, do, lse)


LOG2E = 1.4426950408889634


def _sb_softplus2(qs, ksub, mask):
    z2 = _dot_nt(qs, ksub) * LOG2E
    sp2 = jnp.maximum(z2, 0.0) + jnp.log2(1.0 + jnp.exp2(-jnp.abs(z2)))
    return z2, sp2 if mask is None else jnp.where(mask, sp2, 0.0)


def _strict_tri(n, upper, value):
    r = lax.broadcasted_iota(jnp.int32, (n, n), 0)
    c = lax.broadcasted_iota(jnp.int32, (n, n), 1)
    return jnp.where(r < c if upper else r > c, value, 0.0).astype(CDT)


def _sb_fwd(q, k, v, name):
    S = q.shape[1]
    tq, tk = _att_tiles("sb_fwd", S)
    W = min(W_SB, tk)
    qi, kj, fl = _pairs(S, tq, tk)
    qblk, kblk, qcol, _ = _att_specs(tq, tk)

    def body(qi_ref, kj_ref, fl_ref, q_ref, k_ref, v_ref, o_ref, lt_ref, run_s, acc_s):
        n = pl.program_id(1)
        i, kb, flags = qi_ref[n], kj_ref[n], fl_ref[n]

        @pl.when(flags & FIRST != 0)
        def _():
            run_s[...] = jnp.zeros_like(run_s)
            acc_s[...] = jnp.zeros_like(acc_s)

        def step(masked):
            qs = q_ref[0] * ATT_SCALE
            neg_later = _strict_tri(W, False, -1.0)
            run = run_s[...]
            acc = acc_s[...]
            for sub in range(tk // W - 1, -1, -1):
                cols = slice(sub * W, (sub + 1) * W)
                mask = _causal(tq, W, i * tq - kb * tk - sub * W, True) if masked else None
                z2, sp2 = _sb_softplus2(qs, k_ref[0, cols, :], mask)
                excl = _dot(sp2.astype(CDT), neg_later)
                a = jnp.exp2((z2 - sp2) + (excl + run))
                if masked:
                    a = jnp.where(mask, a, 0.0)
                acc = acc + _dot(a.astype(CDT), v_ref[0, cols, :])
                run = run + (excl[:, 0:1] - sp2[:, 0:1])
            run_s[...] = run
            acc_s[...] = acc

        _masked_or_not(flags, step)

        @pl.when(flags & LAST != 0)
        def _():
            o_ref[0] = acc_s[...]
            lt_ref[0] = run_s[...]

    grid_spec = pltpu.PrefetchScalarGridSpec(
        num_scalar_prefetch=3, grid=(NH, int(qi.shape[0])), in_specs=[qblk, kblk, kblk], out_specs=(qblk, qcol),
        scratch_shapes=[pltpu.VMEM((tq, 1), F32), pltpu.VMEM((tq, HD), F32)])
    return pl.pallas_call(body, name=name, grid_spec=grid_spec,
                          out_shape=(jax.ShapeDtypeStruct((NH, S, HD), F32), jax.ShapeDtypeStruct((NH, S, 1), F32)),
                          compiler_params=_params(("parallel", "arbitrary")))(qi, kj, fl, q, k, v)


def _sb_bwd(q, k, v, do, lt, name):
    S = q.shape[1]
    tq, tk = _att_tiles("sb_bwd", S)
    W = min(W_SB, tk)
    qi, kj, fl = _pairs(S, tq, tk, descending=False)
    qblk, kblk, qcol, _ = _att_specs(tq, tk)

    def body(qi_ref, kj_ref, fl_ref, q_ref, k_ref, v_ref, do_ref, lt_ref, dq_ref, dk_ref, dv_ref, passed_s, gsum_s, dq_s):
        n = pl.program_id(1)
        i, kb, flags = qi_ref[n], kj_ref[n], fl_ref[n]

        @pl.when(n == 0)
        def _():
            dk_ref[...] = jnp.zeros_like(dk_ref)
            dv_ref[...] = jnp.zeros_like(dv_ref)

        @pl.when(flags & FIRST != 0)
        def _():
            passed_s[...] = jnp.zeros_like(passed_s)
            gsum_s[...] = jnp.zeros_like(gsum_s)
            dq_s[...] = jnp.zeros_like(dq_s)

        def step(masked):
            qs = q_ref[0] * ATT_SCALE
            do = do_ref[0]
            neg_later = _strict_tri(W, False, -1.0)
            earlier = _strict_tri(W, True, 1.0)
            for sub in range(tk // W):
                cols = slice(sub * W, (sub + 1) * W)
                mask = _causal(tq, W, i * tq - kb * tk - sub * W, True) if masked else None
                ksub = k_ref[0, cols, :]
                z2, sp2 = _sb_softplus2(qs, ksub, mask)
                excl = _dot(sp2.astype(CDT), neg_later)
                through = passed_s[...] + (excl[:, 0:1] - sp2[:, 0:1])
                t1 = z2 - sp2
                sig = jnp.exp2(t1)
                a = jnp.exp2(t1 + (excl + (lt_ref[0] - through)))
                if masked:
                    a = jnp.where(mask, a, 0.0)
                dl = _dot_nt(do, v_ref[0, cols, :]) * a
                before = _dot(dl.astype(CDT), earlier)
                dz = dl - sig * (dl + (before + gsum_s[...]))
                if masked:
                    dz = jnp.where(mask, dz, 0.0)
                dzb = dz.astype(CDT)
                dq_s[...] += _dot(dzb, ksub)
                rows = pl.ds(pl.multiple_of(kb * tk + sub * W, W), W)
                dk_ref[0, rows, :] += _dot_tn(dzb, qs)
                dv_ref[0, rows, :] += _dot_tn(a.astype(CDT), do)
                passed_s[...] = through
                gsum_s[...] += before[:, W - 1:W] + dl[:, W - 1:W]

        _masked_or_not(flags, step)

        @pl.when(flags & LAST != 0)
        def _():
            dq_ref[0] = (dq_s[...] * ATT_SCALE).astype(dq_ref.dtype)

    whole = pl.BlockSpec((1, S, HD), lambda h, n, qi, kj, fl: (h, 0, 0))
    grid_spec = pltpu.PrefetchScalarGridSpec(
        num_scalar_prefetch=3, grid=(NH, int(qi.shape[0])), in_specs=[qblk, kblk, kblk, qblk, qcol],
        out_specs=(qblk, whole, whole),
        scratch_shapes=[pltpu.VMEM((tq, 1), F32), pltpu.VMEM((tq, 1), F32), pltpu.VMEM((tq, HD), F32)])
    return pl.pallas_call(body, name=name, grid_spec=grid_spec,
                          out_shape=(jax.ShapeDtypeStruct((NH, S, HD), CDT), jax.ShapeDtypeStruct((NH, S, HD), F32),
                                     jax.ShapeDtypeStruct((NH, S, HD), F32)),
                          compiler_params=_params(("parallel", "arbitrary")))(qi, kj, fl, q, k, v, do, lt)


def _mem_probs(q_ref, kv_ref, h):
    cols = slice(h * MHD, (h + 1) * MHD)
    s = _dot_nt(q_ref[:, cols], kv_ref[:, cols]) * MEM_SCALE
    e = jnp.exp(s - jnp.max(s, axis=-1, keepdims=True))
    return e / jnp.sum(e, axis=-1, keepdims=True)


def _xattn_fwd(q, kv, w_mo, x1, name, *, tm=512):
    S = q.shape[0]
    tm = _tile(S, tm, 16)
    nm = kv.shape[0]

    def body(q_ref, kv_ref, w_ref, x_ref, x2_ref, o_ref):
        for h in range(NMH):
            p = _mem_probs(q_ref, kv_ref, h)
            o_ref[:, h * MHD:(h + 1) * MHD] = _dot(p.astype(CDT), kv_ref[:, D + h * MHD:D + (h + 1) * MHD]).astype(CDT)
        x2_ref[...] = x_ref[...] + _dot(o_ref[...], w_ref[...])

    row = pl.BlockSpec((tm, D), lambda i: (i, 0))
    return pl.pallas_call(body, name=name, grid=(S // tm,),
                          in_specs=[row, pl.BlockSpec((nm, 2 * D), lambda i: (0, 0)), pl.BlockSpec((D, D), lambda i: (0, 0)), row],
                          out_specs=(row, row),
                          out_shape=(jax.ShapeDtypeStruct((S, D), F32), jax.ShapeDtypeStruct((S, D), CDT)),
                          compiler_params=_params(("parallel",)))(q, kv, w_mo, x1)


def _xattn_bwd(q, kv, do, name, *, tm=512):
    S = q.shape[0]
    tm = _tile(S, tm, 16)
    nm = kv.shape[0]

    def body(q_ref, kv_ref, do_ref, dq_ref, dkv_ref):
        @pl.when(pl.program_id(0) == 0)
        def _():
            dkv_ref[...] = jnp.zeros_like(dkv_ref)
        for h in range(NMH):
            cols = slice(h * MHD, (h + 1) * MHD)
            vcols = slice(D + h * MHD, D + (h + 1) * MHD)
            p = _mem_probs(q_ref, kv_ref, h)
            doh = do_ref[:, cols]
            dp = _dot_nt(doh, kv_ref[:, vcols])
            ds = (p * (dp - jnp.sum(p * dp, axis=-1, keepdims=True)) * MEM_SCALE).astype(CDT)
            dq_ref[:, cols] = _dot(ds, kv_ref[:, cols]).astype(CDT)
            dkv_ref[:, cols] += _dot_tn(ds, q_ref[:, cols])
            dkv_ref[:, vcols] += _dot_tn(p.astype(CDT), doh)

    row = pl.BlockSpec((tm, D), lambda i: (i, 0))
    kvs = pl.BlockSpec((nm, 2 * D), lambda i: (0, 0))
    return pl.pallas_call(body, name=name, grid=(S // tm,), in_specs=[row, kvs, row], out_specs=(row, kvs),
                          out_shape=(jax.ShapeDtypeStruct((S, D), CDT), jax.ShapeDtypeStruct((nm, 2 * D), F32)),
                          compiler_params=_params(("arbitrary",)))(q, kv, do)


HALO = 16


def _shift_down(u, prev, s):
    rolled = pltpu.roll(u, s, 0)
    r = lax.broadcasted_iota(jnp.int32, u.shape, 0)
    for t in range(s):
        rolled = jnp.where(r == t, prev[HALO - s + t:HALO - s + t + 1, :], rolled)
    return rolled


def _shift_up(u, nxt, s):
    n = u.shape[0]
    rolled = pltpu.roll(u, n - s, 0)
    r = lax.broadcasted_iota(jnp.int32, u.shape, 0)
    for t in range(s):
        rolled = jnp.where(r == n - s + t, nxt[t:t + 1, :], rolled)
    return rolled


def _conv_taps(u_ref, h_ref, first):
    u = u_ref[...].astype(F32)
    prev = jnp.where(first, 0.0, h_ref[...].astype(F32))
    out = []
    for half in range(2):
        out.append((u[half], _shift_down(u[half], prev[half], 1), _shift_down(u[half], prev[half], 2)))
    return out


def _conv_specs(tm, tn, nsb):
    blk = pl.BlockSpec((2, tm, tn), lambda j, i: (0, i, j))
    prev = pl.BlockSpec((2, HALO, tn), lambda j, i: (0, jnp.maximum(i * (tm // HALO) - 1, 0), j))
    nxt = pl.BlockSpec((2, HALO, tn), lambda j, i: (0, jnp.minimum((i + 1) * (tm // HALO), nsb - 1), j))
    w = pl.BlockSpec((2, 3, tn), lambda j, i: (0, 0, j))
    b = pl.BlockSpec((2, 1, tn), lambda j, i: (0, 0, j))
    return blk, prev, nxt, w, b


def _conv_apply(taps, w_ref, b_ref):
    ys = []
    for half in range(2):
        u, u1, u2 = taps[half]
        w = w_ref[half]
        ys.append(b_ref[half] + u2 * w[0:1, :] + u1 * w[1:2, :] + u * w[2:3, :])
    return ys


def _conv_act(u0, cw, cb, name, *, tm=512, tn=256):
    _, S, F = u0.shape
    tm = _tile(S, tm, HALO)
    tn = _tile(F, tn, 128)
    blk, prev, _, w, b = _conv_specs(tm, tn, S // HALO)

    def body(u_ref, h_ref, w_ref, b_ref, a_ref):
        yg, yv = _conv_apply(_conv_taps(u_ref, h_ref, pl.program_id(1) == 0), w_ref, b_ref)
        a_ref[...] = (yg * jax.nn.sigmoid(yg) * yv).astype(a_ref.dtype)

    return pl.pallas_call(body, name=name, grid=(F // tn, S // tm), in_specs=[blk, prev, w, b],
                          out_specs=pl.BlockSpec((tm, tn), lambda j, i: (i, j)),
                          out_shape=jax.ShapeDtypeStruct((S, F), CDT),
                          compiler_params=_params(("parallel", "parallel")))(u0, u0, cw, cb)


def _conv_act_bwd(u0, da, cw, cb, name, *, tm=512, tn=256):
    _, S, F = u0.shape
    tm = _tile(S, tm, HALO)
    tn = _tile(F, tn, 128)
    blk, prev, _, w, b = _conv_specs(tm, tn, S // HALO)

    def body(u_ref, h_ref, da_ref, w_ref, b_ref, du_ref, dwb_ref):
        @pl.when(pl.program_id(1) == 0)
        def _():
            dwb_ref[...] = jnp.zeros_like(dwb_ref)
        taps = _conv_taps(u_ref, h_ref, pl.program_id(1) == 0)
        yg, yv = _conv_apply(taps, w_ref, b_ref)
        sg = jax.nn.sigmoid(yg)
        da = da_ref[...].astype(F32)
        dus = (da * yv * sg * (1.0 + yg * (1.0 - sg)), da * yg * sg)
        for half in range(2):
            du = dus[half]
            du_ref[half] = du.astype(du_ref.dtype)
            u, u1, u2 = taps[half]
            for row, term in enumerate((du * u2, du * u1, du * u, du)):
                dwb_ref[half, row:row + 1, :] += jnp.sum(term, axis=0, keepdims=True)

    return pl.pallas_call(body, name=name, grid=(F // tn, S // tm),
                          in_specs=[blk, prev, pl.BlockSpec((tm, tn), lambda j, i: (i, j)), w, b],
                          out_specs=(blk, pl.BlockSpec((2, 4, tn), lambda j, i: (0, 0, j))),
                          out_shape=(jax.ShapeDtypeStruct((2, S, F), CDT), jax.ShapeDtypeStruct((2, 4, F), F32)),
                          compiler_params=_params(("parallel", "arbitrary")))(u0, u0, da, cw, cb)


def _conv_bwd_input(du, cw, name, *, tm=512, tn=256):
    _, S, F = du.shape
    tm = _tile(S, tm, HALO)
    tn = _tile(F, tn, 128)
    blk, _, nxt, w, _ = _conv_specs(tm, tn, S // HALO)
    ni = S // tm

    def body(d_ref, h_ref, w_ref, o_ref):
        d = d_ref[...].astype(F32)
        nx = jnp.where(pl.program_id(1) == ni - 1, 0.0, h_ref[...].astype(F32))
        for half in range(2):
            wv = w_ref[half]
            y = d[half] * wv[2:3, :] + _shift_up(d[half], nx[half], 1) * wv[1:2, :] + _shift_up(d[half], nx[half], 2) * wv[0:1, :]
            o_ref[half] = y.astype(o_ref.dtype)

    return pl.pallas_call(body, name=name, grid=(F // tn, ni), in_specs=[blk, nxt, w], out_specs=blk,
                          out_shape=jax.ShapeDtypeStruct((2, S, F), CDT),
                          compiler_params=_params(("parallel", "parallel")))(du, du, cw)


ANY = pl.BlockSpec(memory_space=pl.ANY)


def _place():
    return lax.axis_index("x"), lax.axis_index("y"), lax.axis_index("c")


def _other_chips(x, y):
    return ((1 - x, y), (x, 1 - y), (1 - x, 1 - y))


def _gather_weights(wsh):
    rows, half = wsh.shape[0], wsh.shape[0] // 2

    def body(w_ref, out_ref, send_sems, recv_sems, local_sem):
        x, y, c = _place()
        chips = _other_chips(x, y)

        def part(chip, pc):
            return out_ref.at[2 * chip[0] + chip[1], pl.ds(pl.multiple_of(pc * half, 16), half), :]

        def copy(k, chip, pc, to, src=None):
            return pltpu.make_async_remote_copy(src_ref=part(chip, pc) if src is None else src, dst_ref=part(chip, pc),
                                                send_sem=send_sems.at[k], recv_sem=recv_sems.at[k],
                                                device_id=to, device_id_type=MESH)

        mine = pltpu.make_async_copy(w_ref, out_ref.at[2 * x + y], local_sem)
        mine.start()
        my_half = w_ref.at[pl.ds(pl.multiple_of(c * half, 16), half), :]
        first = [copy(j, (x, y), c, (*chip, c), src=my_half) for j, chip in enumerate(chips)]
        for cp in first:
            cp.start()
        passed = [copy(3 + j, chip, c, (x, y, 1 - c)) for j, chip in enumerate(chips)]
        for j, chip in enumerate(chips):
            copy(j, chip, c, (x, y, c)).wait_recv()
            passed[j].start()
        for j, chip in enumerate(chips):
            copy(3 + j, chip, 1 - c, (x, y, c)).wait_recv()
        for cp in first + passed:
            cp.wait_send()
        mine.wait()

    return pl.pallas_call(body, name="gather_weights", in_specs=[ANY], out_specs=ANY,
                          out_shape=jax.ShapeDtypeStruct((N_CHIP, rows, 128), wsh.dtype),
                          scratch_shapes=[pltpu.SemaphoreType.DMA((6,)), pltpu.SemaphoreType.DMA((6,)), pltpu.SemaphoreType.DMA])(wsh)


def _gather_small(v):
    m = v.shape[0]

    def body(v_ref, out_ref, send_sems, recv_sems, local_sem):
        x, y, c = _place()
        me, sibling = (x, y, c), (x, y, 1 - c)
        chips = _other_chips(x, y)

        def rows(px, py, pc):
            return out_ref.at[pl.ds((4 * px + 2 * py + pc) * m, m), :]

        def copy(k, block, to, src=None):
            return pltpu.make_async_remote_copy(src_ref=rows(*block) if src is None else src, dst_ref=rows(*block),
                                                send_sem=send_sems.at[k], recv_sem=recv_sems.at[k],
                                                device_id=to, device_id_type=MESH)

        mine = pltpu.make_async_copy(v_ref, rows(*me), local_sem)
        mine.start()
        first = [copy(0, me, sibling, src=v_ref)]
        first += [copy(1 + j, me, (*chip, c), src=v_ref) for j, chip in enumerate(chips)]
        for cp in first:
            cp.start()
        passed = [copy(4 + j, (*chip, c), sibling) for j, chip in enumerate(chips)]
        for j, chip in enumerate(chips):
            copy(1 + j, (*chip, c), me).wait_recv()
            passed[j].start()
        copy(0, sibling, me).wait_recv()
        for j, chip in enumerate(chips):
            copy(4 + j, (*chip, 1 - c), me).wait_recv()
        for cp in first + passed:
            cp.wait_send()
        mine.wait()

    vm = pl.BlockSpec(memory_space=pltpu.VMEM)
    return pl.pallas_call(body, name="gather_small", in_specs=[vm], out_specs=vm,
                          out_shape=jax.ShapeDtypeStruct((8 * m, 128), v.dtype),
                          scratch_shapes=[pltpu.SemaphoreType.DMA((7,)), pltpu.SemaphoreType.DMA((7,)), pltpu.SemaphoreType.DMA])(v)


def _swap_halves(g):
    n, rows, _ = g.shape
    half = rows // 2

    def body(g_ref, out_ref, send_sem, recv_sem):
        x, y, c = _place()
        src = g_ref.at[:, pl.ds(pl.multiple_of((1 - c) * half, 8), half), :]
        cp = pltpu.make_async_remote_copy(src_ref=src, dst_ref=out_ref, send_sem=send_sem, recv_sem=recv_sem,
                                          device_id=(x, y, 1 - c), device_id_type=MESH)
        cp.start()
        cp.wait()

    return pl.pallas_call(body, name="swap_halves", in_specs=[ANY], out_specs=ANY,
                          out_shape=jax.ShapeDtypeStruct((n, half, 128), g.dtype),
                          scratch_shapes=[pltpu.SemaphoreType.DMA, pltpu.SemaphoreType.DMA])(g)


def _scatter_chips(hsum):
    n, half, _ = hsum.shape

    def body(h_ref, out_ref, send_sems, recv_sems):
        x, y, c = _place()
        cps = []
        for j, chip in enumerate(_other_chips(x, y)):
            cp = pltpu.make_async_remote_copy(src_ref=h_ref.at[2 * chip[0] + chip[1]], dst_ref=out_ref.at[j],
                                              send_sem=send_sems.at[j], recv_sem=recv_sems.at[j],
                                              device_id=(*chip, c), device_id_type=MESH)
            cp.start()
            cps.append(cp)
        for cp in cps:
            cp.wait()

    return pl.pallas_call(body, name="scatter_chips", in_specs=[ANY], out_specs=ANY,
                          out_shape=jax.ShapeDtypeStruct((3, half, 128), hsum.dtype),
                          scratch_shapes=[pltpu.SemaphoreType.DMA((3,)), pltpu.SemaphoreType.DMA((3,))])(hsum)


def _join_halves(gh):
    half = gh.shape[0]

    def body(g_ref, out_ref, send_sem, recv_sem, local_sem):
        x, y, c = _place()
        dst = out_ref.at[pl.ds(pl.multiple_of(c * half, 8), half), :]
        mine = pltpu.make_async_copy(g_ref, dst, local_sem)
        mine.start()
        cp = pltpu.make_async_remote_copy(src_ref=g_ref, dst_ref=dst, send_sem=send_sem, recv_sem=recv_sem,
                                          device_id=(x, y, 1 - c), device_id_type=MESH)
        cp.start()
        cp.wait_send()
        other = out_ref.at[pl.ds(pl.multiple_of((1 - c) * half, 8), half), :]
        pltpu.make_async_remote_copy(src_ref=g_ref, dst_ref=other, send_sem=send_sem, recv_sem=recv_sem,
                                     device_id=(x, y, 1 - c), device_id_type=MESH).wait_recv()
        mine.wait()

    return pl.pallas_call(body, name="join_halves", in_specs=[ANY], out_specs=ANY,
                          out_shape=jax.ShapeDtypeStruct((2 * half, 128), gh.dtype),
                          scratch_shapes=[pltpu.SemaphoreType.DMA, pltpu.SemaphoreType.DMA, pltpu.SemaphoreType.DMA])(gh)


def _add_sibling(g, recv, c_idx, name):
    n, rows, _ = g.shape
    half = rows // 2
    tr = _tile(half, ADAM_ROWS, 8)
    nb = half // tr

    def body(c_ref, g_ref, r_ref, o_ref):
        o_ref[...] = g_ref[...] + r_ref[...]

    grid_spec = pltpu.PrefetchScalarGridSpec(
        num_scalar_prefetch=1, grid=(n, nb),
        in_specs=[pl.BlockSpec((None, tr, 128), lambda k, i, c: (k, c[0] * nb + i, 0)),
                  pl.BlockSpec((None, tr, 128), lambda k, i, c: (k, i, 0))],
        out_specs=pl.BlockSpec((None, tr, 128), lambda k, i, c: (k, i, 0)))
    return pl.pallas_call(body, name=name, grid_spec=grid_spec, out_shape=jax.ShapeDtypeStruct((n, half, 128), F32),
                          compiler_params=_params(("parallel", "parallel")))(c_idx, g, recv)


def _add_chips(hsum, recv, chip_idx, name):
    n, half, _ = hsum.shape
    tr = _tile(half, ADAM_ROWS, 8)

    def body(k_ref, h_ref, r_ref, o_ref):
        o_ref[...] = ((h_ref[...] + r_ref[0]) + r_ref[1]) + r_ref[2]

    grid_spec = pltpu.PrefetchScalarGridSpec(
        num_scalar_prefetch=1, grid=(half // tr,),
        in_specs=[pl.BlockSpec((None, tr, 128), lambda i, k: (k[0], i, 0)),
                  pl.BlockSpec((3, tr, 128), lambda i, k: (0, i, 0))],
        out_specs=pl.BlockSpec((tr, 128), lambda i, k: (i, 0)))
    return pl.pallas_call(body, name=name, grid_spec=grid_spec, out_shape=jax.ShapeDtypeStruct((half, 128), F32),
                          compiler_params=_params(("parallel",)))(chip_idx, hsum, recv)


def _adamw_math(g, w, m, v):
    m2 = B1 * m + (1.0 - B1) * g
    v2 = B2 * v + (1.0 - B2) * (g * g)
    delta = -LR * ((m2 / BC1) / (jnp.sqrt(v2 / BC2) + AEPS) + WD * w)
    return delta, m2, v2


def _adamw(g, w, m, v, name):
    rows = g.shape[0]
    tr = _tile(rows, ADAM_ROWS, 8)

    def body(g_ref, w_ref, m_ref, v_ref, d_ref, m2_ref, v2_ref):
        d_ref[...], m2_ref[...], v2_ref[...] = _adamw_math(g_ref[...], w_ref[...], m_ref[...], v_ref[...])

    blk = pl.BlockSpec((tr, 128), lambda i: (i, 0))
    shp = jax.ShapeDtypeStruct((rows, 128), F32)
    return pl.pallas_call(body, name=name, grid=(rows // tr,), in_specs=[blk] * 4, out_specs=(blk,) * 3,
                          out_shape=(shp,) * 3, compiler_params=_params(("parallel",)))(g, w, m, v)


def _adamw_small(parts, w, m, v, name):
    rows = w.shape[0]

    def body(p_ref, w_ref, m_ref, v_ref, g_ref, d_ref, m2_ref, v2_ref):
        g = p_ref[0]
        for k in range(1, 8):
            g = g + p_ref[k]
        g_ref[...] = g
        d_ref[...], m2_ref[...], v2_ref[...] = _adamw_math(g, w_ref[...], m_ref[...], v_ref[...])

    shp = jax.ShapeDtypeStruct((rows, 128), F32)
    return pl.pallas_call(body, name=name, out_shape=(shp,) * 4)(parts, w, m, v)


def _pack_rows(parts, rows):
    flat = jnp.concatenate([p.reshape(-1) for p in parts])
    return jnp.pad(flat, (0, rows * 128 - flat.shape[0])).reshape(rows, 128)


def _unpack(flat, sizes, shapes):
    out, off = [], 0
    for n, s in zip(sizes, shapes):
        out.append(flat[off:off + n].reshape(s))
        off += n
    return out


def _to_shards(full, shard_shape, axis):
    if axis == 0:
        return full.reshape(N_CHIP, -1)
    r, cs = shard_shape
    return full.reshape(r, N_CHIP, cs).transpose(1, 0, 2).reshape(N_CHIP, -1)


def _from_shards(sh, shard_shape, axis):
    r, cs = shard_shape
    if axis == 0:
        return sh.reshape(N_CHIP * r, cs)
    return sh.reshape(N_CHIP, r, cs).transpose(1, 0, 2).reshape(r, N_CHIP * cs)


def _heads(t, n):
    s = t.shape[0]
    return t.reshape(s, n * NH, HD).transpose(1, 0, 2)


def _merge(t):
    return t.transpose(1, 0, 2).reshape(t.shape[1], GW)


def _local_step(x0, mem, tgt, W, gains):
    S = x0.shape[0]
    w_in = jnp.pad(W["w_in"], ((0, 0), (0, IN_PAD - IN_COLS)))
    b_f = jnp.pad(gains["b_forget"], ((0, 0), (0, 128 - NH)))
    cw = W["conv_w"].reshape(3, 2, DFF).transpose(1, 0, 2)
    cb = gains["conv_b"].reshape(2, 1, DFF)

    h1 = _rms_cast(x0, gains["attn_norm_g"], "norm_attn")
    qkv = _heads(_mm_nn(h1, w_in[:, :NQKV], CDT, "proj_qkv"), 6)
    fl = _mm_nn(h1, w_in[:, NQKV:], F32, "proj_gate")
    cum = _gate_fwd(fl, b_f, "gate_cumsum")
    c_hm = cum[:, :NH].T
    cq, ck = c_hm[:, :, None], c_hm[:, None, :]
    fq, fk, fv, sq, sk, sv = (qkv[n * NH:(n + 1) * NH] for n in range(6))
    fo_h, lse = _fox_fwd(fq, fk, fv, cq, ck, "fox_fwd")
    so_h, s_lt = _sb_fwd(sq, sk, sv, "sb_fwd")
    fo, so = _merge(fo_h), _merge(so_h)
    x1, mixed = _out_proj(fo, so, gains["fox_out_g"], gains["sb_out_g"], W["w_out"], x0, "out_proj")

    h2 = _rms_cast(x1, gains["xattn_norm_g"], "norm_xattn")
    mn = _rms_cast(mem, gains["mem_norm_g"], "norm_mem")
    mq = _mm_nn(h2, W["w_mq"], CDT, "proj_mq")
    kv = _mm_nn(mn, W["w_mkv"], CDT, "proj_mkv")
    x2, mo = _xattn_fwd(mq, kv, W["w_mo"], x1, "xattn_fwd")

    h3 = _rms_cast(x2, gains["ffn_norm_g"], "norm_ffn")
    u0 = _mm_nn(h3, W["w_up"], CDT, "ffn_up", halves=True)
    act = _conv_act(u0, cw, cb, "conv_act")
    x3 = _mm_nn(act, W["w_down"], F32, "ffn_down", tm=512, residual=x2)
    loss, dx3, dg_final = _loss_bwd(x3, tgt, gains["final_norm_g"].reshape(1, D), "loss")

    gw, gs = {}, {"final_norm_g": dg_final}
    da = _mm_nt(dx3, W["w_down"], "ffn_down_dx", tn=1408, out_dtype=CDT)
    gw["w_down"] = _mm_tn(act, dx3, "ffn_down_dw", tka=1408)
    du, dwb = _conv_act_bwd(u0, da, cw, cb, "conv_act_bwd")
    gw["conv_w"] = dwb[:, :3].transpose(1, 0, 2).reshape(3, 2 * DFF)
    gs["conv_b"] = dwb[:, 3].reshape(1, 2 * DFF)
    du0 = _conv_bwd_input(du, cw, "conv_bwd_input")
    gw["w_up"] = _mm_tn(h3, du0, "ffn_up_dw", tn=1408, b_halves=True)
    dx2, gs["ffn_norm_g"] = _mm_nt_rmsbwd(du0, W["w_up"], x2, gains["ffn_norm_g"], dx3, "ffn_up_dx", tk=1408, a_halves=True)

    dmo = _mm_nt(dx2, W["w_mo"], "mo_dx", tn=512, out_dtype=CDT)
    gw["w_mo"] = _mm_tn(mo, dx2, "mo_dw")
    dmq, dkv = _xattn_bwd(mq, kv, dmo, "xattn_bwd")
    gw["w_mq"] = _mm_tn(h2, dmq, "mq_dw")
    dx1, gs["xattn_norm_g"] = _mm_nt_rmsbwd(dmq, W["w_mq"], x1, gains["xattn_norm_g"], dx2, "mq_dx")
    gw["w_mkv"] = _mm_tn(mn, dkv, "mkv_dw")
    _, gs["mem_norm_g"] = _mm_nt_rmsbwd(dkv, W["w_mkv"], mem, gains["mem_norm_g"], jnp.zeros_like(mem), "mkv_dx")

    gw["w_out"] = _mm_tn(mixed, dx1, "out_dw")
    dfo, dso, gs["fox_out_g"], gs["sb_out_g"] = _out_proj_bwd(dx1, W["w_out"], fo, so, gains["fox_out_g"], gains["sb_out_g"], "out_dx")
    dfo_h, dso_h = _heads(dfo, 1), _heads(dso, 1)
    dfq, dfk, dfv, dck, dcq = _fox_bwd(fq, fk, fv, cq, ck, fo_h, dfo_h, lse, "fox_bwd")
    dsq, dsk, dsv = _sb_bwd(sq, sk, sv, dso_h, s_lt, "sb_bwd")
    dc = jnp.pad((dck[:, 0, :] + dcq[:, :, 0]).T, ((0, 0), (0, 128 - NH)))
    dfl, db = _gate_bwd(dc, fl, b_f, "gate_bwd")
    gs["b_forget"] = db[:, :NH]
    dqkv = jnp.concatenate([dfq, dfk.astype(CDT), dfv.astype(CDT), dsq, dsk.astype(CDT), dsv.astype(CDT)], axis=0)
    dproj = jnp.concatenate([dqkv.transpose(1, 0, 2).reshape(S, NQKV), dfl.astype(CDT)], axis=1)
    gw["w_in"] = _mm_tn(h1, dproj, "in_dw", tn=640)[:, :IN_COLS]
    dx0, gs["attn_norm_g"] = _mm_nt_rmsbwd(dproj, w_in, x0, gains["attn_norm_g"], dx1, "in_dx", tk=640)
    return loss, dx0, gw, gs


NAMES = ("attn_norm_g", "w_in", "b_forget", "fox_out_g", "sb_out_g", "w_out", "xattn_norm_g", "mem_norm_g", "w_mq",
         "w_mkv", "w_mo", "ffn_norm_g", "w_up", "conv_w", "conv_b", "w_down", "final_norm_g")


def _step(x, mem, loss_target, w, m, v):
    xi, yi, ci = _place()
    big_shapes = [s for _, s, _ in BIG]

    parts = []
    for name, shape, _ in BIG:
        blk = w[name].reshape(shape)
        parts.append(lax.bitcast_convert_type(blk, CDT) if name == "conv_w" else blk.astype(CDT))
    gathered = _gather_weights(_pack_rows(parts, ROWS_G)).reshape(N_CHIP, ROWS_G * 128)
    W, off = {}, 0
    for (name, shape, axis), n in zip(BIG, GATHER_SIZES):
        sh = gathered[:, off:off + n]
        off += n
        if name == "conv_w":
            sh = lax.bitcast_convert_type(sh.reshape(N_CHIP, n // 2, 2), F32)
        W[name] = _from_shards(sh, shape, axis)
    gains = {name: w[name].reshape(1, -1) for name, _ in SMALL}

    loss, grad_x, gw, gs = _local_step(x[0], mem[0], loss_target[0], W, gains)

    g_flat = jnp.concatenate([_to_shards(gw[name], shape, axis) for name, shape, axis in BIG], axis=1)
    g_flat = jnp.pad(g_flat, ((0, 0), (0, ROWS_F * 128 - P_BIG))).reshape(N_CHIP, ROWS_F, 128)
    pair_sum = _add_sibling(g_flat, _swap_halves(g_flat), jnp.reshape(ci, (1,)).astype(jnp.int32), "add_sibling")
    g_half = _add_chips(pair_sum, _scatter_chips(pair_sum), jnp.reshape(2 * xi + yi, (1,)).astype(jnp.int32), "add_chips")
    g_big = _join_halves(g_half)
    small = jnp.concatenate([gs[name].reshape(-1) for name, _ in SMALL] + [loss[0, :1]])
    small = jnp.pad(small, (0, ROWS_S * 128 - P_SMALL)).reshape(ROWS_S, 128)
    small_parts = _gather_small(small).reshape(8, ROWS_S, 128)

    def flat_big(d, prefix):
        return _pack_rows([d[prefix + name].reshape(shape) for name, shape, _ in BIG], ROWS_F)

    def flat_small(d, prefix):
        return _pack_rows([d[prefix + name] for name, _ in SMALL], ROWS_S)

    allw = {**w, **{"m_" + k: a for k, a in m.items()}, **{"v_" + k: a for k, a in v.items()}}
    d_big, m_big, v_big = _adamw(g_big, flat_big(allw, ""), flat_big(allw, "m_"), flat_big(allw, "v_"), "adamw")
    g_sm, d_sm, m_sm, v_sm = _adamw_small(small_parts, flat_small(allw, ""), flat_small(allw, "m_"), flat_small(allw, "v_"), "adamw_small")

    outs = {}
    for prefix, big, sm in (("grad_", g_big, g_sm), ("delta_", d_big, d_sm), ("new_m_", m_big, m_sm), ("new_v_", v_big, v_sm)):
        for (name, _, _), arr in zip(BIG, _unpack(big.reshape(-1), BIG_SIZES, big_shapes)):
            outs[prefix + name] = arr.reshape(w[name].shape)
        for (name, n), arr in zip(SMALL, _unpack(sm.reshape(-1), [n for _, n in SMALL], [(n,) for _, n in SMALL])):
            outs[prefix + name] = arr.reshape(w[name].shape)
    total_loss = g_sm.reshape(-1)[P_SMALL - 1]
    return (total_loss, grad_x[None], *[outs[p + n] for p in ("grad_", "delta_", "new_m_", "new_v_") for n in NAMES])


def kernel(x, mem, attn_norm_g, w_in, b_forget, fox_out_g, sb_out_g, w_out, xattn_norm_g, mem_norm_g, w_mq, w_mkv, w_mo, ffn_norm_g, w_up, conv_w, conv_b, w_down, final_norm_g, loss_target, m_attn_norm_g, m_w_in, m_b_forget, m_fox_out_g, m_sb_out_g, m_w_out, m_xattn_norm_g, m_mem_norm_g, m_w_mq, m_w_mkv, m_w_mo, m_ffn_norm_g, m_w_up, m_conv_w, m_conv_b, m_w_down, m_final_norm_g, v_attn_norm_g, v_w_in, v_b_forget, v_fox_out_g, v_sb_out_g, v_w_out, v_xattn_norm_g, v_mem_norm_g, v_w_mq, v_w_mkv, v_w_mo, v_ffn_norm_g, v_w_up, v_conv_w, v_conv_b, v_w_down, v_final_norm_g):
    given = dict(locals())
    w = {n: given[n] for n in NAMES}
    m = {n: given["m_" + n] for n in NAMES}
    v = {n: given["v_" + n] for n in NAMES}
    return _step(x, mem, loss_target, w, m, v)
```

```python
import functools

import numpy as np
import jax
import jax.numpy as jnp
from jax import lax
from jax.experimental import pallas as pl
from jax.experimental.pallas import tpu as pltpu

F32 = jnp.float32
CDT = jnp.bfloat16
MESH = pl.DeviceIdType.MESH

D = 1024
HD = 64
NH = 8
GW = NH * HD
NQKV = 6 * GW
IN_COLS = NQKV + NH
IN_PAD = NQKV + 128
NMH = 4
MHD = D // NMH
DFF = 2816
EPS = 1e-6
ATT_SCALE = HD ** -0.5
MEM_SCALE = MHD ** -0.5
NEG = -1e30

LR, B1, B2, AEPS, WD, STEP = 0.001, 0.9, 0.999, 1e-08, 0.01, 10
BC1 = 1.0 - B1 ** STEP
BC2 = 1.0 - B2 ** STEP

ATT_TILES = {"fox_fwd": (512, 1024), "fox_bwd": (512, 1024), "sb_fwd": (512, 1024), "sb_bwd": (1024, 1024)}
W_SB = 256
VMEM_LIMIT = 52 * 2 ** 20

N_CHIP = 4
BIG = (("w_in", (D, IN_COLS // N_CHIP), 1), ("w_out", (D // N_CHIP, D), 0), ("w_mq", (D // N_CHIP, D), 0),
       ("w_mkv", (D, 2 * D // N_CHIP), 1), ("w_mo", (D // N_CHIP, D), 0), ("w_up", (D, 2 * DFF // N_CHIP), 1),
       ("conv_w", (3, 2 * DFF // N_CHIP), 1), ("w_down", (DFF // N_CHIP, D), 0))
BIG_SIZES = tuple(int(np.prod(s)) for _, s, _ in BIG)
P_BIG = sum(BIG_SIZES)
ROWS_F = 33 * 1024
assert ROWS_F * 128 >= P_BIG
HALF_F = ROWS_F // 2
ADAM_ROWS = 1536
GATHER_SIZES = tuple(2 * n if name == "conv_w" else n for (name, _, _), n in zip(BIG, BIG_SIZES))
ROWS_G = -(-sum(GATHER_SIZES) // 4096) * 32
HALF_G = ROWS_G // 2
SMALL = (("attn_norm_g", 1024), ("b_forget", 8), ("fox_out_g", 512), ("sb_out_g", 512), ("xattn_norm_g", 1024),
         ("mem_norm_g", 1024), ("ffn_norm_g", 1024), ("conv_b", 2 * DFF), ("final_norm_g", 1024))
P_SMALL = sum(n for _, n in SMALL) + 1
ROWS_S = -(-P_SMALL // 1024) * 8


def _params(sem=None, vmem=VMEM_LIMIT):
    return pltpu.CompilerParams(dimension_semantics=sem, vmem_limit_bytes=vmem)


def _tile(n, pref, mult):
    t = (min(pref, n) // mult) * mult
    while t >= mult:
        if n % t == 0:
            return t
        t -= mult
    return n


def _dot(a, b):
    return jnp.dot(a, b, preferred_element_type=F32)


def _dot_nt(a, b):
    return lax.dot_general(a, b, (((1,), (1,)), ((), ())), preferred_element_type=F32)


def _dot_tn(a, b):
    return lax.dot_general(a, b, (((0,), (0,)), ((), ())), preferred_element_type=F32)


def _split3(x):
    h1 = x.astype(CDT)
    r1 = x - h1.astype(F32)
    h2 = r1.astype(CDT)
    h3 = (r1 - h2.astype(F32)).astype(CDT)
    return h1, h2, h3


def _split2(x):
    h1 = x.astype(CDT)
    return h1, (x - h1.astype(F32)).astype(CDT)


def _rms_bwd(dh, x, g):
    r = lax.rsqrt(jnp.mean(x * x, axis=-1, keepdims=True) + EPS)
    xn = x * r
    dg = jnp.sum(dh * xn, axis=0, keepdims=True)
    dhg = dh * g
    dx = r * (dhg - xn * jnp.mean(dhg * xn, axis=-1, keepdims=True))
    return dx, dg


def _mm_nn(a, b, out_dtype, name, *, tm=1024, tn=512, residual=None, halves=False):
    M, K = a.shape
    N = b.shape[1]
    tm = _tile(M, tm, 16)
    tn = _tile(N // 2 if halves else N, tn, 128)
    nj = N // tn

    def body(*refs):
        a_ref, b_ref = refs[0], refs[1]
        o_ref = refs[-1]
        acc = _dot(a_ref[...].astype(CDT), b_ref[...].astype(CDT))
        if residual is not None:
            acc = acc + refs[2][...]
        o_ref[...] = acc.astype(o_ref.dtype)

    in_specs = [pl.BlockSpec((tm, K), lambda i, j: (i, 0)), pl.BlockSpec((K, tn), lambda i, j: (0, j))]
    ops = [a, b]
    if residual is not None:
        in_specs.append(pl.BlockSpec((tm, tn), lambda i, j: (i, j)))
        ops.append(residual)
    if halves:
        njh = nj // 2
        out_shape = jax.ShapeDtypeStruct((2, M, N // 2), out_dtype)
        out_spec = pl.BlockSpec((None, tm, tn), lambda i, j: (j // njh, i, j % njh))
    else:
        out_shape = jax.ShapeDtypeStruct((M, N), out_dtype)
        out_spec = pl.BlockSpec((tm, tn), lambda i, j: (i, j))
    return pl.pallas_call(body, name=name, grid=(M // tm, nj), in_specs=in_specs, out_specs=out_spec,
                          out_shape=out_shape, compiler_params=_params(("parallel", "parallel")))(*ops)


def _mm_tn(a, b, name, *, tka=512, tn=1024, ts=512, b_halves=False):
    S, Ka = a.shape
    N = 2 * b.shape[2] if b_halves else b.shape[1]
    tka = _tile(Ka, tka, 128)
    tn = _tile(N // 2 if b_halves else N, tn, 128)
    ts = _tile(S, ts, 16)
    nn = N // tn

    def body(a_ref, b_ref, o_ref):
        @pl.when(pl.program_id(2) == 0)
        def _():
            o_ref[...] = jnp.zeros_like(o_ref)
        o_ref[...] += _dot_tn(a_ref[...].astype(CDT), b_ref[...].astype(CDT))

    if b_halves:
        nnh = nn // 2
        b_spec = pl.BlockSpec((None, ts, tn), lambda i, j, s: (j // nnh, s, j % nnh))
    else:
        b_spec = pl.BlockSpec((ts, tn), lambda i, j, s: (s, j))
    return pl.pallas_call(
        body, name=name, grid=(Ka // tka, nn, S // ts),
        in_specs=[pl.BlockSpec((ts, tka), lambda i, j, s: (s, i)), b_spec],
        out_specs=pl.BlockSpec((tka, tn), lambda i, j, s: (i, j)),
        out_shape=jax.ShapeDtypeStruct((Ka, N), F32),
        compiler_params=_params(("parallel", "parallel", "arbitrary")))(a, b)


def _mm_nt(a, b, name, *, tm=512, tn=None, tk=None, a_halves=False, out_dtype=F32,
           epilogue=None, extra=(), extra_specs=(), out_shape=None, out_specs=None):
    if a_halves:
        M, K = a.shape[1], 2 * a.shape[2]
    else:
        M, K = a.shape
    N = b.shape[0]
    tm = _tile(M, tm, 16)
    tn = N if (epilogue is not None or tn is None) else _tile(N, tn, 128)
    tk = K if tk is None else _tile(K // 2 if a_halves else K, tk, 128)
    nk = K // tk
    n_extra = len(extra)

    def body(*refs):
        a_ref, b_ref = refs[0], refs[1]
        extra_refs = refs[2:2 + n_extra]
        out_refs = refs[2 + n_extra:-1]
        acc_ref = refs[-1]
        k = pl.program_id(2)

        @pl.when(k == 0)
        def _():
            acc_ref[...] = jnp.zeros_like(acc_ref)
        acc_ref[...] += _dot_nt(a_ref[...].astype(CDT), b_ref[...].astype(CDT))

        @pl.when(k == nk - 1)
        def _():
            if epilogue is None:
                out_refs[0][...] = acc_ref[...].astype(out_refs[0].dtype)
            else:
                epilogue(acc_ref[...], pl.program_id(0), extra_refs, out_refs)

    if a_halves:
        nkh = nk // 2
        a_spec = pl.BlockSpec((None, tm, tk), lambda i, j, k: (k // nkh, i, k % nkh))
    else:
        a_spec = pl.BlockSpec((tm, tk), lambda i, j, k: (i, k))
    if epilogue is None:
        out_shape = jax.ShapeDtypeStruct((M, N), out_dtype)
        out_specs = pl.BlockSpec((tm, tn), lambda i, j, k: (i, j))
        sem = ("parallel", "parallel", "arbitrary")
    else:
        sem = ("arbitrary", "arbitrary", "arbitrary")
    return pl.pallas_call(
        body, name=name, grid=(M // tm, N // tn, nk),
        in_specs=[a_spec, pl.BlockSpec((tn, tk), lambda i, j, k: (j, k)), *extra_specs],
        out_specs=out_specs, out_shape=out_shape,
        scratch_shapes=[pltpu.VMEM((tm, tn), F32)],
        compiler_params=_params(sem))(a, b, *extra)


def _mm_nt_rmsbwd(a, b, x, g, dres, name, *, tm=512, tk=None, a_halves=False):
    M = x.shape[0]
    tm = _tile(M, tm, 16)

    def epilogue(acc, i, extra_refs, out_refs):
        x_ref, g_ref, r_ref = extra_refs
        dx_ref, dg_ref = out_refs
        dx, dg = _rms_bwd(acc, x_ref[...], g_ref[...])
        dx_ref[...] = r_ref[...] + dx

        @pl.when(i == 0)
        def _():
            dg_ref[...] = jnp.zeros_like(dg_ref)
        dg_ref[...] += dg

    row = pl.BlockSpec((tm, D), lambda i, j, k: (i, 0))
    vec = pl.BlockSpec((1, D), lambda i, j, k: (0, 0))
    return _mm_nt(a, b, name, tm=tm, tk=tk, a_halves=a_halves, epilogue=epilogue,
                  extra=(x, g, dres), extra_specs=(row, vec, row),
                  out_shape=(jax.ShapeDtypeStruct((M, D), F32), jax.ShapeDtypeStruct((1, D), F32)),
                  out_specs=(row, vec))


def _rms_cast(x, g, name, *, tm=512):
    M, W = x.shape
    tm = _tile(M, tm, 16)

    def body(x_ref, g_ref, o_ref):
        xf = x_ref[...]
        r = lax.rsqrt(jnp.mean(xf * xf, axis=-1, keepdims=True) + EPS)
        o_ref[...] = (xf * r * g_ref[...]).astype(o_ref.dtype)

    return pl.pallas_call(body, name=name, grid=(M // tm,),
                          in_specs=[pl.BlockSpec((tm, W), lambda i: (i, 0)), pl.BlockSpec((1, W), lambda i: (0, 0))],
                          out_specs=pl.BlockSpec((tm, W), lambda i: (i, 0)),
                          out_shape=jax.ShapeDtypeStruct((M, W), CDT),
                          compiler_params=_params(("parallel",)))(x, g)


def _tri(n, lower):
    r = lax.broadcasted_iota(jnp.int32, (n, n), 0)
    c = lax.broadcasted_iota(jnp.int32, (n, n), 1)
    return (c <= r if lower else c >= r).astype(CDT)


def _gate_fwd(fl, b, name, *, tm=512):
    S = fl.shape[0]
    tm = _tile(S, tm, 16)

    def body(f_ref, b_ref, c_ref, carry):
        @pl.when(pl.program_id(0) == 0)
        def _():
            carry[...] = jnp.zeros_like(carry)
        z = f_ref[...] + b_ref[...]
        lf = jnp.minimum(z, 0.0) - jnp.log(1.0 + jnp.exp(-jnp.abs(z)))
        tri = _tri(tm, True)
        cum = sum(_dot(tri, p) for p in _split3(lf)) + carry[...]
        c_ref[...] = cum
        carry[...] = cum[tm - 1:tm, :]

    return pl.pallas_call(body, name=name, grid=(S // tm,),
                          in_specs=[pl.BlockSpec((tm, 128), lambda i: (i, 0)), pl.BlockSpec((1, 128), lambda i: (0, 0))],
                          out_specs=pl.BlockSpec((tm, 128), lambda i: (i, 0)),
                          out_shape=jax.ShapeDtypeStruct((S, 128), F32),
                          scratch_shapes=[pltpu.VMEM((1, 128), F32)],
                          compiler_params=_params(("arbitrary",)))(fl, b)


def _gate_bwd(dc, fl, b, name, *, tm=512):
    S = fl.shape[0]
    tm = _tile(S, tm, 16)
    nb = S // tm

    def body(dc_ref, f_ref, b_ref, df_ref, db_ref, carry):
        @pl.when(pl.program_id(0) == 0)
        def _():
            carry[...] = jnp.zeros_like(carry)
            db_ref[...] = jnp.zeros_like(db_ref)
        tri = _tri(tm, False)
        suf = sum(_dot(tri, p) for p in _split3(dc_ref[...])) + carry[...]
        carry[...] = suf[0:1, :]
        df = suf * jax.nn.sigmoid(-(f_ref[...] + b_ref[...]))
        df_ref[...] = df
        db_ref[...] += jnp.sum(df, axis=0, keepdims=True)

    rev = pl.BlockSpec((tm, 128), lambda i: (nb - 1 - i, 0))
    vec = pl.BlockSpec((1, 128), lambda i: (0, 0))
    return pl.pallas_call(body, name=name, grid=(nb,), in_specs=[rev, rev, vec], out_specs=(rev, vec),
                          out_shape=(jax.ShapeDtypeStruct((S, 128), F32), jax.ShapeDtypeStruct((1, 128), F32)),
                          scratch_shapes=[pltpu.VMEM((1, 128), F32)],
                          compiler_params=_params(("arbitrary",)))(dc, fl, b)


def _out_proj(fo, so, gf, gs, w_out, x0, name, *, tm=512):
    S = fo.shape[0]
    tm = _tile(S, tm, 16)

    def body(fo_ref, so_ref, gf_ref, gs_ref, w_ref, x_ref, x1_ref, mx_ref):
        for ref, g_ref, lo in ((fo_ref, gf_ref, 0), (so_ref, gs_ref, GW)):
            o = ref[...]
            r = lax.rsqrt(jnp.mean(o * o, axis=-1, keepdims=True) + EPS)
            mx_ref[:, lo:lo + GW] = (o * r * g_ref[...]).astype(CDT)
        x1_ref[...] = x_ref[...] + _dot(mx_ref[...], w_ref[...])

    half = pl.BlockSpec((tm, GW), lambda i: (i, 0))
    gvec = pl.BlockSpec((1, GW), lambda i: (0, 0))
    row = pl.BlockSpec((tm, D), lambda i: (i, 0))
    return pl.pallas_call(body, name=name, grid=(S // tm,),
                          in_specs=[half, half, gvec, gvec, pl.BlockSpec((D, D), lambda i: (0, 0)), row],
                          out_specs=(row, row),
                          out_shape=(jax.ShapeDtypeStruct((S, D), F32), jax.ShapeDtypeStruct((S, D), CDT)),
                          compiler_params=_params(("parallel",)))(fo, so, gf, gs, w_out, x0)


def _out_proj_bwd(dx1, w_out, fo, so, gf, gs, name, *, tm=512):
    S = fo.shape[0]
    tm = _tile(S, tm, 16)

    def epilogue(acc, i, extra_refs, out_refs):
        fo_ref, so_ref, gf_ref, gs_ref = extra_refs
        dfo_ref, dso_ref, dgf_ref, dgs_ref = out_refs

        @pl.when(i == 0)
        def _():
            dgf_ref[...] = jnp.zeros_like(dgf_ref)
            dgs_ref[...] = jnp.zeros_like(dgs_ref)
        for lo, o_ref, g_ref, do_ref, dg_ref in ((0, fo_ref, gf_ref, dfo_ref, dgf_ref), (GW, so_ref, gs_ref, dso_ref, dgs_ref)):
            dx, dg = _rms_bwd(acc[:, lo:lo + GW], o_ref[...], g_ref[...])
            do_ref[...] = dx.astype(do_ref.dtype)
            dg_ref[...] += dg

    half = pl.BlockSpec((tm, GW), lambda i, j, k: (i, 0))
    gvec = pl.BlockSpec((1, GW), lambda i, j, k: (0, 0))
    return _mm_nt(dx1, w_out, name, tm=tm, epilogue=epilogue, extra=(fo, so, gf, gs),
                  extra_specs=(half, half, gvec, gvec),
                  out_shape=(jax.ShapeDtypeStruct((S, GW), CDT), jax.ShapeDtypeStruct((S, GW), CDT),
                             jax.ShapeDtypeStruct((1, GW), F32), jax.ShapeDtypeStruct((1, GW), F32)),
                  out_specs=(half, half, gvec, gvec))


def _loss_bwd(x3, tgt, g, name, *, tm=512):
    S = x3.shape[0]
    tm = _tile(S, tm, 16)

    def body(x_ref, t_ref, g_ref, dx_ref, loss_ref, dg_ref):
        @pl.when(pl.program_id(0) == 0)
        def _():
            loss_ref[...] = jnp.zeros_like(loss_ref)
            dg_ref[...] = jnp.zeros_like(dg_ref)
        x = x_ref[...]
        gv = g_ref[...]
        r = lax.rsqrt(jnp.mean(x * x, axis=-1, keepdims=True) + EPS)
        xn = x * r
        err = xn * gv - t_ref[...]
        loss_ref[...] += jnp.full(loss_ref.shape, 0.5 * jnp.sum(jnp.mean(err * err, axis=-1, keepdims=True)), F32)
        dy = err * (1.0 / D)
        dg_ref[...] += jnp.sum(dy * xn, axis=0, keepdims=True)
        dyg = dy * gv
        dx_ref[...] = r * (dyg - xn * jnp.mean(dyg * xn, axis=-1, keepdims=True))

    row = pl.BlockSpec((tm, D), lambda i: (i, 0))
    vec = pl.BlockSpec((1, D), lambda i: (0, 0))
    dx3, loss, dg = pl.pallas_call(
        body, name=name, grid=(S // tm,), in_specs=[row, row, vec],
        out_specs=(row, pl.BlockSpec((1, 128), lambda i: (0, 0)), vec),
        out_shape=(jax.ShapeDtypeStruct((S, D), F32), jax.ShapeDtypeStruct((1, 128), F32), jax.ShapeDtypeStruct((1, D), F32)),
        compiler_params=_params(("arbitrary",)))(x3, tgt, g)
    return loss, dx3, dg


MASKED, FIRST, LAST = 1, 2, 4


def _att_tiles(name, S):
    tq, tk = ATT_TILES[name]
    return min(tq, S), min(tk, S)


def _pairs(S, tq, tk, descending=True):
    assert tk % tq == 0 and S % tk == 0
    qi, kj, fl = [], [], []
    for i in range(S // tq):
        last = ((i + 1) * tq - 1) // tk
        order = list(range(last, -1, -1) if descending else range(last + 1))
        for pos, kb in enumerate(order):
            qi.append(i)
            kj.append(kb)
            fl.append((MASKED if (kb + 1) * tk - 1 > i * tq else 0) | (FIRST if pos == 0 else 0) | (LAST if pos == last else 0))
    return tuple(jnp.asarray(np.asarray(a, np.int32)) for a in (qi, kj, fl))


def _att_specs(tq, tk):
    qblk = pl.BlockSpec((1, tq, HD), lambda h, n, qi, kj, fl: (h, qi[n], 0))
    kblk = pl.BlockSpec((1, tk, HD), lambda h, n, qi, kj, fl: (h, kj[n], 0))
    qcol = pl.BlockSpec((1, tq, 1), lambda h, n, qi, kj, fl: (h, qi[n], 0))
    krow = pl.BlockSpec((1, 1, tk), lambda h, n, qi, kj, fl: (h, 0, kj[n]))
    return qblk, kblk, qcol, krow


def _causal(tq, w, ahead, strict):
    diff = lax.broadcasted_iota(jnp.int32, (tq, w), 1) - lax.broadcasted_iota(jnp.int32, (tq, w), 0)
    return diff < ahead if strict else diff <= ahead


def _masked_or_not(flags, step):
    pl.when(flags % 2 == 1)(functools.partial(step, True))
    pl.when(flags % 2 == 0)(functools.partial(step, False))


def _fox_fwd(q, k, v, cq, ck, name):
    S = q.shape[1]
    tq, tk = _att_tiles("fox_fwd", S)
    qi, kj, fl = _pairs(S, tq, tk)
    qblk, kblk, qcol, krow = _att_specs(tq, tk)

    def body(qi_ref, kj_ref, fl_ref, q_ref, k_ref, v_ref, cq_ref, ck_ref, o_ref, lse_ref, m_s, l_s, acc_s):
        n = pl.program_id(1)
        i, kb, flags = qi_ref[n], kj_ref[n], fl_ref[n]

        @pl.when(flags & FIRST != 0)
        def _():
            m_s[...] = jnp.full_like(m_s, NEG)
            l_s[...] = jnp.zeros_like(l_s)
            acc_s[...] = jnp.zeros_like(acc_s)

        def step(masked):
            s = _dot_nt(q_ref[0] * ATT_SCALE, k_ref[0]) + cq_ref[0] - ck_ref[0]
            if masked:
                s = jnp.where(_causal(tq, tk, i * tq - kb * tk, False), s, NEG)
            m_new = jnp.maximum(m_s[...], jnp.max(s, axis=-1, keepdims=True))
            alpha = jnp.exp(m_s[...] - m_new)
            p = jnp.exp(s - m_new)
            l_s[...] = alpha * l_s[...] + jnp.sum(p, axis=-1, keepdims=True)
            acc_s[...] = alpha * acc_s[...] + _dot(p.astype(CDT), v_ref[0])
            m_s[...] = m_new

        _masked_or_not(flags, step)

        @pl.when(flags & LAST != 0)
        def _():
            o_ref[0] = acc_s[...] / l_s[...]
            lse_ref[0] = m_s[...] + jnp.log(l_s[...])

    grid_spec = pltpu.PrefetchScalarGridSpec(
        num_scalar_prefetch=3, grid=(NH, int(qi.shape[0])),
        in_specs=[qblk, kblk, kblk, qcol, krow], out_specs=(qblk, qcol),
        scratch_shapes=[pltpu.VMEM((tq, 1), F32), pltpu.VMEM((tq, 1), F32), pltpu.VMEM((tq, HD), F32)])
    return pl.pallas_call(body, name=name, grid_spec=grid_spec,
                          out_shape=(jax.ShapeDtypeStruct((NH, S, HD), F32), jax.ShapeDtypeStruct((NH, S, 1), F32)),
                          compiler_params=_params(("parallel", "arbitrary")))(qi, kj, fl, q, k, v, cq, ck)


def _fox_bwd(q, k, v, cq, ck, o, do, lse, name):
    S = q.shape[1]
    tq, tk = _att_tiles("fox_bwd", S)
    qi, kj, fl = _pairs(S, tq, tk)
    qblk, kblk, qcol, krow = _att_specs(tq, tk)

    def body(qi_ref, kj_ref, fl_ref, q_ref, k_ref, v_ref, cq_ref, ck_ref, o_ref, do_ref, lse_ref,
             dq_ref, dk_ref, dv_ref, dck_ref, dcq_ref, dq_s, dl_s, dcq_s):
        n = pl.program_id(1)
        i, kb, flags = qi_ref[n], kj_ref[n], fl_ref[n]

        @pl.when(n == 0)
        def _():
            dk_ref[...] = jnp.zeros_like(dk_ref)
            dv_ref[...] = jnp.zeros_like(dv_ref)
            dck_ref[...] = jnp.zeros_like(dck_ref)

        @pl.when(flags & FIRST != 0)
        def _():
            dq_s[...] = jnp.zeros_like(dq_s)
            dcq_s[...] = jnp.zeros_like(dcq_s)
            dl_s[...] = jnp.sum(do_ref[0].astype(F32) * o_ref[0], axis=-1, keepdims=True)

        def step(masked):
            qs = q_ref[0] * ATT_SCALE
            do = do_ref[0]
            p = jnp.exp(_dot_nt(qs, k_ref[0]) + cq_ref[0] - ck_ref[0] - lse_ref[0])
            if masked:
                p = jnp.where(_causal(tq, tk, i * tq - kb * tk, False), p, 0.0)
            ds = p * (_dot_nt(do, v_ref[0]) - dl_s[...])
            dsb = ds.astype(CDT)
            dq_s[...] += _dot(dsb, k_ref[0])
            rows = pl.ds(pl.multiple_of(kb * tk, tk), tk)
            dk_ref[0, rows, :] += _dot_tn(dsb, qs)
            dv_ref[0, rows, :] += _dot_tn(p.astype(CDT), do)
            dck_ref[0, :, rows] += -jnp.sum(ds, axis=0, keepdims=True)
            dcq_s[...] += jnp.sum(ds, axis=-1, keepdims=True)

        _masked_or_not(flags, step)

        @pl.when(flags & LAST != 0)
        def _():
            dq_ref[0] = (dq_s[...] * ATT_SCALE).astype(dq_ref.dtype)
            dcq_ref[0] = dcq_s[...]

    whole = pl.BlockSpec((1, S, HD), lambda h, n, qi, kj, fl: (h, 0, 0))
    grid_spec = pltpu.PrefetchScalarGridSpec(
        num_scalar_prefetch=3, grid=(NH, int(qi.shape[0])),
        in_specs=[qblk, kblk, kblk, qcol, krow, qblk, qblk, qcol],
        out_specs=(qblk, whole, whole, pl.BlockSpec((1, 1, S), lambda h, n, qi, kj, fl: (h, 0, 0)), qcol),
        scratch_shapes=[pltpu.VMEM((tq, HD), F32), pltpu.VMEM((tq, 1), F32), pltpu.VMEM((tq, 1), F32)])
    return pl.pallas_call(body, name=name, grid_spec=grid_spec,
                          out_shape=(jax.ShapeDtypeStruct((NH, S, HD), CDT), jax.ShapeDtypeStruct((NH, S, HD), F32),
                                     jax.ShapeDtypeStruct((NH, S, HD), F32), jax.ShapeDtypeStruct((NH, 1, S), F32),
                                     jax.ShapeDtypeStruct((NH, S, 1), F32)),
                          compiler_params=_params(("parallel", "arbitrary")))(qi, kj, fl, q, k, v, cq, ck, o, do, lse)


LOG2E = 1.4426950408889634


def _sb_softplus2(qs, ksub, mask):
    z2 = _dot_nt(qs, ksub) * LOG2E
    sp2 = jnp.maximum(z2, 0.0) + jnp.log2(1.0 + jnp.exp2(-jnp.abs(z2)))
    return z2, sp2 if mask is None else jnp.where(mask, sp2, 0.0)


def _strict_tri(n, upper, value):
    r = lax.broadcasted_iota(jnp.int32, (n, n), 0)
    c = lax.broadcasted_iota(jnp.int32, (n, n), 1)
    return jnp.where(r < c if upper else r > c, value, 0.0).astype(CDT)


def _sb_fwd(q, k, v, name):
    S = q.shape[1]
    tq, tk = _att_tiles("sb_fwd", S)
    W = min(W_SB, tk)
    qi, kj, fl = _pairs(S, tq, tk)
    qblk, kblk, qcol, _ = _att_specs(tq, tk)

    def body(qi_ref, kj_ref, fl_ref, q_ref, k_ref, v_ref, o_ref, lt_ref, run_s, acc_s):
        n = pl.program_id(1)
        i, kb, flags = qi_ref[n], kj_ref[n], fl_ref[n]

        @pl.when(flags & FIRST != 0)
        def _():
            run_s[...] = jnp.zeros_like(run_s)
            acc_s[...] = jnp.zeros_like(acc_s)

        def step(masked):
            qs = q_ref[0] * ATT_SCALE
            neg_later = _strict_tri(W, False, -1.0)
            run = run_s[...]
            acc = acc_s[...]
            for sub in range(tk // W - 1, -1, -1):
                cols = slice(sub * W, (sub + 1) * W)
                mask = _causal(tq, W, i * tq - kb * tk - sub * W, True) if masked else None
                z2, sp2 = _sb_softplus2(qs, k_ref[0, cols, :], mask)
                excl = _dot(sp2.astype(CDT), neg_later)
                a = jnp.exp2((z2 - sp2) + (excl + run))
                if masked:
                    a = jnp.where(mask, a, 0.0)
                acc = acc + _dot(a.astype(CDT), v_ref[0, cols, :])
                run = run + (excl[:, 0:1] - sp2[:, 0:1])
            run_s[...] = run
            acc_s[...] = acc

        _masked_or_not(flags, step)

        @pl.when(flags & LAST != 0)
        def _():
            o_ref[0] = acc_s[...]
            lt_ref[0] = run_s[...]

    grid_spec = pltpu.PrefetchScalarGridSpec(
        num_scalar_prefetch=3, grid=(NH, int(qi.shape[0])), in_specs=[qblk, kblk, kblk], out_specs=(qblk, qcol),
        scratch_shapes=[pltpu.VMEM((tq, 1), F32), pltpu.VMEM((tq, HD), F32)])
    return pl.pallas_call(body, name=name, grid_spec=grid_spec,
                          out_shape=(jax.ShapeDtypeStruct((NH, S, HD), F32), jax.ShapeDtypeStruct((NH, S, 1), F32)),
                          compiler_params=_params(("parallel", "arbitrary")))(qi, kj, fl, q, k, v)


def _sb_bwd(q, k, v, do, lt, name):
    S = q.shape[1]
    tq, tk = _att_tiles("sb_bwd", S)
    W = min(W_SB, tk)
    qi, kj, fl = _pairs(S, tq, tk, descending=False)
    qblk, kblk, qcol, _ = _att_specs(tq, tk)

    def body(qi_ref, kj_ref, fl_ref, q_ref, k_ref, v_ref, do_ref, lt_ref, dq_ref, dk_ref, dv_ref, passed_s, gsum_s, dq_s):
        n = pl.program_id(1)
        i, kb, flags = qi_ref[n], kj_ref[n], fl_ref[n]

        @pl.when(n == 0)
        def _():
            dk_ref[...] = jnp.zeros_like(dk_ref)
            dv_ref[...] = jnp.zeros_like(dv_ref)

        @pl.when(flags & FIRST != 0)
        def _():
            passed_s[...] = jnp.zeros_like(passed_s)
            gsum_s[...] = jnp.zeros_like(gsum_s)
            dq_s[...] = jnp.zeros_like(dq_s)

        def step(masked):
            qs = q_ref[0] * ATT_SCALE
            do = do_ref[0]
            neg_later = _strict_tri(W, False, -1.0)
            earlier = _strict_tri(W, True, 1.0)
            for sub in range(tk // W):
                cols = slice(sub * W, (sub + 1) * W)
                mask = _causal(tq, W, i * tq - kb * tk - sub * W, True) if masked else None
                ksub = k_ref[0, cols, :]
                z2, sp2 = _sb_softplus2(qs, ksub, mask)
                excl = _dot(sp2.astype(CDT), neg_later)
                through = passed_s[...] + (excl[:, 0:1] - sp2[:, 0:1])
                t1 = z2 - sp2
                sig = jnp.exp2(t1)
                a = jnp.exp2(t1 + (excl + (lt_ref[0] - through)))
                if masked:
                    a = jnp.where(mask, a, 0.0)
                dl = _dot_nt(do, v_ref[0, cols, :]) * a
                before = _dot(dl.astype(CDT), earlier)
                dz = dl - sig * (dl + (before + gsum_s[...]))
                if masked:
                    dz = jnp.where(mask, dz, 0.0)
                dzb = dz.astype(CDT)
                dq_s[...] += _dot(dzb, ksub)
                rows = pl.ds(pl.multiple_of(kb * tk + sub * W, W), W)
                dk_ref[0, rows, :] += _dot_tn(dzb, qs)
                dv_ref[0, rows, :] += _dot_tn(a.astype(CDT), do)
                passed_s[...] = through
                gsum_s[...] += before[:, W - 1:W] + dl[:, W - 1:W]

        _masked_or_not(flags, step)

        @pl.when(flags & LAST != 0)
        def _():
            dq_ref[0] = (dq_s[...] * ATT_SCALE).astype(dq_ref.dtype)

    whole = pl.BlockSpec((1, S, HD), lambda h, n, qi, kj, fl: (h, 0, 0))
    grid_spec = pltpu.PrefetchScalarGridSpec(
        num_scalar_prefetch=3, grid=(NH, int(qi.shape[0])), in_specs=[qblk, kblk, kblk, qblk, qcol],
        out_specs=(qblk, whole, whole),
        scratch_shapes=[pltpu.VMEM((tq, 1), F32), pltpu.VMEM((tq, 1), F32), pltpu.VMEM((tq, HD), F32)])
    return pl.pallas_call(body, name=name, grid_spec=grid_spec,
                          out_shape=(jax.ShapeDtypeStruct((NH, S, HD), CDT), jax.ShapeDtypeStruct((NH, S, HD), F32),
                                     jax.ShapeDtypeStruct((NH, S, HD), F32)),
                          compiler_params=_params(("parallel", "arbitrary")))(qi, kj, fl, q, k, v, do, lt)


def _mem_probs(q_ref, kv_ref, h):
    cols = slice(h * MHD, (h + 1) * MHD)
    s = _dot_nt(q_ref[:, cols], kv_ref[:, cols]) * MEM_SCALE
    e = jnp.exp(s - jnp.max(s, axis=-1, keepdims=True))
    return e / jnp.sum(e, axis=-1, keepdims=True)


def _xattn_fwd(q, kv, w_mo, x1, name, *, tm=512):
    S = q.shape[0]
    tm = _tile(S, tm, 16)
    nm = kv.shape[0]

    def body(q_ref, kv_ref, w_ref, x_ref, x2_ref, o_ref):
        for h in range(NMH):
            p = _mem_probs(q_ref, kv_ref, h)
            o_ref[:, h * MHD:(h + 1) * MHD] = _dot(p.astype(CDT), kv_ref[:, D + h * MHD:D + (h + 1) * MHD]).astype(CDT)
        x2_ref[...] = x_ref[...] + _dot(o_ref[...], w_ref[...])

    row = pl.BlockSpec((tm, D), lambda i: (i, 0))
    return pl.pallas_call(body, name=name, grid=(S // tm,),
                          in_specs=[row, pl.BlockSpec((nm, 2 * D), lambda i: (0, 0)), pl.BlockSpec((D, D), lambda i: (0, 0)), row],
                          out_specs=(row, row),
                          out_shape=(jax.ShapeDtypeStruct((S, D), F32), jax.ShapeDtypeStruct((S, D), CDT)),
                          compiler_params=_params(("parallel",)))(q, kv, w_mo, x1)


def _xattn_bwd(q, kv, do, name, *, tm=512):
    S = q.shape[0]
    tm = _tile(S, tm, 16)
    nm = kv.shape[0]

    def body(q_ref, kv_ref, do_ref, dq_ref, dkv_ref):
        @pl.when(pl.program_id(0) == 0)
        def _():
            dkv_ref[...] = jnp.zeros_like(dkv_ref)
        for h in range(NMH):
            cols = slice(h * MHD, (h + 1) * MHD)
            vcols = slice(D + h * MHD, D + (h + 1) * MHD)
            p = _mem_probs(q_ref, kv_ref, h)
            doh = do_ref[:, cols]
            dp = _dot_nt(doh, kv_ref[:, vcols])
            ds = (p * (dp - jnp.sum(p * dp, axis=-1, keepdims=True)) * MEM_SCALE).astype(CDT)
            dq_ref[:, cols] = _dot(ds, kv_ref[:, cols]).astype(CDT)
            dkv_ref[:, cols] += _dot_tn(ds, q_ref[:, cols])
            dkv_ref[:, vcols] += _dot_tn(p.astype(CDT), doh)

    row = pl.BlockSpec((tm, D), lambda i: (i, 0))
    kvs = pl.BlockSpec((nm, 2 * D), lambda i: (0, 0))
    return pl.pallas_call(body, name=name, grid=(S // tm,), in_specs=[row, kvs, row], out_specs=(row, kvs),
                          out_shape=(jax.ShapeDtypeStruct((S, D), CDT), jax.ShapeDtypeStruct((nm, 2 * D), F32)),
                          compiler_params=_params(("arbitrary",)))(q, kv, do)


HALO = 16


def _shift_down(u, prev, s):
    rolled = pltpu.roll(u, s, 0)
    r = lax.broadcasted_iota(jnp.int32, u.shape, 0)
    for t in range(s):
        rolled = jnp.where(r == t, prev[HALO - s + t:HALO - s + t + 1, :], rolled)
    return rolled


def _shift_up(u, nxt, s):
    n = u.shape[0]
    rolled = pltpu.roll(u, n - s, 0)
    r = lax.broadcasted_iota(jnp.int32, u.shape, 0)
    for t in range(s):
        rolled = jnp.where(r == n - s + t, nxt[t:t + 1, :], rolled)
    return rolled


def _conv_taps(u_ref, h_ref, first):
    u = u_ref[...].astype(F32)
    prev = jnp.where(first, 0.0, h_ref[...].astype(F32))
    out = []
    for half in range(2):
        out.append((u[half], _shift_down(u[half], prev[half], 1), _shift_down(u[half], prev[half], 2)))
    return out


def _conv_specs(tm, tn, nsb):
    blk = pl.BlockSpec((2, tm, tn), lambda j, i: (0, i, j))
    prev = pl.BlockSpec((2, HALO, tn), lambda j, i: (0, jnp.maximum(i * (tm // HALO) - 1, 0), j))
    nxt = pl.BlockSpec((2, HALO, tn), lambda j, i: (0, jnp.minimum((i + 1) * (tm // HALO), nsb - 1), j))
    w = pl.BlockSpec((2, 3, tn), lambda j, i: (0, 0, j))
    b = pl.BlockSpec((2, 1, tn), lambda j, i: (0, 0, j))
    return blk, prev, nxt, w, b


def _conv_apply(taps, w_ref, b_ref):
    ys = []
    for half in range(2):
        u, u1, u2 = taps[half]
        w = w_ref[half]
        ys.append(b_ref[half] + u2 * w[0:1, :] + u1 * w[1:2, :] + u * w[2:3, :])
    return ys


def _conv_act(u0, cw, cb, name, *, tm=512, tn=256):
    _, S, F = u0.shape
    tm = _tile(S, tm, HALO)
    tn = _tile(F, tn, 128)
    blk, prev, _, w, b = _conv_specs(tm, tn, S // HALO)

    def body(u_ref, h_ref, w_ref, b_ref, a_ref):
        yg, yv = _conv_apply(_conv_taps(u_ref, h_ref, pl.program_id(1) == 0), w_ref, b_ref)
        a_ref[...] = (yg * jax.nn.sigmoid(yg) * yv).astype(a_ref.dtype)

    return pl.pallas_call(body, name=name, grid=(F // tn, S // tm), in_specs=[blk, prev, w, b],
                          out_specs=pl.BlockSpec((tm, tn), lambda j, i: (i, j)),
                          out_shape=jax.ShapeDtypeStruct((S, F), CDT),
                          compiler_params=_params(("parallel", "parallel")))(u0, u0, cw, cb)


def _conv_act_bwd(u0, da, cw, cb, name, *, tm=512, tn=256):
    _, S, F = u0.shape
    tm = _tile(S, tm, HALO)
    tn = _tile(F, tn, 128)
    blk, prev, _, w, b = _conv_specs(tm, tn, S // HALO)

    def body(u_ref, h_ref, da_ref, w_ref, b_ref, du_ref, dwb_ref):
        @pl.when(pl.program_id(1) == 0)
        def _():
            dwb_ref[...] = jnp.zeros_like(dwb_ref)
        taps = _conv_taps(u_ref, h_ref, pl.program_id(1) == 0)
        yg, yv = _conv_apply(taps, w_ref, b_ref)
        sg = jax.nn.sigmoid(yg)
        da = da_ref[...].astype(F32)
        dus = (da * yv * sg * (1.0 + yg * (1.0 - sg)), da * yg * sg)
        for half in range(2):
            du = dus[half]
            du_ref[half] = du.astype(du_ref.dtype)
            u, u1, u2 = taps[half]
            for row, term in enumerate((du * u2, du * u1, du * u, du)):
                dwb_ref[half, row:row + 1, :] += jnp.sum(term, axis=0, keepdims=True)

    return pl.pallas_call(body, name=name, grid=(F // tn, S // tm),
                          in_specs=[blk, prev, pl.BlockSpec((tm, tn), lambda j, i: (i, j)), w, b],
                          out_specs=(blk, pl.BlockSpec((2, 4, tn), lambda j, i: (0, 0, j))),
                          out_shape=(jax.ShapeDtypeStruct((2, S, F), CDT), jax.ShapeDtypeStruct((2, 4, F), F32)),
                          compiler_params=_params(("parallel", "arbitrary")))(u0, u0, da, cw, cb)


def _conv_bwd_input(du, cw, name, *, tm=512, tn=256):
    _, S, F = du.shape
    tm = _tile(S, tm, HALO)
    tn = _tile(F, tn, 128)
    blk, _, nxt, w, _ = _conv_specs(tm, tn, S // HALO)
    ni = S // tm

    def body(d_ref, h_ref, w_ref, o_ref):
        d = d_ref[...].astype(F32)
        nx = jnp.where(pl.program_id(1) == ni - 1, 0.0, h_ref[...].astype(F32))
        for half in range(2):
            wv = w_ref[half]
            y = d[half] * wv[2:3, :] + _shift_up(d[half], nx[half], 1) * wv[1:2, :] + _shift_up(d[half], nx[half], 2) * wv[0:1, :]
            o_ref[half] = y.astype(o_ref.dtype)

    return pl.pallas_call(body, name=name, grid=(F // tn, ni), in_specs=[blk, nxt, w], out_specs=blk,
                          out_shape=jax.ShapeDtypeStruct((2, S, F), CDT),
                          compiler_params=_params(("parallel", "parallel")))(du, du, cw)


ANY = pl.BlockSpec(memory_space=pl.ANY)


def _place():
    return lax.axis_index("x"), lax.axis_index("y"), lax.axis_index("c")


def _other_chips(x, y):
    return ((1 - x, y), (x, 1 - y), (1 - x, 1 - y))


def _gather_weights(wsh):
    rows, half = wsh.shape[0], wsh.shape[0] // 2

    def body(w_ref, out_ref, send_sems, recv_sems, local_sem):
        x, y, c = _place()
        chips = _other_chips(x, y)

        def part(chip, pc):
            return out_ref.at[2 * chip[0] + chip[1], pl.ds(pl.multiple_of(pc * half, 16), half), :]

        def copy(k, chip, pc, to, src=None):
            return pltpu.make_async_remote_copy(src_ref=part(chip, pc) if src is None else src, dst_ref=part(chip, pc),
                                                send_sem=send_sems.at[k], recv_sem=recv_sems.at[k],
                                                device_id=to, device_id_type=MESH)

        mine = pltpu.make_async_copy(w_ref, out_ref.at[2 * x + y], local_sem)
        mine.start()
        my_half = w_ref.at[pl.ds(pl.multiple_of(c * half, 16), half), :]
        first = [copy(j, (x, y), c, (*chip, c), src=my_half) for j, chip in enumerate(chips)]
        for cp in first:
            cp.start()
        passed = [copy(3 + j, chip, c, (x, y, 1 - c)) for j, chip in enumerate(chips)]
        for j, chip in enumerate(chips):
            copy(j, chip, c, (x, y, c)).wait_recv()
            passed[j].start()
        for j, chip in enumerate(chips):
            copy(3 + j, chip, 1 - c, (x, y, c)).wait_recv()
        for cp in first + passed:
            cp.wait_send()
        mine.wait()

    return pl.pallas_call(body, name="gather_weights", in_specs=[ANY], out_specs=ANY,
                          out_shape=jax.ShapeDtypeStruct((N_CHIP, rows, 128), wsh.dtype),
                          scratch_shapes=[pltpu.SemaphoreType.DMA((6,)), pltpu.SemaphoreType.DMA((6,)), pltpu.SemaphoreType.DMA])(wsh)


def _gather_small(v):
    m = v.shape[0]

    def body(v_ref, out_ref, send_sems, recv_sems, local_sem):
        x, y, c = _place()
        me, sibling = (x, y, c), (x, y, 1 - c)
        chips = _other_chips(x, y)

        def rows(px, py, pc):
            return out_ref.at[pl.ds((4 * px + 2 * py + pc) * m, m), :]

        def copy(k, block, to, src=None):
            return pltpu.make_async_remote_copy(src_ref=rows(*block) if src is None else src, dst_ref=rows(*block),
                                                send_sem=send_sems.at[k], recv_sem=recv_sems.at[k],
                                                device_id=to, device_id_type=MESH)

        mine = pltpu.make_async_copy(v_ref, rows(*me), local_sem)
        mine.start()
        first = [copy(0, me, sibling, src=v_ref)]
        first += [copy(1 + j, me, (*chip, c), src=v_ref) for j, chip in enumerate(chips)]
        for cp in first:
            cp.start()
        passed = [copy(4 + j, (*chip, c), sibling) for j, chip in enumerate(chips)]
        for j, chip in enumerate(chips):
            copy(1 + j, (*chip, c), me).wait_recv()
            passed[j].start()
        copy(0, sibling, me).wait_recv()
        for j, chip in enumerate(chips):
            copy(4 + j, (*chip, 1 - c), me).wait_recv()
        for cp in first + passed:
            cp.wait_send()
        mine.wait()

    vm = pl.BlockSpec(memory_space=pltpu.VMEM)
    return pl.pallas_call(body, name="gather_small", in_specs=[vm], out_specs=vm,
                          out_shape=jax.ShapeDtypeStruct((8 * m, 128), v.dtype),
                          scratch_shapes=[pltpu.SemaphoreType.DMA((7,)), pltpu.SemaphoreType.DMA((7,)), pltpu.SemaphoreType.DMA])(v)


def _swap_halves(g):
    n, rows, _ = g.shape
    half = rows // 2

    def body(g_ref, out_ref, send_sem, recv_sem):
        x, y, c = _place()
        src = g_ref.at[:, pl.ds(pl.multiple_of((1 - c) * half, 8), half), :]
        cp = pltpu.make_async_remote_copy(src_ref=src, dst_ref=out_ref, send_sem=send_sem, recv_sem=recv_sem,
                                          device_id=(x, y, 1 - c), device_id_type=MESH)
        cp.start()
        cp.wait()

    return pl.pallas_call(body, name="swap_halves", in_specs=[ANY], out_specs=ANY,
                          out_shape=jax.ShapeDtypeStruct((n, half, 128), g.dtype),
                          scratch_shapes=[pltpu.SemaphoreType.DMA, pltpu.SemaphoreType.DMA])(g)


def _scatter_chips(hsum):
    n, half, _ = hsum.shape

    def body(h_ref, out_ref, send_sems, recv_sems):
        x, y, c = _place()
        cps = []
        for j, chip in enumerate(_other_chips(x, y)):
            cp = pltpu.make_async_remote_copy(src_ref=h_ref.at[2 * chip[0] + chip[1]], dst_ref=out_ref.at[j],
                                              send_sem=send_sems.at[j], recv_sem=recv_sems.at[j],
                                              device_id=(*chip, c), device_id_type=MESH)
            cp.start()
            cps.append(cp)
        for cp in cps:
            cp.wait()

    return pl.pallas_call(body, name="scatter_chips", in_specs=[ANY], out_specs=ANY,
                          out_shape=jax.ShapeDtypeStruct((3, half, 128), hsum.dtype),
                          scratch_shapes=[pltpu.SemaphoreType.DMA((3,)), pltpu.SemaphoreType.DMA((3,))])(hsum)


def _join_halves(gh):
    half = gh.shape[0]

    def body(g_ref, out_ref, send_sem, recv_sem, local_sem):
        x, y, c = _place()
        dst = out_ref.at[pl.ds(pl.multiple_of(c * half, 8), half), :]
        mine = pltpu.make_async_copy(g_ref, dst, local_sem)
        mine.start()
        cp = pltpu.make_async_remote_copy(src_ref=g_ref, dst_ref=dst, send_sem=send_sem, recv_sem=recv_sem,
                                          device_id=(x, y, 1 - c), device_id_type=MESH)
        cp.start()
        cp.wait_send()
        other = out_ref.at[pl.ds(pl.multiple_of((1 - c) * half, 8), half), :]
        pltpu.make_async_remote_copy(src_ref=g_ref, dst_ref=other, send_sem=send_sem, recv_sem=recv_sem,
                                     device_id=(x, y, 1 - c), device_id_type=MESH).wait_recv()
        mine.wait()

    return pl.pallas_call(body, name="join_halves", in_specs=[ANY], out_specs=ANY,
                          out_shape=jax.ShapeDtypeStruct((2 * half, 128), gh.dtype),
                          scratch_shapes=[pltpu.SemaphoreType.DMA, pltpu.SemaphoreType.DMA, pltpu.SemaphoreType.DMA])(gh)


def _add_sibling(g, recv, c_idx, name):
    n, rows, _ = g.shape
    half = rows // 2
    tr = _tile(half, ADAM_ROWS, 16)
    nb = half // tr

    def body(c_ref, g_ref, r_ref, o_ref, ob_ref):
        s = g_ref[...] + r_ref[...]
        o_ref[...] = s
        ob_ref[...] = s.astype(CDT)

    out = pl.BlockSpec((None, tr, 128), lambda k, i, c: (k, i, 0))
    grid_spec = pltpu.PrefetchScalarGridSpec(
        num_scalar_prefetch=1, grid=(n, nb),
        in_specs=[pl.BlockSpec((None, tr, 128), lambda k, i, c: (k, c[0] * nb + i, 0)), out],
        out_specs=(out, out))
    return pl.pallas_call(body, name=name, grid_spec=grid_spec,
                          out_shape=(jax.ShapeDtypeStruct((n, half, 128), F32), jax.ShapeDtypeStruct((n, half, 128), CDT)),
                          compiler_params=_params(("parallel", "parallel")))(c_idx, g, recv)


def _add_chips(hsum, recv, chip_idx, name):
    n, half, _ = hsum.shape
    tr = _tile(half, ADAM_ROWS, 16)

    def body(k_ref, h_ref, r_ref, o_ref):
        o_ref[...] = ((h_ref[...] + r_ref[0].astype(F32)) + r_ref[1].astype(F32)) + r_ref[2].astype(F32)

    grid_spec = pltpu.PrefetchScalarGridSpec(
        num_scalar_prefetch=1, grid=(half // tr,),
        in_specs=[pl.BlockSpec((None, tr, 128), lambda i, k: (k[0], i, 0)),
                  pl.BlockSpec((3, tr, 128), lambda i, k: (0, i, 0))],
        out_specs=pl.BlockSpec((tr, 128), lambda i, k: (i, 0)))
    return pl.pallas_call(body, name=name, grid_spec=grid_spec, out_shape=jax.ShapeDtypeStruct((half, 128), F32),
                          compiler_params=_params(("parallel",)))(chip_idx, hsum, recv)


def _adamw_math(g, w, m, v):
    m2 = B1 * m + (1.0 - B1) * g
    v2 = B2 * v + (1.0 - B2) * (g * g)
    delta = -LR * ((m2 / BC1) / (jnp.sqrt(v2 / BC2) + AEPS) + WD * w)
    return delta, m2, v2


def _adamw(g, w, m, v, name):
    rows, cols = g.shape
    tr = _tile(rows, max(8, (ADAM_ROWS * 128 // cols) // 8 * 8), 8)

    def body(g_ref, w_ref, m_ref, v_ref, d_ref, m2_ref, v2_ref):
        d_ref[...], m2_ref[...], v2_ref[...] = _adamw_math(g_ref[...], w_ref[...], m_ref[...], v_ref[...])

    blk = pl.BlockSpec((tr, cols), lambda i: (i, 0))
    shp = jax.ShapeDtypeStruct((rows, cols), F32)
    return pl.pallas_call(body, name=name, grid=(rows // tr,), in_specs=[blk] * 4, out_specs=(blk,) * 3,
                          out_shape=(shp,) * 3, compiler_params=_params(("parallel",)))(g, w, m, v)


def _adamw_small(parts, w, m, v, name):
    rows = w.shape[0]

    def body(p_ref, w_ref, m_ref, v_ref, g_ref, d_ref, m2_ref, v2_ref):
        g = p_ref[0]
        for k in range(1, 8):
            g = g + p_ref[k]
        g_ref[...] = g
        d_ref[...], m2_ref[...], v2_ref[...] = _adamw_math(g, w_ref[...], m_ref[...], v_ref[...])

    shp = jax.ShapeDtypeStruct((rows, 128), F32)
    return pl.pallas_call(body, name=name, out_shape=(shp,) * 4)(parts, w, m, v)


def _pack_rows(parts, rows):
    flat = jnp.concatenate([p.reshape(-1) for p in parts])
    return jnp.pad(flat, (0, rows * 128 - flat.shape[0])).reshape(rows, 128)


def _unpack(flat, sizes, shapes):
    out, off = [], 0
    for n, s in zip(sizes, shapes):
        out.append(flat[off:off + n].reshape(s))
        off += n
    return out


def _to_shards(full, shard_shape, axis):
    if axis == 0:
        return full.reshape(N_CHIP, -1)
    r, cs = shard_shape
    return full.reshape(r, N_CHIP, cs).transpose(1, 0, 2).reshape(N_CHIP, -1)


def _from_shards(sh, shard_shape, axis):
    r, cs = shard_shape
    if axis == 0:
        return sh.reshape(N_CHIP * r, cs)
    return sh.reshape(N_CHIP, r, cs).transpose(1, 0, 2).reshape(r, N_CHIP * cs)


def _heads(t, n):
    s = t.shape[0]
    return t.reshape(s, n * NH, HD).transpose(1, 0, 2)


def _merge(t):
    return t.transpose(1, 0, 2).reshape(t.shape[1], GW)


def _local_step(x0, mem, tgt, W, gains):
    S = x0.shape[0]
    w_in = jnp.pad(W["w_in"], ((0, 0), (0, IN_PAD - IN_COLS)))
    b_f = jnp.pad(gains["b_forget"], ((0, 0), (0, 128 - NH)))
    cw = W["conv_w"].reshape(3, 2, DFF).transpose(1, 0, 2)
    cb = gains["conv_b"].reshape(2, 1, DFF)

    h1 = _rms_cast(x0, gains["attn_norm_g"], "norm_attn")
    qkv = _heads(_mm_nn(h1, w_in[:, :NQKV], CDT, "proj_qkv"), 6)
    fl = _mm_nn(h1, w_in[:, NQKV:], F32, "proj_gate")
    cum = _gate_fwd(fl, b_f, "gate_cumsum")
    c_hm = cum[:, :NH].T
    cq, ck = c_hm[:, :, None], c_hm[:, None, :]
    fq, fk, fv, sq, sk, sv = (qkv[n * NH:(n + 1) * NH] for n in range(6))
    fo_h, lse = _fox_fwd(fq, fk, fv, cq, ck, "fox_fwd")
    so_h, s_lt = _sb_fwd(sq, sk, sv, "sb_fwd")
    fo, so = _merge(fo_h), _merge(so_h)
    x1, mixed = _out_proj(fo, so, gains["fox_out_g"], gains["sb_out_g"], W["w_out"], x0, "out_proj")

    h2 = _rms_cast(x1, gains["xattn_norm_g"], "norm_xattn")
    mn = _rms_cast(mem, gains["mem_norm_g"], "norm_mem")
    mq = _mm_nn(h2, W["w_mq"], CDT, "proj_mq")
    kv = _mm_nn(mn, W["w_mkv"], CDT, "proj_mkv")
    x2, mo = _xattn_fwd(mq, kv, W["w_mo"], x1, "xattn_fwd")

    h3 = _rms_cast(x2, gains["ffn_norm_g"], "norm_ffn")
    u0 = _mm_nn(h3, W["w_up"], CDT, "ffn_up", halves=True)
    act = _conv_act(u0, cw, cb, "conv_act")
    x3 = _mm_nn(act, W["w_down"], F32, "ffn_down", tm=512, residual=x2)
    loss, dx3, dg_final = _loss_bwd(x3, tgt, gains["final_norm_g"].reshape(1, D), "loss")

    gw, gs = {}, {"final_norm_g": dg_final}
    da = _mm_nt(dx3, W["w_down"], "ffn_down_dx", tn=1408, out_dtype=CDT)
    gw["w_down"] = _mm_tn(act, dx3, "ffn_down_dw", tka=1408)
    du, dwb = _conv_act_bwd(u0, da, cw, cb, "conv_act_bwd")
    gw["conv_w"] = dwb[:, :3].transpose(1, 0, 2).reshape(3, 2 * DFF)
    gs["conv_b"] = dwb[:, 3].reshape(1, 2 * DFF)
    du0 = _conv_bwd_input(du, cw, "conv_bwd_input")
    gw["w_up"] = _mm_tn(h3, du0, "ffn_up_dw", tn=1408, b_halves=True)
    dx2, gs["ffn_norm_g"] = _mm_nt_rmsbwd(du0, W["w_up"], x2, gains["ffn_norm_g"], dx3, "ffn_up_dx", tk=1408, a_halves=True)

    dmo = _mm_nt(dx2, W["w_mo"], "mo_dx", tn=512, out_dtype=CDT)
    gw["w_mo"] = _mm_tn(mo, dx2, "mo_dw")
    dmq, dkv = _xattn_bwd(mq, kv, dmo, "xattn_bwd")
    gw["w_mq"] = _mm_tn(h2, dmq, "mq_dw")
    dx1, gs["xattn_norm_g"] = _mm_nt_rmsbwd(dmq, W["w_mq"], x1, gains["xattn_norm_g"], dx2, "mq_dx")
    gw["w_mkv"] = _mm_tn(mn, dkv, "mkv_dw")
    _, gs["mem_norm_g"] = _mm_nt_rmsbwd(dkv, W["w_mkv"], mem, gains["mem_norm_g"], jnp.zeros_like(mem), "mkv_dx")

    gw["w_out"] = _mm_tn(mixed, dx1, "out_dw")
    dfo, dso, gs["fox_out_g"], gs["sb_out_g"] = _out_proj_bwd(dx1, W["w_out"], fo, so, gains["fox_out_g"], gains["sb_out_g"], "out_dx")
    dfo_h, dso_h = _heads(dfo, 1), _heads(dso, 1)
    dfq, dfk, dfv, dck, dcq = _fox_bwd(fq, fk, fv, cq, ck, fo_h, dfo_h, lse, "fox_bwd")
    dsq, dsk, dsv = _sb_bwd(sq, sk, sv, dso_h, s_lt, "sb_bwd")
    dc = jnp.pad((dck[:, 0, :] + dcq[:, :, 0]).T, ((0, 0), (0, 128 - NH)))
    dfl, db = _gate_bwd(dc, fl, b_f, "gate_bwd")
    gs["b_forget"] = db[:, :NH]
    dqkv = jnp.concatenate([dfq, dfk.astype(CDT), dfv.astype(CDT), dsq, dsk.astype(CDT), dsv.astype(CDT)], axis=0)
    dproj = jnp.concatenate([dqkv.transpose(1, 0, 2).reshape(S, NQKV), dfl.astype(CDT)], axis=1)
    gw["w_in"] = _mm_tn(h1, dproj, "in_dw", tn=640)[:, :IN_COLS]
    dx0, gs["attn_norm_g"] = _mm_nt_rmsbwd(dproj, w_in, x0, gains["attn_norm_g"], dx1, "in_dx", tk=640)
    return loss, dx0, gw, gs


NAMES = ("attn_norm_g", "w_in", "b_forget", "fox_out_g", "sb_out_g", "w_out", "xattn_norm_g", "mem_norm_g", "w_mq",
         "w_mkv", "w_mo", "ffn_norm_g", "w_up", "conv_w", "conv_b", "w_down", "final_norm_g")


def _step(x, mem, loss_target, w, m, v):
    xi, yi, ci = _place()
    big_shapes = [s for _, s, _ in BIG]

    parts = []
    for name, shape, _ in BIG:
        blk = w[name].reshape(shape)
        parts.append(lax.bitcast_convert_type(blk, CDT) if name == "conv_w" else blk.astype(CDT))
    gathered = _gather_weights(_pack_rows(parts, ROWS_G)).reshape(N_CHIP, ROWS_G * 128)
    W, off = {}, 0
    for (name, shape, axis), n in zip(BIG, GATHER_SIZES):
        sh = gathered[:, off:off + n]
        off += n
        if name == "conv_w":
            sh = lax.bitcast_convert_type(sh.reshape(N_CHIP, n // 2, 2), F32)
        W[name] = _from_shards(sh, shape, axis)
    gains = {name: w[name].reshape(1, -1) for name, _ in SMALL}

    loss, grad_x, gw, gs = _local_step(x[0], mem[0], loss_target[0], W, gains)

    g_flat = jnp.concatenate([_to_shards(gw[name], shape, axis) for name, shape, axis in BIG], axis=1)
    g_flat = jnp.pad(g_flat, ((0, 0), (0, ROWS_F * 128 - P_BIG))).reshape(N_CHIP, ROWS_F, 128)
    pair_sum, pair_sum_b = _add_sibling(g_flat, _swap_halves(g_flat), jnp.reshape(ci, (1,)).astype(jnp.int32), "add_sibling")
    g_half = _add_chips(pair_sum, _scatter_chips(pair_sum_b), jnp.reshape(2 * xi + yi, (1,)).astype(jnp.int32), "add_chips")
    g_big = _join_halves(g_half)
    small = jnp.concatenate([gs[name].reshape(-1) for name, _ in SMALL] + [loss[0, :1]])
    small = jnp.pad(small, (0, ROWS_S * 128 - P_SMALL)).reshape(ROWS_S, 128)
    small_parts = _gather_small(small).reshape(8, ROWS_S, 128)

    def flat_small(d):
        return _pack_rows([d[name] for name, _ in SMALL], ROWS_S)

    outs = {}
    for (name, shape, _), g in zip(BIG, _unpack(g_big.reshape(-1), BIG_SIZES, big_shapes)):
        res = _adamw(g, w[name].reshape(shape), m[name].reshape(shape), v[name].reshape(shape), "adamw_" + name)
        for prefix, arr in zip(("grad_", "delta_", "new_m_", "new_v_"), (g, *res)):
            outs[prefix + name] = arr.reshape(w[name].shape)
    small_res = _adamw_small(small_parts, flat_small(w), flat_small(m), flat_small(v), "adamw_small")
    g_sm = small_res[0]
    for prefix, sm in zip(("grad_", "delta_", "new_m_", "new_v_"), small_res):
        for (name, n), arr in zip(SMALL, _unpack(sm.reshape(-1), [n for _, n in SMALL], [(n,) for _, n in SMALL])):
            outs[prefix + name] = arr.reshape(w[name].shape)
    total_loss = g_sm.reshape(-1)[P_SMALL - 1]
    return (total_loss, grad_x[None], *[outs[p + n] for p in ("grad_", "delta_", "new_m_", "new_v_") for n in NAMES])


def kernel(x, mem, attn_norm_g, w_in, b_forget, fox_out_g, sb_out_g, w_out, xattn_norm_g, mem_norm_g, w_mq, w_mkv, w_mo, ffn_norm_g, w_up, conv_w, conv_b, w_down, final_norm_g, loss_target, m_attn_norm_g, m_w_in, m_b_forget, m_fox_out_g, m_sb_out_g, m_w_out, m_xattn_norm_g, m_mem_norm_g, m_w_mq, m_w_mkv, m_w_mo, m_ffn_norm_g, m_w_up, m_conv_w, m_conv_b, m_w_down, m_final_norm_g, v_attn_norm_g, v_w_in, v_b_forget, v_fox_out_g, v_sb_out_g, v_w_out, v_xattn_norm_g, v_mem_norm_g, v_w_mq, v_w_mkv, v_w_mo, v_ffn_norm_g, v_w_up, v_conv_w, v_conv_b, v_w_down, v_final_norm_g):
    given = dict(locals())
    w = {n: given[n] for n in NAMES}
    m = {n: given["m_" + n] for n in NAMES}
    v = {n: given["v_" + n] for n in NAMES}
    return _step(x, mem, loss_target, w, m, v)
```

```python
import functools

import numpy as np
import jax
import jax.numpy as jnp
from jax import lax
from jax.experimental import pallas as pl
from jax.experimental.pallas import tpu as pltpu

F32 = jnp.float32
CDT = jnp.bfloat16
MESH = pl.DeviceIdType.MESH

D = 1024
HD = 64
NH = 8
GW = NH * HD
NQKV = 6 * GW
IN_COLS = NQKV + NH
IN_PAD = NQKV + 256
NMH = 4
MHD = D // NMH
DFF = 2816
EPS = 1e-6
ATT_SCALE = HD ** -0.5
MEM_SCALE = MHD ** -0.5
NEG = -1e30

LR, B1, B2, AEPS, WD, STEP = 0.001, 0.9, 0.999, 1e-08, 0.01, 10
BC1 = 1.0 - B1 ** STEP
BC2 = 1.0 - B2 ** STEP

ATT_TILES = {"fox_fwd": (512, 1024), "fox_bwd": (512, 1024), "sb_fwd": (512, 1024), "sb_bwd": (1024, 1024)}
W_SB = 256
VMEM_LIMIT = 52 * 2 ** 20

N_CHIP = 4
BIG = (("w_in", (D, IN_COLS // N_CHIP), 1), ("w_out", (D // N_CHIP, D), 0), ("w_mq", (D // N_CHIP, D), 0),
       ("w_mkv", (D, 2 * D // N_CHIP), 1), ("w_mo", (D // N_CHIP, D), 0), ("w_up", (D, 2 * DFF // N_CHIP), 1),
       ("conv_w", (3, 2 * DFF // N_CHIP), 1), ("w_down", (DFF // N_CHIP, D), 0))
BIG_SIZES = tuple(int(np.prod(s)) for _, s, _ in BIG)
P_BIG = sum(BIG_SIZES)
ROWS_F = 33 * 1024
assert ROWS_F * 128 >= P_BIG
HALF_F = ROWS_F // 2
ADAM_ROWS = 1536
GATHER_SIZES = tuple(2 * n if name == "conv_w" else n for (name, _, _), n in zip(BIG, BIG_SIZES))
ROWS_G = -(-sum(GATHER_SIZES) // 4096) * 32
HALF_G = ROWS_G // 2
SMALL = (("attn_norm_g", 1024), ("b_forget", 8), ("fox_out_g", 512), ("sb_out_g", 512), ("xattn_norm_g", 1024),
         ("mem_norm_g", 1024), ("ffn_norm_g", 1024), ("conv_b", 2 * DFF), ("final_norm_g", 1024))
P_SMALL = sum(n for _, n in SMALL) + 1
ROWS_S = -(-P_SMALL // 1024) * 8


def _params(sem=None, vmem=VMEM_LIMIT):
    return pltpu.CompilerParams(dimension_semantics=sem, vmem_limit_bytes=vmem)


def _tile(n, pref, mult):
    t = (min(pref, n) // mult) * mult
    while t >= mult:
        if n % t == 0:
            return t
        t -= mult
    return n


def _dot(a, b):
    return jnp.dot(a, b, preferred_element_type=F32)


def _dot_nt(a, b):
    return lax.dot_general(a, b, (((1,), (1,)), ((), ())), preferred_element_type=F32)


def _dot_tn(a, b):
    return lax.dot_general(a, b, (((0,), (0,)), ((), ())), preferred_element_type=F32)


def _split3(x):
    h1 = x.astype(CDT)
    r1 = x - h1.astype(F32)
    h2 = r1.astype(CDT)
    h3 = (r1 - h2.astype(F32)).astype(CDT)
    return h1, h2, h3


def _split2(x):
    h1 = x.astype(CDT)
    return h1, (x - h1.astype(F32)).astype(CDT)


def _rms_bwd(dh, x, g):
    r = lax.rsqrt(jnp.mean(x * x, axis=-1, keepdims=True) + EPS)
    xn = x * r
    dg = jnp.sum(dh * xn, axis=0, keepdims=True)
    dhg = dh * g
    dx = r * (dhg - xn * jnp.mean(dhg * xn, axis=-1, keepdims=True))
    return dx, dg


def _mm_nn(a, b, out_dtype, name, *, tm=1024, tn=512, residual=None, halves=False):
    M, K = a.shape
    N = b.shape[1]
    tm = _tile(M, tm, 16)
    tn = _tile(N // 2 if halves else N, tn, 128)
    nj = N // tn

    def body(*refs):
        a_ref, b_ref = refs[0], refs[1]
        o_ref = refs[-1]
        acc = _dot(a_ref[...].astype(CDT), b_ref[...].astype(CDT))
        if residual is not None:
            acc = acc + refs[2][...]
        o_ref[...] = acc.astype(o_ref.dtype)

    in_specs = [pl.BlockSpec((tm, K), lambda i, j: (i, 0)), pl.BlockSpec((K, tn), lambda i, j: (0, j))]
    ops = [a, b]
    if residual is not None:
        in_specs.append(pl.BlockSpec((tm, tn), lambda i, j: (i, j)))
        ops.append(residual)
    if halves:
        njh = nj // 2
        out_shape = jax.ShapeDtypeStruct((2, M, N // 2), out_dtype)
        out_spec = pl.BlockSpec((None, tm, tn), lambda i, j: (j // njh, i, j % njh))
    else:
        out_shape = jax.ShapeDtypeStruct((M, N), out_dtype)
        out_spec = pl.BlockSpec((tm, tn), lambda i, j: (i, j))
    return pl.pallas_call(body, name=name, grid=(M // tm, nj), in_specs=in_specs, out_specs=out_spec,
                          out_shape=out_shape, compiler_params=_params(("parallel", "parallel")))(*ops)


def _mm_tn(a, b, name, *, tka=512, tn=1024, ts=512, b_halves=False):
    S, Ka = a.shape
    N = 2 * b.shape[2] if b_halves else b.shape[1]
    tka = _tile(Ka, tka, 128)
    tn = _tile(N // 2 if b_halves else N, tn, 128)
    ts = _tile(S, ts, 16)
    nn = N // tn

    def body(a_ref, b_ref, o_ref):
        @pl.when(pl.program_id(2) == 0)
        def _():
            o_ref[...] = jnp.zeros_like(o_ref)
        o_ref[...] += _dot_tn(a_ref[...].astype(CDT), b_ref[...].astype(CDT))

    if b_halves:
        nnh = nn // 2
        b_spec = pl.BlockSpec((None, ts, tn), lambda i, j, s: (j // nnh, s, j % nnh))
    else:
        b_spec = pl.BlockSpec((ts, tn), lambda i, j, s: (s, j))
    return pl.pallas_call(
        body, name=name, grid=(Ka // tka, nn, S // ts),
        in_specs=[pl.BlockSpec((ts, tka), lambda i, j, s: (s, i)), b_spec],
        out_specs=pl.BlockSpec((tka, tn), lambda i, j, s: (i, j)),
        out_shape=jax.ShapeDtypeStruct((Ka, N), F32),
        compiler_params=_params(("parallel", "parallel", "arbitrary")))(a, b)


def _mm_nt(a, b, name, *, tm=512, tn=None, tk=None, a_halves=False, out_dtype=F32,
           epilogue=None, extra=(), extra_specs=(), out_shape=None, out_specs=None):
    if a_halves:
        M, K = a.shape[1], 2 * a.shape[2]
    else:
        M, K = a.shape
    N = b.shape[0]
    tm = _tile(M, tm, 16)
    tn = N if (epilogue is not None or tn is None) else _tile(N, tn, 128)
    tk = K if tk is None else _tile(K // 2 if a_halves else K, tk, 128)
    nk = K // tk
    n_extra = len(extra)

    def body(*refs):
        a_ref, b_ref = refs[0], refs[1]
        extra_refs = refs[2:2 + n_extra]
        out_refs = refs[2 + n_extra:-1]
        acc_ref = refs[-1]
        k = pl.program_id(2)

        @pl.when(k == 0)
        def _():
            acc_ref[...] = jnp.zeros_like(acc_ref)
        acc_ref[...] += _dot_nt(a_ref[...].astype(CDT), b_ref[...].astype(CDT))

        @pl.when(k == nk - 1)
        def _():
            if epilogue is None:
                out_refs[0][...] = acc_ref[...].astype(out_refs[0].dtype)
            else:
                epilogue(acc_ref[...], pl.program_id(0), extra_refs, out_refs)

    if a_halves:
        nkh = nk // 2
        a_spec = pl.BlockSpec((None, tm, tk), lambda i, j, k: (k // nkh, i, k % nkh))
    else:
        a_spec = pl.BlockSpec((tm, tk), lambda i, j, k: (i, k))
    if epilogue is None:
        out_shape = jax.ShapeDtypeStruct((M, N), out_dtype)
        out_specs = pl.BlockSpec((tm, tn), lambda i, j, k: (i, j))
        sem = ("parallel", "parallel", "arbitrary")
    else:
        sem = ("arbitrary", "arbitrary", "arbitrary")
    return pl.pallas_call(
        body, name=name, grid=(M // tm, N // tn, nk),
        in_specs=[a_spec, pl.BlockSpec((tn, tk), lambda i, j, k: (j, k)), *extra_specs],
        out_specs=out_specs, out_shape=out_shape,
        scratch_shapes=[pltpu.VMEM((tm, tn), F32)],
        compiler_params=_params(sem))(a, b, *extra)


def _mm_nt_rmsbwd(a, b, x, g, dres, name, *, tm=512, tk=None, a_halves=False):
    M = x.shape[0]
    tm = _tile(M, tm, 16)

    def epilogue(acc, i, extra_refs, out_refs):
        x_ref, g_ref, r_ref = extra_refs
        dx_ref, dg_ref = out_refs
        dx, dg = _rms_bwd(acc, x_ref[...], g_ref[...])
        dx_ref[...] = r_ref[...] + dx

        @pl.when(i == 0)
        def _():
            dg_ref[...] = jnp.zeros_like(dg_ref)
        dg_ref[...] += dg

    row = pl.BlockSpec((tm, D), lambda i, j, k: (i, 0))
    vec = pl.BlockSpec((1, D), lambda i, j, k: (0, 0))
    return _mm_nt(a, b, name, tm=tm, tk=tk, a_halves=a_halves, epilogue=epilogue,
                  extra=(x, g, dres), extra_specs=(row, vec, row),
                  out_shape=(jax.ShapeDtypeStruct((M, D), F32), jax.ShapeDtypeStruct((1, D), F32)),
                  out_specs=(row, vec))


def _rms_cast(x, g, name, *, tm=512):
    M, W = x.shape
    tm = _tile(M, tm, 16)

    def body(x_ref, g_ref, o_ref):
        xf = x_ref[...]
        r = lax.rsqrt(jnp.mean(xf * xf, axis=-1, keepdims=True) + EPS)
        o_ref[...] = (xf * r * g_ref[...]).astype(o_ref.dtype)

    return pl.pallas_call(body, name=name, grid=(M // tm,),
                          in_specs=[pl.BlockSpec((tm, W), lambda i: (i, 0)), pl.BlockSpec((1, W), lambda i: (0, 0))],
                          out_specs=pl.BlockSpec((tm, W), lambda i: (i, 0)),
                          out_shape=jax.ShapeDtypeStruct((M, W), CDT),
                          compiler_params=_params(("parallel",)))(x, g)


def _tri(n, lower):
    r = lax.broadcasted_iota(jnp.int32, (n, n), 0)
    c = lax.broadcasted_iota(jnp.int32, (n, n), 1)
    return (c <= r if lower else c >= r).astype(CDT)


def _gate_fwd(fl, b, name, *, tm=512):
    S = fl.shape[0]
    tm = _tile(S, tm, 16)

    def body(f_ref, b_ref, c_ref, carry):
        @pl.when(pl.program_id(0) == 0)
        def _():
            carry[...] = jnp.zeros_like(carry)
        z = f_ref[...] + b_ref[...]
        lf = jnp.minimum(z, 0.0) - jnp.log(1.0 + jnp.exp(-jnp.abs(z)))
        tri = _tri(tm, True)
        cum = sum(_dot(tri, p) for p in _split3(lf)) + carry[...]
        c_ref[...] = cum
        carry[...] = cum[tm - 1:tm, :]

    return pl.pallas_call(body, name=name, grid=(S // tm,),
                          in_specs=[pl.BlockSpec((tm, 128), lambda i: (i, 0)), pl.BlockSpec((1, 128), lambda i: (0, 0))],
                          out_specs=pl.BlockSpec((tm, 128), lambda i: (i, 0)),
                          out_shape=jax.ShapeDtypeStruct((S, 128), F32),
                          scratch_shapes=[pltpu.VMEM((1, 128), F32)],
                          compiler_params=_params(("arbitrary",)))(fl, b)


def _gate_bwd(dc, fl, b, name, *, tm=512):
    S = fl.shape[0]
    tm = _tile(S, tm, 16)
    nb = S // tm

    def body(dc_ref, f_ref, b_ref, df_ref, db_ref, carry):
        @pl.when(pl.program_id(0) == 0)
        def _():
            carry[...] = jnp.zeros_like(carry)
            db_ref[...] = jnp.zeros_like(db_ref)
        tri = _tri(tm, False)
        suf = sum(_dot(tri, p) for p in _split3(dc_ref[...])) + carry[...]
        carry[...] = suf[0:1, :]
        df = suf * jax.nn.sigmoid(-(f_ref[...] + b_ref[...]))
        df_ref[...] = df
        db_ref[...] += jnp.sum(df, axis=0, keepdims=True)

    rev = pl.BlockSpec((tm, 128), lambda i: (nb - 1 - i, 0))
    vec = pl.BlockSpec((1, 128), lambda i: (0, 0))
    return pl.pallas_call(body, name=name, grid=(nb,), in_specs=[rev, rev, vec], out_specs=(rev, vec),
                          out_shape=(jax.ShapeDtypeStruct((S, 128), F32), jax.ShapeDtypeStruct((1, 128), F32)),
                          scratch_shapes=[pltpu.VMEM((1, 128), F32)],
                          compiler_params=_params(("arbitrary",)))(dc, fl, b)


def _out_proj(fo, so, gf, gs, w_out, x0, name, *, tm=512):
    S = fo.shape[0]
    tm = _tile(S, tm, 16)

    def body(fo_ref, so_ref, gf_ref, gs_ref, w_ref, x_ref, x1_ref, mx_ref):
        for ref, g_ref, lo in ((fo_ref, gf_ref, 0), (so_ref, gs_ref, GW)):
            o = ref[...]
            r = lax.rsqrt(jnp.mean(o * o, axis=-1, keepdims=True) + EPS)
            mx_ref[:, lo:lo + GW] = (o * r * g_ref[...]).astype(CDT)
        x1_ref[...] = x_ref[...] + _dot(mx_ref[...], w_ref[...])

    half = pl.BlockSpec((tm, GW), lambda i: (i, 0))
    gvec = pl.BlockSpec((1, GW), lambda i: (0, 0))
    row = pl.BlockSpec((tm, D), lambda i: (i, 0))
    return pl.pallas_call(body, name=name, grid=(S // tm,),
                          in_specs=[half, half, gvec, gvec, pl.BlockSpec((D, D), lambda i: (0, 0)), row],
                          out_specs=(row, row),
                          out_shape=(jax.ShapeDtypeStruct((S, D), F32), jax.ShapeDtypeStruct((S, D), CDT)),
                          compiler_params=_params(("parallel",)))(fo, so, gf, gs, w_out, x0)


def _out_proj_bwd(dx1, w_out, fo, so, gf, gs, name, *, tm=512):
    S = fo.shape[0]
    tm = _tile(S, tm, 16)

    def epilogue(acc, i, extra_refs, out_refs):
        fo_ref, so_ref, gf_ref, gs_ref = extra_refs
        dfo_ref, dso_ref, dgf_ref, dgs_ref = out_refs

        @pl.when(i == 0)
        def _():
            dgf_ref[...] = jnp.zeros_like(dgf_ref)
            dgs_ref[...] = jnp.zeros_like(dgs_ref)
        for lo, o_ref, g_ref, do_ref, dg_ref in ((0, fo_ref, gf_ref, dfo_ref, dgf_ref), (GW, so_ref, gs_ref, dso_ref, dgs_ref)):
            dx, dg = _rms_bwd(acc[:, lo:lo + GW], o_ref[...], g_ref[...])
            do_ref[...] = dx.astype(do_ref.dtype)
            dg_ref[...] += dg

    half = pl.BlockSpec((tm, GW), lambda i, j, k: (i, 0))
    gvec = pl.BlockSpec((1, GW), lambda i, j, k: (0, 0))
    return _mm_nt(dx1, w_out, name, tm=tm, epilogue=epilogue, extra=(fo, so, gf, gs),
                  extra_specs=(half, half, gvec, gvec),
                  out_shape=(jax.ShapeDtypeStruct((S, GW), CDT), jax.ShapeDtypeStruct((S, GW), CDT),
                             jax.ShapeDtypeStruct((1, GW), F32), jax.ShapeDtypeStruct((1, GW), F32)),
                  out_specs=(half, half, gvec, gvec))


def _loss_bwd(x3, tgt, g, name, *, tm=512):
    S = x3.shape[0]
    tm = _tile(S, tm, 16)

    def body(x_ref, t_ref, g_ref, dx_ref, loss_ref, dg_ref):
        @pl.when(pl.program_id(0) == 0)
        def _():
            loss_ref[...] = jnp.zeros_like(loss_ref)
            dg_ref[...] = jnp.zeros_like(dg_ref)
        x = x_ref[...]
        gv = g_ref[...]
        r = lax.rsqrt(jnp.mean(x * x, axis=-1, keepdims=True) + EPS)
        xn = x * r
        err = xn * gv - t_ref[...]
        loss_ref[...] += jnp.full(loss_ref.shape, 0.5 * jnp.sum(jnp.mean(err * err, axis=-1, keepdims=True)), F32)
        dy = err * (1.0 / D)
        dg_ref[...] += jnp.sum(dy * xn, axis=0, keepdims=True)
        dyg = dy * gv
        dx_ref[...] = r * (dyg - xn * jnp.mean(dyg * xn, axis=-1, keepdims=True))

    row = pl.BlockSpec((tm, D), lambda i: (i, 0))
    vec = pl.BlockSpec((1, D), lambda i: (0, 0))
    dx3, loss, dg = pl.pallas_call(
        body, name=name, grid=(S // tm,), in_specs=[row, row, vec],
        out_specs=(row, pl.BlockSpec((1, 128), lambda i: (0, 0)), vec),
        out_shape=(jax.ShapeDtypeStruct((S, D), F32), jax.ShapeDtypeStruct((1, 128), F32), jax.ShapeDtypeStruct((1, D), F32)),
        compiler_params=_params(("arbitrary",)))(x3, tgt, g)
    return loss, dx3, dg


MASKED, FIRST, LAST = 1, 2, 4


def _att_tiles(name, S):
    tq, tk = ATT_TILES[name]
    return min(tq, S), min(tk, S)


def _pairs(S, tq, tk, descending=True):
    assert tk % tq == 0 and S % tk == 0
    qi, kj, fl = [], [], []
    for i in range(S // tq):
        last = ((i + 1) * tq - 1) // tk
        order = list(range(last, -1, -1) if descending else range(last + 1))
        for pos, kb in enumerate(order):
            qi.append(i)
            kj.append(kb)
            fl.append((MASKED if (kb + 1) * tk - 1 > i * tq else 0) | (FIRST if pos == 0 else 0) | (LAST if pos == last else 0))
    return tuple(jnp.asarray(np.asarray(a, np.int32)) for a in (qi, kj, fl))


def _att_specs(tq, tk):
    qblk = pl.BlockSpec((1, tq, HD), lambda h, n, qi, kj, fl: (h, qi[n], 0))
    kblk = pl.BlockSpec((1, tk, HD), lambda h, n, qi, kj, fl: (h, kj[n], 0))
    qcol = pl.BlockSpec((1, tq, 1), lambda h, n, qi, kj, fl: (h, qi[n], 0))
    krow = pl.BlockSpec((1, 1, tk), lambda h, n, qi, kj, fl: (h, 0, kj[n]))
    return qblk, kblk, qcol, krow


def _causal(tq, w, ahead, strict):
    diff = lax.broadcasted_iota(jnp.int32, (tq, w), 1) - lax.broadcasted_iota(jnp.int32, (tq, w), 0)
    return diff < ahead if strict else diff <= ahead


def _masked_or_not(flags, step):
    pl.when(flags % 2 == 1)(functools.partial(step, True))
    pl.when(flags % 2 == 0)(functools.partial(step, False))


def _fox_fwd(q, k, v, cq, ck, name):
    S = q.shape[1]
    tq, tk = _att_tiles("fox_fwd", S)
    qi, kj, fl = _pairs(S, tq, tk)
    qblk, kblk, qcol, krow = _att_specs(tq, tk)

    def body(qi_ref, kj_ref, fl_ref, q_ref, k_ref, v_ref, cq_ref, ck_ref, o_ref, lse_ref, m_s, l_s, acc_s):
        n = pl.program_id(1)
        i, kb, flags = qi_ref[n], kj_ref[n], fl_ref[n]

        @pl.when(flags & FIRST != 0)
        def _():
            m_s[...] = jnp.full_like(m_s, NEG)
            l_s[...] = jnp.zeros_like(l_s)
            acc_s[...] = jnp.zeros_like(acc_s)

        def step(masked):
            s = _dot_nt(q_ref[0] * ATT_SCALE, k_ref[0]) + cq_ref[0] - ck_ref[0]
            if masked:
                s = jnp.where(_causal(tq, tk, i * tq - kb * tk, False), s, NEG)
            m_new = jnp.maximum(m_s[...], jnp.max(s, axis=-1, keepdims=True))
            alpha = jnp.exp(m_s[...] - m_new)
            p = jnp.exp(s - m_new)
            l_s[...] = alpha * l_s[...] + jnp.sum(p, axis=-1, keepdims=True)
            acc_s[...] = alpha * acc_s[...] + _dot(p.astype(CDT), v_ref[0])
            m_s[...] = m_new

        _masked_or_not(flags, step)

        @pl.when(flags & LAST != 0)
        def _():
            o_ref[0] = acc_s[...] / l_s[...]
            lse_ref[0] = m_s[...] + jnp.log(l_s[...])

    grid_spec = pltpu.PrefetchScalarGridSpec(
        num_scalar_prefetch=3, grid=(NH, int(qi.shape[0])),
        in_specs=[qblk, kblk, kblk, qcol, krow], out_specs=(qblk, qcol),
        scratch_shapes=[pltpu.VMEM((tq, 1), F32), pltpu.VMEM((tq, 1), F32), pltpu.VMEM((tq, HD), F32)])
    return pl.pallas_call(body, name=name, grid_spec=grid_spec,
                          out_shape=(jax.ShapeDtypeStruct((NH, S, HD), F32), jax.ShapeDtypeStruct((NH, S, 1), F32)),
                          compiler_params=_params(("parallel", "arbitrary")))(qi, kj, fl, q, k, v, cq, ck)


def _fox_bwd(q, k, v, cq, ck, o, do, lse, name):
    S = q.shape[1]
    tq, tk = _att_tiles("fox_bwd", S)
    qi, kj, fl = _pairs(S, tq, tk)
    qblk, kblk, qcol, krow = _att_specs(tq, tk)

    def body(qi_ref, kj_ref, fl_ref, q_ref, k_ref, v_ref, cq_ref, ck_ref, o_ref, do_ref, lse_ref,
             dq_ref, dk_ref, dv_ref, dck_ref, dcq_ref, dq_s, dl_s, dcq_s):
        n = pl.program_id(1)
        i, kb, flags = qi_ref[n], kj_ref[n], fl_ref[n]

        @pl.when(n == 0)
        def _():
            dk_ref[...] = jnp.zeros_like(dk_ref)
            dv_ref[...] = jnp.zeros_like(dv_ref)
            dck_ref[...] = jnp.zeros_like(dck_ref)

        @pl.when(flags & FIRST != 0)
        def _():
            dq_s[...] = jnp.zeros_like(dq_s)
            dcq_s[...] = jnp.zeros_like(dcq_s)
            dl_s[...] = jnp.sum(do_ref[0].astype(F32) * o_ref[0], axis=-1, keepdims=True)

        def step(masked):
            qs = q_ref[0] * ATT_SCALE
            do = do_ref[0]
            p = jnp.exp(_dot_nt(qs, k_ref[0]) + cq_ref[0] - ck_ref[0] - lse_ref[0])
            if masked:
                p = jnp.where(_causal(tq, tk, i * tq - kb * tk, False), p, 0.0)
            ds = p * (_dot_nt(do, v_ref[0]) - dl_s[...])
            dsb = ds.astype(CDT)
            dq_s[...] += _dot(dsb, k_ref[0])
            rows = pl.ds(pl.multiple_of(kb * tk, tk), tk)
            dk_ref[0, rows, :] += _dot_tn(dsb, qs)
            dv_ref[0, rows, :] += _dot_tn(p.astype(CDT), do)
            dck_ref[0, :, rows] += -jnp.sum(ds, axis=0, keepdims=True)
            dcq_s[...] += jnp.sum(ds, axis=-1, keepdims=True)

        _masked_or_not(flags, step)

        @pl.when(flags & LAST != 0)
        def _():
            dq_ref[0] = (dq_s[...] * ATT_SCALE).astype(dq_ref.dtype)
            dcq_ref[0] = dcq_s[...]

    whole = pl.BlockSpec((1, S, HD), lambda h, n, qi, kj, fl: (h, 0, 0))
    grid_spec = pltpu.PrefetchScalarGridSpec(
        num_scalar_prefetch=3, grid=(NH, int(qi.shape[0])),
        in_specs=[qblk, kblk, kblk, qcol, krow, qblk, qblk, qcol],
        out_specs=(qblk, whole, whole, pl.BlockSpec((1, 1, S), lambda h, n, qi, kj, fl: (h, 0, 0)), qcol),
        scratch_shapes=[pltpu.VMEM((tq, HD), F32), pltpu.VMEM((tq, 1), F32), pltpu.VMEM((tq, 1), F32)])
    return pl.pallas_call(body, name=name, grid_spec=grid_spec,
                          out_shape=(jax.ShapeDtypeStruct((NH, S, HD), CDT), jax.ShapeDtypeStruct((NH, S, HD), F32),
                                     jax.ShapeDtypeStruct((NH, S, HD), F32), jax.ShapeDtypeStruct((NH, 1, S), F32),
                                     jax.ShapeDtypeStruct((NH, S, 1), F32)),
                          compiler_params=_params(("parallel", "arbitrary")))(qi, kj, fl, q, k, v, cq, ck, o, do, lse)


LOG2E = 1.4426950408889634


def _sb_softplus2(qs, ksub, mask):
    z2 = _dot_nt(qs, ksub) * LOG2E
    sp2 = jnp.maximum(z2, 0.0) + jnp.log2(1.0 + jnp.exp2(-jnp.abs(z2)))
    return z2, sp2 if mask is None else jnp.where(mask, sp2, 0.0)


def _strict_tri(n, upper, value):
    r = lax.broadcasted_iota(jnp.int32, (n, n), 0)
    c = lax.broadcasted_iota(jnp.int32, (n, n), 1)
    return jnp.where(r < c if upper else r > c, value, 0.0).astype(CDT)


def _sb_fwd(q, k, v, name):
    S = q.shape[1]
    tq, tk = _att_tiles("sb_fwd", S)
    W = min(W_SB, tk)
    qi, kj, fl = _pairs(S, tq, tk)
    qblk, kblk, qcol, _ = _att_specs(tq, tk)

    def body(qi_ref, kj_ref, fl_ref, q_ref, k_ref, v_ref, o_ref, lt_ref, run_s, acc_s):
        n = pl.program_id(1)
        i, kb, flags = qi_ref[n], kj_ref[n], fl_ref[n]

        @pl.when(flags & FIRST != 0)
        def _():
            run_s[...] = jnp.zeros_like(run_s)
            acc_s[...] = jnp.zeros_like(acc_s)

        def step(masked):
            qs = q_ref[0] * ATT_SCALE
            neg_later = _strict_tri(W, False, -1.0)
            run = run_s[...]
            acc = acc_s[...]
            for sub in range(tk // W - 1, -1, -1):
                cols = slice(sub * W, (sub + 1) * W)
                mask = _causal(tq, W, i * tq - kb * tk - sub * W, True) if masked else None
                z2, sp2 = _sb_softplus2(qs, k_ref[0, cols, :], mask)
                excl = _dot(sp2.astype(CDT), neg_later)
                a = jnp.exp2((z2 - sp2) + (excl + run))
                if masked:
                    a = jnp.where(mask, a, 0.0)
                acc = acc + _dot(a.astype(CDT), v_ref[0, cols, :])
                run = run + (excl[:, 0:1] - sp2[:, 0:1])
            run_s[...] = run
            acc_s[...] = acc

        _masked_or_not(flags, step)

        @pl.when(flags & LAST != 0)
        def _():
            o_ref[0] = acc_s[...]
            lt_ref[0] = run_s[...]

    grid_spec = pltpu.PrefetchScalarGridSpec(
        num_scalar_prefetch=3, grid=(NH, int(qi.shape[0])), in_specs=[qblk, kblk, kblk], out_specs=(qblk, qcol),
        scratch_shapes=[pltpu.VMEM((tq, 1), F32), pltpu.VMEM((tq, HD), F32)])
    return pl.pallas_call(body, name=name, grid_spec=grid_spec,
                          out_shape=(jax.ShapeDtypeStruct((NH, S, HD), F32), jax.ShapeDtypeStruct((NH, S, 1), F32)),
                          compiler_params=_params(("parallel", "arbitrary")))(qi, kj, fl, q, k, v)


def _sb_bwd(q, k, v, do, lt, name):
    S = q.shape[1]
    tq, tk = _att_tiles("sb_bwd", S)
    W = min(W_SB, tk)
    qi, kj, fl = _pairs(S, tq, tk, descending=False)
    qblk, kblk, qcol, _ = _att_specs(tq, tk)

    def body(qi_ref, kj_ref, fl_ref, q_ref, k_ref, v_ref, do_ref, lt_ref, dq_ref, dk_ref, dv_ref, passed_s, gsum_s, dq_s):
        n = pl.program_id(1)
        i, kb, flags = qi_ref[n], kj_ref[n], fl_ref[n]

        @pl.when(n == 0)
        def _():
            dk_ref[...] = jnp.zeros_like(dk_ref)
            dv_ref[...] = jnp.zeros_like(dv_ref)

        @pl.when(flags & FIRST != 0)
        def _():
            passed_s[...] = jnp.zeros_like(passed_s)
            gsum_s[...] = jnp.zeros_like(gsum_s)
            dq_s[...] = jnp.zeros_like(dq_s)

        def step(masked):
            qs = q_ref[0] * ATT_SCALE
            do = do_ref[0]
            neg_later = _strict_tri(W, False, -1.0)
            earlier = _strict_tri(W, True, 1.0)
            for sub in range(tk // W):
                cols = slice(sub * W, (sub + 1) * W)
                mask = _causal(tq, W, i * tq - kb * tk - sub * W, True) if masked else None
                ksub = k_ref[0, cols, :]
                z2, sp2 = _sb_softplus2(qs, ksub, mask)
                excl = _dot(sp2.astype(CDT), neg_later)
                through = passed_s[...] + (excl[:, 0:1] - sp2[:, 0:1])
                t1 = z2 - sp2
                sig = jnp.exp2(t1)
                a = jnp.exp2(t1 + (excl + (lt_ref[0] - through)))
                if masked:
                    a = jnp.where(mask, a, 0.0)
                dl = _dot_nt(do, v_ref[0, cols, :]) * a
                before = _dot(dl.astype(CDT), earlier)
                dz = dl - sig * (dl + (before + gsum_s[...]))
                if masked:
                    dz = jnp.where(mask, dz, 0.0)
                dzb = dz.astype(CDT)
                dq_s[...] += _dot(dzb, ksub)
                rows = pl.ds(pl.multiple_of(kb * tk + sub * W, W), W)
                dk_ref[0, rows, :] += _dot_tn(dzb, qs)
                dv_ref[0, rows, :] += _dot_tn(a.astype(CDT), do)
                passed_s[...] = through
                gsum_s[...] += before[:, W - 1:W] + dl[:, W - 1:W]

        _masked_or_not(flags, step)

        @pl.when(flags & LAST != 0)
        def _():
            dq_ref[0] = (dq_s[...] * ATT_SCALE).astype(dq_ref.dtype)

    whole = pl.BlockSpec((1, S, HD), lambda h, n, qi, kj, fl: (h, 0, 0))
    grid_spec = pltpu.PrefetchScalarGridSpec(
        num_scalar_prefetch=3, grid=(NH, int(qi.shape[0])), in_specs=[qblk, kblk, kblk, qblk, qcol],
        out_specs=(qblk, whole, whole),
        scratch_shapes=[pltpu.VMEM((tq, 1), F32), pltpu.VMEM((tq, 1), F32), pltpu.VMEM((tq, HD), F32)])
    return pl.pallas_call(body, name=name, grid_spec=grid_spec,
                          out_shape=(jax.ShapeDtypeStruct((NH, S, HD), CDT), jax.ShapeDtypeStruct((NH, S, HD), F32),
                                     jax.ShapeDtypeStruct((NH, S, HD), F32)),
                          compiler_params=_params(("parallel", "arbitrary")))(qi, kj, fl, q, k, v, do, lt)


def _mem_probs(q_ref, kv_ref, h):
    cols = slice(h * MHD, (h + 1) * MHD)
    s = _dot_nt(q_ref[:, cols], kv_ref[:, cols]) * MEM_SCALE
    e = jnp.exp(s - jnp.max(s, axis=-1, keepdims=True))
    return e / jnp.sum(e, axis=-1, keepdims=True)


def _xattn_fwd(q, kv, w_mo, x1, name, *, tm=512):
    S = q.shape[0]
    tm = _tile(S, tm, 16)
    nm = kv.shape[0]

    def body(q_ref, kv_ref, w_ref, x_ref, x2_ref, o_ref):
        for h in range(NMH):
            p = _mem_probs(q_ref, kv_ref, h)
            o_ref[:, h * MHD:(h + 1) * MHD] = _dot(p.astype(CDT), kv_ref[:, D + h * MHD:D + (h + 1) * MHD]).astype(CDT)
        x2_ref[...] = x_ref[...] + _dot(o_ref[...], w_ref[...])

    row = pl.BlockSpec((tm, D), lambda i: (i, 0))
    return pl.pallas_call(body, name=name, grid=(S // tm,),
                          in_specs=[row, pl.BlockSpec((nm, 2 * D), lambda i: (0, 0)), pl.BlockSpec((D, D), lambda i: (0, 0)), row],
                          out_specs=(row, row),
                          out_shape=(jax.ShapeDtypeStruct((S, D), F32), jax.ShapeDtypeStruct((S, D), CDT)),
                          compiler_params=_params(("parallel",)))(q, kv, w_mo, x1)


def _xattn_bwd(q, kv, do, name, *, tm=512):
    S = q.shape[0]
    tm = _tile(S, tm, 16)
    nm = kv.shape[0]

    def body(q_ref, kv_ref, do_ref, dq_ref, dkv_ref):
        @pl.when(pl.program_id(0) == 0)
        def _():
            dkv_ref[...] = jnp.zeros_like(dkv_ref)
        for h in range(NMH):
            cols = slice(h * MHD, (h + 1) * MHD)
            vcols = slice(D + h * MHD, D + (h + 1) * MHD)
            p = _mem_probs(q_ref, kv_ref, h)
            doh = do_ref[:, cols]
            dp = _dot_nt(doh, kv_ref[:, vcols])
            ds = (p * (dp - jnp.sum(p * dp, axis=-1, keepdims=True)) * MEM_SCALE).astype(CDT)
            dq_ref[:, cols] = _dot(ds, kv_ref[:, cols]).astype(CDT)
            dkv_ref[:, cols] += _dot_tn(ds, q_ref[:, cols])
            dkv_ref[:, vcols] += _dot_tn(p.astype(CDT), doh)

    row = pl.BlockSpec((tm, D), lambda i: (i, 0))
    kvs = pl.BlockSpec((nm, 2 * D), lambda i: (0, 0))
    return pl.pallas_call(body, name=name, grid=(S // tm,), in_specs=[row, kvs, row], out_specs=(row, kvs),
                          out_shape=(jax.ShapeDtypeStruct((S, D), CDT), jax.ShapeDtypeStruct((nm, 2 * D), F32)),
                          compiler_params=_params(("arbitrary",)))(q, kv, do)


HALO = 16
SLAB = 8


def _shift_down(u, prev, s):
    rolled = pltpu.roll(u, s, 0)
    top = rolled[0:SLAB]
    r = lax.broadcasted_iota(jnp.int32, top.shape, 0)
    for t in range(s):
        top = jnp.where(r == t, prev[HALO - s + t:HALO - s + t + 1, :], top)
    return jnp.concatenate([top, rolled[SLAB:]], axis=0)


def _shift_up(u, nxt, s):
    n = u.shape[0]
    rolled = pltpu.roll(u, n - s, 0)
    bottom = rolled[n - SLAB:]
    r = lax.broadcasted_iota(jnp.int32, bottom.shape, 0)
    for t in range(s):
        bottom = jnp.where(r == SLAB - s + t, nxt[t:t + 1, :], bottom)
    return jnp.concatenate([rolled[:n - SLAB], bottom], axis=0)


def _conv_taps(u_ref, h_ref, first):
    u = u_ref[...].astype(F32)
    prev = jnp.where(first, 0.0, h_ref[...].astype(F32))
    out = []
    for half in range(2):
        out.append((u[half], _shift_down(u[half], prev[half], 1), _shift_down(u[half], prev[half], 2)))
    return out


def _conv_specs(tm, tn, nsb):
    blk = pl.BlockSpec((2, tm, tn), lambda j, i: (0, i, j))
    prev = pl.BlockSpec((2, HALO, tn), lambda j, i: (0, jnp.maximum(i * (tm // HALO) - 1, 0), j))
    nxt = pl.BlockSpec((2, HALO, tn), lambda j, i: (0, jnp.minimum((i + 1) * (tm // HALO), nsb - 1), j))
    w = pl.BlockSpec((2, 3, tn), lambda j, i: (0, 0, j))
    b = pl.BlockSpec((2, 1, tn), lambda j, i: (0, 0, j))
    return blk, prev, nxt, w, b


def _conv_apply(taps, w_ref, b_ref):
    ys = []
    for half in range(2):
        u, u1, u2 = taps[half]
        w = w_ref[half]
        ys.append(b_ref[half] + u2 * w[0:1, :] + u1 * w[1:2, :] + u * w[2:3, :])
    return ys


def _conv_act(u0, cw, cb, name, *, tm=512, tn=256):
    _, S, F = u0.shape
    tm = _tile(S, tm, HALO)
    tn = _tile(F, tn, 128)
    blk, prev, _, w, b = _conv_specs(tm, tn, S // HALO)

    def body(u_ref, h_ref, w_ref, b_ref, a_ref):
        yg, yv = _conv_apply(_conv_taps(u_ref, h_ref, pl.program_id(1) == 0), w_ref, b_ref)
        a_ref[...] = (yg * jax.nn.sigmoid(yg) * yv).astype(a_ref.dtype)

    return pl.pallas_call(body, name=name, grid=(F // tn, S // tm), in_specs=[blk, prev, w, b],
                          out_specs=pl.BlockSpec((tm, tn), lambda j, i: (i, j)),
                          out_shape=jax.ShapeDtypeStruct((S, F), CDT),
                          compiler_params=_params(("parallel", "parallel")))(u0, u0, cw, cb)


def _conv_act_bwd(u0, da, cw, cb, name, *, tm=512, tn=256):
    _, S, F = u0.shape
    tm = _tile(S, tm, HALO)
    tn = _tile(F, tn, 128)
    blk, prev, _, w, b = _conv_specs(tm, tn, S // HALO)

    def body(u_ref, h_ref, da_ref, w_ref, b_ref, du_ref, dwb_ref):
        @pl.when(pl.program_id(1) == 0)
        def _():
            dwb_ref[...] = jnp.zeros_like(dwb_ref)
        taps = _conv_taps(u_ref, h_ref, pl.program_id(1) == 0)
        yg, yv = _conv_apply(taps, w_ref, b_ref)
        sg = jax.nn.sigmoid(yg)
        da = da_ref[...].astype(F32)
        dus = (da * yv * sg * (1.0 + yg * (1.0 - sg)), da * yg * sg)
        for half in range(2):
            du = dus[half]
            du_ref[half] = du.astype(du_ref.dtype)
            u, u1, u2 = taps[half]
            for row, term in enumerate((du * u2, du * u1, du * u, du)):
                dwb_ref[half, row:row + 1, :] += jnp.sum(term, axis=0, keepdims=True)

    return pl.pallas_call(body, name=name, grid=(F // tn, S // tm),
                          in_specs=[blk, prev, pl.BlockSpec((tm, tn), lambda j, i: (i, j)), w, b],
                          out_specs=(blk, pl.BlockSpec((2, 4, tn), lambda j, i: (0, 0, j))),
                          out_shape=(jax.ShapeDtypeStruct((2, S, F), CDT), jax.ShapeDtypeStruct((2, 4, F), F32)),
                          compiler_params=_params(("parallel", "arbitrary")))(u0, u0, da, cw, cb)


def _conv_bwd_input(du, cw, name, *, tm=512, tn=256):
    _, S, F = du.shape
    tm = _tile(S, tm, HALO)
    tn = _tile(F, tn, 128)
    blk, _, nxt, w, _ = _conv_specs(tm, tn, S // HALO)
    ni = S // tm

    def body(d_ref, h_ref, w_ref, o_ref):
        d = d_ref[...].astype(F32)
        nx = jnp.where(pl.program_id(1) == ni - 1, 0.0, h_ref[...].astype(F32))
        for half in range(2):
            wv = w_ref[half]
            y = d[half] * wv[2:3, :] + _shift_up(d[half], nx[half], 1) * wv[1:2, :] + _shift_up(d[half], nx[half], 2) * wv[0:1, :]
            o_ref[half] = y.astype(o_ref.dtype)

    return pl.pallas_call(body, name=name, grid=(F // tn, ni), in_specs=[blk, nxt, w], out_specs=blk,
                          out_shape=jax.ShapeDtypeStruct((2, S, F), CDT),
                          compiler_params=_params(("parallel", "parallel")))(du, du, cw)


ANY = pl.BlockSpec(memory_space=pl.ANY)


def _place():
    return lax.axis_index("x"), lax.axis_index("y"), lax.axis_index("c")


def _other_chips(x, y):
    return ((1 - x, y), (x, 1 - y), (1 - x, 1 - y))


def _gather_weights(buf):
    n, rows, _ = buf.shape
    half = rows // 2

    def body(buf_ref, out_ref, send_sems, recv_sems):
        del buf_ref
        x, y, c = _place()
        chips = _other_chips(x, y)

        def part(chip, pc):
            return out_ref.at[2 * chip[0] + chip[1], pl.ds(pl.multiple_of(pc * half, 16), half), :]

        def copy(k, chip, pc, to):
            return pltpu.make_async_remote_copy(src_ref=part(chip, pc), dst_ref=part(chip, pc),
                                                send_sem=send_sems.at[k], recv_sem=recv_sems.at[k],
                                                device_id=to, device_id_type=MESH)

        first = [copy(j, (x, y), c, (*chip, c)) for j, chip in enumerate(chips)]
        for cp in first:
            cp.start()
        passed = [copy(3 + j, chip, c, (x, y, 1 - c)) for j, chip in enumerate(chips)]
        for j, chip in enumerate(chips):
            copy(j, chip, c, (x, y, c)).wait_recv()
            passed[j].start()
        for j, chip in enumerate(chips):
            copy(3 + j, chip, 1 - c, (x, y, c)).wait_recv()
        for cp in first + passed:
            cp.wait_send()

    return pl.pallas_call(body, name="gather_weights", in_specs=[ANY], out_specs=ANY,
                          out_shape=jax.ShapeDtypeStruct(buf.shape, buf.dtype), input_output_aliases={0: 0},
                          scratch_shapes=[pltpu.SemaphoreType.DMA((6,)), pltpu.SemaphoreType.DMA((6,))])(buf)


def _gather_small(v):
    m = v.shape[0]

    def body(v_ref, out_ref, send_sems, recv_sems, local_sem):
        x, y, c = _place()
        me, sibling = (x, y, c), (x, y, 1 - c)
        chips = _other_chips(x, y)

        def rows(px, py, pc):
            return out_ref.at[pl.ds((4 * px + 2 * py + pc) * m, m), :]

        def copy(k, block, to, src=None):
            return pltpu.make_async_remote_copy(src_ref=rows(*block) if src is None else src, dst_ref=rows(*block),
                                                send_sem=send_sems.at[k], recv_sem=recv_sems.at[k],
                                                device_id=to, device_id_type=MESH)

        mine = pltpu.make_async_copy(v_ref, rows(*me), local_sem)
        mine.start()
        first = [copy(0, me, sibling, src=v_ref)]
        first += [copy(1 + j, me, (*chip, c), src=v_ref) for j, chip in enumerate(chips)]
        for cp in first:
            cp.start()
        passed = [copy(4 + j, (*chip, c), sibling) for j, chip in enumerate(chips)]
        for j, chip in enumerate(chips):
            copy(1 + j, (*chip, c), me).wait_recv()
            passed[j].start()
        copy(0, sibling, me).wait_recv()
        for j, chip in enumerate(chips):
            copy(4 + j, (*chip, 1 - c), me).wait_recv()
        for cp in first + passed:
            cp.wait_send()
        mine.wait()

    vm = pl.BlockSpec(memory_space=pltpu.VMEM)
    return pl.pallas_call(body, name="gather_small", in_specs=[vm], out_specs=vm,
                          out_shape=jax.ShapeDtypeStruct((8 * m, 128), v.dtype),
                          scratch_shapes=[pltpu.SemaphoreType.DMA((7,)), pltpu.SemaphoreType.DMA((7,)), pltpu.SemaphoreType.DMA])(v)


def _swap_halves(g):
    n, rows, _ = g.shape
    half = rows // 2

    def body(g_ref, out_ref, send_sem, recv_sem):
        x, y, c = _place()
        src = g_ref.at[:, pl.ds(pl.multiple_of((1 - c) * half, 8), half), :]
        cp = pltpu.make_async_remote_copy(src_ref=src, dst_ref=out_ref, send_sem=send_sem, recv_sem=recv_sem,
                                          device_id=(x, y, 1 - c), device_id_type=MESH)
        cp.start()
        cp.wait()

    return pl.pallas_call(body, name="swap_halves", in_specs=[ANY], out_specs=ANY,
                          out_shape=jax.ShapeDtypeStruct((n, half, 128), g.dtype),
                          scratch_shapes=[pltpu.SemaphoreType.DMA, pltpu.SemaphoreType.DMA])(g)


def _scatter_chips(hsum):
    n, half, _ = hsum.shape

    def body(h_ref, out_ref, send_sems, recv_sems):
        x, y, c = _place()
        cps = []
        for j, chip in enumerate(_other_chips(x, y)):
            cp = pltpu.make_async_remote_copy(src_ref=h_ref.at[2 * chip[0] + chip[1]], dst_ref=out_ref.at[j],
                                              send_sem=send_sems.at[j], recv_sem=recv_sems.at[j],
                                              device_id=(*chip, c), device_id_type=MESH)
            cp.start()
            cps.append(cp)
        for cp in cps:
            cp.wait()

    return pl.pallas_call(body, name="scatter_chips", in_specs=[ANY], out_specs=ANY,
                          out_shape=jax.ShapeDtypeStruct((3, half, 128), hsum.dtype),
                          scratch_shapes=[pltpu.SemaphoreType.DMA((3,)), pltpu.SemaphoreType.DMA((3,))])(hsum)


def _join_halves(buf):
    half = buf.shape[0] // 2

    def body(buf_ref, out_ref, send_sem, recv_sem):
        del buf_ref
        x, y, c = _place()
        mine = out_ref.at[pl.ds(pl.multiple_of(c * half, 8), half), :]
        other = out_ref.at[pl.ds(pl.multiple_of((1 - c) * half, 8), half), :]
        cp = pltpu.make_async_remote_copy(src_ref=mine, dst_ref=mine, send_sem=send_sem, recv_sem=recv_sem,
                                          device_id=(x, y, 1 - c), device_id_type=MESH)
        cp.start()
        cp.wait_send()
        pltpu.make_async_remote_copy(src_ref=other, dst_ref=other, send_sem=send_sem, recv_sem=recv_sem,
                                     device_id=(x, y, 1 - c), device_id_type=MESH).wait_recv()

    return pl.pallas_call(body, name="join_halves", in_specs=[ANY], out_specs=ANY,
                          out_shape=jax.ShapeDtypeStruct(buf.shape, buf.dtype), input_output_aliases={0: 0},
                          scratch_shapes=[pltpu.SemaphoreType.DMA, pltpu.SemaphoreType.DMA])(buf)


def _add_sibling(g, recv, c_idx, name):
    n, rows, _ = g.shape
    half = rows // 2
    tr = _tile(half, ADAM_ROWS, 16)
    nb = half // tr

    def body(c_ref, g_ref, r_ref, o_ref, ob_ref):
        s = g_ref[...] + r_ref[...]
        o_ref[...] = s
        ob_ref[...] = s.astype(CDT)

    out = pl.BlockSpec((None, tr, 128), lambda k, i, c: (k, i, 0))
    grid_spec = pltpu.PrefetchScalarGridSpec(
        num_scalar_prefetch=1, grid=(n, nb),
        in_specs=[pl.BlockSpec((None, tr, 128), lambda k, i, c: (k, c[0] * nb + i, 0)), out],
        out_specs=(out, out))
    return pl.pallas_call(body, name=name, grid_spec=grid_spec,
                          out_shape=(jax.ShapeDtypeStruct((n, half, 128), F32), jax.ShapeDtypeStruct((n, half, 128), CDT)),
                          compiler_params=_params(("parallel", "parallel")))(c_idx, g, recv)


def _add_chips(hsum, recv, chip_idx, name):
    n, half, _ = hsum.shape
    tr = _tile(half, ADAM_ROWS, 16)

    def body(k_ref, h_ref, r_ref, o_ref):
        o_ref[...] = ((h_ref[...] + r_ref[0].astype(F32)) + r_ref[1].astype(F32)) + r_ref[2].astype(F32)

    grid_spec = pltpu.PrefetchScalarGridSpec(
        num_scalar_prefetch=1, grid=(half // tr,),
        in_specs=[pl.BlockSpec((None, tr, 128), lambda i, k: (k[0], i, 0)),
                  pl.BlockSpec((3, tr, 128), lambda i, k: (0, i, 0))],
        out_specs=pl.BlockSpec((tr, 128), lambda i, k: (i, 0)))
    return pl.pallas_call(body, name=name, grid_spec=grid_spec, out_shape=jax.ShapeDtypeStruct((half, 128), F32),
                          compiler_params=_params(("parallel",)))(chip_idx, hsum, recv)


def _adamw_math(g, w, m, v):
    m2 = B1 * m + (1.0 - B1) * g
    v2 = B2 * v + (1.0 - B2) * (g * g)
    delta = -LR * ((m2 / BC1) / (jnp.sqrt(v2 / BC2) + AEPS) + WD * w)
    return delta, m2, v2


def _adamw(g, w, m, v, name):
    rows, cols = g.shape
    tr = _tile(rows, max(8, (ADAM_ROWS * 128 // cols) // 8 * 8), 8)

    def body(g_ref, w_ref, m_ref, v_ref, d_ref, m2_ref, v2_ref):
        d_ref[...], m2_ref[...], v2_ref[...] = _adamw_math(g_ref[...], w_ref[...], m_ref[...], v_ref[...])

    blk = pl.BlockSpec((tr, cols), lambda i: (i, 0))
    shp = jax.ShapeDtypeStruct((rows, cols), F32)
    return pl.pallas_call(body, name=name, grid=(rows // tr,), in_specs=[blk] * 4, out_specs=(blk,) * 3,
                          out_shape=(shp,) * 3, compiler_params=_params(("parallel",)))(g, w, m, v)


def _adamw_small(parts, w, m, v, name):
    rows = w.shape[0]

    def body(p_ref, w_ref, m_ref, v_ref, g_ref, d_ref, m2_ref, v2_ref):
        g = p_ref[0]
        for k in range(1, 8):
            g = g + p_ref[k]
        g_ref[...] = g
        d_ref[...], m2_ref[...], v2_ref[...] = _adamw_math(g, w_ref[...], m_ref[...], v_ref[...])

    shp = jax.ShapeDtypeStruct((rows, 128), F32)
    return pl.pallas_call(body, name=name, out_shape=(shp,) * 4)(parts, w, m, v)


def _pack_rows(parts, rows):
    flat = jnp.concatenate([p.reshape(-1) for p in parts])
    return jnp.pad(flat, (0, rows * 128 - flat.shape[0])).reshape(rows, 128)


def _unpack(flat, sizes, shapes):
    out, off = [], 0
    for n, s in zip(sizes, shapes):
        out.append(flat[off:off + n].reshape(s))
        off += n
    return out


def _to_shards(full, shard_shape, axis):
    if axis == 0:
        return full.reshape(N_CHIP, -1)
    r, cs = shard_shape
    return full.reshape(r, N_CHIP, cs).transpose(1, 0, 2).reshape(N_CHIP, -1)


def _from_shards(sh, shard_shape, axis):
    r, cs = shard_shape
    if axis == 0:
        return sh.reshape(N_CHIP * r, cs)
    return sh.reshape(N_CHIP, r, cs).transpose(1, 0, 2).reshape(r, N_CHIP * cs)


def _heads(t, n):
    s = t.shape[0]
    return t.reshape(s, n * NH, HD).transpose(1, 0, 2)


def _merge(t):
    return t.transpose(1, 0, 2).reshape(t.shape[1], GW)


def _local_step(x0, mem, tgt, W, gains):
    S = x0.shape[0]
    w_in = jnp.pad(W["w_in"], ((0, 0), (0, IN_PAD - IN_COLS)))
    b_f = jnp.pad(gains["b_forget"], ((0, 0), (0, 128 - NH)))
    cw = W["conv_w"].reshape(3, 2, DFF).transpose(1, 0, 2)
    cb = gains["conv_b"].reshape(2, 1, DFF)

    h1 = _rms_cast(x0, gains["attn_norm_g"], "norm_attn")
    qkv = _heads(_mm_nn(h1, w_in[:, :NQKV], CDT, "proj_qkv"), 6)
    fl = _mm_nn(h1, w_in[:, NQKV:NQKV + 128], F32, "proj_gate")
    cum = _gate_fwd(fl, b_f, "gate_cumsum")
    c_hm = cum[:, :NH].T
    cq, ck = c_hm[:, :, None], c_hm[:, None, :]
    fq, fk, fv, sq, sk, sv = (qkv[n * NH:(n + 1) * NH] for n in range(6))
    fo_h, lse = _fox_fwd(fq, fk, fv, cq, ck, "fox_fwd")
    so_h, s_lt = _sb_fwd(sq, sk, sv, "sb_fwd")
    fo, so = _merge(fo_h), _merge(so_h)
    x1, mixed = _out_proj(fo, so, gains["fox_out_g"], gains["sb_out_g"], W["w_out"], x0, "out_proj")

    h2 = _rms_cast(x1, gains["xattn_norm_g"], "norm_xattn")
    mn = _rms_cast(mem, gains["mem_norm_g"], "norm_mem")
    mq = _mm_nn(h2, W["w_mq"], CDT, "proj_mq")
    kv = _mm_nn(mn, W["w_mkv"], CDT, "proj_mkv")
    x2, mo = _xattn_fwd(mq, kv, W["w_mo"], x1, "xattn_fwd")

    h3 = _rms_cast(x2, gains["ffn_norm_g"], "norm_ffn")
    u0 = _mm_nn(h3, W["w_up"], CDT, "ffn_up", tm=512, tn=DFF, halves=True)
    act = _conv_act(u0, cw, cb, "conv_act")
    x3 = _mm_nn(act, W["w_down"], F32, "ffn_down", tm=512, residual=x2)
    loss, dx3, dg_final = _loss_bwd(x3, tgt, gains["final_norm_g"].reshape(1, D), "loss")

    gw, gs = {}, {"final_norm_g": dg_final}
    da = _mm_nt(dx3, W["w_down"], "ffn_down_dx", tn=DFF, out_dtype=CDT)
    gw["w_down"] = _mm_tn(act, dx3, "ffn_down_dw", tka=DFF)
    du, dwb = _conv_act_bwd(u0, da, cw, cb, "conv_act_bwd")
    gw["conv_w"] = dwb[:, :3].transpose(1, 0, 2).reshape(3, 2 * DFF)
    gs["conv_b"] = dwb[:, 3].reshape(1, 2 * DFF)
    du0 = _conv_bwd_input(du, cw, "conv_bwd_input")
    gw["w_up"] = _mm_tn(h3, du0, "ffn_up_dw", tn=DFF, b_halves=True)
    dx2, gs["ffn_norm_g"] = _mm_nt_rmsbwd(du0, W["w_up"], x2, gains["ffn_norm_g"], dx3, "ffn_up_dx", tk=DFF, a_halves=True)

    dmo = _mm_nt(dx2, W["w_mo"], "mo_dx", tn=512, out_dtype=CDT)
    gw["w_mo"] = _mm_tn(mo, dx2, "mo_dw")
    dmq, dkv = _xattn_bwd(mq, kv, dmo, "xattn_bwd")
    gw["w_mq"] = _mm_tn(h2, dmq, "mq_dw")
    dx1, gs["xattn_norm_g"] = _mm_nt_rmsbwd(dmq, W["w_mq"], x1, gains["xattn_norm_g"], dx2, "mq_dx")
    gw["w_mkv"] = _mm_tn(mn, dkv, "mkv_dw")
    _, gs["mem_norm_g"] = _mm_nt_rmsbwd(dkv, W["w_mkv"], mem, gains["mem_norm_g"], jnp.zeros_like(mem), "mkv_dx")

    gw["w_out"] = _mm_tn(mixed, dx1, "out_dw")
    dfo, dso, gs["fox_out_g"], gs["sb_out_g"] = _out_proj_bwd(dx1, W["w_out"], fo, so, gains["fox_out_g"], gains["sb_out_g"], "out_dx")
    dfo_h, dso_h = _heads(dfo, 1), _heads(dso, 1)
    dfq, dfk, dfv, dck, dcq = _fox_bwd(fq, fk, fv, cq, ck, fo_h, dfo_h, lse, "fox_bwd")
    dsq, dsk, dsv = _sb_bwd(sq, sk, sv, dso_h, s_lt, "sb_bwd")
    dc = jnp.pad((dck[:, 0, :] + dcq[:, :, 0]).T, ((0, 0), (0, 128 - NH)))
    dfl, db = _gate_bwd(dc, fl, b_f, "gate_bwd")
    gs["b_forget"] = db[:, :NH]
    dqkv = jnp.concatenate([dfq, dfk.astype(CDT), dfv.astype(CDT), dsq, dsk.astype(CDT), dsv.astype(CDT)], axis=0)
    dproj = jnp.concatenate([dqkv.transpose(1, 0, 2).reshape(S, NQKV), dfl.astype(CDT),
                             jnp.zeros((S, IN_PAD - NQKV - 128), CDT)], axis=1)
    gw["w_in"] = _mm_tn(h1, dproj, "in_dw", tn=IN_PAD)[:, :IN_COLS]
    dx0, gs["attn_norm_g"] = _mm_nt_rmsbwd(dproj, w_in, x0, gains["attn_norm_g"], dx1, "in_dx", tk=IN_PAD)
    return loss, dx0, gw, gs


NAMES = ("attn_norm_g", "w_in", "b_forget", "fox_out_g", "sb_out_g", "w_out", "xattn_norm_g", "mem_norm_g", "w_mq",
         "w_mkv", "w_mo", "ffn_norm_g", "w_up", "conv_w", "conv_b", "w_down", "final_norm_g")


def _step(x, mem, loss_target, w, m, v):
    xi, yi, ci = _place()
    big_shapes = [s for _, s, _ in BIG]

    parts = []
    for name, shape, _ in BIG:
        blk = w[name].reshape(shape)
        parts.append(lax.bitcast_convert_type(blk, CDT) if name == "conv_w" else blk.astype(CDT))
    slots = lax.dynamic_update_slice(lax.empty((N_CHIP, ROWS_G, 128), CDT), _pack_rows(parts, ROWS_G)[None], (2 * xi + yi, 0, 0))
    gathered = _gather_weights(slots).reshape(N_CHIP, ROWS_G * 128)
    W, off = {}, 0
    for (name, shape, axis), n in zip(BIG, GATHER_SIZES):
        sh = gathered[:, off:off + n]
        off += n
        if name == "conv_w":
            sh = lax.bitcast_convert_type(sh.reshape(N_CHIP, n // 2, 2), F32)
        W[name] = _from_shards(sh, shape, axis)
    gains = {name: w[name].reshape(1, -1) for name, _ in SMALL}

    loss, grad_x, gw, gs = _local_step(x[0], mem[0], loss_target[0], W, gains)

    g_flat = jnp.concatenate([_to_shards(gw[name], shape, axis) for name, shape, axis in BIG], axis=1)
    g_flat = jnp.pad(g_flat, ((0, 0), (0, ROWS_F * 128 - P_BIG))).reshape(N_CHIP, ROWS_F, 128)
    pair_sum, pair_sum_b = _add_sibling(g_flat, _swap_halves(g_flat), jnp.reshape(ci, (1,)).astype(jnp.int32), "add_sibling")
    g_half = _add_chips(pair_sum, _scatter_chips(pair_sum_b), jnp.reshape(2 * xi + yi, (1,)).astype(jnp.int32), "add_chips")
    g_big = _join_halves(lax.dynamic_update_slice(lax.empty((ROWS_F, 128), F32), g_half, (ci * HALF_F, 0)))
    small = jnp.concatenate([gs[name].reshape(-1) for name, _ in SMALL] + [loss[0, :1]])
    small = jnp.pad(small, (0, ROWS_S * 128 - P_SMALL)).reshape(ROWS_S, 128)
    small_parts = _gather_small(small).reshape(8, ROWS_S, 128)

    def flat_small(d):
        return _pack_rows([d[name] for name, _ in SMALL], ROWS_S)

    outs = {}
    for (name, shape, _), g in zip(BIG, _unpack(g_big.reshape(-1), BIG_SIZES, big_shapes)):
        res = _adamw(g, w[name].reshape(shape), m[name].reshape(shape), v[name].reshape(shape), "adamw_" + name)
        for prefix, arr in zip(("grad_", "delta_", "new_m_", "new_v_"), (g, *res)):
            outs[prefix + name] = arr.reshape(w[name].shape)
    small_res = _adamw_small(small_parts, flat_small(w), flat_small(m), flat_small(v), "adamw_small")
    g_sm = small_res[0]
    for prefix, sm in zip(("grad_", "delta_", "new_m_", "new_v_"), small_res):
        for (name, n), arr in zip(SMALL, _unpack(sm.reshape(-1), [n for _, n in SMALL], [(n,) for _, n in SMALL])):
            outs[prefix + name] = arr.reshape(w[name].shape)
    total_loss = g_sm.reshape(-1)[P_SMALL - 1]
    return (total_loss, grad_x[None], *[outs[p + n] for p in ("grad_", "delta_", "new_m_", "new_v_") for n in NAMES])


def kernel(x, mem, attn_norm_g, w_in, b_forget, fox_out_g, sb_out_g, w_out, xattn_norm_g, mem_norm_g, w_mq, w_mkv, w_mo, ffn_norm_g, w_up, conv_w, conv_b, w_down, final_norm_g, loss_target, m_attn_norm_g, m_w_in, m_b_forget, m_fox_out_g, m_sb_out_g, m_w_out, m_xattn_norm_g, m_mem_norm_g, m_w_mq, m_w_mkv, m_w_mo, m_ffn_norm_g, m_w_up, m_conv_w, m_conv_b, m_w_down, m_final_norm_g, v_attn_norm_g, v_w_in, v_b_forget, v_fox_out_g, v_sb_out_g, v_w_out, v_xattn_norm_g, v_mem_norm_g, v_w_mq, v_w_mkv, v_w_mo, v_ffn_norm_g, v_w_up, v_conv_w, v_conv_b, v_w_down, v_final_norm_g):
    given = dict(locals())
    w = {n: given[n] for n in NAMES}
    m = {n: given["m_" + n] for n in NAMES}
    v = {n: given["v_" + n] for n in NAMES}
    return _step(x, mem, loss_target, w, m, v)
```

```python
import functools

import numpy as np
import jax
import jax.numpy as jnp
from jax import lax
from jax.experimental import pallas as pl
from jax.experimental.pallas import tpu as pltpu

F32 = jnp.float32
CDT = jnp.bfloat16
MESH = pl.DeviceIdType.MESH

D = 1024
HD = 64
NH = 8
GW = NH * HD
NQKV = 6 * GW
IN_COLS = NQKV + NH
IN_PAD = NQKV + 256
NMH = 4
MHD = D // NMH
DFF = 2816
EPS = 1e-6
ATT_SCALE = HD ** -0.5
MEM_SCALE = MHD ** -0.5
NEG = -1e30

LR, B1, B2, AEPS, WD, STEP = 0.001, 0.9, 0.999, 1e-08, 0.01, 10
BC1 = 1.0 - B1 ** STEP
BC2 = 1.0 - B2 ** STEP

ATT_TILES = {"fox_fwd": (512, 1024), "fox_bwd": (512, 1024), "sb_fwd": (512, 1024), "sb_bwd": (1024, 1024)}
W_SB = 256
VMEM_LIMIT = 52 * 2 ** 20

N_CHIP = 4
BIG = (("w_in", (D, IN_COLS // N_CHIP), 1), ("w_out", (D // N_CHIP, D), 0), ("w_mq", (D // N_CHIP, D), 0),
       ("w_mkv", (D, 2 * D // N_CHIP), 1), ("w_mo", (D // N_CHIP, D), 0), ("w_up", (D, 2 * DFF // N_CHIP), 1),
       ("conv_w", (3, 2 * DFF // N_CHIP), 1), ("w_down", (DFF // N_CHIP, D), 0))
BIG_SIZES = tuple(int(np.prod(s)) for _, s, _ in BIG)
P_BIG = sum(BIG_SIZES)
ROWS_F = 33 * 1024
assert ROWS_F * 128 >= P_BIG
HALF_F = ROWS_F // 2
ADAM_ROWS = 1536
GATHER_SIZES = tuple(2 * n if name == "conv_w" else n for (name, _, _), n in zip(BIG, BIG_SIZES))
ROWS_G = -(-sum(GATHER_SIZES) // 4096) * 32
HALF_G = ROWS_G // 2
SMALL = (("attn_norm_g", 1024), ("b_forget", 8), ("fox_out_g", 512), ("sb_out_g", 512), ("xattn_norm_g", 1024),
         ("mem_norm_g", 1024), ("ffn_norm_g", 1024), ("conv_b", 2 * DFF), ("final_norm_g", 1024))
P_SMALL = sum(n for _, n in SMALL) + 1
ROWS_S = -(-P_SMALL // 1024) * 8


def _params(sem=None, vmem=VMEM_LIMIT):
    return pltpu.CompilerParams(dimension_semantics=sem, vmem_limit_bytes=vmem)


def _tile(n, pref, mult):
    t = (min(pref, n) // mult) * mult
    while t >= mult:
        if n % t == 0:
            return t
        t -= mult
    return n


def _dot(a, b):
    return jnp.dot(a, b, preferred_element_type=F32)


def _dot_nt(a, b):
    return lax.dot_general(a, b, (((1,), (1,)), ((), ())), preferred_element_type=F32)


def _dot_tn(a, b):
    return lax.dot_general(a, b, (((0,), (0,)), ((), ())), preferred_element_type=F32)


def _split3(x):
    h1 = x.astype(CDT)
    r1 = x - h1.astype(F32)
    h2 = r1.astype(CDT)
    h3 = (r1 - h2.astype(F32)).astype(CDT)
    return h1, h2, h3


def _split2(x):
    h1 = x.astype(CDT)
    return h1, (x - h1.astype(F32)).astype(CDT)


def _rms_bwd(dh, x, g):
    r = lax.rsqrt(jnp.mean(x * x, axis=-1, keepdims=True) + EPS)
    xn = x * r
    dg = jnp.sum(dh * xn, axis=0, keepdims=True)
    dhg = dh * g
    dx = r * (dhg - xn * jnp.mean(dhg * xn, axis=-1, keepdims=True))
    return dx, dg


def _mm_nn(a, b, out_dtype, name, *, tm=1024, tn=512, residual=None, halves=False):
    M, K = a.shape
    N = b.shape[1]
    tm = _tile(M, tm, 16)
    tn = _tile(N // 2 if halves else N, tn, 128)
    nj = N // tn

    def body(*refs):
        a_ref, b_ref = refs[0], refs[1]
        o_ref = refs[-1]
        acc = _dot(a_ref[...].astype(CDT), b_ref[...].astype(CDT))
        if residual is not None:
            acc = acc + refs[2][...]
        o_ref[...] = acc.astype(o_ref.dtype)

    in_specs = [pl.BlockSpec((tm, K), lambda i, j: (i, 0)), pl.BlockSpec((K, tn), lambda i, j: (0, j))]
    ops = [a, b]
    if residual is not None:
        in_specs.append(pl.BlockSpec((tm, tn), lambda i, j: (i, j)))
        ops.append(residual)
    if halves:
        njh = nj // 2
        out_shape = jax.ShapeDtypeStruct((2, M, N // 2), out_dtype)
        out_spec = pl.BlockSpec((None, tm, tn), lambda i, j: (j // njh, i, j % njh))
    else:
        out_shape = jax.ShapeDtypeStruct((M, N), out_dtype)
        out_spec = pl.BlockSpec((tm, tn), lambda i, j: (i, j))
    return pl.pallas_call(body, name=name, grid=(M // tm, nj), in_specs=in_specs, out_specs=out_spec,
                          out_shape=out_shape, compiler_params=_params(("parallel", "parallel")))(*ops)


def _mm_tn(a, b, name, *, tka=512, tn=1024, ts=512, b_halves=False):
    S, Ka = a.shape
    N = 2 * b.shape[2] if b_halves else b.shape[1]
    tka = _tile(Ka, tka, 128)
    tn = _tile(N // 2 if b_halves else N, tn, 128)
    ts = _tile(S, ts, 16)
    nn = N // tn

    def body(a_ref, b_ref, o_ref):
        @pl.when(pl.program_id(2) == 0)
        def _():
            o_ref[...] = jnp.zeros_like(o_ref)
        o_ref[...] += _dot_tn(a_ref[...].astype(CDT), b_ref[...].astype(CDT))

    if b_halves:
        nnh = nn // 2
        b_spec = pl.BlockSpec((None, ts, tn), lambda i, j, s: (j // nnh, s, j % nnh))
    else:
        b_spec = pl.BlockSpec((ts, tn), lambda i, j, s: (s, j))
    return pl.pallas_call(
        body, name=name, grid=(Ka // tka, nn, S // ts),
        in_specs=[pl.BlockSpec((ts, tka), lambda i, j, s: (s, i)), b_spec],
        out_specs=pl.BlockSpec((tka, tn), lambda i, j, s: (i, j)),
        out_shape=jax.ShapeDtypeStruct((Ka, N), F32),
        compiler_params=_params(("parallel", "parallel", "arbitrary")))(a, b)


def _mm_nt(a, b, name, *, tm=512, tn=None, tk=None, a_halves=False, out_dtype=F32,
           epilogue=None, extra=(), extra_specs=(), out_shape=None, out_specs=None):
    if a_halves:
        M, K = a.shape[1], 2 * a.shape[2]
    else:
        M, K = a.shape
    N = b.shape[0]
    tm = _tile(M, tm, 16)
    tn = N if (epilogue is not None or tn is None) else _tile(N, tn, 128)
    tk = K if tk is None else _tile(K // 2 if a_halves else K, tk, 128)
    nk = K // tk
    n_extra = len(extra)

    def body(*refs):
        a_ref, b_ref = refs[0], refs[1]
        extra_refs = refs[2:2 + n_extra]
        out_refs = refs[2 + n_extra:-1]
        acc_ref = refs[-1]
        k = pl.program_id(2)

        @pl.when(k == 0)
        def _():
            acc_ref[...] = jnp.zeros_like(acc_ref)
        acc_ref[...] += _dot_nt(a_ref[...].astype(CDT), b_ref[...].astype(CDT))

        @pl.when(k == nk - 1)
        def _():
            if epilogue is None:
                out_refs[0][...] = acc_ref[...].astype(out_refs[0].dtype)
            else:
                epilogue(acc_ref[...], pl.program_id(0), extra_refs, out_refs)

    if a_halves:
        nkh = nk // 2
        a_spec = pl.BlockSpec((None, tm, tk), lambda i, j, k: (k // nkh, i, k % nkh))
    else:
        a_spec = pl.BlockSpec((tm, tk), lambda i, j, k: (i, k))
    if epilogue is None:
        out_shape = jax.ShapeDtypeStruct((M, N), out_dtype)
        out_specs = pl.BlockSpec((tm, tn), lambda i, j, k: (i, j))
        sem = ("parallel", "parallel", "arbitrary")
    else:
        sem = ("arbitrary", "arbitrary", "arbitrary")
    return pl.pallas_call(
        body, name=name, grid=(M // tm, N // tn, nk),
        in_specs=[a_spec, pl.BlockSpec((tn, tk), lambda i, j, k: (j, k)), *extra_specs],
        out_specs=out_specs, out_shape=out_shape,
        scratch_shapes=[pltpu.VMEM((tm, tn), F32)],
        compiler_params=_params(sem))(a, b, *extra)


def _mm_nt_rmsbwd(a, b, x, g, dres, name, *, tm=512, tk=None, a_halves=False):
    M = x.shape[0]
    tm = _tile(M, tm, 16)

    def epilogue(acc, i, extra_refs, out_refs):
        x_ref, g_ref, r_ref = extra_refs
        dx_ref, dg_ref = out_refs
        dx, dg = _rms_bwd(acc, x_ref[...], g_ref[...])
        dx_ref[...] = r_ref[...] + dx

        @pl.when(i == 0)
        def _():
            dg_ref[...] = jnp.zeros_like(dg_ref)
        dg_ref[...] += dg

    row = pl.BlockSpec((tm, D), lambda i, j, k: (i, 0))
    vec = pl.BlockSpec((1, D), lambda i, j, k: (0, 0))
    return _mm_nt(a, b, name, tm=tm, tk=tk, a_halves=a_halves, epilogue=epilogue,
                  extra=(x, g, dres), extra_specs=(row, vec, row),
                  out_shape=(jax.ShapeDtypeStruct((M, D), F32), jax.ShapeDtypeStruct((1, D), F32)),
                  out_specs=(row, vec))


def _rms_cast(x, g, name, *, tm=512):
    M, W = x.shape
    tm = _tile(M, tm, 16)

    def body(x_ref, g_ref, o_ref):
        xf = x_ref[...]
        r = lax.rsqrt(jnp.mean(xf * xf, axis=-1, keepdims=True) + EPS)
        o_ref[...] = (xf * r * g_ref[...]).astype(o_ref.dtype)

    return pl.pallas_call(body, name=name, grid=(M // tm,),
                          in_specs=[pl.BlockSpec((tm, W), lambda i: (i, 0)), pl.BlockSpec((1, W), lambda i: (0, 0))],
                          out_specs=pl.BlockSpec((tm, W), lambda i: (i, 0)),
                          out_shape=jax.ShapeDtypeStruct((M, W), CDT),
                          compiler_params=_params(("parallel",)))(x, g)


def _tri(n, lower):
    r = lax.broadcasted_iota(jnp.int32, (n, n), 0)
    c = lax.broadcasted_iota(jnp.int32, (n, n), 1)
    return (c <= r if lower else c >= r).astype(CDT)


def _gate_fwd(fl, b, name, *, tm=512):
    S = fl.shape[0]
    tm = _tile(S, tm, 16)

    def body(f_ref, b_ref, c_ref, carry):
        @pl.when(pl.program_id(0) == 0)
        def _():
            carry[...] = jnp.zeros_like(carry)
        z = f_ref[...] + b_ref[...]
        lf = jnp.minimum(z, 0.0) - jnp.log(1.0 + jnp.exp(-jnp.abs(z)))
        tri = _tri(tm, True)
        cum = sum(_dot(tri, p) for p in _split3(lf)) + carry[...]
        c_ref[...] = cum
        carry[...] = cum[tm - 1:tm, :]

    return pl.pallas_call(body, name=name, grid=(S // tm,),
                          in_specs=[pl.BlockSpec((tm, 128), lambda i: (i, 0)), pl.BlockSpec((1, 128), lambda i: (0, 0))],
                          out_specs=pl.BlockSpec((tm, 128), lambda i: (i, 0)),
                          out_shape=jax.ShapeDtypeStruct((S, 128), F32),
                          scratch_shapes=[pltpu.VMEM((1, 128), F32)],
                          compiler_params=_params(("arbitrary",)))(fl, b)


def _gate_bwd(dc, fl, b, name, *, tm=512):
    S = fl.shape[0]
    tm = _tile(S, tm, 16)
    nb = S // tm

    def body(dc_ref, f_ref, b_ref, df_ref, db_ref, carry):
        @pl.when(pl.program_id(0) == 0)
        def _():
            carry[...] = jnp.zeros_like(carry)
            db_ref[...] = jnp.zeros_like(db_ref)
        tri = _tri(tm, False)
        suf = sum(_dot(tri, p) for p in _split3(dc_ref[...])) + carry[...]
        carry[...] = suf[0:1, :]
        df = suf * jax.nn.sigmoid(-(f_ref[...] + b_ref[...]))
        df_ref[...] = df
        db_ref[...] += jnp.sum(df, axis=0, keepdims=True)

    rev = pl.BlockSpec((tm, 128), lambda i: (nb - 1 - i, 0))
    vec = pl.BlockSpec((1, 128), lambda i: (0, 0))
    return pl.pallas_call(body, name=name, grid=(nb,), in_specs=[rev, rev, vec], out_specs=(rev, vec),
                          out_shape=(jax.ShapeDtypeStruct((S, 128), F32), jax.ShapeDtypeStruct((1, 128), F32)),
                          scratch_shapes=[pltpu.VMEM((1, 128), F32)],
                          compiler_params=_params(("arbitrary",)))(dc, fl, b)


def _out_proj(fo, so, gf, gs, w_out, x0, name, *, tm=512):
    S = fo.shape[0]
    tm = _tile(S, tm, 16)

    def body(fo_ref, so_ref, gf_ref, gs_ref, w_ref, x_ref, x1_ref, mx_ref):
        for ref, g_ref, lo in ((fo_ref, gf_ref, 0), (so_ref, gs_ref, GW)):
            o = ref[...]
            r = lax.rsqrt(jnp.mean(o * o, axis=-1, keepdims=True) + EPS)
            mx_ref[:, lo:lo + GW] = (o * r * g_ref[...]).astype(CDT)
        x1_ref[...] = x_ref[...] + _dot(mx_ref[...], w_ref[...])

    half = pl.BlockSpec((tm, GW), lambda i: (i, 0))
    gvec = pl.BlockSpec((1, GW), lambda i: (0, 0))
    row = pl.BlockSpec((tm, D), lambda i: (i, 0))
    return pl.pallas_call(body, name=name, grid=(S // tm,),
                          in_specs=[half, half, gvec, gvec, pl.BlockSpec((D, D), lambda i: (0, 0)), row],
                          out_specs=(row, row),
                          out_shape=(jax.ShapeDtypeStruct((S, D), F32), jax.ShapeDtypeStruct((S, D), CDT)),
                          compiler_params=_params(("parallel",)))(fo, so, gf, gs, w_out, x0)


def _out_proj_bwd(dx1, w_out, fo, so, gf, gs, name, *, tm=512):
    S = fo.shape[0]
    tm = _tile(S, tm, 16)

    def epilogue(acc, i, extra_refs, out_refs):
        fo_ref, so_ref, gf_ref, gs_ref = extra_refs
        dfo_ref, dso_ref, dgf_ref, dgs_ref = out_refs

        @pl.when(i == 0)
        def _():
            dgf_ref[...] = jnp.zeros_like(dgf_ref)
            dgs_ref[...] = jnp.zeros_like(dgs_ref)
        for lo, o_ref, g_ref, do_ref, dg_ref in ((0, fo_ref, gf_ref, dfo_ref, dgf_ref), (GW, so_ref, gs_ref, dso_ref, dgs_ref)):
            dx, dg = _rms_bwd(acc[:, lo:lo + GW], o_ref[...], g_ref[...])
            do_ref[...] = dx.astype(do_ref.dtype)
            dg_ref[...] += dg

    half = pl.BlockSpec((tm, GW), lambda i, j, k: (i, 0))
    gvec = pl.BlockSpec((1, GW), lambda i, j, k: (0, 0))
    return _mm_nt(dx1, w_out, name, tm=tm, epilogue=epilogue, extra=(fo, so, gf, gs),
                  extra_specs=(half, half, gvec, gvec),
                  out_shape=(jax.ShapeDtypeStruct((S, GW), CDT), jax.ShapeDtypeStruct((S, GW), CDT),
                             jax.ShapeDtypeStruct((1, GW), F32), jax.ShapeDtypeStruct((1, GW), F32)),
                  out_specs=(half, half, gvec, gvec))


def _loss_bwd(x3, tgt, g, name, *, tm=512):
    S = x3.shape[0]
    tm = _tile(S, tm, 16)

    def body(x_ref, t_ref, g_ref, dx_ref, loss_ref, dg_ref):
        @pl.when(pl.program_id(0) == 0)
        def _():
            loss_ref[...] = jnp.zeros_like(loss_ref)
            dg_ref[...] = jnp.zeros_like(dg_ref)
        x = x_ref[...]
        gv = g_ref[...]
        r = lax.rsqrt(jnp.mean(x * x, axis=-1, keepdims=True) + EPS)
        xn = x * r
        err = xn * gv - t_ref[...]
        loss_ref[...] += jnp.full(loss_ref.shape, 0.5 * jnp.sum(jnp.mean(err * err, axis=-1, keepdims=True)), F32)
        dy = err * (1.0 / D)
        dg_ref[...] += jnp.sum(dy * xn, axis=0, keepdims=True)
        dyg = dy * gv
        dx_ref[...] = r * (dyg - xn * jnp.mean(dyg * xn, axis=-1, keepdims=True))

    row = pl.BlockSpec((tm, D), lambda i: (i, 0))
    vec = pl.BlockSpec((1, D), lambda i: (0, 0))
    dx3, loss, dg = pl.pallas_call(
        body, name=name, grid=(S // tm,), in_specs=[row, row, vec],
        out_specs=(row, pl.BlockSpec((1, 128), lambda i: (0, 0)), vec),
        out_shape=(jax.ShapeDtypeStruct((S, D), F32), jax.ShapeDtypeStruct((1, 128), F32), jax.ShapeDtypeStruct((1, D), F32)),
        compiler_params=_params(("arbitrary",)))(x3, tgt, g)
    return loss, dx3, dg


MASKED, FIRST, LAST = 1, 2, 4


def _att_tiles(name, S):
    tq, tk = ATT_TILES[name]
    return min(tq, S), min(tk, S)


def _pairs(S, tq, tk, descending=True):
    assert tk % tq == 0 and S % tk == 0
    qi, kj, fl = [], [], []
    for i in range(S // tq):
        last = ((i + 1) * tq - 1) // tk
        order = list(range(last, -1, -1) if descending else range(last + 1))
        for pos, kb in enumerate(order):
            qi.append(i)
            kj.append(kb)
            fl.append((MASKED if (kb + 1) * tk - 1 > i * tq else 0) | (FIRST if pos == 0 else 0) | (LAST if pos == last else 0))
    return tuple(jnp.asarray(np.asarray(a, np.int32)) for a in (qi, kj, fl))


def _att_specs(tq, tk, width=HD):
    qblk = pl.BlockSpec((1, tq, width), lambda h, n, qi, kj, fl: (h, qi[n], 0))
    kblk = pl.BlockSpec((1, tk, width), lambda h, n, qi, kj, fl: (h, kj[n], 0))
    qcol = pl.BlockSpec((1, tq, 1), lambda h, n, qi, kj, fl: (h, qi[n], 0))
    krow = pl.BlockSpec((1, 1, tk), lambda h, n, qi, kj, fl: (h, 0, kj[n]))
    return qblk, kblk, qcol, krow


def _causal(tq, w, ahead, strict):
    diff = lax.broadcasted_iota(jnp.int32, (tq, w), 1) - lax.broadcasted_iota(jnp.int32, (tq, w), 0)
    return diff < ahead if strict else diff <= ahead


def _masked_or_not(flags, step):
    pl.when(flags % 2 == 1)(functools.partial(step, True))
    pl.when(flags % 2 == 0)(functools.partial(step, False))


def _fox_fwd(q, k, v, cq, ck, name):
    S = q.shape[1]
    tq, tk = _att_tiles("fox_fwd", S)
    qi, kj, fl = _pairs(S, tq, tk)
    qblk, kblk, qcol, krow = _att_specs(tq, tk)

    def body(qi_ref, kj_ref, fl_ref, q_ref, k_ref, v_ref, cq_ref, ck_ref, o_ref, lse_ref, m_s, l_s, acc_s):
        n = pl.program_id(1)
        i, kb, flags = qi_ref[n], kj_ref[n], fl_ref[n]

        @pl.when(flags & FIRST != 0)
        def _():
            m_s[...] = jnp.full_like(m_s, NEG)
            l_s[...] = jnp.zeros_like(l_s)
            acc_s[...] = jnp.zeros_like(acc_s)

        def step(masked):
            s = _dot_nt(q_ref[0] * ATT_SCALE, k_ref[0]) + cq_ref[0] - ck_ref[0]
            if masked:
                s = jnp.where(_causal(tq, tk, i * tq - kb * tk, False), s, NEG)
            m_new = jnp.maximum(m_s[...], jnp.max(s, axis=-1, keepdims=True))
            alpha = jnp.exp(m_s[...] - m_new)
            p = jnp.exp(s - m_new)
            l_s[...] = alpha * l_s[...] + jnp.sum(p, axis=-1, keepdims=True)
            acc_s[...] = alpha * acc_s[...] + _dot(p.astype(CDT), v_ref[0])
            m_s[...] = m_new

        _masked_or_not(flags, step)

        @pl.when(flags & LAST != 0)
        def _():
            o_ref[0] = acc_s[...] / l_s[...]
            lse_ref[0] = m_s[...] + jnp.log(l_s[...])

    grid_spec = pltpu.PrefetchScalarGridSpec(
        num_scalar_prefetch=3, grid=(NH, int(qi.shape[0])),
        in_specs=[qblk, kblk, kblk, qcol, krow], out_specs=(qblk, qcol),
        scratch_shapes=[pltpu.VMEM((tq, 1), F32), pltpu.VMEM((tq, 1), F32), pltpu.VMEM((tq, HD), F32)])
    return pl.pallas_call(body, name=name, grid_spec=grid_spec,
                          out_shape=(jax.ShapeDtypeStruct((NH, S, HD), F32), jax.ShapeDtypeStruct((NH, S, 1), F32)),
                          compiler_params=_params(("parallel", "arbitrary")))(qi, kj, fl, q, k, v, cq, ck)


def _fox_bwd(q, k, v, cq, ck, o, do, lse, name):
    S = q.shape[1]
    tq, tk = _att_tiles("fox_bwd", S)
    qi, kj, fl = _pairs(S, tq, tk)
    qblk, kblk, qcol, krow = _att_specs(tq, tk)

    def body(qi_ref, kj_ref, fl_ref, q_ref, k_ref, v_ref, cq_ref, ck_ref, o_ref, do_ref, lse_ref,
             dq_ref, dk_ref, dv_ref, dck_ref, dcq_ref, dq_s, dl_s, dcq_s):
        n = pl.program_id(1)
        i, kb, flags = qi_ref[n], kj_ref[n], fl_ref[n]

        @pl.when(n == 0)
        def _():
            dk_ref[...] = jnp.zeros_like(dk_ref)
            dv_ref[...] = jnp.zeros_like(dv_ref)
            dck_ref[...] = jnp.zeros_like(dck_ref)

        @pl.when(flags & FIRST != 0)
        def _():
            dq_s[...] = jnp.zeros_like(dq_s)
            dcq_s[...] = jnp.zeros_like(dcq_s)
            dl_s[...] = jnp.sum(do_ref[0].astype(F32) * o_ref[0], axis=-1, keepdims=True)

        def step(masked):
            qs = q_ref[0] * ATT_SCALE
            do = do_ref[0]
            p = jnp.exp(_dot_nt(qs, k_ref[0]) + cq_ref[0] - ck_ref[0] - lse_ref[0])
            if masked:
                p = jnp.where(_causal(tq, tk, i * tq - kb * tk, False), p, 0.0)
            ds = p * (_dot_nt(do, v_ref[0]) - dl_s[...])
            dsb = ds.astype(CDT)
            dq_s[...] += _dot(dsb, k_ref[0])
            rows = pl.ds(pl.multiple_of(kb * tk, tk), tk)
            dk_ref[0, rows, :] += _dot_tn(dsb, qs)
            dv_ref[0, rows, :] += _dot_tn(p.astype(CDT), do)
            dck_ref[0, :, rows] += -jnp.sum(ds, axis=0, keepdims=True)
            dcq_s[...] += jnp.sum(ds, axis=-1, keepdims=True)

        _masked_or_not(flags, step)

        @pl.when(flags & LAST != 0)
        def _():
            dq_ref[0] = (dq_s[...] * ATT_SCALE).astype(dq_ref.dtype)
            dcq_ref[0] = dcq_s[...]

    whole = pl.BlockSpec((1, S, HD), lambda h, n, qi, kj, fl: (h, 0, 0))
    grid_spec = pltpu.PrefetchScalarGridSpec(
        num_scalar_prefetch=3, grid=(NH, int(qi.shape[0])),
        in_specs=[qblk, kblk, kblk, qcol, krow, qblk, qblk, qcol],
        out_specs=(qblk, whole, whole, pl.BlockSpec((1, 1, S), lambda h, n, qi, kj, fl: (h, 0, 0)), qcol),
        scratch_shapes=[pltpu.VMEM((tq, HD), F32), pltpu.VMEM((tq, 1), F32), pltpu.VMEM((tq, 1), F32)])
    return pl.pallas_call(body, name=name, grid_spec=grid_spec,
                          out_shape=(jax.ShapeDtypeStruct((NH, S, HD), CDT), jax.ShapeDtypeStruct((NH, S, HD), F32),
                                     jax.ShapeDtypeStruct((NH, S, HD), F32), jax.ShapeDtypeStruct((NH, 1, S), F32),
                                     jax.ShapeDtypeStruct((NH, S, 1), F32)),
                          compiler_params=_params(("parallel", "arbitrary")))(qi, kj, fl, q, k, v, cq, ck, o, do, lse)


LOG2E = 1.4426950408889634


def _log2_operands(q, k):
    qf = q.astype(F32) * (ATT_SCALE * LOG2E)
    hi = lax.reduce_precision(qf, 8, 7)
    return jnp.concatenate([hi.astype(CDT), (qf - hi).astype(CDT)], axis=-1), jnp.concatenate([k, k], axis=-1)


def _sb_softplus2(q2, k2sub, mask):
    z2 = _dot_nt(q2, k2sub)
    sp2 = jnp.maximum(z2, 0.0) + jnp.log2(1.0 + jnp.exp2(-jnp.abs(z2)))
    return z2, sp2 if mask is None else jnp.where(mask, sp2, 0.0)


def _strict_tri(n, upper, value):
    r = lax.broadcasted_iota(jnp.int32, (n, n), 0)
    c = lax.broadcasted_iota(jnp.int32, (n, n), 1)
    return jnp.where(r < c if upper else r > c, value, 0.0).astype(CDT)


def _sb_fwd(q2, k2, v, name):
    S = q2.shape[1]
    tq, tk = _att_tiles("sb_fwd", S)
    W = min(W_SB, tk)
    qi, kj, fl = _pairs(S, tq, tk)
    qblk, kblk, qcol, _ = _att_specs(tq, tk)
    q2blk, k2blk, _, _ = _att_specs(tq, tk, 2 * HD)

    def body(qi_ref, kj_ref, fl_ref, q_ref, k_ref, v_ref, o_ref, lt_ref, run_s, acc_s):
        n = pl.program_id(1)
        i, kb, flags = qi_ref[n], kj_ref[n], fl_ref[n]

        @pl.when(flags & FIRST != 0)
        def _():
            run_s[...] = jnp.zeros_like(run_s)
            acc_s[...] = jnp.zeros_like(acc_s)

        def step(masked):
            neg_later = _strict_tri(W, False, -1.0)
            run = run_s[...]
            acc = acc_s[...]
            for sub in range(tk // W - 1, -1, -1):
                cols = slice(sub * W, (sub + 1) * W)
                mask = _causal(tq, W, i * tq - kb * tk - sub * W, True) if masked else None
                z2, sp2 = _sb_softplus2(q_ref[0], k_ref[0, cols, :], mask)
                excl = _dot(sp2.astype(CDT), neg_later)
                a = jnp.exp2((z2 - sp2) + (excl + run))
                if masked:
                    a = jnp.where(mask, a, 0.0)
                acc = acc + _dot(a.astype(CDT), v_ref[0, cols, :])
                run = run + (excl[:, 0:1] - sp2[:, 0:1])
            run_s[...] = run
            acc_s[...] = acc

        _masked_or_not(flags, step)

        @pl.when(flags & LAST != 0)
        def _():
            o_ref[0] = acc_s[...]
            lt_ref[0] = run_s[...]

    grid_spec = pltpu.PrefetchScalarGridSpec(
        num_scalar_prefetch=3, grid=(NH, int(qi.shape[0])), in_specs=[q2blk, k2blk, kblk], out_specs=(qblk, qcol),
        scratch_shapes=[pltpu.VMEM((tq, 1), F32), pltpu.VMEM((tq, HD), F32)])
    return pl.pallas_call(body, name=name, grid_spec=grid_spec,
                          out_shape=(jax.ShapeDtypeStruct((NH, S, HD), F32), jax.ShapeDtypeStruct((NH, S, 1), F32)),
                          compiler_params=_params(("parallel", "arbitrary")))(qi, kj, fl, q2, k2, v)


def _sb_bwd(q, q2, k2, v, do, lt, name):
    S = q.shape[1]
    tq, tk = _att_tiles("sb_bwd", S)
    W = min(W_SB, tk)
    qi, kj, fl = _pairs(S, tq, tk, descending=False)
    qblk, kblk, qcol, _ = _att_specs(tq, tk)
    q2blk, k2blk, _, _ = _att_specs(tq, tk, 2 * HD)

    def body(qi_ref, kj_ref, fl_ref, q_ref, q2_ref, k2_ref, v_ref, do_ref, lt_ref, dq_ref, dk_ref, dv_ref, passed_s, gsum_s, dq_s):
        n = pl.program_id(1)
        i, kb, flags = qi_ref[n], kj_ref[n], fl_ref[n]

        @pl.when(n == 0)
        def _():
            dk_ref[...] = jnp.zeros_like(dk_ref)
            dv_ref[...] = jnp.zeros_like(dv_ref)

        @pl.when(flags & FIRST != 0)
        def _():
            passed_s[...] = jnp.zeros_like(passed_s)
            gsum_s[...] = jnp.zeros_like(gsum_s)
            dq_s[...] = jnp.zeros_like(dq_s)

        def step(masked):
            qs = q_ref[0] * ATT_SCALE
            do = do_ref[0]
            neg_later = _strict_tri(W, False, -1.0)
            earlier = _strict_tri(W, True, 1.0)
            for sub in range(tk // W):
                cols = slice(sub * W, (sub + 1) * W)
                mask = _causal(tq, W, i * tq - kb * tk - sub * W, True) if masked else None
                ksub = k2_ref[0, cols, 0:HD]
                z2, sp2 = _sb_softplus2(q2_ref[0], k2_ref[0, cols, :], mask)
                excl = _dot(sp2.astype(CDT), neg_later)
                through = passed_s[...] + (excl[:, 0:1] - sp2[:, 0:1])
                t1 = z2 - sp2
                sig = jnp.exp2(t1)
                a = jnp.exp2(t1 + (excl + (lt_ref[0] - through)))
                if masked:
                    a = jnp.where(mask, a, 0.0)
                dl = _dot_nt(do, v_ref[0, cols, :]) * a
                before = _dot(dl.astype(CDT), earlier)
                dz = dl - sig * (dl + (before + gsum_s[...]))
                if masked:
                    dz = jnp.where(mask, dz, 0.0)
                dzb = dz.astype(CDT)
                dq_s[...] += _dot(dzb, ksub)
                rows = pl.ds(pl.multiple_of(kb * tk + sub * W, W), W)
                dk_ref[0, rows, :] += _dot_tn(dzb, qs)
                dv_ref[0, rows, :] += _dot_tn(a.astype(CDT), do)
                passed_s[...] = through
                gsum_s[...] += before[:, W - 1:W] + dl[:, W - 1:W]

        _masked_or_not(flags, step)

        @pl.when(flags & LAST != 0)
        def _():
            dq_ref[0] = (dq_s[...] * ATT_SCALE).astype(dq_ref.dtype)

    whole = pl.BlockSpec((1, S, HD), lambda h, n, qi, kj, fl: (h, 0, 0))
    grid_spec = pltpu.PrefetchScalarGridSpec(
        num_scalar_prefetch=3, grid=(NH, int(qi.shape[0])), in_specs=[qblk, q2blk, k2blk, kblk, qblk, qcol],
        out_specs=(qblk, whole, whole),
        scratch_shapes=[pltpu.VMEM((tq, 1), F32), pltpu.VMEM((tq, 1), F32), pltpu.VMEM((tq, HD), F32)])
    return pl.pallas_call(body, name=name, grid_spec=grid_spec,
                          out_shape=(jax.ShapeDtypeStruct((NH, S, HD), CDT), jax.ShapeDtypeStruct((NH, S, HD), F32),
                                     jax.ShapeDtypeStruct((NH, S, HD), F32)),
                          compiler_params=_params(("parallel", "arbitrary")))(qi, kj, fl, q, q2, k2, v, do, lt)


def _mem_probs(q_ref, kv_ref, h):
    cols = slice(h * MHD, (h + 1) * MHD)
    s = _dot_nt(q_ref[:, cols], kv_ref[:, cols]) * MEM_SCALE
    e = jnp.exp(s - jnp.max(s, axis=-1, keepdims=True))
    return e / jnp.sum(e, axis=-1, keepdims=True)


def _xattn_fwd(q, kv, w_mo, x1, name, *, tm=512):
    S = q.shape[0]
    tm = _tile(S, tm, 16)
    nm = kv.shape[0]

    def body(q_ref, kv_ref, w_ref, x_ref, x2_ref, o_ref):
        for h in range(NMH):
            p = _mem_probs(q_ref, kv_ref, h)
            o_ref[:, h * MHD:(h + 1) * MHD] = _dot(p.astype(CDT), kv_ref[:, D + h * MHD:D + (h + 1) * MHD]).astype(CDT)
        x2_ref[...] = x_ref[...] + _dot(o_ref[...], w_ref[...])

    row = pl.BlockSpec((tm, D), lambda i: (i, 0))
    return pl.pallas_call(body, name=name, grid=(S // tm,),
                          in_specs=[row, pl.BlockSpec((nm, 2 * D), lambda i: (0, 0)), pl.BlockSpec((D, D), lambda i: (0, 0)), row],
                          out_specs=(row, row),
                          out_shape=(jax.ShapeDtypeStruct((S, D), F32), jax.ShapeDtypeStruct((S, D), CDT)),
                          compiler_params=_params(("parallel",)))(q, kv, w_mo, x1)


def _xattn_bwd(q, kv, do, name, *, tm=512):
    S = q.shape[0]
    tm = _tile(S, tm, 16)
    nm = kv.shape[0]

    def body(q_ref, kv_ref, do_ref, dq_ref, dkv_ref):
        @pl.when(pl.program_id(0) == 0)
        def _():
            dkv_ref[...] = jnp.zeros_like(dkv_ref)
        for h in range(NMH):
            cols = slice(h * MHD, (h + 1) * MHD)
            vcols = slice(D + h * MHD, D + (h + 1) * MHD)
            p = _mem_probs(q_ref, kv_ref, h)
            doh = do_ref[:, cols]
            dp = _dot_nt(doh, kv_ref[:, vcols])
            ds = (p * (dp - jnp.sum(p * dp, axis=-1, keepdims=True)) * MEM_SCALE).astype(CDT)
            dq_ref[:, cols] = _dot(ds, kv_ref[:, cols]).astype(CDT)
            dkv_ref[:, cols] += _dot_tn(ds, q_ref[:, cols])
            dkv_ref[:, vcols] += _dot_tn(p.astype(CDT), doh)

    row = pl.BlockSpec((tm, D), lambda i: (i, 0))
    kvs = pl.BlockSpec((nm, 2 * D), lambda i: (0, 0))
    return pl.pallas_call(body, name=name, grid=(S // tm,), in_specs=[row, kvs, row], out_specs=(row, kvs),
                          out_shape=(jax.ShapeDtypeStruct((S, D), CDT), jax.ShapeDtypeStruct((nm, 2 * D), F32)),
                          compiler_params=_params(("arbitrary",)))(q, kv, do)


HALO = 16
SLAB = 8


def _shift_down(u, prev, s):
    rolled = pltpu.roll(u, s, 0)
    top = rolled[0:SLAB]
    r = lax.broadcasted_iota(jnp.int32, top.shape, 0)
    for t in range(s):
        top = jnp.where(r == t, prev[HALO - s + t:HALO - s + t + 1, :], top)
    return jnp.concatenate([top, rolled[SLAB:]], axis=0)


def _shift_up(u, nxt, s):
    n = u.shape[0]
    rolled = pltpu.roll(u, n - s, 0)
    bottom = rolled[n - SLAB:]
    r = lax.broadcasted_iota(jnp.int32, bottom.shape, 0)
    for t in range(s):
        bottom = jnp.where(r == SLAB - s + t, nxt[t:t + 1, :], bottom)
    return jnp.concatenate([rolled[:n - SLAB], bottom], axis=0)


def _conv_taps(u_ref, h_ref, first):
    u = u_ref[...].astype(F32)
    prev = jnp.where(first, 0.0, h_ref[...].astype(F32))
    out = []
    for half in range(2):
        out.append((u[half], _shift_down(u[half], prev[half], 1), _shift_down(u[half], prev[half], 2)))
    return out


def _conv_specs(tm, tn, nsb):
    blk = pl.BlockSpec((2, tm, tn), lambda j, i: (0, i, j))
    prev = pl.BlockSpec((2, HALO, tn), lambda j, i: (0, jnp.maximum(i * (tm // HALO) - 1, 0), j))
    nxt = pl.BlockSpec((2, HALO, tn), lambda j, i: (0, jnp.minimum((i + 1) * (tm // HALO), nsb - 1), j))
    w = pl.BlockSpec((2, 3, tn), lambda j, i: (0, 0, j))
    b = pl.BlockSpec((2, 1, tn), lambda j, i: (0, 0, j))
    return blk, prev, nxt, w, b


def _conv_apply(taps, w_ref, b_ref):
    ys = []
    for half in range(2):
        u, u1, u2 = taps[half]
        w = w_ref[half]
        ys.append(b_ref[half] + u2 * w[0:1, :] + u1 * w[1:2, :] + u * w[2:3, :])
    return ys


def _conv_act(u0, cw, cb, name, *, tm=2048, tn=256):
    _, S, F = u0.shape
    tm = _tile(S, tm, HALO)
    tn = _tile(F, tn, 128)
    blk, prev, _, w, b = _conv_specs(tm, tn, S // HALO)

    def body(u_ref, h_ref, w_ref, b_ref, a_ref):
        yg, yv = _conv_apply(_conv_taps(u_ref, h_ref, pl.program_id(1) == 0), w_ref, b_ref)
        a_ref[...] = (yg * jax.nn.sigmoid(yg) * yv).astype(a_ref.dtype)

    return pl.pallas_call(body, name=name, grid=(F // tn, S // tm), in_specs=[blk, prev, w, b],
                          out_specs=pl.BlockSpec((tm, tn), lambda j, i: (i, j)),
                          out_shape=jax.ShapeDtypeStruct((S, F), CDT),
                          compiler_params=_params(("parallel", "parallel")))(u0, u0, cw, cb)


def _conv_act_bwd(u0, da, cw, cb, name, *, tm=2048, tn=256):
    _, S, F = u0.shape
    tm = _tile(S, tm, HALO)
    tn = _tile(F, tn, 128)
    blk, prev, _, w, b = _conv_specs(tm, tn, S // HALO)

    def body(u_ref, h_ref, da_ref, w_ref, b_ref, du_ref, dwb_ref):
        @pl.when(pl.program_id(1) == 0)
        def _():
            dwb_ref[...] = jnp.zeros_like(dwb_ref)
        taps = _conv_taps(u_ref, h_ref, pl.program_id(1) == 0)
        yg, yv = _conv_apply(taps, w_ref, b_ref)
        sg = jax.nn.sigmoid(yg)
        da = da_ref[...].astype(F32)
        dus = (da * yv * sg * (1.0 + yg * (1.0 - sg)), da * yg * sg)
        for half in range(2):
            du = dus[half]
            du_ref[half] = du.astype(du_ref.dtype)
            u, u1, u2 = taps[half]
            for row, term in enumerate((du * u2, du * u1, du * u, du)):
                dwb_ref[half, row:row + 1, :] += jnp.sum(term, axis=0, keepdims=True)

    return pl.pallas_call(body, name=name, grid=(F // tn, S // tm),
                          in_specs=[blk, prev, pl.BlockSpec((tm, tn), lambda j, i: (i, j)), w, b],
                          out_specs=(blk, pl.BlockSpec((2, 4, tn), lambda j, i: (0, 0, j))),
                          out_shape=(jax.ShapeDtypeStruct((2, S, F), CDT), jax.ShapeDtypeStruct((2, 4, F), F32)),
                          compiler_params=_params(("parallel", "arbitrary")))(u0, u0, da, cw, cb)


def _conv_bwd_input(du, cw, name, *, tm=2048, tn=256):
    _, S, F = du.shape
    tm = _tile(S, tm, HALO)
    tn = _tile(F, tn, 128)
    blk, _, nxt, w, _ = _conv_specs(tm, tn, S // HALO)
    ni = S // tm

    def body(d_ref, h_ref, w_ref, o_ref):
        d = d_ref[...].astype(F32)
        nx = jnp.where(pl.program_id(1) == ni - 1, 0.0, h_ref[...].astype(F32))
        for half in range(2):
            wv = w_ref[half]
            y = d[half] * wv[2:3, :] + _shift_up(d[half], nx[half], 1) * wv[1:2, :] + _shift_up(d[half], nx[half], 2) * wv[0:1, :]
            o_ref[half] = y.astype(o_ref.dtype)

    return pl.pallas_call(body, name=name, grid=(F // tn, ni), in_specs=[blk, nxt, w], out_specs=blk,
                          out_shape=jax.ShapeDtypeStruct((2, S, F), CDT),
                          compiler_params=_params(("parallel", "parallel")))(du, du, cw)


ANY = pl.BlockSpec(memory_space=pl.ANY)


def _place():
    return lax.axis_index("x"), lax.axis_index("y"), lax.axis_index("c")


def _other_chips(x, y):
    return ((1 - x, y), (x, 1 - y), (1 - x, 1 - y))


def _gather_weights(buf):
    n, rows, _ = buf.shape
    half = rows // 2

    def body(buf_ref, out_ref, send_sems, recv_sems):
        del buf_ref
        x, y, c = _place()
        chips = _other_chips(x, y)

        def part(chip, pc):
            return out_ref.at[2 * chip[0] + chip[1], pl.ds(pl.multiple_of(pc * half, 16), half), :]

        def copy(k, chip, pc, to):
            return pltpu.make_async_remote_copy(src_ref=part(chip, pc), dst_ref=part(chip, pc),
                                                send_sem=send_sems.at[k], recv_sem=recv_sems.at[k],
                                                device_id=to, device_id_type=MESH)

        first = [copy(j, (x, y), c, (*chip, c)) for j, chip in enumerate(chips)]
        for cp in first:
            cp.start()
        passed = [copy(3 + j, chip, c, (x, y, 1 - c)) for j, chip in enumerate(chips)]
        for j, chip in enumerate(chips):
            copy(j, chip, c, (x, y, c)).wait_recv()
            passed[j].start()
        for j, chip in enumerate(chips):
            copy(3 + j, chip, 1 - c, (x, y, c)).wait_recv()
        for cp in first + passed:
            cp.wait_send()

    return pl.pallas_call(body, name="gather_weights", in_specs=[ANY], out_specs=ANY,
                          out_shape=jax.ShapeDtypeStruct(buf.shape, buf.dtype), input_output_aliases={0: 0},
                          scratch_shapes=[pltpu.SemaphoreType.DMA((6,)), pltpu.SemaphoreType.DMA((6,))])(buf)


def _gather_small(v):
    m = v.shape[0]

    def body(v_ref, out_ref, send_sems, recv_sems, local_sem):
        x, y, c = _place()
        me, sibling = (x, y, c), (x, y, 1 - c)
        chips = _other_chips(x, y)

        def rows(px, py, pc):
            return out_ref.at[pl.ds((4 * px + 2 * py + pc) * m, m), :]

        def copy(k, block, to, src=None):
            return pltpu.make_async_remote_copy(src_ref=rows(*block) if src is None else src, dst_ref=rows(*block),
                                                send_sem=send_sems.at[k], recv_sem=recv_sems.at[k],
                                                device_id=to, device_id_type=MESH)

        mine = pltpu.make_async_copy(v_ref, rows(*me), local_sem)
        mine.start()
        first = [copy(0, me, sibling, src=v_ref)]
        first += [copy(1 + j, me, (*chip, c), src=v_ref) for j, chip in enumerate(chips)]
        for cp in first:
            cp.start()
        passed = [copy(4 + j, (*chip, c), sibling) for j, chip in enumerate(chips)]
        for j, chip in enumerate(chips):
            copy(1 + j, (*chip, c), me).wait_recv()
            passed[j].start()
        copy(0, sibling, me).wait_recv()
        for j, chip in enumerate(chips):
            copy(4 + j, (*chip, 1 - c), me).wait_recv()
        for cp in first + passed:
            cp.wait_send()
        mine.wait()

    vm = pl.BlockSpec(memory_space=pltpu.VMEM)
    return pl.pallas_call(body, name="gather_small", in_specs=[vm], out_specs=vm,
                          out_shape=jax.ShapeDtypeStruct((8 * m, 128), v.dtype),
                          scratch_shapes=[pltpu.SemaphoreType.DMA((7,)), pltpu.SemaphoreType.DMA((7,)), pltpu.SemaphoreType.DMA])(v)


def _swap_halves(g):
    n, rows, _ = g.shape
    half = rows // 2

    def body(g_ref, out_ref, send_sem, recv_sem):
        x, y, c = _place()
        src = g_ref.at[:, pl.ds(pl.multiple_of((1 - c) * half, 8), half), :]
        cp = pltpu.make_async_remote_copy(src_ref=src, dst_ref=out_ref, send_sem=send_sem, recv_sem=recv_sem,
                                          device_id=(x, y, 1 - c), device_id_type=MESH)
        cp.start()
        cp.wait()

    return pl.pallas_call(body, name="swap_halves", in_specs=[ANY], out_specs=ANY,
                          out_shape=jax.ShapeDtypeStruct((n, half, 128), g.dtype),
                          scratch_shapes=[pltpu.SemaphoreType.DMA, pltpu.SemaphoreType.DMA])(g)


def _scatter_chips(hsum):
    n, half, _ = hsum.shape

    def body(h_ref, out_ref, send_sems, recv_sems):
        x, y, c = _place()
        cps = []
        for j, chip in enumerate(_other_chips(x, y)):
            cp = pltpu.make_async_remote_copy(src_ref=h_ref.at[2 * chip[0] + chip[1]], dst_ref=out_ref.at[j],
                                              send_sem=send_sems.at[j], recv_sem=recv_sems.at[j],
                                              device_id=(*chip, c), device_id_type=MESH)
            cp.start()
            cps.append(cp)
        for cp in cps:
            cp.wait()

    return pl.pallas_call(body, name="scatter_chips", in_specs=[ANY], out_specs=ANY,
                          out_shape=jax.ShapeDtypeStruct((3, half, 128), hsum.dtype),
                          scratch_shapes=[pltpu.SemaphoreType.DMA((3,)), pltpu.SemaphoreType.DMA((3,))])(hsum)


def _join_halves(buf):
    half = buf.shape[0] // 2

    def body(buf_ref, out_ref, send_sem, recv_sem):
        del buf_ref
        x, y, c = _place()
        mine = out_ref.at[pl.ds(pl.multiple_of(c * half, 8), half), :]
        other = out_ref.at[pl.ds(pl.multiple_of((1 - c) * half, 8), half), :]
        cp = pltpu.make_async_remote_copy(src_ref=mine, dst_ref=mine, send_sem=send_sem, recv_sem=recv_sem,
                                          device_id=(x, y, 1 - c), device_id_type=MESH)
        cp.start()
        cp.wait_send()
        pltpu.make_async_remote_copy(src_ref=other, dst_ref=other, send_sem=send_sem, recv_sem=recv_sem,
                                     device_id=(x, y, 1 - c), device_id_type=MESH).wait_recv()

    return pl.pallas_call(body, name="join_halves", in_specs=[ANY], out_specs=ANY,
                          out_shape=jax.ShapeDtypeStruct(buf.shape, buf.dtype), input_output_aliases={0: 0},
                          scratch_shapes=[pltpu.SemaphoreType.DMA, pltpu.SemaphoreType.DMA])(buf)


def _add_sibling(g, recv, c_idx, name):
    n, rows, _ = g.shape
    half = rows // 2
    tr = _tile(half, ADAM_ROWS, 16)
    nb = half // tr

    def body(c_ref, g_ref, r_ref, o_ref, ob_ref):
        s = g_ref[...] + r_ref[...]
        o_ref[...] = s
        ob_ref[...] = s.astype(CDT)

    out = pl.BlockSpec((None, tr, 128), lambda k, i, c: (k, i, 0))
    grid_spec = pltpu.PrefetchScalarGridSpec(
        num_scalar_prefetch=1, grid=(n, nb),
        in_specs=[pl.BlockSpec((None, tr, 128), lambda k, i, c: (k, c[0] * nb + i, 0)), out],
        out_specs=(out, out))
    return pl.pallas_call(body, name=name, grid_spec=grid_spec,
                          out_shape=(jax.ShapeDtypeStruct((n, half, 128), F32), jax.ShapeDtypeStruct((n, half, 128), CDT)),
                          compiler_params=_params(("parallel", "parallel")))(c_idx, g, recv)


def _add_chips(hsum, recv, chip_idx, name):
    n, half, _ = hsum.shape
    tr = _tile(half, ADAM_ROWS, 16)

    def body(k_ref, h_ref, r_ref, o_ref):
        o_ref[...] = ((h_ref[...] + r_ref[0].astype(F32)) + r_ref[1].astype(F32)) + r_ref[2].astype(F32)

    grid_spec = pltpu.PrefetchScalarGridSpec(
        num_scalar_prefetch=1, grid=(half // tr,),
        in_specs=[pl.BlockSpec((None, tr, 128), lambda i, k: (k[0], i, 0)),
                  pl.BlockSpec((3, tr, 128), lambda i, k: (0, i, 0))],
        out_specs=pl.BlockSpec((tr, 128), lambda i, k: (i, 0)))
    return pl.pallas_call(body, name=name, grid_spec=grid_spec, out_shape=jax.ShapeDtypeStruct((half, 128), F32),
                          compiler_params=_params(("parallel",)))(chip_idx, hsum, recv)


def _adamw_math(g, w, m, v):
    m2 = B1 * m + (1.0 - B1) * g
    v2 = B2 * v + (1.0 - B2) * (g * g)
    delta = -LR * ((m2 / BC1) / (jnp.sqrt(v2 / BC2) + AEPS) + WD * w)
    return delta, m2, v2


def _adamw(g, w, m, v, name):
    rows, cols = g.shape
    tr = _tile(rows, max(8, (ADAM_ROWS * 128 // cols) // 8 * 8), 8)

    def body(g_ref, w_ref, m_ref, v_ref, d_ref, m2_ref, v2_ref):
        d_ref[...], m2_ref[...], v2_ref[...] = _adamw_math(g_ref[...], w_ref[...], m_ref[...], v_ref[...])

    blk = pl.BlockSpec((tr, cols), lambda i: (i, 0))
    shp = jax.ShapeDtypeStruct((rows, cols), F32)
    return pl.pallas_call(body, name=name, grid=(rows // tr,), in_specs=[blk] * 4, out_specs=(blk,) * 3,
                          out_shape=(shp,) * 3, compiler_params=_params(("parallel",)))(g, w, m, v)


def _adamw_small(parts, w, m, v, name):
    rows = w.shape[0]

    def body(p_ref, w_ref, m_ref, v_ref, g_ref, d_ref, m2_ref, v2_ref):
        g = p_ref[0]
        for k in range(1, 8):
            g = g + p_ref[k]
        g_ref[...] = g
        d_ref[...], m2_ref[...], v2_ref[...] = _adamw_math(g, w_ref[...], m_ref[...], v_ref[...])

    shp = jax.ShapeDtypeStruct((rows, 128), F32)
    return pl.pallas_call(body, name=name, out_shape=(shp,) * 4)(parts, w, m, v)


def _pack_rows(parts, rows):
    flat = jnp.concatenate([p.reshape(-1) for p in parts])
    return jnp.pad(flat, (0, rows * 128 - flat.shape[0])).reshape(rows, 128)


def _unpack(flat, sizes, shapes):
    out, off = [], 0
    for n, s in zip(sizes, shapes):
        out.append(flat[off:off + n].reshape(s))
        off += n
    return out


def _to_shards(full, shard_shape, axis):
    if axis == 0:
        return full.reshape(N_CHIP, -1)
    r, cs = shard_shape
    return full.reshape(r, N_CHIP, cs).transpose(1, 0, 2).reshape(N_CHIP, -1)


def _from_shards(sh, shard_shape, axis):
    r, cs = shard_shape
    if axis == 0:
        return sh.reshape(N_CHIP * r, cs)
    return sh.reshape(N_CHIP, r, cs).transpose(1, 0, 2).reshape(r, N_CHIP * cs)


def _heads(t, n):
    s = t.shape[0]
    return t.reshape(s, n * NH, HD).transpose(1, 0, 2)


def _merge(t):
    return t.transpose(1, 0, 2).reshape(t.shape[1], GW)


def _local_step(x0, mem, tgt, W, gains):
    S = x0.shape[0]
    w_in = jnp.pad(W["w_in"], ((0, 0), (0, IN_PAD - IN_COLS)))
    b_f = jnp.pad(gains["b_forget"], ((0, 0), (0, 128 - NH)))
    cw = W["conv_w"].reshape(3, 2, DFF).transpose(1, 0, 2)
    cb = gains["conv_b"].reshape(2, 1, DFF)

    h1 = _rms_cast(x0, gains["attn_norm_g"], "norm_attn")
    qkv = _heads(_mm_nn(h1, w_in[:, :NQKV], CDT, "proj_qkv"), 6)
    fl = _mm_nn(h1, w_in[:, NQKV:NQKV + 128], F32, "proj_gate")
    cum = _gate_fwd(fl, b_f, "gate_cumsum")
    c_hm = cum[:, :NH].T
    cq, ck = c_hm[:, :, None], c_hm[:, None, :]
    fq, fk, fv, sq, sk, sv = (qkv[n * NH:(n + 1) * NH] for n in range(6))
    fo_h, lse = _fox_fwd(fq, fk, fv, cq, ck, "fox_fwd")
    sq2, sk2 = _log2_operands(sq, sk)
    so_h, s_lt = _sb_fwd(sq2, sk2, sv, "sb_fwd")
    fo, so = _merge(fo_h), _merge(so_h)
    x1, mixed = _out_proj(fo, so, gains["fox_out_g"], gains["sb_out_g"], W["w_out"], x0, "out_proj")

    h2 = _rms_cast(x1, gains["xattn_norm_g"], "norm_xattn")
    mn = _rms_cast(mem, gains["mem_norm_g"], "norm_mem")
    mq = _mm_nn(h2, W["w_mq"], CDT, "proj_mq")
    kv = _mm_nn(mn, W["w_mkv"], CDT, "proj_mkv")
    x2, mo = _xattn_fwd(mq, kv, W["w_mo"], x1, "xattn_fwd")

    h3 = _rms_cast(x2, gains["ffn_norm_g"], "norm_ffn")
    u0 = _mm_nn(h3, W["w_up"], CDT, "ffn_up", tm=512, tn=DFF, halves=True)
    act = _conv_act(u0, cw, cb, "conv_act")
    x3 = _mm_nn(act, W["w_down"], F32, "ffn_down", tm=512, residual=x2)
    loss, dx3, dg_final = _loss_bwd(x3, tgt, gains["final_norm_g"].reshape(1, D), "loss")

    gw, gs = {}, {"final_norm_g": dg_final}
    da = _mm_nt(dx3, W["w_down"], "ffn_down_dx", tn=DFF, out_dtype=CDT)
    gw["w_down"] = _mm_tn(act, dx3, "ffn_down_dw", tka=DFF)
    du, dwb = _conv_act_bwd(u0, da, cw, cb, "conv_act_bwd")
    gw["conv_w"] = dwb[:, :3].transpose(1, 0, 2).reshape(3, 2 * DFF)
    gs["conv_b"] = dwb[:, 3].reshape(1, 2 * DFF)
    du0 = _conv_bwd_input(du, cw, "conv_bwd_input")
    gw["w_up"] = _mm_tn(h3, du0, "ffn_up_dw", tn=DFF, b_halves=True)
    dx2, gs["ffn_norm_g"] = _mm_nt_rmsbwd(du0, W["w_up"], x2, gains["ffn_norm_g"], dx3, "ffn_up_dx", tk=DFF, a_halves=True)

    dmo = _mm_nt(dx2, W["w_mo"], "mo_dx", tn=512, out_dtype=CDT)
    gw["w_mo"] = _mm_tn(mo, dx2, "mo_dw")
    dmq, dkv = _xattn_bwd(mq, kv, dmo, "xattn_bwd")
    gw["w_mq"] = _mm_tn(h2, dmq, "mq_dw")
    dx1, gs["xattn_norm_g"] = _mm_nt_rmsbwd(dmq, W["w_mq"], x1, gains["xattn_norm_g"], dx2, "mq_dx")
    gw["w_mkv"] = _mm_tn(mn, dkv, "mkv_dw")
    _, gs["mem_norm_g"] = _mm_nt_rmsbwd(dkv, W["w_mkv"], mem, gains["mem_norm_g"], jnp.zeros_like(mem), "mkv_dx")

    gw["w_out"] = _mm_tn(mixed, dx1, "out_dw")
    dfo, dso, gs["fox_out_g"], gs["sb_out_g"] = _out_proj_bwd(dx1, W["w_out"], fo, so, gains["fox_out_g"], gains["sb_out_g"], "out_dx")
    dfo_h, dso_h = _heads(dfo, 1), _heads(dso, 1)
    dfq, dfk, dfv, dck, dcq = _fox_bwd(fq, fk, fv, cq, ck, fo_h, dfo_h, lse, "fox_bwd")
    dsq, dsk, dsv = _sb_bwd(sq, sq2, sk2, sv, dso_h, s_lt, "sb_bwd")
    dc = jnp.pad((dck[:, 0, :] + dcq[:, :, 0]).T, ((0, 0), (0, 128 - NH)))
    dfl, db = _gate_bwd(dc, fl, b_f, "gate_bwd")
    gs["b_forget"] = db[:, :NH]
    dqkv = jnp.concatenate([dfq, dfk.astype(CDT), dfv.astype(CDT), dsq, dsk.astype(CDT), dsv.astype(CDT)], axis=0)
    dproj = jnp.concatenate([dqkv.transpose(1, 0, 2).reshape(S, NQKV), dfl.astype(CDT),
                             jnp.zeros((S, IN_PAD - NQKV - 128), CDT)], axis=1)
    gw["w_in"] = _mm_tn(h1, dproj, "in_dw", tn=IN_PAD)[:, :IN_COLS]
    dx0, gs["attn_norm_g"] = _mm_nt_rmsbwd(dproj, w_in, x0, gains["attn_norm_g"], dx1, "in_dx", tk=IN_PAD)
    return loss, dx0, gw, gs


NAMES = ("attn_norm_g", "w_in", "b_forget", "fox_out_g", "sb_out_g", "w_out", "xattn_norm_g", "mem_norm_g", "w_mq",
         "w_mkv", "w_mo", "ffn_norm_g", "w_up", "conv_w", "conv_b", "w_down", "final_norm_g")


def _step(x, mem, loss_target, w, m, v):
    xi, yi, ci = _place()
    big_shapes = [s for _, s, _ in BIG]

    parts = []
    for name, shape, _ in BIG:
        blk = w[name].reshape(shape)
        parts.append(lax.bitcast_convert_type(blk, CDT) if name == "conv_w" else blk.astype(CDT))
    slots = lax.dynamic_update_slice(lax.empty((N_CHIP, ROWS_G, 128), CDT), _pack_rows(parts, ROWS_G)[None], (2 * xi + yi, 0, 0))
    gathered = _gather_weights(slots).reshape(N_CHIP, ROWS_G * 128)
    W, off = {}, 0
    for (name, shape, axis), n in zip(BIG, GATHER_SIZES):
        sh = gathered[:, off:off + n]
        off += n
        if name == "conv_w":
            sh = lax.bitcast_convert_type(sh.reshape(N_CHIP, n // 2, 2), F32)
        W[name] = _from_shards(sh, shape, axis)
    gains = {name: w[name].reshape(1, -1) for name, _ in SMALL}

    loss, grad_x, gw, gs = _local_step(x[0], mem[0], loss_target[0], W, gains)

    g_flat = jnp.concatenate([_to_shards(gw[name], shape, axis) for name, shape, axis in BIG], axis=1)
    g_flat = jnp.pad(g_flat, ((0, 0), (0, ROWS_F * 128 - P_BIG))).reshape(N_CHIP, ROWS_F, 128)
    pair_sum, pair_sum_b = _add_sibling(g_flat, _swap_halves(g_flat), jnp.reshape(ci, (1,)).astype(jnp.int32), "add_sibling")
    g_half = _add_chips(pair_sum, _scatter_chips(pair_sum_b), jnp.reshape(2 * xi + yi, (1,)).astype(jnp.int32), "add_chips")
    g_big = _join_halves(lax.dynamic_update_slice(lax.empty((ROWS_F, 128), F32), g_half, (ci * HALF_F, 0)))
    small = jnp.concatenate([gs[name].reshape(-1) for name, _ in SMALL] + [loss[0, :1]])
    small = jnp.pad(small, (0, ROWS_S * 128 - P_SMALL)).reshape(ROWS_S, 128)
    small_parts = _gather_small(small).reshape(8, ROWS_S, 128)

    def flat_small(d):
        return _pack_rows([d[name] for name, _ in SMALL], ROWS_S)

    outs = {}
    for (name, shape, _), g in zip(BIG, _unpack(g_big.reshape(-1), BIG_SIZES, big_shapes)):
        res = _adamw(g, w[name].reshape(shape), m[name].reshape(shape), v[name].reshape(shape), "adamw_" + name)
        for prefix, arr in zip(("grad_", "delta_", "new_m_", "new_v_"), (g, *res)):
            outs[prefix + name] = arr.reshape(w[name].shape)
    small_res = _adamw_small(small_parts, flat_small(w), flat_small(m), flat_small(v), "adamw_small")
    g_sm = small_res[0]
    for prefix, sm in zip(("grad_", "delta_", "new_m_", "new_v_"), small_res):
        for (name, n), arr in zip(SMALL, _unpack(sm.reshape(-1), [n for _, n in SMALL], [(n,) for _, n in SMALL])):
            outs[prefix + name] = arr.reshape(w[name].shape)
    total_loss = g_sm.reshape(-1)[P_SMALL - 1]
    return (total_loss, grad_x[None], *[outs[p + n] for p in ("grad_", "delta_", "new_m_", "new_v_") for n in NAMES])


def kernel(x, mem, attn_norm_g, w_in, b_forget, fox_out_g, sb_out_g, w_out, xattn_norm_g, mem_norm_g, w_mq, w_mkv, w_mo, ffn_norm_g, w_up, conv_w, conv_b, w_down, final_norm_g, loss_target, m_attn_norm_g, m_w_in, m_b_forget, m_fox_out_g, m_sb_out_g, m_w_out, m_xattn_norm_g, m_mem_norm_g, m_w_mq, m_w_mkv, m_w_mo, m_ffn_norm_g, m_w_up, m_conv_w, m_conv_b, m_w_down, m_final_norm_g, v_attn_norm_g, v_w_in, v_b_forget, v_fox_out_g, v_sb_out_g, v_w_out, v_xattn_norm_g, v_mem_norm_g, v_w_mq, v_w_mkv, v_w_mo, v_ffn_norm_g, v_w_up, v_conv_w, v_conv_b, v_w_down, v_final_norm_g):
    given = dict(locals())
    w = {n: given[n] for n in NAMES}
    m = {n: given["m_" + n] for n in NAMES}
    v = {n: given["v_" + n] for n in NAMES}
    return _step(x, mem, loss_target, w, m, v)
```

```python
import functools

import numpy as np
import jax
import jax.numpy as jnp
from jax import lax
from jax.experimental import pallas as pl
from jax.experimental.pallas import tpu as pltpu

F32 = jnp.float32
CDT = jnp.bfloat16
MESH = pl.DeviceIdType.MESH

D = 1024
HD = 64
NH = 8
GW = NH * HD
NQKV = 6 * GW
IN_COLS = NQKV + NH
IN_PAD = NQKV + 256
NMH = 4
MHD = D // NMH
DFF = 2816
EPS = 1e-6
ATT_SCALE = HD ** -0.5
MEM_SCALE = MHD ** -0.5
NEG = -1e30

LR, B1, B2, AEPS, WD, STEP = 0.001, 0.9, 0.999, 1e-08, 0.01, 10
BC1 = 1.0 - B1 ** STEP
BC2 = 1.0 - B2 ** STEP

ATT_TILES = {"fox_fwd": (512, 1024), "fox_bwd": (512, 1024), "sb_fwd": (512, 1024), "sb_bwd": (1024, 1024)}
W_SB = 256
VMEM_LIMIT = 52 * 2 ** 20

N_CHIP = 4
BIG = (("w_in", (D, IN_COLS // N_CHIP), 1), ("w_out", (D // N_CHIP, D), 0), ("w_mq", (D // N_CHIP, D), 0),
       ("w_mkv", (D, 2 * D // N_CHIP), 1), ("w_mo", (D // N_CHIP, D), 0), ("w_up", (D, 2 * DFF // N_CHIP), 1),
       ("conv_w", (3, 2 * DFF // N_CHIP), 1), ("w_down", (DFF // N_CHIP, D), 0))
GROUPS = {"a": BIG[:1], "b": BIG[1:]}
ADAM_ROWS = 1536


def _sizes(group):
    return tuple(int(np.prod(s)) for _, s, _ in group)


def _gather_sizes(group):
    return tuple(2 * n if name == "conv_w" else n for (name, _, _), n in zip(group, _sizes(group)))


def _rows_g(group):
    return -(-sum(_gather_sizes(group)) // 4096) * 32


def _rows_f(group):
    return -(-sum(_sizes(group)) // 65536) * 512
SMALL = (("attn_norm_g", 1024), ("b_forget", 8), ("fox_out_g", 512), ("sb_out_g", 512), ("xattn_norm_g", 1024),
         ("mem_norm_g", 1024), ("ffn_norm_g", 1024), ("conv_b", 2 * DFF), ("final_norm_g", 1024))
P_SMALL = sum(n for _, n in SMALL) + 1
ROWS_S = -(-P_SMALL // 1024) * 8


def _params(sem=None, vmem=VMEM_LIMIT):
    return pltpu.CompilerParams(dimension_semantics=sem, vmem_limit_bytes=vmem)


def _tile(n, pref, mult):
    t = (min(pref, n) // mult) * mult
    while t >= mult:
        if n % t == 0:
            return t
        t -= mult
    return n


def _dot(a, b):
    return jnp.dot(a, b, preferred_element_type=F32)


def _dot_nt(a, b):
    return lax.dot_general(a, b, (((1,), (1,)), ((), ())), preferred_element_type=F32)


def _dot_tn(a, b):
    return lax.dot_general(a, b, (((0,), (0,)), ((), ())), preferred_element_type=F32)


def _split3(x):
    h1 = x.astype(CDT)
    r1 = x - h1.astype(F32)
    h2 = r1.astype(CDT)
    h3 = (r1 - h2.astype(F32)).astype(CDT)
    return h1, h2, h3


def _split2(x):
    h1 = x.astype(CDT)
    return h1, (x - h1.astype(F32)).astype(CDT)


def _rms_bwd(dh, x, g):
    r = lax.rsqrt(jnp.mean(x * x, axis=-1, keepdims=True) + EPS)
    xn = x * r
    dg = jnp.sum(dh * xn, axis=0, keepdims=True)
    dhg = dh * g
    dx = r * (dhg - xn * jnp.mean(dhg * xn, axis=-1, keepdims=True))
    return dx, dg


def _mm_nn(a, b, out_dtype, name, *, tm=1024, tn=512, residual=None, halves=False):
    M, K = a.shape
    N = b.shape[1]
    tm = _tile(M, tm, 16)
    tn = _tile(N // 2 if halves else N, tn, 128)
    nj = N // tn

    def body(*refs):
        a_ref, b_ref = refs[0], refs[1]
        o_ref = refs[-1]
        acc = _dot(a_ref[...].astype(CDT), b_ref[...].astype(CDT))
        if residual is not None:
            acc = acc + refs[2][...]
        o_ref[...] = acc.astype(o_ref.dtype)

    in_specs = [pl.BlockSpec((tm, K), lambda i, j: (i, 0)), pl.BlockSpec((K, tn), lambda i, j: (0, j))]
    ops = [a, b]
    if residual is not None:
        in_specs.append(pl.BlockSpec((tm, tn), lambda i, j: (i, j)))
        ops.append(residual)
    if halves:
        njh = nj // 2
        out_shape = jax.ShapeDtypeStruct((2, M, N // 2), out_dtype)
        out_spec = pl.BlockSpec((None, tm, tn), lambda i, j: (j // njh, i, j % njh))
    else:
        out_shape = jax.ShapeDtypeStruct((M, N), out_dtype)
        out_spec = pl.BlockSpec((tm, tn), lambda i, j: (i, j))
    return pl.pallas_call(body, name=name, grid=(M // tm, nj), in_specs=in_specs, out_specs=out_spec,
                          out_shape=out_shape, compiler_params=_params(("parallel", "parallel")))(*ops)


def _mm_tn(a, b, name, *, tka=512, tn=1024, ts=512, b_halves=False):
    S, Ka = a.shape
    N = 2 * b.shape[2] if b_halves else b.shape[1]
    tka = _tile(Ka, tka, 128)
    tn = _tile(N // 2 if b_halves else N, tn, 128)
    ts = _tile(S, ts, 16)
    nn = N // tn

    def body(a_ref, b_ref, o_ref):
        @pl.when(pl.program_id(2) == 0)
        def _():
            o_ref[...] = jnp.zeros_like(o_ref)
        o_ref[...] += _dot_tn(a_ref[...].astype(CDT), b_ref[...].astype(CDT))

    if b_halves:
        nnh = nn // 2
        b_spec = pl.BlockSpec((None, ts, tn), lambda i, j, s: (j // nnh, s, j % nnh))
    else:
        b_spec = pl.BlockSpec((ts, tn), lambda i, j, s: (s, j))
    return pl.pallas_call(
        body, name=name, grid=(Ka // tka, nn, S // ts),
        in_specs=[pl.BlockSpec((ts, tka), lambda i, j, s: (s, i)), b_spec],
        out_specs=pl.BlockSpec((tka, tn), lambda i, j, s: (i, j)),
        out_shape=jax.ShapeDtypeStruct((Ka, N), F32),
        compiler_params=_params(("parallel", "parallel", "arbitrary")))(a, b)


def _mm_nt(a, b, name, *, tm=512, tn=None, tk=None, a_halves=False, out_dtype=F32,
           epilogue=None, extra=(), extra_specs=(), out_shape=None, out_specs=None):
    if a_halves:
        M, K = a.shape[1], 2 * a.shape[2]
    else:
        M, K = a.shape
    N = b.shape[0]
    tm = _tile(M, tm, 16)
    tn = N if (epilogue is not None or tn is None) else _tile(N, tn, 128)
    tk = K if tk is None else _tile(K // 2 if a_halves else K, tk, 128)
    nk = K // tk
    n_extra = len(extra)

    def body(*refs):
        a_ref, b_ref = refs[0], refs[1]
        extra_refs = refs[2:2 + n_extra]
        out_refs = refs[2 + n_extra:-1]
        acc_ref = refs[-1]
        k = pl.program_id(2)

        @pl.when(k == 0)
        def _():
            acc_ref[...] = jnp.zeros_like(acc_ref)
        acc_ref[...] += _dot_nt(a_ref[...].astype(CDT), b_ref[...].astype(CDT))

        @pl.when(k == nk - 1)
        def _():
            if epilogue is None:
                out_refs[0][...] = acc_ref[...].astype(out_refs[0].dtype)
            else:
                epilogue(acc_ref[...], pl.program_id(0), extra_refs, out_refs)

    if a_halves:
        nkh = nk // 2
        a_spec = pl.BlockSpec((None, tm, tk), lambda i, j, k: (k // nkh, i, k % nkh))
    else:
        a_spec = pl.BlockSpec((tm, tk), lambda i, j, k: (i, k))
    if epilogue is None:
        out_shape = jax.ShapeDtypeStruct((M, N), out_dtype)
        out_specs = pl.BlockSpec((tm, tn), lambda i, j, k: (i, j))
        sem = ("parallel", "parallel", "arbitrary")
    else:
        sem = ("arbitrary", "arbitrary", "arbitrary")
    return pl.pallas_call(
        body, name=name, grid=(M // tm, N // tn, nk),
        in_specs=[a_spec, pl.BlockSpec((tn, tk), lambda i, j, k: (j, k)), *extra_specs],
        out_specs=out_specs, out_shape=out_shape,
        scratch_shapes=[pltpu.VMEM((tm, tn), F32)],
        compiler_params=_params(sem))(a, b, *extra)


def _mm_nt_rmsbwd(a, b, x, g, dres, name, *, tm=512, tk=None, a_halves=False):
    M = x.shape[0]
    tm = _tile(M, tm, 16)

    def epilogue(acc, i, extra_refs, out_refs):
        x_ref, g_ref, r_ref = extra_refs
        dx_ref, dg_ref = out_refs
        dx, dg = _rms_bwd(acc, x_ref[...], g_ref[...])
        dx_ref[...] = r_ref[...] + dx

        @pl.when(i == 0)
        def _():
            dg_ref[...] = jnp.zeros_like(dg_ref)
        dg_ref[...] += dg

    row = pl.BlockSpec((tm, D), lambda i, j, k: (i, 0))
    vec = pl.BlockSpec((1, D), lambda i, j, k: (0, 0))
    return _mm_nt(a, b, name, tm=tm, tk=tk, a_halves=a_halves, epilogue=epilogue,
                  extra=(x, g, dres), extra_specs=(row, vec, row),
                  out_shape=(jax.ShapeDtypeStruct((M, D), F32), jax.ShapeDtypeStruct((1, D), F32)),
                  out_specs=(row, vec))


def _rms_cast(x, g, name, *, tm=512):
    M, W = x.shape
    tm = _tile(M, tm, 16)

    def body(x_ref, g_ref, o_ref):
        xf = x_ref[...]
        r = lax.rsqrt(jnp.mean(xf * xf, axis=-1, keepdims=True) + EPS)
        o_ref[...] = (xf * r * g_ref[...]).astype(o_ref.dtype)

    return pl.pallas_call(body, name=name, grid=(M // tm,),
                          in_specs=[pl.BlockSpec((tm, W), lambda i: (i, 0)), pl.BlockSpec((1, W), lambda i: (0, 0))],
                          out_specs=pl.BlockSpec((tm, W), lambda i: (i, 0)),
                          out_shape=jax.ShapeDtypeStruct((M, W), CDT),
                          compiler_params=_params(("parallel",)))(x, g)


def _tri(n, lower):
    r = lax.broadcasted_iota(jnp.int32, (n, n), 0)
    c = lax.broadcasted_iota(jnp.int32, (n, n), 1)
    return (c <= r if lower else c >= r).astype(CDT)


def _gate_fwd(fl, b, name, *, tm=512):
    S = fl.shape[0]
    tm = _tile(S, tm, 16)

    def body(f_ref, b_ref, c_ref, carry):
        @pl.when(pl.program_id(0) == 0)
        def _():
            carry[...] = jnp.zeros_like(carry)
        z = f_ref[...] + b_ref[...]
        lf = jnp.minimum(z, 0.0) - jnp.log(1.0 + jnp.exp(-jnp.abs(z)))
        tri = _tri(tm, True)
        cum = sum(_dot(tri, p) for p in _split3(lf)) + carry[...]
        c_ref[...] = cum
        carry[...] = cum[tm - 1:tm, :]

    return pl.pallas_call(body, name=name, grid=(S // tm,),
                          in_specs=[pl.BlockSpec((tm, 128), lambda i: (i, 0)), pl.BlockSpec((1, 128), lambda i: (0, 0))],
                          out_specs=pl.BlockSpec((tm, 128), lambda i: (i, 0)),
                          out_shape=jax.ShapeDtypeStruct((S, 128), F32),
                          scratch_shapes=[pltpu.VMEM((1, 128), F32)],
                          compiler_params=_params(("arbitrary",)))(fl, b)


def _gate_bwd(dc, fl, b, name, *, tm=512):
    S = fl.shape[0]
    tm = _tile(S, tm, 16)
    nb = S // tm

    def body(dc_ref, f_ref, b_ref, df_ref, db_ref, carry):
        @pl.when(pl.program_id(0) == 0)
        def _():
            carry[...] = jnp.zeros_like(carry)
            db_ref[...] = jnp.zeros_like(db_ref)
        tri = _tri(tm, False)
        suf = sum(_dot(tri, p) for p in _split3(dc_ref[...])) + carry[...]
        carry[...] = suf[0:1, :]
        df = suf * jax.nn.sigmoid(-(f_ref[...] + b_ref[...]))
        df_ref[...] = df
        db_ref[...] += jnp.sum(df, axis=0, keepdims=True)

    rev = pl.BlockSpec((tm, 128), lambda i: (nb - 1 - i, 0))
    vec = pl.BlockSpec((1, 128), lambda i: (0, 0))
    return pl.pallas_call(body, name=name, grid=(nb,), in_specs=[rev, rev, vec], out_specs=(rev, vec),
                          out_shape=(jax.ShapeDtypeStruct((S, 128), F32), jax.ShapeDtypeStruct((1, 128), F32)),
                          scratch_shapes=[pltpu.VMEM((1, 128), F32)],
                          compiler_params=_params(("arbitrary",)))(dc, fl, b)


def _out_proj(fo, so, gf, gs, w_out, x0, name, *, tm=512):
    S = fo.shape[0]
    tm = _tile(S, tm, 16)

    def body(fo_ref, so_ref, gf_ref, gs_ref, w_ref, x_ref, x1_ref, mx_ref):
        for ref, g_ref, lo in ((fo_ref, gf_ref, 0), (so_ref, gs_ref, GW)):
            o = ref[...]
            r = lax.rsqrt(jnp.mean(o * o, axis=-1, keepdims=True) + EPS)
            mx_ref[:, lo:lo + GW] = (o * r * g_ref[...]).astype(CDT)
        x1_ref[...] = x_ref[...] + _dot(mx_ref[...], w_ref[...])

    half = pl.BlockSpec((tm, GW), lambda i: (i, 0))
    gvec = pl.BlockSpec((1, GW), lambda i: (0, 0))
    row = pl.BlockSpec((tm, D), lambda i: (i, 0))
    return pl.pallas_call(body, name=name, grid=(S // tm,),
                          in_specs=[half, half, gvec, gvec, pl.BlockSpec((D, D), lambda i: (0, 0)), row],
                          out_specs=(row, row),
                          out_shape=(jax.ShapeDtypeStruct((S, D), F32), jax.ShapeDtypeStruct((S, D), CDT)),
                          compiler_params=_params(("parallel",)))(fo, so, gf, gs, w_out, x0)


def _out_proj_bwd(dx1, w_out, fo, so, gf, gs, name, *, tm=512):
    S = fo.shape[0]
    tm = _tile(S, tm, 16)

    def epilogue(acc, i, extra_refs, out_refs):
        fo_ref, so_ref, gf_ref, gs_ref = extra_refs
        dfo_ref, dso_ref, dgf_ref, dgs_ref = out_refs

        @pl.when(i == 0)
        def _():
            dgf_ref[...] = jnp.zeros_like(dgf_ref)
            dgs_ref[...] = jnp.zeros_like(dgs_ref)
        for lo, o_ref, g_ref, do_ref, dg_ref in ((0, fo_ref, gf_ref, dfo_ref, dgf_ref), (GW, so_ref, gs_ref, dso_ref, dgs_ref)):
            dx, dg = _rms_bwd(acc[:, lo:lo + GW], o_ref[...], g_ref[...])
            do_ref[...] = dx.astype(do_ref.dtype)
            dg_ref[...] += dg

    half = pl.BlockSpec((tm, GW), lambda i, j, k: (i, 0))
    gvec = pl.BlockSpec((1, GW), lambda i, j, k: (0, 0))
    return _mm_nt(dx1, w_out, name, tm=tm, epilogue=epilogue, extra=(fo, so, gf, gs),
                  extra_specs=(half, half, gvec, gvec),
                  out_shape=(jax.ShapeDtypeStruct((S, GW), CDT), jax.ShapeDtypeStruct((S, GW), CDT),
                             jax.ShapeDtypeStruct((1, GW), F32), jax.ShapeDtypeStruct((1, GW), F32)),
                  out_specs=(half, half, gvec, gvec))


def _loss_bwd(x3, tgt, g, name, *, tm=512):
    S = x3.shape[0]
    tm = _tile(S, tm, 16)

    def body(x_ref, t_ref, g_ref, dx_ref, loss_ref, dg_ref):
        @pl.when(pl.program_id(0) == 0)
        def _():
            loss_ref[...] = jnp.zeros_like(loss_ref)
            dg_ref[...] = jnp.zeros_like(dg_ref)
        x = x_ref[...]
        gv = g_ref[...]
        r = lax.rsqrt(jnp.mean(x * x, axis=-1, keepdims=True) + EPS)
        xn = x * r
        err = xn * gv - t_ref[...]
        loss_ref[...] += jnp.full(loss_ref.shape, 0.5 * jnp.sum(jnp.mean(err * err, axis=-1, keepdims=True)), F32)
        dy = err * (1.0 / D)
        dg_ref[...] += jnp.sum(dy * xn, axis=0, keepdims=True)
        dyg = dy * gv
        dx_ref[...] = r * (dyg - xn * jnp.mean(dyg * xn, axis=-1, keepdims=True))

    row = pl.BlockSpec((tm, D), lambda i: (i, 0))
    vec = pl.BlockSpec((1, D), lambda i: (0, 0))
    dx3, loss, dg = pl.pallas_call(
        body, name=name, grid=(S // tm,), in_specs=[row, row, vec],
        out_specs=(row, pl.BlockSpec((1, 128), lambda i: (0, 0)), vec),
        out_shape=(jax.ShapeDtypeStruct((S, D), F32), jax.ShapeDtypeStruct((1, 128), F32), jax.ShapeDtypeStruct((1, D), F32)),
        compiler_params=_params(("arbitrary",)))(x3, tgt, g)
    return loss, dx3, dg


MASKED, FIRST, LAST = 1, 2, 4


def _att_tiles(name, S):
    tq, tk = ATT_TILES[name]
    return min(tq, S), min(tk, S)


def _pairs(S, tq, tk, descending=True):
    assert tk % tq == 0 and S % tk == 0
    qi, kj, fl = [], [], []
    for i in range(S // tq):
        last = ((i + 1) * tq - 1) // tk
        order = list(range(last, -1, -1) if descending else range(last + 1))
        for pos, kb in enumerate(order):
            qi.append(i)
            kj.append(kb)
            fl.append((MASKED if (kb + 1) * tk - 1 > i * tq else 0) | (FIRST if pos == 0 else 0) | (LAST if pos == last else 0))
    return tuple(jnp.asarray(np.asarray(a, np.int32)) for a in (qi, kj, fl))


def _att_specs(tq, tk, width=HD):
    qblk = pl.BlockSpec((1, tq, width), lambda h, n, qi, kj, fl: (h, qi[n], 0))
    kblk = pl.BlockSpec((1, tk, width), lambda h, n, qi, kj, fl: (h, kj[n], 0))
    qcol = pl.BlockSpec((1, tq, 1), lambda h, n, qi, kj, fl: (h, qi[n], 0))
    krow = pl.BlockSpec((1, 1, tk), lambda h, n, qi, kj, fl: (h, 0, kj[n]))
    return qblk, kblk, qcol, krow


def _causal(tq, w, ahead, strict):
    diff = lax.broadcasted_iota(jnp.int32, (tq, w), 1) - lax.broadcasted_iota(jnp.int32, (tq, w), 0)
    return diff < ahead if strict else diff <= ahead


def _masked_or_not(flags, step):
    pl.when(flags % 2 == 1)(functools.partial(step, True))
    pl.when(flags % 2 == 0)(functools.partial(step, False))


def _fox_fwd(q, k, v, cq, ck, name, slots=None):
    S = q.shape[1]
    tq, tk = _att_tiles("fox_fwd", S)
    qi, kj, fl = _pairs(S, tq, tk)
    qblk, kblk, qcol, krow = _att_specs(tq, tk)
    npairs = int(qi.shape[0])

    def body(qi_ref, kj_ref, fl_ref, q_ref, k_ref, v_ref, cq_ref, ck_ref, *rest):
        if slots is None:
            o_ref, lse_ref, m_s, l_s, acc_s = rest
        else:
            _, o_ref, lse_ref, slots_ref, m_s, l_s, acc_s, send_sems, recv_sems = rest
        h, n = pl.program_id(0), pl.program_id(1)
        i, kb, flags = qi_ref[n], kj_ref[n], fl_ref[n]
        if slots is not None:
            _gather_steps(slots_ref, send_sems, recv_sems, first=(h == 0) & (n == 0), middle=(h == NH // 4) & (n == 0),
                          last=(h == NH - 1) & (n == npairs - 1))

        @pl.when(flags & FIRST != 0)
        def _():
            m_s[...] = jnp.full_like(m_s, NEG)
            l_s[...] = jnp.zeros_like(l_s)
            acc_s[...] = jnp.zeros_like(acc_s)

        def step(masked):
            s = _dot_nt(q_ref[0] * ATT_SCALE, k_ref[0]) + cq_ref[0] - ck_ref[0]
            if masked:
                s = jnp.where(_causal(tq, tk, i * tq - kb * tk, False), s, NEG)
            m_new = jnp.maximum(m_s[...], jnp.max(s, axis=-1, keepdims=True))
            alpha = jnp.exp(m_s[...] - m_new)
            p = jnp.exp(s - m_new)
            l_s[...] = alpha * l_s[...] + jnp.sum(p, axis=-1, keepdims=True)
            acc_s[...] = alpha * acc_s[...] + _dot(p.astype(CDT), v_ref[0])
            m_s[...] = m_new

        _masked_or_not(flags, step)

        @pl.when(flags & LAST != 0)
        def _():
            o_ref[0] = acc_s[...] / l_s[...]
            lse_ref[0] = m_s[...] + jnp.log(l_s[...])

    scratch = [pltpu.VMEM((tq, 1), F32), pltpu.VMEM((tq, 1), F32), pltpu.VMEM((tq, HD), F32)]
    out_shape = (jax.ShapeDtypeStruct((NH, S, HD), F32), jax.ShapeDtypeStruct((NH, S, 1), F32))
    if slots is None:
        grid_spec = pltpu.PrefetchScalarGridSpec(num_scalar_prefetch=3, grid=(NH, npairs), in_specs=[qblk, kblk, kblk, qcol, krow],
                                                 out_specs=(qblk, qcol), scratch_shapes=scratch)
        o, lse = pl.pallas_call(body, name=name, grid_spec=grid_spec, out_shape=out_shape,
                                compiler_params=_params(("parallel", "arbitrary")))(qi, kj, fl, q, k, v, cq, ck)
        return o, lse, None
    grid_spec = pltpu.PrefetchScalarGridSpec(num_scalar_prefetch=3, grid=(NH, npairs), in_specs=[qblk, kblk, kblk, qcol, krow, ANY],
                                             out_specs=(qblk, qcol, ANY), scratch_shapes=scratch + list(GATHER_SEMS))
    return pl.pallas_call(body, name=name, grid_spec=grid_spec, out_shape=(*out_shape, jax.ShapeDtypeStruct(slots.shape, slots.dtype)),
                          input_output_aliases={8: 2},
                          compiler_params=_params(("arbitrary", "arbitrary")))(qi, kj, fl, q, k, v, cq, ck, slots)


def _fox_bwd(q, k, v, cq, ck, o, do, lse, name):
    S = q.shape[1]
    tq, tk = _att_tiles("fox_bwd", S)
    qi, kj, fl = _pairs(S, tq, tk)
    qblk, kblk, qcol, krow = _att_specs(tq, tk)

    def body(qi_ref, kj_ref, fl_ref, q_ref, k_ref, v_ref, cq_ref, ck_ref, o_ref, do_ref, lse_ref,
             dq_ref, dk_ref, dv_ref, dck_ref, dcq_ref, dq_s, dl_s, dcq_s):
        n = pl.program_id(1)
        i, kb, flags = qi_ref[n], kj_ref[n], fl_ref[n]

        @pl.when(n == 0)
        def _():
            dk_ref[...] = jnp.zeros_like(dk_ref)
            dv_ref[...] = jnp.zeros_like(dv_ref)
            dck_ref[...] = jnp.zeros_like(dck_ref)

        @pl.when(flags & FIRST != 0)
        def _():
            dq_s[...] = jnp.zeros_like(dq_s)
            dcq_s[...] = jnp.zeros_like(dcq_s)
            dl_s[...] = jnp.sum(do_ref[0].astype(F32) * o_ref[0], axis=-1, keepdims=True)

        def step(masked):
            qs = q_ref[0] * ATT_SCALE
            do = do_ref[0]
            p = jnp.exp(_dot_nt(qs, k_ref[0]) + cq_ref[0] - ck_ref[0] - lse_ref[0])
            if masked:
                p = jnp.where(_causal(tq, tk, i * tq - kb * tk, False), p, 0.0)
            ds = p * (_dot_nt(do, v_ref[0]) - dl_s[...])
            dsb = ds.astype(CDT)
            dq_s[...] += _dot(dsb, k_ref[0])
            rows = pl.ds(pl.multiple_of(kb * tk, tk), tk)
            dk_ref[0, rows, :] += _dot_tn(dsb, qs)
            dv_ref[0, rows, :] += _dot_tn(p.astype(CDT), do)
            dck_ref[0, :, rows] += -jnp.sum(ds, axis=0, keepdims=True)
            dcq_s[...] += jnp.sum(ds, axis=-1, keepdims=True)

        _masked_or_not(flags, step)

        @pl.when(flags & LAST != 0)
        def _():
            dq_ref[0] = (dq_s[...] * ATT_SCALE).astype(dq_ref.dtype)
            dcq_ref[0] = dcq_s[...]

    whole = pl.BlockSpec((1, S, HD), lambda h, n, qi, kj, fl: (h, 0, 0))
    grid_spec = pltpu.PrefetchScalarGridSpec(
        num_scalar_prefetch=3, grid=(NH, int(qi.shape[0])),
        in_specs=[qblk, kblk, kblk, qcol, krow, qblk, qblk, qcol],
        out_specs=(qblk, whole, whole, pl.BlockSpec((1, 1, S), lambda h, n, qi, kj, fl: (h, 0, 0)), qcol),
        scratch_shapes=[pltpu.VMEM((tq, HD), F32), pltpu.VMEM((tq, 1), F32), pltpu.VMEM((tq, 1), F32)])
    return pl.pallas_call(body, name=name, grid_spec=grid_spec,
                          out_shape=(jax.ShapeDtypeStruct((NH, S, HD), CDT), jax.ShapeDtypeStruct((NH, S, HD), F32),
                                     jax.ShapeDtypeStruct((NH, S, HD), F32), jax.ShapeDtypeStruct((NH, 1, S), F32),
                                     jax.ShapeDtypeStruct((NH, S, 1), F32)),
                          compiler_params=_params(("parallel", "arbitrary")))(qi, kj, fl, q, k, v, cq, ck, o, do, lse)


LOG2E = 1.4426950408889634


def _log2_operands(q, k):
    qf = q.astype(F32) * (ATT_SCALE * LOG2E)
    hi = lax.reduce_precision(qf, 8, 7)
    return jnp.concatenate([hi.astype(CDT), (qf - hi).astype(CDT)], axis=-1), jnp.concatenate([k, k], axis=-1)


def _sb_softplus2(q2, k2sub, mask):
    z2 = _dot_nt(q2, k2sub)
    sp2 = jnp.maximum(z2, 0.0) + jnp.log2(1.0 + jnp.exp2(-jnp.abs(z2)))
    return z2, sp2 if mask is None else jnp.where(mask, sp2, 0.0)


def _strict_tri(n, upper, value):
    r = lax.broadcasted_iota(jnp.int32, (n, n), 0)
    c = lax.broadcasted_iota(jnp.int32, (n, n), 1)
    return jnp.where(r < c if upper else r > c, value, 0.0).astype(CDT)


def _sb_fwd(q2, k2, v, name):
    S = q2.shape[1]
    tq, tk = _att_tiles("sb_fwd", S)
    W = min(W_SB, tk)
    qi, kj, fl = _pairs(S, tq, tk)
    qblk, kblk, qcol, _ = _att_specs(tq, tk)
    q2blk, k2blk, _, _ = _att_specs(tq, tk, 2 * HD)

    def body(qi_ref, kj_ref, fl_ref, q_ref, k_ref, v_ref, o_ref, lt_ref, run_s, acc_s):
        n = pl.program_id(1)
        i, kb, flags = qi_ref[n], kj_ref[n], fl_ref[n]

        @pl.when(flags & FIRST != 0)
        def _():
            run_s[...] = jnp.zeros_like(run_s)
            acc_s[...] = jnp.zeros_like(acc_s)

        def step(masked):
            neg_later = _strict_tri(W, False, -1.0)
            run = run_s[...]
            acc = acc_s[...]
            for sub in range(tk // W - 1, -1, -1):
                cols = slice(sub * W, (sub + 1) * W)
                mask = _causal(tq, W, i * tq - kb * tk - sub * W, True) if masked else None
                z2, sp2 = _sb_softplus2(q_ref[0], k_ref[0, cols, :], mask)
                excl = _dot(sp2.astype(CDT), neg_later)
                a = jnp.exp2((z2 - sp2) + (excl + run))
                if masked:
                    a = jnp.where(mask, a, 0.0)
                acc = acc + _dot(a.astype(CDT), v_ref[0, cols, :])
                run = run + (excl[:, 0:1] - sp2[:, 0:1])
            run_s[...] = run
            acc_s[...] = acc

        _masked_or_not(flags, step)

        @pl.when(flags & LAST != 0)
        def _():
            o_ref[0] = acc_s[...]
            lt_ref[0] = run_s[...]

    grid_spec = pltpu.PrefetchScalarGridSpec(
        num_scalar_prefetch=3, grid=(NH, int(qi.shape[0])), in_specs=[q2blk, k2blk, kblk], out_specs=(qblk, qcol),
        scratch_shapes=[pltpu.VMEM((tq, 1), F32), pltpu.VMEM((tq, HD), F32)])
    return pl.pallas_call(body, name=name, grid_spec=grid_spec,
                          out_shape=(jax.ShapeDtypeStruct((NH, S, HD), F32), jax.ShapeDtypeStruct((NH, S, 1), F32)),
                          compiler_params=_params(("parallel", "arbitrary")))(qi, kj, fl, q2, k2, v)


def _sb_bwd(q, q2, k2, v, do, lt, name, scatter=None):
    S = q.shape[1]
    tq, tk = _att_tiles("sb_bwd", S)
    W = min(W_SB, tk)
    qi, kj, fl = _pairs(S, tq, tk, descending=False)
    qblk, kblk, qcol, _ = _att_specs(tq, tk)
    q2blk, k2blk, _, _ = _att_specs(tq, tk, 2 * HD)
    npairs = int(qi.shape[0])

    def body(qi_ref, kj_ref, fl_ref, q_ref, q2_ref, k2_ref, v_ref, do_ref, lt_ref, *rest):
        if scatter is None:
            dq_ref, dk_ref, dv_ref, passed_s, gsum_s, dq_s = rest
        else:
            h_ref, dq_ref, dk_ref, dv_ref, recv_ref, passed_s, gsum_s, dq_s, send_sems, recv_sems = rest
        n = pl.program_id(1)
        i, kb, flags = qi_ref[n], kj_ref[n], fl_ref[n]
        if scatter is not None:
            h = pl.program_id(0)
            _scatter_steps(h_ref, recv_ref, send_sems, recv_sems, first=(h == 0) & (n == 0), last=(h == NH - 1) & (n == npairs - 1))

        @pl.when(n == 0)
        def _():
            dk_ref[...] = jnp.zeros_like(dk_ref)
            dv_ref[...] = jnp.zeros_like(dv_ref)

        @pl.when(flags & FIRST != 0)
        def _():
            passed_s[...] = jnp.zeros_like(passed_s)
            gsum_s[...] = jnp.zeros_like(gsum_s)
            dq_s[...] = jnp.zeros_like(dq_s)

        def step(masked):
            qs = q_ref[0] * ATT_SCALE
            do = do_ref[0]
            neg_later = _strict_tri(W, False, -1.0)
            earlier = _strict_tri(W, True, 1.0)
            for sub in range(tk // W):
                cols = slice(sub * W, (sub + 1) * W)
                mask = _causal(tq, W, i * tq - kb * tk - sub * W, True) if masked else None
                ksub = k2_ref[0, cols, 0:HD]
                z2, sp2 = _sb_softplus2(q2_ref[0], k2_ref[0, cols, :], mask)
                excl = _dot(sp2.astype(CDT), neg_later)
                through = passed_s[...] + (excl[:, 0:1] - sp2[:, 0:1])
                t1 = z2 - sp2
                sig = jnp.exp2(t1)
                a = jnp.exp2(t1 + (excl + (lt_ref[0] - through)))
                if masked:
                    a = jnp.where(mask, a, 0.0)
                dl = _dot_nt(do, v_ref[0, cols, :]) * a
                before = _dot(dl.astype(CDT), earlier)
                dz = dl - sig * (dl + (before + gsum_s[...]))
                if masked:
                    dz = jnp.where(mask, dz, 0.0)
                dzb = dz.astype(CDT)
                dq_s[...] += _dot(dzb, ksub)
                rows = pl.ds(pl.multiple_of(kb * tk + sub * W, W), W)
                dk_ref[0, rows, :] += _dot_tn(dzb, qs)
                dv_ref[0, rows, :] += _dot_tn(a.astype(CDT), do)
                passed_s[...] = through
                gsum_s[...] += before[:, W - 1:W] + dl[:, W - 1:W]

        _masked_or_not(flags, step)

        @pl.when(flags & LAST != 0)
        def _():
            dq_ref[0] = (dq_s[...] * ATT_SCALE).astype(dq_ref.dtype)

    whole = pl.BlockSpec((1, S, HD), lambda h, n, qi, kj, fl: (h, 0, 0))
    in_specs = [qblk, q2blk, k2blk, kblk, qblk, qcol]
    out_specs = (qblk, whole, whole)
    out_shape = (jax.ShapeDtypeStruct((NH, S, HD), CDT), jax.ShapeDtypeStruct((NH, S, HD), F32), jax.ShapeDtypeStruct((NH, S, HD), F32))
    scratch = [pltpu.VMEM((tq, 1), F32), pltpu.VMEM((tq, 1), F32), pltpu.VMEM((tq, HD), F32)]
    if scatter is None:
        grid_spec = pltpu.PrefetchScalarGridSpec(num_scalar_prefetch=3, grid=(NH, npairs), in_specs=in_specs,
                                                 out_specs=out_specs, scratch_shapes=scratch)
        return (*pl.pallas_call(body, name=name, grid_spec=grid_spec, out_shape=out_shape,
                                compiler_params=_params(("parallel", "arbitrary")))(qi, kj, fl, q, q2, k2, v, do, lt), None)
    grid_spec = pltpu.PrefetchScalarGridSpec(num_scalar_prefetch=3, grid=(NH, npairs), in_specs=in_specs + [ANY],
                                             out_specs=(*out_specs, ANY), scratch_shapes=scratch + list(SCATTER_SEMS))
    recv_shape = jax.ShapeDtypeStruct((3,) + scatter.shape[1:], scatter.dtype)
    return pl.pallas_call(body, name=name, grid_spec=grid_spec, out_shape=(*out_shape, recv_shape),
                          compiler_params=_params(("arbitrary", "arbitrary")))(qi, kj, fl, q, q2, k2, v, do, lt, scatter)


def _mem_probs(q_ref, kv_ref, h):
    cols = slice(h * MHD, (h + 1) * MHD)
    s = _dot_nt(q_ref[:, cols], kv_ref[:, cols]) * MEM_SCALE
    e = jnp.exp(s - jnp.max(s, axis=-1, keepdims=True))
    return e / jnp.sum(e, axis=-1, keepdims=True)


def _xattn_fwd(q, kv, w_mo, x1, name, *, tm=512):
    S = q.shape[0]
    tm = _tile(S, tm, 16)
    nm = kv.shape[0]

    def body(q_ref, kv_ref, w_ref, x_ref, x2_ref, o_ref):
        for h in range(NMH):
            p = _mem_probs(q_ref, kv_ref, h)
            o_ref[:, h * MHD:(h + 1) * MHD] = _dot(p.astype(CDT), kv_ref[:, D + h * MHD:D + (h + 1) * MHD]).astype(CDT)
        x2_ref[...] = x_ref[...] + _dot(o_ref[...], w_ref[...])

    row = pl.BlockSpec((tm, D), lambda i: (i, 0))
    return pl.pallas_call(body, name=name, grid=(S // tm,),
                          in_specs=[row, pl.BlockSpec((nm, 2 * D), lambda i: (0, 0)), pl.BlockSpec((D, D), lambda i: (0, 0)), row],
                          out_specs=(row, row),
                          out_shape=(jax.ShapeDtypeStruct((S, D), F32), jax.ShapeDtypeStruct((S, D), CDT)),
                          compiler_params=_params(("parallel",)))(q, kv, w_mo, x1)


def _xattn_bwd(q, kv, do, name, *, tm=512):
    S = q.shape[0]
    tm = _tile(S, tm, 16)
    nm = kv.shape[0]

    def body(q_ref, kv_ref, do_ref, dq_ref, dkv_ref):
        @pl.when(pl.program_id(0) == 0)
        def _():
            dkv_ref[...] = jnp.zeros_like(dkv_ref)
        for h in range(NMH):
            cols = slice(h * MHD, (h + 1) * MHD)
            vcols = slice(D + h * MHD, D + (h + 1) * MHD)
            p = _mem_probs(q_ref, kv_ref, h)
            doh = do_ref[:, cols]
            dp = _dot_nt(doh, kv_ref[:, vcols])
            ds = (p * (dp - jnp.sum(p * dp, axis=-1, keepdims=True)) * MEM_SCALE).astype(CDT)
            dq_ref[:, cols] = _dot(ds, kv_ref[:, cols]).astype(CDT)
            dkv_ref[:, cols] += _dot_tn(ds, q_ref[:, cols])
            dkv_ref[:, vcols] += _dot_tn(p.astype(CDT), doh)

    row = pl.BlockSpec((tm, D), lambda i: (i, 0))
    kvs = pl.BlockSpec((nm, 2 * D), lambda i: (0, 0))
    return pl.pallas_call(body, name=name, grid=(S // tm,), in_specs=[row, kvs, row], out_specs=(row, kvs),
                          out_shape=(jax.ShapeDtypeStruct((S, D), CDT), jax.ShapeDtypeStruct((nm, 2 * D), F32)),
                          compiler_params=_params(("arbitrary",)))(q, kv, do)


HALO = 16
SLAB = 8


def _shift_down(u, prev, s):
    rolled = pltpu.roll(u, s, 0)
    top = rolled[0:SLAB]
    r = lax.broadcasted_iota(jnp.int32, top.shape, 0)
    for t in range(s):
        top = jnp.where(r == t, prev[HALO - s + t:HALO - s + t + 1, :], top)
    return jnp.concatenate([top, rolled[SLAB:]], axis=0)


def _shift_up(u, nxt, s):
    n = u.shape[0]
    rolled = pltpu.roll(u, n - s, 0)
    bottom = rolled[n - SLAB:]
    r = lax.broadcasted_iota(jnp.int32, bottom.shape, 0)
    for t in range(s):
        bottom = jnp.where(r == SLAB - s + t, nxt[t:t + 1, :], bottom)
    return jnp.concatenate([rolled[:n - SLAB], bottom], axis=0)


def _conv_taps(u_ref, h_ref, first):
    u = u_ref[...].astype(F32)
    prev = jnp.where(first, 0.0, h_ref[...].astype(F32))
    out = []
    for half in range(2):
        out.append((u[half], _shift_down(u[half], prev[half], 1), _shift_down(u[half], prev[half], 2)))
    return out


def _conv_specs(tm, tn, nsb):
    blk = pl.BlockSpec((2, tm, tn), lambda j, i: (0, i, j))
    prev = pl.BlockSpec((2, HALO, tn), lambda j, i: (0, jnp.maximum(i * (tm // HALO) - 1, 0), j))
    nxt = pl.BlockSpec((2, HALO, tn), lambda j, i: (0, jnp.minimum((i + 1) * (tm // HALO), nsb - 1), j))
    w = pl.BlockSpec((2, 3, tn), lambda j, i: (0, 0, j))
    b = pl.BlockSpec((2, 1, tn), lambda j, i: (0, 0, j))
    return blk, prev, nxt, w, b


def _conv_apply(taps, w_ref, b_ref):
    ys = []
    for half in range(2):
        u, u1, u2 = taps[half]
        w = w_ref[half]
        ys.append(b_ref[half] + u2 * w[0:1, :] + u1 * w[1:2, :] + u * w[2:3, :])
    return ys


def _conv_act(u0, cw, cb, name, *, tm=2048, tn=256):
    _, S, F = u0.shape
    tm = _tile(S, tm, HALO)
    tn = _tile(F, tn, 128)
    blk, prev, _, w, b = _conv_specs(tm, tn, S // HALO)

    def body(u_ref, h_ref, w_ref, b_ref, a_ref):
        yg, yv = _conv_apply(_conv_taps(u_ref, h_ref, pl.program_id(1) == 0), w_ref, b_ref)
        a_ref[...] = (yg * jax.nn.sigmoid(yg) * yv).astype(a_ref.dtype)

    return pl.pallas_call(body, name=name, grid=(F // tn, S // tm), in_specs=[blk, prev, w, b],
                          out_specs=pl.BlockSpec((tm, tn), lambda j, i: (i, j)),
                          out_shape=jax.ShapeDtypeStruct((S, F), CDT),
                          compiler_params=_params(("parallel", "parallel")))(u0, u0, cw, cb)


def _conv_act_bwd(u0, da, cw, cb, name, *, tm=2048, tn=256):
    _, S, F = u0.shape
    tm = _tile(S, tm, HALO)
    tn = _tile(F, tn, 128)
    blk, prev, _, w, b = _conv_specs(tm, tn, S // HALO)

    def body(u_ref, h_ref, da_ref, w_ref, b_ref, du_ref, dwb_ref):
        @pl.when(pl.program_id(1) == 0)
        def _():
            dwb_ref[...] = jnp.zeros_like(dwb_ref)
        taps = _conv_taps(u_ref, h_ref, pl.program_id(1) == 0)
        yg, yv = _conv_apply(taps, w_ref, b_ref)
        sg = jax.nn.sigmoid(yg)
        da = da_ref[...].astype(F32)
        dus = (da * yv * sg * (1.0 + yg * (1.0 - sg)), da * yg * sg)
        for half in range(2):
            du = dus[half]
            du_ref[half] = du.astype(du_ref.dtype)
            u, u1, u2 = taps[half]
            for row, term in enumerate((du * u2, du * u1, du * u, du)):
                dwb_ref[half, row:row + 1, :] += jnp.sum(term, axis=0, keepdims=True)

    return pl.pallas_call(body, name=name, grid=(F // tn, S // tm),
                          in_specs=[blk, prev, pl.BlockSpec((tm, tn), lambda j, i: (i, j)), w, b],
                          out_specs=(blk, pl.BlockSpec((2, 4, tn), lambda j, i: (0, 0, j))),
                          out_shape=(jax.ShapeDtypeStruct((2, S, F), CDT), jax.ShapeDtypeStruct((2, 4, F), F32)),
                          compiler_params=_params(("parallel", "arbitrary")))(u0, u0, da, cw, cb)


def _conv_bwd_input(du, cw, name, *, tm=2048, tn=256):
    _, S, F = du.shape
    tm = _tile(S, tm, HALO)
    tn = _tile(F, tn, 128)
    blk, _, nxt, w, _ = _conv_specs(tm, tn, S // HALO)
    ni = S // tm

    def body(d_ref, h_ref, w_ref, o_ref):
        d = d_ref[...].astype(F32)
        nx = jnp.where(pl.program_id(1) == ni - 1, 0.0, h_ref[...].astype(F32))
        for half in range(2):
            wv = w_ref[half]
            y = d[half] * wv[2:3, :] + _shift_up(d[half], nx[half], 1) * wv[1:2, :] + _shift_up(d[half], nx[half], 2) * wv[0:1, :]
            o_ref[half] = y.astype(o_ref.dtype)

    return pl.pallas_call(body, name=name, grid=(F // tn, ni), in_specs=[blk, nxt, w], out_specs=blk,
                          out_shape=jax.ShapeDtypeStruct((2, S, F), CDT),
                          compiler_params=_params(("parallel", "parallel")))(du, du, cw)


ANY = pl.BlockSpec(memory_space=pl.ANY)


def _place():
    return lax.axis_index("x"), lax.axis_index("y"), lax.axis_index("c")


def _other_chips(x, y):
    return ((1 - x, y), (x, 1 - y), (1 - x, 1 - y))


def _when(pred, fn):
    if pred is True:
        fn()
    else:
        pl.when(pred)(fn)


GATHER_SEMS = (pltpu.SemaphoreType.DMA((6,)), pltpu.SemaphoreType.DMA((6,)))
SCATTER_SEMS = (pltpu.SemaphoreType.DMA((3,)), pltpu.SemaphoreType.DMA((3,)))


def _gather_steps(out_ref, send_sems, recv_sems, first=True, middle=True, last=True):
    half = out_ref.shape[1] // 2
    x, y, c = _place()
    chips = _other_chips(x, y)

    def part(chip, pc):
        return out_ref.at[2 * chip[0] + chip[1], pl.ds(pl.multiple_of(pc * half, 16), half), :]

    def copy(k, chip, pc, to):
        return pltpu.make_async_remote_copy(src_ref=part(chip, pc), dst_ref=part(chip, pc),
                                            send_sem=send_sems.at[k], recv_sem=recv_sems.at[k],
                                            device_id=to, device_id_type=MESH)

    def send_mine():
        for j, chip in enumerate(chips):
            copy(j, (x, y), c, (*chip, c)).start()

    def pass_on():
        for j, chip in enumerate(chips):
            copy(j, chip, c, (x, y, c)).wait_recv()
            copy(3 + j, chip, c, (x, y, 1 - c)).start()

    def finish():
        for j, chip in enumerate(chips):
            copy(3 + j, chip, 1 - c, (x, y, c)).wait_recv()
        for j, chip in enumerate(chips):
            copy(j, (x, y), c, (*chip, c)).wait_send()
            copy(3 + j, chip, c, (x, y, 1 - c)).wait_send()

    _when(first, send_mine)
    _when(middle, pass_on)
    _when(last, finish)


def _gather_weights(buf):
    def body(buf_ref, out_ref, send_sems, recv_sems):
        del buf_ref
        _gather_steps(out_ref, send_sems, recv_sems)

    return pl.pallas_call(body, name="gather_weights", in_specs=[ANY], out_specs=ANY,
                          out_shape=jax.ShapeDtypeStruct(buf.shape, buf.dtype), input_output_aliases={0: 0},
                          scratch_shapes=list(GATHER_SEMS))(buf)


def _gather_small(v):
    m = v.shape[0]

    def body(v_ref, out_ref, send_sems, recv_sems, local_sem):
        x, y, c = _place()
        me, sibling = (x, y, c), (x, y, 1 - c)
        chips = _other_chips(x, y)

        def rows(px, py, pc):
            return out_ref.at[pl.ds((4 * px + 2 * py + pc) * m, m), :]

        def copy(k, block, to, src=None):
            return pltpu.make_async_remote_copy(src_ref=rows(*block) if src is None else src, dst_ref=rows(*block),
                                                send_sem=send_sems.at[k], recv_sem=recv_sems.at[k],
                                                device_id=to, device_id_type=MESH)

        mine = pltpu.make_async_copy(v_ref, rows(*me), local_sem)
        mine.start()
        first = [copy(0, me, sibling, src=v_ref)]
        first += [copy(1 + j, me, (*chip, c), src=v_ref) for j, chip in enumerate(chips)]
        for cp in first:
            cp.start()
        passed = [copy(4 + j, (*chip, c), sibling) for j, chip in enumerate(chips)]
        for j, chip in enumerate(chips):
            copy(1 + j, (*chip, c), me).wait_recv()
            passed[j].start()
        copy(0, sibling, me).wait_recv()
        for j, chip in enumerate(chips):
            copy(4 + j, (*chip, 1 - c), me).wait_recv()
        for cp in first + passed:
            cp.wait_send()
        mine.wait()

    vm = pl.BlockSpec(memory_space=pltpu.VMEM)
    return pl.pallas_call(body, name="gather_small", in_specs=[vm], out_specs=vm,
                          out_shape=jax.ShapeDtypeStruct((8 * m, 128), v.dtype),
                          scratch_shapes=[pltpu.SemaphoreType.DMA((7,)), pltpu.SemaphoreType.DMA((7,)), pltpu.SemaphoreType.DMA])(v)


def _swap_halves(g, name):
    n, rows, _ = g.shape
    half = rows // 2

    def body(g_ref, out_ref, send_sem, recv_sem):
        x, y, c = _place()
        src = g_ref.at[:, pl.ds(pl.multiple_of((1 - c) * half, 8), half), :]
        cp = pltpu.make_async_remote_copy(src_ref=src, dst_ref=out_ref, send_sem=send_sem, recv_sem=recv_sem,
                                          device_id=(x, y, 1 - c), device_id_type=MESH)
        cp.start()
        cp.wait()

    return pl.pallas_call(body, name=name, in_specs=[ANY], out_specs=ANY,
                          out_shape=jax.ShapeDtypeStruct((n, half, 128), g.dtype),
                          scratch_shapes=[pltpu.SemaphoreType.DMA, pltpu.SemaphoreType.DMA])(g)


def _scatter_steps(h_ref, out_ref, send_sems, recv_sems, first=True, last=True):
    x, y, c = _place()

    def copies():
        return [pltpu.make_async_remote_copy(src_ref=h_ref.at[2 * chip[0] + chip[1]], dst_ref=out_ref.at[j],
                                             send_sem=send_sems.at[j], recv_sem=recv_sems.at[j],
                                             device_id=(*chip, c), device_id_type=MESH)
                for j, chip in enumerate(_other_chips(x, y))]

    def start():
        for cp in copies():
            cp.start()

    def finish():
        for cp in copies():
            cp.wait()

    _when(first, start)
    _when(last, finish)


def _scatter_chips(hsum):
    n, half, _ = hsum.shape

    def body(h_ref, out_ref, send_sems, recv_sems):
        _scatter_steps(h_ref, out_ref, send_sems, recv_sems)

    return pl.pallas_call(body, name="scatter_chips", in_specs=[ANY], out_specs=ANY,
                          out_shape=jax.ShapeDtypeStruct((3, half, 128), hsum.dtype),
                          scratch_shapes=list(SCATTER_SEMS))(hsum)


def _join_halves(buf, name):
    half = buf.shape[0] // 2

    def body(buf_ref, out_ref, send_sem, recv_sem):
        del buf_ref
        x, y, c = _place()
        mine = out_ref.at[pl.ds(pl.multiple_of(c * half, 8), half), :]
        other = out_ref.at[pl.ds(pl.multiple_of((1 - c) * half, 8), half), :]
        cp = pltpu.make_async_remote_copy(src_ref=mine, dst_ref=mine, send_sem=send_sem, recv_sem=recv_sem,
                                          device_id=(x, y, 1 - c), device_id_type=MESH)
        cp.start()
        cp.wait_send()
        pltpu.make_async_remote_copy(src_ref=other, dst_ref=other, send_sem=send_sem, recv_sem=recv_sem,
                                     device_id=(x, y, 1 - c), device_id_type=MESH).wait_recv()

    return pl.pallas_call(body, name=name, in_specs=[ANY], out_specs=ANY,
                          out_shape=jax.ShapeDtypeStruct(buf.shape, buf.dtype), input_output_aliases={0: 0},
                          scratch_shapes=[pltpu.SemaphoreType.DMA, pltpu.SemaphoreType.DMA])(buf)


def _add_sibling(g, recv, c_idx, name):
    n, rows, _ = g.shape
    half = rows // 2
    tr = _tile(half, ADAM_ROWS, 16)
    nb = half // tr

    def body(c_ref, g_ref, r_ref, o_ref, ob_ref):
        s = g_ref[...] + r_ref[...]
        o_ref[...] = s
        ob_ref[...] = s.astype(CDT)

    out = pl.BlockSpec((None, tr, 128), lambda k, i, c: (k, i, 0))
    grid_spec = pltpu.PrefetchScalarGridSpec(
        num_scalar_prefetch=1, grid=(n, nb),
        in_specs=[pl.BlockSpec((None, tr, 128), lambda k, i, c: (k, c[0] * nb + i, 0)), out],
        out_specs=(out, out))
    return pl.pallas_call(body, name=name, grid_spec=grid_spec,
                          out_shape=(jax.ShapeDtypeStruct((n, half, 128), F32), jax.ShapeDtypeStruct((n, half, 128), CDT)),
                          compiler_params=_params(("parallel", "parallel")))(c_idx, g, recv)


def _add_chips(hsum, recv, chip_idx, name):
    n, half, _ = hsum.shape
    tr = _tile(half, ADAM_ROWS, 16)

    def body(k_ref, h_ref, r_ref, o_ref):
        o_ref[...] = ((h_ref[...] + r_ref[0].astype(F32)) + r_ref[1].astype(F32)) + r_ref[2].astype(F32)

    grid_spec = pltpu.PrefetchScalarGridSpec(
        num_scalar_prefetch=1, grid=(half // tr,),
        in_specs=[pl.BlockSpec((None, tr, 128), lambda i, k: (k[0], i, 0)),
                  pl.BlockSpec((3, tr, 128), lambda i, k: (0, i, 0))],
        out_specs=pl.BlockSpec((tr, 128), lambda i, k: (i, 0)))
    return pl.pallas_call(body, name=name, grid_spec=grid_spec, out_shape=jax.ShapeDtypeStruct((half, 128), F32),
                          compiler_params=_params(("parallel",)))(chip_idx, hsum, recv)


def _adamw_math(g, w, m, v):
    m2 = B1 * m + (1.0 - B1) * g
    v2 = B2 * v + (1.0 - B2) * (g * g)
    delta = -LR * ((m2 / BC1) / (jnp.sqrt(v2 / BC2) + AEPS) + WD * w)
    return delta, m2, v2


def _adamw(g, w, m, v, name):
    rows, cols = g.shape
    tr = _tile(rows, max(8, (ADAM_ROWS * 128 // cols) // 8 * 8), 8)

    def body(g_ref, w_ref, m_ref, v_ref, d_ref, m2_ref, v2_ref):
        d_ref[...], m2_ref[...], v2_ref[...] = _adamw_math(g_ref[...], w_ref[...], m_ref[...], v_ref[...])

    blk = pl.BlockSpec((tr, cols), lambda i: (i, 0))
    shp = jax.ShapeDtypeStruct((rows, cols), F32)
    return pl.pallas_call(body, name=name, grid=(rows // tr,), in_specs=[blk] * 4, out_specs=(blk,) * 3,
                          out_shape=(shp,) * 3, compiler_params=_params(("parallel",)))(g, w, m, v)


def _adamw_small(parts, w, m, v, name):
    rows = w.shape[0]

    def body(p_ref, w_ref, m_ref, v_ref, g_ref, d_ref, m2_ref, v2_ref):
        g = p_ref[0]
        for k in range(1, 8):
            g = g + p_ref[k]
        g_ref[...] = g
        d_ref[...], m2_ref[...], v2_ref[...] = _adamw_math(g, w_ref[...], m_ref[...], v_ref[...])

    shp = jax.ShapeDtypeStruct((rows, 128), F32)
    return pl.pallas_call(body, name=name, out_shape=(shp,) * 4)(parts, w, m, v)


def _pack_rows(parts, rows):
    flat = jnp.concatenate([p.reshape(-1) for p in parts])
    return jnp.pad(flat, (0, rows * 128 - flat.shape[0])).reshape(rows, 128)


def _unpack(flat, sizes, shapes):
    out, off = [], 0
    for n, s in zip(sizes, shapes):
        out.append(flat[off:off + n].reshape(s))
        off += n
    return out


def _to_shards(full, shard_shape, axis):
    if axis == 0:
        return full.reshape(N_CHIP, -1)
    r, cs = shard_shape
    return full.reshape(r, N_CHIP, cs).transpose(1, 0, 2).reshape(N_CHIP, -1)


def _from_shards(sh, shard_shape, axis):
    r, cs = shard_shape
    if axis == 0:
        return sh.reshape(N_CHIP * r, cs)
    return sh.reshape(N_CHIP, r, cs).transpose(1, 0, 2).reshape(r, N_CHIP * cs)


def _heads(t, n):
    s = t.shape[0]
    return t.reshape(s, n * NH, HD).transpose(1, 0, 2)


def _merge(t):
    return t.transpose(1, 0, 2).reshape(t.shape[1], GW)


def _local_step(x0, mem, tgt, W, gains, ex=None):
    S = x0.shape[0]
    w_in = jnp.pad(W["w_in"], ((0, 0), (0, IN_PAD - IN_COLS)))
    b_f = jnp.pad(gains["b_forget"], ((0, 0), (0, 128 - NH)))

    h1 = _rms_cast(x0, gains["attn_norm_g"], "norm_attn")
    qkv = _heads(_mm_nn(h1, w_in[:, :NQKV], CDT, "proj_qkv"), 6)
    fl = _mm_nn(h1, w_in[:, NQKV:NQKV + 128], F32, "proj_gate")
    cum = _gate_fwd(fl, b_f, "gate_cumsum")
    c_hm = cum[:, :NH].T
    cq, ck = c_hm[:, :, None], c_hm[:, None, :]
    fq, fk, fv, sq, sk, sv = (qkv[n * NH:(n + 1) * NH] for n in range(6))
    fo_h, lse, gathered = _fox_fwd(fq, fk, fv, cq, ck, "fox_fwd", slots=None if ex is None else ex.slots("b"))
    if ex is not None:
        W = {**W, **ex.unpack("b", gathered)}
    cw = W["conv_w"].reshape(3, 2, DFF).transpose(1, 0, 2)
    cb = gains["conv_b"].reshape(2, 1, DFF)
    sq2, sk2 = _log2_operands(sq, sk)
    so_h, s_lt = _sb_fwd(sq2, sk2, sv, "sb_fwd")
    fo, so = _merge(fo_h), _merge(so_h)
    x1, mixed = _out_proj(fo, so, gains["fox_out_g"], gains["sb_out_g"], W["w_out"], x0, "out_proj")

    h2 = _rms_cast(x1, gains["xattn_norm_g"], "norm_xattn")
    mn = _rms_cast(mem, gains["mem_norm_g"], "norm_mem")
    mq = _mm_nn(h2, W["w_mq"], CDT, "proj_mq")
    kv = _mm_nn(mn, W["w_mkv"], CDT, "proj_mkv")
    x2, mo = _xattn_fwd(mq, kv, W["w_mo"], x1, "xattn_fwd")

    h3 = _rms_cast(x2, gains["ffn_norm_g"], "norm_ffn")
    u0 = _mm_nn(h3, W["w_up"], CDT, "ffn_up", tm=512, tn=DFF, halves=True)
    act = _conv_act(u0, cw, cb, "conv_act")
    x3 = _mm_nn(act, W["w_down"], F32, "ffn_down", tm=512, residual=x2)
    loss, dx3, dg_final = _loss_bwd(x3, tgt, gains["final_norm_g"].reshape(1, D), "loss")

    gw, gs = {}, {"final_norm_g": dg_final}
    da = _mm_nt(dx3, W["w_down"], "ffn_down_dx", tn=DFF, out_dtype=CDT)
    gw["w_down"] = _mm_tn(act, dx3, "ffn_down_dw", tka=DFF)
    du, dwb = _conv_act_bwd(u0, da, cw, cb, "conv_act_bwd")
    gw["conv_w"] = dwb[:, :3].transpose(1, 0, 2).reshape(3, 2 * DFF)
    gs["conv_b"] = dwb[:, 3].reshape(1, 2 * DFF)
    du0 = _conv_bwd_input(du, cw, "conv_bwd_input")
    gw["w_up"] = _mm_tn(h3, du0, "ffn_up_dw", tn=DFF, b_halves=True)
    dx2, gs["ffn_norm_g"] = _mm_nt_rmsbwd(du0, W["w_up"], x2, gains["ffn_norm_g"], dx3, "ffn_up_dx", tk=DFF, a_halves=True)

    dmo = _mm_nt(dx2, W["w_mo"], "mo_dx", tn=512, out_dtype=CDT)
    gw["w_mo"] = _mm_tn(mo, dx2, "mo_dw")
    dmq, dkv = _xattn_bwd(mq, kv, dmo, "xattn_bwd")
    gw["w_mq"] = _mm_tn(h2, dmq, "mq_dw")
    dx1, gs["xattn_norm_g"] = _mm_nt_rmsbwd(dmq, W["w_mq"], x1, gains["xattn_norm_g"], dx2, "mq_dx")
    gw["w_mkv"] = _mm_tn(mn, dkv, "mkv_dw")
    _, gs["mem_norm_g"] = _mm_nt_rmsbwd(dkv, W["w_mkv"], mem, gains["mem_norm_g"], jnp.zeros_like(mem), "mkv_dx")

    gw["w_out"] = _mm_tn(mixed, dx1, "out_dw")
    dfo, dso, gs["fox_out_g"], gs["sb_out_g"] = _out_proj_bwd(dx1, W["w_out"], fo, so, gains["fox_out_g"], gains["sb_out_g"], "out_dx")
    dfo_h, dso_h = _heads(dfo, 1), _heads(dso, 1)
    dfq, dfk, dfv, dck, dcq = _fox_bwd(fq, fk, fv, cq, ck, fo_h, dfo_h, lse, "fox_bwd")
    pair, pair16 = (None, None) if ex is None else ex.pair_sums("b", gw)
    dsq, dsk, dsv, arrived = _sb_bwd(sq, sq2, sk2, sv, dso_h, s_lt, "sb_bwd", scatter=pair16)
    dc = jnp.pad((dck[:, 0, :] + dcq[:, :, 0]).T, ((0, 0), (0, 128 - NH)))
    dfl, db = _gate_bwd(dc, fl, b_f, "gate_bwd")
    gs["b_forget"] = db[:, :NH]
    dqkv = jnp.concatenate([dfq, dfk.astype(CDT), dfv.astype(CDT), dsq, dsk.astype(CDT), dsv.astype(CDT)], axis=0)
    dproj = jnp.concatenate([dqkv.transpose(1, 0, 2).reshape(S, NQKV), dfl.astype(CDT),
                             jnp.zeros((S, IN_PAD - NQKV - 128), CDT)], axis=1)
    gw["w_in"] = _mm_tn(h1, dproj, "in_dw", tn=IN_PAD)[:, :IN_COLS]
    dx0, gs["attn_norm_g"] = _mm_nt_rmsbwd(dproj, w_in, x0, gains["attn_norm_g"], dx1, "in_dx", tk=IN_PAD)
    return loss, dx0, gw, gs, (pair, arrived)


NAMES = ("attn_norm_g", "w_in", "b_forget", "fox_out_g", "sb_out_g", "w_out", "xattn_norm_g", "mem_norm_g", "w_mq",
         "w_mkv", "w_mo", "ffn_norm_g", "w_up", "conv_w", "conv_b", "w_down", "final_norm_g")


class _Exchange:
    def __init__(self, w):
        self.w = w
        xi, yi, ci = _place()
        self.core = ci
        self.chip = 2 * xi + yi
        self.core_idx = jnp.reshape(ci, (1,)).astype(jnp.int32)
        self.chip_idx = jnp.reshape(self.chip, (1,)).astype(jnp.int32)

    def slots(self, g):
        parts = []
        for name, shape, _ in GROUPS[g]:
            blk = self.w[name].reshape(shape)
            parts.append(lax.bitcast_convert_type(blk, CDT) if name == "conv_w" else blk.astype(CDT))
        rows = _rows_g(GROUPS[g])
        return lax.dynamic_update_slice(lax.empty((N_CHIP, rows, 128), CDT), _pack_rows(parts, rows)[None], (self.chip, 0, 0))

    def unpack(self, g, gathered):
        flat, full, off = gathered.reshape(N_CHIP, -1), {}, 0
        for (name, shape, axis), n in zip(GROUPS[g], _gather_sizes(GROUPS[g])):
            sh = flat[:, off:off + n]
            off += n
            if name == "conv_w":
                sh = lax.bitcast_convert_type(sh.reshape(N_CHIP, n // 2, 2), F32)
            full[name] = _from_shards(sh, shape, axis)
        return full

    def pair_sums(self, g, gw):
        rows = _rows_f(GROUPS[g])
        flat = jnp.concatenate([_to_shards(gw[name], shape, axis) for name, shape, axis in GROUPS[g]], axis=1)
        flat = jnp.pad(flat, ((0, 0), (0, rows * 128 - flat.shape[1]))).reshape(N_CHIP, rows, 128)
        return _add_sibling(flat, _swap_halves(flat, "swap_halves_" + g), self.core_idx, "add_sibling_" + g)

    def finish(self, g, pair, arrived):
        rows = _rows_f(GROUPS[g])
        mine = _add_chips(pair, arrived, self.chip_idx, "add_chips_" + g)
        whole = _join_halves(lax.dynamic_update_slice(lax.empty((rows, 128), F32), mine, (self.core * (rows // 2), 0)), "join_halves_" + g)
        shapes = [s for _, s, _ in GROUPS[g]]
        return {name: arr for (name, _, _), arr in zip(GROUPS[g], _unpack(whole.reshape(-1), _sizes(GROUPS[g]), shapes))}


def _step(x, mem, loss_target, w, m, v):
    ex = _Exchange(w)

    W = ex.unpack("a", _gather_weights(ex.slots("a")))
    gains = {name: w[name].reshape(1, -1) for name, _ in SMALL}

    loss, grad_x, gw, gs, (pair_b, arrived_b) = _local_step(x[0], mem[0], loss_target[0], W, gains, ex)

    grads = ex.finish("b", pair_b, arrived_b)
    pair_a, pair16_a = ex.pair_sums("a", gw)
    grads.update(ex.finish("a", pair_a, _scatter_chips(pair16_a)))
    small = jnp.concatenate([gs[name].reshape(-1) for name, _ in SMALL] + [loss[0, :1]])
    small = jnp.pad(small, (0, ROWS_S * 128 - P_SMALL)).reshape(ROWS_S, 128)
    small_parts = _gather_small(small).reshape(8, ROWS_S, 128)

    def flat_small(d):
        return _pack_rows([d[name] for name, _ in SMALL], ROWS_S)

    outs = {}
    for name, shape, _ in BIG:
        g = grads[name]
        res = _adamw(g, w[name].reshape(shape), m[name].reshape(shape), v[name].reshape(shape), "adamw_" + name)
        for prefix, arr in zip(("grad_", "delta_", "new_m_", "new_v_"), (g, *res)):
            outs[prefix + name] = arr.reshape(w[name].shape)
    small_res = _adamw_small(small_parts, flat_small(w), flat_small(m), flat_small(v), "adamw_small")
    g_sm = small_res[0]
    for prefix, sm in zip(("grad_", "delta_", "new_m_", "new_v_"), small_res):
        for (name, n), arr in zip(SMALL, _unpack(sm.reshape(-1), [n for _, n in SMALL], [(n,) for _, n in SMALL])):
            outs[prefix + name] = arr.reshape(w[name].shape)
    total_loss = g_sm.reshape(-1)[P_SMALL - 1]
    return (total_loss, grad_x[None], *[outs[p + n] for p in ("grad_", "delta_", "new_m_", "new_v_") for n in NAMES])


def kernel(x, mem, attn_norm_g, w_in, b_forget, fox_out_g, sb_out_g, w_out, xattn_norm_g, mem_norm_g, w_mq, w_mkv, w_mo, ffn_norm_g, w_up, conv_w, conv_b, w_down, final_norm_g, loss_target, m_attn_norm_g, m_w_in, m_b_forget, m_fox_out_g, m_sb_out_g, m_w_out, m_xattn_norm_g, m_mem_norm_g, m_w_mq, m_w_mkv, m_w_mo, m_ffn_norm_g, m_w_up, m_conv_w, m_conv_b, m_w_down, m_final_norm_g, v_attn_norm_g, v_w_in, v_b_forget, v_fox_out_g, v_sb_out_g, v_w_out, v_xattn_norm_g, v_mem_norm_g, v_w_mq, v_w_mkv, v_w_mo, v_ffn_norm_g, v_w_up, v_conv_w, v_conv_b, v_w_down, v_final_norm_g):
    given = dict(locals())
    w = {n: given[n] for n in NAMES}
    m = {n: given["m_" + n] for n in NAMES}
    v = {n: given["v_" + n] for n in NAMES}
    return _step(x, mem, loss_target, w, m, v)
```

```python
import functools

import numpy as np
import jax
import jax.numpy as jnp
from jax import lax
from jax.experimental import pallas as pl
from jax.experimental.pallas import tpu as pltpu

F32 = jnp.float32
CDT = jnp.bfloat16
MESH = pl.DeviceIdType.MESH

D = 1024
HD = 64
NH = 8
GW = NH * HD
NQKV = 6 * GW
IN_COLS = NQKV + NH
IN_PAD = NQKV + 256
NMH = 4
MHD = D // NMH
DFF = 2816
EPS = 1e-6
ATT_SCALE = HD ** -0.5
MEM_SCALE = MHD ** -0.5
NEG = -1e30

LR, B1, B2, AEPS, WD, STEP = 0.001, 0.9, 0.999, 1e-08, 0.01, 10
BC1 = 1.0 - B1 ** STEP
BC2 = 1.0 - B2 ** STEP

ATT_TILES = {"fox_fwd": (512, 1024), "fox_bwd": (512, 1024), "sb_fwd": (512, 1024), "sb_bwd": (1024, 1024)}
W_SB = 256
VMEM_LIMIT = 52 * 2 ** 20

N_CHIP = 4
BIG = (("w_in", (D, IN_COLS // N_CHIP), 1), ("w_out", (D // N_CHIP, D), 0), ("w_mq", (D // N_CHIP, D), 0),
       ("w_mkv", (D, 2 * D // N_CHIP), 1), ("w_mo", (D // N_CHIP, D), 0), ("w_up", (D, 2 * DFF // N_CHIP), 1),
       ("conv_w", (3, 2 * DFF // N_CHIP), 1), ("w_down", (DFF // N_CHIP, D), 0))
GROUPS = {"a": BIG[:1], "b": BIG[1:]}
ADAM_ROWS = 1536


def _sizes(group):
    return tuple(int(np.prod(s)) for _, s, _ in group)


def _gather_sizes(group):
    return tuple(2 * n if name == "conv_w" else n for (name, _, _), n in zip(group, _sizes(group)))


def _rows_g(group):
    return -(-sum(_gather_sizes(group)) // 4096) * 32


def _rows_f(group):
    return -(-sum(_sizes(group)) // 65536) * 512
SMALL = (("attn_norm_g", 1024), ("b_forget", 8), ("fox_out_g", 512), ("sb_out_g", 512), ("xattn_norm_g", 1024),
         ("mem_norm_g", 1024), ("ffn_norm_g", 1024), ("conv_b", 2 * DFF), ("final_norm_g", 1024))
P_SMALL = sum(n for _, n in SMALL) + 1
ROWS_S = -(-P_SMALL // 1024) * 8


def _params(sem=None, vmem=VMEM_LIMIT):
    return pltpu.CompilerParams(dimension_semantics=sem, vmem_limit_bytes=vmem)


def _tile(n, pref, mult):
    t = (min(pref, n) // mult) * mult
    while t >= mult:
        if n % t == 0:
            return t
        t -= mult
    return n


def _dot(a, b):
    return jnp.dot(a, b, preferred_element_type=F32)


def _dot_nt(a, b):
    return lax.dot_general(a, b, (((1,), (1,)), ((), ())), preferred_element_type=F32)


def _dot_tn(a, b):
    return lax.dot_general(a, b, (((0,), (0,)), ((), ())), preferred_element_type=F32)


def _split3(x):
    h1 = x.astype(CDT)
    r1 = x - h1.astype(F32)
    h2 = r1.astype(CDT)
    h3 = (r1 - h2.astype(F32)).astype(CDT)
    return h1, h2, h3


def _split2(x):
    h1 = x.astype(CDT)
    return h1, (x - h1.astype(F32)).astype(CDT)


def _rms_bwd(dh, x, g):
    r = lax.rsqrt(jnp.mean(x * x, axis=-1, keepdims=True) + EPS)
    xn = x * r
    dg = jnp.sum(dh * xn, axis=0, keepdims=True)
    dhg = dh * g
    dx = r * (dhg - xn * jnp.mean(dhg * xn, axis=-1, keepdims=True))
    return dx, dg


def _mm_nn(a, b, out_dtype, name, *, tm=1024, tn=512, residual=None, halves=False, heads=False):
    M, K = a.shape
    N = b.shape[1]
    tm = _tile(M, tm, 16)
    tn = _tile(N // 2 if halves else N, tn, 128)
    nj = N // tn

    def body(*refs):
        a_ref, b_ref = refs[0], refs[1]
        o_ref = refs[-1]
        acc = _dot(a_ref[...].astype(CDT), b_ref[...].astype(CDT))
        if residual is not None:
            acc = acc + refs[2][...]
        if heads:
            for h in range(tn // HD):
                o_ref[h] = acc[:, h * HD:(h + 1) * HD].astype(o_ref.dtype)
        else:
            o_ref[...] = acc.astype(o_ref.dtype)

    in_specs = [pl.BlockSpec((tm, K), lambda i, j: (i, 0)), pl.BlockSpec((K, tn), lambda i, j: (0, j))]
    ops = [a, b]
    if residual is not None:
        in_specs.append(pl.BlockSpec((tm, tn), lambda i, j: (i, j)))
        ops.append(residual)
    if halves:
        njh = nj // 2
        out_shape = jax.ShapeDtypeStruct((2, M, N // 2), out_dtype)
        out_spec = pl.BlockSpec((None, tm, tn), lambda i, j: (j // njh, i, j % njh))
    elif heads:
        out_shape = jax.ShapeDtypeStruct((N // HD, M, HD), out_dtype)
        out_spec = pl.BlockSpec((tn // HD, tm, HD), lambda i, j: (j, i, 0))
    else:
        out_shape = jax.ShapeDtypeStruct((M, N), out_dtype)
        out_spec = pl.BlockSpec((tm, tn), lambda i, j: (i, j))
    return pl.pallas_call(body, name=name, grid=(M // tm, nj), in_specs=in_specs, out_specs=out_spec,
                          out_shape=out_shape, compiler_params=_params(("parallel", "parallel")))(*ops)


def _mm_tn(a, b, name, *, tka=512, tn=1024, ts=512, b_halves=False):
    S, Ka = a.shape
    N = 2 * b.shape[2] if b_halves else b.shape[1]
    tka = _tile(Ka, tka, 128)
    tn = _tile(N // 2 if b_halves else N, tn, 128)
    ts = _tile(S, ts, 16)
    nn = N // tn

    def body(a_ref, b_ref, o_ref):
        @pl.when(pl.program_id(2) == 0)
        def _():
            o_ref[...] = jnp.zeros_like(o_ref)
        o_ref[...] += _dot_tn(a_ref[...].astype(CDT), b_ref[...].astype(CDT))

    if b_halves:
        nnh = nn // 2
        b_spec = pl.BlockSpec((None, ts, tn), lambda i, j, s: (j // nnh, s, j % nnh))
    else:
        b_spec = pl.BlockSpec((ts, tn), lambda i, j, s: (s, j))
    return pl.pallas_call(
        body, name=name, grid=(Ka // tka, nn, S // ts),
        in_specs=[pl.BlockSpec((ts, tka), lambda i, j, s: (s, i)), b_spec],
        out_specs=pl.BlockSpec((tka, tn), lambda i, j, s: (i, j)),
        out_shape=jax.ShapeDtypeStruct((Ka, N), F32),
        compiler_params=_params(("parallel", "parallel", "arbitrary")))(a, b)


def _mm_nt(a, b, name, *, tm=512, tn=None, tk=None, a_halves=False, out_dtype=F32,
           epilogue=None, extra=(), extra_specs=(), out_shape=None, out_specs=None):
    if a_halves:
        M, K = a.shape[1], 2 * a.shape[2]
    else:
        M, K = a.shape
    N = b.shape[0]
    tm = _tile(M, tm, 16)
    tn = N if (epilogue is not None or tn is None) else _tile(N, tn, 128)
    tk = K if tk is None else _tile(K // 2 if a_halves else K, tk, 128)
    nk = K // tk
    n_extra = len(extra)

    def body(*refs):
        a_ref, b_ref = refs[0], refs[1]
        extra_refs = refs[2:2 + n_extra]
        out_refs = refs[2 + n_extra:-1]
        acc_ref = refs[-1]
        k = pl.program_id(2)

        @pl.when(k == 0)
        def _():
            acc_ref[...] = jnp.zeros_like(acc_ref)
        acc_ref[...] += _dot_nt(a_ref[...].astype(CDT), b_ref[...].astype(CDT))

        @pl.when(k == nk - 1)
        def _():
            if epilogue is None:
                out_refs[0][...] = acc_ref[...].astype(out_refs[0].dtype)
            else:
                epilogue(acc_ref[...], pl.program_id(0), extra_refs, out_refs)

    if a_halves:
        nkh = nk // 2
        a_spec = pl.BlockSpec((None, tm, tk), lambda i, j, k: (k // nkh, i, k % nkh))
    else:
        a_spec = pl.BlockSpec((tm, tk), lambda i, j, k: (i, k))
    if epilogue is None:
        out_shape = jax.ShapeDtypeStruct((M, N), out_dtype)
        out_specs = pl.BlockSpec((tm, tn), lambda i, j, k: (i, j))
        sem = ("parallel", "parallel", "arbitrary")
    else:
        sem = ("arbitrary", "arbitrary", "arbitrary")
    return pl.pallas_call(
        body, name=name, grid=(M // tm, N // tn, nk),
        in_specs=[a_spec, pl.BlockSpec((tn, tk), lambda i, j, k: (j, k)), *extra_specs],
        out_specs=out_specs, out_shape=out_shape,
        scratch_shapes=[pltpu.VMEM((tm, tn), F32)],
        compiler_params=_params(sem))(a, b, *extra)


def _mm_nt_rmsbwd(a, b, x, g, dres, name, *, tm=512, tk=None, a_halves=False):
    M = x.shape[0]
    tm = _tile(M, tm, 16)

    def epilogue(acc, i, extra_refs, out_refs):
        x_ref, g_ref, r_ref = extra_refs
        dx_ref, dg_ref = out_refs
        dx, dg = _rms_bwd(acc, x_ref[...], g_ref[...])
        dx_ref[...] = r_ref[...] + dx

        @pl.when(i == 0)
        def _():
            dg_ref[...] = jnp.zeros_like(dg_ref)
        dg_ref[...] += dg

    row = pl.BlockSpec((tm, D), lambda i, j, k: (i, 0))
    vec = pl.BlockSpec((1, D), lambda i, j, k: (0, 0))
    return _mm_nt(a, b, name, tm=tm, tk=tk, a_halves=a_halves, epilogue=epilogue,
                  extra=(x, g, dres), extra_specs=(row, vec, row),
                  out_shape=(jax.ShapeDtypeStruct((M, D), F32), jax.ShapeDtypeStruct((1, D), F32)),
                  out_specs=(row, vec))


def _rms_cast(x, g, name, *, tm=512):
    M, W = x.shape
    tm = _tile(M, tm, 16)

    def body(x_ref, g_ref, o_ref):
        xf = x_ref[...]
        r = lax.rsqrt(jnp.mean(xf * xf, axis=-1, keepdims=True) + EPS)
        o_ref[...] = (xf * r * g_ref[...]).astype(o_ref.dtype)

    return pl.pallas_call(body, name=name, grid=(M // tm,),
                          in_specs=[pl.BlockSpec((tm, W), lambda i: (i, 0)), pl.BlockSpec((1, W), lambda i: (0, 0))],
                          out_specs=pl.BlockSpec((tm, W), lambda i: (i, 0)),
                          out_shape=jax.ShapeDtypeStruct((M, W), CDT),
                          compiler_params=_params(("parallel",)))(x, g)


def _tri(n, lower):
    r = lax.broadcasted_iota(jnp.int32, (n, n), 0)
    c = lax.broadcasted_iota(jnp.int32, (n, n), 1)
    return (c <= r if lower else c >= r).astype(CDT)


def _gate_fwd(fl, b, name, *, tm=512):
    S = fl.shape[0]
    tm = _tile(S, tm, 16)

    def body(f_ref, b_ref, c_ref, carry):
        @pl.when(pl.program_id(0) == 0)
        def _():
            carry[...] = jnp.zeros_like(carry)
        z = f_ref[...] + b_ref[...]
        lf = jnp.minimum(z, 0.0) - jnp.log(1.0 + jnp.exp(-jnp.abs(z)))
        tri = _tri(tm, True)
        cum = sum(_dot(tri, p) for p in _split3(lf)) + carry[...]
        c_ref[...] = cum
        carry[...] = cum[tm - 1:tm, :]

    return pl.pallas_call(body, name=name, grid=(S // tm,),
                          in_specs=[pl.BlockSpec((tm, 128), lambda i: (i, 0)), pl.BlockSpec((1, 128), lambda i: (0, 0))],
                          out_specs=pl.BlockSpec((tm, 128), lambda i: (i, 0)),
                          out_shape=jax.ShapeDtypeStruct((S, 128), F32),
                          scratch_shapes=[pltpu.VMEM((1, 128), F32)],
                          compiler_params=_params(("arbitrary",)))(fl, b)


def _gate_bwd(dc, fl, b, name, *, tm=512):
    S = fl.shape[0]
    tm = _tile(S, tm, 16)
    nb = S // tm

    def body(dc_ref, f_ref, b_ref, df_ref, db_ref, carry):
        @pl.when(pl.program_id(0) == 0)
        def _():
            carry[...] = jnp.zeros_like(carry)
            db_ref[...] = jnp.zeros_like(db_ref)
        tri = _tri(tm, False)
        suf = sum(_dot(tri, p) for p in _split3(dc_ref[...])) + carry[...]
        carry[...] = suf[0:1, :]
        df = suf * jax.nn.sigmoid(-(f_ref[...] + b_ref[...]))
        df_ref[...] = df
        db_ref[...] += jnp.sum(df, axis=0, keepdims=True)

    rev = pl.BlockSpec((tm, 128), lambda i: (nb - 1 - i, 0))
    vec = pl.BlockSpec((1, 128), lambda i: (0, 0))
    return pl.pallas_call(body, name=name, grid=(nb,), in_specs=[rev, rev, vec], out_specs=(rev, vec),
                          out_shape=(jax.ShapeDtypeStruct((S, 128), F32), jax.ShapeDtypeStruct((1, 128), F32)),
                          scratch_shapes=[pltpu.VMEM((1, 128), F32)],
                          compiler_params=_params(("arbitrary",)))(dc, fl, b)


def _group_rows(o_ref):
    return jnp.concatenate([o_ref[h] for h in range(NH)], axis=1)


def _out_proj(fo, so, gf, gs, w_out, x0, name, *, tm=512):
    S = fo.shape[1]
    tm = _tile(S, tm, 16)

    def body(fo_ref, so_ref, gf_ref, gs_ref, w_ref, x_ref, x1_ref, mx_ref):
        for ref, g_ref, lo in ((fo_ref, gf_ref, 0), (so_ref, gs_ref, GW)):
            o = _group_rows(ref)
            r = lax.rsqrt(jnp.mean(o * o, axis=-1, keepdims=True) + EPS)
            mx_ref[:, lo:lo + GW] = (o * r * g_ref[...]).astype(CDT)
        x1_ref[...] = x_ref[...] + _dot(mx_ref[...], w_ref[...])

    half = pl.BlockSpec((NH, tm, HD), lambda i: (0, i, 0))
    gvec = pl.BlockSpec((1, GW), lambda i: (0, 0))
    row = pl.BlockSpec((tm, D), lambda i: (i, 0))
    return pl.pallas_call(body, name=name, grid=(S // tm,),
                          in_specs=[half, half, gvec, gvec, pl.BlockSpec((D, D), lambda i: (0, 0)), row],
                          out_specs=(row, row),
                          out_shape=(jax.ShapeDtypeStruct((S, D), F32), jax.ShapeDtypeStruct((S, D), CDT)),
                          compiler_params=_params(("parallel",)))(fo, so, gf, gs, w_out, x0)


def _out_proj_bwd(dx1, w_out, fo, so, gf, gs, name, *, tm=512):
    S = fo.shape[1]
    tm = _tile(S, tm, 16)

    def epilogue(acc, i, extra_refs, out_refs):
        fo_ref, so_ref, gf_ref, gs_ref = extra_refs
        dfo_ref, dso_ref, dgf_ref, dgs_ref = out_refs

        @pl.when(i == 0)
        def _():
            dgf_ref[...] = jnp.zeros_like(dgf_ref)
            dgs_ref[...] = jnp.zeros_like(dgs_ref)
        for lo, o_ref, g_ref, do_ref, dg_ref in ((0, fo_ref, gf_ref, dfo_ref, dgf_ref), (GW, so_ref, gs_ref, dso_ref, dgs_ref)):
            dx, dg = _rms_bwd(acc[:, lo:lo + GW], _group_rows(o_ref), g_ref[...])
            for h in range(NH):
                do_ref[h] = dx[:, h * HD:(h + 1) * HD].astype(do_ref.dtype)
            dg_ref[...] += dg

    half = pl.BlockSpec((NH, tm, HD), lambda i, j, k: (0, i, 0))
    gvec = pl.BlockSpec((1, GW), lambda i, j, k: (0, 0))
    return _mm_nt(dx1, w_out, name, tm=tm, epilogue=epilogue, extra=(fo, so, gf, gs),
                  extra_specs=(half, half, gvec, gvec),
                  out_shape=(jax.ShapeDtypeStruct((NH, S, HD), CDT), jax.ShapeDtypeStruct((NH, S, HD), CDT),
                             jax.ShapeDtypeStruct((1, GW), F32), jax.ShapeDtypeStruct((1, GW), F32)),
                  out_specs=(half, half, gvec, gvec))


def _loss_bwd(x3, tgt, g, name, *, tm=512):
    S = x3.shape[0]
    tm = _tile(S, tm, 16)

    def body(x_ref, t_ref, g_ref, dx_ref, loss_ref, dg_ref):
        @pl.when(pl.program_id(0) == 0)
        def _():
            loss_ref[...] = jnp.zeros_like(loss_ref)
            dg_ref[...] = jnp.zeros_like(dg_ref)
        x = x_ref[...]
        gv = g_ref[...]
        r = lax.rsqrt(jnp.mean(x * x, axis=-1, keepdims=True) + EPS)
        xn = x * r
        err = xn * gv - t_ref[...]
        loss_ref[...] += jnp.full(loss_ref.shape, 0.5 * jnp.sum(jnp.mean(err * err, axis=-1, keepdims=True)), F32)
        dy = err * (1.0 / D)
        dg_ref[...] += jnp.sum(dy * xn, axis=0, keepdims=True)
        dyg = dy * gv
        dx_ref[...] = r * (dyg - xn * jnp.mean(dyg * xn, axis=-1, keepdims=True))

    row = pl.BlockSpec((tm, D), lambda i: (i, 0))
    vec = pl.BlockSpec((1, D), lambda i: (0, 0))
    dx3, loss, dg = pl.pallas_call(
        body, name=name, grid=(S // tm,), in_specs=[row, row, vec],
        out_specs=(row, pl.BlockSpec((1, 128), lambda i: (0, 0)), vec),
        out_shape=(jax.ShapeDtypeStruct((S, D), F32), jax.ShapeDtypeStruct((1, 128), F32), jax.ShapeDtypeStruct((1, D), F32)),
        compiler_params=_params(("arbitrary",)))(x3, tgt, g)
    return loss, dx3, dg


MASKED, FIRST, LAST = 1, 2, 4


def _att_tiles(name, S):
    tq, tk = ATT_TILES[name]
    return min(tq, S), min(tk, S)


def _pairs(S, tq, tk, descending=True):
    assert tk % tq == 0 and S % tk == 0
    qi, kj, fl = [], [], []
    for i in range(S // tq):
        last = ((i + 1) * tq - 1) // tk
        order = list(range(last, -1, -1) if descending else range(last + 1))
        for pos, kb in enumerate(order):
            qi.append(i)
            kj.append(kb)
            fl.append((MASKED if (kb + 1) * tk - 1 > i * tq else 0) | (FIRST if pos == 0 else 0) | (LAST if pos == last else 0))
    return tuple(jnp.asarray(np.asarray(a, np.int32)) for a in (qi, kj, fl))


def _att_specs(tq, tk, width=HD):
    qblk = pl.BlockSpec((1, tq, width), lambda h, n, qi, kj, fl: (h, qi[n], 0))
    kblk = pl.BlockSpec((1, tk, width), lambda h, n, qi, kj, fl: (h, kj[n], 0))
    qcol = pl.BlockSpec((1, tq, 1), lambda h, n, qi, kj, fl: (h, qi[n], 0))
    krow = pl.BlockSpec((1, 1, tk), lambda h, n, qi, kj, fl: (h, 0, kj[n]))
    return qblk, kblk, qcol, krow


def _causal(tq, w, ahead, strict):
    diff = lax.broadcasted_iota(jnp.int32, (tq, w), 1) - lax.broadcasted_iota(jnp.int32, (tq, w), 0)
    return diff < ahead if strict else diff <= ahead


def _masked_or_not(flags, step):
    pl.when(flags % 2 == 1)(functools.partial(step, True))
    pl.when(flags % 2 == 0)(functools.partial(step, False))


def _fox_fwd(q, k, v, cq, ck, name, slots=None):
    S = q.shape[1]
    tq, tk = _att_tiles("fox_fwd", S)
    qi, kj, fl = _pairs(S, tq, tk)
    qblk, kblk, qcol, krow = _att_specs(tq, tk)
    npairs = int(qi.shape[0])

    def body(qi_ref, kj_ref, fl_ref, q_ref, k_ref, v_ref, cq_ref, ck_ref, *rest):
        if slots is None:
            o_ref, lse_ref, m_s, l_s, acc_s = rest
        else:
            _, o_ref, lse_ref, slots_ref, m_s, l_s, acc_s, send_sems, recv_sems = rest
        h, n = pl.program_id(0), pl.program_id(1)
        i, kb, flags = qi_ref[n], kj_ref[n], fl_ref[n]
        if slots is not None:
            _gather_steps(slots_ref, send_sems, recv_sems, first=(h == 0) & (n == 0), middle=(h == NH // 2) & (n == 0),
                          last=(h == NH - 1) & (n == npairs - 1))

        @pl.when(flags & FIRST != 0)
        def _():
            m_s[...] = jnp.full_like(m_s, NEG)
            l_s[...] = jnp.zeros_like(l_s)
            acc_s[...] = jnp.zeros_like(acc_s)

        def step(masked):
            s = _dot_nt(q_ref[0] * ATT_SCALE, k_ref[0]) + cq_ref[0] - ck_ref[0]
            if masked:
                s = jnp.where(_causal(tq, tk, i * tq - kb * tk, False), s, NEG)
            m_new = jnp.maximum(m_s[...], jnp.max(s, axis=-1, keepdims=True))
            alpha = jnp.exp(m_s[...] - m_new)
            p = jnp.exp(s - m_new)
            l_s[...] = alpha * l_s[...] + jnp.sum(p, axis=-1, keepdims=True)
            acc_s[...] = alpha * acc_s[...] + _dot(p.astype(CDT), v_ref[0])
            m_s[...] = m_new

        _masked_or_not(flags, step)

        @pl.when(flags & LAST != 0)
        def _():
            o_ref[0] = acc_s[...] / l_s[...]
            lse_ref[0] = m_s[...] + jnp.log(l_s[...])

    scratch = [pltpu.VMEM((tq, 1), F32), pltpu.VMEM((tq, 1), F32), pltpu.VMEM((tq, HD), F32)]
    out_shape = (jax.ShapeDtypeStruct((NH, S, HD), F32), jax.ShapeDtypeStruct((NH, S, 1), F32))
    if slots is None:
        grid_spec = pltpu.PrefetchScalarGridSpec(num_scalar_prefetch=3, grid=(NH, npairs), in_specs=[qblk, kblk, kblk, qcol, krow],
                                                 out_specs=(qblk, qcol), scratch_shapes=scratch)
        o, lse = pl.pallas_call(body, name=name, grid_spec=grid_spec, out_shape=out_shape,
                                compiler_params=_params(("parallel", "arbitrary")))(qi, kj, fl, q, k, v, cq, ck)
        return o, lse, None
    grid_spec = pltpu.PrefetchScalarGridSpec(num_scalar_prefetch=3, grid=(NH, npairs), in_specs=[qblk, kblk, kblk, qcol, krow, ANY],
                                             out_specs=(qblk, qcol, ANY), scratch_shapes=scratch + list(GATHER_SEMS))
    return pl.pallas_call(body, name=name, grid_spec=grid_spec, out_shape=(*out_shape, jax.ShapeDtypeStruct(slots.shape, slots.dtype)),
                          input_output_aliases={8: 2},
                          compiler_params=_params(("arbitrary", "arbitrary")))(qi, kj, fl, q, k, v, cq, ck, slots)


def _fox_bwd(q, k, v, cq, ck, o, do, lse, name):
    S = q.shape[1]
    tq, tk = _att_tiles("fox_bwd", S)
    qi, kj, fl = _pairs(S, tq, tk)
    qblk, kblk, qcol, krow = _att_specs(tq, tk)

    def body(qi_ref, kj_ref, fl_ref, q_ref, k_ref, v_ref, cq_ref, ck_ref, o_ref, do_ref, lse_ref,
             dq_ref, dk_ref, dv_ref, dck_ref, dcq_ref, dq_s, dl_s, dcq_s):
        n = pl.program_id(1)
        i, kb, flags = qi_ref[n], kj_ref[n], fl_ref[n]

        @pl.when(n == 0)
        def _():
            dk_ref[...] = jnp.zeros_like(dk_ref)
            dv_ref[...] = jnp.zeros_like(dv_ref)
            dck_ref[...] = jnp.zeros_like(dck_ref)

        @pl.when(flags & FIRST != 0)
        def _():
            dq_s[...] = jnp.zeros_like(dq_s)
            dcq_s[...] = jnp.zeros_like(dcq_s)
            dl_s[...] = jnp.sum(do_ref[0].astype(F32) * o_ref[0], axis=-1, keepdims=True)

        def step(masked):
            qs = q_ref[0] * ATT_SCALE
            do = do_ref[0]
            p = jnp.exp(_dot_nt(qs, k_ref[0]) + cq_ref[0] - ck_ref[0] - lse_ref[0])
            if masked:
                p = jnp.where(_causal(tq, tk, i * tq - kb * tk, False), p, 0.0)
            ds = p * (_dot_nt(do, v_ref[0]) - dl_s[...])
            dsb = ds.astype(CDT)
            dq_s[...] += _dot(dsb, k_ref[0])
            rows = pl.ds(pl.multiple_of(kb * tk, tk), tk)
            dk_ref[0, rows, :] += _dot_tn(dsb, qs)
            dv_ref[0, rows, :] += _dot_tn(p.astype(CDT), do)
            dck_ref[0, :, rows] += -jnp.sum(ds, axis=0, keepdims=True)
            dcq_s[...] += jnp.sum(ds, axis=-1, keepdims=True)

        _masked_or_not(flags, step)

        @pl.when(flags & LAST != 0)
        def _():
            dq_ref[0] = (dq_s[...] * ATT_SCALE).astype(dq_ref.dtype)
            dcq_ref[0] = dcq_s[...]

    whole = pl.BlockSpec((1, S, HD), lambda h, n, qi, kj, fl: (h, 0, 0))
    grid_spec = pltpu.PrefetchScalarGridSpec(
        num_scalar_prefetch=3, grid=(NH, int(qi.shape[0])),
        in_specs=[qblk, kblk, kblk, qcol, krow, qblk, qblk, qcol],
        out_specs=(qblk, whole, whole, pl.BlockSpec((1, 1, S), lambda h, n, qi, kj, fl: (h, 0, 0)), qcol),
        scratch_shapes=[pltpu.VMEM((tq, HD), F32), pltpu.VMEM((tq, 1), F32), pltpu.VMEM((tq, 1), F32)])
    return pl.pallas_call(body, name=name, grid_spec=grid_spec,
                          out_shape=(jax.ShapeDtypeStruct((NH, S, HD), CDT), jax.ShapeDtypeStruct((NH, S, HD), F32),
                                     jax.ShapeDtypeStruct((NH, S, HD), F32), jax.ShapeDtypeStruct((NH, 1, S), F32),
                                     jax.ShapeDtypeStruct((NH, S, 1), F32)),
                          compiler_params=_params(("parallel", "arbitrary")))(qi, kj, fl, q, k, v, cq, ck, o, do, lse)


LOG2E = 1.4426950408889634


def _log2_operands(q, k):
    qf = q.astype(F32) * (ATT_SCALE * LOG2E)
    hi = lax.reduce_precision(qf, 8, 7)
    return jnp.concatenate([hi.astype(CDT), (qf - hi).astype(CDT)], axis=-1), jnp.concatenate([k, k], axis=-1)


def _sb_softplus2(q2, k2sub, mask):
    z2 = _dot_nt(q2, k2sub)
    sp2 = jnp.maximum(z2, 0.0) + jnp.log2(1.0 + jnp.exp2(-jnp.abs(z2)))
    return z2, sp2 if mask is None else jnp.where(mask, sp2, 0.0)


def _strict_tri(n, upper, value):
    r = lax.broadcasted_iota(jnp.int32, (n, n), 0)
    c = lax.broadcasted_iota(jnp.int32, (n, n), 1)
    return jnp.where(r < c if upper else r > c, value, 0.0).astype(CDT)


def _sb_fwd(q2, k2, v, name):
    S = q2.shape[1]
    tq, tk = _att_tiles("sb_fwd", S)
    W = min(W_SB, tk)
    qi, kj, fl = _pairs(S, tq, tk)
    qblk, kblk, qcol, _ = _att_specs(tq, tk)
    q2blk, k2blk, _, _ = _att_specs(tq, tk, 2 * HD)

    def body(qi_ref, kj_ref, fl_ref, q_ref, k_ref, v_ref, o_ref, lt_ref, run_s, acc_s):
        n = pl.program_id(1)
        i, kb, flags = qi_ref[n], kj_ref[n], fl_ref[n]

        @pl.when(flags & FIRST != 0)
        def _():
            run_s[...] = jnp.zeros_like(run_s)
            acc_s[...] = jnp.zeros_like(acc_s)

        def step(masked):
            neg_later = _strict_tri(W, False, -1.0)
            run = run_s[...]
            acc = acc_s[...]
            for sub in range(tk // W - 1, -1, -1):
                cols = slice(sub * W, (sub + 1) * W)
                mask = _causal(tq, W, i * tq - kb * tk - sub * W, True) if masked else None
                z2, sp2 = _sb_softplus2(q_ref[0], k_ref[0, cols, :], mask)
                excl = _dot(sp2.astype(CDT), neg_later)
                a = jnp.exp2((z2 - sp2) + (excl + run))
                if masked:
                    a = jnp.where(mask, a, 0.0)
                acc = acc + _dot(a.astype(CDT), v_ref[0, cols, :])
                run = run + (excl[:, 0:1] - sp2[:, 0:1])
            run_s[...] = run
            acc_s[...] = acc

        _masked_or_not(flags, step)

        @pl.when(flags & LAST != 0)
        def _():
            o_ref[0] = acc_s[...]
            lt_ref[0] = run_s[...]

    grid_spec = pltpu.PrefetchScalarGridSpec(
        num_scalar_prefetch=3, grid=(NH, int(qi.shape[0])), in_specs=[q2blk, k2blk, kblk], out_specs=(qblk, qcol),
        scratch_shapes=[pltpu.VMEM((tq, 1), F32), pltpu.VMEM((tq, HD), F32)])
    return pl.pallas_call(body, name=name, grid_spec=grid_spec,
                          out_shape=(jax.ShapeDtypeStruct((NH, S, HD), F32), jax.ShapeDtypeStruct((NH, S, 1), F32)),
                          compiler_params=_params(("parallel", "arbitrary")))(qi, kj, fl, q2, k2, v)


def _sb_bwd(q, q2, k2, v, do, lt, name, scatter=None):
    S = q.shape[1]
    tq, tk = _att_tiles("sb_bwd", S)
    W = min(W_SB, tk)
    qi, kj, fl = _pairs(S, tq, tk, descending=False)
    qblk, kblk, qcol, _ = _att_specs(tq, tk)
    q2blk, k2blk, _, _ = _att_specs(tq, tk, 2 * HD)
    npairs = int(qi.shape[0])

    def body(qi_ref, kj_ref, fl_ref, q_ref, q2_ref, k2_ref, v_ref, do_ref, lt_ref, *rest):
        if scatter is None:
            dq_ref, dk_ref, dv_ref, passed_s, gsum_s, dq_s = rest
        else:
            h_ref, dq_ref, dk_ref, dv_ref, recv_ref, passed_s, gsum_s, dq_s, send_sems, recv_sems = rest
        n = pl.program_id(1)
        i, kb, flags = qi_ref[n], kj_ref[n], fl_ref[n]
        if scatter is not None:
            h = pl.program_id(0)
            _scatter_steps(h_ref, recv_ref, send_sems, recv_sems, first=(h == 0) & (n == 0), last=(h == NH - 1) & (n == npairs - 1))

        @pl.when(n == 0)
        def _():
            dk_ref[...] = jnp.zeros_like(dk_ref)
            dv_ref[...] = jnp.zeros_like(dv_ref)

        @pl.when(flags & FIRST != 0)
        def _():
            passed_s[...] = jnp.zeros_like(passed_s)
            gsum_s[...] = jnp.zeros_like(gsum_s)
            dq_s[...] = jnp.zeros_like(dq_s)

        def step(masked):
            qs = q_ref[0] * ATT_SCALE
            do = do_ref[0]
            neg_later = _strict_tri(W, False, -1.0)
            earlier = _strict_tri(W, True, 1.0)
            for sub in range(tk // W):
                cols = slice(sub * W, (sub + 1) * W)
                mask = _causal(tq, W, i * tq - kb * tk - sub * W, True) if masked else None
                ksub = k2_ref[0, cols, 0:HD]
                z2, sp2 = _sb_softplus2(q2_ref[0], k2_ref[0, cols, :], mask)
                excl = _dot(sp2.astype(CDT), neg_later)
                through = passed_s[...] + (excl[:, 0:1] - sp2[:, 0:1])
                t1 = z2 - sp2
                sig = jnp.exp2(t1)
                a = jnp.exp2(t1 + (excl + (lt_ref[0] - through)))
                if masked:
                    a = jnp.where(mask, a, 0.0)
                dl = _dot_nt(do, v_ref[0, cols, :]) * a
                before = _dot(dl.astype(CDT), earlier)
                dz = dl - sig * (dl + (before + gsum_s[...]))
                if masked:
                    dz = jnp.where(mask, dz, 0.0)
                dzb = dz.astype(CDT)
                dq_s[...] += _dot(dzb, ksub)
                rows = pl.ds(pl.multiple_of(kb * tk + sub * W, W), W)
                dk_ref[0, rows, :] += _dot_tn(dzb, qs)
                dv_ref[0, rows, :] += _dot_tn(a.astype(CDT), do)
                passed_s[...] = through
                gsum_s[...] += before[:, W - 1:W] + dl[:, W - 1:W]

        _masked_or_not(flags, step)

        @pl.when(flags & LAST != 0)
        def _():
            dq_ref[0] = (dq_s[...] * ATT_SCALE).astype(dq_ref.dtype)

    whole = pl.BlockSpec((1, S, HD), lambda h, n, qi, kj, fl: (h, 0, 0))
    in_specs = [qblk, q2blk, k2blk, kblk, qblk, qcol]
    out_specs = (qblk, whole, whole)
    out_shape = (jax.ShapeDtypeStruct((NH, S, HD), CDT), jax.ShapeDtypeStruct((NH, S, HD), F32), jax.ShapeDtypeStruct((NH, S, HD), F32))
    scratch = [pltpu.VMEM((tq, 1), F32), pltpu.VMEM((tq, 1), F32), pltpu.VMEM((tq, HD), F32)]
    if scatter is None:
        grid_spec = pltpu.PrefetchScalarGridSpec(num_scalar_prefetch=3, grid=(NH, npairs), in_specs=in_specs,
                                                 out_specs=out_specs, scratch_shapes=scratch)
        return (*pl.pallas_call(body, name=name, grid_spec=grid_spec, out_shape=out_shape,
                                compiler_params=_params(("parallel", "arbitrary")))(qi, kj, fl, q, q2, k2, v, do, lt), None)
    grid_spec = pltpu.PrefetchScalarGridSpec(num_scalar_prefetch=3, grid=(NH, npairs), in_specs=in_specs + [ANY],
                                             out_specs=(*out_specs, ANY), scratch_shapes=scratch + list(SCATTER_SEMS))
    recv_shape = jax.ShapeDtypeStruct((3,) + scatter.shape[1:], scatter.dtype)
    return pl.pallas_call(body, name=name, grid_spec=grid_spec, out_shape=(*out_shape, recv_shape),
                          compiler_params=_params(("arbitrary", "arbitrary")))(qi, kj, fl, q, q2, k2, v, do, lt, scatter)


def _mem_probs(q_ref, kv_ref, h):
    cols = slice(h * MHD, (h + 1) * MHD)
    s = _dot_nt(q_ref[:, cols], kv_ref[:, cols]) * MEM_SCALE
    e = jnp.exp(s - jnp.max(s, axis=-1, keepdims=True))
    return e / jnp.sum(e, axis=-1, keepdims=True)


def _xattn_fwd(q, kv, w_mo, x1, name, *, tm=512):
    S = q.shape[0]
    tm = _tile(S, tm, 16)
    nm = kv.shape[0]

    def body(q_ref, kv_ref, w_ref, x_ref, x2_ref, o_ref):
        for h in range(NMH):
            p = _mem_probs(q_ref, kv_ref, h)
            o_ref[:, h * MHD:(h + 1) * MHD] = _dot(p.astype(CDT), kv_ref[:, D + h * MHD:D + (h + 1) * MHD]).astype(CDT)
        x2_ref[...] = x_ref[...] + _dot(o_ref[...], w_ref[...])

    row = pl.BlockSpec((tm, D), lambda i: (i, 0))
    return pl.pallas_call(body, name=name, grid=(S // tm,),
                          in_specs=[row, pl.BlockSpec((nm, 2 * D), lambda i: (0, 0)), pl.BlockSpec((D, D), lambda i: (0, 0)), row],
                          out_specs=(row, row),
                          out_shape=(jax.ShapeDtypeStruct((S, D), F32), jax.ShapeDtypeStruct((S, D), CDT)),
                          compiler_params=_params(("parallel",)))(q, kv, w_mo, x1)


def _xattn_bwd(q, kv, do, name, *, tm=512):
    S = q.shape[0]
    tm = _tile(S, tm, 16)
    nm = kv.shape[0]

    def body(q_ref, kv_ref, do_ref, dq_ref, dkv_ref):
        @pl.when(pl.program_id(0) == 0)
        def _():
            dkv_ref[...] = jnp.zeros_like(dkv_ref)
        for h in range(NMH):
            cols = slice(h * MHD, (h + 1) * MHD)
            vcols = slice(D + h * MHD, D + (h + 1) * MHD)
            p = _mem_probs(q_ref, kv_ref, h)
            doh = do_ref[:, cols]
            dp = _dot_nt(doh, kv_ref[:, vcols])
            ds = (p * (dp - jnp.sum(p * dp, axis=-1, keepdims=True)) * MEM_SCALE).astype(CDT)
            dq_ref[:, cols] = _dot(ds, kv_ref[:, cols]).astype(CDT)
            dkv_ref[:, cols] += _dot_tn(ds, q_ref[:, cols])
            dkv_ref[:, vcols] += _dot_tn(p.astype(CDT), doh)

    row = pl.BlockSpec((tm, D), lambda i: (i, 0))
    kvs = pl.BlockSpec((nm, 2 * D), lambda i: (0, 0))
    return pl.pallas_call(body, name=name, grid=(S // tm,), in_specs=[row, kvs, row], out_specs=(row, kvs),
                          out_shape=(jax.ShapeDtypeStruct((S, D), CDT), jax.ShapeDtypeStruct((nm, 2 * D), F32)),
                          compiler_params=_params(("arbitrary",)))(q, kv, do)


HALO = 16
SLAB = 8


def _shift_down(u, prev, s):
    rolled = pltpu.roll(u, s, 0)
    top = rolled[0:SLAB]
    r = lax.broadcasted_iota(jnp.int32, top.shape, 0)
    for t in range(s):
        top = jnp.where(r == t, prev[HALO - s + t:HALO - s + t + 1, :], top)
    return jnp.concatenate([top, rolled[SLAB:]], axis=0)


def _shift_up(u, nxt, s):
    n = u.shape[0]
    rolled = pltpu.roll(u, n - s, 0)
    bottom = rolled[n - SLAB:]
    r = lax.broadcasted_iota(jnp.int32, bottom.shape, 0)
    for t in range(s):
        bottom = jnp.where(r == SLAB - s + t, nxt[t:t + 1, :], bottom)
    return jnp.concatenate([rolled[:n - SLAB], bottom], axis=0)


def _conv_taps(u_ref, h_ref, first):
    u = u_ref[...].astype(F32)
    prev = jnp.where(first, 0.0, h_ref[...].astype(F32))
    out = []
    for half in range(2):
        out.append((u[half], _shift_down(u[half], prev[half], 1), _shift_down(u[half], prev[half], 2)))
    return out


def _conv_specs(tm, tn, nsb):
    blk = pl.BlockSpec((2, tm, tn), lambda j, i: (0, i, j))
    prev = pl.BlockSpec((2, HALO, tn), lambda j, i: (0, jnp.maximum(i * (tm // HALO) - 1, 0), j))
    nxt = pl.BlockSpec((2, HALO, tn), lambda j, i: (0, jnp.minimum((i + 1) * (tm // HALO), nsb - 1), j))
    w = pl.BlockSpec((2, 3, tn), lambda j, i: (0, 0, j))
    b = pl.BlockSpec((2, 1, tn), lambda j, i: (0, 0, j))
    return blk, prev, nxt, w, b


def _conv_apply(taps, w_ref, b_ref):
    ys = []
    for half in range(2):
        u, u1, u2 = taps[half]
        w = w_ref[half]
        ys.append(b_ref[half] + u2 * w[0:1, :] + u1 * w[1:2, :] + u * w[2:3, :])
    return ys


def _conv_act(u0, cw, cb, name, *, tm=2048, tn=256):
    _, S, F = u0.shape
    tm = _tile(S, tm, HALO)
    tn = _tile(F, tn, 128)
    blk, prev, _, w, b = _conv_specs(tm, tn, S // HALO)

    def body(u_ref, h_ref, w_ref, b_ref, a_ref):
        yg, yv = _conv_apply(_conv_taps(u_ref, h_ref, pl.program_id(1) == 0), w_ref, b_ref)
        a_ref[...] = (yg * jax.nn.sigmoid(yg) * yv).astype(a_ref.dtype)

    return pl.pallas_call(body, name=name, grid=(F // tn, S // tm), in_specs=[blk, prev, w, b],
                          out_specs=pl.BlockSpec((tm, tn), lambda j, i: (i, j)),
                          out_shape=jax.ShapeDtypeStruct((S, F), CDT),
                          compiler_params=_params(("parallel", "parallel")))(u0, u0, cw, cb)


def _conv_act_bwd(u0, da, cw, cb, name, *, tm=2048, tn=256):
    _, S, F = u0.shape
    tm = _tile(S, tm, HALO)
    tn = _tile(F, tn, 128)
    blk, prev, _, w, b = _conv_specs(tm, tn, S // HALO)

    def body(u_ref, h_ref, da_ref, w_ref, b_ref, du_ref, dwb_ref):
        @pl.when(pl.program_id(1) == 0)
        def _():
            dwb_ref[...] = jnp.zeros_like(dwb_ref)
        taps = _conv_taps(u_ref, h_ref, pl.program_id(1) == 0)
        yg, yv = _conv_apply(taps, w_ref, b_ref)
        sg = jax.nn.sigmoid(yg)
        da = da_ref[...].astype(F32)
        dus = (da * yv * sg * (1.0 + yg * (1.0 - sg)), da * yg * sg)
        for half in range(2):
            du = dus[half]
            du_ref[half] = du.astype(du_ref.dtype)
            u, u1, u2 = taps[half]
            for row, term in enumerate((du * u2, du * u1, du * u, du)):
                dwb_ref[half, row:row + 1, :] += jnp.sum(term, axis=0, keepdims=True)

    return pl.pallas_call(body, name=name, grid=(F // tn, S // tm),
                          in_specs=[blk, prev, pl.BlockSpec((tm, tn), lambda j, i: (i, j)), w, b],
                          out_specs=(blk, pl.BlockSpec((2, 4, tn), lambda j, i: (0, 0, j))),
                          out_shape=(jax.ShapeDtypeStruct((2, S, F), CDT), jax.ShapeDtypeStruct((2, 4, F), F32)),
                          compiler_params=_params(("parallel", "arbitrary")))(u0, u0, da, cw, cb)


def _conv_bwd_input(du, cw, name, *, tm=2048, tn=256):
    _, S, F = du.shape
    tm = _tile(S, tm, HALO)
    tn = _tile(F, tn, 128)
    blk, _, nxt, w, _ = _conv_specs(tm, tn, S // HALO)
    ni = S // tm

    def body(d_ref, h_ref, w_ref, o_ref):
        d = d_ref[...].astype(F32)
        nx = jnp.where(pl.program_id(1) == ni - 1, 0.0, h_ref[...].astype(F32))
        for half in range(2):
            wv = w_ref[half]
            y = d[half] * wv[2:3, :] + _shift_up(d[half], nx[half], 1) * wv[1:2, :] + _shift_up(d[half], nx[half], 2) * wv[0:1, :]
            o_ref[half] = y.astype(o_ref.dtype)

    return pl.pallas_call(body, name=name, grid=(F // tn, ni), in_specs=[blk, nxt, w], out_specs=blk,
                          out_shape=jax.ShapeDtypeStruct((2, S, F), CDT),
                          compiler_params=_params(("parallel", "parallel")))(du, du, cw)


ANY = pl.BlockSpec(memory_space=pl.ANY)


def _place():
    return lax.axis_index("x"), lax.axis_index("y"), lax.axis_index("c")


def _other_chips(x, y):
    return ((1 - x, y), (x, 1 - y), (1 - x, 1 - y))


def _when(pred, fn):
    if pred is True:
        fn()
    else:
        pl.when(pred)(fn)


GATHER_SEMS = (pltpu.SemaphoreType.DMA((6,)), pltpu.SemaphoreType.DMA((6,)))
SCATTER_SEMS = (pltpu.SemaphoreType.DMA((3,)), pltpu.SemaphoreType.DMA((3,)))


def _gather_steps(out_ref, send_sems, recv_sems, first=True, middle=True, last=True):
    half = out_ref.shape[1] // 2
    x, y, c = _place()
    chips = _other_chips(x, y)

    def part(chip, pc):
        return out_ref.at[2 * chip[0] + chip[1], pl.ds(pl.multiple_of(pc * half, 16), half), :]

    def copy(k, chip, pc, to):
        return pltpu.make_async_remote_copy(src_ref=part(chip, pc), dst_ref=part(chip, pc),
                                            send_sem=send_sems.at[k], recv_sem=recv_sems.at[k],
                                            device_id=to, device_id_type=MESH)

    def send_mine():
        for j, chip in enumerate(chips):
            copy(j, (x, y), c, (*chip, c)).start()

    def pass_on():
        for j, chip in enumerate(chips):
            copy(j, chip, c, (x, y, c)).wait_recv()
            copy(3 + j, chip, c, (x, y, 1 - c)).start()

    def finish():
        for j, chip in enumerate(chips):
            copy(3 + j, chip, 1 - c, (x, y, c)).wait_recv()
        for j, chip in enumerate(chips):
            copy(j, (x, y), c, (*chip, c)).wait_send()
            copy(3 + j, chip, c, (x, y, 1 - c)).wait_send()

    _when(first, send_mine)
    _when(middle, pass_on)
    _when(last, finish)


def _gather_weights(buf):
    def body(buf_ref, out_ref, send_sems, recv_sems):
        del buf_ref
        _gather_steps(out_ref, send_sems, recv_sems)

    return pl.pallas_call(body, name="gather_weights", in_specs=[ANY], out_specs=ANY,
                          out_shape=jax.ShapeDtypeStruct(buf.shape, buf.dtype), input_output_aliases={0: 0},
                          scratch_shapes=list(GATHER_SEMS))(buf)


def _gather_small(v):
    m = v.shape[0]

    def body(v_ref, out_ref, send_sems, recv_sems, local_sem):
        x, y, c = _place()
        me, sibling = (x, y, c), (x, y, 1 - c)
        chips = _other_chips(x, y)

        def rows(px, py, pc):
            return out_ref.at[pl.ds((4 * px + 2 * py + pc) * m, m), :]

        def copy(k, block, to, src=None):
            return pltpu.make_async_remote_copy(src_ref=rows(*block) if src is None else src, dst_ref=rows(*block),
                                                send_sem=send_sems.at[k], recv_sem=recv_sems.at[k],
                                                device_id=to, device_id_type=MESH)

        mine = pltpu.make_async_copy(v_ref, rows(*me), local_sem)
        mine.start()
        first = [copy(0, me, sibling, src=v_ref)]
        first += [copy(1 + j, me, (*chip, c), src=v_ref) for j, chip in enumerate(chips)]
        for cp in first:
            cp.start()
        passed = [copy(4 + j, (*chip, c), sibling) for j, chip in enumerate(chips)]
        for j, chip in enumerate(chips):
            copy(1 + j, (*chip, c), me).wait_recv()
            passed[j].start()
        copy(0, sibling, me).wait_recv()
        for j, chip in enumerate(chips):
            copy(4 + j, (*chip, 1 - c), me).wait_recv()
        for cp in first + passed:
            cp.wait_send()
        mine.wait()

    vm = pl.BlockSpec(memory_space=pltpu.VMEM)
    return pl.pallas_call(body, name="gather_small", in_specs=[vm], out_specs=vm,
                          out_shape=jax.ShapeDtypeStruct((8 * m, 128), v.dtype),
                          scratch_shapes=[pltpu.SemaphoreType.DMA((7,)), pltpu.SemaphoreType.DMA((7,)), pltpu.SemaphoreType.DMA])(v)


def _swap_halves(g, name):
    n, rows, _ = g.shape
    half = rows // 2

    def body(g_ref, out_ref, send_sem, recv_sem):
        x, y, c = _place()
        src = g_ref.at[:, pl.ds(pl.multiple_of((1 - c) * half, 8), half), :]
        cp = pltpu.make_async_remote_copy(src_ref=src, dst_ref=out_ref, send_sem=send_sem, recv_sem=recv_sem,
                                          device_id=(x, y, 1 - c), device_id_type=MESH)
        cp.start()
        cp.wait()

    return pl.pallas_call(body, name=name, in_specs=[ANY], out_specs=ANY,
                          out_shape=jax.ShapeDtypeStruct((n, half, 128), g.dtype),
                          scratch_shapes=[pltpu.SemaphoreType.DMA, pltpu.SemaphoreType.DMA])(g)


def _scatter_steps(h_ref, out_ref, send_sems, recv_sems, first=True, last=True):
    x, y, c = _place()

    def copies():
        return [pltpu.make_async_remote_copy(src_ref=h_ref.at[2 * chip[0] + chip[1]], dst_ref=out_ref.at[j],
                                             send_sem=send_sems.at[j], recv_sem=recv_sems.at[j],
                                             device_id=(*chip, c), device_id_type=MESH)
                for j, chip in enumerate(_other_chips(x, y))]

    def start():
        for cp in copies():
            cp.start()

    def finish():
        for cp in copies():
            cp.wait()

    _when(first, start)
    _when(last, finish)


def _scatter_chips(hsum):
    n, half, _ = hsum.shape

    def body(h_ref, out_ref, send_sems, recv_sems):
        _scatter_steps(h_ref, out_ref, send_sems, recv_sems)

    return pl.pallas_call(body, name="scatter_chips", in_specs=[ANY], out_specs=ANY,
                          out_shape=jax.ShapeDtypeStruct((3, half, 128), hsum.dtype),
                          scratch_shapes=list(SCATTER_SEMS))(hsum)


def _join_halves(buf, name):
    half = buf.shape[0] // 2

    def body(buf_ref, out_ref, send_sem, recv_sem):
        del buf_ref
        x, y, c = _place()
        mine = out_ref.at[pl.ds(pl.multiple_of(c * half, 8), half), :]
        other = out_ref.at[pl.ds(pl.multiple_of((1 - c) * half, 8), half), :]
        cp = pltpu.make_async_remote_copy(src_ref=mine, dst_ref=mine, send_sem=send_sem, recv_sem=recv_sem,
                                          device_id=(x, y, 1 - c), device_id_type=MESH)
        cp.start()
        cp.wait_send()
        pltpu.make_async_remote_copy(src_ref=other, dst_ref=other, send_sem=send_sem, recv_sem=recv_sem,
                                     device_id=(x, y, 1 - c), device_id_type=MESH).wait_recv()

    return pl.pallas_call(body, name=name, in_specs=[ANY], out_specs=ANY,
                          out_shape=jax.ShapeDtypeStruct(buf.shape, buf.dtype), input_output_aliases={0: 0},
                          scratch_shapes=[pltpu.SemaphoreType.DMA, pltpu.SemaphoreType.DMA])(buf)


def _add_sibling(g, recv, c_idx, name):
    n, rows, _ = g.shape
    half = rows // 2
    tr = _tile(half, ADAM_ROWS, 16)
    nb = half // tr

    def body(c_ref, g_ref, r_ref, o_ref, ob_ref):
        s = g_ref[...] + r_ref[...]
        o_ref[...] = s
        ob_ref[...] = s.astype(CDT)

    out = pl.BlockSpec((None, tr, 128), lambda k, i, c: (k, i, 0))
    grid_spec = pltpu.PrefetchScalarGridSpec(
        num_scalar_prefetch=1, grid=(n, nb),
        in_specs=[pl.BlockSpec((None, tr, 128), lambda k, i, c: (k, c[0] * nb + i, 0)), out],
        out_specs=(out, out))
    return pl.pallas_call(body, name=name, grid_spec=grid_spec,
                          out_shape=(jax.ShapeDtypeStruct((n, half, 128), F32), jax.ShapeDtypeStruct((n, half, 128), CDT)),
                          compiler_params=_params(("parallel", "parallel")))(c_idx, g, recv)


def _add_chips(hsum, recv, chip_idx, name):
    n, half, _ = hsum.shape
    tr = _tile(half, ADAM_ROWS, 16)

    def body(k_ref, h_ref, r_ref, o_ref):
        o_ref[...] = ((h_ref[...] + r_ref[0].astype(F32)) + r_ref[1].astype(F32)) + r_ref[2].astype(F32)

    grid_spec = pltpu.PrefetchScalarGridSpec(
        num_scalar_prefetch=1, grid=(half // tr,),
        in_specs=[pl.BlockSpec((None, tr, 128), lambda i, k: (k[0], i, 0)),
                  pl.BlockSpec((3, tr, 128), lambda i, k: (0, i, 0))],
        out_specs=pl.BlockSpec((tr, 128), lambda i, k: (i, 0)))
    return pl.pallas_call(body, name=name, grid_spec=grid_spec, out_shape=jax.ShapeDtypeStruct((half, 128), F32),
                          compiler_params=_params(("parallel",)))(chip_idx, hsum, recv)


def _adamw_math(g, w, m, v):
    m2 = B1 * m + (1.0 - B1) * g
    v2 = B2 * v + (1.0 - B2) * (g * g)
    delta = -LR * ((m2 / BC1) / (jnp.sqrt(v2 / BC2) + AEPS) + WD * w)
    return delta, m2, v2


def _adamw(g, w, m, v, name):
    rows, cols = g.shape
    tr = _tile(rows, max(8, (ADAM_ROWS * 128 // cols) // 8 * 8), 8)

    def body(g_ref, w_ref, m_ref, v_ref, d_ref, m2_ref, v2_ref):
        d_ref[...], m2_ref[...], v2_ref[...] = _adamw_math(g_ref[...], w_ref[...], m_ref[...], v_ref[...])

    blk = pl.BlockSpec((tr, cols), lambda i: (i, 0))
    shp = jax.ShapeDtypeStruct((rows, cols), F32)
    return pl.pallas_call(body, name=name, grid=(rows // tr,), in_specs=[blk] * 4, out_specs=(blk,) * 3,
                          out_shape=(shp,) * 3, compiler_params=_params(("parallel",)))(g, w, m, v)


def _adamw_small(parts, w, m, v, name):
    rows = w.shape[0]

    def body(p_ref, w_ref, m_ref, v_ref, g_ref, d_ref, m2_ref, v2_ref):
        g = p_ref[0]
        for k in range(1, 8):
            g = g + p_ref[k]
        g_ref[...] = g
        d_ref[...], m2_ref[...], v2_ref[...] = _adamw_math(g, w_ref[...], m_ref[...], v_ref[...])

    shp = jax.ShapeDtypeStruct((rows, 128), F32)
    return pl.pallas_call(body, name=name, out_shape=(shp,) * 4)(parts, w, m, v)


def _pack_rows(parts, rows):
    flat = jnp.concatenate([p.reshape(-1) for p in parts])
    return jnp.pad(flat, (0, rows * 128 - flat.shape[0])).reshape(rows, 128)


def _unpack(flat, sizes, shapes):
    out, off = [], 0
    for n, s in zip(sizes, shapes):
        out.append(flat[off:off + n].reshape(s))
        off += n
    return out


def _to_shards(full, shard_shape, axis):
    if axis == 0:
        return full.reshape(N_CHIP, -1)
    r, cs = shard_shape
    return full.reshape(r, N_CHIP, cs).transpose(1, 0, 2).reshape(N_CHIP, -1)


def _from_shards(sh, shard_shape, axis):
    r, cs = shard_shape
    if axis == 0:
        return sh.reshape(N_CHIP * r, cs)
    return sh.reshape(N_CHIP, r, cs).transpose(1, 0, 2).reshape(r, N_CHIP * cs)


def _local_step(x0, mem, tgt, W, gains, ex=None):
    S = x0.shape[0]
    w_in = jnp.pad(W["w_in"], ((0, 0), (0, IN_PAD - IN_COLS)))
    b_f = jnp.pad(gains["b_forget"], ((0, 0), (0, 128 - NH)))

    h1 = _rms_cast(x0, gains["attn_norm_g"], "norm_attn")
    qkv = _mm_nn(h1, w_in[:, :NQKV], CDT, "proj_qkv", heads=True)
    fl = _mm_nn(h1, w_in[:, NQKV:NQKV + 128], F32, "proj_gate")
    cum = _gate_fwd(fl, b_f, "gate_cumsum")
    c_hm = cum[:, :NH].T
    cq, ck = c_hm[:, :, None], c_hm[:, None, :]
    fq, fk, fv, sq, sk, sv = (qkv[n * NH:(n + 1) * NH] for n in range(6))
    fo_h, lse, gathered = _fox_fwd(fq, fk, fv, cq, ck, "fox_fwd", slots=None if ex is None else ex.slots("b"))
    if ex is not None:
        W = {**W, **ex.unpack("b", gathered)}
    cw = W["conv_w"].reshape(3, 2, DFF).transpose(1, 0, 2)
    cb = gains["conv_b"].reshape(2, 1, DFF)
    sq2, sk2 = _log2_operands(sq, sk)
    so_h, s_lt = _sb_fwd(sq2, sk2, sv, "sb_fwd")
    x1, mixed = _out_proj(fo_h, so_h, gains["fox_out_g"], gains["sb_out_g"], W["w_out"], x0, "out_proj")

    h2 = _rms_cast(x1, gains["xattn_norm_g"], "norm_xattn")
    mn = _rms_cast(mem, gains["mem_norm_g"], "norm_mem")
    mq = _mm_nn(h2, W["w_mq"], CDT, "proj_mq")
    kv = _mm_nn(mn, W["w_mkv"], CDT, "proj_mkv")
    x2, mo = _xattn_fwd(mq, kv, W["w_mo"], x1, "xattn_fwd")

    h3 = _rms_cast(x2, gains["ffn_norm_g"], "norm_ffn")
    u0 = _mm_nn(h3, W["w_up"], CDT, "ffn_up", tm=512, tn=DFF, halves=True)
    act = _conv_act(u0, cw, cb, "conv_act")
    x3 = _mm_nn(act, W["w_down"], F32, "ffn_down", tm=512, residual=x2)
    loss, dx3, dg_final = _loss_bwd(x3, tgt, gains["final_norm_g"].reshape(1, D), "loss")

    gw, gs = {}, {"final_norm_g": dg_final}
    da = _mm_nt(dx3, W["w_down"], "ffn_down_dx", tn=DFF, out_dtype=CDT)
    gw["w_down"] = _mm_tn(act, dx3, "ffn_down_dw", tka=DFF)
    du, dwb = _conv_act_bwd(u0, da, cw, cb, "conv_act_bwd")
    gw["conv_w"] = dwb[:, :3].transpose(1, 0, 2).reshape(3, 2 * DFF)
    gs["conv_b"] = dwb[:, 3].reshape(1, 2 * DFF)
    du0 = _conv_bwd_input(du, cw, "conv_bwd_input")
    gw["w_up"] = _mm_tn(h3, du0, "ffn_up_dw", tn=DFF, b_halves=True)
    dx2, gs["ffn_norm_g"] = _mm_nt_rmsbwd(du0, W["w_up"], x2, gains["ffn_norm_g"], dx3, "ffn_up_dx", tk=DFF, a_halves=True)

    dmo = _mm_nt(dx2, W["w_mo"], "mo_dx", tn=512, out_dtype=CDT)
    gw["w_mo"] = _mm_tn(mo, dx2, "mo_dw")
    dmq, dkv = _xattn_bwd(mq, kv, dmo, "xattn_bwd")
    gw["w_mq"] = _mm_tn(h2, dmq, "mq_dw")
    dx1, gs["xattn_norm_g"] = _mm_nt_rmsbwd(dmq, W["w_mq"], x1, gains["xattn_norm_g"], dx2, "mq_dx")
    gw["w_mkv"] = _mm_tn(mn, dkv, "mkv_dw")
    _, gs["mem_norm_g"] = _mm_nt_rmsbwd(dkv, W["w_mkv"], mem, gains["mem_norm_g"], jnp.zeros_like(mem), "mkv_dx")

    gw["w_out"] = _mm_tn(mixed, dx1, "out_dw")
    dfo_h, dso_h, gs["fox_out_g"], gs["sb_out_g"] = _out_proj_bwd(dx1, W["w_out"], fo_h, so_h, gains["fox_out_g"], gains["sb_out_g"], "out_dx")
    dfq, dfk, dfv, dck, dcq = _fox_bwd(fq, fk, fv, cq, ck, fo_h, dfo_h, lse, "fox_bwd")
    pair, pair16 = (None, None) if ex is None else ex.pair_sums("b", gw)
    dsq, dsk, dsv, arrived = _sb_bwd(sq, sq2, sk2, sv, dso_h, s_lt, "sb_bwd", scatter=pair16)
    dc = jnp.pad((dck[:, 0, :] + dcq[:, :, 0]).T, ((0, 0), (0, 128 - NH)))
    dfl, db = _gate_bwd(dc, fl, b_f, "gate_bwd")
    gs["b_forget"] = db[:, :NH]
    dqkv = jnp.concatenate([dfq, dfk.astype(CDT), dfv.astype(CDT), dsq, dsk.astype(CDT), dsv.astype(CDT)], axis=0)
    dproj = jnp.concatenate([dqkv.transpose(1, 0, 2).reshape(S, NQKV), dfl.astype(CDT),
                             jnp.zeros((S, IN_PAD - NQKV - 128), CDT)], axis=1)
    gw["w_in"] = _mm_tn(h1, dproj, "in_dw", tn=IN_PAD)[:, :IN_COLS]
    dx0, gs["attn_norm_g"] = _mm_nt_rmsbwd(dproj, w_in, x0, gains["attn_norm_g"], dx1, "in_dx", tk=IN_PAD)
    return loss, dx0, gw, gs, (pair, arrived)


NAMES = ("attn_norm_g", "w_in", "b_forget", "fox_out_g", "sb_out_g", "w_out", "xattn_norm_g", "mem_norm_g", "w_mq",
         "w_mkv", "w_mo", "ffn_norm_g", "w_up", "conv_w", "conv_b", "w_down", "final_norm_g")


class _Exchange:
    def __init__(self, w):
        self.w = w
        xi, yi, ci = _place()
        self.core = ci
        self.chip = 2 * xi + yi
        self.core_idx = jnp.reshape(ci, (1,)).astype(jnp.int32)
        self.chip_idx = jnp.reshape(self.chip, (1,)).astype(jnp.int32)

    def slots(self, g):
        parts = []
        for name, shape, _ in GROUPS[g]:
            blk = self.w[name].reshape(shape)
            parts.append(lax.bitcast_convert_type(blk, CDT) if name == "conv_w" else blk.astype(CDT))
        rows = _rows_g(GROUPS[g])
        return lax.dynamic_update_slice(lax.empty((N_CHIP, rows, 128), CDT), _pack_rows(parts, rows)[None], (self.chip, 0, 0))

    def unpack(self, g, gathered):
        flat, full, off = gathered.reshape(N_CHIP, -1), {}, 0
        for (name, shape, axis), n in zip(GROUPS[g], _gather_sizes(GROUPS[g])):
            sh = flat[:, off:off + n]
            off += n
            if name == "conv_w":
                sh = lax.bitcast_convert_type(sh.reshape(N_CHIP, n // 2, 2), F32)
            full[name] = _from_shards(sh, shape, axis)
        return full

    def pair_sums(self, g, gw):
        rows = _rows_f(GROUPS[g])
        flat = jnp.concatenate([_to_shards(gw[name], shape, axis) for name, shape, axis in GROUPS[g]], axis=1)
        flat = jnp.pad(flat, ((0, 0), (0, rows * 128 - flat.shape[1]))).reshape(N_CHIP, rows, 128)
        return _add_sibling(flat, _swap_halves(flat, "swap_halves_" + g), self.core_idx, "add_sibling_" + g)

    def finish(self, g, pair, arrived):
        rows = _rows_f(GROUPS[g])
        mine = _add_chips(pair, arrived, self.chip_idx, "add_chips_" + g)
        whole = _join_halves(lax.dynamic_update_slice(lax.empty((rows, 128), F32), mine, (self.core * (rows // 2), 0)), "join_halves_" + g)
        shapes = [s for _, s, _ in GROUPS[g]]
        return {name: arr for (name, _, _), arr in zip(GROUPS[g], _unpack(whole.reshape(-1), _sizes(GROUPS[g]), shapes))}


def _step(x, mem, loss_target, w, m, v):
    ex = _Exchange(w)

    W = ex.unpack("a", _gather_weights(ex.slots("a")))
    gains = {name: w[name].reshape(1, -1) for name, _ in SMALL}

    loss, grad_x, gw, gs, (pair_b, arrived_b) = _local_step(x[0], mem[0], loss_target[0], W, gains, ex)

    grads = ex.finish("b", pair_b, arrived_b)
    pair_a, pair16_a = ex.pair_sums("a", gw)
    grads.update(ex.finish("a", pair_a, _scatter_chips(pair16_a)))
    small = jnp.concatenate([gs[name].reshape(-1) for name, _ in SMALL] + [loss[0, :1]])
    small = jnp.pad(small, (0, ROWS_S * 128 - P_SMALL)).reshape(ROWS_S, 128)
    small_parts = _gather_small(small).reshape(8, ROWS_S, 128)

    def flat_small(d):
        return _pack_rows([d[name] for name, _ in SMALL], ROWS_S)

    outs = {}
    for name, shape, _ in BIG:
        g = grads[name]
        res = _adamw(g, w[name].reshape(shape), m[name].reshape(shape), v[name].reshape(shape), "adamw_" + name)
        for prefix, arr in zip(("grad_", "delta_", "new_m_", "new_v_"), (g, *res)):
            outs[prefix + name] = arr.reshape(w[name].shape)
    small_res = _adamw_small(small_parts, flat_small(w), flat_small(m), flat_small(v), "adamw_small")
    g_sm = small_res[0]
    for prefix, sm in zip(("grad_", "delta_", "new_m_", "new_v_"), small_res):
        for (name, n), arr in zip(SMALL, _unpack(sm.reshape(-1), [n for _, n in SMALL], [(n,) for _, n in SMALL])):
            outs[prefix + name] = arr.reshape(w[name].shape)
    total_loss = g_sm.reshape(-1)[P_SMALL - 1]
    return (total_loss, grad_x[None], *[outs[p + n] for p in ("grad_", "delta_", "new_m_", "new_v_") for n in NAMES])


def kernel(x, mem, attn_norm_g, w_in, b_forget, fox_out_g, sb_out_g, w_out, xattn_norm_g, mem_norm_g, w_mq, w_mkv, w_mo, ffn_norm_g, w_up, conv_w, conv_b, w_down, final_norm_g, loss_target, m_attn_norm_g, m_w_in, m_b_forget, m_fox_out_g, m_sb_out_g, m_w_out, m_xattn_norm_g, m_mem_norm_g, m_w_mq, m_w_mkv, m_w_mo, m_ffn_norm_g, m_w_up, m_conv_w, m_conv_b, m_w_down, m_final_norm_g, v_attn_norm_g, v_w_in, v_b_forget, v_fox_out_g, v_sb_out_g, v_w_out, v_xattn_norm_g, v_mem_norm_g, v_w_mq, v_w_mkv, v_w_mo, v_ffn_norm_g, v_w_up, v_conv_w, v_conv_b, v_w_down, v_final_norm_g):
    given = dict(locals())
    w = {n: given[n] for n in NAMES}
    m = {n: given["m_" + n] for n in NAMES}
    v = {n: given["v_" + n] for n in NAMES}
    return _step(x, mem, loss_target, w, m, v)
```

```python
import functools

import numpy as np
import jax
import jax.numpy as jnp
from jax import lax
from jax.experimental import pallas as pl
from jax.experimental.pallas import tpu as pltpu

F32 = jnp.float32
CDT = jnp.bfloat16
MESH = pl.DeviceIdType.MESH

D = 1024
HD = 64
NH = 8
GW = NH * HD
NQKV = 6 * GW
IN_COLS = NQKV + NH
IN_PAD = NQKV + 256
NMH = 4
MHD = D // NMH
DFF = 2816
EPS = 1e-6
ATT_SCALE = HD ** -0.5
MEM_SCALE = MHD ** -0.5
NEG = -1e30

LR, B1, B2, AEPS, WD, STEP = 0.001, 0.9, 0.999, 1e-08, 0.01, 10
BC1 = 1.0 - B1 ** STEP
BC2 = 1.0 - B2 ** STEP

ATT_TILES = {"fox_fwd": (512, 1024), "fox_bwd": (512, 1024), "sb_fwd": (512, 1024), "sb_bwd": (1024, 1024)}
W_SB = 256
VMEM_LIMIT = 52 * 2 ** 20

N_CHIP = 4
BIG = (("w_in", (D, IN_COLS // N_CHIP), 1), ("w_out", (D // N_CHIP, D), 0), ("w_mq", (D // N_CHIP, D), 0),
       ("w_mkv", (D, 2 * D // N_CHIP), 1), ("w_mo", (D // N_CHIP, D), 0), ("w_up", (D, 2 * DFF // N_CHIP), 1),
       ("conv_w", (3, 2 * DFF // N_CHIP), 1), ("w_down", (DFF // N_CHIP, D), 0))
GROUPS = {"a": BIG[:1], "b": BIG[1:]}
ADAM_ROWS = 1536


def _sizes(group):
    return tuple(int(np.prod(s)) for _, s, _ in group)


def _gather_sizes(group):
    return tuple(2 * n if name == "conv_w" else n for (name, _, _), n in zip(group, _sizes(group)))


def _rows_g(group):
    return -(-sum(_gather_sizes(group)) // 4096) * 32


def _rows_f(group):
    return -(-sum(_sizes(group)) // 65536) * 512
SMALL = (("attn_norm_g", 1024), ("b_forget", 8), ("fox_out_g", 512), ("sb_out_g", 512), ("xattn_norm_g", 1024),
         ("mem_norm_g", 1024), ("ffn_norm_g", 1024), ("conv_b", 2 * DFF), ("final_norm_g", 1024))
P_SMALL = sum(n for _, n in SMALL) + 1
ROWS_S = -(-P_SMALL // 1024) * 8


def _params(sem=None, vmem=VMEM_LIMIT):
    return pltpu.CompilerParams(dimension_semantics=sem, vmem_limit_bytes=vmem)


def _tile(n, pref, mult):
    t = (min(pref, n) // mult) * mult
    while t >= mult:
        if n % t == 0:
            return t
        t -= mult
    return n


def _dot(a, b):
    return jnp.dot(a, b, preferred_element_type=F32)


def _dot_nt(a, b):
    return lax.dot_general(a, b, (((1,), (1,)), ((), ())), preferred_element_type=F32)


def _dot_tn(a, b):
    return lax.dot_general(a, b, (((0,), (0,)), ((), ())), preferred_element_type=F32)


def _split3(x):
    h1 = x.astype(CDT)
    r1 = x - h1.astype(F32)
    h2 = r1.astype(CDT)
    h3 = (r1 - h2.astype(F32)).astype(CDT)
    return h1, h2, h3


def _split2(x):
    h1 = x.astype(CDT)
    return h1, (x - h1.astype(F32)).astype(CDT)


def _rms_bwd(dh, x, g):
    r = lax.rsqrt(jnp.mean(x * x, axis=-1, keepdims=True) + EPS)
    xn = x * r
    dg = jnp.sum(dh * xn, axis=0, keepdims=True)
    dhg = dh * g
    dx = r * (dhg - xn * jnp.mean(dhg * xn, axis=-1, keepdims=True))
    return dx, dg


def _mm_nn(a, b, out_dtype, name, *, tm=1024, tn=512, residual=None, halves=False):
    M, K = a.shape
    N = b.shape[1]
    tm = _tile(M, tm, 16)
    tn = _tile(N // 2 if halves else N, tn, 128)
    nj = N // tn

    def body(*refs):
        a_ref, b_ref = refs[0], refs[1]
        o_ref = refs[-1]
        acc = _dot(a_ref[...].astype(CDT), b_ref[...].astype(CDT))
        if residual is not None:
            acc = acc + refs[2][...]
        o_ref[...] = acc.astype(o_ref.dtype)

    in_specs = [pl.BlockSpec((tm, K), lambda i, j: (i, 0)), pl.BlockSpec((K, tn), lambda i, j: (0, j))]
    ops = [a, b]
    if residual is not None:
        in_specs.append(pl.BlockSpec((tm, tn), lambda i, j: (i, j)))
        ops.append(residual)
    if halves:
        njh = nj // 2
        out_shape = jax.ShapeDtypeStruct((2, M, N // 2), out_dtype)
        out_spec = pl.BlockSpec((None, tm, tn), lambda i, j: (j // njh, i, j % njh))
    else:
        out_shape = jax.ShapeDtypeStruct((M, N), out_dtype)
        out_spec = pl.BlockSpec((tm, tn), lambda i, j: (i, j))
    return pl.pallas_call(body, name=name, grid=(M // tm, nj), in_specs=in_specs, out_specs=out_spec,
                          out_shape=out_shape, compiler_params=_params(("parallel", "parallel")))(*ops)


def _mm_tn(a, b, name, *, tka=512, tn=1024, ts=512, b_halves=False):
    S, Ka = a.shape
    N = 2 * b.shape[2] if b_halves else b.shape[1]
    tka = _tile(Ka, tka, 128)
    tn = _tile(N // 2 if b_halves else N, tn, 128)
    ts = _tile(S, ts, 16)
    nn = N // tn

    def body(a_ref, b_ref, o_ref):
        @pl.when(pl.program_id(2) == 0)
        def _():
            o_ref[...] = jnp.zeros_like(o_ref)
        o_ref[...] += _dot_tn(a_ref[...].astype(CDT), b_ref[...].astype(CDT))

    if b_halves:
        nnh = nn // 2
        b_spec = pl.BlockSpec((None, ts, tn), lambda i, j, s: (j // nnh, s, j % nnh))
    else:
        b_spec = pl.BlockSpec((ts, tn), lambda i, j, s: (s, j))
    return pl.pallas_call(
        body, name=name, grid=(Ka // tka, nn, S // ts),
        in_specs=[pl.BlockSpec((ts, tka), lambda i, j, s: (s, i)), b_spec],
        out_specs=pl.BlockSpec((tka, tn), lambda i, j, s: (i, j)),
        out_shape=jax.ShapeDtypeStruct((Ka, N), F32),
        compiler_params=_params(("parallel", "parallel", "arbitrary")))(a, b)


def _mm_nt(a, b, name, *, tm=512, tn=None, tk=None, a_halves=False, out_dtype=F32,
           epilogue=None, extra=(), extra_specs=(), out_shape=None, out_specs=None):
    if a_halves:
        M, K = a.shape[1], 2 * a.shape[2]
    else:
        M, K = a.shape
    N = b.shape[0]
    tm = _tile(M, tm, 16)
    tn = N if (epilogue is not None or tn is None) else _tile(N, tn, 128)
    tk = K if tk is None else _tile(K // 2 if a_halves else K, tk, 128)
    nk = K // tk
    n_extra = len(extra)

    def body(*refs):
        a_ref, b_ref = refs[0], refs[1]
        extra_refs = refs[2:2 + n_extra]
        out_refs = refs[2 + n_extra:-1]
        acc_ref = refs[-1]
        k = pl.program_id(2)

        @pl.when(k == 0)
        def _():
            acc_ref[...] = jnp.zeros_like(acc_ref)
        acc_ref[...] += _dot_nt(a_ref[...].astype(CDT), b_ref[...].astype(CDT))

        @pl.when(k == nk - 1)
        def _():
            if epilogue is None:
                out_refs[0][...] = acc_ref[...].astype(out_refs[0].dtype)
            else:
                epilogue(acc_ref[...], pl.program_id(0), extra_refs, out_refs)

    if a_halves:
        nkh = nk // 2
        a_spec = pl.BlockSpec((None, tm, tk), lambda i, j, k: (k // nkh, i, k % nkh))
    else:
        a_spec = pl.BlockSpec((tm, tk), lambda i, j, k: (i, k))
    if epilogue is None:
        out_shape = jax.ShapeDtypeStruct((M, N), out_dtype)
        out_specs = pl.BlockSpec((tm, tn), lambda i, j, k: (i, j))
        sem = ("parallel", "parallel", "arbitrary")
    else:
        sem = ("arbitrary", "arbitrary", "arbitrary")
    return pl.pallas_call(
        body, name=name, grid=(M // tm, N // tn, nk),
        in_specs=[a_spec, pl.BlockSpec((tn, tk), lambda i, j, k: (j, k)), *extra_specs],
        out_specs=out_specs, out_shape=out_shape,
        scratch_shapes=[pltpu.VMEM((tm, tn), F32)],
        compiler_params=_params(sem))(a, b, *extra)


def _mm_nt_rmsbwd(a, b, x, g, dres, name, *, tm=512, tk=None, a_halves=False):
    M = x.shape[0]
    tm = _tile(M, tm, 16)

    def epilogue(acc, i, extra_refs, out_refs):
        x_ref, g_ref, r_ref = extra_refs
        dx_ref, dg_ref = out_refs
        dx, dg = _rms_bwd(acc, x_ref[...], g_ref[...])
        dx_ref[...] = r_ref[...] + dx

        @pl.when(i == 0)
        def _():
            dg_ref[...] = jnp.zeros_like(dg_ref)
        dg_ref[...] += dg

    row = pl.BlockSpec((tm, D), lambda i, j, k: (i, 0))
    vec = pl.BlockSpec((1, D), lambda i, j, k: (0, 0))
    return _mm_nt(a, b, name, tm=tm, tk=tk, a_halves=a_halves, epilogue=epilogue,
                  extra=(x, g, dres), extra_specs=(row, vec, row),
                  out_shape=(jax.ShapeDtypeStruct((M, D), F32), jax.ShapeDtypeStruct((1, D), F32)),
                  out_specs=(row, vec))


def _rms_cast(x, g, name, *, tm=512):
    M, W = x.shape
    tm = _tile(M, tm, 16)

    def body(x_ref, g_ref, o_ref):
        xf = x_ref[...]
        r = lax.rsqrt(jnp.mean(xf * xf, axis=-1, keepdims=True) + EPS)
        o_ref[...] = (xf * r * g_ref[...]).astype(o_ref.dtype)

    return pl.pallas_call(body, name=name, grid=(M // tm,),
                          in_specs=[pl.BlockSpec((tm, W), lambda i: (i, 0)), pl.BlockSpec((1, W), lambda i: (0, 0))],
                          out_specs=pl.BlockSpec((tm, W), lambda i: (i, 0)),
                          out_shape=jax.ShapeDtypeStruct((M, W), CDT),
                          compiler_params=_params(("parallel",)))(x, g)


def _tri(n, lower):
    r = lax.broadcasted_iota(jnp.int32, (n, n), 0)
    c = lax.broadcasted_iota(jnp.int32, (n, n), 1)
    return (c <= r if lower else c >= r).astype(CDT)


def _gate_fwd(fl, b, name, *, tm=512):
    S = fl.shape[0]
    tm = _tile(S, tm, 16)

    def body(f_ref, b_ref, c_ref, carry):
        @pl.when(pl.program_id(0) == 0)
        def _():
            carry[...] = jnp.zeros_like(carry)
        z = f_ref[...] + b_ref[...]
        lf = jnp.minimum(z, 0.0) - jnp.log(1.0 + jnp.exp(-jnp.abs(z)))
        tri = _tri(tm, True)
        cum = sum(_dot(tri, p) for p in _split3(lf)) + carry[...]
        c_ref[...] = cum
        carry[...] = cum[tm - 1:tm, :]

    return pl.pallas_call(body, name=name, grid=(S // tm,),
                          in_specs=[pl.BlockSpec((tm, 128), lambda i: (i, 0)), pl.BlockSpec((1, 128), lambda i: (0, 0))],
                          out_specs=pl.BlockSpec((tm, 128), lambda i: (i, 0)),
                          out_shape=jax.ShapeDtypeStruct((S, 128), F32),
                          scratch_shapes=[pltpu.VMEM((1, 128), F32)],
                          compiler_params=_params(("arbitrary",)))(fl, b)


def _gate_bwd(dc, fl, b, name, *, tm=512):
    S = fl.shape[0]
    tm = _tile(S, tm, 16)
    nb = S // tm

    def body(dc_ref, f_ref, b_ref, df_ref, db_ref, carry):
        @pl.when(pl.program_id(0) == 0)
        def _():
            carry[...] = jnp.zeros_like(carry)
            db_ref[...] = jnp.zeros_like(db_ref)
        tri = _tri(tm, False)
        suf = sum(_dot(tri, p) for p in _split3(dc_ref[...])) + carry[...]
        carry[...] = suf[0:1, :]
        df = suf * jax.nn.sigmoid(-(f_ref[...] + b_ref[...]))
        df_ref[...] = df
        db_ref[...] += jnp.sum(df, axis=0, keepdims=True)

    rev = pl.BlockSpec((tm, 128), lambda i: (nb - 1 - i, 0))
    vec = pl.BlockSpec((1, 128), lambda i: (0, 0))
    return pl.pallas_call(body, name=name, grid=(nb,), in_specs=[rev, rev, vec], out_specs=(rev, vec),
                          out_shape=(jax.ShapeDtypeStruct((S, 128), F32), jax.ShapeDtypeStruct((1, 128), F32)),
                          scratch_shapes=[pltpu.VMEM((1, 128), F32)],
                          compiler_params=_params(("arbitrary",)))(dc, fl, b)


def _group_rows(o_ref):
    return jnp.concatenate([o_ref[h] for h in range(NH)], axis=1)


def _out_proj(fo, so, gf, gs, w_out, x0, name, *, tm=512):
    S = fo.shape[1]
    tm = _tile(S, tm, 16)

    def body(fo_ref, so_ref, gf_ref, gs_ref, w_ref, x_ref, x1_ref, mx_ref):
        for ref, g_ref, lo in ((fo_ref, gf_ref, 0), (so_ref, gs_ref, GW)):
            o = _group_rows(ref)
            r = lax.rsqrt(jnp.mean(o * o, axis=-1, keepdims=True) + EPS)
            mx_ref[:, lo:lo + GW] = (o * r * g_ref[...]).astype(CDT)
        x1_ref[...] = x_ref[...] + _dot(mx_ref[...], w_ref[...])

    half = pl.BlockSpec((NH, tm, HD), lambda i: (0, i, 0))
    gvec = pl.BlockSpec((1, GW), lambda i: (0, 0))
    row = pl.BlockSpec((tm, D), lambda i: (i, 0))
    return pl.pallas_call(body, name=name, grid=(S // tm,),
                          in_specs=[half, half, gvec, gvec, pl.BlockSpec((D, D), lambda i: (0, 0)), row],
                          out_specs=(row, row),
                          out_shape=(jax.ShapeDtypeStruct((S, D), F32), jax.ShapeDtypeStruct((S, D), CDT)),
                          compiler_params=_params(("parallel",)))(fo, so, gf, gs, w_out, x0)


def _out_proj_bwd(dx1, w_out, fo, so, gf, gs, name, *, tm=512):
    S = fo.shape[1]
    tm = _tile(S, tm, 16)

    def epilogue(acc, i, extra_refs, out_refs):
        fo_ref, so_ref, gf_ref, gs_ref = extra_refs
        dfo_ref, dso_ref, dgf_ref, dgs_ref = out_refs

        @pl.when(i == 0)
        def _():
            dgf_ref[...] = jnp.zeros_like(dgf_ref)
            dgs_ref[...] = jnp.zeros_like(dgs_ref)
        for lo, o_ref, g_ref, do_ref, dg_ref in ((0, fo_ref, gf_ref, dfo_ref, dgf_ref), (GW, so_ref, gs_ref, dso_ref, dgs_ref)):
            dx, dg = _rms_bwd(acc[:, lo:lo + GW], _group_rows(o_ref), g_ref[...])
            for h in range(NH):
                do_ref[h] = dx[:, h * HD:(h + 1) * HD].astype(do_ref.dtype)
            dg_ref[...] += dg

    half = pl.BlockSpec((NH, tm, HD), lambda i, j, k: (0, i, 0))
    gvec = pl.BlockSpec((1, GW), lambda i, j, k: (0, 0))
    return _mm_nt(dx1, w_out, name, tm=tm, epilogue=epilogue, extra=(fo, so, gf, gs),
                  extra_specs=(half, half, gvec, gvec),
                  out_shape=(jax.ShapeDtypeStruct((NH, S, HD), CDT), jax.ShapeDtypeStruct((NH, S, HD), CDT),
                             jax.ShapeDtypeStruct((1, GW), F32), jax.ShapeDtypeStruct((1, GW), F32)),
                  out_specs=(half, half, gvec, gvec))


def _loss_bwd(x3, tgt, g, name, *, tm=512):
    S = x3.shape[0]
    tm = _tile(S, tm, 16)

    def body(x_ref, t_ref, g_ref, dx_ref, loss_ref, dg_ref):
        @pl.when(pl.program_id(0) == 0)
        def _():
            loss_ref[...] = jnp.zeros_like(loss_ref)
            dg_ref[...] = jnp.zeros_like(dg_ref)
        x = x_ref[...]
        gv = g_ref[...]
        r = lax.rsqrt(jnp.mean(x * x, axis=-1, keepdims=True) + EPS)
        xn = x * r
        err = xn * gv - t_ref[...]
        loss_ref[...] += jnp.full(loss_ref.shape, 0.5 * jnp.sum(jnp.mean(err * err, axis=-1, keepdims=True)), F32)
        dy = err * (1.0 / D)
        dg_ref[...] += jnp.sum(dy * xn, axis=0, keepdims=True)
        dyg = dy * gv
        dx_ref[...] = r * (dyg - xn * jnp.mean(dyg * xn, axis=-1, keepdims=True))

    row = pl.BlockSpec((tm, D), lambda i: (i, 0))
    vec = pl.BlockSpec((1, D), lambda i: (0, 0))
    dx3, loss, dg = pl.pallas_call(
        body, name=name, grid=(S // tm,), in_specs=[row, row, vec],
        out_specs=(row, pl.BlockSpec((1, 128), lambda i: (0, 0)), vec),
        out_shape=(jax.ShapeDtypeStruct((S, D), F32), jax.ShapeDtypeStruct((1, 128), F32), jax.ShapeDtypeStruct((1, D), F32)),
        compiler_params=_params(("arbitrary",)))(x3, tgt, g)
    return loss, dx3, dg


MASKED, FIRST, LAST = 1, 2, 4


def _att_tiles(name, S):
    tq, tk = ATT_TILES[name]
    return min(tq, S), min(tk, S)


def _pairs(S, tq, tk, descending=True):
    assert tk % tq == 0 and S % tk == 0
    qi, kj, fl = [], [], []
    for i in range(S // tq):
        last = ((i + 1) * tq - 1) // tk
        order = list(range(last, -1, -1) if descending else range(last + 1))
        for pos, kb in enumerate(order):
            qi.append(i)
            kj.append(kb)
            fl.append((MASKED if (kb + 1) * tk - 1 > i * tq else 0) | (FIRST if pos == 0 else 0) | (LAST if pos == last else 0))
    return tuple(jnp.asarray(np.asarray(a, np.int32)) for a in (qi, kj, fl))


def _head_blk(rows, by_key, head0, width=HD):
    if by_key:
        return pl.BlockSpec((1, rows, width), lambda h, n, qi, kj, fl: (h + head0, kj[n], 0))
    return pl.BlockSpec((1, rows, width), lambda h, n, qi, kj, fl: (h + head0, qi[n], 0))


def _att_specs(tq, tk, width=HD):
    qblk = pl.BlockSpec((1, tq, width), lambda h, n, qi, kj, fl: (h, qi[n], 0))
    kblk = pl.BlockSpec((1, tk, width), lambda h, n, qi, kj, fl: (h, kj[n], 0))
    qcol = pl.BlockSpec((1, tq, 1), lambda h, n, qi, kj, fl: (h, qi[n], 0))
    krow = pl.BlockSpec((1, 1, tk), lambda h, n, qi, kj, fl: (h, 0, kj[n]))
    return qblk, kblk, qcol, krow


def _causal(tq, w, ahead, strict):
    diff = lax.broadcasted_iota(jnp.int32, (tq, w), 1) - lax.broadcasted_iota(jnp.int32, (tq, w), 0)
    return diff < ahead if strict else diff <= ahead


def _masked_or_not(flags, step):
    pl.when(flags % 2 == 1)(functools.partial(step, True))
    pl.when(flags % 2 == 0)(functools.partial(step, False))


def _fox_fwd(q, k, v, cq, ck, name, slots=None, h0=(0, 0, 0)):
    S = q.shape[1]
    tq, tk = _att_tiles("fox_fwd", S)
    qi, kj, fl = _pairs(S, tq, tk)
    qblk, kblk, qcol, krow = _att_specs(tq, tk)
    npairs = int(qi.shape[0])

    def body(qi_ref, kj_ref, fl_ref, q_ref, k_ref, v_ref, cq_ref, ck_ref, *rest):
        if slots is None:
            o_ref, lse_ref, m_s, l_s, acc_s = rest
        else:
            _, o_ref, lse_ref, slots_ref, m_s, l_s, acc_s, send_sems, recv_sems = rest
        h, n = pl.program_id(0), pl.program_id(1)
        i, kb, flags = qi_ref[n], kj_ref[n], fl_ref[n]
        if slots is not None:
            _gather_steps(slots_ref, send_sems, recv_sems, first=(h == 0) & (n == 0), middle=(h == NH // 2) & (n == 0),
                          last=(h == NH - 1) & (n == npairs - 1))

        @pl.when(flags & FIRST != 0)
        def _():
            m_s[...] = jnp.full_like(m_s, NEG)
            l_s[...] = jnp.zeros_like(l_s)
            acc_s[...] = jnp.zeros_like(acc_s)

        def step(masked):
            s = _dot_nt(q_ref[0] * ATT_SCALE, k_ref[0]) + cq_ref[0] - ck_ref[0]
            if masked:
                s = jnp.where(_causal(tq, tk, i * tq - kb * tk, False), s, NEG)
            m_new = jnp.maximum(m_s[...], jnp.max(s, axis=-1, keepdims=True))
            alpha = jnp.exp(m_s[...] - m_new)
            p = jnp.exp(s - m_new)
            l_s[...] = alpha * l_s[...] + jnp.sum(p, axis=-1, keepdims=True)
            acc_s[...] = alpha * acc_s[...] + _dot(p.astype(CDT), v_ref[0])
            m_s[...] = m_new

        _masked_or_not(flags, step)

        @pl.when(flags & LAST != 0)
        def _():
            o_ref[0] = acc_s[...] / l_s[...]
            lse_ref[0] = m_s[...] + jnp.log(l_s[...])

    scratch = [pltpu.VMEM((tq, 1), F32), pltpu.VMEM((tq, 1), F32), pltpu.VMEM((tq, HD), F32)]
    out_shape = (jax.ShapeDtypeStruct((NH, S, HD), F32), jax.ShapeDtypeStruct((NH, S, 1), F32))
    qkv_specs = (_head_blk(tq, False, h0[0]), _head_blk(tk, True, h0[1]), _head_blk(tk, True, h0[2]))
    if slots is None:
        grid_spec = pltpu.PrefetchScalarGridSpec(num_scalar_prefetch=3, grid=(NH, npairs), in_specs=[*qkv_specs, qcol, krow],
                                                 out_specs=(qblk, qcol), scratch_shapes=scratch)
        o, lse = pl.pallas_call(body, name=name, grid_spec=grid_spec, out_shape=out_shape,
                                compiler_params=_params(("parallel", "arbitrary")))(qi, kj, fl, q, k, v, cq, ck)
        return o, lse, None
    grid_spec = pltpu.PrefetchScalarGridSpec(num_scalar_prefetch=3, grid=(NH, npairs), in_specs=[*qkv_specs, qcol, krow, ANY],
                                             out_specs=(qblk, qcol, ANY), scratch_shapes=scratch + list(GATHER_SEMS))
    return pl.pallas_call(body, name=name, grid_spec=grid_spec, out_shape=(*out_shape, jax.ShapeDtypeStruct(slots.shape, slots.dtype)),
                          input_output_aliases={8: 2},
                          compiler_params=_params(("arbitrary", "arbitrary")))(qi, kj, fl, q, k, v, cq, ck, slots)


def _fox_bwd(q, k, v, cq, ck, o, do, lse, name, h0=(0, 0, 0)):
    S = q.shape[1]
    tq, tk = _att_tiles("fox_bwd", S)
    qi, kj, fl = _pairs(S, tq, tk)
    qblk, kblk, qcol, krow = _att_specs(tq, tk)

    def body(qi_ref, kj_ref, fl_ref, q_ref, k_ref, v_ref, cq_ref, ck_ref, o_ref, do_ref, lse_ref,
             dq_ref, dk_ref, dv_ref, dck_ref, dcq_ref, dq_s, dl_s, dcq_s):
        n = pl.program_id(1)
        i, kb, flags = qi_ref[n], kj_ref[n], fl_ref[n]

        @pl.when(n == 0)
        def _():
            dk_ref[...] = jnp.zeros_like(dk_ref)
            dv_ref[...] = jnp.zeros_like(dv_ref)
            dck_ref[...] = jnp.zeros_like(dck_ref)

        @pl.when(flags & FIRST != 0)
        def _():
            dq_s[...] = jnp.zeros_like(dq_s)
            dcq_s[...] = jnp.zeros_like(dcq_s)
            dl_s[...] = jnp.sum(do_ref[0].astype(F32) * o_ref[0], axis=-1, keepdims=True)

        def step(masked):
            qs = q_ref[0] * ATT_SCALE
            do = do_ref[0]
            p = jnp.exp(_dot_nt(qs, k_ref[0]) + cq_ref[0] - ck_ref[0] - lse_ref[0])
            if masked:
                p = jnp.where(_causal(tq, tk, i * tq - kb * tk, False), p, 0.0)
            ds = p * (_dot_nt(do, v_ref[0]) - dl_s[...])
            dsb = ds.astype(CDT)
            dq_s[...] += _dot(dsb, k_ref[0])
            rows = pl.ds(pl.multiple_of(kb * tk, tk), tk)
            dk_ref[0, rows, :] += _dot_tn(dsb, qs)
            dv_ref[0, rows, :] += _dot_tn(p.astype(CDT), do)
            dck_ref[0, :, rows] += -jnp.sum(ds, axis=0, keepdims=True)
            dcq_s[...] += jnp.sum(ds, axis=-1, keepdims=True)

        _masked_or_not(flags, step)

        @pl.when(flags & LAST != 0)
        def _():
            dq_ref[0] = (dq_s[...] * ATT_SCALE).astype(dq_ref.dtype)
            dcq_ref[0] = dcq_s[...]

    whole = pl.BlockSpec((1, S, HD), lambda h, n, qi, kj, fl: (h, 0, 0))
    grid_spec = pltpu.PrefetchScalarGridSpec(
        num_scalar_prefetch=3, grid=(NH, int(qi.shape[0])),
        in_specs=[_head_blk(tq, False, h0[0]), _head_blk(tk, True, h0[1]), _head_blk(tk, True, h0[2]), qcol, krow, qblk, qblk, qcol],
        out_specs=(qblk, whole, whole, pl.BlockSpec((1, 1, S), lambda h, n, qi, kj, fl: (h, 0, 0)), qcol),
        scratch_shapes=[pltpu.VMEM((tq, HD), F32), pltpu.VMEM((tq, 1), F32), pltpu.VMEM((tq, 1), F32)])
    return pl.pallas_call(body, name=name, grid_spec=grid_spec,
                          out_shape=(jax.ShapeDtypeStruct((NH, S, HD), CDT), jax.ShapeDtypeStruct((NH, S, HD), F32),
                                     jax.ShapeDtypeStruct((NH, S, HD), F32), jax.ShapeDtypeStruct((NH, 1, S), F32),
                                     jax.ShapeDtypeStruct((NH, S, 1), F32)),
                          compiler_params=_params(("parallel", "arbitrary")))(qi, kj, fl, q, k, v, cq, ck, o, do, lse)


LOG2E = 1.4426950408889634


def _proj_qkv(h1, w_qkv, name, *, tm=1024):
    S, K = h1.shape
    tm = _tile(S, tm, 16)
    SQ, SK = 3, 4

    def heads(t):
        return [t[:, h * HD:(h + 1) * HD] for h in range(NH)]

    def body(a_ref, b_ref, o_ref, q2_ref, k2_ref):
        j = pl.program_id(1)
        ob = _dot(a_ref[...], b_ref[...]).astype(CDT)
        for h, t in enumerate(heads(ob)):
            o_ref[h] = t

        @pl.when(j == SQ)
        def _():
            qf = ob.astype(F32) * (ATT_SCALE * LOG2E)
            hi = qf.astype(CDT)
            lo = (qf - hi.astype(F32)).astype(CDT)
            for h, (th, tl) in enumerate(zip(heads(hi), heads(lo))):
                q2_ref[h] = jnp.concatenate([th, tl], axis=1)

        @pl.when(j == SK)
        def _():
            for h, t in enumerate(heads(ob)):
                k2_ref[h] = jnp.concatenate([t, t], axis=1)

    wide = pl.BlockSpec((NH, tm, 2 * HD), lambda i, j: (0, i, 0))
    return pl.pallas_call(
        body, name=name, grid=(S // tm, 6),
        in_specs=[pl.BlockSpec((tm, K), lambda i, j: (i, 0)), pl.BlockSpec((K, GW), lambda i, j: (0, j))],
        out_specs=(pl.BlockSpec((NH, tm, HD), lambda i, j: (j, i, 0)), wide, wide),
        out_shape=(jax.ShapeDtypeStruct((6 * NH, S, HD), CDT), jax.ShapeDtypeStruct((NH, S, 2 * HD), CDT),
                   jax.ShapeDtypeStruct((NH, S, 2 * HD), CDT)),
        compiler_params=_params(("parallel", "arbitrary")))(h1, w_qkv)


def _sb_softplus2(q2, k2sub, mask):
    z2 = _dot_nt(q2, k2sub)
    sp2 = jnp.maximum(z2, 0.0) + jnp.log2(1.0 + jnp.exp2(-jnp.abs(z2)))
    return z2, sp2 if mask is None else jnp.where(mask, sp2, 0.0)


def _strict_tri(n, upper, value):
    r = lax.broadcasted_iota(jnp.int32, (n, n), 0)
    c = lax.broadcasted_iota(jnp.int32, (n, n), 1)
    return jnp.where(r < c if upper else r > c, value, 0.0).astype(CDT)


def _sb_fwd(q2, k2, v, name, v0=0):
    S = q2.shape[1]
    tq, tk = _att_tiles("sb_fwd", S)
    W = min(W_SB, tk)
    qi, kj, fl = _pairs(S, tq, tk)
    qblk, kblk, qcol, _ = _att_specs(tq, tk)
    q2blk, k2blk, _, _ = _att_specs(tq, tk, 2 * HD)

    def body(qi_ref, kj_ref, fl_ref, q_ref, k_ref, v_ref, o_ref, lt_ref, run_s, acc_s):
        n = pl.program_id(1)
        i, kb, flags = qi_ref[n], kj_ref[n], fl_ref[n]

        @pl.when(flags & FIRST != 0)
        def _():
            run_s[...] = jnp.zeros_like(run_s)
            acc_s[...] = jnp.zeros_like(acc_s)

        def step(masked):
            neg_later = _strict_tri(W, False, -1.0)
            run = run_s[...]
            acc = acc_s[...]
            for sub in range(tk // W - 1, -1, -1):
                cols = slice(sub * W, (sub + 1) * W)
                mask = _causal(tq, W, i * tq - kb * tk - sub * W, True) if masked else None
                z2, sp2 = _sb_softplus2(q_ref[0], k_ref[0, cols, :], mask)
                excl = _dot(sp2.astype(CDT), neg_later)
                a = jnp.exp2((z2 - sp2) + (excl + run))
                if masked:
                    a = jnp.where(mask, a, 0.0)
                acc = acc + _dot(a.astype(CDT), v_ref[0, cols, :])
                run = run + (excl[:, 0:1] - sp2[:, 0:1])
            run_s[...] = run
            acc_s[...] = acc

        _masked_or_not(flags, step)

        @pl.when(flags & LAST != 0)
        def _():
            o_ref[0] = acc_s[...]
            lt_ref[0] = run_s[...]

    grid_spec = pltpu.PrefetchScalarGridSpec(
        num_scalar_prefetch=3, grid=(NH, int(qi.shape[0])), in_specs=[q2blk, k2blk, _head_blk(tk, True, v0)], out_specs=(qblk, qcol),
        scratch_shapes=[pltpu.VMEM((tq, 1), F32), pltpu.VMEM((tq, HD), F32)])
    return pl.pallas_call(body, name=name, grid_spec=grid_spec,
                          out_shape=(jax.ShapeDtypeStruct((NH, S, HD), F32), jax.ShapeDtypeStruct((NH, S, 1), F32)),
                          compiler_params=_params(("parallel", "arbitrary")))(qi, kj, fl, q2, k2, v)


def _sb_bwd(q, q2, k2, v, do, lt, name, scatter=None, q0=0, v0=0):
    S = q.shape[1]
    tq, tk = _att_tiles("sb_bwd", S)
    W = min(W_SB, tk)
    qi, kj, fl = _pairs(S, tq, tk, descending=False)
    qblk, kblk, qcol, _ = _att_specs(tq, tk)
    q2blk, k2blk, _, _ = _att_specs(tq, tk, 2 * HD)
    npairs = int(qi.shape[0])

    def body(qi_ref, kj_ref, fl_ref, q_ref, q2_ref, k2_ref, v_ref, do_ref, lt_ref, *rest):
        if scatter is None:
            dq_ref, dk_ref, dv_ref, passed_s, gsum_s, dq_s = rest
        else:
            h_ref, dq_ref, dk_ref, dv_ref, recv_ref, passed_s, gsum_s, dq_s, send_sems, recv_sems = rest
        n = pl.program_id(1)
        i, kb, flags = qi_ref[n], kj_ref[n], fl_ref[n]
        if scatter is not None:
            h = pl.program_id(0)
            _scatter_steps(h_ref, recv_ref, send_sems, recv_sems, first=(h == 0) & (n == 0), last=(h == NH - 1) & (n == npairs - 1))

        @pl.when(n == 0)
        def _():
            dk_ref[...] = jnp.zeros_like(dk_ref)
            dv_ref[...] = jnp.zeros_like(dv_ref)

        @pl.when(flags & FIRST != 0)
        def _():
            passed_s[...] = jnp.zeros_like(passed_s)
            gsum_s[...] = jnp.zeros_like(gsum_s)
            dq_s[...] = jnp.zeros_like(dq_s)

        def step(masked):
            qs = q_ref[0] * ATT_SCALE
            do = do_ref[0]
            neg_later = _strict_tri(W, False, -1.0)
            earlier = _strict_tri(W, True, 1.0)
            for sub in range(tk // W):
                cols = slice(sub * W, (sub + 1) * W)
                mask = _causal(tq, W, i * tq - kb * tk - sub * W, True) if masked else None
                ksub = k2_ref[0, cols, 0:HD]
                z2, sp2 = _sb_softplus2(q2_ref[0], k2_ref[0, cols, :], mask)
                excl = _dot(sp2.astype(CDT), neg_later)
                through = passed_s[...] + (excl[:, 0:1] - sp2[:, 0:1])
                t1 = z2 - sp2
                sig = jnp.exp2(t1)
                a = jnp.exp2(t1 + (excl + (lt_ref[0] - through)))
                if masked:
                    a = jnp.where(mask, a, 0.0)
                dl = _dot_nt(do, v_ref[0, cols, :]) * a
                before = _dot(dl.astype(CDT), earlier)
                dz = dl - sig * (dl + (before + gsum_s[...]))
                if masked:
                    dz = jnp.where(mask, dz, 0.0)
                dzb = dz.astype(CDT)
                dq_s[...] += _dot(dzb, ksub)
                rows = pl.ds(pl.multiple_of(kb * tk + sub * W, W), W)
                dk_ref[0, rows, :] += _dot_tn(dzb, qs)
                dv_ref[0, rows, :] += _dot_tn(a.astype(CDT), do)
                passed_s[...] = through
                gsum_s[...] += before[:, W - 1:W] + dl[:, W - 1:W]

        _masked_or_not(flags, step)

        @pl.when(flags & LAST != 0)
        def _():
            dq_ref[0] = (dq_s[...] * ATT_SCALE).astype(dq_ref.dtype)

    whole = pl.BlockSpec((1, S, HD), lambda h, n, qi, kj, fl: (h, 0, 0))
    in_specs = [_head_blk(tq, False, q0), q2blk, k2blk, _head_blk(tk, True, v0), qblk, qcol]
    out_specs = (qblk, whole, whole)
    out_shape = (jax.ShapeDtypeStruct((NH, S, HD), CDT), jax.ShapeDtypeStruct((NH, S, HD), F32), jax.ShapeDtypeStruct((NH, S, HD), F32))
    scratch = [pltpu.VMEM((tq, 1), F32), pltpu.VMEM((tq, 1), F32), pltpu.VMEM((tq, HD), F32)]
    if scatter is None:
        grid_spec = pltpu.PrefetchScalarGridSpec(num_scalar_prefetch=3, grid=(NH, npairs), in_specs=in_specs,
                                                 out_specs=out_specs, scratch_shapes=scratch)
        return (*pl.pallas_call(body, name=name, grid_spec=grid_spec, out_shape=out_shape,
                                compiler_params=_params(("parallel", "arbitrary")))(qi, kj, fl, q, q2, k2, v, do, lt), None)
    grid_spec = pltpu.PrefetchScalarGridSpec(num_scalar_prefetch=3, grid=(NH, npairs), in_specs=in_specs + [ANY],
                                             out_specs=(*out_specs, ANY), scratch_shapes=scratch + list(SCATTER_SEMS))
    recv_shape = jax.ShapeDtypeStruct((3,) + scatter.shape[1:], scatter.dtype)
    return pl.pallas_call(body, name=name, grid_spec=grid_spec, out_shape=(*out_shape, recv_shape),
                          compiler_params=_params(("arbitrary", "arbitrary")))(qi, kj, fl, q, q2, k2, v, do, lt, scatter)


def _mem_probs(q_ref, kv_ref, h):
    cols = slice(h * MHD, (h + 1) * MHD)
    s = _dot_nt(q_ref[:, cols], kv_ref[:, cols]) * MEM_SCALE
    e = jnp.exp(s - jnp.max(s, axis=-1, keepdims=True))
    return e / jnp.sum(e, axis=-1, keepdims=True)


def _xattn_fwd(q, kv, w_mo, x1, name, *, tm=512):
    S = q.shape[0]
    tm = _tile(S, tm, 16)
    nm = kv.shape[0]

    def body(q_ref, kv_ref, w_ref, x_ref, x2_ref, o_ref):
        for h in range(NMH):
            p = _mem_probs(q_ref, kv_ref, h)
            o_ref[:, h * MHD:(h + 1) * MHD] = _dot(p.astype(CDT), kv_ref[:, D + h * MHD:D + (h + 1) * MHD]).astype(CDT)
        x2_ref[...] = x_ref[...] + _dot(o_ref[...], w_ref[...])

    row = pl.BlockSpec((tm, D), lambda i: (i, 0))
    return pl.pallas_call(body, name=name, grid=(S // tm,),
                          in_specs=[row, pl.BlockSpec((nm, 2 * D), lambda i: (0, 0)), pl.BlockSpec((D, D), lambda i: (0, 0)), row],
                          out_specs=(row, row),
                          out_shape=(jax.ShapeDtypeStruct((S, D), F32), jax.ShapeDtypeStruct((S, D), CDT)),
                          compiler_params=_params(("parallel",)))(q, kv, w_mo, x1)


def _xattn_bwd(q, kv, do, name, *, tm=512):
    S = q.shape[0]
    tm = _tile(S, tm, 16)
    nm = kv.shape[0]

    def body(q_ref, kv_ref, do_ref, dq_ref, dkv_ref):
        @pl.when(pl.program_id(0) == 0)
        def _():
            dkv_ref[...] = jnp.zeros_like(dkv_ref)
        for h in range(NMH):
            cols = slice(h * MHD, (h + 1) * MHD)
            vcols = slice(D + h * MHD, D + (h + 1) * MHD)
            p = _mem_probs(q_ref, kv_ref, h)
            doh = do_ref[:, cols]
            dp = _dot_nt(doh, kv_ref[:, vcols])
            ds = (p * (dp - jnp.sum(p * dp, axis=-1, keepdims=True)) * MEM_SCALE).astype(CDT)
            dq_ref[:, cols] = _dot(ds, kv_ref[:, cols]).astype(CDT)
            dkv_ref[:, cols] += _dot_tn(ds, q_ref[:, cols])
            dkv_ref[:, vcols] += _dot_tn(p.astype(CDT), doh)

    row = pl.BlockSpec((tm, D), lambda i: (i, 0))
    kvs = pl.BlockSpec((nm, 2 * D), lambda i: (0, 0))
    return pl.pallas_call(body, name=name, grid=(S // tm,), in_specs=[row, kvs, row], out_specs=(row, kvs),
                          out_shape=(jax.ShapeDtypeStruct((S, D), CDT), jax.ShapeDtypeStruct((nm, 2 * D), F32)),
                          compiler_params=_params(("arbitrary",)))(q, kv, do)


HALO = 16
SLAB = 8


def _shift_down(u, prev, s):
    rolled = pltpu.roll(u, s, 0)
    top = rolled[0:SLAB]
    r = lax.broadcasted_iota(jnp.int32, top.shape, 0)
    for t in range(s):
        top = jnp.where(r == t, prev[HALO - s + t:HALO - s + t + 1, :], top)
    return jnp.concatenate([top, rolled[SLAB:]], axis=0)


def _shift_up(u, nxt, s):
    n = u.shape[0]
    rolled = pltpu.roll(u, n - s, 0)
    bottom = rolled[n - SLAB:]
    r = lax.broadcasted_iota(jnp.int32, bottom.shape, 0)
    for t in range(s):
        bottom = jnp.where(r == SLAB - s + t, nxt[t:t + 1, :], bottom)
    return jnp.concatenate([rolled[:n - SLAB], bottom], axis=0)


def _conv_taps(u_ref, h_ref, first):
    u = u_ref[...].astype(F32)
    prev = jnp.where(first, 0.0, h_ref[...].astype(F32))
    out = []
    for half in range(2):
        out.append((u[half], _shift_down(u[half], prev[half], 1), _shift_down(u[half], prev[half], 2)))
    return out


def _conv_specs(tm, tn, nsb):
    blk = pl.BlockSpec((2, tm, tn), lambda j, i: (0, i, j))
    prev = pl.BlockSpec((2, HALO, tn), lambda j, i: (0, jnp.maximum(i * (tm // HALO) - 1, 0), j))
    nxt = pl.BlockSpec((2, HALO, tn), lambda j, i: (0, jnp.minimum((i + 1) * (tm // HALO), nsb - 1), j))
    w = pl.BlockSpec((2, 3, tn), lambda j, i: (0, 0, j))
    b = pl.BlockSpec((2, 1, tn), lambda j, i: (0, 0, j))
    return blk, prev, nxt, w, b


def _conv_apply(taps, w_ref, b_ref):
    ys = []
    for half in range(2):
        u, u1, u2 = taps[half]
        w = w_ref[half]
        ys.append(b_ref[half] + u2 * w[0:1, :] + u1 * w[1:2, :] + u * w[2:3, :])
    return ys


def _conv_act(u0, cw, cb, name, *, tm=2048, tn=256):
    _, S, F = u0.shape
    tm = _tile(S, tm, HALO)
    tn = _tile(F, tn, 128)
    blk, prev, _, w, b = _conv_specs(tm, tn, S // HALO)

    def body(u_ref, h_ref, w_ref, b_ref, a_ref):
        yg, yv = _conv_apply(_conv_taps(u_ref, h_ref, pl.program_id(1) == 0), w_ref, b_ref)
        a_ref[...] = (yg * jax.nn.sigmoid(yg) * yv).astype(a_ref.dtype)

    return pl.pallas_call(body, name=name, grid=(F // tn, S // tm), in_specs=[blk, prev, w, b],
                          out_specs=pl.BlockSpec((tm, tn), lambda j, i: (i, j)),
                          out_shape=jax.ShapeDtypeStruct((S, F), CDT),
                          compiler_params=_params(("parallel", "parallel")))(u0, u0, cw, cb)


def _conv_act_bwd(u0, da, cw, cb, name, *, tm=2048, tn=256):
    _, S, F = u0.shape
    tm = _tile(S, tm, HALO)
    tn = _tile(F, tn, 128)
    blk, prev, _, w, b = _conv_specs(tm, tn, S // HALO)

    def body(u_ref, h_ref, da_ref, w_ref, b_ref, du_ref, dwb_ref):
        @pl.when(pl.program_id(1) == 0)
        def _():
            dwb_ref[...] = jnp.zeros_like(dwb_ref)
        taps = _conv_taps(u_ref, h_ref, pl.program_id(1) == 0)
        yg, yv = _conv_apply(taps, w_ref, b_ref)
        sg = jax.nn.sigmoid(yg)
        da = da_ref[...].astype(F32)
        dus = (da * yv * sg * (1.0 + yg * (1.0 - sg)), da * yg * sg)
        for half in range(2):
            du = dus[half]
            du_ref[half] = du.astype(du_ref.dtype)
            u, u1, u2 = taps[half]
            for row, term in enumerate((du * u2, du * u1, du * u, du)):
                dwb_ref[half, row:row + 1, :] += jnp.sum(term, axis=0, keepdims=True)

    return pl.pallas_call(body, name=name, grid=(F // tn, S // tm),
                          in_specs=[blk, prev, pl.BlockSpec((tm, tn), lambda j, i: (i, j)), w, b],
                          out_specs=(blk, pl.BlockSpec((2, 4, tn), lambda j, i: (0, 0, j))),
                          out_shape=(jax.ShapeDtypeStruct((2, S, F), CDT), jax.ShapeDtypeStruct((2, 4, F), F32)),
                          compiler_params=_params(("parallel", "arbitrary")))(u0, u0, da, cw, cb)


def _conv_bwd_input(du, cw, name, *, tm=2048, tn=256):
    _, S, F = du.shape
    tm = _tile(S, tm, HALO)
    tn = _tile(F, tn, 128)
    blk, _, nxt, w, _ = _conv_specs(tm, tn, S // HALO)
    ni = S // tm

    def body(d_ref, h_ref, w_ref, o_ref):
        d = d_ref[...].astype(F32)
        nx = jnp.where(pl.program_id(1) == ni - 1, 0.0, h_ref[...].astype(F32))
        for half in range(2):
            wv = w_ref[half]
            y = d[half] * wv[2:3, :] + _shift_up(d[half], nx[half], 1) * wv[1:2, :] + _shift_up(d[half], nx[half], 2) * wv[0:1, :]
            o_ref[half] = y.astype(o_ref.dtype)

    return pl.pallas_call(body, name=name, grid=(F // tn, ni), in_specs=[blk, nxt, w], out_specs=blk,
                          out_shape=jax.ShapeDtypeStruct((2, S, F), CDT),
                          compiler_params=_params(("parallel", "parallel")))(du, du, cw)


ANY = pl.BlockSpec(memory_space=pl.ANY)


def _place():
    return lax.axis_index("x"), lax.axis_index("y"), lax.axis_index("c")


def _other_chips(x, y):
    return ((1 - x, y), (x, 1 - y), (1 - x, 1 - y))


def _when(pred, fn):
    if pred is True:
        fn()
    else:
        pl.when(pred)(fn)


GATHER_SEMS = (pltpu.SemaphoreType.DMA((6,)), pltpu.SemaphoreType.DMA((6,)))
SCATTER_SEMS = (pltpu.SemaphoreType.DMA((3,)), pltpu.SemaphoreType.DMA((3,)))


def _gather_steps(out_ref, send_sems, recv_sems, first=True, middle=True, last=True):
    half = out_ref.shape[1] // 2
    x, y, c = _place()
    chips = _other_chips(x, y)

    def part(chip, pc):
        return out_ref.at[2 * chip[0] + chip[1], pl.ds(pl.multiple_of(pc * half, 16), half), :]

    def copy(k, chip, pc, to):
        return pltpu.make_async_remote_copy(src_ref=part(chip, pc), dst_ref=part(chip, pc),
                                            send_sem=send_sems.at[k], recv_sem=recv_sems.at[k],
                                            device_id=to, device_id_type=MESH)

    def send_mine():
        for j, chip in enumerate(chips):
            copy(j, (x, y), c, (*chip, c)).start()

    def pass_on():
        for j, chip in enumerate(chips):
            copy(j, chip, c, (x, y, c)).wait_recv()
            copy(3 + j, chip, c, (x, y, 1 - c)).start()

    def finish():
        for j, chip in enumerate(chips):
            copy(3 + j, chip, 1 - c, (x, y, c)).wait_recv()
        for j, chip in enumerate(chips):
            copy(j, (x, y), c, (*chip, c)).wait_send()
            copy(3 + j, chip, c, (x, y, 1 - c)).wait_send()

    _when(first, send_mine)
    _when(middle, pass_on)
    _when(last, finish)


def _gather_weights(buf):
    def body(buf_ref, out_ref, send_sems, recv_sems):
        del buf_ref
        _gather_steps(out_ref, send_sems, recv_sems)

    return pl.pallas_call(body, name="gather_weights", in_specs=[ANY], out_specs=ANY,
                          out_shape=jax.ShapeDtypeStruct(buf.shape, buf.dtype), input_output_aliases={0: 0},
                          scratch_shapes=list(GATHER_SEMS))(buf)


def _gather_small(v):
    m = v.shape[0]

    def body(v_ref, out_ref, send_sems, recv_sems, local_sem):
        x, y, c = _place()
        me, sibling = (x, y, c), (x, y, 1 - c)
        chips = _other_chips(x, y)

        def rows(px, py, pc):
            return out_ref.at[pl.ds((4 * px + 2 * py + pc) * m, m), :]

        def copy(k, block, to, src=None):
            return pltpu.make_async_remote_copy(src_ref=rows(*block) if src is None else src, dst_ref=rows(*block),
                                                send_sem=send_sems.at[k], recv_sem=recv_sems.at[k],
                                                device_id=to, device_id_type=MESH)

        mine = pltpu.make_async_copy(v_ref, rows(*me), local_sem)
        mine.start()
        first = [copy(0, me, sibling, src=v_ref)]
        first += [copy(1 + j, me, (*chip, c), src=v_ref) for j, chip in enumerate(chips)]
        for cp in first:
            cp.start()
        passed = [copy(4 + j, (*chip, c), sibling) for j, chip in enumerate(chips)]
        for j, chip in enumerate(chips):
            copy(1 + j, (*chip, c), me).wait_recv()
            passed[j].start()
        copy(0, sibling, me).wait_recv()
        for j, chip in enumerate(chips):
            copy(4 + j, (*chip, 1 - c), me).wait_recv()
        for cp in first + passed:
            cp.wait_send()
        mine.wait()

    vm = pl.BlockSpec(memory_space=pltpu.VMEM)
    return pl.pallas_call(body, name="gather_small", in_specs=[vm], out_specs=vm,
                          out_shape=jax.ShapeDtypeStruct((8 * m, 128), v.dtype),
                          scratch_shapes=[pltpu.SemaphoreType.DMA((7,)), pltpu.SemaphoreType.DMA((7,)), pltpu.SemaphoreType.DMA])(v)


def _swap_halves(g, name):
    n, rows, _ = g.shape
    half = rows // 2

    def body(g_ref, out_ref, send_sem, recv_sem):
        x, y, c = _place()
        src = g_ref.at[:, pl.ds(pl.multiple_of((1 - c) * half, 8), half), :]
        cp = pltpu.make_async_remote_copy(src_ref=src, dst_ref=out_ref, send_sem=send_sem, recv_sem=recv_sem,
                                          device_id=(x, y, 1 - c), device_id_type=MESH)
        cp.start()
        cp.wait()

    return pl.pallas_call(body, name=name, in_specs=[ANY], out_specs=ANY,
                          out_shape=jax.ShapeDtypeStruct((n, half, 128), g.dtype),
                          scratch_shapes=[pltpu.SemaphoreType.DMA, pltpu.SemaphoreType.DMA])(g)


def _scatter_steps(h_ref, out_ref, send_sems, recv_sems, first=True, last=True):
    x, y, c = _place()

    def copies():
        return [pltpu.make_async_remote_copy(src_ref=h_ref.at[2 * chip[0] + chip[1]], dst_ref=out_ref.at[j],
                                             send_sem=send_sems.at[j], recv_sem=recv_sems.at[j],
                                             device_id=(*chip, c), device_id_type=MESH)
                for j, chip in enumerate(_other_chips(x, y))]

    def start():
        for cp in copies():
            cp.start()

    def finish():
        for cp in copies():
            cp.wait()

    _when(first, start)
    _when(last, finish)


def _scatter_chips(hsum):
    n, half, _ = hsum.shape

    def body(h_ref, out_ref, send_sems, recv_sems):
        _scatter_steps(h_ref, out_ref, send_sems, recv_sems)

    return pl.pallas_call(body, name="scatter_chips", in_specs=[ANY], out_specs=ANY,
                          out_shape=jax.ShapeDtypeStruct((3, half, 128), hsum.dtype),
                          scratch_shapes=list(SCATTER_SEMS))(hsum)


def _join_halves(buf, name):
    half = buf.shape[0] // 2

    def body(buf_ref, out_ref, send_sem, recv_sem):
        del buf_ref
        x, y, c = _place()
        mine = out_ref.at[pl.ds(pl.multiple_of(c * half, 8), half), :]
        other = out_ref.at[pl.ds(pl.multiple_of((1 - c) * half, 8), half), :]
        cp = pltpu.make_async_remote_copy(src_ref=mine, dst_ref=mine, send_sem=send_sem, recv_sem=recv_sem,
                                          device_id=(x, y, 1 - c), device_id_type=MESH)
        cp.start()
        cp.wait_send()
        pltpu.make_async_remote_copy(src_ref=other, dst_ref=other, send_sem=send_sem, recv_sem=recv_sem,
                                     device_id=(x, y, 1 - c), device_id_type=MESH).wait_recv()

    return pl.pallas_call(body, name=name, in_specs=[ANY], out_specs=ANY,
                          out_shape=jax.ShapeDtypeStruct(buf.shape, buf.dtype), input_output_aliases={0: 0},
                          scratch_shapes=[pltpu.SemaphoreType.DMA, pltpu.SemaphoreType.DMA])(buf)


def _add_sibling(g, recv, c_idx, name):
    n, rows, _ = g.shape
    half = rows // 2
    tr = _tile(half, ADAM_ROWS, 16)
    nb = half // tr

    def body(c_ref, g_ref, r_ref, o_ref, ob_ref):
        s = g_ref[...] + r_ref[...]
        o_ref[...] = s
        ob_ref[...] = s.astype(CDT)

    out = pl.BlockSpec((None, tr, 128), lambda k, i, c: (k, i, 0))
    grid_spec = pltpu.PrefetchScalarGridSpec(
        num_scalar_prefetch=1, grid=(n, nb),
        in_specs=[pl.BlockSpec((None, tr, 128), lambda k, i, c: (k, c[0] * nb + i, 0)), out],
        out_specs=(out, out))
    return pl.pallas_call(body, name=name, grid_spec=grid_spec,
                          out_shape=(jax.ShapeDtypeStruct((n, half, 128), F32), jax.ShapeDtypeStruct((n, half, 128), CDT)),
                          compiler_params=_params(("parallel", "parallel")))(c_idx, g, recv)


def _add_chips(hsum, recv, chip_idx, name):
    n, half, _ = hsum.shape
    tr = _tile(half, ADAM_ROWS, 16)

    def body(k_ref, h_ref, r_ref, o_ref):
        o_ref[...] = ((h_ref[...] + r_ref[0].astype(F32)) + r_ref[1].astype(F32)) + r_ref[2].astype(F32)

    grid_spec = pltpu.PrefetchScalarGridSpec(
        num_scalar_prefetch=1, grid=(half // tr,),
        in_specs=[pl.BlockSpec((None, tr, 128), lambda i, k: (k[0], i, 0)),
                  pl.BlockSpec((3, tr, 128), lambda i, k: (0, i, 0))],
        out_specs=pl.BlockSpec((tr, 128), lambda i, k: (i, 0)))
    return pl.pallas_call(body, name=name, grid_spec=grid_spec, out_shape=jax.ShapeDtypeStruct((half, 128), F32),
                          compiler_params=_params(("parallel",)))(chip_idx, hsum, recv)


def _adamw_math(g, w, m, v):
    m2 = B1 * m + (1.0 - B1) * g
    v2 = B2 * v + (1.0 - B2) * (g * g)
    delta = -LR * ((m2 / BC1) / (jnp.sqrt(v2 / BC2) + AEPS) + WD * w)
    return delta, m2, v2


def _adamw(g, w, m, v, name):
    rows, cols = g.shape
    tr = _tile(rows, max(8, (ADAM_ROWS * 128 // cols) // 8 * 8), 8)

    def body(g_ref, w_ref, m_ref, v_ref, d_ref, m2_ref, v2_ref):
        d_ref[...], m2_ref[...], v2_ref[...] = _adamw_math(g_ref[...], w_ref[...], m_ref[...], v_ref[...])

    blk = pl.BlockSpec((None, tr, cols), lambda i: (0, i, 0))
    shp = jax.ShapeDtypeStruct((1, rows, cols), F32)
    return pl.pallas_call(body, name=name, grid=(rows // tr,), in_specs=[pl.BlockSpec((tr, cols), lambda i: (i, 0))] + [blk] * 3,
                          out_specs=(blk,) * 3, out_shape=(shp,) * 3, compiler_params=_params(("parallel",)))(g, w, m, v)


def _adamw_small(parts, w, m, v, name):
    rows = w.shape[0]

    def body(p_ref, w_ref, m_ref, v_ref, g_ref, d_ref, m2_ref, v2_ref):
        g = p_ref[0]
        for k in range(1, 8):
            g = g + p_ref[k]
        g_ref[...] = g
        d_ref[...], m2_ref[...], v2_ref[...] = _adamw_math(g, w_ref[...], m_ref[...], v_ref[...])

    shp = jax.ShapeDtypeStruct((rows, 128), F32)
    return pl.pallas_call(body, name=name, out_shape=(shp,) * 4)(parts, w, m, v)


def _pack_rows(parts, rows):
    flat = jnp.concatenate([p.reshape(-1) for p in parts])
    return jnp.pad(flat, (0, rows * 128 - flat.shape[0])).reshape(rows, 128)


def _unpack(flat, sizes, shapes):
    out, off = [], 0
    for n, s in zip(sizes, shapes):
        out.append(flat[off:off + n].reshape(s))
        off += n
    return out


def _to_shards(full, shard_shape, axis):
    if axis == 0:
        return full.reshape(N_CHIP, -1)
    r, cs = shard_shape
    return full.reshape(r, N_CHIP, cs).transpose(1, 0, 2).reshape(N_CHIP, -1)


def _from_shards(sh, shard_shape, axis):
    r, cs = shard_shape
    if axis == 0:
        return sh.reshape(N_CHIP * r, cs)
    return sh.reshape(N_CHIP, r, cs).transpose(1, 0, 2).reshape(r, N_CHIP * cs)


def _local_step(x0, mem, tgt, W, gains, ex=None):
    S = x0.shape[0]
    w_in = jnp.pad(W["w_in"], ((0, 0), (0, IN_PAD - IN_COLS)))
    b_f = jnp.pad(gains["b_forget"], ((0, 0), (0, 128 - NH)))

    h1 = _rms_cast(x0, gains["attn_norm_g"], "norm_attn")
    qkv, sq2, sk2 = _proj_qkv(h1, w_in[:, :NQKV], "proj_qkv")
    fox, sb_q, sb_v = (0, NH, 2 * NH), 3 * NH, 5 * NH
    fl = _mm_nn(h1, w_in[:, NQKV:NQKV + 128], F32, "proj_gate")
    cum = _gate_fwd(fl, b_f, "gate_cumsum")
    c_hm = cum[:, :NH].T
    cq, ck = c_hm[:, :, None], c_hm[:, None, :]
    fo_h, lse, gathered = _fox_fwd(qkv, qkv, qkv, cq, ck, "fox_fwd", slots=None if ex is None else ex.slots("b"), h0=fox)
    if ex is not None:
        W = {**W, **ex.unpack("b", gathered)}
    cw = W["conv_w"].reshape(3, 2, DFF).transpose(1, 0, 2)
    cb = gains["conv_b"].reshape(2, 1, DFF)
    so_h, s_lt = _sb_fwd(sq2, sk2, qkv, "sb_fwd", v0=sb_v)
    x1, mixed = _out_proj(fo_h, so_h, gains["fox_out_g"], gains["sb_out_g"], W["w_out"], x0, "out_proj")

    h2 = _rms_cast(x1, gains["xattn_norm_g"], "norm_xattn")
    mn = _rms_cast(mem, gains["mem_norm_g"], "norm_mem")
    mq = _mm_nn(h2, W["w_mq"], CDT, "proj_mq")
    kv = _mm_nn(mn, W["w_mkv"], CDT, "proj_mkv")
    x2, mo = _xattn_fwd(mq, kv, W["w_mo"], x1, "xattn_fwd")

    h3 = _rms_cast(x2, gains["ffn_norm_g"], "norm_ffn")
    u0 = _mm_nn(h3, W["w_up"], CDT, "ffn_up", tm=512, tn=DFF, halves=True)
    act = _conv_act(u0, cw, cb, "conv_act")
    x3 = _mm_nn(act, W["w_down"], F32, "ffn_down", tm=512, residual=x2)
    loss, dx3, dg_final = _loss_bwd(x3, tgt, gains["final_norm_g"].reshape(1, D), "loss")

    gw, gs = {}, {"final_norm_g": dg_final}
    da = _mm_nt(dx3, W["w_down"], "ffn_down_dx", tn=DFF, out_dtype=CDT)
    gw["w_down"] = _mm_tn(act, dx3, "ffn_down_dw", tka=DFF)
    du, dwb = _conv_act_bwd(u0, da, cw, cb, "conv_act_bwd")
    gw["conv_w"] = dwb[:, :3].transpose(1, 0, 2).reshape(3, 2 * DFF)
    gs["conv_b"] = dwb[:, 3].reshape(1, 2 * DFF)
    du0 = _conv_bwd_input(du, cw, "conv_bwd_input")
    gw["w_up"] = _mm_tn(h3, du0, "ffn_up_dw", tn=DFF, b_halves=True)
    dx2, gs["ffn_norm_g"] = _mm_nt_rmsbwd(du0, W["w_up"], x2, gains["ffn_norm_g"], dx3, "ffn_up_dx", tk=DFF, a_halves=True)

    dmo = _mm_nt(dx2, W["w_mo"], "mo_dx", tn=512, out_dtype=CDT)
    gw["w_mo"] = _mm_tn(mo, dx2, "mo_dw")
    dmq, dkv = _xattn_bwd(mq, kv, dmo, "xattn_bwd")
    gw["w_mq"] = _mm_tn(h2, dmq, "mq_dw")
    dx1, gs["xattn_norm_g"] = _mm_nt_rmsbwd(dmq, W["w_mq"], x1, gains["xattn_norm_g"], dx2, "mq_dx")
    gw["w_mkv"] = _mm_tn(mn, dkv, "mkv_dw")
    _, gs["mem_norm_g"] = _mm_nt_rmsbwd(dkv, W["w_mkv"], mem, gains["mem_norm_g"], jnp.zeros_like(mem), "mkv_dx")

    gw["w_out"] = _mm_tn(mixed, dx1, "out_dw")
    dfo_h, dso_h, gs["fox_out_g"], gs["sb_out_g"] = _out_proj_bwd(dx1, W["w_out"], fo_h, so_h, gains["fox_out_g"], gains["sb_out_g"], "out_dx")
    dfq, dfk, dfv, dck, dcq = _fox_bwd(qkv, qkv, qkv, cq, ck, fo_h, dfo_h, lse, "fox_bwd", h0=fox)
    pair, pair16 = (None, None) if ex is None else ex.pair_sums("b", gw)
    dsq, dsk, dsv, arrived = _sb_bwd(qkv, sq2, sk2, qkv, dso_h, s_lt, "sb_bwd", scatter=pair16, q0=sb_q, v0=sb_v)
    dc = jnp.pad((dck[:, 0, :] + dcq[:, :, 0]).T, ((0, 0), (0, 128 - NH)))
    dfl, db = _gate_bwd(dc, fl, b_f, "gate_bwd")
    gs["b_forget"] = db[:, :NH]
    dqkv = jnp.concatenate([dfq, dfk.astype(CDT), dfv.astype(CDT), dsq, dsk.astype(CDT), dsv.astype(CDT)], axis=0)
    dproj = jnp.concatenate([dqkv.transpose(1, 0, 2).reshape(S, NQKV), dfl.astype(CDT),
                             jnp.zeros((S, IN_PAD - NQKV - 128), CDT)], axis=1)
    gw["w_in"] = _mm_tn(h1, dproj, "in_dw", tn=IN_PAD)[:, :IN_COLS]
    dx0, gs["attn_norm_g"] = _mm_nt_rmsbwd(dproj, w_in, x0, gains["attn_norm_g"], dx1, "in_dx", tk=IN_PAD)
    return loss, dx0, gw, gs, (pair, arrived)


NAMES = ("attn_norm_g", "w_in", "b_forget", "fox_out_g", "sb_out_g", "w_out", "xattn_norm_g", "mem_norm_g", "w_mq",
         "w_mkv", "w_mo", "ffn_norm_g", "w_up", "conv_w", "conv_b", "w_down", "final_norm_g")


class _Exchange:
    def __init__(self, w):
        self.w = w
        xi, yi, ci = _place()
        self.core = ci
        self.chip = 2 * xi + yi
        self.core_idx = jnp.reshape(ci, (1,)).astype(jnp.int32)
        self.chip_idx = jnp.reshape(self.chip, (1,)).astype(jnp.int32)

    def slots(self, g):
        parts = []
        for name, shape, _ in GROUPS[g]:
            blk = self.w[name].reshape(shape)
            parts.append(lax.bitcast_convert_type(blk, CDT) if name == "conv_w" else blk.astype(CDT))
        rows = _rows_g(GROUPS[g])
        return lax.dynamic_update_slice(lax.empty((N_CHIP, rows, 128), CDT), _pack_rows(parts, rows)[None], (self.chip, 0, 0))

    def unpack(self, g, gathered):
        flat, full, off = gathered.reshape(N_CHIP, -1), {}, 0
        for (name, shape, axis), n in zip(GROUPS[g], _gather_sizes(GROUPS[g])):
            sh = flat[:, off:off + n]
            off += n
            if name == "conv_w":
                sh = lax.bitcast_convert_type(sh.reshape(N_CHIP, n // 2, 2), F32)
            full[name] = _from_shards(sh, shape, axis)
        return full

    def pair_sums(self, g, gw):
        rows = _rows_f(GROUPS[g])
        flat = jnp.concatenate([_to_shards(gw[name], shape, axis) for name, shape, axis in GROUPS[g]], axis=1)
        flat = jnp.pad(flat, ((0, 0), (0, rows * 128 - flat.shape[1]))).reshape(N_CHIP, rows, 128)
        return _add_sibling(flat, _swap_halves(flat, "swap_halves_" + g), self.core_idx, "add_sibling_" + g)

    def finish(self, g, pair, arrived):
        rows = _rows_f(GROUPS[g])
        mine = _add_chips(pair, arrived, self.chip_idx, "add_chips_" + g)
        whole = _join_halves(lax.dynamic_update_slice(lax.empty((rows, 128), F32), mine, (self.core * (rows // 2), 0)), "join_halves_" + g)
        shapes = [s for _, s, _ in GROUPS[g]]
        return {name: arr for (name, _, _), arr in zip(GROUPS[g], _unpack(whole.reshape(-1), _sizes(GROUPS[g]), shapes))}


def _step(x, mem, loss_target, w, m, v):
    ex = _Exchange(w)

    W = ex.unpack("a", _gather_weights(ex.slots("a")))
    gains = {name: w[name].reshape(1, -1) for name, _ in SMALL}

    loss, grad_x, gw, gs, (pair_b, arrived_b) = _local_step(x[0], mem[0], loss_target[0], W, gains, ex)

    grads = ex.finish("b", pair_b, arrived_b)
    pair_a, pair16_a = ex.pair_sums("a", gw)
    grads.update(ex.finish("a", pair_a, _scatter_chips(pair16_a)))
    small = jnp.concatenate([gs[name].reshape(-1) for name, _ in SMALL] + [loss[0, :1]])
    small = jnp.pad(small, (0, ROWS_S * 128 - P_SMALL)).reshape(ROWS_S, 128)
    small_parts = _gather_small(small).reshape(8, ROWS_S, 128)

    def flat_small(d):
        return _pack_rows([d[name] for name, _ in SMALL], ROWS_S)

    outs = {}
    for name, shape, _ in BIG:
        g = grads[name]
        res = _adamw(g, w[name], m[name], v[name], "adamw_" + name)
        for prefix, arr in zip(("grad_", "delta_", "new_m_", "new_v_"), (g, *res)):
            outs[prefix + name] = arr.reshape(w[name].shape)
    small_res = _adamw_small(small_parts, flat_small(w), flat_small(m), flat_small(v), "adamw_small")
    g_sm = small_res[0]
    for prefix, sm in zip(("grad_", "delta_", "new_m_", "new_v_"), small_res):
        for (name, n), arr in zip(SMALL, _unpack(sm.reshape(-1), [n for _, n in SMALL], [(n,) for _, n in SMALL])):
            outs[prefix + name] = arr.reshape(w[name].shape)
    total_loss = g_sm.reshape(-1)[P_SMALL - 1]
    return (total_loss, grad_x[None], *[outs[p + n] for p in ("grad_", "delta_", "new_m_", "new_v_") for n in NAMES])


def kernel(x, mem, attn_norm_g, w_in, b_forget, fox_out_g, sb_out_g, w_out, xattn_norm_g, mem_norm_g, w_mq, w_mkv, w_mo, ffn_norm_g, w_up, conv_w, conv_b, w_down, final_norm_g, loss_target, m_attn_norm_g, m_w_in, m_b_forget, m_fox_out_g, m_sb_out_g, m_w_out, m_xattn_norm_g, m_mem_norm_g, m_w_mq, m_w_mkv, m_w_mo, m_ffn_norm_g, m_w_up, m_conv_w, m_conv_b, m_w_down, m_final_norm_g, v_attn_norm_g, v_w_in, v_b_forget, v_fox_out_g, v_sb_out_g, v_w_out, v_xattn_norm_g, v_mem_norm_g, v_w_mq, v_w_mkv, v_w_mo, v_ffn_norm_g, v_w_up, v_conv_w, v_conv_b, v_w_down, v_final_norm_g):
    given = dict(locals())
    w = {n: given[n] for n in NAMES}
    m = {n: given["m_" + n] for n in NAMES}
    v = {n: given["v_" + n] for n in NAMES}
    return _step(x, mem, loss_target, w, m, v)
```

```python
import functools

import numpy as np
import jax
import jax.numpy as jnp
from jax import lax
from jax.experimental import pallas as pl
from jax.experimental.pallas import tpu as pltpu

F32 = jnp.float32
CDT = jnp.bfloat16
MESH = pl.DeviceIdType.MESH

D = 1024
HD = 64
NH = 8
GW = NH * HD
NQKV = 6 * GW
IN_COLS = NQKV + NH
IN_PAD = NQKV + 256
NMH = 4
MHD = D // NMH
DFF = 2816
EPS = 1e-6
ATT_SCALE = HD ** -0.5
MEM_SCALE = MHD ** -0.5
NEG = -1e30

LR, B1, B2, AEPS, WD, STEP = 0.001, 0.9, 0.999, 1e-08, 0.01, 10
BC1 = 1.0 - B1 ** STEP
BC2 = 1.0 - B2 ** STEP

ATT_TILES = {"fox_fwd": (512, 1024), "fox_bwd": (512, 1024), "sb_fwd": (512, 1024), "sb_bwd": (1024, 1024)}
W_SB = 256
VMEM_LIMIT = 52 * 2 ** 20

N_CHIP = 4
BIG = (("w_in", (D, IN_COLS // N_CHIP), 1), ("w_out", (D // N_CHIP, D), 0), ("w_mq", (D // N_CHIP, D), 0),
       ("w_mkv", (D, 2 * D // N_CHIP), 1), ("w_mo", (D // N_CHIP, D), 0), ("w_up", (D, 2 * DFF // N_CHIP), 1),
       ("conv_w", (3, 2 * DFF // N_CHIP), 1), ("w_down", (DFF // N_CHIP, D), 0))
GROUPS = {"a": BIG[:1], "b": BIG[1:]}
ADAM_ROWS = 1536


def _sizes(group):
    return tuple(int(np.prod(s)) for _, s, _ in group)


def _gather_sizes(group):
    return tuple(2 * n if name == "conv_w" else n for (name, _, _), n in zip(group, _sizes(group)))


def _rows_g(group):
    return -(-sum(_gather_sizes(group)) // 4096) * 32


def _rows_f(group):
    return -(-sum(_sizes(group)) // 65536) * 512
SMALL = (("attn_norm_g", 1024), ("b_forget", 8), ("fox_out_g", 512), ("sb_out_g", 512), ("xattn_norm_g", 1024),
         ("mem_norm_g", 1024), ("ffn_norm_g", 1024), ("conv_b", 2 * DFF), ("final_norm_g", 1024))
P_SMALL = sum(n for _, n in SMALL) + 1
ROWS_S = -(-P_SMALL // 1024) * 8


def _params(sem=None, vmem=VMEM_LIMIT):
    return pltpu.CompilerParams(dimension_semantics=sem, vmem_limit_bytes=vmem)


def _tile(n, pref, mult):
    t = (min(pref, n) // mult) * mult
    while t >= mult:
        if n % t == 0:
            return t
        t -= mult
    return n


def _dot(a, b):
    return jnp.dot(a, b, preferred_element_type=F32)


def _dot_nt(a, b):
    return lax.dot_general(a, b, (((1,), (1,)), ((), ())), preferred_element_type=F32)


def _dot_tn(a, b):
    return lax.dot_general(a, b, (((0,), (0,)), ((), ())), preferred_element_type=F32)


def _split3(x):
    h1 = x.astype(CDT)
    r1 = x - h1.astype(F32)
    h2 = r1.astype(CDT)
    h3 = (r1 - h2.astype(F32)).astype(CDT)
    return h1, h2, h3


def _split2(x):
    h1 = x.astype(CDT)
    return h1, (x - h1.astype(F32)).astype(CDT)


def _rms_bwd(dh, x, g):
    r = lax.rsqrt(jnp.mean(x * x, axis=-1, keepdims=True) + EPS)
    xn = x * r
    dg = jnp.sum(dh * xn, axis=0, keepdims=True)
    dhg = dh * g
    dx = r * (dhg - xn * jnp.mean(dhg * xn, axis=-1, keepdims=True))
    return dx, dg


def _mm_nn(a, b, out_dtype, name, *, tm=1024, tn=512, residual=None, halves=False):
    M, K = a.shape
    N = b.shape[1]
    tm = _tile(M, tm, 16)
    tn = _tile(N // 2 if halves else N, tn, 128)
    nj = N // tn

    def body(*refs):
        a_ref, b_ref = refs[0], refs[1]
        o_ref = refs[-1]
        acc = _dot(a_ref[...].astype(CDT), b_ref[...].astype(CDT))
        if residual is not None:
            acc = acc + refs[2][...]
        o_ref[...] = acc.astype(o_ref.dtype)

    in_specs = [pl.BlockSpec((tm, K), lambda i, j: (i, 0)), pl.BlockSpec((K, tn), lambda i, j: (0, j))]
    ops = [a, b]
    if residual is not None:
        in_specs.append(pl.BlockSpec((tm, tn), lambda i, j: (i, j)))
        ops.append(residual)
    if halves:
        njh = nj // 2
        out_shape = jax.ShapeDtypeStruct((2, M, N // 2), out_dtype)
        out_spec = pl.BlockSpec((None, tm, tn), lambda i, j: (j // njh, i, j % njh))
    else:
        out_shape = jax.ShapeDtypeStruct((M, N), out_dtype)
        out_spec = pl.BlockSpec((tm, tn), lambda i, j: (i, j))
    return pl.pallas_call(body, name=name, grid=(M // tm, nj), in_specs=in_specs, out_specs=out_spec,
                          out_shape=out_shape, compiler_params=_params(("parallel", "parallel")))(*ops)


def _mm_tn(a, b, name, *, tka=512, tn=1024, ts=512, b_halves=False):
    S, Ka = a.shape
    N = 2 * b.shape[2] if b_halves else b.shape[1]
    tka = _tile(Ka, tka, 128)
    tn = _tile(N // 2 if b_halves else N, tn, 128)
    ts = _tile(S, ts, 16)
    nn = N // tn

    def body(a_ref, b_ref, o_ref):
        @pl.when(pl.program_id(2) == 0)
        def _():
            o_ref[...] = jnp.zeros_like(o_ref)
        o_ref[...] += _dot_tn(a_ref[...].astype(CDT), b_ref[...].astype(CDT))

    if b_halves:
        nnh = nn // 2
        b_spec = pl.BlockSpec((None, ts, tn), lambda i, j, s: (j // nnh, s, j % nnh))
    else:
        b_spec = pl.BlockSpec((ts, tn), lambda i, j, s: (s, j))
    return pl.pallas_call(
        body, name=name, grid=(Ka // tka, nn, S // ts),
        in_specs=[pl.BlockSpec((ts, tka), lambda i, j, s: (s, i)), b_spec],
        out_specs=pl.BlockSpec((tka, tn), lambda i, j, s: (i, j)),
        out_shape=jax.ShapeDtypeStruct((Ka, N), F32),
        compiler_params=_params(("parallel", "parallel", "arbitrary")))(a, b)


def _mm_nt(a, b, name, *, tm=512, tn=None, tk=None, a_halves=False, out_dtype=F32,
           epilogue=None, extra=(), extra_specs=(), out_shape=None, out_specs=None):
    if a_halves:
        M, K = a.shape[1], 2 * a.shape[2]
    else:
        M, K = a.shape
    N = b.shape[0]
    tm = _tile(M, tm, 16)
    tn = N if (epilogue is not None or tn is None) else _tile(N, tn, 128)
    tk = K if tk is None else _tile(K // 2 if a_halves else K, tk, 128)
    nk = K // tk
    n_extra = len(extra)

    def body(*refs):
        a_ref, b_ref = refs[0], refs[1]
        extra_refs = refs[2:2 + n_extra]
        out_refs = refs[2 + n_extra:-1]
        acc_ref = refs[-1]
        k = pl.program_id(2)

        @pl.when(k == 0)
        def _():
            acc_ref[...] = jnp.zeros_like(acc_ref)
        acc_ref[...] += _dot_nt(a_ref[...].astype(CDT), b_ref[...].astype(CDT))

        @pl.when(k == nk - 1)
        def _():
            if epilogue is None:
                out_refs[0][...] = acc_ref[...].astype(out_refs[0].dtype)
            else:
                epilogue(acc_ref[...], pl.program_id(0), extra_refs, out_refs)

    if a_halves:
        nkh = nk // 2
        a_spec = pl.BlockSpec((None, tm, tk), lambda i, j, k: (k // nkh, i, k % nkh))
    else:
        a_spec = pl.BlockSpec((tm, tk), lambda i, j, k: (i, k))
    if epilogue is None:
        out_shape = jax.ShapeDtypeStruct((M, N), out_dtype)
        out_specs = pl.BlockSpec((tm, tn), lambda i, j, k: (i, j))
        sem = ("parallel", "parallel", "arbitrary")
    else:
        sem = ("arbitrary", "arbitrary", "arbitrary")
    return pl.pallas_call(
        body, name=name, grid=(M // tm, N // tn, nk),
        in_specs=[a_spec, pl.BlockSpec((tn, tk), lambda i, j, k: (j, k)), *extra_specs],
        out_specs=out_specs, out_shape=out_shape,
        scratch_shapes=[pltpu.VMEM((tm, tn), F32)],
        compiler_params=_params(sem))(a, b, *extra)


def _mm_nt_rmsbwd(a, b, x, g, dres, name, *, tm=512, tk=None, a_halves=False):
    M = x.shape[0]
    tm = _tile(M, tm, 16)

    def epilogue(acc, i, extra_refs, out_refs):
        x_ref, g_ref, r_ref = extra_refs
        dx_ref, dg_ref = out_refs
        dx, dg = _rms_bwd(acc, x_ref[...], g_ref[...])
        dx_ref[...] = r_ref[...] + dx

        @pl.when(i == 0)
        def _():
            dg_ref[...] = jnp.zeros_like(dg_ref)
        dg_ref[...] += dg

    row = pl.BlockSpec((tm, D), lambda i, j, k: (i, 0))
    vec = pl.BlockSpec((1, D), lambda i, j, k: (0, 0))
    return _mm_nt(a, b, name, tm=tm, tk=tk, a_halves=a_halves, epilogue=epilogue,
                  extra=(x, g, dres), extra_specs=(row, vec, row),
                  out_shape=(jax.ShapeDtypeStruct((M, D), F32), jax.ShapeDtypeStruct((1, D), F32)),
                  out_specs=(row, vec))


def _rms_cast(x, g, name, *, tm=512):
    M, W = x.shape
    tm = _tile(M, tm, 16)

    def body(x_ref, g_ref, o_ref):
        xf = x_ref[...]
        r = lax.rsqrt(jnp.mean(xf * xf, axis=-1, keepdims=True) + EPS)
        o_ref[...] = (xf * r * g_ref[...]).astype(o_ref.dtype)

    return pl.pallas_call(body, name=name, grid=(M // tm,),
                          in_specs=[pl.BlockSpec((tm, W), lambda i: (i, 0)), pl.BlockSpec((1, W), lambda i: (0, 0))],
                          out_specs=pl.BlockSpec((tm, W), lambda i: (i, 0)),
                          out_shape=jax.ShapeDtypeStruct((M, W), CDT),
                          compiler_params=_params(("parallel",)))(x, g)


def _tri(n, lower):
    r = lax.broadcasted_iota(jnp.int32, (n, n), 0)
    c = lax.broadcasted_iota(jnp.int32, (n, n), 1)
    return (c <= r if lower else c >= r).astype(CDT)


def _gate_fwd(fl, b, name, *, tm=512):
    S = fl.shape[0]
    tm = _tile(S, tm, 16)

    def body(f_ref, b_ref, c_ref, carry):
        @pl.when(pl.program_id(0) == 0)
        def _():
            carry[...] = jnp.zeros_like(carry)
        z = f_ref[...] + b_ref[...]
        lf = jnp.minimum(z, 0.0) - jnp.log(1.0 + jnp.exp(-jnp.abs(z)))
        tri = _tri(tm, True)
        cum = sum(_dot(tri, p) for p in _split3(lf)) + carry[...]
        c_ref[...] = cum
        carry[...] = cum[tm - 1:tm, :]

    return pl.pallas_call(body, name=name, grid=(S // tm,),
                          in_specs=[pl.BlockSpec((tm, 128), lambda i: (i, 0)), pl.BlockSpec((1, 128), lambda i: (0, 0))],
                          out_specs=pl.BlockSpec((tm, 128), lambda i: (i, 0)),
                          out_shape=jax.ShapeDtypeStruct((S, 128), F32),
                          scratch_shapes=[pltpu.VMEM((1, 128), F32)],
                          compiler_params=_params(("arbitrary",)))(fl, b)


def _gate_bwd(dck, dcq, fl, b, name, *, tm=512):
    S = fl.shape[0]
    tm = _tile(S, tm, 16)
    nb = S // tm

    def body(dck_ref, dcq_ref, f_ref, b_ref, df_ref, db_ref, carry):
        @pl.when(pl.program_id(0) == 0)
        def _():
            carry[...] = jnp.zeros_like(carry)
            db_ref[...] = jnp.zeros_like(db_ref)
        lane = lax.broadcasted_iota(jnp.int32, (1, 128), 1)
        dc = dck_ref[...]
        for h in range(NH):
            dc = dc + dcq_ref[h] * (lane == h).astype(F32)
        tri = _tri(tm, False)
        suf = sum(_dot(tri, p) for p in _split3(dc)) + carry[...]
        carry[...] = suf[0:1, :]
        df = suf * jax.nn.sigmoid(-(f_ref[...] + b_ref[...]))
        df_ref[...] = df
        db_ref[...] += jnp.sum(df, axis=0, keepdims=True)

    rev = pl.BlockSpec((tm, 128), lambda i: (nb - 1 - i, 0))
    cols = pl.BlockSpec((NH, tm, 1), lambda i: (0, nb - 1 - i, 0))
    vec = pl.BlockSpec((1, 128), lambda i: (0, 0))
    return pl.pallas_call(body, name=name, grid=(nb,), in_specs=[rev, cols, rev, vec], out_specs=(rev, vec),
                          out_shape=(jax.ShapeDtypeStruct((S, 128), F32), jax.ShapeDtypeStruct((1, 128), F32)),
                          scratch_shapes=[pltpu.VMEM((1, 128), F32)],
                          compiler_params=_params(("arbitrary",)))(dck, dcq, fl, b)


def _group_rows(o_ref):
    return jnp.concatenate([o_ref[h] for h in range(NH)], axis=1)


def _out_proj(fo, so, gf, gs, w_out, x0, name, *, tm=512):
    S = fo.shape[1]
    tm = _tile(S, tm, 16)

    def body(fo_ref, so_ref, gf_ref, gs_ref, w_ref, x_ref, x1_ref, mx_ref):
        for ref, g_ref, lo in ((fo_ref, gf_ref, 0), (so_ref, gs_ref, GW)):
            o = _group_rows(ref)
            r = lax.rsqrt(jnp.mean(o * o, axis=-1, keepdims=True) + EPS)
            mx_ref[:, lo:lo + GW] = (o * r * g_ref[...]).astype(CDT)
        x1_ref[...] = x_ref[...] + _dot(mx_ref[...], w_ref[...])

    half = pl.BlockSpec((NH, tm, HD), lambda i: (0, i, 0))
    gvec = pl.BlockSpec((1, GW), lambda i: (0, 0))
    row = pl.BlockSpec((tm, D), lambda i: (i, 0))
    return pl.pallas_call(body, name=name, grid=(S // tm,),
                          in_specs=[half, half, gvec, gvec, pl.BlockSpec((D, D), lambda i: (0, 0)), row],
                          out_specs=(row, row),
                          out_shape=(jax.ShapeDtypeStruct((S, D), F32), jax.ShapeDtypeStruct((S, D), CDT)),
                          compiler_params=_params(("parallel",)))(fo, so, gf, gs, w_out, x0)


def _out_proj_bwd(dx1, w_out, fo, so, gf, gs, name, *, tm=512):
    S = fo.shape[1]
    tm = _tile(S, tm, 16)

    def epilogue(acc, i, extra_refs, out_refs):
        fo_ref, so_ref, gf_ref, gs_ref = extra_refs
        dfo_ref, dso_ref, dgf_ref, dgs_ref = out_refs

        @pl.when(i == 0)
        def _():
            dgf_ref[...] = jnp.zeros_like(dgf_ref)
            dgs_ref[...] = jnp.zeros_like(dgs_ref)
        for lo, o_ref, g_ref, do_ref, dg_ref in ((0, fo_ref, gf_ref, dfo_ref, dgf_ref), (GW, so_ref, gs_ref, dso_ref, dgs_ref)):
            dx, dg = _rms_bwd(acc[:, lo:lo + GW], _group_rows(o_ref), g_ref[...])
            for h in range(NH):
                do_ref[h] = dx[:, h * HD:(h + 1) * HD].astype(do_ref.dtype)
            dg_ref[...] += dg

    half = pl.BlockSpec((NH, tm, HD), lambda i, j, k: (0, i, 0))
    gvec = pl.BlockSpec((1, GW), lambda i, j, k: (0, 0))
    return _mm_nt(dx1, w_out, name, tm=tm, epilogue=epilogue, extra=(fo, so, gf, gs),
                  extra_specs=(half, half, gvec, gvec),
                  out_shape=(jax.ShapeDtypeStruct((NH, S, HD), CDT), jax.ShapeDtypeStruct((NH, S, HD), CDT),
                             jax.ShapeDtypeStruct((1, GW), F32), jax.ShapeDtypeStruct((1, GW), F32)),
                  out_specs=(half, half, gvec, gvec))


def _loss_bwd(x3, tgt, g, name, *, tm=512):
    S = x3.shape[0]
    tm = _tile(S, tm, 16)

    def body(x_ref, t_ref, g_ref, dx_ref, loss_ref, dg_ref):
        @pl.when(pl.program_id(0) == 0)
        def _():
            loss_ref[...] = jnp.zeros_like(loss_ref)
            dg_ref[...] = jnp.zeros_like(dg_ref)
        x = x_ref[...]
        gv = g_ref[...]
        r = lax.rsqrt(jnp.mean(x * x, axis=-1, keepdims=True) + EPS)
        xn = x * r
        err = xn * gv - t_ref[...]
        loss_ref[...] += jnp.full(loss_ref.shape, 0.5 * jnp.sum(jnp.mean(err * err, axis=-1, keepdims=True)), F32)
        dy = err * (1.0 / D)
        dg_ref[...] += jnp.sum(dy * xn, axis=0, keepdims=True)
        dyg = dy * gv
        dx_ref[...] = r * (dyg - xn * jnp.mean(dyg * xn, axis=-1, keepdims=True))

    row = pl.BlockSpec((tm, D), lambda i: (i, 0))
    vec = pl.BlockSpec((1, D), lambda i: (0, 0))
    dx3, loss, dg = pl.pallas_call(
        body, name=name, grid=(S // tm,), in_specs=[row, row, vec],
        out_specs=(row, pl.BlockSpec((1, 128), lambda i: (0, 0)), vec),
        out_shape=(jax.ShapeDtypeStruct((S, D), F32), jax.ShapeDtypeStruct((1, 128), F32), jax.ShapeDtypeStruct((1, D), F32)),
        compiler_params=_params(("arbitrary",)))(x3, tgt, g)
    return loss, dx3, dg


MASKED, FIRST, LAST = 1, 2, 4


def _att_tiles(name, S):
    tq, tk = ATT_TILES[name]
    return min(tq, S), min(tk, S)


def _pairs(S, tq, tk, descending=True):
    assert tk % tq == 0 and S % tk == 0
    qi, kj, fl = [], [], []
    for i in range(S // tq):
        last = ((i + 1) * tq - 1) // tk
        order = list(range(last, -1, -1) if descending else range(last + 1))
        for pos, kb in enumerate(order):
            qi.append(i)
            kj.append(kb)
            fl.append((MASKED if (kb + 1) * tk - 1 > i * tq else 0) | (FIRST if pos == 0 else 0) | (LAST if pos == last else 0))
    return tuple(jnp.asarray(np.asarray(a, np.int32)) for a in (qi, kj, fl))


def _head_blk(rows, by_key, head0, width=HD):
    if by_key:
        return pl.BlockSpec((1, rows, width), lambda h, n, qi, kj, fl: (h + head0, kj[n], 0))
    return pl.BlockSpec((1, rows, width), lambda h, n, qi, kj, fl: (h + head0, qi[n], 0))


def _att_specs(tq, tk, width=HD):
    qblk = pl.BlockSpec((1, tq, width), lambda h, n, qi, kj, fl: (h, qi[n], 0))
    kblk = pl.BlockSpec((1, tk, width), lambda h, n, qi, kj, fl: (h, kj[n], 0))
    qcol = pl.BlockSpec((1, tq, 1), lambda h, n, qi, kj, fl: (h, qi[n], 0))
    krow = pl.BlockSpec((1, 1, tk), lambda h, n, qi, kj, fl: (h, 0, kj[n]))
    return qblk, kblk, qcol, krow


def _causal(tq, w, ahead, strict):
    diff = lax.broadcasted_iota(jnp.int32, (tq, w), 1) - lax.broadcasted_iota(jnp.int32, (tq, w), 0)
    return diff < ahead if strict else diff <= ahead


def _masked_or_not(flags, step):
    pl.when(flags % 2 == 1)(functools.partial(step, True))
    pl.when(flags % 2 == 0)(functools.partial(step, False))


def _fox_fwd(q, k, v, cq, ck, name, slots=None, h0=(0, 0, 0)):
    S = q.shape[1]
    tq, tk = _att_tiles("fox_fwd", S)
    qi, kj, fl = _pairs(S, tq, tk)
    qblk, kblk, qcol, krow = _att_specs(tq, tk)
    npairs = int(qi.shape[0])

    def body(qi_ref, kj_ref, fl_ref, q_ref, k_ref, v_ref, cq_ref, ck_ref, *rest):
        if slots is None:
            o_ref, lse_ref, m_s, l_s, acc_s = rest
        else:
            _, o_ref, lse_ref, slots_ref, m_s, l_s, acc_s, send_sems, recv_sems = rest
        h, n = pl.program_id(0), pl.program_id(1)
        i, kb, flags = qi_ref[n], kj_ref[n], fl_ref[n]
        if slots is not None:
            _gather_steps(slots_ref, send_sems, recv_sems, first=(h == 0) & (n == 0), middle=(h == NH // 2) & (n == 0),
                          last=(h == NH - 1) & (n == npairs - 1))

        @pl.when(flags & FIRST != 0)
        def _():
            m_s[...] = jnp.full_like(m_s, NEG)
            l_s[...] = jnp.zeros_like(l_s)
            acc_s[...] = jnp.zeros_like(acc_s)

        def step(masked):
            s = _dot_nt(q_ref[0] * ATT_SCALE, k_ref[0]) + cq_ref[0] - ck_ref[0]
            if masked:
                s = jnp.where(_causal(tq, tk, i * tq - kb * tk, False), s, NEG)
            m_new = jnp.maximum(m_s[...], jnp.max(s, axis=-1, keepdims=True))
            alpha = jnp.exp(m_s[...] - m_new)
            p = jnp.exp(s - m_new)
            l_s[...] = alpha * l_s[...] + jnp.sum(p, axis=-1, keepdims=True)
            acc_s[...] = alpha * acc_s[...] + _dot(p.astype(CDT), v_ref[0])
            m_s[...] = m_new

        _masked_or_not(flags, step)

        @pl.when(flags & LAST != 0)
        def _():
            o_ref[0] = acc_s[...] / l_s[...]
            lse_ref[0] = m_s[...] + jnp.log(l_s[...])

    scratch = [pltpu.VMEM((tq, 1), F32), pltpu.VMEM((tq, 1), F32), pltpu.VMEM((tq, HD), F32)]
    out_shape = (jax.ShapeDtypeStruct((NH, S, HD), F32), jax.ShapeDtypeStruct((NH, S, 1), F32))
    qkv_specs = (_head_blk(tq, False, h0[0]), _head_blk(tk, True, h0[1]), _head_blk(tk, True, h0[2]))
    if slots is None:
        grid_spec = pltpu.PrefetchScalarGridSpec(num_scalar_prefetch=3, grid=(NH, npairs), in_specs=[*qkv_specs, qcol, krow],
                                                 out_specs=(qblk, qcol), scratch_shapes=scratch)
        o, lse = pl.pallas_call(body, name=name, grid_spec=grid_spec, out_shape=out_shape,
                                compiler_params=_params(("parallel", "arbitrary")))(qi, kj, fl, q, k, v, cq, ck)
        return o, lse, None
    grid_spec = pltpu.PrefetchScalarGridSpec(num_scalar_prefetch=3, grid=(NH, npairs), in_specs=[*qkv_specs, qcol, krow, ANY],
                                             out_specs=(qblk, qcol, ANY), scratch_shapes=scratch + list(GATHER_SEMS))
    return pl.pallas_call(body, name=name, grid_spec=grid_spec, out_shape=(*out_shape, jax.ShapeDtypeStruct(slots.shape, slots.dtype)),
                          input_output_aliases={8: 2},
                          compiler_params=_params(("arbitrary", "arbitrary")))(qi, kj, fl, q, k, v, cq, ck, slots)


def _fox_bwd(q, k, v, cq, ck, o, do, lse, name, h0=(0, 0, 0), swap=None):
    S = q.shape[1]
    tq, tk = _att_tiles("fox_bwd", S)
    qi, kj, fl = _pairs(S, tq, tk)
    qblk, kblk, qcol, krow = _att_specs(tq, tk)
    npairs = int(qi.shape[0])

    def body(qi_ref, kj_ref, fl_ref, q_ref, k_ref, v_ref, cq_ref, ck_ref, o_ref, do_ref, lse_ref, *rest):
        if swap is None:
            dq_ref, dk_ref, dv_ref, dck_ref, dcq_ref, dq_s, dl_s, dcq_s = rest
        else:
            g_ref, dq_ref, dk_ref, dv_ref, dck_ref, dcq_ref, got_ref, dq_s, dl_s, dcq_s, send_sem, recv_sem = rest
        n = pl.program_id(1)
        i, kb, flags = qi_ref[n], kj_ref[n], fl_ref[n]
        if swap is not None:
            h = pl.program_id(0)
            _swap_steps(g_ref, got_ref, send_sem, recv_sem, first=(h == 0) & (n == 0), last=(h == NH - 1) & (n == npairs - 1))

        @pl.when(n == 0)
        def _():
            dk_ref[...] = jnp.zeros_like(dk_ref)
            dv_ref[...] = jnp.zeros_like(dv_ref)
            dck_ref[...] = jnp.zeros_like(dck_ref)

        @pl.when(flags & FIRST != 0)
        def _():
            dq_s[...] = jnp.zeros_like(dq_s)
            dcq_s[...] = jnp.zeros_like(dcq_s)
            dl_s[...] = jnp.sum(do_ref[0].astype(F32) * o_ref[0], axis=-1, keepdims=True)

        def step(masked):
            qs = q_ref[0] * ATT_SCALE
            do = do_ref[0]
            p = jnp.exp(_dot_nt(qs, k_ref[0]) + cq_ref[0] - ck_ref[0] - lse_ref[0])
            if masked:
                p = jnp.where(_causal(tq, tk, i * tq - kb * tk, False), p, 0.0)
            ds = p * (_dot_nt(do, v_ref[0]) - dl_s[...])
            dsb = ds.astype(CDT)
            dq_s[...] += _dot(dsb, k_ref[0])
            rows = pl.ds(pl.multiple_of(kb * tk, tk), tk)
            dk_ref[0, rows, :] += _dot_tn(dsb, qs)
            dv_ref[0, rows, :] += _dot_tn(p.astype(CDT), do)
            dck_ref[0, :, rows] += -jnp.sum(ds, axis=0, keepdims=True)
            dcq_s[...] += jnp.sum(ds, axis=-1, keepdims=True)

        _masked_or_not(flags, step)

        @pl.when(flags & LAST != 0)
        def _():
            dq_ref[0] = (dq_s[...] * ATT_SCALE).astype(dq_ref.dtype)
            dcq_ref[0] = dcq_s[...]

    whole = pl.BlockSpec((1, S, HD), lambda h, n, qi, kj, fl: (h, 0, 0))
    in_specs = [_head_blk(tq, False, h0[0]), _head_blk(tk, True, h0[1]), _head_blk(tk, True, h0[2]), qcol, krow, qblk, qblk, qcol]
    out_specs = (qblk, whole, whole, pl.BlockSpec((1, 1, S), lambda h, n, qi, kj, fl: (h, 0, 0)), qcol)
    out_shape = (jax.ShapeDtypeStruct((NH, S, HD), CDT), jax.ShapeDtypeStruct((NH, S, HD), F32), jax.ShapeDtypeStruct((NH, S, HD), F32),
                 jax.ShapeDtypeStruct((NH, 1, S), F32), jax.ShapeDtypeStruct((NH, S, 1), F32))
    scratch = [pltpu.VMEM((tq, HD), F32), pltpu.VMEM((tq, 1), F32), pltpu.VMEM((tq, 1), F32)]
    if swap is None:
        grid_spec = pltpu.PrefetchScalarGridSpec(num_scalar_prefetch=3, grid=(NH, npairs), in_specs=in_specs,
                                                 out_specs=out_specs, scratch_shapes=scratch)
        return (*pl.pallas_call(body, name=name, grid_spec=grid_spec, out_shape=out_shape,
                                compiler_params=_params(("parallel", "arbitrary")))(qi, kj, fl, q, k, v, cq, ck, o, do, lse), None)
    grid_spec = pltpu.PrefetchScalarGridSpec(num_scalar_prefetch=3, grid=(NH, npairs), in_specs=in_specs + [ANY],
                                             out_specs=(*out_specs, ANY), scratch_shapes=scratch + list(SWAP_SEMS))
    got_shape = jax.ShapeDtypeStruct((swap.shape[0], swap.shape[1] // 2, 128), swap.dtype)
    return pl.pallas_call(body, name=name, grid_spec=grid_spec, out_shape=(*out_shape, got_shape),
                          compiler_params=_params(("arbitrary", "arbitrary")))(qi, kj, fl, q, k, v, cq, ck, o, do, lse, swap)


LOG2E = 1.4426950408889634


def _proj_qkv(h1, w_qkv, name, *, tm=1024):
    S, K = h1.shape
    tm = _tile(S, tm, 16)
    SQ, SK = 3, 4

    def heads(t):
        return [t[:, h * HD:(h + 1) * HD] for h in range(NH)]

    def body(a_ref, b_ref, o_ref, q2_ref, k2_ref):
        j = pl.program_id(1)
        ob = _dot(a_ref[...], b_ref[...]).astype(CDT)
        for h, t in enumerate(heads(ob)):
            o_ref[h] = t

        @pl.when(j == SQ)
        def _():
            qf = ob.astype(F32) * (ATT_SCALE * LOG2E)
            hi = qf.astype(CDT)
            lo = (qf - hi.astype(F32)).astype(CDT)
            for h, (th, tl) in enumerate(zip(heads(hi), heads(lo))):
                q2_ref[h] = jnp.concatenate([th, tl], axis=1)

        @pl.when(j == SK)
        def _():
            for h, t in enumerate(heads(ob)):
                k2_ref[h] = jnp.concatenate([t, t], axis=1)

    wide = pl.BlockSpec((NH, tm, 2 * HD), lambda i, j: (0, i, 0))
    return pl.pallas_call(
        body, name=name, grid=(S // tm, 6),
        in_specs=[pl.BlockSpec((tm, K), lambda i, j: (i, 0)), pl.BlockSpec((K, GW), lambda i, j: (0, j))],
        out_specs=(pl.BlockSpec((NH, tm, HD), lambda i, j: (j, i, 0)), wide, wide),
        out_shape=(jax.ShapeDtypeStruct((6 * NH, S, HD), CDT), jax.ShapeDtypeStruct((NH, S, 2 * HD), CDT),
                   jax.ShapeDtypeStruct((NH, S, 2 * HD), CDT)),
        compiler_params=_params(("parallel", "arbitrary")))(h1, w_qkv)


def _sb_softplus2(q2, k2sub, mask):
    z2 = _dot_nt(q2, k2sub)
    sp2 = jnp.maximum(z2, 0.0) + jnp.log2(1.0 + jnp.exp2(-jnp.abs(z2)))
    return z2, sp2 if mask is None else jnp.where(mask, sp2, 0.0)


def _strict_tri(n, upper, value):
    r = lax.broadcasted_iota(jnp.int32, (n, n), 0)
    c = lax.broadcasted_iota(jnp.int32, (n, n), 1)
    return jnp.where(r < c if upper else r > c, value, 0.0).astype(CDT)


def _sb_fwd(q2, k2, v, name, v0=0):
    S = q2.shape[1]
    tq, tk = _att_tiles("sb_fwd", S)
    W = min(W_SB, tk)
    qi, kj, fl = _pairs(S, tq, tk)
    qblk, kblk, qcol, _ = _att_specs(tq, tk)
    q2blk, k2blk, _, _ = _att_specs(tq, tk, 2 * HD)

    def body(qi_ref, kj_ref, fl_ref, q_ref, k_ref, v_ref, o_ref, lt_ref, run_s, acc_s):
        n = pl.program_id(1)
        i, kb, flags = qi_ref[n], kj_ref[n], fl_ref[n]

        @pl.when(flags & FIRST != 0)
        def _():
            run_s[...] = jnp.zeros_like(run_s)
            acc_s[...] = jnp.zeros_like(acc_s)

        def step(masked):
            neg_later = _strict_tri(W, False, -1.0)
            run = run_s[...]
            acc = acc_s[...]
            for sub in range(tk // W - 1, -1, -1):
                cols = slice(sub * W, (sub + 1) * W)
                mask = _causal(tq, W, i * tq - kb * tk - sub * W, True) if masked else None
                z2, sp2 = _sb_softplus2(q_ref[0], k_ref[0, cols, :], mask)
                excl = _dot(sp2.astype(CDT), neg_later)
                a = jnp.exp2((z2 - sp2) + (excl + run))
                if masked:
                    a = jnp.where(mask, a, 0.0)
                acc = acc + _dot(a.astype(CDT), v_ref[0, cols, :])
                run = run + (excl[:, 0:1] - sp2[:, 0:1])
            run_s[...] = run
            acc_s[...] = acc

        _masked_or_not(flags, step)

        @pl.when(flags & LAST != 0)
        def _():
            o_ref[0] = acc_s[...]
            lt_ref[0] = run_s[...]

    grid_spec = pltpu.PrefetchScalarGridSpec(
        num_scalar_prefetch=3, grid=(NH, int(qi.shape[0])), in_specs=[q2blk, k2blk, _head_blk(tk, True, v0)], out_specs=(qblk, qcol),
        scratch_shapes=[pltpu.VMEM((tq, 1), F32), pltpu.VMEM((tq, HD), F32)])
    return pl.pallas_call(body, name=name, grid_spec=grid_spec,
                          out_shape=(jax.ShapeDtypeStruct((NH, S, HD), F32), jax.ShapeDtypeStruct((NH, S, 1), F32)),
                          compiler_params=_params(("parallel", "arbitrary")))(qi, kj, fl, q2, k2, v)


def _sb_bwd(q, q2, k2, v, do, lt, name, scatter=None, q0=0, v0=0):
    S = q.shape[1]
    tq, tk = _att_tiles("sb_bwd", S)
    W = min(W_SB, tk)
    qi, kj, fl = _pairs(S, tq, tk, descending=False)
    qblk, kblk, qcol, _ = _att_specs(tq, tk)
    q2blk, k2blk, _, _ = _att_specs(tq, tk, 2 * HD)
    npairs = int(qi.shape[0])

    def body(qi_ref, kj_ref, fl_ref, q_ref, q2_ref, k2_ref, v_ref, do_ref, lt_ref, *rest):
        if scatter is None:
            dq_ref, dk_ref, dv_ref, passed_s, gsum_s, dq_s = rest
        else:
            h_ref, dq_ref, dk_ref, dv_ref, recv_ref, passed_s, gsum_s, dq_s, send_sems, recv_sems = rest
        n = pl.program_id(1)
        i, kb, flags = qi_ref[n], kj_ref[n], fl_ref[n]
        if scatter is not None:
            h = pl.program_id(0)
            _scatter_steps(h_ref, recv_ref, send_sems, recv_sems, first=(h == 0) & (n == 0), last=(h == NH - 1) & (n == npairs - 1))

        @pl.when(n == 0)
        def _():
            dk_ref[...] = jnp.zeros_like(dk_ref)
            dv_ref[...] = jnp.zeros_like(dv_ref)

        @pl.when(flags & FIRST != 0)
        def _():
            passed_s[...] = jnp.zeros_like(passed_s)
            gsum_s[...] = jnp.zeros_like(gsum_s)
            dq_s[...] = jnp.zeros_like(dq_s)

        def step(masked):
            qs = q_ref[0] * ATT_SCALE
            do = do_ref[0]
            neg_later = _strict_tri(W, False, -1.0)
            earlier = _strict_tri(W, True, 1.0)
            for sub in range(tk // W):
                cols = slice(sub * W, (sub + 1) * W)
                mask = _causal(tq, W, i * tq - kb * tk - sub * W, True) if masked else None
                ksub = k2_ref[0, cols, 0:HD]
                z2, sp2 = _sb_softplus2(q2_ref[0], k2_ref[0, cols, :], mask)
                excl = _dot(sp2.astype(CDT), neg_later)
                through = passed_s[...] + (excl[:, 0:1] - sp2[:, 0:1])
                t1 = z2 - sp2
                sig = jnp.exp2(t1)
                a = jnp.exp2(t1 + (excl + (lt_ref[0] - through)))
                if masked:
                    a = jnp.where(mask, a, 0.0)
                dl = _dot_nt(do, v_ref[0, cols, :]) * a
                before = _dot(dl.astype(CDT), earlier)
                dz = dl - sig * (dl + (before + gsum_s[...]))
                if masked:
                    dz = jnp.where(mask, dz, 0.0)
                dzb = dz.astype(CDT)
                dq_s[...] += _dot(dzb, ksub)
                rows = pl.ds(pl.multiple_of(kb * tk + sub * W, W), W)
                dk_ref[0, rows, :] += _dot_tn(dzb, qs)
                dv_ref[0, rows, :] += _dot_tn(a.astype(CDT), do)
                passed_s[...] = through
                gsum_s[...] += before[:, W - 1:W] + dl[:, W - 1:W]

        _masked_or_not(flags, step)

        @pl.when(flags & LAST != 0)
        def _():
            dq_ref[0] = (dq_s[...] * ATT_SCALE).astype(dq_ref.dtype)

    whole = pl.BlockSpec((1, S, HD), lambda h, n, qi, kj, fl: (h, 0, 0))
    in_specs = [_head_blk(tq, False, q0), q2blk, k2blk, _head_blk(tk, True, v0), qblk, qcol]
    out_specs = (qblk, whole, whole)
    out_shape = (jax.ShapeDtypeStruct((NH, S, HD), CDT), jax.ShapeDtypeStruct((NH, S, HD), F32), jax.ShapeDtypeStruct((NH, S, HD), F32))
    scratch = [pltpu.VMEM((tq, 1), F32), pltpu.VMEM((tq, 1), F32), pltpu.VMEM((tq, HD), F32)]
    if scatter is None:
        grid_spec = pltpu.PrefetchScalarGridSpec(num_scalar_prefetch=3, grid=(NH, npairs), in_specs=in_specs,
                                                 out_specs=out_specs, scratch_shapes=scratch)
        return (*pl.pallas_call(body, name=name, grid_spec=grid_spec, out_shape=out_shape,
                                compiler_params=_params(("parallel", "arbitrary")))(qi, kj, fl, q, q2, k2, v, do, lt), None)
    grid_spec = pltpu.PrefetchScalarGridSpec(num_scalar_prefetch=3, grid=(NH, npairs), in_specs=in_specs + [ANY],
                                             out_specs=(*out_specs, ANY), scratch_shapes=scratch + list(SCATTER_SEMS))
    recv_shape = jax.ShapeDtypeStruct((3,) + scatter.shape[1:], scatter.dtype)
    return pl.pallas_call(body, name=name, grid_spec=grid_spec, out_shape=(*out_shape, recv_shape),
                          compiler_params=_params(("arbitrary", "arbitrary")))(qi, kj, fl, q, q2, k2, v, do, lt, scatter)


def _mem_probs(q_ref, kv_ref, h):
    cols = slice(h * MHD, (h + 1) * MHD)
    s = _dot_nt(q_ref[:, cols], kv_ref[:, cols]) * MEM_SCALE
    e = jnp.exp(s - jnp.max(s, axis=-1, keepdims=True))
    return e / jnp.sum(e, axis=-1, keepdims=True)


def _xattn_fwd(q, kv, w_mo, x1, name, *, tm=512):
    S = q.shape[0]
    tm = _tile(S, tm, 16)
    nm = kv.shape[0]

    def body(q_ref, kv_ref, w_ref, x_ref, x2_ref, o_ref):
        for h in range(NMH):
            p = _mem_probs(q_ref, kv_ref, h)
            o_ref[:, h * MHD:(h + 1) * MHD] = _dot(p.astype(CDT), kv_ref[:, D + h * MHD:D + (h + 1) * MHD]).astype(CDT)
        x2_ref[...] = x_ref[...] + _dot(o_ref[...], w_ref[...])

    row = pl.BlockSpec((tm, D), lambda i: (i, 0))
    return pl.pallas_call(body, name=name, grid=(S // tm,),
                          in_specs=[row, pl.BlockSpec((nm, 2 * D), lambda i: (0, 0)), pl.BlockSpec((D, D), lambda i: (0, 0)), row],
                          out_specs=(row, row),
                          out_shape=(jax.ShapeDtypeStruct((S, D), F32), jax.ShapeDtypeStruct((S, D), CDT)),
                          compiler_params=_params(("parallel",)))(q, kv, w_mo, x1)


def _xattn_bwd(q, kv, do, name, *, tm=512):
    S = q.shape[0]
    tm = _tile(S, tm, 16)
    nm = kv.shape[0]

    def body(q_ref, kv_ref, do_ref, dq_ref, dkv_ref):
        @pl.when(pl.program_id(0) == 0)
        def _():
            dkv_ref[...] = jnp.zeros_like(dkv_ref)
        for h in range(NMH):
            cols = slice(h * MHD, (h + 1) * MHD)
            vcols = slice(D + h * MHD, D + (h + 1) * MHD)
            p = _mem_probs(q_ref, kv_ref, h)
            doh = do_ref[:, cols]
            dp = _dot_nt(doh, kv_ref[:, vcols])
            ds = (p * (dp - jnp.sum(p * dp, axis=-1, keepdims=True)) * MEM_SCALE).astype(CDT)
            dq_ref[:, cols] = _dot(ds, kv_ref[:, cols]).astype(CDT)
            dkv_ref[:, cols] += _dot_tn(ds, q_ref[:, cols])
            dkv_ref[:, vcols] += _dot_tn(p.astype(CDT), doh)

    row = pl.BlockSpec((tm, D), lambda i: (i, 0))
    kvs = pl.BlockSpec((nm, 2 * D), lambda i: (0, 0))
    return pl.pallas_call(body, name=name, grid=(S // tm,), in_specs=[row, kvs, row], out_specs=(row, kvs),
                          out_shape=(jax.ShapeDtypeStruct((S, D), CDT), jax.ShapeDtypeStruct((nm, 2 * D), F32)),
                          compiler_params=_params(("arbitrary",)))(q, kv, do)


HALO = 16
SLAB = 8


def _shift_down(u, prev, s):
    rolled = pltpu.roll(u, s, 0)
    top = rolled[0:SLAB]
    r = lax.broadcasted_iota(jnp.int32, top.shape, 0)
    for t in range(s):
        top = jnp.where(r == t, prev[HALO - s + t:HALO - s + t + 1, :], top)
    return jnp.concatenate([top, rolled[SLAB:]], axis=0)


def _shift_up(u, nxt, s):
    n = u.shape[0]
    rolled = pltpu.roll(u, n - s, 0)
    bottom = rolled[n - SLAB:]
    r = lax.broadcasted_iota(jnp.int32, bottom.shape, 0)
    for t in range(s):
        bottom = jnp.where(r == SLAB - s + t, nxt[t:t + 1, :], bottom)
    return jnp.concatenate([rolled[:n - SLAB], bottom], axis=0)


def _conv_taps(u_ref, h_ref, first):
    u = u_ref[...].astype(F32)
    prev = jnp.where(first, 0.0, h_ref[...].astype(F32))
    out = []
    for half in range(2):
        out.append((u[half], _shift_down(u[half], prev[half], 1), _shift_down(u[half], prev[half], 2)))
    return out


def _conv_specs(tm, tn, nsb):
    blk = pl.BlockSpec((2, tm, tn), lambda j, i: (0, i, j))
    prev = pl.BlockSpec((2, HALO, tn), lambda j, i: (0, jnp.maximum(i * (tm // HALO) - 1, 0), j))
    nxt = pl.BlockSpec((2, HALO, tn), lambda j, i: (0, jnp.minimum((i + 1) * (tm // HALO), nsb - 1), j))
    w = pl.BlockSpec((2, 3, tn), lambda j, i: (0, 0, j))
    b = pl.BlockSpec((2, 1, tn), lambda j, i: (0, 0, j))
    return blk, prev, nxt, w, b


def _conv_apply(taps, w_ref, b_ref):
    ys = []
    for half in range(2):
        u, u1, u2 = taps[half]
        w = w_ref[half]
        ys.append(b_ref[half] + u2 * w[0:1, :] + u1 * w[1:2, :] + u * w[2:3, :])
    return ys


def _conv_act(u0, cw, cb, name, *, tm=2048, tn=256):
    _, S, F = u0.shape
    tm = _tile(S, tm, HALO)
    tn = _tile(F, tn, 128)
    blk, prev, _, w, b = _conv_specs(tm, tn, S // HALO)

    def body(u_ref, h_ref, w_ref, b_ref, a_ref):
        yg, yv = _conv_apply(_conv_taps(u_ref, h_ref, pl.program_id(1) == 0), w_ref, b_ref)
        a_ref[...] = (yg * jax.nn.sigmoid(yg) * yv).astype(a_ref.dtype)

    return pl.pallas_call(body, name=name, grid=(F // tn, S // tm), in_specs=[blk, prev, w, b],
                          out_specs=pl.BlockSpec((tm, tn), lambda j, i: (i, j)),
                          out_shape=jax.ShapeDtypeStruct((S, F), CDT),
                          compiler_params=_params(("parallel", "parallel")))(u0, u0, cw, cb)


def _conv_act_bwd(u0, da, cw, cb, name, *, tm=2048, tn=256):
    _, S, F = u0.shape
    tm = _tile(S, tm, HALO)
    tn = _tile(F, tn, 128)
    blk, prev, _, w, b = _conv_specs(tm, tn, S // HALO)

    def body(u_ref, h_ref, da_ref, w_ref, b_ref, du_ref, dwb_ref):
        @pl.when(pl.program_id(1) == 0)
        def _():
            dwb_ref[...] = jnp.zeros_like(dwb_ref)
        taps = _conv_taps(u_ref, h_ref, pl.program_id(1) == 0)
        yg, yv = _conv_apply(taps, w_ref, b_ref)
        sg = jax.nn.sigmoid(yg)
        da = da_ref[...].astype(F32)
        dus = (da * yv * sg * (1.0 + yg * (1.0 - sg)), da * yg * sg)
        for half in range(2):
            du = dus[half]
            du_ref[half] = du.astype(du_ref.dtype)
            u, u1, u2 = taps[half]
            for row, term in enumerate((du * u2, du * u1, du * u, du)):
                dwb_ref[half, row:row + 1, :] += jnp.sum(term, axis=0, keepdims=True)

    return pl.pallas_call(body, name=name, grid=(F // tn, S // tm),
                          in_specs=[blk, prev, pl.BlockSpec((tm, tn), lambda j, i: (i, j)), w, b],
                          out_specs=(blk, pl.BlockSpec((2, 4, tn), lambda j, i: (0, 0, j))),
                          out_shape=(jax.ShapeDtypeStruct((2, S, F), CDT), jax.ShapeDtypeStruct((2, 4, F), F32)),
                          compiler_params=_params(("parallel", "arbitrary")))(u0, u0, da, cw, cb)


def _conv_bwd_input(du, cw, name, *, tm=2048, tn=256):
    _, S, F = du.shape
    tm = _tile(S, tm, HALO)
    tn = _tile(F, tn, 128)
    blk, _, nxt, w, _ = _conv_specs(tm, tn, S // HALO)
    ni = S // tm

    def body(d_ref, h_ref, w_ref, o_ref):
        d = d_ref[...].astype(F32)
        nx = jnp.where(pl.program_id(1) == ni - 1, 0.0, h_ref[...].astype(F32))
        for half in range(2):
            wv = w_ref[half]
            y = d[half] * wv[2:3, :] + _shift_up(d[half], nx[half], 1) * wv[1:2, :] + _shift_up(d[half], nx[half], 2) * wv[0:1, :]
            o_ref[half] = y.astype(o_ref.dtype)

    return pl.pallas_call(body, name=name, grid=(F // tn, ni), in_specs=[blk, nxt, w], out_specs=blk,
                          out_shape=jax.ShapeDtypeStruct((2, S, F), CDT),
                          compiler_params=_params(("parallel", "parallel")))(du, du, cw)


ANY = pl.BlockSpec(memory_space=pl.ANY)


def _place():
    return lax.axis_index("x"), lax.axis_index("y"), lax.axis_index("c")


def _other_chips(x, y):
    return ((1 - x, y), (x, 1 - y), (1 - x, 1 - y))


def _when(pred, fn):
    if pred is True:
        fn()
    else:
        pl.when(pred)(fn)


GATHER_SEMS = (pltpu.SemaphoreType.DMA((6,)), pltpu.SemaphoreType.DMA((6,)))
SCATTER_SEMS = (pltpu.SemaphoreType.DMA((3,)), pltpu.SemaphoreType.DMA((3,)))


def _gather_steps(out_ref, send_sems, recv_sems, first=True, middle=True, last=True):
    half = out_ref.shape[1] // 2
    x, y, c = _place()
    chips = _other_chips(x, y)

    def part(chip, pc):
        return out_ref.at[2 * chip[0] + chip[1], pl.ds(pl.multiple_of(pc * half, 16), half), :]

    def copy(k, chip, pc, to):
        return pltpu.make_async_remote_copy(src_ref=part(chip, pc), dst_ref=part(chip, pc),
                                            send_sem=send_sems.at[k], recv_sem=recv_sems.at[k],
                                            device_id=to, device_id_type=MESH)

    def send_mine():
        for j, chip in enumerate(chips):
            copy(j, (x, y), c, (*chip, c)).start()

    def pass_on():
        for j, chip in enumerate(chips):
            copy(j, chip, c, (x, y, c)).wait_recv()
            copy(3 + j, chip, c, (x, y, 1 - c)).start()

    def finish():
        for j, chip in enumerate(chips):
            copy(3 + j, chip, 1 - c, (x, y, c)).wait_recv()
        for j, chip in enumerate(chips):
            copy(j, (x, y), c, (*chip, c)).wait_send()
            copy(3 + j, chip, c, (x, y, 1 - c)).wait_send()

    _when(first, send_mine)
    _when(middle, pass_on)
    _when(last, finish)


def _gather_weights(buf):
    def body(buf_ref, out_ref, send_sems, recv_sems):
        del buf_ref
        _gather_steps(out_ref, send_sems, recv_sems)

    return pl.pallas_call(body, name="gather_weights", in_specs=[ANY], out_specs=ANY,
                          out_shape=jax.ShapeDtypeStruct(buf.shape, buf.dtype), input_output_aliases={0: 0},
                          scratch_shapes=list(GATHER_SEMS))(buf)


def _gather_small(v):
    m = v.shape[0]

    def body(v_ref, out_ref, send_sems, recv_sems, local_sem):
        x, y, c = _place()
        me, sibling = (x, y, c), (x, y, 1 - c)
        chips = _other_chips(x, y)

        def rows(px, py, pc):
            return out_ref.at[pl.ds((4 * px + 2 * py + pc) * m, m), :]

        def copy(k, block, to, src=None):
            return pltpu.make_async_remote_copy(src_ref=rows(*block) if src is None else src, dst_ref=rows(*block),
                                                send_sem=send_sems.at[k], recv_sem=recv_sems.at[k],
                                                device_id=to, device_id_type=MESH)

        mine = pltpu.make_async_copy(v_ref, rows(*me), local_sem)
        mine.start()
        first = [copy(0, me, sibling, src=v_ref)]
        first += [copy(1 + j, me, (*chip, c), src=v_ref) for j, chip in enumerate(chips)]
        for cp in first:
            cp.start()
        passed = [copy(4 + j, (*chip, c), sibling) for j, chip in enumerate(chips)]
        for j, chip in enumerate(chips):
            copy(1 + j, (*chip, c), me).wait_recv()
            passed[j].start()
        copy(0, sibling, me).wait_recv()
        for j, chip in enumerate(chips):
            copy(4 + j, (*chip, 1 - c), me).wait_recv()
        for cp in first + passed:
            cp.wait_send()
        mine.wait()

    vm = pl.BlockSpec(memory_space=pltpu.VMEM)
    return pl.pallas_call(body, name="gather_small", in_specs=[vm], out_specs=vm,
                          out_shape=jax.ShapeDtypeStruct((8 * m, 128), v.dtype),
                          scratch_shapes=[pltpu.SemaphoreType.DMA((7,)), pltpu.SemaphoreType.DMA((7,)), pltpu.SemaphoreType.DMA])(v)


SWAP_SEMS = (pltpu.SemaphoreType.DMA, pltpu.SemaphoreType.DMA)


def _swap_steps(g_ref, out_ref, send_sem, recv_sem, first=True, last=True):
    half = out_ref.shape[1]
    x, y, c = _place()

    def copy():
        src = g_ref.at[:, pl.ds(pl.multiple_of((1 - c) * half, 8), half), :]
        return pltpu.make_async_remote_copy(src_ref=src, dst_ref=out_ref, send_sem=send_sem, recv_sem=recv_sem,
                                            device_id=(x, y, 1 - c), device_id_type=MESH)

    _when(first, lambda: copy().start())
    _when(last, lambda: copy().wait())


def _swap_halves(g, name):
    n, rows, _ = g.shape

    def body(g_ref, out_ref, send_sem, recv_sem):
        _swap_steps(g_ref, out_ref, send_sem, recv_sem)

    return pl.pallas_call(body, name=name, in_specs=[ANY], out_specs=ANY,
                          out_shape=jax.ShapeDtypeStruct((n, rows // 2, 128), g.dtype),
                          scratch_shapes=list(SWAP_SEMS))(g)


def _scatter_steps(h_ref, out_ref, send_sems, recv_sems, first=True, last=True):
    x, y, c = _place()

    def copies():
        return [pltpu.make_async_remote_copy(src_ref=h_ref.at[2 * chip[0] + chip[1]], dst_ref=out_ref.at[j],
                                             send_sem=send_sems.at[j], recv_sem=recv_sems.at[j],
                                             device_id=(*chip, c), device_id_type=MESH)
                for j, chip in enumerate(_other_chips(x, y))]

    def start():
        for cp in copies():
            cp.start()

    def finish():
        for cp in copies():
            cp.wait()

    _when(first, start)
    _when(last, finish)


def _scatter_chips(hsum):
    n, half, _ = hsum.shape

    def body(h_ref, out_ref, send_sems, recv_sems):
        _scatter_steps(h_ref, out_ref, send_sems, recv_sems)

    return pl.pallas_call(body, name="scatter_chips", in_specs=[ANY], out_specs=ANY,
                          out_shape=jax.ShapeDtypeStruct((3, half, 128), hsum.dtype),
                          scratch_shapes=list(SCATTER_SEMS))(hsum)


def _join_halves(buf, name):
    half = buf.shape[0] // 2

    def body(buf_ref, out_ref, send_sem, recv_sem):
        del buf_ref
        x, y, c = _place()
        mine = out_ref.at[pl.ds(pl.multiple_of(c * half, 8), half), :]
        other = out_ref.at[pl.ds(pl.multiple_of((1 - c) * half, 8), half), :]
        cp = pltpu.make_async_remote_copy(src_ref=mine, dst_ref=mine, send_sem=send_sem, recv_sem=recv_sem,
                                          device_id=(x, y, 1 - c), device_id_type=MESH)
        cp.start()
        cp.wait_send()
        pltpu.make_async_remote_copy(src_ref=other, dst_ref=other, send_sem=send_sem, recv_sem=recv_sem,
                                     device_id=(x, y, 1 - c), device_id_type=MESH).wait_recv()

    return pl.pallas_call(body, name=name, in_specs=[ANY], out_specs=ANY,
                          out_shape=jax.ShapeDtypeStruct(buf.shape, buf.dtype), input_output_aliases={0: 0},
                          scratch_shapes=[pltpu.SemaphoreType.DMA, pltpu.SemaphoreType.DMA])(buf)


def _add_sibling(g, recv, c_idx, name):
    n, rows, _ = g.shape
    half = rows // 2
    tr = _tile(half, ADAM_ROWS, 16)
    nb = half // tr

    def body(c_ref, g_ref, r_ref, o_ref, ob_ref):
        s = g_ref[...] + r_ref[...]
        o_ref[...] = s
        ob_ref[...] = s.astype(CDT)

    out = pl.BlockSpec((None, tr, 128), lambda k, i, c: (k, i, 0))
    grid_spec = pltpu.PrefetchScalarGridSpec(
        num_scalar_prefetch=1, grid=(n, nb),
        in_specs=[pl.BlockSpec((None, tr, 128), lambda k, i, c: (k, c[0] * nb + i, 0)), out],
        out_specs=(out, out))
    return pl.pallas_call(body, name=name, grid_spec=grid_spec,
                          out_shape=(jax.ShapeDtypeStruct((n, half, 128), F32), jax.ShapeDtypeStruct((n, half, 128), CDT)),
                          compiler_params=_params(("parallel", "parallel")))(c_idx, g, recv)


def _add_chips(hsum, recv, chip_idx, name):
    n, half, _ = hsum.shape
    tr = _tile(half, ADAM_ROWS, 16)

    def body(k_ref, h_ref, r_ref, o_ref):
        o_ref[...] = ((h_ref[...] + r_ref[0].astype(F32)) + r_ref[1].astype(F32)) + r_ref[2].astype(F32)

    grid_spec = pltpu.PrefetchScalarGridSpec(
        num_scalar_prefetch=1, grid=(half // tr,),
        in_specs=[pl.BlockSpec((None, tr, 128), lambda i, k: (k[0], i, 0)),
                  pl.BlockSpec((3, tr, 128), lambda i, k: (0, i, 0))],
        out_specs=pl.BlockSpec((tr, 128), lambda i, k: (i, 0)))
    return pl.pallas_call(body, name=name, grid_spec=grid_spec, out_shape=jax.ShapeDtypeStruct((half, 128), F32),
                          compiler_params=_params(("parallel",)))(chip_idx, hsum, recv)


def _adamw_math(g, w, m, v):
    m2 = B1 * m + (1.0 - B1) * g
    v2 = B2 * v + (1.0 - B2) * (g * g)
    delta = -LR * ((m2 / BC1) / (jnp.sqrt(v2 / BC2) + AEPS) + WD * w)
    return delta, m2, v2


def _adamw(g, w, m, v, name):
    rows, cols = g.shape
    tr = _tile(rows, max(8, (ADAM_ROWS * 128 // cols) // 8 * 8), 8)

    def body(g_ref, w_ref, m_ref, v_ref, d_ref, m2_ref, v2_ref):
        d_ref[...], m2_ref[...], v2_ref[...] = _adamw_math(g_ref[...], w_ref[...], m_ref[...], v_ref[...])

    blk = pl.BlockSpec((None, tr, cols), lambda i: (0, i, 0))
    shp = jax.ShapeDtypeStruct((1, rows, cols), F32)
    return pl.pallas_call(body, name=name, grid=(rows // tr,), in_specs=[pl.BlockSpec((tr, cols), lambda i: (i, 0))] + [blk] * 3,
                          out_specs=(blk,) * 3, out_shape=(shp,) * 3, compiler_params=_params(("parallel",)))(g, w, m, v)


def _adamw_small(parts, w, m, v, name):
    rows = w.shape[0]

    def body(p_ref, w_ref, m_ref, v_ref, g_ref, d_ref, m2_ref, v2_ref):
        g = p_ref[0]
        for k in range(1, 8):
            g = g + p_ref[k]
        g_ref[...] = g
        d_ref[...], m2_ref[...], v2_ref[...] = _adamw_math(g, w_ref[...], m_ref[...], v_ref[...])

    shp = jax.ShapeDtypeStruct((rows, 128), F32)
    return pl.pallas_call(body, name=name, out_shape=(shp,) * 4)(parts, w, m, v)


def _pack_rows(parts, rows):
    flat = jnp.concatenate([p.reshape(-1) for p in parts])
    return jnp.pad(flat, (0, rows * 128 - flat.shape[0])).reshape(rows, 128)


def _unpack(flat, sizes, shapes):
    out, off = [], 0
    for n, s in zip(sizes, shapes):
        out.append(flat[off:off + n].reshape(s))
        off += n
    return out


def _to_shards(full, shard_shape, axis):
    if axis == 0:
        return full.reshape(N_CHIP, -1)
    r, cs = shard_shape
    return full.reshape(r, N_CHIP, cs).transpose(1, 0, 2).reshape(N_CHIP, -1)


def _from_shards(sh, shard_shape, axis):
    r, cs = shard_shape
    if axis == 0:
        return sh.reshape(N_CHIP * r, cs)
    return sh.reshape(N_CHIP, r, cs).transpose(1, 0, 2).reshape(r, N_CHIP * cs)


def _local_step(x0, mem, tgt, W, gains, ex=None):
    S = x0.shape[0]
    w_in = jnp.pad(W["w_in"], ((0, 0), (0, IN_PAD - IN_COLS)))
    b_f = jnp.pad(gains["b_forget"], ((0, 0), (0, 128 - NH)))

    h1 = _rms_cast(x0, gains["attn_norm_g"], "norm_attn")
    qkv, sq2, sk2 = _proj_qkv(h1, w_in[:, :NQKV], "proj_qkv")
    fox, sb_q, sb_v = (0, NH, 2 * NH), 3 * NH, 5 * NH
    fl = _mm_nn(h1, w_in[:, NQKV:NQKV + 128], F32, "proj_gate")
    cum = _gate_fwd(fl, b_f, "gate_cumsum")
    c_hm = cum[:, :NH].T
    cq, ck = c_hm[:, :, None], c_hm[:, None, :]
    fo_h, lse, gathered = _fox_fwd(qkv, qkv, qkv, cq, ck, "fox_fwd", slots=None if ex is None else ex.slots("b"), h0=fox)
    if ex is not None:
        W = {**W, **ex.unpack("b", gathered)}
    cw = W["conv_w"].reshape(3, 2, DFF).transpose(1, 0, 2)
    cb = gains["conv_b"].reshape(2, 1, DFF)
    so_h, s_lt = _sb_fwd(sq2, sk2, qkv, "sb_fwd", v0=sb_v)
    x1, mixed = _out_proj(fo_h, so_h, gains["fox_out_g"], gains["sb_out_g"], W["w_out"], x0, "out_proj")

    h2 = _rms_cast(x1, gains["xattn_norm_g"], "norm_xattn")
    mn = _rms_cast(mem, gains["mem_norm_g"], "norm_mem")
    mq = _mm_nn(h2, W["w_mq"], CDT, "proj_mq")
    kv = _mm_nn(mn, W["w_mkv"], CDT, "proj_mkv")
    x2, mo = _xattn_fwd(mq, kv, W["w_mo"], x1, "xattn_fwd")

    h3 = _rms_cast(x2, gains["ffn_norm_g"], "norm_ffn")
    u0 = _mm_nn(h3, W["w_up"], CDT, "ffn_up", tm=512, tn=DFF, halves=True)
    act = _conv_act(u0, cw, cb, "conv_act")
    x3 = _mm_nn(act, W["w_down"], F32, "ffn_down", tm=512, residual=x2)
    loss, dx3, dg_final = _loss_bwd(x3, tgt, gains["final_norm_g"].reshape(1, D), "loss")

    gw, gs = {}, {"final_norm_g": dg_final}
    da = _mm_nt(dx3, W["w_down"], "ffn_down_dx", tn=DFF, out_dtype=CDT)
    gw["w_down"] = _mm_tn(act, dx3, "ffn_down_dw", tka=DFF)
    du, dwb = _conv_act_bwd(u0, da, cw, cb, "conv_act_bwd")
    gw["conv_w"] = dwb[:, :3].transpose(1, 0, 2).reshape(3, 2 * DFF)
    gs["conv_b"] = dwb[:, 3].reshape(1, 2 * DFF)
    du0 = _conv_bwd_input(du, cw, "conv_bwd_input")
    gw["w_up"] = _mm_tn(h3, du0, "ffn_up_dw", tn=DFF, b_halves=True)
    dx2, gs["ffn_norm_g"] = _mm_nt_rmsbwd(du0, W["w_up"], x2, gains["ffn_norm_g"], dx3, "ffn_up_dx", tk=DFF, a_halves=True)

    dmo = _mm_nt(dx2, W["w_mo"], "mo_dx", tn=512, out_dtype=CDT)
    gw["w_mo"] = _mm_tn(mo, dx2, "mo_dw")
    dmq, dkv = _xattn_bwd(mq, kv, dmo, "xattn_bwd")
    gw["w_mq"] = _mm_tn(h2, dmq, "mq_dw")
    dx1, gs["xattn_norm_g"] = _mm_nt_rmsbwd(dmq, W["w_mq"], x1, gains["xattn_norm_g"], dx2, "mq_dx")
    gw["w_mkv"] = _mm_tn(mn, dkv, "mkv_dw")
    _, gs["mem_norm_g"] = _mm_nt_rmsbwd(dkv, W["w_mkv"], mem, gains["mem_norm_g"], jnp.zeros_like(mem), "mkv_dx")

    gw["w_out"] = _mm_tn(mixed, dx1, "out_dw")
    dfo_h, dso_h, gs["fox_out_g"], gs["sb_out_g"] = _out_proj_bwd(dx1, W["w_out"], fo_h, so_h, gains["fox_out_g"], gains["sb_out_g"], "out_dx")
    flat = None if ex is None else ex.flat("b", gw)
    dfq, dfk, dfv, dck, dcq, got = _fox_bwd(qkv, qkv, qkv, cq, ck, fo_h, dfo_h, lse, "fox_bwd", h0=fox, swap=flat)
    pair, pair16 = (None, None) if ex is None else ex.pair_sums("b", flat, got)
    dsq, dsk, dsv, arrived = _sb_bwd(qkv, sq2, sk2, qkv, dso_h, s_lt, "sb_bwd", scatter=pair16, q0=sb_q, v0=sb_v)
    dfl, db = _gate_bwd(jnp.pad(dck[:, 0, :].T, ((0, 0), (0, 128 - NH))), dcq, fl, b_f, "gate_bwd")
    gs["b_forget"] = db[:, :NH]
    dqkv = jnp.concatenate([dfq, dfk.astype(CDT), dfv.astype(CDT), dsq, dsk.astype(CDT), dsv.astype(CDT)], axis=0)
    dproj = jnp.concatenate([dqkv.transpose(1, 0, 2).reshape(S, NQKV), dfl.astype(CDT),
                             jnp.zeros((S, IN_PAD - NQKV - 128), CDT)], axis=1)
    gw["w_in"] = _mm_tn(h1, dproj, "in_dw", tn=IN_PAD)[:, :IN_COLS]
    dx0, gs["attn_norm_g"] = _mm_nt_rmsbwd(dproj, w_in, x0, gains["attn_norm_g"], dx1, "in_dx", tk=IN_PAD)
    return loss, dx0, gw, gs, (pair, arrived)


NAMES = ("attn_norm_g", "w_in", "b_forget", "fox_out_g", "sb_out_g", "w_out", "xattn_norm_g", "mem_norm_g", "w_mq",
         "w_mkv", "w_mo", "ffn_norm_g", "w_up", "conv_w", "conv_b", "w_down", "final_norm_g")


class _Exchange:
    def __init__(self, w):
        self.w = w
        xi, yi, ci = _place()
        self.core = ci
        self.chip = 2 * xi + yi
        self.core_idx = jnp.reshape(ci, (1,)).astype(jnp.int32)
        self.chip_idx = jnp.reshape(self.chip, (1,)).astype(jnp.int32)

    def slots(self, g):
        parts = []
        for name, shape, _ in GROUPS[g]:
            blk = self.w[name].reshape(shape)
            parts.append(lax.bitcast_convert_type(blk, CDT) if name == "conv_w" else blk.astype(CDT))
        rows = _rows_g(GROUPS[g])
        return lax.dynamic_update_slice(lax.empty((N_CHIP, rows, 128), CDT), _pack_rows(parts, rows)[None], (self.chip, 0, 0))

    def unpack(self, g, gathered):
        flat, full, off = gathered.reshape(N_CHIP, -1), {}, 0
        for (name, shape, axis), n in zip(GROUPS[g], _gather_sizes(GROUPS[g])):
            sh = flat[:, off:off + n]
            off += n
            if name == "conv_w":
                sh = lax.bitcast_convert_type(sh.reshape(N_CHIP, n // 2, 2), F32)
            full[name] = _from_shards(sh, shape, axis)
        return full

    def flat(self, g, gw):
        rows = _rows_f(GROUPS[g])
        flat = jnp.concatenate([_to_shards(gw[name], shape, axis) for name, shape, axis in GROUPS[g]], axis=1)
        return jnp.pad(flat, ((0, 0), (0, rows * 128 - flat.shape[1]))).reshape(N_CHIP, rows, 128)

    def pair_sums(self, g, flat, got=None):
        if got is None:
            got = _swap_halves(flat, "swap_halves_" + g)
        return _add_sibling(flat, got, self.core_idx, "add_sibling_" + g)

    def finish(self, g, pair, arrived):
        rows = _rows_f(GROUPS[g])
        mine = _add_chips(pair, arrived, self.chip_idx, "add_chips_" + g)
        whole = _join_halves(lax.dynamic_update_slice(lax.empty((rows, 128), F32), mine, (self.core * (rows // 2), 0)), "join_halves_" + g)
        shapes = [s for _, s, _ in GROUPS[g]]
        return {name: arr for (name, _, _), arr in zip(GROUPS[g], _unpack(whole.reshape(-1), _sizes(GROUPS[g]), shapes))}


def _step(x, mem, loss_target, w, m, v):
    ex = _Exchange(w)

    W = ex.unpack("a", _gather_weights(ex.slots("a")))
    gains = {name: w[name].reshape(1, -1) for name, _ in SMALL}

    loss, grad_x, gw, gs, (pair_b, arrived_b) = _local_step(x[0], mem[0], loss_target[0], W, gains, ex)

    grads = ex.finish("b", pair_b, arrived_b)
    pair_a, pair16_a = ex.pair_sums("a", ex.flat("a", gw))
    grads.update(ex.finish("a", pair_a, _scatter_chips(pair16_a)))
    small = jnp.concatenate([gs[name].reshape(-1) for name, _ in SMALL] + [loss[0, :1]])
    small = jnp.pad(small, (0, ROWS_S * 128 - P_SMALL)).reshape(ROWS_S, 128)
    small_parts = _gather_small(small).reshape(8, ROWS_S, 128)

    def flat_small(d):
        return _pack_rows([d[name] for name, _ in SMALL], ROWS_S)

    outs = {}
    for name, shape, _ in BIG:
        g = grads[name]
        res = _adamw(g, w[name], m[name], v[name], "adamw_" + name)
        for prefix, arr in zip(("grad_", "delta_", "new_m_", "new_v_"), (g, *res)):
            outs[prefix + name] = arr.reshape(w[name].shape)
    small_res = _adamw_small(small_parts, flat_small(w), flat_small(m), flat_small(v), "adamw_small")
    g_sm = small_res[0]
    for prefix, sm in zip(("grad_", "delta_", "new_m_", "new_v_"), small_res):
        for (name, n), arr in zip(SMALL, _unpack(sm.reshape(-1), [n for _, n in SMALL], [(n,) for _, n in SMALL])):
            outs[prefix + name] = arr.reshape(w[name].shape)
    total_loss = g_sm.reshape(-1)[P_SMALL - 1]
    return (total_loss, grad_x[None], *[outs[p + n] for p in ("grad_", "delta_", "new_m_", "new_v_") for n in NAMES])


def kernel(x, mem, attn_norm_g, w_in, b_forget, fox_out_g, sb_out_g, w_out, xattn_norm_g, mem_norm_g, w_mq, w_mkv, w_mo, ffn_norm_g, w_up, conv_w, conv_b, w_down, final_norm_g, loss_target, m_attn_norm_g, m_w_in, m_b_forget, m_fox_out_g, m_sb_out_g, m_w_out, m_xattn_norm_g, m_mem_norm_g, m_w_mq, m_w_mkv, m_w_mo, m_ffn_norm_g, m_w_up, m_conv_w, m_conv_b, m_w_down, m_final_norm_g, v_attn_norm_g, v_w_in, v_b_forget, v_fox_out_g, v_sb_out_g, v_w_out, v_xattn_norm_g, v_mem_norm_g, v_w_mq, v_w_mkv, v_w_mo, v_ffn_norm_g, v_w_up, v_conv_w, v_conv_b, v_w_down, v_final_norm_g):
    given = dict(locals())
    w = {n: given[n] for n in NAMES}
    m = {n: given["m_" + n] for n in NAMES}
    v = {n: given["v_" + n] for n in NAMES}
    return _step(x, mem, loss_target, w, m, v)
```

```python
import functools

import numpy as np
import jax
import jax.numpy as jnp
from jax import lax
from jax.experimental import pallas as pl
from jax.experimental.pallas import tpu as pltpu

F32 = jnp.float32
CDT = jnp.bfloat16
MESH = pl.DeviceIdType.MESH

D = 1024
HD = 64
NH = 8
GW = NH * HD
NQKV = 6 * GW
IN_COLS = NQKV + NH
IN_PAD = NQKV + 256
NMH = 4
MHD = D // NMH
DFF = 2816
EPS = 1e-6
ATT_SCALE = HD ** -0.5
MEM_SCALE = MHD ** -0.5
NEG = -1e30

LR, B1, B2, AEPS, WD, STEP = 0.001, 0.9, 0.999, 1e-08, 0.01, 10
BC1 = 1.0 - B1 ** STEP
BC2 = 1.0 - B2 ** STEP

ATT_TILES = {"fox_fwd": (1024, 2048), "fox_bwd": (512, 1024), "sb_fwd": (512, 1024), "sb_bwd": (1024, 1024)}
W_SB = 256
VMEM_LIMIT = 52 * 2 ** 20

N_CHIP = 4
BIG = (("w_in", (D, IN_COLS // N_CHIP), 1), ("w_out", (D // N_CHIP, D), 0), ("w_mq", (D // N_CHIP, D), 0),
       ("w_mkv", (D, 2 * D // N_CHIP), 1), ("w_mo", (D // N_CHIP, D), 0), ("w_up", (D, 2 * DFF // N_CHIP), 1),
       ("conv_w", (3, 2 * DFF // N_CHIP), 1), ("w_down", (DFF // N_CHIP, D), 0))
GROUPS = {"a": BIG[:1], "b": BIG[1:]}
ADAM_ROWS = 1536


def _sizes(group):
    return tuple(int(np.prod(s)) for _, s, _ in group)


def _gather_sizes(group):
    return tuple(2 * n if name == "conv_w" else n for (name, _, _), n in zip(group, _sizes(group)))


def _rows_g(group):
    return -(-sum(_gather_sizes(group)) // 4096) * 32


def _rows_f(group):
    return -(-sum(_sizes(group)) // 65536) * 512
SMALL = (("attn_norm_g", 1024), ("b_forget", 8), ("fox_out_g", 512), ("sb_out_g", 512), ("xattn_norm_g", 1024),
         ("mem_norm_g", 1024), ("ffn_norm_g", 1024), ("conv_b", 2 * DFF), ("final_norm_g", 1024))
P_SMALL = sum(n for _, n in SMALL) + 1
ROWS_S = -(-P_SMALL // 1024) * 8


def _params(sem=None, vmem=VMEM_LIMIT):
    return pltpu.CompilerParams(dimension_semantics=sem, vmem_limit_bytes=vmem)


def _tile(n, pref, mult):
    t = (min(pref, n) // mult) * mult
    while t >= mult:
        if n % t == 0:
            return t
        t -= mult
    return n


def _dot(a, b):
    return jnp.dot(a, b, preferred_element_type=F32)


def _dot_nt(a, b):
    return lax.dot_general(a, b, (((1,), (1,)), ((), ())), preferred_element_type=F32)


def _dot_tn(a, b):
    return lax.dot_general(a, b, (((0,), (0,)), ((), ())), preferred_element_type=F32)


def _split3(x):
    h1 = x.astype(CDT)
    r1 = x - h1.astype(F32)
    h2 = r1.astype(CDT)
    h3 = (r1 - h2.astype(F32)).astype(CDT)
    return h1, h2, h3


def _split2(x):
    h1 = x.astype(CDT)
    return h1, (x - h1.astype(F32)).astype(CDT)


def _rms_bwd(dh, x, g):
    r = lax.rsqrt(jnp.mean(x * x, axis=-1, keepdims=True) + EPS)
    xn = x * r
    dg = jnp.sum(dh * xn, axis=0, keepdims=True)
    dhg = dh * g
    dx = r * (dhg - xn * jnp.mean(dhg * xn, axis=-1, keepdims=True))
    return dx, dg


def _mm_nn(a, b, out_dtype, name, *, tm=1024, tn=512, residual=None, halves=False):
    M, K = a.shape
    N = b.shape[1]
    tm = _tile(M, tm, 16)
    tn = _tile(N // 2 if halves else N, tn, 128)
    nj = N // tn

    def body(*refs):
        a_ref, b_ref = refs[0], refs[1]
        o_ref = refs[-1]
        acc = _dot(a_ref[...].astype(CDT), b_ref[...].astype(CDT))
        if residual is not None:
            acc = acc + refs[2][...]
        o_ref[...] = acc.astype(o_ref.dtype)

    in_specs = [pl.BlockSpec((tm, K), lambda i, j: (i, 0)), pl.BlockSpec((K, tn), lambda i, j: (0, j))]
    ops = [a, b]
    if residual is not None:
        in_specs.append(pl.BlockSpec((tm, tn), lambda i, j: (i, j)))
        ops.append(residual)
    if halves:
        njh = nj // 2
        out_shape = jax.ShapeDtypeStruct((2, M, N // 2), out_dtype)
        out_spec = pl.BlockSpec((None, tm, tn), lambda i, j: (j // njh, i, j % njh))
    else:
        out_shape = jax.ShapeDtypeStruct((M, N), out_dtype)
        out_spec = pl.BlockSpec((tm, tn), lambda i, j: (i, j))
    return pl.pallas_call(body, name=name, grid=(M // tm, nj), in_specs=in_specs, out_specs=out_spec,
                          out_shape=out_shape, compiler_params=_params(("parallel", "parallel")))(*ops)


def _mm_tn(a, b, name, *, tka=512, tn=1024, ts=512, b_halves=False):
    S, Ka = a.shape
    N = 2 * b.shape[2] if b_halves else b.shape[1]
    tka = _tile(Ka, tka, 128)
    tn = _tile(N // 2 if b_halves else N, tn, 128)
    ts = _tile(S, ts, 16)
    nn = N // tn

    def body(a_ref, b_ref, o_ref):
        @pl.when(pl.program_id(2) == 0)
        def _():
            o_ref[...] = jnp.zeros_like(o_ref)
        o_ref[...] += _dot_tn(a_ref[...].astype(CDT), b_ref[...].astype(CDT))

    if b_halves:
        nnh = nn // 2
        b_spec = pl.BlockSpec((None, ts, tn), lambda i, j, s: (j // nnh, s, j % nnh))
    else:
        b_spec = pl.BlockSpec((ts, tn), lambda i, j, s: (s, j))
    return pl.pallas_call(
        body, name=name, grid=(Ka // tka, nn, S // ts),
        in_specs=[pl.BlockSpec((ts, tka), lambda i, j, s: (s, i)), b_spec],
        out_specs=pl.BlockSpec((tka, tn), lambda i, j, s: (i, j)),
        out_shape=jax.ShapeDtypeStruct((Ka, N), F32),
        compiler_params=_params(("parallel", "parallel", "arbitrary")))(a, b)


def _mm_nt(a, b, name, *, tm=512, tn=None, tk=None, a_halves=False, out_dtype=F32,
           epilogue=None, extra=(), extra_specs=(), out_shape=None, out_specs=None):
    if a_halves:
        M, K = a.shape[1], 2 * a.shape[2]
    else:
        M, K = a.shape
    N = b.shape[0]
    tm = _tile(M, tm, 16)
    tn = N if (epilogue is not None or tn is None) else _tile(N, tn, 128)
    tk = K if tk is None else _tile(K // 2 if a_halves else K, tk, 128)
    nk = K // tk
    n_extra = len(extra)

    def body(*refs):
        a_ref, b_ref = refs[0], refs[1]
        extra_refs = refs[2:2 + n_extra]
        out_refs = refs[2 + n_extra:-1]
        acc_ref = refs[-1]
        k = pl.program_id(2)

        @pl.when(k == 0)
        def _():
            acc_ref[...] = jnp.zeros_like(acc_ref)
        acc_ref[...] += _dot_nt(a_ref[...].astype(CDT), b_ref[...].astype(CDT))

        @pl.when(k == nk - 1)
        def _():
            if epilogue is None:
                out_refs[0][...] = acc_ref[...].astype(out_refs[0].dtype)
            else:
                epilogue(acc_ref[...], pl.program_id(0), extra_refs, out_refs)

    if a_halves:
        nkh = nk // 2
        a_spec = pl.BlockSpec((None, tm, tk), lambda i, j, k: (k // nkh, i, k % nkh))
    else:
        a_spec = pl.BlockSpec((tm, tk), lambda i, j, k: (i, k))
    if epilogue is None:
        out_shape = jax.ShapeDtypeStruct((M, N), out_dtype)
        out_specs = pl.BlockSpec((tm, tn), lambda i, j, k: (i, j))
        sem = ("parallel", "parallel", "arbitrary")
    else:
        sem = ("arbitrary", "arbitrary", "arbitrary")
    return pl.pallas_call(
        body, name=name, grid=(M // tm, N // tn, nk),
        in_specs=[a_spec, pl.BlockSpec((tn, tk), lambda i, j, k: (j, k)), *extra_specs],
        out_specs=out_specs, out_shape=out_shape,
        scratch_shapes=[pltpu.VMEM((tm, tn), F32)],
        compiler_params=_params(sem))(a, b, *extra)


def _mm_nt_rmsbwd(a, b, x, g, dres, name, *, tm=512, tk=None, a_halves=False):
    M = x.shape[0]
    tm = _tile(M, tm, 16)

    def epilogue(acc, i, extra_refs, out_refs):
        x_ref, g_ref, r_ref = extra_refs
        dx_ref, dg_ref = out_refs
        dx, dg = _rms_bwd(acc, x_ref[...], g_ref[...])
        dx_ref[...] = r_ref[...] + dx

        @pl.when(i == 0)
        def _():
            dg_ref[...] = jnp.zeros_like(dg_ref)
        dg_ref[...] += dg

    row = pl.BlockSpec((tm, D), lambda i, j, k: (i, 0))
    vec = pl.BlockSpec((1, D), lambda i, j, k: (0, 0))
    return _mm_nt(a, b, name, tm=tm, tk=tk, a_halves=a_halves, epilogue=epilogue,
                  extra=(x, g, dres), extra_specs=(row, vec, row),
                  out_shape=(jax.ShapeDtypeStruct((M, D), F32), jax.ShapeDtypeStruct((1, D), F32)),
                  out_specs=(row, vec))


def _rms_cast(x, g, name, *, tm=512):
    M, W = x.shape
    tm = _tile(M, tm, 16)

    def body(x_ref, g_ref, o_ref):
        xf = x_ref[...]
        r = lax.rsqrt(jnp.mean(xf * xf, axis=-1, keepdims=True) + EPS)
        o_ref[...] = (xf * r * g_ref[...]).astype(o_ref.dtype)

    return pl.pallas_call(body, name=name, grid=(M // tm,),
                          in_specs=[pl.BlockSpec((tm, W), lambda i: (i, 0)), pl.BlockSpec((1, W), lambda i: (0, 0))],
                          out_specs=pl.BlockSpec((tm, W), lambda i: (i, 0)),
                          out_shape=jax.ShapeDtypeStruct((M, W), CDT),
                          compiler_params=_params(("parallel",)))(x, g)


def _tri(n, lower):
    r = lax.broadcasted_iota(jnp.int32, (n, n), 0)
    c = lax.broadcasted_iota(jnp.int32, (n, n), 1)
    return (c <= r if lower else c >= r).astype(CDT)


def _gate_fwd(fl, b, name, *, tm=512):
    S = fl.shape[0]
    tm = _tile(S, tm, 16)

    def body(f_ref, b_ref, c_ref, carry):
        @pl.when(pl.program_id(0) == 0)
        def _():
            carry[...] = jnp.zeros_like(carry)
        z = f_ref[...] + b_ref[...]
        lf = jnp.minimum(z, 0.0) - jnp.log(1.0 + jnp.exp(-jnp.abs(z)))
        tri = _tri(tm, True)
        cum = sum(_dot(tri, p) for p in _split3(lf)) + carry[...]
        c_ref[...] = cum
        carry[...] = cum[tm - 1:tm, :]

    return pl.pallas_call(body, name=name, grid=(S // tm,),
                          in_specs=[pl.BlockSpec((tm, 128), lambda i: (i, 0)), pl.BlockSpec((1, 128), lambda i: (0, 0))],
                          out_specs=pl.BlockSpec((tm, 128), lambda i: (i, 0)),
                          out_shape=jax.ShapeDtypeStruct((S, 128), F32),
                          scratch_shapes=[pltpu.VMEM((1, 128), F32)],
                          compiler_params=_params(("arbitrary",)))(fl, b)


def _gate_bwd(dck, dcq, fl, b, name, *, tm=512):
    S = fl.shape[0]
    tm = _tile(S, tm, 16)
    nb = S // tm

    def body(dck_ref, dcq_ref, f_ref, b_ref, df_ref, db_ref, carry):
        @pl.when(pl.program_id(0) == 0)
        def _():
            carry[...] = jnp.zeros_like(carry)
            db_ref[...] = jnp.zeros_like(db_ref)
        lane = lax.broadcasted_iota(jnp.int32, (1, 128), 1)
        dc = dck_ref[...]
        for h in range(NH):
            dc = dc + dcq_ref[h] * (lane == h).astype(F32)
        tri = _tri(tm, False)
        suf = sum(_dot(tri, p) for p in _split3(dc)) + carry[...]
        carry[...] = suf[0:1, :]
        df = suf * jax.nn.sigmoid(-(f_ref[...] + b_ref[...]))
        df_ref[...] = df
        db_ref[...] += jnp.sum(df, axis=0, keepdims=True)

    rev = pl.BlockSpec((tm, 128), lambda i: (nb - 1 - i, 0))
    cols = pl.BlockSpec((NH, tm, 1), lambda i: (0, nb - 1 - i, 0))
    vec = pl.BlockSpec((1, 128), lambda i: (0, 0))
    return pl.pallas_call(body, name=name, grid=(nb,), in_specs=[rev, cols, rev, vec], out_specs=(rev, vec),
                          out_shape=(jax.ShapeDtypeStruct((S, 128), F32), jax.ShapeDtypeStruct((1, 128), F32)),
                          scratch_shapes=[pltpu.VMEM((1, 128), F32)],
                          compiler_params=_params(("arbitrary",)))(dck, dcq, fl, b)


def _group_rows(o_ref):
    return jnp.concatenate([o_ref[h] for h in range(NH)], axis=1)


def _out_proj(fo, so, gf, gs, w_out, x0, name, *, tm=512):
    S = fo.shape[1]
    tm = _tile(S, tm, 16)

    def body(fo_ref, so_ref, gf_ref, gs_ref, w_ref, x_ref, x1_ref, mx_ref):
        for ref, g_ref, lo in ((fo_ref, gf_ref, 0), (so_ref, gs_ref, GW)):
            o = _group_rows(ref)
            r = lax.rsqrt(jnp.mean(o * o, axis=-1, keepdims=True) + EPS)
            mx_ref[:, lo:lo + GW] = (o * r * g_ref[...]).astype(CDT)
        x1_ref[...] = x_ref[...] + _dot(mx_ref[...], w_ref[...])

    half = pl.BlockSpec((NH, tm, HD), lambda i: (0, i, 0))
    gvec = pl.BlockSpec((1, GW), lambda i: (0, 0))
    row = pl.BlockSpec((tm, D), lambda i: (i, 0))
    return pl.pallas_call(body, name=name, grid=(S // tm,),
                          in_specs=[half, half, gvec, gvec, pl.BlockSpec((D, D), lambda i: (0, 0)), row],
                          out_specs=(row, row),
                          out_shape=(jax.ShapeDtypeStruct((S, D), F32), jax.ShapeDtypeStruct((S, D), CDT)),
                          compiler_params=_params(("parallel",)))(fo, so, gf, gs, w_out, x0)


def _out_proj_bwd(dx1, w_out, fo, so, gf, gs, name, *, tm=512):
    S = fo.shape[1]
    tm = _tile(S, tm, 16)

    def epilogue(acc, i, extra_refs, out_refs):
        fo_ref, so_ref, gf_ref, gs_ref = extra_refs
        dfo_ref, dso_ref, dgf_ref, dgs_ref = out_refs

        @pl.when(i == 0)
        def _():
            dgf_ref[...] = jnp.zeros_like(dgf_ref)
            dgs_ref[...] = jnp.zeros_like(dgs_ref)
        for lo, o_ref, g_ref, do_ref, dg_ref in ((0, fo_ref, gf_ref, dfo_ref, dgf_ref), (GW, so_ref, gs_ref, dso_ref, dgs_ref)):
            dx, dg = _rms_bwd(acc[:, lo:lo + GW], _group_rows(o_ref), g_ref[...])
            for h in range(NH):
                do_ref[h] = dx[:, h * HD:(h + 1) * HD].astype(do_ref.dtype)
            dg_ref[...] += dg

    half = pl.BlockSpec((NH, tm, HD), lambda i, j, k: (0, i, 0))
    gvec = pl.BlockSpec((1, GW), lambda i, j, k: (0, 0))
    return _mm_nt(dx1, w_out, name, tm=tm, epilogue=epilogue, extra=(fo, so, gf, gs),
                  extra_specs=(half, half, gvec, gvec),
                  out_shape=(jax.ShapeDtypeStruct((NH, S, HD), CDT), jax.ShapeDtypeStruct((NH, S, HD), CDT),
                             jax.ShapeDtypeStruct((1, GW), F32), jax.ShapeDtypeStruct((1, GW), F32)),
                  out_specs=(half, half, gvec, gvec))


def _loss_bwd(x3, tgt, g, name, *, tm=512):
    S = x3.shape[0]
    tm = _tile(S, tm, 16)

    def body(x_ref, t_ref, g_ref, dx_ref, loss_ref, dg_ref):
        @pl.when(pl.program_id(0) == 0)
        def _():
            loss_ref[...] = jnp.zeros_like(loss_ref)
            dg_ref[...] = jnp.zeros_like(dg_ref)
        x = x_ref[...]
        gv = g_ref[...]
        r = lax.rsqrt(jnp.mean(x * x, axis=-1, keepdims=True) + EPS)
        xn = x * r
        err = xn * gv - t_ref[...]
        loss_ref[...] += jnp.full(loss_ref.shape, 0.5 * jnp.sum(jnp.mean(err * err, axis=-1, keepdims=True)), F32)
        dy = err * (1.0 / D)
        dg_ref[...] += jnp.sum(dy * xn, axis=0, keepdims=True)
        dyg = dy * gv
        dx_ref[...] = r * (dyg - xn * jnp.mean(dyg * xn, axis=-1, keepdims=True))

    row = pl.BlockSpec((tm, D), lambda i: (i, 0))
    vec = pl.BlockSpec((1, D), lambda i: (0, 0))
    dx3, loss, dg = pl.pallas_call(
        body, name=name, grid=(S // tm,), in_specs=[row, row, vec],
        out_specs=(row, pl.BlockSpec((1, 128), lambda i: (0, 0)), vec),
        out_shape=(jax.ShapeDtypeStruct((S, D), F32), jax.ShapeDtypeStruct((1, 128), F32), jax.ShapeDtypeStruct((1, D), F32)),
        compiler_params=_params(("arbitrary",)))(x3, tgt, g)
    return loss, dx3, dg


MASKED, FIRST, LAST = 1, 2, 4


def _att_tiles(name, S):
    tq, tk = ATT_TILES[name]
    return min(tq, S), min(tk, S)


def _pairs(S, tq, tk, descending=True):
    assert tk % tq == 0 and S % tk == 0
    qi, kj, fl = [], [], []
    for i in range(S // tq):
        last = ((i + 1) * tq - 1) // tk
        order = list(range(last, -1, -1) if descending else range(last + 1))
        for pos, kb in enumerate(order):
            qi.append(i)
            kj.append(kb)
            fl.append((MASKED if (kb + 1) * tk - 1 > i * tq else 0) | (FIRST if pos == 0 else 0) | (LAST if pos == last else 0))
    return tuple(jnp.asarray(np.asarray(a, np.int32)) for a in (qi, kj, fl))


def _head_blk(rows, by_key, head0, width=HD):
    if by_key:
        return pl.BlockSpec((1, rows, width), lambda h, n, qi, kj, fl: (h + head0, kj[n], 0))
    return pl.BlockSpec((1, rows, width), lambda h, n, qi, kj, fl: (h + head0, qi[n], 0))


def _att_specs(tq, tk, width=HD):
    qblk = pl.BlockSpec((1, tq, width), lambda h, n, qi, kj, fl: (h, qi[n], 0))
    kblk = pl.BlockSpec((1, tk, width), lambda h, n, qi, kj, fl: (h, kj[n], 0))
    qcol = pl.BlockSpec((1, tq, 1), lambda h, n, qi, kj, fl: (h, qi[n], 0))
    krow = pl.BlockSpec((1, 1, tk), lambda h, n, qi, kj, fl: (h, 0, kj[n]))
    return qblk, kblk, qcol, krow


def _causal(tq, w, ahead, strict):
    diff = lax.broadcasted_iota(jnp.int32, (tq, w), 1) - lax.broadcasted_iota(jnp.int32, (tq, w), 0)
    return diff < ahead if strict else diff <= ahead


def _masked_or_not(flags, step):
    pl.when(flags % 2 == 1)(functools.partial(step, True))
    pl.when(flags % 2 == 0)(functools.partial(step, False))


FOX_DEPTH = 2 * HD


def _fox_operands(qkv, cum, name, *, tm=512):
    S = qkv.shape[1]
    tm = _tile(S, tm, 16)

    def body(q_ref, k_ref, c_ref, q2_ref, k2_ref):
        lane = lax.broadcasted_iota(jnp.int32, (1, HD), 1)
        c = c_ref[...]
        for h in range(NH):
            pieces = [p.astype(F32) for p in _split3(c[:, h:h + 1])]
            qa = sum(jnp.where(lane == 2 * n, pieces[n], 0.0) for n in range(3)) + jnp.where((lane < 6) & (lane % 2 == 1), 1.0, 0.0)
            ka = sum(jnp.where(lane == 2 * n + 1, -pieces[n], 0.0) for n in range(3)) + jnp.where((lane < 6) & (lane % 2 == 0), 1.0, 0.0)
            q2_ref[h] = jnp.concatenate([qa.astype(CDT), q_ref[h] * ATT_SCALE], axis=1)
            k2_ref[h] = jnp.concatenate([ka.astype(CDT), k_ref[h]], axis=1)

    wide = pl.BlockSpec((NH, tm, FOX_DEPTH), lambda i: (0, i, 0))
    shp = jax.ShapeDtypeStruct((NH, S, FOX_DEPTH), CDT)
    return pl.pallas_call(body, name=name, grid=(S // tm,),
                          in_specs=[pl.BlockSpec((NH, tm, HD), lambda i: (0, i, 0)), pl.BlockSpec((NH, tm, HD), lambda i: (1, i, 0)),
                                    pl.BlockSpec((tm, 128), lambda i: (i, 0))],
                          out_specs=(wide, wide), out_shape=(shp, shp), compiler_params=_params(("parallel",)))(qkv, qkv, cum)


def _fox_fwd(q2, k2, v, name, slots=None, v0=0):
    S = q2.shape[1]
    tq, tk = _att_tiles("fox_fwd", S)
    qi, kj, fl = _pairs(S, tq, tk)
    qblk, kblk, qcol, krow = _att_specs(tq, tk)
    q2blk, k2blk, _, _ = _att_specs(tq, tk, FOX_DEPTH)
    npairs = int(qi.shape[0])

    def body(qi_ref, kj_ref, fl_ref, q2_ref, k2_ref, v_ref, *rest):
        if slots is None:
            o_ref, lse_ref, m_s, l_s, acc_s = rest
        else:
            _, o_ref, lse_ref, slots_ref, m_s, l_s, acc_s, send_sems, recv_sems = rest
        h, n = pl.program_id(0), pl.program_id(1)
        i, kb, flags = qi_ref[n], kj_ref[n], fl_ref[n]
        if slots is not None:
            _gather_steps(slots_ref, send_sems, recv_sems, first=(h == 0) & (n == 0), middle=(h == NH // 2) & (n == 0),
                          last=(h == NH - 1) & (n == npairs - 1))

        @pl.when(flags & FIRST != 0)
        def _():
            m_s[...] = jnp.full_like(m_s, NEG)
            l_s[...] = jnp.zeros_like(l_s)
            acc_s[...] = jnp.zeros_like(acc_s)

        def step(masked):
            s = _dot_nt(q2_ref[0], k2_ref[0])
            if masked:
                s = jnp.where(_causal(tq, tk, i * tq - kb * tk, False), s, NEG)
            m_new = jnp.maximum(m_s[...], jnp.max(s, axis=-1, keepdims=True))
            alpha = jnp.exp(m_s[...] - m_new)
            p = jnp.exp(s - m_new)
            l_s[...] = alpha * l_s[...] + jnp.sum(p, axis=-1, keepdims=True)
            acc_s[...] = alpha * acc_s[...] + _dot(p.astype(CDT), v_ref[0])
            m_s[...] = m_new

        _masked_or_not(flags, step)

        @pl.when(flags & LAST != 0)
        def _():
            o_ref[0] = acc_s[...] / l_s[...]
            lse_ref[0] = m_s[...] + jnp.log(l_s[...])

    scratch = [pltpu.VMEM((tq, 1), F32), pltpu.VMEM((tq, 1), F32), pltpu.VMEM((tq, HD), F32)]
    out_shape = (jax.ShapeDtypeStruct((NH, S, HD), F32), jax.ShapeDtypeStruct((NH, S, 1), F32))
    in_specs = [q2blk, k2blk, _head_blk(tk, True, v0)]
    if slots is None:
        grid_spec = pltpu.PrefetchScalarGridSpec(num_scalar_prefetch=3, grid=(NH, npairs), in_specs=in_specs,
                                                 out_specs=(qblk, qcol), scratch_shapes=scratch)
        o, lse = pl.pallas_call(body, name=name, grid_spec=grid_spec, out_shape=out_shape,
                                compiler_params=_params(("parallel", "arbitrary")))(qi, kj, fl, q2, k2, v)
        return o, lse, None
    grid_spec = pltpu.PrefetchScalarGridSpec(num_scalar_prefetch=3, grid=(NH, npairs), in_specs=in_specs + [ANY],
                                             out_specs=(qblk, qcol, ANY), scratch_shapes=scratch + list(GATHER_SEMS))
    return pl.pallas_call(body, name=name, grid_spec=grid_spec, out_shape=(*out_shape, jax.ShapeDtypeStruct(slots.shape, slots.dtype)),
                          input_output_aliases={6: 2},
                          compiler_params=_params(("arbitrary", "arbitrary")))(qi, kj, fl, q2, k2, v, slots)


def _fox_bwd(q, k, v, q2, k2, o, do, lse, name, h0=(0, 0, 0), swap=None):
    S = q.shape[1]
    tq, tk = _att_tiles("fox_bwd", S)
    qi, kj, fl = _pairs(S, tq, tk)
    qblk, kblk, qcol, krow = _att_specs(tq, tk)
    npairs = int(qi.shape[0])

    def body(qi_ref, kj_ref, fl_ref, q_ref, k_ref, v_ref, q2_ref, k2_ref, o_ref, do_ref, lse_ref, *rest):
        if swap is None:
            dq_ref, dk_ref, dv_ref, dck_ref, dcq_ref, dq_s, dl_s, dcq_s = rest
        else:
            g_ref, dq_ref, dk_ref, dv_ref, dck_ref, dcq_ref, got_ref, dq_s, dl_s, dcq_s, send_sem, recv_sem = rest
        n = pl.program_id(1)
        i, kb, flags = qi_ref[n], kj_ref[n], fl_ref[n]
        if swap is not None:
            h = pl.program_id(0)
            _swap_steps(g_ref, got_ref, send_sem, recv_sem, first=(h == 0) & (n == 0), last=(h == NH - 1) & (n == npairs - 1))

        @pl.when(n == 0)
        def _():
            dk_ref[...] = jnp.zeros_like(dk_ref)
            dv_ref[...] = jnp.zeros_like(dv_ref)
            dck_ref[...] = jnp.zeros_like(dck_ref)

        @pl.when(flags & FIRST != 0)
        def _():
            dq_s[...] = jnp.zeros_like(dq_s)
            dcq_s[...] = jnp.zeros_like(dcq_s)
            dl_s[...] = jnp.sum(do_ref[0].astype(F32) * o_ref[0], axis=-1, keepdims=True)

        def step(masked):
            qs = q_ref[0] * ATT_SCALE
            do = do_ref[0]
            p = jnp.exp(_dot_nt(q2_ref[0], k2_ref[0]) - lse_ref[0])
            if masked:
                p = jnp.where(_causal(tq, tk, i * tq - kb * tk, False), p, 0.0)
            ds = p * (_dot_nt(do, v_ref[0]) - dl_s[...])
            dsb = ds.astype(CDT)
            dq_s[...] += _dot(dsb, k_ref[0])
            rows = pl.ds(pl.multiple_of(kb * tk, tk), tk)
            dk_ref[0, rows, :] += _dot_tn(dsb, qs)
            dv_ref[0, rows, :] += _dot_tn(p.astype(CDT), do)
            dck_ref[0, :, rows] += -jnp.sum(ds, axis=0, keepdims=True)
            dcq_s[...] += jnp.sum(ds, axis=-1, keepdims=True)

        _masked_or_not(flags, step)

        @pl.when(flags & LAST != 0)
        def _():
            dq_ref[0] = (dq_s[...] * ATT_SCALE).astype(dq_ref.dtype)
            dcq_ref[0] = dcq_s[...]

    whole = pl.BlockSpec((1, S, HD), lambda h, n, qi, kj, fl: (h, 0, 0))
    q2blk, k2blk, _, _ = _att_specs(tq, tk, FOX_DEPTH)
    in_specs = [_head_blk(tq, False, h0[0]), _head_blk(tk, True, h0[1]), _head_blk(tk, True, h0[2]), q2blk, k2blk, qblk, qblk, qcol]
    out_specs = (qblk, whole, whole, pl.BlockSpec((1, 1, S), lambda h, n, qi, kj, fl: (h, 0, 0)), qcol)
    out_shape = (jax.ShapeDtypeStruct((NH, S, HD), CDT), jax.ShapeDtypeStruct((NH, S, HD), F32), jax.ShapeDtypeStruct((NH, S, HD), F32),
                 jax.ShapeDtypeStruct((NH, 1, S), F32), jax.ShapeDtypeStruct((NH, S, 1), F32))
    scratch = [pltpu.VMEM((tq, HD), F32), pltpu.VMEM((tq, 1), F32), pltpu.VMEM((tq, 1), F32)]
    if swap is None:
        grid_spec = pltpu.PrefetchScalarGridSpec(num_scalar_prefetch=3, grid=(NH, npairs), in_specs=in_specs,
                                                 out_specs=out_specs, scratch_shapes=scratch)
        return (*pl.pallas_call(body, name=name, grid_spec=grid_spec, out_shape=out_shape,
                                compiler_params=_params(("parallel", "arbitrary")))(qi, kj, fl, q, k, v, q2, k2, o, do, lse), None)
    grid_spec = pltpu.PrefetchScalarGridSpec(num_scalar_prefetch=3, grid=(NH, npairs), in_specs=in_specs + [ANY],
                                             out_specs=(*out_specs, ANY), scratch_shapes=scratch + list(SWAP_SEMS))
    got_shape = jax.ShapeDtypeStruct((swap.shape[0], swap.shape[1] // 2, 128), swap.dtype)
    return pl.pallas_call(body, name=name, grid_spec=grid_spec, out_shape=(*out_shape, got_shape),
                          compiler_params=_params(("arbitrary", "arbitrary")))(qi, kj, fl, q, k, v, q2, k2, o, do, lse, swap)


LOG2E = 1.4426950408889634


def _proj_qkv(h1, w_qkv, name, *, tm=1024):
    S, K = h1.shape
    tm = _tile(S, tm, 16)
    SQ, SK = 3, 4

    def heads(t):
        return [t[:, h * HD:(h + 1) * HD] for h in range(NH)]

    def body(a_ref, b_ref, o_ref, q2_ref, k2_ref):
        j = pl.program_id(1)
        ob = _dot(a_ref[...], b_ref[...]).astype(CDT)
        for h, t in enumerate(heads(ob)):
            o_ref[h] = t

        @pl.when(j == SQ)
        def _():
            qf = ob.astype(F32) * (ATT_SCALE * LOG2E)
            hi = qf.astype(CDT)
            lo = (qf - hi.astype(F32)).astype(CDT)
            for h, (th, tl) in enumerate(zip(heads(hi), heads(lo))):
                q2_ref[h] = jnp.concatenate([th, tl], axis=1)

        @pl.when(j == SK)
        def _():
            for h, t in enumerate(heads(ob)):
                k2_ref[h] = jnp.concatenate([t, t], axis=1)

    wide = pl.BlockSpec((NH, tm, 2 * HD), lambda i, j: (0, i, 0))
    return pl.pallas_call(
        body, name=name, grid=(S // tm, 6),
        in_specs=[pl.BlockSpec((tm, K), lambda i, j: (i, 0)), pl.BlockSpec((K, GW), lambda i, j: (0, j))],
        out_specs=(pl.BlockSpec((NH, tm, HD), lambda i, j: (j, i, 0)), wide, wide),
        out_shape=(jax.ShapeDtypeStruct((6 * NH, S, HD), CDT), jax.ShapeDtypeStruct((NH, S, 2 * HD), CDT),
                   jax.ShapeDtypeStruct((NH, S, 2 * HD), CDT)),
        compiler_params=_params(("parallel", "arbitrary")))(h1, w_qkv)


def _sb_softplus2(q2, k2sub, mask):
    z2 = _dot_nt(q2, k2sub)
    sp2 = jnp.maximum(z2, 0.0) + jnp.log2(1.0 + jnp.exp2(-jnp.abs(z2)))
    return z2, sp2 if mask is None else jnp.where(mask, sp2, 0.0)


def _strict_tri(n, upper, value):
    r = lax.broadcasted_iota(jnp.int32, (n, n), 0)
    c = lax.broadcasted_iota(jnp.int32, (n, n), 1)
    return jnp.where(r < c if upper else r > c, value, 0.0).astype(CDT)


def _sb_fwd(q2, k2, v, name, v0=0):
    S = q2.shape[1]
    tq, tk = _att_tiles("sb_fwd", S)
    W = min(W_SB, tk)
    qi, kj, fl = _pairs(S, tq, tk)
    qblk, kblk, qcol, _ = _att_specs(tq, tk)
    q2blk, k2blk, _, _ = _att_specs(tq, tk, 2 * HD)

    def body(qi_ref, kj_ref, fl_ref, q_ref, k_ref, v_ref, o_ref, lt_ref, run_s, acc_s):
        n = pl.program_id(1)
        i, kb, flags = qi_ref[n], kj_ref[n], fl_ref[n]

        @pl.when(flags & FIRST != 0)
        def _():
            run_s[...] = jnp.zeros_like(run_s)
            acc_s[...] = jnp.zeros_like(acc_s)

        def step(masked):
            neg_later = _strict_tri(W, False, -1.0)
            run = run_s[...]
            acc = acc_s[...]
            for sub in range(tk // W - 1, -1, -1):
                cols = slice(sub * W, (sub + 1) * W)
                mask = _causal(tq, W, i * tq - kb * tk - sub * W, True) if masked else None
                z2, sp2 = _sb_softplus2(q_ref[0], k_ref[0, cols, :], mask)
                excl = _dot(sp2.astype(CDT), neg_later)
                a = jnp.exp2((z2 - sp2) + (excl + run))
                if masked:
                    a = jnp.where(mask, a, 0.0)
                acc = acc + _dot(a.astype(CDT), v_ref[0, cols, :])
                run = run + (excl[:, 0:1] - sp2[:, 0:1])
            run_s[...] = run
            acc_s[...] = acc

        _masked_or_not(flags, step)

        @pl.when(flags & LAST != 0)
        def _():
            o_ref[0] = acc_s[...]
            lt_ref[0] = run_s[...]

    grid_spec = pltpu.PrefetchScalarGridSpec(
        num_scalar_prefetch=3, grid=(NH, int(qi.shape[0])), in_specs=[q2blk, k2blk, _head_blk(tk, True, v0)], out_specs=(qblk, qcol),
        scratch_shapes=[pltpu.VMEM((tq, 1), F32), pltpu.VMEM((tq, HD), F32)])
    return pl.pallas_call(body, name=name, grid_spec=grid_spec,
                          out_shape=(jax.ShapeDtypeStruct((NH, S, HD), F32), jax.ShapeDtypeStruct((NH, S, 1), F32)),
                          compiler_params=_params(("parallel", "arbitrary")))(qi, kj, fl, q2, k2, v)


def _sb_bwd(q, q2, k2, v, do, lt, name, scatter=None, q0=0, v0=0):
    S = q.shape[1]
    tq, tk = _att_tiles("sb_bwd", S)
    W = min(W_SB, tk)
    qi, kj, fl = _pairs(S, tq, tk, descending=False)
    qblk, kblk, qcol, _ = _att_specs(tq, tk)
    q2blk, k2blk, _, _ = _att_specs(tq, tk, 2 * HD)
    npairs = int(qi.shape[0])

    def body(qi_ref, kj_ref, fl_ref, q_ref, q2_ref, k2_ref, v_ref, do_ref, lt_ref, *rest):
        if scatter is None:
            dq_ref, dk_ref, dv_ref, passed_s, gsum_s, dq_s = rest
        else:
            h_ref, dq_ref, dk_ref, dv_ref, recv_ref, passed_s, gsum_s, dq_s, send_sems, recv_sems = rest
        n = pl.program_id(1)
        i, kb, flags = qi_ref[n], kj_ref[n], fl_ref[n]
        if scatter is not None:
            h = pl.program_id(0)
            _scatter_steps(h_ref, recv_ref, send_sems, recv_sems, first=(h == 0) & (n == 0), last=(h == NH - 1) & (n == npairs - 1))

        @pl.when(n == 0)
        def _():
            dk_ref[...] = jnp.zeros_like(dk_ref)
            dv_ref[...] = jnp.zeros_like(dv_ref)

        @pl.when(flags & FIRST != 0)
        def _():
            passed_s[...] = jnp.zeros_like(passed_s)
            gsum_s[...] = jnp.zeros_like(gsum_s)
            dq_s[...] = jnp.zeros_like(dq_s)

        def step(masked):
            qs = q_ref[0] * ATT_SCALE
            do = do_ref[0]
            neg_later = _strict_tri(W, False, -1.0)
            earlier = _strict_tri(W, True, 1.0)
            for sub in range(tk // W):
                cols = slice(sub * W, (sub + 1) * W)
                mask = _causal(tq, W, i * tq - kb * tk - sub * W, True) if masked else None
                ksub = k2_ref[0, cols, 0:HD]
                z2, sp2 = _sb_softplus2(q2_ref[0], k2_ref[0, cols, :], mask)
                excl = _dot(sp2.astype(CDT), neg_later)
                through = passed_s[...] + (excl[:, 0:1] - sp2[:, 0:1])
                t1 = z2 - sp2
                sig = jnp.exp2(t1)
                a = jnp.exp2(t1 + (excl + (lt_ref[0] - through)))
                if masked:
                    a = jnp.where(mask, a, 0.0)
                dl = _dot_nt(do, v_ref[0, cols, :]) * a
                before = _dot(dl.astype(CDT), earlier)
                dz = dl - sig * (dl + (before + gsum_s[...]))
                if masked:
                    dz = jnp.where(mask, dz, 0.0)
                dzb = dz.astype(CDT)
                dq_s[...] += _dot(dzb, ksub)
                rows = pl.ds(pl.multiple_of(kb * tk + sub * W, W), W)
                dk_ref[0, rows, :] += _dot_tn(dzb, qs)
                dv_ref[0, rows, :] += _dot_tn(a.astype(CDT), do)
                passed_s[...] = through
                gsum_s[...] += before[:, W - 1:W] + dl[:, W - 1:W]

        _masked_or_not(flags, step)

        @pl.when(flags & LAST != 0)
        def _():
            dq_ref[0] = (dq_s[...] * ATT_SCALE).astype(dq_ref.dtype)

    whole = pl.BlockSpec((1, S, HD), lambda h, n, qi, kj, fl: (h, 0, 0))
    in_specs = [_head_blk(tq, False, q0), q2blk, k2blk, _head_blk(tk, True, v0), qblk, qcol]
    out_specs = (qblk, whole, whole)
    out_shape = (jax.ShapeDtypeStruct((NH, S, HD), CDT), jax.ShapeDtypeStruct((NH, S, HD), F32), jax.ShapeDtypeStruct((NH, S, HD), F32))
    scratch = [pltpu.VMEM((tq, 1), F32), pltpu.VMEM((tq, 1), F32), pltpu.VMEM((tq, HD), F32)]
    if scatter is None:
        grid_spec = pltpu.PrefetchScalarGridSpec(num_scalar_prefetch=3, grid=(NH, npairs), in_specs=in_specs,
                                                 out_specs=out_specs, scratch_shapes=scratch)
        return (*pl.pallas_call(body, name=name, grid_spec=grid_spec, out_shape=out_shape,
                                compiler_params=_params(("parallel", "arbitrary")))(qi, kj, fl, q, q2, k2, v, do, lt), None)
    grid_spec = pltpu.PrefetchScalarGridSpec(num_scalar_prefetch=3, grid=(NH, npairs), in_specs=in_specs + [ANY],
                                             out_specs=(*out_specs, ANY), scratch_shapes=scratch + list(SCATTER_SEMS))
    recv_shape = jax.ShapeDtypeStruct((3,) + scatter.shape[1:], scatter.dtype)
    return pl.pallas_call(body, name=name, grid_spec=grid_spec, out_shape=(*out_shape, recv_shape),
                          compiler_params=_params(("arbitrary", "arbitrary")))(qi, kj, fl, q, q2, k2, v, do, lt, scatter)


def _mem_probs(q_ref, kv_ref, h):
    cols = slice(h * MHD, (h + 1) * MHD)
    s = _dot_nt(q_ref[:, cols], kv_ref[:, cols]) * MEM_SCALE
    e = jnp.exp(s - jnp.max(s, axis=-1, keepdims=True))
    return e / jnp.sum(e, axis=-1, keepdims=True)


def _xattn_fwd(q, kv, w_mo, x1, name, *, tm=512):
    S = q.shape[0]
    tm = _tile(S, tm, 16)
    nm = kv.shape[0]

    def body(q_ref, kv_ref, w_ref, x_ref, x2_ref, o_ref):
        for h in range(NMH):
            p = _mem_probs(q_ref, kv_ref, h)
            o_ref[:, h * MHD:(h + 1) * MHD] = _dot(p.astype(CDT), kv_ref[:, D + h * MHD:D + (h + 1) * MHD]).astype(CDT)
        x2_ref[...] = x_ref[...] + _dot(o_ref[...], w_ref[...])

    row = pl.BlockSpec((tm, D), lambda i: (i, 0))
    return pl.pallas_call(body, name=name, grid=(S // tm,),
                          in_specs=[row, pl.BlockSpec((nm, 2 * D), lambda i: (0, 0)), pl.BlockSpec((D, D), lambda i: (0, 0)), row],
                          out_specs=(row, row),
                          out_shape=(jax.ShapeDtypeStruct((S, D), F32), jax.ShapeDtypeStruct((S, D), CDT)),
                          compiler_params=_params(("parallel",)))(q, kv, w_mo, x1)


def _xattn_bwd(q, kv, do, name, *, tm=512):
    S = q.shape[0]
    tm = _tile(S, tm, 16)
    nm = kv.shape[0]

    def body(q_ref, kv_ref, do_ref, dq_ref, dkv_ref):
        @pl.when(pl.program_id(0) == 0)
        def _():
            dkv_ref[...] = jnp.zeros_like(dkv_ref)
        for h in range(NMH):
            cols = slice(h * MHD, (h + 1) * MHD)
            vcols = slice(D + h * MHD, D + (h + 1) * MHD)
            p = _mem_probs(q_ref, kv_ref, h)
            doh = do_ref[:, cols]
            dp = _dot_nt(doh, kv_ref[:, vcols])
            ds = (p * (dp - jnp.sum(p * dp, axis=-1, keepdims=True)) * MEM_SCALE).astype(CDT)
            dq_ref[:, cols] = _dot(ds, kv_ref[:, cols]).astype(CDT)
            dkv_ref[:, cols] += _dot_tn(ds, q_ref[:, cols])
            dkv_ref[:, vcols] += _dot_tn(p.astype(CDT), doh)

    row = pl.BlockSpec((tm, D), lambda i: (i, 0))
    kvs = pl.BlockSpec((nm, 2 * D), lambda i: (0, 0))
    return pl.pallas_call(body, name=name, grid=(S // tm,), in_specs=[row, kvs, row], out_specs=(row, kvs),
                          out_shape=(jax.ShapeDtypeStruct((S, D), CDT), jax.ShapeDtypeStruct((nm, 2 * D), F32)),
                          compiler_params=_params(("arbitrary",)))(q, kv, do)


HALO = 16
SLAB = 8


def _shift_down(u, prev, s):
    rolled = pltpu.roll(u, s, 0)
    top = rolled[0:SLAB]
    r = lax.broadcasted_iota(jnp.int32, top.shape, 0)
    for t in range(s):
        top = jnp.where(r == t, prev[HALO - s + t:HALO - s + t + 1, :], top)
    return jnp.concatenate([top, rolled[SLAB:]], axis=0)


def _shift_up(u, nxt, s):
    n = u.shape[0]
    rolled = pltpu.roll(u, n - s, 0)
    bottom = rolled[n - SLAB:]
    r = lax.broadcasted_iota(jnp.int32, bottom.shape, 0)
    for t in range(s):
        bottom = jnp.where(r == SLAB - s + t, nxt[t:t + 1, :], bottom)
    return jnp.concatenate([rolled[:n - SLAB], bottom], axis=0)


def _conv_taps(u_ref, h_ref, first):
    u = u_ref[...].astype(F32)
    prev = jnp.where(first, 0.0, h_ref[...].astype(F32))
    out = []
    for half in range(2):
        out.append((u[half], _shift_down(u[half], prev[half], 1), _shift_down(u[half], prev[half], 2)))
    return out


def _conv_specs(tm, tn, nsb):
    blk = pl.BlockSpec((2, tm, tn), lambda j, i: (0, i, j))
    prev = pl.BlockSpec((2, HALO, tn), lambda j, i: (0, jnp.maximum(i * (tm // HALO) - 1, 0), j))
    nxt = pl.BlockSpec((2, HALO, tn), lambda j, i: (0, jnp.minimum((i + 1) * (tm // HALO), nsb - 1), j))
    w = pl.BlockSpec((2, 3, tn), lambda j, i: (0, 0, j))
    b = pl.BlockSpec((2, 1, tn), lambda j, i: (0, 0, j))
    return blk, prev, nxt, w, b


def _conv_apply(taps, w_ref, b_ref):
    ys = []
    for half in range(2):
        u, u1, u2 = taps[half]
        w = w_ref[half]
        ys.append(b_ref[half] + u2 * w[0:1, :] + u1 * w[1:2, :] + u * w[2:3, :])
    return ys


def _conv_act(u0, cw, cb, name, *, tm=2048, tn=256):
    _, S, F = u0.shape
    tm = _tile(S, tm, HALO)
    tn = _tile(F, tn, 128)
    blk, prev, _, w, b = _conv_specs(tm, tn, S // HALO)

    def body(u_ref, h_ref, w_ref, b_ref, a_ref):
        yg, yv = _conv_apply(_conv_taps(u_ref, h_ref, pl.program_id(1) == 0), w_ref, b_ref)
        a_ref[...] = (yg * jax.nn.sigmoid(yg) * yv).astype(a_ref.dtype)

    return pl.pallas_call(body, name=name, grid=(F // tn, S // tm), in_specs=[blk, prev, w, b],
                          out_specs=pl.BlockSpec((tm, tn), lambda j, i: (i, j)),
                          out_shape=jax.ShapeDtypeStruct((S, F), CDT),
                          compiler_params=_params(("parallel", "parallel")))(u0, u0, cw, cb)


def _conv_act_bwd(u0, da, cw, cb, name, *, tm=2048, tn=256):
    _, S, F = u0.shape
    tm = _tile(S, tm, HALO)
    tn = _tile(F, tn, 128)
    blk, prev, _, w, b = _conv_specs(tm, tn, S // HALO)

    def body(u_ref, h_ref, da_ref, w_ref, b_ref, du_ref, dwb_ref):
        @pl.when(pl.program_id(1) == 0)
        def _():
            dwb_ref[...] = jnp.zeros_like(dwb_ref)
        taps = _conv_taps(u_ref, h_ref, pl.program_id(1) == 0)
        yg, yv = _conv_apply(taps, w_ref, b_ref)
        sg = jax.nn.sigmoid(yg)
        da = da_ref[...].astype(F32)
        dus = (da * yv * sg * (1.0 + yg * (1.0 - sg)), da * yg * sg)
        for half in range(2):
            du = dus[half]
            du_ref[half] = du.astype(du_ref.dtype)
            u, u1, u2 = taps[half]
            for row, term in enumerate((du * u2, du * u1, du * u, du)):
                dwb_ref[half, row:row + 1, :] += jnp.sum(term, axis=0, keepdims=True)

    return pl.pallas_call(body, name=name, grid=(F // tn, S // tm),
                          in_specs=[blk, prev, pl.BlockSpec((tm, tn), lambda j, i: (i, j)), w, b],
                          out_specs=(blk, pl.BlockSpec((2, 4, tn), lambda j, i: (0, 0, j))),
                          out_shape=(jax.ShapeDtypeStruct((2, S, F), CDT), jax.ShapeDtypeStruct((2, 4, F), F32)),
                          compiler_params=_params(("parallel", "arbitrary")))(u0, u0, da, cw, cb)


def _conv_bwd_input(du, cw, name, *, tm=2048, tn=256):
    _, S, F = du.shape
    tm = _tile(S, tm, HALO)
    tn = _tile(F, tn, 128)
    blk, _, nxt, w, _ = _conv_specs(tm, tn, S // HALO)
    ni = S // tm

    def body(d_ref, h_ref, w_ref, o_ref):
        d = d_ref[...].astype(F32)
        nx = jnp.where(pl.program_id(1) == ni - 1, 0.0, h_ref[...].astype(F32))
        for half in range(2):
            wv = w_ref[half]
            y = d[half] * wv[2:3, :] + _shift_up(d[half], nx[half], 1) * wv[1:2, :] + _shift_up(d[half], nx[half], 2) * wv[0:1, :]
            o_ref[half] = y.astype(o_ref.dtype)

    return pl.pallas_call(body, name=name, grid=(F // tn, ni), in_specs=[blk, nxt, w], out_specs=blk,
                          out_shape=jax.ShapeDtypeStruct((2, S, F), CDT),
                          compiler_params=_params(("parallel", "parallel")))(du, du, cw)


ANY = pl.BlockSpec(memory_space=pl.ANY)


def _place():
    return lax.axis_index("x"), lax.axis_index("y"), lax.axis_index("c")


def _other_chips(x, y):
    return ((1 - x, y), (x, 1 - y), (1 - x, 1 - y))


def _when(pred, fn):
    if pred is True:
        fn()
    else:
        pl.when(pred)(fn)


GATHER_SEMS = (pltpu.SemaphoreType.DMA((6,)), pltpu.SemaphoreType.DMA((6,)))
SCATTER_SEMS = (pltpu.SemaphoreType.DMA((3,)), pltpu.SemaphoreType.DMA((3,)))


def _gather_steps(out_ref, send_sems, recv_sems, first=True, middle=True, last=True):
    half = out_ref.shape[1] // 2
    x, y, c = _place()
    chips = _other_chips(x, y)

    def part(chip, pc):
        return out_ref.at[2 * chip[0] + chip[1], pl.ds(pl.multiple_of(pc * half, 16), half), :]

    def copy(k, chip, pc, to):
        return pltpu.make_async_remote_copy(src_ref=part(chip, pc), dst_ref=part(chip, pc),
                                            send_sem=send_sems.at[k], recv_sem=recv_sems.at[k],
                                            device_id=to, device_id_type=MESH)

    def send_mine():
        for j, chip in enumerate(chips):
            copy(j, (x, y), c, (*chip, c)).start()

    def pass_on():
        for j, chip in enumerate(chips):
            copy(j, chip, c, (x, y, c)).wait_recv()
            copy(3 + j, chip, c, (x, y, 1 - c)).start()

    def finish():
        for j, chip in enumerate(chips):
            copy(3 + j, chip, 1 - c, (x, y, c)).wait_recv()
        for j, chip in enumerate(chips):
            copy(j, (x, y), c, (*chip, c)).wait_send()
            copy(3 + j, chip, c, (x, y, 1 - c)).wait_send()

    _when(first, send_mine)
    _when(middle, pass_on)
    _when(last, finish)


def _gather_weights(buf):
    def body(buf_ref, out_ref, send_sems, recv_sems):
        del buf_ref
        _gather_steps(out_ref, send_sems, recv_sems)

    return pl.pallas_call(body, name="gather_weights", in_specs=[ANY], out_specs=ANY,
                          out_shape=jax.ShapeDtypeStruct(buf.shape, buf.dtype), input_output_aliases={0: 0},
                          scratch_shapes=list(GATHER_SEMS))(buf)


def _gather_small(v):
    m = v.shape[0]

    def body(v_ref, out_ref, send_sems, recv_sems, local_sem):
        x, y, c = _place()
        me, sibling = (x, y, c), (x, y, 1 - c)
        chips = _other_chips(x, y)

        def rows(px, py, pc):
            return out_ref.at[pl.ds((4 * px + 2 * py + pc) * m, m), :]

        def copy(k, block, to, src=None):
            return pltpu.make_async_remote_copy(src_ref=rows(*block) if src is None else src, dst_ref=rows(*block),
                                                send_sem=send_sems.at[k], recv_sem=recv_sems.at[k],
                                                device_id=to, device_id_type=MESH)

        mine = pltpu.make_async_copy(v_ref, rows(*me), local_sem)
        mine.start()
        first = [copy(0, me, sibling, src=v_ref)]
        first += [copy(1 + j, me, (*chip, c), src=v_ref) for j, chip in enumerate(chips)]
        for cp in first:
            cp.start()
        passed = [copy(4 + j, (*chip, c), sibling) for j, chip in enumerate(chips)]
        for j, chip in enumerate(chips):
            copy(1 + j, (*chip, c), me).wait_recv()
            passed[j].start()
        copy(0, sibling, me).wait_recv()
        for j, chip in enumerate(chips):
            copy(4 + j, (*chip, 1 - c), me).wait_recv()
        for cp in first + passed:
            cp.wait_send()
        mine.wait()

    vm = pl.BlockSpec(memory_space=pltpu.VMEM)
    return pl.pallas_call(body, name="gather_small", in_specs=[vm], out_specs=vm,
                          out_shape=jax.ShapeDtypeStruct((8 * m, 128), v.dtype),
                          scratch_shapes=[pltpu.SemaphoreType.DMA((7,)), pltpu.SemaphoreType.DMA((7,)), pltpu.SemaphoreType.DMA])(v)


SWAP_SEMS = (pltpu.SemaphoreType.DMA, pltpu.SemaphoreType.DMA)


def _swap_steps(g_ref, out_ref, send_sem, recv_sem, first=True, last=True):
    half = out_ref.shape[1]
    x, y, c = _place()

    def copy():
        src = g_ref.at[:, pl.ds(pl.multiple_of((1 - c) * half, 8), half), :]
        return pltpu.make_async_remote_copy(src_ref=src, dst_ref=out_ref, send_sem=send_sem, recv_sem=recv_sem,
                                            device_id=(x, y, 1 - c), device_id_type=MESH)

    _when(first, lambda: copy().start())
    _when(last, lambda: copy().wait())


def _swap_halves(g, name):
    n, rows, _ = g.shape

    def body(g_ref, out_ref, send_sem, recv_sem):
        _swap_steps(g_ref, out_ref, send_sem, recv_sem)

    return pl.pallas_call(body, name=name, in_specs=[ANY], out_specs=ANY,
                          out_shape=jax.ShapeDtypeStruct((n, rows // 2, 128), g.dtype),
                          scratch_shapes=list(SWAP_SEMS))(g)


def _scatter_steps(h_ref, out_ref, send_sems, recv_sems, first=True, last=True):
    x, y, c = _place()

    def copies():
        return [pltpu.make_async_remote_copy(src_ref=h_ref.at[2 * chip[0] + chip[1]], dst_ref=out_ref.at[j],
                                             send_sem=send_sems.at[j], recv_sem=recv_sems.at[j],
                                             device_id=(*chip, c), device_id_type=MESH)
                for j, chip in enumerate(_other_chips(x, y))]

    def start():
        for cp in copies():
            cp.start()

    def finish():
        for cp in copies():
            cp.wait()

    _when(first, start)
    _when(last, finish)


def _scatter_chips(hsum):
    n, half, _ = hsum.shape

    def body(h_ref, out_ref, send_sems, recv_sems):
        _scatter_steps(h_ref, out_ref, send_sems, recv_sems)

    return pl.pallas_call(body, name="scatter_chips", in_specs=[ANY], out_specs=ANY,
                          out_shape=jax.ShapeDtypeStruct((3, half, 128), hsum.dtype),
                          scratch_shapes=list(SCATTER_SEMS))(hsum)


def _join_halves(buf, name):
    half = buf.shape[0] // 2

    def body(buf_ref, out_ref, send_sem, recv_sem):
        del buf_ref
        x, y, c = _place()
        mine = out_ref.at[pl.ds(pl.multiple_of(c * half, 8), half), :]
        other = out_ref.at[pl.ds(pl.multiple_of((1 - c) * half, 8), half), :]
        cp = pltpu.make_async_remote_copy(src_ref=mine, dst_ref=mine, send_sem=send_sem, recv_sem=recv_sem,
                                          device_id=(x, y, 1 - c), device_id_type=MESH)
        cp.start()
        cp.wait_send()
        pltpu.make_async_remote_copy(src_ref=other, dst_ref=other, send_sem=send_sem, recv_sem=recv_sem,
                                     device_id=(x, y, 1 - c), device_id_type=MESH).wait_recv()

    return pl.pallas_call(body, name=name, in_specs=[ANY], out_specs=ANY,
                          out_shape=jax.ShapeDtypeStruct(buf.shape, buf.dtype), input_output_aliases={0: 0},
                          scratch_shapes=[pltpu.SemaphoreType.DMA, pltpu.SemaphoreType.DMA])(buf)


def _add_sibling(g, recv, c_idx, name):
    n, rows, _ = g.shape
    half = rows // 2
    tr = _tile(half, ADAM_ROWS, 16)
    nb = half // tr

    def body(c_ref, g_ref, r_ref, o_ref, ob_ref):
        s = g_ref[...] + r_ref[...]
        o_ref[...] = s
        ob_ref[...] = s.astype(CDT)

    out = pl.BlockSpec((None, tr, 128), lambda k, i, c: (k, i, 0))
    grid_spec = pltpu.PrefetchScalarGridSpec(
        num_scalar_prefetch=1, grid=(n, nb),
        in_specs=[pl.BlockSpec((None, tr, 128), lambda k, i, c: (k, c[0] * nb + i, 0)), out],
        out_specs=(out, out))
    return pl.pallas_call(body, name=name, grid_spec=grid_spec,
                          out_shape=(jax.ShapeDtypeStruct((n, half, 128), F32), jax.ShapeDtypeStruct((n, half, 128), CDT)),
                          compiler_params=_params(("parallel", "parallel")))(c_idx, g, recv)


def _add_chips(hsum, recv, chip_idx, name):
    n, half, _ = hsum.shape
    tr = _tile(half, ADAM_ROWS, 16)

    def body(k_ref, h_ref, r_ref, o_ref):
        o_ref[...] = ((h_ref[...] + r_ref[0].astype(F32)) + r_ref[1].astype(F32)) + r_ref[2].astype(F32)

    grid_spec = pltpu.PrefetchScalarGridSpec(
        num_scalar_prefetch=1, grid=(half // tr,),
        in_specs=[pl.BlockSpec((None, tr, 128), lambda i, k: (k[0], i, 0)),
                  pl.BlockSpec((3, tr, 128), lambda i, k: (0, i, 0))],
        out_specs=pl.BlockSpec((tr, 128), lambda i, k: (i, 0)))
    return pl.pallas_call(body, name=name, grid_spec=grid_spec, out_shape=jax.ShapeDtypeStruct((half, 128), F32),
                          compiler_params=_params(("parallel",)))(chip_idx, hsum, recv)


def _adamw_math(g, w, m, v):
    m2 = B1 * m + (1.0 - B1) * g
    v2 = B2 * v + (1.0 - B2) * (g * g)
    delta = -LR * ((m2 / BC1) / (jnp.sqrt(v2 / BC2) + AEPS) + WD * w)
    return delta, m2, v2


def _adamw(g, w, m, v, name):
    rows, cols = g.shape
    tr = _tile(rows, max(8, (ADAM_ROWS * 128 // cols) // 8 * 8), 8)

    def body(g_ref, w_ref, m_ref, v_ref, d_ref, m2_ref, v2_ref):
        d_ref[...], m2_ref[...], v2_ref[...] = _adamw_math(g_ref[...], w_ref[...], m_ref[...], v_ref[...])

    blk = pl.BlockSpec((None, tr, cols), lambda i: (0, i, 0))
    shp = jax.ShapeDtypeStruct((1, rows, cols), F32)
    return pl.pallas_call(body, name=name, grid=(rows // tr,), in_specs=[pl.BlockSpec((tr, cols), lambda i: (i, 0))] + [blk] * 3,
                          out_specs=(blk,) * 3, out_shape=(shp,) * 3, compiler_params=_params(("parallel",)))(g, w, m, v)


def _adamw_small(parts, w, m, v, name):
    rows = w.shape[0]

    def body(p_ref, w_ref, m_ref, v_ref, g_ref, d_ref, m2_ref, v2_ref):
        g = p_ref[0]
        for k in range(1, 8):
            g = g + p_ref[k]
        g_ref[...] = g
        d_ref[...], m2_ref[...], v2_ref[...] = _adamw_math(g, w_ref[...], m_ref[...], v_ref[...])

    shp = jax.ShapeDtypeStruct((rows, 128), F32)
    return pl.pallas_call(body, name=name, out_shape=(shp,) * 4)(parts, w, m, v)


def _pack_rows(parts, rows):
    flat = jnp.concatenate([p.reshape(-1) for p in parts])
    return jnp.pad(flat, (0, rows * 128 - flat.shape[0])).reshape(rows, 128)


def _unpack(flat, sizes, shapes):
    out, off = [], 0
    for n, s in zip(sizes, shapes):
        out.append(flat[off:off + n].reshape(s))
        off += n
    return out


def _to_shards(full, shard_shape, axis):
    if axis == 0:
        return full.reshape(N_CHIP, -1)
    r, cs = shard_shape
    return full.reshape(r, N_CHIP, cs).transpose(1, 0, 2).reshape(N_CHIP, -1)


def _from_shards(sh, shard_shape, axis):
    r, cs = shard_shape
    if axis == 0:
        return sh.reshape(N_CHIP * r, cs)
    return sh.reshape(N_CHIP, r, cs).transpose(1, 0, 2).reshape(r, N_CHIP * cs)


def _local_step(x0, mem, tgt, W, gains, ex=None):
    S = x0.shape[0]
    w_in = jnp.pad(W["w_in"], ((0, 0), (0, IN_PAD - IN_COLS)))
    b_f = jnp.pad(gains["b_forget"], ((0, 0), (0, 128 - NH)))

    h1 = _rms_cast(x0, gains["attn_norm_g"], "norm_attn")
    qkv, sq2, sk2 = _proj_qkv(h1, w_in[:, :NQKV], "proj_qkv")
    fox, sb_q, sb_v = (0, NH, 2 * NH), 3 * NH, 5 * NH
    fl = _mm_nn(h1, w_in[:, NQKV:NQKV + 128], F32, "proj_gate")
    cum = _gate_fwd(fl, b_f, "gate_cumsum")
    fq2, fk2 = _fox_operands(qkv, cum, "fox_operands")
    fo_h, lse, gathered = _fox_fwd(fq2, fk2, qkv, "fox_fwd", slots=None if ex is None else ex.slots("b"), v0=fox[2])
    if ex is not None:
        W = {**W, **ex.unpack("b", gathered)}
    cw = W["conv_w"].reshape(3, 2, DFF).transpose(1, 0, 2)
    cb = gains["conv_b"].reshape(2, 1, DFF)
    so_h, s_lt = _sb_fwd(sq2, sk2, qkv, "sb_fwd", v0=sb_v)
    x1, mixed = _out_proj(fo_h, so_h, gains["fox_out_g"], gains["sb_out_g"], W["w_out"], x0, "out_proj")

    h2 = _rms_cast(x1, gains["xattn_norm_g"], "norm_xattn")
    mn = _rms_cast(mem, gains["mem_norm_g"], "norm_mem")
    mq = _mm_nn(h2, W["w_mq"], CDT, "proj_mq")
    kv = _mm_nn(mn, W["w_mkv"], CDT, "proj_mkv")
    x2, mo = _xattn_fwd(mq, kv, W["w_mo"], x1, "xattn_fwd")

    h3 = _rms_cast(x2, gains["ffn_norm_g"], "norm_ffn")
    u0 = _mm_nn(h3, W["w_up"], CDT, "ffn_up", tm=512, tn=DFF, halves=True)
    act = _conv_act(u0, cw, cb, "conv_act")
    x3 = _mm_nn(act, W["w_down"], F32, "ffn_down", tm=512, residual=x2)
    loss, dx3, dg_final = _loss_bwd(x3, tgt, gains["final_norm_g"].reshape(1, D), "loss")

    gw, gs = {}, {"final_norm_g": dg_final}
    da = _mm_nt(dx3, W["w_down"], "ffn_down_dx", tn=DFF, out_dtype=CDT)
    gw["w_down"] = _mm_tn(act, dx3, "ffn_down_dw", tka=DFF)
    du, dwb = _conv_act_bwd(u0, da, cw, cb, "conv_act_bwd")
    gw["conv_w"] = dwb[:, :3].transpose(1, 0, 2).reshape(3, 2 * DFF)
    gs["conv_b"] = dwb[:, 3].reshape(1, 2 * DFF)
    du0 = _conv_bwd_input(du, cw, "conv_bwd_input")
    gw["w_up"] = _mm_tn(h3, du0, "ffn_up_dw", tn=DFF, b_halves=True)
    dx2, gs["ffn_norm_g"] = _mm_nt_rmsbwd(du0, W["w_up"], x2, gains["ffn_norm_g"], dx3, "ffn_up_dx", tk=DFF, a_halves=True)

    dmo = _mm_nt(dx2, W["w_mo"], "mo_dx", tn=512, out_dtype=CDT)
    gw["w_mo"] = _mm_tn(mo, dx2, "mo_dw")
    dmq, dkv = _xattn_bwd(mq, kv, dmo, "xattn_bwd")
    gw["w_mq"] = _mm_tn(h2, dmq, "mq_dw")
    dx1, gs["xattn_norm_g"] = _mm_nt_rmsbwd(dmq, W["w_mq"], x1, gains["xattn_norm_g"], dx2, "mq_dx")
    gw["w_mkv"] = _mm_tn(mn, dkv, "mkv_dw")
    _, gs["mem_norm_g"] = _mm_nt_rmsbwd(dkv, W["w_mkv"], mem, gains["mem_norm_g"], jnp.zeros_like(mem), "mkv_dx")

    gw["w_out"] = _mm_tn(mixed, dx1, "out_dw")
    dfo_h, dso_h, gs["fox_out_g"], gs["sb_out_g"] = _out_proj_bwd(dx1, W["w_out"], fo_h, so_h, gains["fox_out_g"], gains["sb_out_g"], "out_dx")
    flat = None if ex is None else ex.flat("b", gw)
    dfq, dfk, dfv, dck, dcq, got = _fox_bwd(qkv, qkv, qkv, fq2, fk2, fo_h, dfo_h, lse, "fox_bwd", h0=fox, swap=flat)
    pair, pair16 = (None, None) if ex is None else ex.pair_sums("b", flat, got)
    dsq, dsk, dsv, arrived = _sb_bwd(qkv, sq2, sk2, qkv, dso_h, s_lt, "sb_bwd", scatter=pair16, q0=sb_q, v0=sb_v)
    dfl, db = _gate_bwd(jnp.pad(dck[:, 0, :].T, ((0, 0), (0, 128 - NH))), dcq, fl, b_f, "gate_bwd")
    gs["b_forget"] = db[:, :NH]
    dqkv = jnp.concatenate([dfq, dfk.astype(CDT), dfv.astype(CDT), dsq, dsk.astype(CDT), dsv.astype(CDT)], axis=0)
    dproj = jnp.concatenate([dqkv.transpose(1, 0, 2).reshape(S, NQKV), dfl.astype(CDT),
                             jnp.zeros((S, IN_PAD - NQKV - 128), CDT)], axis=1)
    gw["w_in"] = _mm_tn(h1, dproj, "in_dw", tn=IN_PAD)[:, :IN_COLS]
    dx0, gs["attn_norm_g"] = _mm_nt_rmsbwd(dproj, w_in, x0, gains["attn_norm_g"], dx1, "in_dx", tk=IN_PAD)
    return loss, dx0, gw, gs, (pair, arrived)


NAMES = ("attn_norm_g", "w_in", "b_forget", "fox_out_g", "sb_out_g", "w_out", "xattn_norm_g", "mem_norm_g", "w_mq",
         "w_mkv", "w_mo", "ffn_norm_g", "w_up", "conv_w", "conv_b", "w_down", "final_norm_g")


class _Exchange:
    def __init__(self, w):
        self.w = w
        xi, yi, ci = _place()
        self.core = ci
        self.chip = 2 * xi + yi
        self.core_idx = jnp.reshape(ci, (1,)).astype(jnp.int32)
        self.chip_idx = jnp.reshape(self.chip, (1,)).astype(jnp.int32)

    def slots(self, g):
        parts = []
        for name, shape, _ in GROUPS[g]:
            blk = self.w[name].reshape(shape)
            parts.append(lax.bitcast_convert_type(blk, CDT) if name == "conv_w" else blk.astype(CDT))
        rows = _rows_g(GROUPS[g])
        return lax.dynamic_update_slice(lax.empty((N_CHIP, rows, 128), CDT), _pack_rows(parts, rows)[None], (self.chip, 0, 0))

    def unpack(self, g, gathered):
        flat, full, off = gathered.reshape(N_CHIP, -1), {}, 0
        for (name, shape, axis), n in zip(GROUPS[g], _gather_sizes(GROUPS[g])):
            sh = flat[:, off:off + n]
            off += n
            if name == "conv_w":
                sh = lax.bitcast_convert_type(sh.reshape(N_CHIP, n // 2, 2), F32)
            full[name] = _from_shards(sh, shape, axis)
        return full

    def flat(self, g, gw):
        rows = _rows_f(GROUPS[g])
        flat = jnp.concatenate([_to_shards(gw[name], shape, axis) for name, shape, axis in GROUPS[g]], axis=1)
        return jnp.pad(flat, ((0, 0), (0, rows * 128 - flat.shape[1]))).reshape(N_CHIP, rows, 128)

    def pair_sums(self, g, flat, got=None):
        if got is None:
            got = _swap_halves(flat, "swap_halves_" + g)
        return _add_sibling(flat, got, self.core_idx, "add_sibling_" + g)

    def finish(self, g, pair, arrived):
        rows = _rows_f(GROUPS[g])
        mine = _add_chips(pair, arrived, self.chip_idx, "add_chips_" + g)
        whole = _join_halves(lax.dynamic_update_slice(lax.empty((rows, 128), F32), mine, (self.core * (rows // 2), 0)), "join_halves_" + g)
        shapes = [s for _, s, _ in GROUPS[g]]
        return {name: arr for (name, _, _), arr in zip(GROUPS[g], _unpack(whole.reshape(-1), _sizes(GROUPS[g]), shapes))}


def _step(x, mem, loss_target, w, m, v):
    ex = _Exchange(w)

    W = ex.unpack("a", _gather_weights(ex.slots("a")))
    gains = {name: w[name].reshape(1, -1) for name, _ in SMALL}

    loss, grad_x, gw, gs, (pair_b, arrived_b) = _local_step(x[0], mem[0], loss_target[0], W, gains, ex)

    grads = ex.finish("b", pair_b, arrived_b)
    pair_a, pair16_a = ex.pair_sums("a", ex.flat("a", gw))
    grads.update(ex.finish("a", pair_a, _scatter_chips(pair16_a)))
    small = jnp.concatenate([gs[name].reshape(-1) for name, _ in SMALL] + [loss[0, :1]])
    small = jnp.pad(small, (0, ROWS_S * 128 - P_SMALL)).reshape(ROWS_S, 128)
    small_parts = _gather_small(small).reshape(8, ROWS_S, 128)

    def flat_small(d):
        return _pack_rows([d[name] for name, _ in SMALL], ROWS_S)

    outs = {}
    for name, shape, _ in BIG:
        g = grads[name]
        res = _adamw(g, w[name], m[name], v[name], "adamw_" + name)
        for prefix, arr in zip(("grad_", "delta_", "new_m_", "new_v_"), (g, *res)):
            outs[prefix + name] = arr.reshape(w[name].shape)
    small_res = _adamw_small(small_parts, flat_small(w), flat_small(m), flat_small(v), "adamw_small")
    g_sm = small_res[0]
    for prefix, sm in zip(("grad_", "delta_", "new_m_", "new_v_"), small_res):
        for (name, n), arr in zip(SMALL, _unpack(sm.reshape(-1), [n for _, n in SMALL], [(n,) for _, n in SMALL])):
            outs[prefix + name] = arr.reshape(w[name].shape)
    total_loss = g_sm.reshape(-1)[P_SMALL - 1]
    return (total_loss, grad_x[None], *[outs[p + n] for p in ("grad_", "delta_", "new_m_", "new_v_") for n in NAMES])


def kernel(x, mem, attn_norm_g, w_in, b_forget, fox_out_g, sb_out_g, w_out, xattn_norm_g, mem_norm_g, w_mq, w_mkv, w_mo, ffn_norm_g, w_up, conv_w, conv_b, w_down, final_norm_g, loss_target, m_attn_norm_g, m_w_in, m_b_forget, m_fox_out_g, m_sb_out_g, m_w_out, m_xattn_norm_g, m_mem_norm_g, m_w_mq, m_w_mkv, m_w_mo, m_ffn_norm_g, m_w_up, m_conv_w, m_conv_b, m_w_down, m_final_norm_g, v_attn_norm_g, v_w_in, v_b_forget, v_fox_out_g, v_sb_out_g, v_w_out, v_xattn_norm_g, v_mem_norm_g, v_w_mq, v_w_mkv, v_w_mo, v_ffn_norm_g, v_w_up, v_conv_w, v_conv_b, v_w_down, v_final_norm_g):
    given = dict(locals())
    w = {n: given[n] for n in NAMES}
    m = {n: given["m_" + n] for n in NAMES}
    v = {n: given["v_" + n] for n in NAMES}
    return _step(x, mem, loss_target, w, m, v)
```

```python
import functools

import numpy as np
import jax
import jax.numpy as jnp
from jax import lax
from jax.experimental import pallas as pl
from jax.experimental.pallas import tpu as pltpu

F32 = jnp.float32
CDT = jnp.bfloat16
MESH = pl.DeviceIdType.MESH

D = 1024
HD = 64
NH = 8
GW = NH * HD
NQKV = 6 * GW
IN_COLS = NQKV + NH
IN_PAD = NQKV + 256
NMH = 4
MHD = D // NMH
DFF = 2816
EPS = 1e-6
ATT_SCALE = HD ** -0.5
MEM_SCALE = MHD ** -0.5
NEG = -1e30

LR, B1, B2, AEPS, WD, STEP = 0.001, 0.9, 0.999, 1e-08, 0.01, 10
BC1 = 1.0 - B1 ** STEP
BC2 = 1.0 - B2 ** STEP

ATT_TILES = {"fox_fwd": (1024, 2048), "fox_bwd": (512, 1024), "sb_fwd": (1024, 1024), "sb_bwd": (1024, 1024)}
W_SB = 256
VMEM_LIMIT = 52 * 2 ** 20

N_CHIP = 4
BIG = (("w_in", (D, IN_COLS // N_CHIP), 1), ("w_out", (D // N_CHIP, D), 0), ("w_mq", (D // N_CHIP, D), 0),
       ("w_mkv", (D, 2 * D // N_CHIP), 1), ("w_mo", (D // N_CHIP, D), 0), ("w_up", (D, 2 * DFF // N_CHIP), 1),
       ("conv_w", (3, 2 * DFF // N_CHIP), 1), ("w_down", (DFF // N_CHIP, D), 0))
GROUPS = {"a": BIG[:1], "b": BIG[1:]}
ADAM_ROWS = 1536


def _sizes(group):
    return tuple(int(np.prod(s)) for _, s, _ in group)


def _gather_sizes(group):
    return tuple(2 * n if name == "conv_w" else n for (name, _, _), n in zip(group, _sizes(group)))


def _rows_g(group):
    return -(-sum(_gather_sizes(group)) // 4096) * 32


def _rows_f(group):
    return -(-sum(_sizes(group)) // 65536) * 512
SMALL = (("attn_norm_g", 1024), ("b_forget", 8), ("fox_out_g", 512), ("sb_out_g", 512), ("xattn_norm_g", 1024),
         ("mem_norm_g", 1024), ("ffn_norm_g", 1024), ("conv_b", 2 * DFF), ("final_norm_g", 1024))
P_SMALL = sum(n for _, n in SMALL) + 1
ROWS_S = -(-P_SMALL // 1024) * 8


def _params(sem=None, vmem=VMEM_LIMIT):
    return pltpu.CompilerParams(dimension_semantics=sem, vmem_limit_bytes=vmem)


def _tile(n, pref, mult):
    t = (min(pref, n) // mult) * mult
    while t >= mult:
        if n % t == 0:
            return t
        t -= mult
    return n


def _dot(a, b):
    return jnp.dot(a, b, preferred_element_type=F32)


def _dot_nt(a, b):
    return lax.dot_general(a, b, (((1,), (1,)), ((), ())), preferred_element_type=F32)


def _dot_tn(a, b):
    return lax.dot_general(a, b, (((0,), (0,)), ((), ())), preferred_element_type=F32)


def _split3(x):
    h1 = x.astype(CDT)
    r1 = x - h1.astype(F32)
    h2 = r1.astype(CDT)
    h3 = (r1 - h2.astype(F32)).astype(CDT)
    return h1, h2, h3


def _split2(x):
    h1 = x.astype(CDT)
    return h1, (x - h1.astype(F32)).astype(CDT)


def _rms_bwd(dh, x, g):
    r = lax.rsqrt(jnp.mean(x * x, axis=-1, keepdims=True) + EPS)
    xn = x * r
    dg = jnp.sum(dh * xn, axis=0, keepdims=True)
    dhg = dh * g
    dx = r * (dhg - xn * jnp.mean(dhg * xn, axis=-1, keepdims=True))
    return dx, dg


def _mm_nn(a, b, out_dtype, name, *, tm=1024, tn=512, residual=None, halves=False):
    M, K = a.shape
    N = b.shape[1]
    tm = _tile(M, tm, 16)
    tn = _tile(N // 2 if halves else N, tn, 128)
    nj = N // tn

    def body(*refs):
        a_ref, b_ref = refs[0], refs[1]
        o_ref = refs[-1]
        acc = _dot(a_ref[...].astype(CDT), b_ref[...].astype(CDT))
        if residual is not None:
            acc = acc + refs[2][...]
        o_ref[...] = acc.astype(o_ref.dtype)

    in_specs = [pl.BlockSpec((tm, K), lambda i, j: (i, 0)), pl.BlockSpec((K, tn), lambda i, j: (0, j))]
    ops = [a, b]
    if residual is not None:
        in_specs.append(pl.BlockSpec((tm, tn), lambda i, j: (i, j)))
        ops.append(residual)
    if halves:
        njh = nj // 2
        out_shape = jax.ShapeDtypeStruct((2, M, N // 2), out_dtype)
        out_spec = pl.BlockSpec((None, tm, tn), lambda i, j: (j // njh, i, j % njh))
    else:
        out_shape = jax.ShapeDtypeStruct((M, N), out_dtype)
        out_spec = pl.BlockSpec((tm, tn), lambda i, j: (i, j))
    return pl.pallas_call(body, name=name, grid=(M // tm, nj), in_specs=in_specs, out_specs=out_spec,
                          out_shape=out_shape, compiler_params=_params(("parallel", "parallel")))(*ops)


def _mm_tn(a, b, name, *, tka=512, tn=1024, ts=512, b_halves=False):
    S, Ka = a.shape
    N = 2 * b.shape[2] if b_halves else b.shape[1]
    tka = _tile(Ka, tka, 128)
    tn = _tile(N // 2 if b_halves else N, tn, 128)
    ts = _tile(S, ts, 16)
    nn = N // tn

    def body(a_ref, b_ref, o_ref):
        @pl.when(pl.program_id(2) == 0)
        def _():
            o_ref[...] = jnp.zeros_like(o_ref)
        o_ref[...] += _dot_tn(a_ref[...].astype(CDT), b_ref[...].astype(CDT))

    if b_halves:
        nnh = nn // 2
        b_spec = pl.BlockSpec((None, ts, tn), lambda i, j, s: (j // nnh, s, j % nnh))
    else:
        b_spec = pl.BlockSpec((ts, tn), lambda i, j, s: (s, j))
    return pl.pallas_call(
        body, name=name, grid=(Ka // tka, nn, S // ts),
        in_specs=[pl.BlockSpec((ts, tka), lambda i, j, s: (s, i)), b_spec],
        out_specs=pl.BlockSpec((tka, tn), lambda i, j, s: (i, j)),
        out_shape=jax.ShapeDtypeStruct((Ka, N), F32),
        compiler_params=_params(("parallel", "parallel", "arbitrary")))(a, b)


def _mm_nt(a, b, name, *, tm=512, tn=None, tk=None, a_halves=False, out_dtype=F32,
           epilogue=None, extra=(), extra_specs=(), out_shape=None, out_specs=None):
    if a_halves:
        M, K = a.shape[1], 2 * a.shape[2]
    else:
        M, K = a.shape
    N = b.shape[0]
    tm = _tile(M, tm, 16)
    tn = N if (epilogue is not None or tn is None) else _tile(N, tn, 128)
    tk = K if tk is None else _tile(K // 2 if a_halves else K, tk, 128)
    nk = K // tk
    n_extra = len(extra)

    def body(*refs):
        a_ref, b_ref = refs[0], refs[1]
        extra_refs = refs[2:2 + n_extra]
        out_refs = refs[2 + n_extra:-1]
        acc_ref = refs[-1]
        k = pl.program_id(2)

        @pl.when(k == 0)
        def _():
            acc_ref[...] = jnp.zeros_like(acc_ref)
        acc_ref[...] += _dot_nt(a_ref[...].astype(CDT), b_ref[...].astype(CDT))

        @pl.when(k == nk - 1)
        def _():
            if epilogue is None:
                out_refs[0][...] = acc_ref[...].astype(out_refs[0].dtype)
            else:
                epilogue(acc_ref[...], pl.program_id(0), extra_refs, out_refs)

    if a_halves:
        nkh = nk // 2
        a_spec = pl.BlockSpec((None, tm, tk), lambda i, j, k: (k // nkh, i, k % nkh))
    else:
        a_spec = pl.BlockSpec((tm, tk), lambda i, j, k: (i, k))
    if epilogue is None:
        out_shape = jax.ShapeDtypeStruct((M, N), out_dtype)
        out_specs = pl.BlockSpec((tm, tn), lambda i, j, k: (i, j))
        sem = ("parallel", "parallel", "arbitrary")
    else:
        sem = ("arbitrary", "arbitrary", "arbitrary")
    return pl.pallas_call(
        body, name=name, grid=(M // tm, N // tn, nk),
        in_specs=[a_spec, pl.BlockSpec((tn, tk), lambda i, j, k: (j, k)), *extra_specs],
        out_specs=out_specs, out_shape=out_shape,
        scratch_shapes=[pltpu.VMEM((tm, tn), F32)],
        compiler_params=_params(sem))(a, b, *extra)


def _mm_nt_rmsbwd(a, b, x, g, dres, name, *, tm=512, tk=None, a_halves=False):
    M = x.shape[0]
    tm = _tile(M, tm, 16)

    def epilogue(acc, i, extra_refs, out_refs):
        x_ref, g_ref, r_ref = extra_refs
        dx_ref, dg_ref = out_refs
        dx, dg = _rms_bwd(acc, x_ref[...], g_ref[...])
        dx_ref[...] = r_ref[...] + dx

        @pl.when(i == 0)
        def _():
            dg_ref[...] = jnp.zeros_like(dg_ref)
        dg_ref[...] += dg

    row = pl.BlockSpec((tm, D), lambda i, j, k: (i, 0))
    vec = pl.BlockSpec((1, D), lambda i, j, k: (0, 0))
    return _mm_nt(a, b, name, tm=tm, tk=tk, a_halves=a_halves, epilogue=epilogue,
                  extra=(x, g, dres), extra_specs=(row, vec, row),
                  out_shape=(jax.ShapeDtypeStruct((M, D), F32), jax.ShapeDtypeStruct((1, D), F32)),
                  out_specs=(row, vec))


def _rms_cast(x, g, name, *, tm=512):
    M, W = x.shape
    tm = _tile(M, tm, 16)

    def body(x_ref, g_ref, o_ref):
        xf = x_ref[...]
        r = lax.rsqrt(jnp.mean(xf * xf, axis=-1, keepdims=True) + EPS)
        o_ref[...] = (xf * r * g_ref[...]).astype(o_ref.dtype)

    return pl.pallas_call(body, name=name, grid=(M // tm,),
                          in_specs=[pl.BlockSpec((tm, W), lambda i: (i, 0)), pl.BlockSpec((1, W), lambda i: (0, 0))],
                          out_specs=pl.BlockSpec((tm, W), lambda i: (i, 0)),
                          out_shape=jax.ShapeDtypeStruct((M, W), CDT),
                          compiler_params=_params(("parallel",)))(x, g)


def _tri(n, lower):
    r = lax.broadcasted_iota(jnp.int32, (n, n), 0)
    c = lax.broadcasted_iota(jnp.int32, (n, n), 1)
    return (c <= r if lower else c >= r).astype(CDT)


def _gate_fwd(fl, b, name, *, tm=512):
    S = fl.shape[0]
    tm = _tile(S, tm, 16)

    def body(f_ref, b_ref, c_ref, carry):
        @pl.when(pl.program_id(0) == 0)
        def _():
            carry[...] = jnp.zeros_like(carry)
        z = f_ref[...] + b_ref[...]
        lf = jnp.minimum(z, 0.0) - jnp.log(1.0 + jnp.exp(-jnp.abs(z)))
        tri = _tri(tm, True)
        cum = sum(_dot(tri, p) for p in _split3(lf)) + carry[...]
        c_ref[...] = cum
        carry[...] = cum[tm - 1:tm, :]

    return pl.pallas_call(body, name=name, grid=(S // tm,),
                          in_specs=[pl.BlockSpec((tm, 128), lambda i: (i, 0)), pl.BlockSpec((1, 128), lambda i: (0, 0))],
                          out_specs=pl.BlockSpec((tm, 128), lambda i: (i, 0)),
                          out_shape=jax.ShapeDtypeStruct((S, 128), F32),
                          scratch_shapes=[pltpu.VMEM((1, 128), F32)],
                          compiler_params=_params(("arbitrary",)))(fl, b)


def _gate_bwd(dck, dcq, fl, b, name, *, tm=512):
    S = fl.shape[0]
    tm = _tile(S, tm, 16)
    nb = S // tm

    def body(dck_ref, dcq_ref, f_ref, b_ref, df_ref, db_ref, carry):
        @pl.when(pl.program_id(0) == 0)
        def _():
            carry[...] = jnp.zeros_like(carry)
            db_ref[...] = jnp.zeros_like(db_ref)
        lane = lax.broadcasted_iota(jnp.int32, (1, 128), 1)
        dc = dck_ref[...]
        for h in range(NH):
            dc = dc + dcq_ref[h] * (lane == h).astype(F32)
        tri = _tri(tm, False)
        suf = sum(_dot(tri, p) for p in _split3(dc)) + carry[...]
        carry[...] = suf[0:1, :]
        df = suf * jax.nn.sigmoid(-(f_ref[...] + b_ref[...]))
        df_ref[...] = df
        db_ref[...] += jnp.sum(df, axis=0, keepdims=True)

    rev = pl.BlockSpec((tm, 128), lambda i: (nb - 1 - i, 0))
    cols = pl.BlockSpec((NH, tm, 1), lambda i: (0, nb - 1 - i, 0))
    vec = pl.BlockSpec((1, 128), lambda i: (0, 0))
    return pl.pallas_call(body, name=name, grid=(nb,), in_specs=[rev, cols, rev, vec], out_specs=(rev, vec),
                          out_shape=(jax.ShapeDtypeStruct((S, 128), F32), jax.ShapeDtypeStruct((1, 128), F32)),
                          scratch_shapes=[pltpu.VMEM((1, 128), F32)],
                          compiler_params=_params(("arbitrary",)))(dck, dcq, fl, b)


def _group_rows(o_ref):
    return jnp.concatenate([o_ref[h] for h in range(NH)], axis=1)


def _out_proj(fo, so, gf, gs, w_out, x0, name, *, tm=512):
    S = fo.shape[1]
    tm = _tile(S, tm, 16)

    def body(fo_ref, so_ref, gf_ref, gs_ref, w_ref, x_ref, x1_ref, mx_ref):
        for ref, g_ref, lo in ((fo_ref, gf_ref, 0), (so_ref, gs_ref, GW)):
            o = _group_rows(ref)
            r = lax.rsqrt(jnp.mean(o * o, axis=-1, keepdims=True) + EPS)
            mx_ref[:, lo:lo + GW] = (o * r * g_ref[...]).astype(CDT)
        x1_ref[...] = x_ref[...] + _dot(mx_ref[...], w_ref[...])

    half = pl.BlockSpec((NH, tm, HD), lambda i: (0, i, 0))
    gvec = pl.BlockSpec((1, GW), lambda i: (0, 0))
    row = pl.BlockSpec((tm, D), lambda i: (i, 0))
    return pl.pallas_call(body, name=name, grid=(S // tm,),
                          in_specs=[half, half, gvec, gvec, pl.BlockSpec((D, D), lambda i: (0, 0)), row],
                          out_specs=(row, row),
                          out_shape=(jax.ShapeDtypeStruct((S, D), F32), jax.ShapeDtypeStruct((S, D), CDT)),
                          compiler_params=_params(("parallel",)))(fo, so, gf, gs, w_out, x0)


def _out_proj_bwd(dx1, w_out, fo, so, gf, gs, name, *, tm=512):
    S = fo.shape[1]
    tm = _tile(S, tm, 16)

    def epilogue(acc, i, extra_refs, out_refs):
        fo_ref, so_ref, gf_ref, gs_ref = extra_refs
        dfo_ref, dso_ref, dgf_ref, dgs_ref = out_refs

        @pl.when(i == 0)
        def _():
            dgf_ref[...] = jnp.zeros_like(dgf_ref)
            dgs_ref[...] = jnp.zeros_like(dgs_ref)
        for lo, o_ref, g_ref, do_ref, dg_ref in ((0, fo_ref, gf_ref, dfo_ref, dgf_ref), (GW, so_ref, gs_ref, dso_ref, dgs_ref)):
            dx, dg = _rms_bwd(acc[:, lo:lo + GW], _group_rows(o_ref), g_ref[...])
            for h in range(NH):
                do_ref[h] = dx[:, h * HD:(h + 1) * HD].astype(do_ref.dtype)
            dg_ref[...] += dg

    half = pl.BlockSpec((NH, tm, HD), lambda i, j, k: (0, i, 0))
    gvec = pl.BlockSpec((1, GW), lambda i, j, k: (0, 0))
    return _mm_nt(dx1, w_out, name, tm=tm, epilogue=epilogue, extra=(fo, so, gf, gs),
                  extra_specs=(half, half, gvec, gvec),
                  out_shape=(jax.ShapeDtypeStruct((NH, S, HD), CDT), jax.ShapeDtypeStruct((NH, S, HD), CDT),
                             jax.ShapeDtypeStruct((1, GW), F32), jax.ShapeDtypeStruct((1, GW), F32)),
                  out_specs=(half, half, gvec, gvec))


def _loss_bwd(x3, tgt, g, name, *, tm=512):
    S = x3.shape[0]
    tm = _tile(S, tm, 16)

    def body(x_ref, t_ref, g_ref, dx_ref, loss_ref, dg_ref):
        @pl.when(pl.program_id(0) == 0)
        def _():
            loss_ref[...] = jnp.zeros_like(loss_ref)
            dg_ref[...] = jnp.zeros_like(dg_ref)
        x = x_ref[...]
        gv = g_ref[...]
        r = lax.rsqrt(jnp.mean(x * x, axis=-1, keepdims=True) + EPS)
        xn = x * r
        err = xn * gv - t_ref[...]
        loss_ref[...] += jnp.full(loss_ref.shape, 0.5 * jnp.sum(jnp.mean(err * err, axis=-1, keepdims=True)), F32)
        dy = err * (1.0 / D)
        dg_ref[...] += jnp.sum(dy * xn, axis=0, keepdims=True)
        dyg = dy * gv
        dx_ref[...] = r * (dyg - xn * jnp.mean(dyg * xn, axis=-1, keepdims=True))

    row = pl.BlockSpec((tm, D), lambda i: (i, 0))
    vec = pl.BlockSpec((1, D), lambda i: (0, 0))
    dx3, loss, dg = pl.pallas_call(
        body, name=name, grid=(S // tm,), in_specs=[row, row, vec],
        out_specs=(row, pl.BlockSpec((1, 128), lambda i: (0, 0)), vec),
        out_shape=(jax.ShapeDtypeStruct((S, D), F32), jax.ShapeDtypeStruct((1, 128), F32), jax.ShapeDtypeStruct((1, D), F32)),
        compiler_params=_params(("arbitrary",)))(x3, tgt, g)
    return loss, dx3, dg


MASKED, FIRST, LAST = 1, 2, 4


def _att_tiles(name, S):
    tq, tk = ATT_TILES[name]
    return min(tq, S), min(tk, S)


def _pairs(S, tq, tk, descending=True):
    assert tk % tq == 0 and S % tk == 0
    qi, kj, fl = [], [], []
    for i in range(S // tq):
        last = ((i + 1) * tq - 1) // tk
        order = list(range(last, -1, -1) if descending else range(last + 1))
        for pos, kb in enumerate(order):
            qi.append(i)
            kj.append(kb)
            fl.append((MASKED if (kb + 1) * tk - 1 > i * tq else 0) | (FIRST if pos == 0 else 0) | (LAST if pos == last else 0))
    return tuple(jnp.asarray(np.asarray(a, np.int32)) for a in (qi, kj, fl))


def _head_blk(rows, by_key, head0, width=HD):
    if by_key:
        return pl.BlockSpec((1, rows, width), lambda h, n, qi, kj, fl: (h + head0, kj[n], 0))
    return pl.BlockSpec((1, rows, width), lambda h, n, qi, kj, fl: (h + head0, qi[n], 0))


def _att_specs(tq, tk, width=HD):
    qblk = pl.BlockSpec((1, tq, width), lambda h, n, qi, kj, fl: (h, qi[n], 0))
    kblk = pl.BlockSpec((1, tk, width), lambda h, n, qi, kj, fl: (h, kj[n], 0))
    qcol = pl.BlockSpec((1, tq, 1), lambda h, n, qi, kj, fl: (h, qi[n], 0))
    krow = pl.BlockSpec((1, 1, tk), lambda h, n, qi, kj, fl: (h, 0, kj[n]))
    return qblk, kblk, qcol, krow


def _causal(tq, w, ahead, strict):
    diff = lax.broadcasted_iota(jnp.int32, (tq, w), 1) - lax.broadcasted_iota(jnp.int32, (tq, w), 0)
    return diff < ahead if strict else diff <= ahead


def _masked_or_not(flags, step):
    pl.when(flags % 2 == 1)(functools.partial(step, True))
    pl.when(flags % 2 == 0)(functools.partial(step, False))


FOX_DEPTH = 2 * HD


def _fox_operands(qkv, cum, name, *, tm=512):
    S = qkv.shape[1]
    tm = _tile(S, tm, 16)

    def body(q_ref, k_ref, c_ref, q2_ref, k2_ref):
        lane = lax.broadcasted_iota(jnp.int32, (1, HD), 1)
        c = c_ref[...]
        for h in range(NH):
            pieces = [p.astype(F32) for p in _split3(c[:, h:h + 1])]
            qa = sum(jnp.where(lane == 2 * n, pieces[n], 0.0) for n in range(3)) + jnp.where((lane < 6) & (lane % 2 == 1), 1.0, 0.0)
            ka = sum(jnp.where(lane == 2 * n + 1, -pieces[n], 0.0) for n in range(3)) + jnp.where((lane < 6) & (lane % 2 == 0), 1.0, 0.0)
            q2_ref[h] = jnp.concatenate([qa.astype(CDT), q_ref[h] * ATT_SCALE], axis=1)
            k2_ref[h] = jnp.concatenate([ka.astype(CDT), k_ref[h]], axis=1)

    wide = pl.BlockSpec((NH, tm, FOX_DEPTH), lambda i: (0, i, 0))
    shp = jax.ShapeDtypeStruct((NH, S, FOX_DEPTH), CDT)
    return pl.pallas_call(body, name=name, grid=(S // tm,),
                          in_specs=[pl.BlockSpec((NH, tm, HD), lambda i: (0, i, 0)), pl.BlockSpec((NH, tm, HD), lambda i: (1, i, 0)),
                                    pl.BlockSpec((tm, 128), lambda i: (i, 0))],
                          out_specs=(wide, wide), out_shape=(shp, shp), compiler_params=_params(("parallel",)))(qkv, qkv, cum)


def _fox_fwd(q2, k2, v, name, slots=None, v0=0):
    S = q2.shape[1]
    tq, tk = _att_tiles("fox_fwd", S)
    qi, kj, fl = _pairs(S, tq, tk)
    qblk, kblk, qcol, krow = _att_specs(tq, tk)
    q2blk, k2blk, _, _ = _att_specs(tq, tk, FOX_DEPTH)
    npairs = int(qi.shape[0])

    def body(qi_ref, kj_ref, fl_ref, q2_ref, k2_ref, v_ref, *rest):
        if slots is None:
            o_ref, lse_ref, m_s, l_s, acc_s = rest
        else:
            _, o_ref, lse_ref, slots_ref, m_s, l_s, acc_s, send_sems, recv_sems = rest
        h, n = pl.program_id(0), pl.program_id(1)
        i, kb, flags = qi_ref[n], kj_ref[n], fl_ref[n]
        if slots is not None:
            _gather_steps(slots_ref, send_sems, recv_sems, first=(h == 0) & (n == 0), middle=(h == NH // 2) & (n == 0),
                          last=(h == NH - 1) & (n == npairs - 1))

        @pl.when(flags & FIRST != 0)
        def _():
            m_s[...] = jnp.full_like(m_s, NEG)
            l_s[...] = jnp.zeros_like(l_s)
            acc_s[...] = jnp.zeros_like(acc_s)

        def step(masked):
            s = _dot_nt(q2_ref[0], k2_ref[0])
            if masked:
                s = jnp.where(_causal(tq, tk, i * tq - kb * tk, False), s, NEG)
            m_new = jnp.maximum(m_s[...], jnp.max(s, axis=-1, keepdims=True))
            alpha = jnp.exp(m_s[...] - m_new)
            p = jnp.exp(s - m_new)
            l_s[...] = alpha * l_s[...] + jnp.sum(p, axis=-1, keepdims=True)
            acc_s[...] = alpha * acc_s[...] + _dot(p.astype(CDT), v_ref[0])
            m_s[...] = m_new

        _masked_or_not(flags, step)

        @pl.when(flags & LAST != 0)
        def _():
            o_ref[0] = acc_s[...] / l_s[...]
            lse_ref[0] = m_s[...] + jnp.log(l_s[...])

    scratch = [pltpu.VMEM((tq, 1), F32), pltpu.VMEM((tq, 1), F32), pltpu.VMEM((tq, HD), F32)]
    out_shape = (jax.ShapeDtypeStruct((NH, S, HD), F32), jax.ShapeDtypeStruct((NH, S, 1), F32))
    in_specs = [q2blk, k2blk, _head_blk(tk, True, v0)]
    if slots is None:
        grid_spec = pltpu.PrefetchScalarGridSpec(num_scalar_prefetch=3, grid=(NH, npairs), in_specs=in_specs,
                                                 out_specs=(qblk, qcol), scratch_shapes=scratch)
        o, lse = pl.pallas_call(body, name=name, grid_spec=grid_spec, out_shape=out_shape,
                                compiler_params=_params(("parallel", "arbitrary")))(qi, kj, fl, q2, k2, v)
        return o, lse, None
    grid_spec = pltpu.PrefetchScalarGridSpec(num_scalar_prefetch=3, grid=(NH, npairs), in_specs=in_specs + [ANY],
                                             out_specs=(qblk, qcol, ANY), scratch_shapes=scratch + list(GATHER_SEMS))
    return pl.pallas_call(body, name=name, grid_spec=grid_spec, out_shape=(*out_shape, jax.ShapeDtypeStruct(slots.shape, slots.dtype)),
                          input_output_aliases={6: 2},
                          compiler_params=_params(("arbitrary", "arbitrary")))(qi, kj, fl, q2, k2, v, slots)


def _fox_bwd(q, k, v, q2, k2, o, do, lse, name, h0=(0, 0, 0), swap=None):
    S = q.shape[1]
    tq, tk = _att_tiles("fox_bwd", S)
    qi, kj, fl = _pairs(S, tq, tk)
    qblk, kblk, qcol, krow = _att_specs(tq, tk)
    npairs = int(qi.shape[0])

    def body(qi_ref, kj_ref, fl_ref, q_ref, k_ref, v_ref, q2_ref, k2_ref, o_ref, do_ref, lse_ref, *rest):
        if swap is None:
            dq_ref, dk_ref, dv_ref, dck_ref, dcq_ref, dq_s, dl_s, dcq_s = rest
        else:
            g_ref, dq_ref, dk_ref, dv_ref, dck_ref, dcq_ref, got_ref, dq_s, dl_s, dcq_s, send_sem, recv_sem = rest
        n = pl.program_id(1)
        i, kb, flags = qi_ref[n], kj_ref[n], fl_ref[n]
        if swap is not None:
            h = pl.program_id(0)
            _swap_steps(g_ref, got_ref, send_sem, recv_sem, first=(h == 0) & (n == 0), last=(h == NH - 1) & (n == npairs - 1))

        @pl.when(n == 0)
        def _():
            dk_ref[...] = jnp.zeros_like(dk_ref)
            dv_ref[...] = jnp.zeros_like(dv_ref)
            dck_ref[...] = jnp.zeros_like(dck_ref)

        @pl.when(flags & FIRST != 0)
        def _():
            dq_s[...] = jnp.zeros_like(dq_s)
            dcq_s[...] = jnp.zeros_like(dcq_s)
            dl_s[...] = jnp.sum(do_ref[0].astype(F32) * o_ref[0], axis=-1, keepdims=True)

        def step(masked):
            qs = q_ref[0] * ATT_SCALE
            do = do_ref[0]
            p = jnp.exp(_dot_nt(q2_ref[0], k2_ref[0]) - lse_ref[0])
            if masked:
                p = jnp.where(_causal(tq, tk, i * tq - kb * tk, False), p, 0.0)
            ds = p * (_dot_nt(do, v_ref[0]) - dl_s[...])
            dsb = ds.astype(CDT)
            dq_s[...] += _dot(dsb, k_ref[0])
            rows = pl.ds(pl.multiple_of(kb * tk, tk), tk)
            dk_ref[0, rows, :] += _dot_tn(dsb, qs)
            dv_ref[0, rows, :] += _dot_tn(p.astype(CDT), do)
            dck_ref[0, :, rows] += -jnp.sum(ds, axis=0, keepdims=True)
            dcq_s[...] += jnp.sum(ds, axis=-1, keepdims=True)

        _masked_or_not(flags, step)

        @pl.when(flags & LAST != 0)
        def _():
            dq_ref[0] = (dq_s[...] * ATT_SCALE).astype(dq_ref.dtype)
            dcq_ref[0] = dcq_s[...]

    whole = pl.BlockSpec((1, S, HD), lambda h, n, qi, kj, fl: (h, 0, 0))
    q2blk, k2blk, _, _ = _att_specs(tq, tk, FOX_DEPTH)
    in_specs = [_head_blk(tq, False, h0[0]), _head_blk(tk, True, h0[1]), _head_blk(tk, True, h0[2]), q2blk, k2blk, qblk, qblk, qcol]
    out_specs = (qblk, whole, whole, pl.BlockSpec((1, 1, S), lambda h, n, qi, kj, fl: (h, 0, 0)), qcol)
    out_shape = (jax.ShapeDtypeStruct((NH, S, HD), CDT), jax.ShapeDtypeStruct((NH, S, HD), F32), jax.ShapeDtypeStruct((NH, S, HD), F32),
                 jax.ShapeDtypeStruct((NH, 1, S), F32), jax.ShapeDtypeStruct((NH, S, 1), F32))
    scratch = [pltpu.VMEM((tq, HD), F32), pltpu.VMEM((tq, 1), F32), pltpu.VMEM((tq, 1), F32)]
    if swap is None:
        grid_spec = pltpu.PrefetchScalarGridSpec(num_scalar_prefetch=3, grid=(NH, npairs), in_specs=in_specs,
                                                 out_specs=out_specs, scratch_shapes=scratch)
        return (*pl.pallas_call(body, name=name, grid_spec=grid_spec, out_shape=out_shape,
                                compiler_params=_params(("parallel", "arbitrary")))(qi, kj, fl, q, k, v, q2, k2, o, do, lse), None)
    grid_spec = pltpu.PrefetchScalarGridSpec(num_scalar_prefetch=3, grid=(NH, npairs), in_specs=in_specs + [ANY],
                                             out_specs=(*out_specs, ANY), scratch_shapes=scratch + list(SWAP_SEMS))
    got_shape = jax.ShapeDtypeStruct((swap.shape[0], swap.shape[1] // 2, 128), swap.dtype)
    return pl.pallas_call(body, name=name, grid_spec=grid_spec, out_shape=(*out_shape, got_shape),
                          compiler_params=_params(("arbitrary", "arbitrary")))(qi, kj, fl, q, k, v, q2, k2, o, do, lse, swap)


LOG2E = 1.4426950408889634


def _proj_qkv(h1, w_qkv, name, *, tm=1024):
    S, K = h1.shape
    tm = _tile(S, tm, 16)
    SQ, SK = 3, 4

    def heads(t):
        return [t[:, h * HD:(h + 1) * HD] for h in range(NH)]

    def body(a_ref, b_ref, o_ref, q2_ref, k2_ref):
        j = pl.program_id(1)
        ob = _dot(a_ref[...], b_ref[...]).astype(CDT)
        for h, t in enumerate(heads(ob)):
            o_ref[h] = t

        @pl.when(j == SQ)
        def _():
            qf = ob.astype(F32) * (ATT_SCALE * LOG2E)
            hi = qf.astype(CDT)
            lo = (qf - hi.astype(F32)).astype(CDT)
            for h, (th, tl) in enumerate(zip(heads(hi), heads(lo))):
                q2_ref[h] = jnp.concatenate([th, tl], axis=1)

        @pl.when(j == SK)
        def _():
            for h, t in enumerate(heads(ob)):
                k2_ref[h] = jnp.concatenate([t, t], axis=1)

    wide = pl.BlockSpec((NH, tm, 2 * HD), lambda i, j: (0, i, 0))
    return pl.pallas_call(
        body, name=name, grid=(S // tm, 6),
        in_specs=[pl.BlockSpec((tm, K), lambda i, j: (i, 0)), pl.BlockSpec((K, GW), lambda i, j: (0, j))],
        out_specs=(pl.BlockSpec((NH, tm, HD), lambda i, j: (j, i, 0)), wide, wide),
        out_shape=(jax.ShapeDtypeStruct((6 * NH, S, HD), CDT), jax.ShapeDtypeStruct((NH, S, 2 * HD), CDT),
                   jax.ShapeDtypeStruct((NH, S, 2 * HD), CDT)),
        compiler_params=_params(("parallel", "arbitrary")))(h1, w_qkv)


def _sb_softplus2(q2, k2sub, mask):
    z2 = _dot_nt(q2, k2sub)
    sp2 = jnp.maximum(z2, 0.0) + jnp.log2(1.0 + jnp.exp2(-jnp.abs(z2)))
    return z2, sp2 if mask is None else jnp.where(mask, sp2, 0.0)


def _strict_tri(n, upper, value):
    r = lax.broadcasted_iota(jnp.int32, (n, n), 0)
    c = lax.broadcasted_iota(jnp.int32, (n, n), 1)
    return jnp.where(r < c if upper else r > c, value, 0.0).astype(CDT)


def _sb_fwd(q2, k2, v, name, v0=0):
    S = q2.shape[1]
    tq, tk = _att_tiles("sb_fwd", S)
    W = min(W_SB, tk)
    qi, kj, fl = _pairs(S, tq, tk)
    qblk, kblk, qcol, _ = _att_specs(tq, tk)
    q2blk, k2blk, _, _ = _att_specs(tq, tk, 2 * HD)

    def body(qi_ref, kj_ref, fl_ref, q_ref, k_ref, v_ref, o_ref, lt_ref, run_s, acc_s):
        n = pl.program_id(1)
        i, kb, flags = qi_ref[n], kj_ref[n], fl_ref[n]

        @pl.when(flags & FIRST != 0)
        def _():
            run_s[...] = jnp.zeros_like(run_s)
            acc_s[...] = jnp.zeros_like(acc_s)

        def step(masked):
            neg_later = _strict_tri(W, False, -1.0)
            run = run_s[...]
            acc = acc_s[...]
            for sub in range(tk // W - 1, -1, -1):
                cols = slice(sub * W, (sub + 1) * W)
                mask = _causal(tq, W, i * tq - kb * tk - sub * W, True) if masked else None
                z2, sp2 = _sb_softplus2(q_ref[0], k_ref[0, cols, :], mask)
                excl = _dot(sp2.astype(CDT), neg_later)
                a = jnp.exp2((z2 - sp2) + (excl + run))
                if masked:
                    a = jnp.where(mask, a, 0.0)
                acc = acc + _dot(a.astype(CDT), v_ref[0, cols, :])
                run = run + (excl[:, 0:1] - sp2[:, 0:1])
            run_s[...] = run
            acc_s[...] = acc

        _masked_or_not(flags, step)

        @pl.when(flags & LAST != 0)
        def _():
            o_ref[0] = acc_s[...]
            lt_ref[0] = run_s[...]

    grid_spec = pltpu.PrefetchScalarGridSpec(
        num_scalar_prefetch=3, grid=(NH, int(qi.shape[0])), in_specs=[q2blk, k2blk, _head_blk(tk, True, v0)], out_specs=(qblk, qcol),
        scratch_shapes=[pltpu.VMEM((tq, 1), F32), pltpu.VMEM((tq, HD), F32)])
    return pl.pallas_call(body, name=name, grid_spec=grid_spec,
                          out_shape=(jax.ShapeDtypeStruct((NH, S, HD), F32), jax.ShapeDtypeStruct((NH, S, 1), F32)),
                          compiler_params=_params(("parallel", "arbitrary")))(qi, kj, fl, q2, k2, v)


def _sb_bwd(q, q2, k2, v, do, lt, name, scatter=None, q0=0, v0=0):
    S = q.shape[1]
    tq, tk = _att_tiles("sb_bwd", S)
    W = min(W_SB, tk)
    qi, kj, fl = _pairs(S, tq, tk, descending=False)
    qblk, kblk, qcol, _ = _att_specs(tq, tk)
    q2blk, k2blk, _, _ = _att_specs(tq, tk, 2 * HD)
    npairs = int(qi.shape[0])

    def body(qi_ref, kj_ref, fl_ref, q_ref, q2_ref, k2_ref, v_ref, do_ref, lt_ref, *rest):
        if scatter is None:
            dq_ref, dk_ref, dv_ref, passed_s, gsum_s, dq_s = rest
        else:
            h_ref, dq_ref, dk_ref, dv_ref, recv_ref, passed_s, gsum_s, dq_s, send_sems, recv_sems = rest
        n = pl.program_id(1)
        i, kb, flags = qi_ref[n], kj_ref[n], fl_ref[n]
        if scatter is not None:
            h = pl.program_id(0)
            _scatter_steps(h_ref, recv_ref, send_sems, recv_sems, first=(h == 0) & (n == 0), last=(h == NH - 1) & (n == npairs - 1))

        @pl.when(n == 0)
        def _():
            dk_ref[...] = jnp.zeros_like(dk_ref)
            dv_ref[...] = jnp.zeros_like(dv_ref)

        @pl.when(flags & FIRST != 0)
        def _():
            passed_s[...] = jnp.zeros_like(passed_s)
            gsum_s[...] = jnp.zeros_like(gsum_s)
            dq_s[...] = jnp.zeros_like(dq_s)

        def step(masked):
            qs = q_ref[0] * ATT_SCALE
            do = do_ref[0]
            neg_later = _strict_tri(W, False, -1.0)
            earlier = _strict_tri(W, True, 1.0)
            for sub in range(tk // W):
                cols = slice(sub * W, (sub + 1) * W)
                mask = _causal(tq, W, i * tq - kb * tk - sub * W, True) if masked else None
                ksub = k2_ref[0, cols, 0:HD]
                z2, sp2 = _sb_softplus2(q2_ref[0], k2_ref[0, cols, :], mask)
                excl = _dot(sp2.astype(CDT), neg_later)
                through = passed_s[...] + (excl[:, 0:1] - sp2[:, 0:1])
                t1 = z2 - sp2
                sig = jnp.exp2(t1)
                a = jnp.exp2(t1 + (excl + (lt_ref[0] - through)))
                if masked:
                    a = jnp.where(mask, a, 0.0)
                dl = _dot_nt(do, v_ref[0, cols, :]) * a
                before = _dot(dl.astype(CDT), earlier)
                dz = dl - sig * (dl + (before + gsum_s[...]))
                if masked:
                    dz = jnp.where(mask, dz, 0.0)
                dzb = dz.astype(CDT)
                dq_s[...] += _dot(dzb, ksub)
                rows = pl.ds(pl.multiple_of(kb * tk + sub * W, W), W)
                dk_ref[0, rows, :] += _dot_tn(dzb, qs)
                dv_ref[0, rows, :] += _dot_tn(a.astype(CDT), do)
                passed_s[...] = through
                gsum_s[...] += before[:, W - 1:W] + dl[:, W - 1:W]

        _masked_or_not(flags, step)

        @pl.when(flags & LAST != 0)
        def _():
            dq_ref[0] = (dq_s[...] * ATT_SCALE).astype(dq_ref.dtype)

    whole = pl.BlockSpec((1, S, HD), lambda h, n, qi, kj, fl: (h, 0, 0))
    in_specs = [_head_blk(tq, False, q0), q2blk, k2blk, _head_blk(tk, True, v0), qblk, qcol]
    out_specs = (qblk, whole, whole)
    out_shape = (jax.ShapeDtypeStruct((NH, S, HD), CDT), jax.ShapeDtypeStruct((NH, S, HD), F32), jax.ShapeDtypeStruct((NH, S, HD), F32))
    scratch = [pltpu.VMEM((tq, 1), F32), pltpu.VMEM((tq, 1), F32), pltpu.VMEM((tq, HD), F32)]
    if scatter is None:
        grid_spec = pltpu.PrefetchScalarGridSpec(num_scalar_prefetch=3, grid=(NH, npairs), in_specs=in_specs,
                                                 out_specs=out_specs, scratch_shapes=scratch)
        return (*pl.pallas_call(body, name=name, grid_spec=grid_spec, out_shape=out_shape,
                                compiler_params=_params(("parallel", "arbitrary")))(qi, kj, fl, q, q2, k2, v, do, lt), None)
    grid_spec = pltpu.PrefetchScalarGridSpec(num_scalar_prefetch=3, grid=(NH, npairs), in_specs=in_specs + [ANY],
                                             out_specs=(*out_specs, ANY), scratch_shapes=scratch + list(SCATTER_SEMS))
    recv_shape = jax.ShapeDtypeStruct((3,) + scatter.shape[1:], scatter.dtype)
    return pl.pallas_call(body, name=name, grid_spec=grid_spec, out_shape=(*out_shape, recv_shape),
                          compiler_params=_params(("arbitrary", "arbitrary")))(qi, kj, fl, q, q2, k2, v, do, lt, scatter)


def _mem_probs(q_ref, kv_ref, h):
    cols = slice(h * MHD, (h + 1) * MHD)
    s = _dot_nt(q_ref[:, cols], kv_ref[:, cols]) * MEM_SCALE
    e = jnp.exp(s - jnp.max(s, axis=-1, keepdims=True))
    return e / jnp.sum(e, axis=-1, keepdims=True)


def _xattn_fwd(q, kv, w_mo, x1, name, *, tm=512):
    S = q.shape[0]
    tm = _tile(S, tm, 16)
    nm = kv.shape[0]

    def body(q_ref, kv_ref, w_ref, x_ref, x2_ref, o_ref):
        for h in range(NMH):
            p = _mem_probs(q_ref, kv_ref, h)
            o_ref[:, h * MHD:(h + 1) * MHD] = _dot(p.astype(CDT), kv_ref[:, D + h * MHD:D + (h + 1) * MHD]).astype(CDT)
        x2_ref[...] = x_ref[...] + _dot(o_ref[...], w_ref[...])

    row = pl.BlockSpec((tm, D), lambda i: (i, 0))
    return pl.pallas_call(body, name=name, grid=(S // tm,),
                          in_specs=[row, pl.BlockSpec((nm, 2 * D), lambda i: (0, 0)), pl.BlockSpec((D, D), lambda i: (0, 0)), row],
                          out_specs=(row, row),
                          out_shape=(jax.ShapeDtypeStruct((S, D), F32), jax.ShapeDtypeStruct((S, D), CDT)),
                          compiler_params=_params(("parallel",)))(q, kv, w_mo, x1)


def _xattn_bwd(q, kv, do, name, *, tm=512):
    S = q.shape[0]
    tm = _tile(S, tm, 16)
    nm = kv.shape[0]

    def body(q_ref, kv_ref, do_ref, dq_ref, dkv_ref):
        @pl.when(pl.program_id(0) == 0)
        def _():
            dkv_ref[...] = jnp.zeros_like(dkv_ref)
        for h in range(NMH):
            cols = slice(h * MHD, (h + 1) * MHD)
            vcols = slice(D + h * MHD, D + (h + 1) * MHD)
            p = _mem_probs(q_ref, kv_ref, h)
            doh = do_ref[:, cols]
            dp = _dot_nt(doh, kv_ref[:, vcols])
            ds = (p * (dp - jnp.sum(p * dp, axis=-1, keepdims=True)) * MEM_SCALE).astype(CDT)
            dq_ref[:, cols] = _dot(ds, kv_ref[:, cols]).astype(CDT)
            dkv_ref[:, cols] += _dot_tn(ds, q_ref[:, cols])
            dkv_ref[:, vcols] += _dot_tn(p.astype(CDT), doh)

    row = pl.BlockSpec((tm, D), lambda i: (i, 0))
    kvs = pl.BlockSpec((nm, 2 * D), lambda i: (0, 0))
    return pl.pallas_call(body, name=name, grid=(S // tm,), in_specs=[row, kvs, row], out_specs=(row, kvs),
                          out_shape=(jax.ShapeDtypeStruct((S, D), CDT), jax.ShapeDtypeStruct((nm, 2 * D), F32)),
                          compiler_params=_params(("arbitrary",)))(q, kv, do)


HALO = 16
SLAB = 8


def _shift_down(u, prev, s):
    rolled = pltpu.roll(u, s, 0)
    top = rolled[0:SLAB]
    r = lax.broadcasted_iota(jnp.int32, top.shape, 0)
    for t in range(s):
        top = jnp.where(r == t, prev[HALO - s + t:HALO - s + t + 1, :], top)
    return jnp.concatenate([top, rolled[SLAB:]], axis=0)


def _shift_up(u, nxt, s):
    n = u.shape[0]
    rolled = pltpu.roll(u, n - s, 0)
    bottom = rolled[n - SLAB:]
    r = lax.broadcasted_iota(jnp.int32, bottom.shape, 0)
    for t in range(s):
        bottom = jnp.where(r == SLAB - s + t, nxt[t:t + 1, :], bottom)
    return jnp.concatenate([rolled[:n - SLAB], bottom], axis=0)


def _conv_taps(u_ref, h_ref, first):
    u = u_ref[...].astype(F32)
    prev = jnp.where(first, 0.0, h_ref[...].astype(F32))
    out = []
    for half in range(2):
        out.append((u[half], _shift_down(u[half], prev[half], 1), _shift_down(u[half], prev[half], 2)))
    return out


def _conv_specs(tm, tn, nsb):
    blk = pl.BlockSpec((2, tm, tn), lambda j, i: (0, i, j))
    prev = pl.BlockSpec((2, HALO, tn), lambda j, i: (0, jnp.maximum(i * (tm // HALO) - 1, 0), j))
    nxt = pl.BlockSpec((2, HALO, tn), lambda j, i: (0, jnp.minimum((i + 1) * (tm // HALO), nsb - 1), j))
    w = pl.BlockSpec((2, 3, tn), lambda j, i: (0, 0, j))
    b = pl.BlockSpec((2, 1, tn), lambda j, i: (0, 0, j))
    return blk, prev, nxt, w, b


def _conv_apply(taps, w_ref, b_ref):
    ys = []
    for half in range(2):
        u, u1, u2 = taps[half]
        w = w_ref[half]
        ys.append(b_ref[half] + u2 * w[0:1, :] + u1 * w[1:2, :] + u * w[2:3, :])
    return ys


def _conv_act(u0, cw, cb, name, *, tm=2048, tn=256):
    _, S, F = u0.shape
    tm = _tile(S, tm, HALO)
    tn = _tile(F, tn, 128)
    blk, prev, _, w, b = _conv_specs(tm, tn, S // HALO)

    def body(u_ref, h_ref, w_ref, b_ref, a_ref):
        yg, yv = _conv_apply(_conv_taps(u_ref, h_ref, pl.program_id(1) == 0), w_ref, b_ref)
        a_ref[...] = (yg * jax.nn.sigmoid(yg) * yv).astype(a_ref.dtype)

    return pl.pallas_call(body, name=name, grid=(F // tn, S // tm), in_specs=[blk, prev, w, b],
                          out_specs=pl.BlockSpec((tm, tn), lambda j, i: (i, j)),
                          out_shape=jax.ShapeDtypeStruct((S, F), CDT),
                          compiler_params=_params(("parallel", "parallel")))(u0, u0, cw, cb)


def _conv_act_bwd(u0, da, cw, cb, name, *, tm=2048, tn=256):
    _, S, F = u0.shape
    tm = _tile(S, tm, HALO)
    tn = _tile(F, tn, 128)
    blk, prev, _, w, b = _conv_specs(tm, tn, S // HALO)

    def body(u_ref, h_ref, da_ref, w_ref, b_ref, du_ref, dwb_ref):
        @pl.when(pl.program_id(1) == 0)
        def _():
            dwb_ref[...] = jnp.zeros_like(dwb_ref)
        taps = _conv_taps(u_ref, h_ref, pl.program_id(1) == 0)
        yg, yv = _conv_apply(taps, w_ref, b_ref)
        sg = jax.nn.sigmoid(yg)
        da = da_ref[...].astype(F32)
        dus = (da * yv * sg * (1.0 + yg * (1.0 - sg)), da * yg * sg)
        for half in range(2):
            du = dus[half]
            du_ref[half] = du.astype(du_ref.dtype)
            u, u1, u2 = taps[half]
            for row, term in enumerate((du * u2, du * u1, du * u, du)):
                dwb_ref[half, row:row + 1, :] += jnp.sum(term, axis=0, keepdims=True)

    return pl.pallas_call(body, name=name, grid=(F // tn, S // tm),
                          in_specs=[blk, prev, pl.BlockSpec((tm, tn), lambda j, i: (i, j)), w, b],
                          out_specs=(blk, pl.BlockSpec((2, 4, tn), lambda j, i: (0, 0, j))),
                          out_shape=(jax.ShapeDtypeStruct((2, S, F), CDT), jax.ShapeDtypeStruct((2, 4, F), F32)),
                          compiler_params=_params(("parallel", "arbitrary")))(u0, u0, da, cw, cb)


def _conv_bwd_input(du, cw, name, *, tm=2048, tn=256):
    _, S, F = du.shape
    tm = _tile(S, tm, HALO)
    tn = _tile(F, tn, 128)
    blk, _, nxt, w, _ = _conv_specs(tm, tn, S // HALO)
    ni = S // tm

    def body(d_ref, h_ref, w_ref, o_ref):
        d = d_ref[...].astype(F32)
        nx = jnp.where(pl.program_id(1) == ni - 1, 0.0, h_ref[...].astype(F32))
        for half in range(2):
            wv = w_ref[half]
            y = d[half] * wv[2:3, :] + _shift_up(d[half], nx[half], 1) * wv[1:2, :] + _shift_up(d[half], nx[half], 2) * wv[0:1, :]
            o_ref[half] = y.astype(o_ref.dtype)

    return pl.pallas_call(body, name=name, grid=(F // tn, ni), in_specs=[blk, nxt, w], out_specs=blk,
                          out_shape=jax.ShapeDtypeStruct((2, S, F), CDT),
                          compiler_params=_params(("parallel", "parallel")))(du, du, cw)


ANY = pl.BlockSpec(memory_space=pl.ANY)


def _place():
    return lax.axis_index("x"), lax.axis_index("y"), lax.axis_index("c")


def _other_chips(x, y):
    return ((1 - x, y), (x, 1 - y), (1 - x, 1 - y))


def _when(pred, fn):
    if pred is True:
        fn()
    else:
        pl.when(pred)(fn)


GATHER_SEMS = (pltpu.SemaphoreType.DMA((6,)), pltpu.SemaphoreType.DMA((6,)))
SCATTER_SEMS = (pltpu.SemaphoreType.DMA((3,)), pltpu.SemaphoreType.DMA((3,)))


def _gather_steps(out_ref, send_sems, recv_sems, first=True, middle=True, last=True):
    half = out_ref.shape[1] // 2
    x, y, c = _place()
    chips = _other_chips(x, y)

    def part(chip, pc):
        return out_ref.at[2 * chip[0] + chip[1], pl.ds(pl.multiple_of(pc * half, 16), half), :]

    def copy(k, chip, pc, to):
        return pltpu.make_async_remote_copy(src_ref=part(chip, pc), dst_ref=part(chip, pc),
                                            send_sem=send_sems.at[k], recv_sem=recv_sems.at[k],
                                            device_id=to, device_id_type=MESH)

    def send_mine():
        for j, chip in enumerate(chips):
            copy(j, (x, y), c, (*chip, c)).start()

    def pass_on():
        for j, chip in enumerate(chips):
            copy(j, chip, c, (x, y, c)).wait_recv()
            copy(3 + j, chip, c, (x, y, 1 - c)).start()

    def finish():
        for j, chip in enumerate(chips):
            copy(3 + j, chip, 1 - c, (x, y, c)).wait_recv()
        for j, chip in enumerate(chips):
            copy(j, (x, y), c, (*chip, c)).wait_send()
            copy(3 + j, chip, c, (x, y, 1 - c)).wait_send()

    _when(first, send_mine)
    _when(middle, pass_on)
    _when(last, finish)


def _gather_weights(buf):
    def body(buf_ref, out_ref, send_sems, recv_sems):
        del buf_ref
        _gather_steps(out_ref, send_sems, recv_sems)

    return pl.pallas_call(body, name="gather_weights", in_specs=[ANY], out_specs=ANY,
                          out_shape=jax.ShapeDtypeStruct(buf.shape, buf.dtype), input_output_aliases={0: 0},
                          scratch_shapes=list(GATHER_SEMS))(buf)


def _gather_small(v):
    m = v.shape[0]

    def body(v_ref, out_ref, send_sems, recv_sems, local_sem):
        x, y, c = _place()
        me, sibling = (x, y, c), (x, y, 1 - c)
        chips = _other_chips(x, y)

        def rows(px, py, pc):
            return out_ref.at[pl.ds((4 * px + 2 * py + pc) * m, m), :]

        def copy(k, block, to, src=None):
            return pltpu.make_async_remote_copy(src_ref=rows(*block) if src is None else src, dst_ref=rows(*block),
                                                send_sem=send_sems.at[k], recv_sem=recv_sems.at[k],
                                                device_id=to, device_id_type=MESH)

        mine = pltpu.make_async_copy(v_ref, rows(*me), local_sem)
        mine.start()
        first = [copy(0, me, sibling, src=v_ref)]
        first += [copy(1 + j, me, (*chip, c), src=v_ref) for j, chip in enumerate(chips)]
        for cp in first:
            cp.start()
        passed = [copy(4 + j, (*chip, c), sibling) for j, chip in enumerate(chips)]
        for j, chip in enumerate(chips):
            copy(1 + j, (*chip, c), me).wait_recv()
            passed[j].start()
        copy(0, sibling, me).wait_recv()
        for j, chip in enumerate(chips):
            copy(4 + j, (*chip, 1 - c), me).wait_recv()
        for cp in first + passed:
            cp.wait_send()
        mine.wait()

    vm = pl.BlockSpec(memory_space=pltpu.VMEM)
    return pl.pallas_call(body, name="gather_small", in_specs=[vm], out_specs=vm,
                          out_shape=jax.ShapeDtypeStruct((8 * m, 128), v.dtype),
                          scratch_shapes=[pltpu.SemaphoreType.DMA((7,)), pltpu.SemaphoreType.DMA((7,)), pltpu.SemaphoreType.DMA])(v)


SWAP_SEMS = (pltpu.SemaphoreType.DMA, pltpu.SemaphoreType.DMA)


def _swap_steps(g_ref, out_ref, send_sem, recv_sem, first=True, last=True):
    half = out_ref.shape[1]
    x, y, c = _place()

    def copy():
        src = g_ref.at[:, pl.ds(pl.multiple_of((1 - c) * half, 8), half), :]
        return pltpu.make_async_remote_copy(src_ref=src, dst_ref=out_ref, send_sem=send_sem, recv_sem=recv_sem,
                                            device_id=(x, y, 1 - c), device_id_type=MESH)

    _when(first, lambda: copy().start())
    _when(last, lambda: copy().wait())


def _swap_halves(g, name):
    n, rows, _ = g.shape

    def body(g_ref, out_ref, send_sem, recv_sem):
        _swap_steps(g_ref, out_ref, send_sem, recv_sem)

    return pl.pallas_call(body, name=name, in_specs=[ANY], out_specs=ANY,
                          out_shape=jax.ShapeDtypeStruct((n, rows // 2, 128), g.dtype),
                          scratch_shapes=list(SWAP_SEMS))(g)


def _scatter_steps(h_ref, out_ref, send_sems, recv_sems, first=True, last=True):
    x, y, c = _place()

    def copies():
        return [pltpu.make_async_remote_copy(src_ref=h_ref.at[2 * chip[0] + chip[1]], dst_ref=out_ref.at[j],
                                             send_sem=send_sems.at[j], recv_sem=recv_sems.at[j],
                                             device_id=(*chip, c), device_id_type=MESH)
                for j, chip in enumerate(_other_chips(x, y))]

    def start():
        for cp in copies():
            cp.start()

    def finish():
        for cp in copies():
            cp.wait()

    _when(first, start)
    _when(last, finish)


def _scatter_chips(hsum):
    n, half, _ = hsum.shape

    def body(h_ref, out_ref, send_sems, recv_sems):
        _scatter_steps(h_ref, out_ref, send_sems, recv_sems)

    return pl.pallas_call(body, name="scatter_chips", in_specs=[ANY], out_specs=ANY,
                          out_shape=jax.ShapeDtypeStruct((3, half, 128), hsum.dtype),
                          scratch_shapes=list(SCATTER_SEMS))(hsum)


def _join_halves(buf, name):
    half = buf.shape[0] // 2

    def body(buf_ref, out_ref, send_sem, recv_sem):
        del buf_ref
        x, y, c = _place()
        mine = out_ref.at[pl.ds(pl.multiple_of(c * half, 8), half), :]
        other = out_ref.at[pl.ds(pl.multiple_of((1 - c) * half, 8), half), :]
        cp = pltpu.make_async_remote_copy(src_ref=mine, dst_ref=mine, send_sem=send_sem, recv_sem=recv_sem,
                                          device_id=(x, y, 1 - c), device_id_type=MESH)
        cp.start()
        cp.wait_send()
        pltpu.make_async_remote_copy(src_ref=other, dst_ref=other, send_sem=send_sem, recv_sem=recv_sem,
                                     device_id=(x, y, 1 - c), device_id_type=MESH).wait_recv()

    return pl.pallas_call(body, name=name, in_specs=[ANY], out_specs=ANY,
                          out_shape=jax.ShapeDtypeStruct(buf.shape, buf.dtype), input_output_aliases={0: 0},
                          scratch_shapes=[pltpu.SemaphoreType.DMA, pltpu.SemaphoreType.DMA])(buf)


def _add_sibling(g, recv, c_idx, name):
    n, rows, _ = g.shape
    half = rows // 2
    tr = _tile(half, ADAM_ROWS, 16)
    nb = half // tr

    def body(c_ref, g_ref, r_ref, o_ref, ob_ref):
        s = g_ref[...] + r_ref[...]
        o_ref[...] = s
        ob_ref[...] = s.astype(CDT)

    out = pl.BlockSpec((None, tr, 128), lambda k, i, c: (k, i, 0))
    grid_spec = pltpu.PrefetchScalarGridSpec(
        num_scalar_prefetch=1, grid=(n, nb),
        in_specs=[pl.BlockSpec((None, tr, 128), lambda k, i, c: (k, c[0] * nb + i, 0)), out],
        out_specs=(out, out))
    return pl.pallas_call(body, name=name, grid_spec=grid_spec,
                          out_shape=(jax.ShapeDtypeStruct((n, half, 128), F32), jax.ShapeDtypeStruct((n, half, 128), CDT)),
                          compiler_params=_params(("parallel", "parallel")))(c_idx, g, recv)


def _add_chips(hsum, recv, chip_idx, name):
    n, half, _ = hsum.shape
    tr = _tile(half, ADAM_ROWS, 16)

    def body(k_ref, h_ref, r_ref, o_ref):
        o_ref[...] = ((h_ref[...] + r_ref[0].astype(F32)) + r_ref[1].astype(F32)) + r_ref[2].astype(F32)

    grid_spec = pltpu.PrefetchScalarGridSpec(
        num_scalar_prefetch=1, grid=(half // tr,),
        in_specs=[pl.BlockSpec((None, tr, 128), lambda i, k: (k[0], i, 0)),
                  pl.BlockSpec((3, tr, 128), lambda i, k: (0, i, 0))],
        out_specs=pl.BlockSpec((tr, 128), lambda i, k: (i, 0)))
    return pl.pallas_call(body, name=name, grid_spec=grid_spec, out_shape=jax.ShapeDtypeStruct((half, 128), F32),
                          compiler_params=_params(("parallel",)))(chip_idx, hsum, recv)


def _adamw_math(g, w, m, v):
    m2 = B1 * m + (1.0 - B1) * g
    v2 = B2 * v + (1.0 - B2) * (g * g)
    delta = -LR * ((m2 / BC1) / (jnp.sqrt(v2 / BC2) + AEPS) + WD * w)
    return delta, m2, v2


def _adamw(g, w, m, v, name):
    rows, cols = g.shape
    tr = _tile(rows, max(8, (ADAM_ROWS * 128 // cols) // 8 * 8), 8)

    def body(g_ref, w_ref, m_ref, v_ref, d_ref, m2_ref, v2_ref):
        d_ref[...], m2_ref[...], v2_ref[...] = _adamw_math(g_ref[...], w_ref[...], m_ref[...], v_ref[...])

    blk = pl.BlockSpec((None, tr, cols), lambda i: (0, i, 0))
    shp = jax.ShapeDtypeStruct((1, rows, cols), F32)
    return pl.pallas_call(body, name=name, grid=(rows // tr,), in_specs=[pl.BlockSpec((tr, cols), lambda i: (i, 0))] + [blk] * 3,
                          out_specs=(blk,) * 3, out_shape=(shp,) * 3, compiler_params=_params(("parallel",)))(g, w, m, v)


def _adamw_small(parts, w, m, v, name):
    rows = w.shape[0]

    def body(p_ref, w_ref, m_ref, v_ref, g_ref, d_ref, m2_ref, v2_ref):
        g = p_ref[0]
        for k in range(1, 8):
            g = g + p_ref[k]
        g_ref[...] = g
        d_ref[...], m2_ref[...], v2_ref[...] = _adamw_math(g, w_ref[...], m_ref[...], v_ref[...])

    shp = jax.ShapeDtypeStruct((rows, 128), F32)
    return pl.pallas_call(body, name=name, out_shape=(shp,) * 4)(parts, w, m, v)


def _pack_rows(parts, rows):
    flat = jnp.concatenate([p.reshape(-1) for p in parts])
    return jnp.pad(flat, (0, rows * 128 - flat.shape[0])).reshape(rows, 128)


def _unpack(flat, sizes, shapes):
    out, off = [], 0
    for n, s in zip(sizes, shapes):
        out.append(flat[off:off + n].reshape(s))
        off += n
    return out


def _to_shards(full, shard_shape, axis):
    if axis == 0:
        return full.reshape(N_CHIP, -1)
    r, cs = shard_shape
    return full.reshape(r, N_CHIP, cs).transpose(1, 0, 2).reshape(N_CHIP, -1)


def _from_shards(sh, shard_shape, axis):
    r, cs = shard_shape
    if axis == 0:
        return sh.reshape(N_CHIP * r, cs)
    return sh.reshape(N_CHIP, r, cs).transpose(1, 0, 2).reshape(r, N_CHIP * cs)


def _local_step(x0, mem, tgt, W, gains, ex=None):
    S = x0.shape[0]
    w_in = jnp.pad(W["w_in"], ((0, 0), (0, IN_PAD - IN_COLS)))
    b_f = jnp.pad(gains["b_forget"], ((0, 0), (0, 128 - NH)))

    h1 = _rms_cast(x0, gains["attn_norm_g"], "norm_attn")
    qkv, sq2, sk2 = _proj_qkv(h1, w_in[:, :NQKV], "proj_qkv")
    fox, sb_q, sb_v = (0, NH, 2 * NH), 3 * NH, 5 * NH
    fl = _mm_nn(h1, w_in[:, NQKV:NQKV + 128], F32, "proj_gate")
    cum = _gate_fwd(fl, b_f, "gate_cumsum")
    fq2, fk2 = _fox_operands(qkv, cum, "fox_operands")
    fo_h, lse, gathered = _fox_fwd(fq2, fk2, qkv, "fox_fwd", slots=None if ex is None else ex.slots("b"), v0=fox[2])
    if ex is not None:
        W = {**W, **ex.unpack("b", gathered)}
    cw = W["conv_w"].reshape(3, 2, DFF).transpose(1, 0, 2)
    cb = gains["conv_b"].reshape(2, 1, DFF)
    so_h, s_lt = _sb_fwd(sq2, sk2, qkv, "sb_fwd", v0=sb_v)
    x1, mixed = _out_proj(fo_h, so_h, gains["fox_out_g"], gains["sb_out_g"], W["w_out"], x0, "out_proj")

    h2 = _rms_cast(x1, gains["xattn_norm_g"], "norm_xattn")
    mn = _rms_cast(mem, gains["mem_norm_g"], "norm_mem")
    mq = _mm_nn(h2, W["w_mq"], CDT, "proj_mq")
    kv = _mm_nn(mn, W["w_mkv"], CDT, "proj_mkv")
    x2, mo = _xattn_fwd(mq, kv, W["w_mo"], x1, "xattn_fwd")

    h3 = _rms_cast(x2, gains["ffn_norm_g"], "norm_ffn")
    u0 = _mm_nn(h3, W["w_up"], CDT, "ffn_up", tm=512, tn=DFF, halves=True)
    act = _conv_act(u0, cw, cb, "conv_act")
    x3 = _mm_nn(act, W["w_down"], F32, "ffn_down", tm=512, residual=x2)
    loss, dx3, dg_final = _loss_bwd(x3, tgt, gains["final_norm_g"].reshape(1, D), "loss")

    gw, gs = {}, {"final_norm_g": dg_final}
    da = _mm_nt(dx3, W["w_down"], "ffn_down_dx", tn=DFF, out_dtype=CDT)
    gw["w_down"] = _mm_tn(act, dx3, "ffn_down_dw", tka=DFF)
    du, dwb = _conv_act_bwd(u0, da, cw, cb, "conv_act_bwd")
    gw["conv_w"] = dwb[:, :3].transpose(1, 0, 2).reshape(3, 2 * DFF)
    gs["conv_b"] = dwb[:, 3].reshape(1, 2 * DFF)
    du0 = _conv_bwd_input(du, cw, "conv_bwd_input")
    gw["w_up"] = _mm_tn(h3, du0, "ffn_up_dw", tn=DFF, b_halves=True)
    dx2, gs["ffn_norm_g"] = _mm_nt_rmsbwd(du0, W["w_up"], x2, gains["ffn_norm_g"], dx3, "ffn_up_dx", tk=DFF, a_halves=True)

    dmo = _mm_nt(dx2, W["w_mo"], "mo_dx", tn=512, out_dtype=CDT)
    gw["w_mo"] = _mm_tn(mo, dx2, "mo_dw")
    dmq, dkv = _xattn_bwd(mq, kv, dmo, "xattn_bwd")
    gw["w_mq"] = _mm_tn(h2, dmq, "mq_dw")
    dx1, gs["xattn_norm_g"] = _mm_nt_rmsbwd(dmq, W["w_mq"], x1, gains["xattn_norm_g"], dx2, "mq_dx")
    gw["w_mkv"] = _mm_tn(mn, dkv, "mkv_dw")
    _, gs["mem_norm_g"] = _mm_nt_rmsbwd(dkv, W["w_mkv"], mem, gains["mem_norm_g"], jnp.zeros_like(mem), "mkv_dx")

    gw["w_out"] = _mm_tn(mixed, dx1, "out_dw")
    dfo_h, dso_h, gs["fox_out_g"], gs["sb_out_g"] = _out_proj_bwd(dx1, W["w_out"], fo_h, so_h, gains["fox_out_g"], gains["sb_out_g"], "out_dx")
    flat = None if ex is None else ex.flat("b", gw)
    dfq, dfk, dfv, dck, dcq, got = _fox_bwd(qkv, qkv, qkv, fq2, fk2, fo_h, dfo_h, lse, "fox_bwd", h0=fox, swap=flat)
    pair, pair16 = (None, None) if ex is None else ex.pair_sums("b", flat, got)
    dsq, dsk, dsv, arrived = _sb_bwd(qkv, sq2, sk2, qkv, dso_h, s_lt, "sb_bwd", scatter=pair16, q0=sb_q, v0=sb_v)
    dfl, db = _gate_bwd(jnp.pad(dck[:, 0, :].T, ((0, 0), (0, 128 - NH))), dcq, fl, b_f, "gate_bwd")
    gs["b_forget"] = db[:, :NH]
    dqkv = jnp.concatenate([dfq, dfk.astype(CDT), dfv.astype(CDT), dsq, dsk.astype(CDT), dsv.astype(CDT)], axis=0)
    dproj = jnp.concatenate([dqkv.transpose(1, 0, 2).reshape(S, NQKV), dfl.astype(CDT),
                             jnp.zeros((S, IN_PAD - NQKV - 128), CDT)], axis=1)
    gw["w_in"] = _mm_tn(h1, dproj, "in_dw", tn=IN_PAD)[:, :IN_COLS]
    dx0, gs["attn_norm_g"] = _mm_nt_rmsbwd(dproj, w_in, x0, gains["attn_norm_g"], dx1, "in_dx", tk=IN_PAD)
    return loss, dx0, gw, gs, (pair, arrived)


NAMES = ("attn_norm_g", "w_in", "b_forget", "fox_out_g", "sb_out_g", "w_out", "xattn_norm_g", "mem_norm_g", "w_mq",
         "w_mkv", "w_mo", "ffn_norm_g", "w_up", "conv_w", "conv_b", "w_down", "final_norm_g")


class _Exchange:
    def __init__(self, w):
        self.w = w
        xi, yi, ci = _place()
        self.core = ci
        self.chip = 2 * xi + yi
        self.core_idx = jnp.reshape(ci, (1,)).astype(jnp.int32)
        self.chip_idx = jnp.reshape(self.chip, (1,)).astype(jnp.int32)

    def slots(self, g):
        parts = []
        for name, shape, _ in GROUPS[g]:
            blk = self.w[name].reshape(shape)
            parts.append(lax.bitcast_convert_type(blk, CDT) if name == "conv_w" else blk.astype(CDT))
        rows = _rows_g(GROUPS[g])
        return lax.dynamic_update_slice(lax.empty((N_CHIP, rows, 128), CDT), _pack_rows(parts, rows)[None], (self.chip, 0, 0))

    def unpack(self, g, gathered):
        flat, full, off = gathered.reshape(N_CHIP, -1), {}, 0
        for (name, shape, axis), n in zip(GROUPS[g], _gather_sizes(GROUPS[g])):
            sh = flat[:, off:off + n]
            off += n
            if name == "conv_w":
                sh = lax.bitcast_convert_type(sh.reshape(N_CHIP, n // 2, 2), F32)
            full[name] = _from_shards(sh, shape, axis)
        return full

    def flat(self, g, gw):
        rows = _rows_f(GROUPS[g])
        flat = jnp.concatenate([_to_shards(gw[name], shape, axis) for name, shape, axis in GROUPS[g]], axis=1)
        return jnp.pad(flat, ((0, 0), (0, rows * 128 - flat.shape[1]))).reshape(N_CHIP, rows, 128)

    def pair_sums(self, g, flat, got=None):
        if got is None:
            got = _swap_halves(flat, "swap_halves_" + g)
        return _add_sibling(flat, got, self.core_idx, "add_sibling_" + g)

    def finish(self, g, pair, arrived):
        rows = _rows_f(GROUPS[g])
        mine = _add_chips(pair, arrived, self.chip_idx, "add_chips_" + g)
        whole = _join_halves(lax.dynamic_update_slice(lax.empty((rows, 128), F32), mine, (self.core * (rows // 2), 0)), "join_halves_" + g)
        shapes = [s for _, s, _ in GROUPS[g]]
        return {name: arr for (name, _, _), arr in zip(GROUPS[g], _unpack(whole.reshape(-1), _sizes(GROUPS[g]), shapes))}


def _step(x, mem, loss_target, w, m, v):
    ex = _Exchange(w)

    W = ex.unpack("a", _gather_weights(ex.slots("a")))
    gains = {name: w[name].reshape(1, -1) for name, _ in SMALL}

    loss, grad_x, gw, gs, (pair_b, arrived_b) = _local_step(x[0], mem[0], loss_target[0], W, gains, ex)

    grads = ex.finish("b", pair_b, arrived_b)
    pair_a, pair16_a = ex.pair_sums("a", ex.flat("a", gw))
    grads.update(ex.finish("a", pair_a, _scatter_chips(pair16_a)))
    small = jnp.concatenate([gs[name].reshape(-1) for name, _ in SMALL] + [loss[0, :1]])
    small = jnp.pad(small, (0, ROWS_S * 128 - P_SMALL)).reshape(ROWS_S, 128)
    small_parts = _gather_small(small).reshape(8, ROWS_S, 128)

    def flat_small(d):
        return _pack_rows([d[name] for name, _ in SMALL], ROWS_S)

    outs = {}
    for name, shape, _ in BIG:
        g = grads[name]
        res = _adamw(g, w[name], m[name], v[name], "adamw_" + name)
        for prefix, arr in zip(("grad_", "delta_", "new_m_", "new_v_"), (g, *res)):
            outs[prefix + name] = arr.reshape(w[name].shape)
    small_res = _adamw_small(small_parts, flat_small(w), flat_small(m), flat_small(v), "adamw_small")
    g_sm = small_res[0]
    for prefix, sm in zip(("grad_", "delta_", "new_m_", "new_v_"), small_res):
        for (name, n), arr in zip(SMALL, _unpack(sm.reshape(-1), [n for _, n in SMALL], [(n,) for _, n in SMALL])):
            outs[prefix + name] = arr.reshape(w[name].shape)
    total_loss = g_sm.reshape(-1)[P_SMALL - 1]
    return (total_loss, grad_x[None], *[outs[p + n] for p in ("grad_", "delta_", "new_m_", "new_v_") for n in NAMES])


def kernel(x, mem, attn_norm_g, w_in, b_forget, fox_out_g, sb_out_g, w_out, xattn_norm_g, mem_norm_g, w_mq, w_mkv, w_mo, ffn_norm_g, w_up, conv_w, conv_b, w_down, final_norm_g, loss_target, m_attn_norm_g, m_w_in, m_b_forget, m_fox_out_g, m_sb_out_g, m_w_out, m_xattn_norm_g, m_mem_norm_g, m_w_mq, m_w_mkv, m_w_mo, m_ffn_norm_g, m_w_up, m_conv_w, m_conv_b, m_w_down, m_final_norm_g, v_attn_norm_g, v_w_in, v_b_forget, v_fox_out_g, v_sb_out_g, v_w_out, v_xattn_norm_g, v_mem_norm_g, v_w_mq, v_w_mkv, v_w_mo, v_ffn_norm_g, v_w_up, v_conv_w, v_conv_b, v_w_down, v_final_norm_g):
    given = dict(locals())
    w = {n: given[n] for n in NAMES}
    m = {n: given["m_" + n] for n in NAMES}
    v = {n: given["v_" + n] for n in NAMES}
    return _step(x, mem, loss_target, w, m, v)
```

```python
import functools

import numpy as np
import jax
import jax.numpy as jnp
from jax import lax
from jax.experimental import pallas as pl
from jax.experimental.pallas import tpu as pltpu

F32 = jnp.float32
CDT = jnp.bfloat16
MESH = pl.DeviceIdType.MESH

D = 1024
HD = 64
NH = 8
GW = NH * HD
NQKV = 6 * GW
IN_COLS = NQKV + NH
IN_PAD = NQKV + 256
NMH = 4
MHD = D // NMH
DFF = 2816
EPS = 1e-6
ATT_SCALE = HD ** -0.5
MEM_SCALE = MHD ** -0.5
NEG = -1e30

LR, B1, B2, AEPS, WD, STEP = 0.001, 0.9, 0.999, 1e-08, 0.01, 10
BC1 = 1.0 - B1 ** STEP
BC2 = 1.0 - B2 ** STEP

ATT_TILES = {"fox_fwd": (1024, 2048), "fox_bwd": (1024, 1024), "sb_fwd": (1024, 1024), "sb_bwd": (1024, 1024)}
W_SB = 256
VMEM_LIMIT = 52 * 2 ** 20

N_CHIP = 4
BIG = (("w_in", (D, IN_COLS // N_CHIP), 1), ("w_out", (D // N_CHIP, D), 0), ("w_mq", (D // N_CHIP, D), 0),
       ("w_mkv", (D, 2 * D // N_CHIP), 1), ("w_mo", (D // N_CHIP, D), 0), ("w_up", (D, 2 * DFF // N_CHIP), 1),
       ("conv_w", (3, 2 * DFF // N_CHIP), 1), ("w_down", (DFF // N_CHIP, D), 0))
GROUPS = {"a": BIG[:1], "b": BIG[1:]}
ADAM_ROWS = 1536


def _sizes(group):
    return tuple(int(np.prod(s)) for _, s, _ in group)


def _gather_sizes(group):
    return tuple(2 * n if name == "conv_w" else n for (name, _, _), n in zip(group, _sizes(group)))


def _rows_g(group):
    return -(-sum(_gather_sizes(group)) // 4096) * 32


def _rows_f(group):
    return -(-sum(_sizes(group)) // 65536) * 512
SMALL = (("attn_norm_g", 1024), ("b_forget", 8), ("fox_out_g", 512), ("sb_out_g", 512), ("xattn_norm_g", 1024),
         ("mem_norm_g", 1024), ("ffn_norm_g", 1024), ("conv_b", 2 * DFF), ("final_norm_g", 1024))
P_SMALL = sum(n for _, n in SMALL) + 1
ROWS_S = -(-P_SMALL // 1024) * 8


def _params(sem=None, vmem=VMEM_LIMIT):
    return pltpu.CompilerParams(dimension_semantics=sem, vmem_limit_bytes=vmem)


def _tile(n, pref, mult):
    t = (min(pref, n) // mult) * mult
    while t >= mult:
        if n % t == 0:
            return t
        t -= mult
    return n


def _dot(a, b):
    return jnp.dot(a, b, preferred_element_type=F32)


def _dot_nt(a, b):
    return lax.dot_general(a, b, (((1,), (1,)), ((), ())), preferred_element_type=F32)


def _dot_tn(a, b):
    return lax.dot_general(a, b, (((0,), (0,)), ((), ())), preferred_element_type=F32)


def _split3(x):
    h1 = x.astype(CDT)
    r1 = x - h1.astype(F32)
    h2 = r1.astype(CDT)
    h3 = (r1 - h2.astype(F32)).astype(CDT)
    return h1, h2, h3


def _rms_bwd(dh, x, g):
    r = lax.rsqrt(jnp.mean(x * x, axis=-1, keepdims=True) + EPS)
    xn = x * r
    dg = jnp.sum(dh * xn, axis=0, keepdims=True)
    dhg = dh * g
    dx = r * (dhg - xn * jnp.mean(dhg * xn, axis=-1, keepdims=True))
    return dx, dg


def _mm_nn(a, b, out_dtype, name, *, tm=1024, tn=512, residual=None, halves=False):
    M, K = a.shape
    N = b.shape[1]
    tm = _tile(M, tm, 16)
    tn = _tile(N // 2 if halves else N, tn, 128)
    nj = N // tn

    def body(*refs):
        a_ref, b_ref = refs[0], refs[1]
        o_ref = refs[-1]
        acc = _dot(a_ref[...].astype(CDT), b_ref[...].astype(CDT))
        if residual is not None:
            acc = acc + refs[2][...]
        o_ref[...] = acc.astype(o_ref.dtype)

    in_specs = [pl.BlockSpec((tm, K), lambda i, j: (i, 0)), pl.BlockSpec((K, tn), lambda i, j: (0, j))]
    ops = [a, b]
    if residual is not None:
        in_specs.append(pl.BlockSpec((tm, tn), lambda i, j: (i, j)))
        ops.append(residual)
    if halves:
        njh = nj // 2
        out_shape = jax.ShapeDtypeStruct((2, M, N // 2), out_dtype)
        out_spec = pl.BlockSpec((None, tm, tn), lambda i, j: (j // njh, i, j % njh))
    else:
        out_shape = jax.ShapeDtypeStruct((M, N), out_dtype)
        out_spec = pl.BlockSpec((tm, tn), lambda i, j: (i, j))
    return pl.pallas_call(body, name=name, grid=(M // tm, nj), in_specs=in_specs, out_specs=out_spec,
                          out_shape=out_shape, compiler_params=_params(("parallel", "parallel")))(*ops)


def _mm_tn(a, b, name, *, tka=512, tn=1024, ts=512, b_halves=False):
    S, Ka = a.shape
    N = 2 * b.shape[2] if b_halves else b.shape[1]
    tka = _tile(Ka, tka, 128)
    tn = _tile(N // 2 if b_halves else N, tn, 128)
    ts = _tile(S, ts, 16)
    nn = N // tn

    def body(a_ref, b_ref, o_ref):
        @pl.when(pl.program_id(2) == 0)
        def _():
            o_ref[...] = jnp.zeros_like(o_ref)
        o_ref[...] += _dot_tn(a_ref[...].astype(CDT), b_ref[...].astype(CDT))

    if b_halves:
        nnh = nn // 2
        b_spec = pl.BlockSpec((None, ts, tn), lambda i, j, s: (j // nnh, s, j % nnh))
    else:
        b_spec = pl.BlockSpec((ts, tn), lambda i, j, s: (s, j))
    return pl.pallas_call(
        body, name=name, grid=(Ka // tka, nn, S // ts),
        in_specs=[pl.BlockSpec((ts, tka), lambda i, j, s: (s, i)), b_spec],
        out_specs=pl.BlockSpec((tka, tn), lambda i, j, s: (i, j)),
        out_shape=jax.ShapeDtypeStruct((Ka, N), F32),
        compiler_params=_params(("parallel", "parallel", "arbitrary")))(a, b)


def _mm_nt(a, b, name, *, tm=512, tn=None, tk=None, a_halves=False, out_dtype=F32,
           epilogue=None, extra=(), extra_specs=(), out_shape=None, out_specs=None):
    if a_halves:
        M, K = a.shape[1], 2 * a.shape[2]
    else:
        M, K = a.shape
    N = b.shape[0]
    tm = _tile(M, tm, 16)
    tn = N if (epilogue is not None or tn is None) else _tile(N, tn, 128)
    tk = K if tk is None else _tile(K // 2 if a_halves else K, tk, 128)
    nk = K // tk
    n_extra = len(extra)

    def body(*refs):
        a_ref, b_ref = refs[0], refs[1]
        extra_refs = refs[2:2 + n_extra]
        out_refs = refs[2 + n_extra:-1]
        acc_ref = refs[-1]
        k = pl.program_id(2)

        @pl.when(k == 0)
        def _():
            acc_ref[...] = jnp.zeros_like(acc_ref)
        acc_ref[...] += _dot_nt(a_ref[...].astype(CDT), b_ref[...].astype(CDT))

        @pl.when(k == nk - 1)
        def _():
            if epilogue is None:
                out_refs[0][...] = acc_ref[...].astype(out_refs[0].dtype)
            else:
                epilogue(acc_ref[...], pl.program_id(0), extra_refs, out_refs)

    if a_halves:
        nkh = nk // 2
        a_spec = pl.BlockSpec((None, tm, tk), lambda i, j, k: (k // nkh, i, k % nkh))
    else:
        a_spec = pl.BlockSpec((tm, tk), lambda i, j, k: (i, k))
    if epilogue is None:
        out_shape = jax.ShapeDtypeStruct((M, N), out_dtype)
        out_specs = pl.BlockSpec((tm, tn), lambda i, j, k: (i, j))
        sem = ("parallel", "parallel", "arbitrary")
    else:
        sem = ("arbitrary", "arbitrary", "arbitrary")
    return pl.pallas_call(
        body, name=name, grid=(M // tm, N // tn, nk),
        in_specs=[a_spec, pl.BlockSpec((tn, tk), lambda i, j, k: (j, k)), *extra_specs],
        out_specs=out_specs, out_shape=out_shape,
        scratch_shapes=[pltpu.VMEM((tm, tn), F32)],
        compiler_params=_params(sem))(a, b, *extra)


def _mm_nt_rmsbwd(a, b, x, g, dres, name, *, tm=512, tk=None, a_halves=False):
    M = x.shape[0]
    tm = _tile(M, tm, 16)

    def epilogue(acc, i, extra_refs, out_refs):
        x_ref, g_ref, r_ref = extra_refs
        dx_ref, dg_ref = out_refs
        dx, dg = _rms_bwd(acc, x_ref[...], g_ref[...])
        dx_ref[...] = r_ref[...] + dx

        @pl.when(i == 0)
        def _():
            dg_ref[...] = jnp.zeros_like(dg_ref)
        dg_ref[...] += dg

    row = pl.BlockSpec((tm, D), lambda i, j, k: (i, 0))
    vec = pl.BlockSpec((1, D), lambda i, j, k: (0, 0))
    return _mm_nt(a, b, name, tm=tm, tk=tk, a_halves=a_halves, epilogue=epilogue,
                  extra=(x, g, dres), extra_specs=(row, vec, row),
                  out_shape=(jax.ShapeDtypeStruct((M, D), F32), jax.ShapeDtypeStruct((1, D), F32)),
                  out_specs=(row, vec))


def _rms_cast(x, g, name, *, tm=512):
    M, W = x.shape
    tm = _tile(M, tm, 16)

    def body(x_ref, g_ref, o_ref):
        xf = x_ref[...]
        r = lax.rsqrt(jnp.mean(xf * xf, axis=-1, keepdims=True) + EPS)
        o_ref[...] = (xf * r * g_ref[...]).astype(o_ref.dtype)

    return pl.pallas_call(body, name=name, grid=(M // tm,),
                          in_specs=[pl.BlockSpec((tm, W), lambda i: (i, 0)), pl.BlockSpec((1, W), lambda i: (0, 0))],
                          out_specs=pl.BlockSpec((tm, W), lambda i: (i, 0)),
                          out_shape=jax.ShapeDtypeStruct((M, W), CDT),
                          compiler_params=_params(("parallel",)))(x, g)


def _tri(n, lower):
    r = lax.broadcasted_iota(jnp.int32, (n, n), 0)
    c = lax.broadcasted_iota(jnp.int32, (n, n), 1)
    return (c <= r if lower else c >= r).astype(CDT)


def _gate_fwd(fl, b, name, *, tm=512):
    S = fl.shape[0]
    tm = _tile(S, tm, 16)

    def body(f_ref, b_ref, c_ref, carry):
        @pl.when(pl.program_id(0) == 0)
        def _():
            carry[...] = jnp.zeros_like(carry)
        z = f_ref[...] + b_ref[...]
        lf = jnp.minimum(z, 0.0) - jnp.log(1.0 + jnp.exp(-jnp.abs(z)))
        tri = _tri(tm, True)
        cum = sum(_dot(tri, p) for p in _split3(lf)) + carry[...]
        c_ref[...] = cum
        carry[...] = cum[tm - 1:tm, :]

    return pl.pallas_call(body, name=name, grid=(S // tm,),
                          in_specs=[pl.BlockSpec((tm, 128), lambda i: (i, 0)), pl.BlockSpec((1, 128), lambda i: (0, 0))],
                          out_specs=pl.BlockSpec((tm, 128), lambda i: (i, 0)),
                          out_shape=jax.ShapeDtypeStruct((S, 128), F32),
                          scratch_shapes=[pltpu.VMEM((1, 128), F32)],
                          compiler_params=_params(("arbitrary",)))(fl, b)


def _gate_bwd(dck, dcq, fl, b, name, *, tm=512):
    S = fl.shape[0]
    tm = _tile(S, tm, 16)
    nb = S // tm

    def body(dck_ref, dcq_ref, f_ref, b_ref, df_ref, db_ref, carry):
        @pl.when(pl.program_id(0) == 0)
        def _():
            carry[...] = jnp.zeros_like(carry)
            db_ref[...] = jnp.zeros_like(db_ref)
        lane = lax.broadcasted_iota(jnp.int32, (1, 128), 1)
        dc = dck_ref[...]
        for h in range(NH):
            dc = dc + dcq_ref[h] * (lane == h).astype(F32)
        tri = _tri(tm, False)
        suf = sum(_dot(tri, p) for p in _split3(dc)) + carry[...]
        carry[...] = suf[0:1, :]
        df = suf * jax.nn.sigmoid(-(f_ref[...] + b_ref[...]))
        df_ref[...] = df
        db_ref[...] += jnp.sum(df, axis=0, keepdims=True)

    rev = pl.BlockSpec((tm, 128), lambda i: (nb - 1 - i, 0))
    cols = pl.BlockSpec((NH, tm, 1), lambda i: (0, nb - 1 - i, 0))
    vec = pl.BlockSpec((1, 128), lambda i: (0, 0))
    return pl.pallas_call(body, name=name, grid=(nb,), in_specs=[rev, cols, rev, vec], out_specs=(rev, vec),
                          out_shape=(jax.ShapeDtypeStruct((S, 128), F32), jax.ShapeDtypeStruct((1, 128), F32)),
                          scratch_shapes=[pltpu.VMEM((1, 128), F32)],
                          compiler_params=_params(("arbitrary",)))(dck, dcq, fl, b)


def _group_rows(o_ref):
    return jnp.concatenate([o_ref[h] for h in range(NH)], axis=1)


def _out_proj(fo, so, gf, gs, w_out, x0, name, *, tm=512):
    S = fo.shape[1]
    tm = _tile(S, tm, 16)

    def body(fo_ref, so_ref, gf_ref, gs_ref, w_ref, x_ref, x1_ref, mx_ref):
        for ref, g_ref, lo in ((fo_ref, gf_ref, 0), (so_ref, gs_ref, GW)):
            o = _group_rows(ref)
            r = lax.rsqrt(jnp.mean(o * o, axis=-1, keepdims=True) + EPS)
            mx_ref[:, lo:lo + GW] = (o * r * g_ref[...]).astype(CDT)
        x1_ref[...] = x_ref[...] + _dot(mx_ref[...], w_ref[...])

    half = pl.BlockSpec((NH, tm, HD), lambda i: (0, i, 0))
    gvec = pl.BlockSpec((1, GW), lambda i: (0, 0))
    row = pl.BlockSpec((tm, D), lambda i: (i, 0))
    return pl.pallas_call(body, name=name, grid=(S // tm,),
                          in_specs=[half, half, gvec, gvec, pl.BlockSpec((D, D), lambda i: (0, 0)), row],
                          out_specs=(row, row),
                          out_shape=(jax.ShapeDtypeStruct((S, D), F32), jax.ShapeDtypeStruct((S, D), CDT)),
                          compiler_params=_params(("parallel",)))(fo, so, gf, gs, w_out, x0)


def _out_proj_bwd(dx1, w_out, fo, so, gf, gs, name, *, tm=512):
    S = fo.shape[1]
    tm = _tile(S, tm, 16)

    def epilogue(acc, i, extra_refs, out_refs):
        fo_ref, so_ref, gf_ref, gs_ref = extra_refs
        dfo_ref, dso_ref, dgf_ref, dgs_ref = out_refs

        @pl.when(i == 0)
        def _():
            dgf_ref[...] = jnp.zeros_like(dgf_ref)
            dgs_ref[...] = jnp.zeros_like(dgs_ref)
        for lo, o_ref, g_ref, do_ref, dg_ref in ((0, fo_ref, gf_ref, dfo_ref, dgf_ref), (GW, so_ref, gs_ref, dso_ref, dgs_ref)):
            dx, dg = _rms_bwd(acc[:, lo:lo + GW], _group_rows(o_ref), g_ref[...])
            for h in range(NH):
                do_ref[h] = dx[:, h * HD:(h + 1) * HD].astype(do_ref.dtype)
            dg_ref[...] += dg

    half = pl.BlockSpec((NH, tm, HD), lambda i, j, k: (0, i, 0))
    gvec = pl.BlockSpec((1, GW), lambda i, j, k: (0, 0))
    return _mm_nt(dx1, w_out, name, tm=tm, epilogue=epilogue, extra=(fo, so, gf, gs),
                  extra_specs=(half, half, gvec, gvec),
                  out_shape=(jax.ShapeDtypeStruct((NH, S, HD), CDT), jax.ShapeDtypeStruct((NH, S, HD), CDT),
                             jax.ShapeDtypeStruct((1, GW), F32), jax.ShapeDtypeStruct((1, GW), F32)),
                  out_specs=(half, half, gvec, gvec))


def _loss_bwd(x3, tgt, g, name, *, tm=512):
    S = x3.shape[0]
    tm = _tile(S, tm, 16)

    def body(x_ref, t_ref, g_ref, dx_ref, loss_ref, dg_ref):
        @pl.when(pl.program_id(0) == 0)
        def _():
            loss_ref[...] = jnp.zeros_like(loss_ref)
            dg_ref[...] = jnp.zeros_like(dg_ref)
        x = x_ref[...]
        gv = g_ref[...]
        r = lax.rsqrt(jnp.mean(x * x, axis=-1, keepdims=True) + EPS)
        xn = x * r
        err = xn * gv - t_ref[...]
        loss_ref[...] += jnp.full(loss_ref.shape, 0.5 * jnp.sum(jnp.mean(err * err, axis=-1, keepdims=True)), F32)
        dy = err * (1.0 / D)
        dg_ref[...] += jnp.sum(dy * xn, axis=0, keepdims=True)
        dyg = dy * gv
        dx_ref[...] = r * (dyg - xn * jnp.mean(dyg * xn, axis=-1, keepdims=True))

    row = pl.BlockSpec((tm, D), lambda i: (i, 0))
    vec = pl.BlockSpec((1, D), lambda i: (0, 0))
    dx3, loss, dg = pl.pallas_call(
        body, name=name, grid=(S // tm,), in_specs=[row, row, vec],
        out_specs=(row, pl.BlockSpec((1, 128), lambda i: (0, 0)), vec),
        out_shape=(jax.ShapeDtypeStruct((S, D), F32), jax.ShapeDtypeStruct((1, 128), F32), jax.ShapeDtypeStruct((1, D), F32)),
        compiler_params=_params(("arbitrary",)))(x3, tgt, g)
    return loss, dx3, dg


MASKED, FIRST, LAST = 1, 2, 4


def _att_tiles(name, S):
    tq, tk = ATT_TILES[name]
    return min(tq, S), min(tk, S)


def _pairs(S, tq, tk, descending=True):
    assert tk % tq == 0 and S % tk == 0
    qi, kj, fl = [], [], []
    for i in range(S // tq):
        last = ((i + 1) * tq - 1) // tk
        order = list(range(last, -1, -1) if descending else range(last + 1))
        for pos, kb in enumerate(order):
            qi.append(i)
            kj.append(kb)
            fl.append((MASKED if (kb + 1) * tk - 1 > i * tq else 0) | (FIRST if pos == 0 else 0) | (LAST if pos == last else 0))
    return tuple(jnp.asarray(np.asarray(a, np.int32)) for a in (qi, kj, fl))


def _head_blk(rows, by_key, head0, width=HD):
    if by_key:
        return pl.BlockSpec((1, rows, width), lambda h, n, qi, kj, fl: (h + head0, kj[n], 0))
    return pl.BlockSpec((1, rows, width), lambda h, n, qi, kj, fl: (h + head0, qi[n], 0))


def _att_specs(tq, tk, width=HD):
    qblk = pl.BlockSpec((1, tq, width), lambda h, n, qi, kj, fl: (h, qi[n], 0))
    kblk = pl.BlockSpec((1, tk, width), lambda h, n, qi, kj, fl: (h, kj[n], 0))
    qcol = pl.BlockSpec((1, tq, 1), lambda h, n, qi, kj, fl: (h, qi[n], 0))
    return qblk, kblk, qcol


def _causal(tq, w, ahead, strict):
    diff = lax.broadcasted_iota(jnp.int32, (tq, w), 1) - lax.broadcasted_iota(jnp.int32, (tq, w), 0)
    return diff < ahead if strict else diff <= ahead


def _masked_or_not(flags, step):
    pl.when(flags % 2 == 1)(functools.partial(step, True))
    pl.when(flags % 2 == 0)(functools.partial(step, False))


FOX_DEPTH = 2 * HD


def _fox_operands(qkv, cum, name, *, tm=512):
    S = qkv.shape[1]
    tm = _tile(S, tm, 16)

    def body(q_ref, k_ref, c_ref, q2_ref, k2_ref):
        lane = lax.broadcasted_iota(jnp.int32, (1, HD), 1)
        c = c_ref[...]
        for h in range(NH):
            pieces = [p.astype(F32) for p in _split3(c[:, h:h + 1])]
            qa = sum(jnp.where(lane == 2 * n, pieces[n], 0.0) for n in range(3)) + jnp.where((lane < 6) & (lane % 2 == 1), 1.0, 0.0)
            ka = sum(jnp.where(lane == 2 * n + 1, -pieces[n], 0.0) for n in range(3)) + jnp.where((lane < 6) & (lane % 2 == 0), 1.0, 0.0)
            q2_ref[h] = jnp.concatenate([qa.astype(CDT), q_ref[h] * ATT_SCALE], axis=1)
            k2_ref[h] = jnp.concatenate([ka.astype(CDT), k_ref[h]], axis=1)

    wide = pl.BlockSpec((NH, tm, FOX_DEPTH), lambda i: (0, i, 0))
    shp = jax.ShapeDtypeStruct((NH, S, FOX_DEPTH), CDT)
    return pl.pallas_call(body, name=name, grid=(S // tm,),
                          in_specs=[pl.BlockSpec((NH, tm, HD), lambda i: (0, i, 0)), pl.BlockSpec((NH, tm, HD), lambda i: (1, i, 0)),
                                    pl.BlockSpec((tm, 128), lambda i: (i, 0))],
                          out_specs=(wide, wide), out_shape=(shp, shp), compiler_params=_params(("parallel",)))(qkv, qkv, cum)


def _fox_fwd(q2, k2, v, name, slots=None, v0=0):
    S = q2.shape[1]
    tq, tk = _att_tiles("fox_fwd", S)
    qi, kj, fl = _pairs(S, tq, tk)
    qblk, kblk, qcol = _att_specs(tq, tk)
    q2blk, k2blk, _ = _att_specs(tq, tk, FOX_DEPTH)
    npairs = int(qi.shape[0])

    def body(qi_ref, kj_ref, fl_ref, q2_ref, k2_ref, v_ref, *rest):
        if slots is None:
            o_ref, lse_ref, m_s, l_s, acc_s = rest
        else:
            _, o_ref, lse_ref, slots_ref, m_s, l_s, acc_s, send_sems, recv_sems = rest
        h, n = pl.program_id(0), pl.program_id(1)
        i, kb, flags = qi_ref[n], kj_ref[n], fl_ref[n]
        if slots is not None:
            _gather_steps(slots_ref, send_sems, recv_sems, first=(h == 0) & (n == 0), middle=(h == NH // 2) & (n == 0),
                          last=(h == NH - 1) & (n == npairs - 1))

        @pl.when(flags & FIRST != 0)
        def _():
            m_s[...] = jnp.full_like(m_s, NEG)
            l_s[...] = jnp.zeros_like(l_s)
            acc_s[...] = jnp.zeros_like(acc_s)

        def step(masked):
            s = _dot_nt(q2_ref[0], k2_ref[0])
            if masked:
                s = jnp.where(_causal(tq, tk, i * tq - kb * tk, False), s, NEG)
            m_new = jnp.maximum(m_s[...], jnp.max(s, axis=-1, keepdims=True))
            alpha = jnp.exp(m_s[...] - m_new)
            p = jnp.exp(s - m_new)
            l_s[...] = alpha * l_s[...] + jnp.sum(p, axis=-1, keepdims=True)
            acc_s[...] = alpha * acc_s[...] + _dot(p.astype(CDT), v_ref[0])
            m_s[...] = m_new

        _masked_or_not(flags, step)

        @pl.when(flags & LAST != 0)
        def _():
            o_ref[0] = acc_s[...] / l_s[...]
            lse_ref[0] = m_s[...] + jnp.log(l_s[...])

    scratch = [pltpu.VMEM((tq, 1), F32), pltpu.VMEM((tq, 1), F32), pltpu.VMEM((tq, HD), F32)]
    out_shape = (jax.ShapeDtypeStruct((NH, S, HD), F32), jax.ShapeDtypeStruct((NH, S, 1), F32))
    in_specs = [q2blk, k2blk, _head_blk(tk, True, v0)]
    if slots is None:
        grid_spec = pltpu.PrefetchScalarGridSpec(num_scalar_prefetch=3, grid=(NH, npairs), in_specs=in_specs,
                                                 out_specs=(qblk, qcol), scratch_shapes=scratch)
        o, lse = pl.pallas_call(body, name=name, grid_spec=grid_spec, out_shape=out_shape,
                                compiler_params=_params(("parallel", "arbitrary")))(qi, kj, fl, q2, k2, v)
        return o, lse, None
    grid_spec = pltpu.PrefetchScalarGridSpec(num_scalar_prefetch=3, grid=(NH, npairs), in_specs=in_specs + [ANY],
                                             out_specs=(qblk, qcol, ANY), scratch_shapes=scratch + list(GATHER_SEMS))
    return pl.pallas_call(body, name=name, grid_spec=grid_spec, out_shape=(*out_shape, jax.ShapeDtypeStruct(slots.shape, slots.dtype)),
                          input_output_aliases={6: 2},
                          compiler_params=_params(("arbitrary", "arbitrary")))(qi, kj, fl, q2, k2, v, slots)


def _fox_bwd(q, k, v, q2, k2, o, do, lse, name, h0=(0, 0, 0), swap=None):
    S = q.shape[1]
    tq, tk = _att_tiles("fox_bwd", S)
    qi, kj, fl = _pairs(S, tq, tk)
    qblk, kblk, qcol = _att_specs(tq, tk)
    npairs = int(qi.shape[0])

    def body(qi_ref, kj_ref, fl_ref, q_ref, k_ref, v_ref, q2_ref, k2_ref, o_ref, do_ref, lse_ref, *rest):
        if swap is None:
            dq_ref, dk_ref, dv_ref, dck_ref, dcq_ref, dq_s, dl_s, dcq_s = rest
        else:
            g_ref, dq_ref, dk_ref, dv_ref, dck_ref, dcq_ref, got_ref, dq_s, dl_s, dcq_s, send_sem, recv_sem = rest
        n = pl.program_id(1)
        i, kb, flags = qi_ref[n], kj_ref[n], fl_ref[n]
        if swap is not None:
            h = pl.program_id(0)
            _swap_steps(g_ref, got_ref, send_sem, recv_sem, first=(h == 0) & (n == 0), last=(h == NH - 1) & (n == npairs - 1))

        @pl.when(n == 0)
        def _():
            dk_ref[...] = jnp.zeros_like(dk_ref)
            dv_ref[...] = jnp.zeros_like(dv_ref)
            dck_ref[...] = jnp.zeros_like(dck_ref)

        @pl.when(flags & FIRST != 0)
        def _():
            dq_s[...] = jnp.zeros_like(dq_s)
            dcq_s[...] = jnp.zeros_like(dcq_s)
            dl_s[...] = jnp.sum(do_ref[0].astype(F32) * o_ref[0], axis=-1, keepdims=True)

        def step(masked):
            qs = q_ref[0] * ATT_SCALE
            do = do_ref[0]
            p = jnp.exp(_dot_nt(q2_ref[0], k2_ref[0]) - lse_ref[0])
            if masked:
                p = jnp.where(_causal(tq, tk, i * tq - kb * tk, False), p, 0.0)
            ds = p * (_dot_nt(do, v_ref[0]) - dl_s[...])
            dsb = ds.astype(CDT)
            dq_s[...] += _dot(dsb, k_ref[0])
            rows = pl.ds(pl.multiple_of(kb * tk, tk), tk)
            dk_ref[0, rows, :] += _dot_tn(dsb, qs)
            dv_ref[0, rows, :] += _dot_tn(p.astype(CDT), do)
            dck_ref[0, :, rows] += -jnp.sum(ds, axis=0, keepdims=True)
            dcq_s[...] += jnp.sum(ds, axis=-1, keepdims=True)

        _masked_or_not(flags, step)

        @pl.when(flags & LAST != 0)
        def _():
            dq_ref[0] = (dq_s[...] * ATT_SCALE).astype(dq_ref.dtype)
            dcq_ref[0] = dcq_s[...]

    whole = pl.BlockSpec((1, S, HD), lambda h, n, qi, kj, fl: (h, 0, 0))
    q2blk, k2blk, _ = _att_specs(tq, tk, FOX_DEPTH)
    in_specs = [_head_blk(tq, False, h0[0]), _head_blk(tk, True, h0[1]), _head_blk(tk, True, h0[2]), q2blk, k2blk, qblk, qblk, qcol]
    out_specs = (qblk, whole, whole, pl.BlockSpec((1, 1, S), lambda h, n, qi, kj, fl: (h, 0, 0)), qcol)
    out_shape = (jax.ShapeDtypeStruct((NH, S, HD), CDT), jax.ShapeDtypeStruct((NH, S, HD), F32), jax.ShapeDtypeStruct((NH, S, HD), F32),
                 jax.ShapeDtypeStruct((NH, 1, S), F32), jax.ShapeDtypeStruct((NH, S, 1), F32))
    scratch = [pltpu.VMEM((tq, HD), F32), pltpu.VMEM((tq, 1), F32), pltpu.VMEM((tq, 1), F32)]
    if swap is None:
        grid_spec = pltpu.PrefetchScalarGridSpec(num_scalar_prefetch=3, grid=(NH, npairs), in_specs=in_specs,
                                                 out_specs=out_specs, scratch_shapes=scratch)
        return (*pl.pallas_call(body, name=name, grid_spec=grid_spec, out_shape=out_shape,
                                compiler_params=_params(("parallel", "arbitrary")))(qi, kj, fl, q, k, v, q2, k2, o, do, lse), None)
    grid_spec = pltpu.PrefetchScalarGridSpec(num_scalar_prefetch=3, grid=(NH, npairs), in_specs=in_specs + [ANY],
                                             out_specs=(*out_specs, ANY), scratch_shapes=scratch + list(SWAP_SEMS))
    got_shape = jax.ShapeDtypeStruct((swap.shape[0], swap.shape[1] // 2, 128), swap.dtype)
    return pl.pallas_call(body, name=name, grid_spec=grid_spec, out_shape=(*out_shape, got_shape),
                          compiler_params=_params(("arbitrary", "arbitrary")))(qi, kj, fl, q, k, v, q2, k2, o, do, lse, swap)


LOG2E = 1.4426950408889634


def _proj_qkv(h1, w_qkv, name, *, tm=1024):
    S, K = h1.shape
    tm = _tile(S, tm, 16)
    SQ, SK = 3, 4

    def heads(t):
        return [t[:, h * HD:(h + 1) * HD] for h in range(NH)]

    def body(a_ref, b_ref, o_ref, q2_ref, k2_ref):
        j = pl.program_id(1)
        ob = _dot(a_ref[...], b_ref[...]).astype(CDT)
        for h, t in enumerate(heads(ob)):
            o_ref[h] = t

        @pl.when(j == SQ)
        def _():
            qf = ob.astype(F32) * (ATT_SCALE * LOG2E)
            hi = qf.astype(CDT)
            lo = (qf - hi.astype(F32)).astype(CDT)
            for h, (th, tl) in enumerate(zip(heads(hi), heads(lo))):
                q2_ref[h] = jnp.concatenate([th, tl], axis=1)

        @pl.when(j == SK)
        def _():
            for h, t in enumerate(heads(ob)):
                k2_ref[h] = jnp.concatenate([t, t], axis=1)

    wide = pl.BlockSpec((NH, tm, 2 * HD), lambda i, j: (0, i, 0))
    return pl.pallas_call(
        body, name=name, grid=(S // tm, 6),
        in_specs=[pl.BlockSpec((tm, K), lambda i, j: (i, 0)), pl.BlockSpec((K, GW), lambda i, j: (0, j))],
        out_specs=(pl.BlockSpec((NH, tm, HD), lambda i, j: (j, i, 0)), wide, wide),
        out_shape=(jax.ShapeDtypeStruct((6 * NH, S, HD), CDT), jax.ShapeDtypeStruct((NH, S, 2 * HD), CDT),
                   jax.ShapeDtypeStruct((NH, S, 2 * HD), CDT)),
        compiler_params=_params(("parallel", "arbitrary")))(h1, w_qkv)


def _sb_softplus2(q2, k2sub, mask):
    z2 = _dot_nt(q2, k2sub)
    sp2 = jnp.maximum(z2, 0.0) + jnp.log2(1.0 + jnp.exp2(-jnp.abs(z2)))
    return z2, sp2 if mask is None else jnp.where(mask, sp2, 0.0)


def _strict_tri(n, upper, value):
    r = lax.broadcasted_iota(jnp.int32, (n, n), 0)
    c = lax.broadcasted_iota(jnp.int32, (n, n), 1)
    return jnp.where(r < c if upper else r > c, value, 0.0).astype(CDT)


def _sb_fwd(q2, k2, v, name, v0=0):
    S = q2.shape[1]
    tq, tk = _att_tiles("sb_fwd", S)
    W = min(W_SB, tk)
    qi, kj, fl = _pairs(S, tq, tk)
    qblk, kblk, qcol = _att_specs(tq, tk)
    q2blk, k2blk, _ = _att_specs(tq, tk, 2 * HD)

    def body(qi_ref, kj_ref, fl_ref, q_ref, k_ref, v_ref, o_ref, lt_ref, run_s, acc_s):
        n = pl.program_id(1)
        i, kb, flags = qi_ref[n], kj_ref[n], fl_ref[n]

        @pl.when(flags & FIRST != 0)
        def _():
            run_s[...] = jnp.zeros_like(run_s)
            acc_s[...] = jnp.zeros_like(acc_s)

        def step(masked):
            neg_later = _strict_tri(W, False, -1.0)
            run = run_s[...]
            acc = acc_s[...]
            for sub in range(tk // W - 1, -1, -1):
                cols = slice(sub * W, (sub + 1) * W)
                mask = _causal(tq, W, i * tq - kb * tk - sub * W, True) if masked else None
                z2, sp2 = _sb_softplus2(q_ref[0], k_ref[0, cols, :], mask)
                excl = _dot(sp2.astype(CDT), neg_later)
                a = jnp.exp2((z2 - sp2) + (excl + run))
                if masked:
                    a = jnp.where(mask, a, 0.0)
                acc = acc + _dot(a.astype(CDT), v_ref[0, cols, :])
                run = run + (excl[:, 0:1] - sp2[:, 0:1])
            run_s[...] = run
            acc_s[...] = acc

        _masked_or_not(flags, step)

        @pl.when(flags & LAST != 0)
        def _():
            o_ref[0] = acc_s[...]
            lt_ref[0] = run_s[...]

    grid_spec = pltpu.PrefetchScalarGridSpec(
        num_scalar_prefetch=3, grid=(NH, int(qi.shape[0])), in_specs=[q2blk, k2blk, _head_blk(tk, True, v0)], out_specs=(qblk, qcol),
        scratch_shapes=[pltpu.VMEM((tq, 1), F32), pltpu.VMEM((tq, HD), F32)])
    return pl.pallas_call(body, name=name, grid_spec=grid_spec,
                          out_shape=(jax.ShapeDtypeStruct((NH, S, HD), F32), jax.ShapeDtypeStruct((NH, S, 1), F32)),
                          compiler_params=_params(("parallel", "arbitrary")))(qi, kj, fl, q2, k2, v)


def _sb_bwd(q, q2, k2, v, do, lt, name, scatter=None, q0=0, v0=0):
    S = q.shape[1]
    tq, tk = _att_tiles("sb_bwd", S)
    W = min(W_SB, tk)
    qi, kj, fl = _pairs(S, tq, tk, descending=False)
    qblk, kblk, qcol = _att_specs(tq, tk)
    q2blk, k2blk, _ = _att_specs(tq, tk, 2 * HD)
    npairs = int(qi.shape[0])

    def body(qi_ref, kj_ref, fl_ref, q_ref, q2_ref, k2_ref, v_ref, do_ref, lt_ref, *rest):
        if scatter is None:
            dq_ref, dk_ref, dv_ref, passed_s, gsum_s, dq_s = rest
        else:
            h_ref, dq_ref, dk_ref, dv_ref, recv_ref, passed_s, gsum_s, dq_s, send_sems, recv_sems = rest
        n = pl.program_id(1)
        i, kb, flags = qi_ref[n], kj_ref[n], fl_ref[n]
        if scatter is not None:
            h = pl.program_id(0)
            _scatter_steps(h_ref, recv_ref, send_sems, recv_sems, first=(h == 0) & (n == 0), last=(h == NH - 1) & (n == npairs - 1))

        @pl.when(n == 0)
        def _():
            dk_ref[...] = jnp.zeros_like(dk_ref)
            dv_ref[...] = jnp.zeros_like(dv_ref)

        @pl.when(flags & FIRST != 0)
        def _():
            passed_s[...] = jnp.zeros_like(passed_s)
            gsum_s[...] = jnp.zeros_like(gsum_s)
            dq_s[...] = jnp.zeros_like(dq_s)

        def step(masked):
            qs = q_ref[0] * ATT_SCALE
            do = do_ref[0]
            neg_later = _strict_tri(W, False, -1.0)
            earlier = _strict_tri(W, True, 1.0)
            for sub in range(tk // W):
                cols = slice(sub * W, (sub + 1) * W)
                mask = _causal(tq, W, i * tq - kb * tk - sub * W, True) if masked else None
                ksub = k2_ref[0, cols, 0:HD]
                z2, sp2 = _sb_softplus2(q2_ref[0], k2_ref[0, cols, :], mask)
                excl = _dot(sp2.astype(CDT), neg_later)
                through = passed_s[...] + (excl[:, 0:1] - sp2[:, 0:1])
                t1 = z2 - sp2
                sig = jnp.exp2(t1)
                a = jnp.exp2(t1 + (excl + (lt_ref[0] - through)))
                if masked:
                    a = jnp.where(mask, a, 0.0)
                dl = _dot_nt(do, v_ref[0, cols, :]) * a
                before = _dot(dl.astype(CDT), earlier)
                dz = dl - sig * (dl + (before + gsum_s[...]))
                if masked:
                    dz = jnp.where(mask, dz, 0.0)
                dzb = dz.astype(CDT)
                dq_s[...] += _dot(dzb, ksub)
                rows = pl.ds(pl.multiple_of(kb * tk + sub * W, W), W)
                dk_ref[0, rows, :] += _dot_tn(dzb, qs)
                dv_ref[0, rows, :] += _dot_tn(a.astype(CDT), do)
                passed_s[...] = through
                gsum_s[...] += before[:, W - 1:W] + dl[:, W - 1:W]

        _masked_or_not(flags, step)

        @pl.when(flags & LAST != 0)
        def _():
            dq_ref[0] = (dq_s[...] * ATT_SCALE).astype(dq_ref.dtype)

    whole = pl.BlockSpec((1, S, HD), lambda h, n, qi, kj, fl: (h, 0, 0))
    in_specs = [_head_blk(tq, False, q0), q2blk, k2blk, _head_blk(tk, True, v0), qblk, qcol]
    out_specs = (qblk, whole, whole)
    out_shape = (jax.ShapeDtypeStruct((NH, S, HD), CDT), jax.ShapeDtypeStruct((NH, S, HD), F32), jax.ShapeDtypeStruct((NH, S, HD), F32))
    scratch = [pltpu.VMEM((tq, 1), F32), pltpu.VMEM((tq, 1), F32), pltpu.VMEM((tq, HD), F32)]
    if scatter is None:
        grid_spec = pltpu.PrefetchScalarGridSpec(num_scalar_prefetch=3, grid=(NH, npairs), in_specs=in_specs,
                                                 out_specs=out_specs, scratch_shapes=scratch)
        return (*pl.pallas_call(body, name=name, grid_spec=grid_spec, out_shape=out_shape,
                                compiler_params=_params(("parallel", "arbitrary")))(qi, kj, fl, q, q2, k2, v, do, lt), None)
    grid_spec = pltpu.PrefetchScalarGridSpec(num_scalar_prefetch=3, grid=(NH, npairs), in_specs=in_specs + [ANY],
                                             out_specs=(*out_specs, ANY), scratch_shapes=scratch + list(SCATTER_SEMS))
    recv_shape = jax.ShapeDtypeStruct((3,) + scatter.shape[1:], scatter.dtype)
    return pl.pallas_call(body, name=name, grid_spec=grid_spec, out_shape=(*out_shape, recv_shape),
                          compiler_params=_params(("arbitrary", "arbitrary")))(qi, kj, fl, q, q2, k2, v, do, lt, scatter)


def _mem_probs(q_ref, kv_ref, h):
    cols = slice(h * MHD, (h + 1) * MHD)
    s = _dot_nt(q_ref[:, cols], kv_ref[:, cols]) * MEM_SCALE
    e = jnp.exp(s - jnp.max(s, axis=-1, keepdims=True))
    return e / jnp.sum(e, axis=-1, keepdims=True)


def _xattn_fwd(q, kv, w_mo, x1, name, *, tm=512):
    S = q.shape[0]
    tm = _tile(S, tm, 16)
    nm = kv.shape[0]

    def body(q_ref, kv_ref, w_ref, x_ref, x2_ref, o_ref):
        for h in range(NMH):
            p = _mem_probs(q_ref, kv_ref, h)
            o_ref[:, h * MHD:(h + 1) * MHD] = _dot(p.astype(CDT), kv_ref[:, D + h * MHD:D + (h + 1) * MHD]).astype(CDT)
        x2_ref[...] = x_ref[...] + _dot(o_ref[...], w_ref[...])

    row = pl.BlockSpec((tm, D), lambda i: (i, 0))
    return pl.pallas_call(body, name=name, grid=(S // tm,),
                          in_specs=[row, pl.BlockSpec((nm, 2 * D), lambda i: (0, 0)), pl.BlockSpec((D, D), lambda i: (0, 0)), row],
                          out_specs=(row, row),
                          out_shape=(jax.ShapeDtypeStruct((S, D), F32), jax.ShapeDtypeStruct((S, D), CDT)),
                          compiler_params=_params(("parallel",)))(q, kv, w_mo, x1)


def _xattn_bwd(q, kv, do, name, *, tm=512):
    S = q.shape[0]
    tm = _tile(S, tm, 16)
    nm = kv.shape[0]

    def body(q_ref, kv_ref, do_ref, dq_ref, dkv_ref):
        @pl.when(pl.program_id(0) == 0)
        def _():
            dkv_ref[...] = jnp.zeros_like(dkv_ref)
        for h in range(NMH):
            cols = slice(h * MHD, (h + 1) * MHD)
            vcols = slice(D + h * MHD, D + (h + 1) * MHD)
            p = _mem_probs(q_ref, kv_ref, h)
            doh = do_ref[:, cols]
            dp = _dot_nt(doh, kv_ref[:, vcols])
            ds = (p * (dp - jnp.sum(p * dp, axis=-1, keepdims=True)) * MEM_SCALE).astype(CDT)
            dq_ref[:, cols] = _dot(ds, kv_ref[:, cols]).astype(CDT)
            dkv_ref[:, cols] += _dot_tn(ds, q_ref[:, cols])
            dkv_ref[:, vcols] += _dot_tn(p.astype(CDT), doh)

    row = pl.BlockSpec((tm, D), lambda i: (i, 0))
    kvs = pl.BlockSpec((nm, 2 * D), lambda i: (0, 0))
    return pl.pallas_call(body, name=name, grid=(S // tm,), in_specs=[row, kvs, row], out_specs=(row, kvs),
                          out_shape=(jax.ShapeDtypeStruct((S, D), CDT), jax.ShapeDtypeStruct((nm, 2 * D), F32)),
                          compiler_params=_params(("arbitrary",)))(q, kv, do)


HALO = 16
SLAB = 8


def _shift_down(u, prev, s):
    rolled = pltpu.roll(u, s, 0)
    top = rolled[0:SLAB]
    r = lax.broadcasted_iota(jnp.int32, top.shape, 0)
    for t in range(s):
        top = jnp.where(r == t, prev[HALO - s + t:HALO - s + t + 1, :], top)
    return jnp.concatenate([top, rolled[SLAB:]], axis=0)


def _shift_up(u, nxt, s):
    n = u.shape[0]
    rolled = pltpu.roll(u, n - s, 0)
    bottom = rolled[n - SLAB:]
    r = lax.broadcasted_iota(jnp.int32, bottom.shape, 0)
    for t in range(s):
        bottom = jnp.where(r == SLAB - s + t, nxt[t:t + 1, :], bottom)
    return jnp.concatenate([rolled[:n - SLAB], bottom], axis=0)


def _conv_taps(u_ref, h_ref, first):
    u = u_ref[...].astype(F32)
    prev = jnp.where(first, 0.0, h_ref[...].astype(F32))
    out = []
    for half in range(2):
        out.append((u[half], _shift_down(u[half], prev[half], 1), _shift_down(u[half], prev[half], 2)))
    return out


def _conv_specs(tm, tn, nsb):
    blk = pl.BlockSpec((2, tm, tn), lambda j, i: (0, i, j))
    prev = pl.BlockSpec((2, HALO, tn), lambda j, i: (0, jnp.maximum(i * (tm // HALO) - 1, 0), j))
    nxt = pl.BlockSpec((2, HALO, tn), lambda j, i: (0, jnp.minimum((i + 1) * (tm // HALO), nsb - 1), j))
    w = pl.BlockSpec((2, 3, tn), lambda j, i: (0, 0, j))
    b = pl.BlockSpec((2, 1, tn), lambda j, i: (0, 0, j))
    return blk, prev, nxt, w, b


def _conv_apply(taps, w_ref, b_ref):
    ys = []
    for half in range(2):
        u, u1, u2 = taps[half]
        w = w_ref[half]
        ys.append(b_ref[half] + u2 * w[0:1, :] + u1 * w[1:2, :] + u * w[2:3, :])
    return ys


def _conv_act(u0, cw, cb, name, *, tm=2048, tn=256):
    _, S, F = u0.shape
    tm = _tile(S, tm, HALO)
    tn = _tile(F, tn, 128)
    blk, prev, _, w, b = _conv_specs(tm, tn, S // HALO)

    def body(u_ref, h_ref, w_ref, b_ref, a_ref):
        yg, yv = _conv_apply(_conv_taps(u_ref, h_ref, pl.program_id(1) == 0), w_ref, b_ref)
        a_ref[...] = (yg * jax.nn.sigmoid(yg) * yv).astype(a_ref.dtype)

    return pl.pallas_call(body, name=name, grid=(F // tn, S // tm), in_specs=[blk, prev, w, b],
                          out_specs=pl.BlockSpec((tm, tn), lambda j, i: (i, j)),
                          out_shape=jax.ShapeDtypeStruct((S, F), CDT),
                          compiler_params=_params(("parallel", "parallel")))(u0, u0, cw, cb)


def _conv_act_bwd(u0, da, cw, cb, name, *, tm=2048, tn=256):
    _, S, F = u0.shape
    tm = _tile(S, tm, HALO)
    tn = _tile(F, tn, 128)
    blk, prev, _, w, b = _conv_specs(tm, tn, S // HALO)

    def body(u_ref, h_ref, da_ref, w_ref, b_ref, du_ref, dwb_ref):
        @pl.when(pl.program_id(1) == 0)
        def _():
            dwb_ref[...] = jnp.zeros_like(dwb_ref)
        taps = _conv_taps(u_ref, h_ref, pl.program_id(1) == 0)
        yg, yv = _conv_apply(taps, w_ref, b_ref)
        sg = jax.nn.sigmoid(yg)
        da = da_ref[...].astype(F32)
        dus = (da * yv * sg * (1.0 + yg * (1.0 - sg)), da * yg * sg)
        for half in range(2):
            du = dus[half]
            du_ref[half] = du.astype(du_ref.dtype)
            u, u1, u2 = taps[half]
            for row, term in enumerate((du * u2, du * u1, du * u, du)):
                dwb_ref[half, row:row + 1, :] += jnp.sum(term, axis=0, keepdims=True)

    return pl.pallas_call(body, name=name, grid=(F // tn, S // tm),
                          in_specs=[blk, prev, pl.BlockSpec((tm, tn), lambda j, i: (i, j)), w, b],
                          out_specs=(blk, pl.BlockSpec((2, 4, tn), lambda j, i: (0, 0, j))),
                          out_shape=(jax.ShapeDtypeStruct((2, S, F), CDT), jax.ShapeDtypeStruct((2, 4, F), F32)),
                          compiler_params=_params(("parallel", "arbitrary")))(u0, u0, da, cw, cb)


def _conv_bwd_input(du, cw, name, *, tm=2048, tn=256):
    _, S, F = du.shape
    tm = _tile(S, tm, HALO)
    tn = _tile(F, tn, 128)
    blk, _, nxt, w, _ = _conv_specs(tm, tn, S // HALO)
    ni = S // tm

    def body(d_ref, h_ref, w_ref, o_ref):
        d = d_ref[...].astype(F32)
        nx = jnp.where(pl.program_id(1) == ni - 1, 0.0, h_ref[...].astype(F32))
        for half in range(2):
            wv = w_ref[half]
            y = d[half] * wv[2:3, :] + _shift_up(d[half], nx[half], 1) * wv[1:2, :] + _shift_up(d[half], nx[half], 2) * wv[0:1, :]
            o_ref[half] = y.astype(o_ref.dtype)

    return pl.pallas_call(body, name=name, grid=(F // tn, ni), in_specs=[blk, nxt, w], out_specs=blk,
                          out_shape=jax.ShapeDtypeStruct((2, S, F), CDT),
                          compiler_params=_params(("parallel", "parallel")))(du, du, cw)


ANY = pl.BlockSpec(memory_space=pl.ANY)


def _place():
    return lax.axis_index("x"), lax.axis_index("y"), lax.axis_index("c")


def _other_chips(x, y):
    return ((1 - x, y), (x, 1 - y), (1 - x, 1 - y))


def _when(pred, fn):
    if pred is True:
        fn()
    else:
        pl.when(pred)(fn)


GATHER_SEMS = (pltpu.SemaphoreType.DMA((6,)), pltpu.SemaphoreType.DMA((6,)))
SCATTER_SEMS = (pltpu.SemaphoreType.DMA((3,)), pltpu.SemaphoreType.DMA((3,)))


def _gather_steps(out_ref, send_sems, recv_sems, first=True, middle=True, last=True):
    half = out_ref.shape[1] // 2
    x, y, c = _place()
    chips = _other_chips(x, y)

    def part(chip, pc):
        return out_ref.at[2 * chip[0] + chip[1], pl.ds(pl.multiple_of(pc * half, 16), half), :]

    def copy(k, chip, pc, to):
        return pltpu.make_async_remote_copy(src_ref=part(chip, pc), dst_ref=part(chip, pc),
                                            send_sem=send_sems.at[k], recv_sem=recv_sems.at[k],
                                            device_id=to, device_id_type=MESH)

    def send_mine():
        for j, chip in enumerate(chips):
            copy(j, (x, y), c, (*chip, c)).start()

    def pass_on():
        for j, chip in enumerate(chips):
            copy(j, chip, c, (x, y, c)).wait_recv()
            copy(3 + j, chip, c, (x, y, 1 - c)).start()

    def finish():
        for j, chip in enumerate(chips):
            copy(3 + j, chip, 1 - c, (x, y, c)).wait_recv()
        for j, chip in enumerate(chips):
            copy(j, (x, y), c, (*chip, c)).wait_send()
            copy(3 + j, chip, c, (x, y, 1 - c)).wait_send()

    _when(first, send_mine)
    _when(middle, pass_on)
    _when(last, finish)


def _gather_weights(buf):
    def body(buf_ref, out_ref, send_sems, recv_sems):
        del buf_ref
        _gather_steps(out_ref, send_sems, recv_sems)

    return pl.pallas_call(body, name="gather_weights", in_specs=[ANY], out_specs=ANY,
                          out_shape=jax.ShapeDtypeStruct(buf.shape, buf.dtype), input_output_aliases={0: 0},
                          scratch_shapes=list(GATHER_SEMS))(buf)


def _gather_small(v):
    m = v.shape[0]

    def body(v_ref, out_ref, send_sems, recv_sems, local_sem):
        x, y, c = _place()
        me, sibling = (x, y, c), (x, y, 1 - c)
        chips = _other_chips(x, y)

        def rows(px, py, pc):
            return out_ref.at[pl.ds((4 * px + 2 * py + pc) * m, m), :]

        def copy(k, block, to, src=None):
            return pltpu.make_async_remote_copy(src_ref=rows(*block) if src is None else src, dst_ref=rows(*block),
                                                send_sem=send_sems.at[k], recv_sem=recv_sems.at[k],
                                                device_id=to, device_id_type=MESH)

        mine = pltpu.make_async_copy(v_ref, rows(*me), local_sem)
        mine.start()
        first = [copy(0, me, sibling, src=v_ref)]
        first += [copy(1 + j, me, (*chip, c), src=v_ref) for j, chip in enumerate(chips)]
        for cp in first:
            cp.start()
        passed = [copy(4 + j, (*chip, c), sibling) for j, chip in enumerate(chips)]
        for j, chip in enumerate(chips):
            copy(1 + j, (*chip, c), me).wait_recv()
            passed[j].start()
        copy(0, sibling, me).wait_recv()
        for j, chip in enumerate(chips):
            copy(4 + j, (*chip, 1 - c), me).wait_recv()
        for cp in first + passed:
            cp.wait_send()
        mine.wait()

    vm = pl.BlockSpec(memory_space=pltpu.VMEM)
    return pl.pallas_call(body, name="gather_small", in_specs=[vm], out_specs=vm,
                          out_shape=jax.ShapeDtypeStruct((8 * m, 128), v.dtype),
                          scratch_shapes=[pltpu.SemaphoreType.DMA((7,)), pltpu.SemaphoreType.DMA((7,)), pltpu.SemaphoreType.DMA])(v)


SWAP_SEMS = (pltpu.SemaphoreType.DMA, pltpu.SemaphoreType.DMA)


def _swap_steps(g_ref, out_ref, send_sem, recv_sem, first=True, last=True):
    half = out_ref.shape[1]
    x, y, c = _place()

    def copy():
        src = g_ref.at[:, pl.ds(pl.multiple_of((1 - c) * half, 8), half), :]
        return pltpu.make_async_remote_copy(src_ref=src, dst_ref=out_ref, send_sem=send_sem, recv_sem=recv_sem,
                                            device_id=(x, y, 1 - c), device_id_type=MESH)

    _when(first, lambda: copy().start())
    _when(last, lambda: copy().wait())


def _swap_halves(g, name):
    n, rows, _ = g.shape

    def body(g_ref, out_ref, send_sem, recv_sem):
        _swap_steps(g_ref, out_ref, send_sem, recv_sem)

    return pl.pallas_call(body, name=name, in_specs=[ANY], out_specs=ANY,
                          out_shape=jax.ShapeDtypeStruct((n, rows // 2, 128), g.dtype),
                          scratch_shapes=list(SWAP_SEMS))(g)


def _scatter_steps(h_ref, out_ref, send_sems, recv_sems, first=True, last=True):
    x, y, c = _place()

    def copies():
        return [pltpu.make_async_remote_copy(src_ref=h_ref.at[2 * chip[0] + chip[1]], dst_ref=out_ref.at[j],
                                             send_sem=send_sems.at[j], recv_sem=recv_sems.at[j],
                                             device_id=(*chip, c), device_id_type=MESH)
                for j, chip in enumerate(_other_chips(x, y))]

    def start():
        for cp in copies():
            cp.start()

    def finish():
        for cp in copies():
            cp.wait()

    _when(first, start)
    _when(last, finish)


def _scatter_chips(hsum):
    n, half, _ = hsum.shape

    def body(h_ref, out_ref, send_sems, recv_sems):
        _scatter_steps(h_ref, out_ref, send_sems, recv_sems)

    return pl.pallas_call(body, name="scatter_chips", in_specs=[ANY], out_specs=ANY,
                          out_shape=jax.ShapeDtypeStruct((3, half, 128), hsum.dtype),
                          scratch_shapes=list(SCATTER_SEMS))(hsum)


def _join_halves(buf, name):
    half = buf.shape[0] // 2

    def body(buf_ref, out_ref, send_sem, recv_sem):
        del buf_ref
        x, y, c = _place()
        mine = out_ref.at[pl.ds(pl.multiple_of(c * half, 8), half), :]
        other = out_ref.at[pl.ds(pl.multiple_of((1 - c) * half, 8), half), :]
        cp = pltpu.make_async_remote_copy(src_ref=mine, dst_ref=mine, send_sem=send_sem, recv_sem=recv_sem,
                                          device_id=(x, y, 1 - c), device_id_type=MESH)
        cp.start()
        cp.wait_send()
        pltpu.make_async_remote_copy(src_ref=other, dst_ref=other, send_sem=send_sem, recv_sem=recv_sem,
                                     device_id=(x, y, 1 - c), device_id_type=MESH).wait_recv()

    return pl.pallas_call(body, name=name, in_specs=[ANY], out_specs=ANY,
                          out_shape=jax.ShapeDtypeStruct(buf.shape, buf.dtype), input_output_aliases={0: 0},
                          scratch_shapes=[pltpu.SemaphoreType.DMA, pltpu.SemaphoreType.DMA])(buf)


def _add_sibling(g, recv, c_idx, name):
    n, rows, _ = g.shape
    half = rows // 2
    tr = _tile(half, ADAM_ROWS, 16)
    nb = half // tr

    def body(c_ref, g_ref, r_ref, o_ref, ob_ref):
        s = g_ref[...] + r_ref[...]
        o_ref[...] = s
        ob_ref[...] = s.astype(CDT)

    out = pl.BlockSpec((None, tr, 128), lambda k, i, c: (k, i, 0))
    grid_spec = pltpu.PrefetchScalarGridSpec(
        num_scalar_prefetch=1, grid=(n, nb),
        in_specs=[pl.BlockSpec((None, tr, 128), lambda k, i, c: (k, c[0] * nb + i, 0)), out],
        out_specs=(out, out))
    return pl.pallas_call(body, name=name, grid_spec=grid_spec,
                          out_shape=(jax.ShapeDtypeStruct((n, half, 128), F32), jax.ShapeDtypeStruct((n, half, 128), CDT)),
                          compiler_params=_params(("parallel", "parallel")))(c_idx, g, recv)


def _add_chips(hsum, recv, chip_idx, name):
    n, half, _ = hsum.shape
    tr = _tile(half, ADAM_ROWS, 16)

    def body(k_ref, h_ref, r_ref, o_ref):
        o_ref[...] = ((h_ref[...] + r_ref[0].astype(F32)) + r_ref[1].astype(F32)) + r_ref[2].astype(F32)

    grid_spec = pltpu.PrefetchScalarGridSpec(
        num_scalar_prefetch=1, grid=(half // tr,),
        in_specs=[pl.BlockSpec((None, tr, 128), lambda i, k: (k[0], i, 0)),
                  pl.BlockSpec((3, tr, 128), lambda i, k: (0, i, 0))],
        out_specs=pl.BlockSpec((tr, 128), lambda i, k: (i, 0)))
    return pl.pallas_call(body, name=name, grid_spec=grid_spec, out_shape=jax.ShapeDtypeStruct((half, 128), F32),
                          compiler_params=_params(("parallel",)))(chip_idx, hsum, recv)


def _adamw_math(g, w, m, v):
    m2 = B1 * m + (1.0 - B1) * g
    v2 = B2 * v + (1.0 - B2) * (g * g)
    delta = -LR * ((m2 / BC1) / (jnp.sqrt(v2 / BC2) + AEPS) + WD * w)
    return delta, m2, v2


def _adamw(g, w, m, v, name):
    rows, cols = g.shape
    tr = _tile(rows, max(8, (ADAM_ROWS * 128 // cols) // 8 * 8), 8)

    def body(g_ref, w_ref, m_ref, v_ref, d_ref, m2_ref, v2_ref):
        d_ref[...], m2_ref[...], v2_ref[...] = _adamw_math(g_ref[...], w_ref[...], m_ref[...], v_ref[...])

    blk = pl.BlockSpec((None, tr, cols), lambda i: (0, i, 0))
    shp = jax.ShapeDtypeStruct((1, rows, cols), F32)
    return pl.pallas_call(body, name=name, grid=(rows // tr,), in_specs=[pl.BlockSpec((tr, cols), lambda i: (i, 0))] + [blk] * 3,
                          out_specs=(blk,) * 3, out_shape=(shp,) * 3, compiler_params=_params(("parallel",)))(g, w, m, v)


def _adamw_small(parts, w, m, v, name):
    rows = w.shape[0]

    def body(p_ref, w_ref, m_ref, v_ref, g_ref, d_ref, m2_ref, v2_ref):
        g = p_ref[0]
        for k in range(1, 8):
            g = g + p_ref[k]
        g_ref[...] = g
        d_ref[...], m2_ref[...], v2_ref[...] = _adamw_math(g, w_ref[...], m_ref[...], v_ref[...])

    shp = jax.ShapeDtypeStruct((rows, 128), F32)
    return pl.pallas_call(body, name=name, out_shape=(shp,) * 4)(parts, w, m, v)


def _pack_rows(parts, rows):
    flat = jnp.concatenate([p.reshape(-1) for p in parts])
    return jnp.pad(flat, (0, rows * 128 - flat.shape[0])).reshape(rows, 128)


def _unpack(flat, sizes, shapes):
    out, off = [], 0
    for n, s in zip(sizes, shapes):
        out.append(flat[off:off + n].reshape(s))
        off += n
    return out


def _to_shards(full, shard_shape, axis):
    if axis == 0:
        return full.reshape(N_CHIP, -1)
    r, cs = shard_shape
    return full.reshape(r, N_CHIP, cs).transpose(1, 0, 2).reshape(N_CHIP, -1)


def _from_shards(sh, shard_shape, axis):
    r, cs = shard_shape
    if axis == 0:
        return sh.reshape(N_CHIP * r, cs)
    return sh.reshape(N_CHIP, r, cs).transpose(1, 0, 2).reshape(r, N_CHIP * cs)


def _local_step(x0, mem, tgt, W, gains, ex=None):
    S = x0.shape[0]
    w_in = jnp.pad(W["w_in"], ((0, 0), (0, IN_PAD - IN_COLS)))
    b_f = jnp.pad(gains["b_forget"], ((0, 0), (0, 128 - NH)))

    h1 = _rms_cast(x0, gains["attn_norm_g"], "norm_attn")
    qkv, sq2, sk2 = _proj_qkv(h1, w_in[:, :NQKV], "proj_qkv")
    fox, sb_q, sb_v = (0, NH, 2 * NH), 3 * NH, 5 * NH
    fl = _mm_nn(h1, w_in[:, NQKV:NQKV + 128], F32, "proj_gate")
    cum = _gate_fwd(fl, b_f, "gate_cumsum")
    fq2, fk2 = _fox_operands(qkv, cum, "fox_operands")
    fo_h, lse, gathered = _fox_fwd(fq2, fk2, qkv, "fox_fwd", slots=None if ex is None else ex.slots("b"), v0=fox[2])
    if ex is not None:
        W = {**W, **ex.unpack("b", gathered)}
    cw = W["conv_w"].reshape(3, 2, DFF).transpose(1, 0, 2)
    cb = gains["conv_b"].reshape(2, 1, DFF)
    so_h, s_lt = _sb_fwd(sq2, sk2, qkv, "sb_fwd", v0=sb_v)
    x1, mixed = _out_proj(fo_h, so_h, gains["fox_out_g"], gains["sb_out_g"], W["w_out"], x0, "out_proj")

    h2 = _rms_cast(x1, gains["xattn_norm_g"], "norm_xattn")
    mn = _rms_cast(mem, gains["mem_norm_g"], "norm_mem")
    mq = _mm_nn(h2, W["w_mq"], CDT, "proj_mq")
    kv = _mm_nn(mn, W["w_mkv"], CDT, "proj_mkv")
    x2, mo = _xattn_fwd(mq, kv, W["w_mo"], x1, "xattn_fwd")

    h3 = _rms_cast(x2, gains["ffn_norm_g"], "norm_ffn")
    u0 = _mm_nn(h3, W["w_up"], CDT, "ffn_up", tm=512, tn=DFF, halves=True)
    act = _conv_act(u0, cw, cb, "conv_act")
    x3 = _mm_nn(act, W["w_down"], F32, "ffn_down", tm=512, residual=x2)
    loss, dx3, dg_final = _loss_bwd(x3, tgt, gains["final_norm_g"].reshape(1, D), "loss")

    gw, gs = {}, {"final_norm_g": dg_final}
    da = _mm_nt(dx3, W["w_down"], "ffn_down_dx", tn=DFF, out_dtype=CDT)
    gw["w_down"] = _mm_tn(act, dx3, "ffn_down_dw", tka=DFF)
    du, dwb = _conv_act_bwd(u0, da, cw, cb, "conv_act_bwd")
    gw["conv_w"] = dwb[:, :3].transpose(1, 0, 2).reshape(3, 2 * DFF)
    gs["conv_b"] = dwb[:, 3].reshape(1, 2 * DFF)
    du0 = _conv_bwd_input(du, cw, "conv_bwd_input")
    gw["w_up"] = _mm_tn(h3, du0, "ffn_up_dw", tn=DFF, b_halves=True)
    dx2, gs["ffn_norm_g"] = _mm_nt_rmsbwd(du0, W["w_up"], x2, gains["ffn_norm_g"], dx3, "ffn_up_dx", tk=DFF, a_halves=True)

    dmo = _mm_nt(dx2, W["w_mo"], "mo_dx", tn=512, out_dtype=CDT)
    gw["w_mo"] = _mm_tn(mo, dx2, "mo_dw")
    dmq, dkv = _xattn_bwd(mq, kv, dmo, "xattn_bwd")
    gw["w_mq"] = _mm_tn(h2, dmq, "mq_dw")
    dx1, gs["xattn_norm_g"] = _mm_nt_rmsbwd(dmq, W["w_mq"], x1, gains["xattn_norm_g"], dx2, "mq_dx")
    gw["w_mkv"] = _mm_tn(mn, dkv, "mkv_dw")
    _, gs["mem_norm_g"] = _mm_nt_rmsbwd(dkv, W["w_mkv"], mem, gains["mem_norm_g"], jnp.zeros_like(mem), "mkv_dx")

    gw["w_out"] = _mm_tn(mixed, dx1, "out_dw")
    dfo_h, dso_h, gs["fox_out_g"], gs["sb_out_g"] = _out_proj_bwd(dx1, W["w_out"], fo_h, so_h, gains["fox_out_g"], gains["sb_out_g"], "out_dx")
    flat = None if ex is None else ex.flat("b", gw)
    dfq, dfk, dfv, dck, dcq, got = _fox_bwd(qkv, qkv, qkv, fq2, fk2, fo_h, dfo_h, lse, "fox_bwd", h0=fox, swap=flat)
    pair, pair16 = (None, None) if ex is None else ex.pair_sums("b", flat, got)
    dsq, dsk, dsv, arrived = _sb_bwd(qkv, sq2, sk2, qkv, dso_h, s_lt, "sb_bwd", scatter=pair16, q0=sb_q, v0=sb_v)
    dfl, db = _gate_bwd(jnp.pad(dck[:, 0, :].T, ((0, 0), (0, 128 - NH))), dcq, fl, b_f, "gate_bwd")
    gs["b_forget"] = db[:, :NH]
    dqkv = jnp.concatenate([dfq, dfk.astype(CDT), dfv.astype(CDT), dsq, dsk.astype(CDT), dsv.astype(CDT)], axis=0)
    dproj = jnp.concatenate([dqkv.transpose(1, 0, 2).reshape(S, NQKV), dfl.astype(CDT),
                             jnp.zeros((S, IN_PAD - NQKV - 128), CDT)], axis=1)
    gw["w_in"] = _mm_tn(h1, dproj, "in_dw", tn=IN_PAD)[:, :IN_COLS]
    dx0, gs["attn_norm_g"] = _mm_nt_rmsbwd(dproj, w_in, x0, gains["attn_norm_g"], dx1, "in_dx", tk=IN_PAD)
    return loss, dx0, gw, gs, (pair, arrived)


NAMES = ("attn_norm_g", "w_in", "b_forget", "fox_out_g", "sb_out_g", "w_out", "xattn_norm_g", "mem_norm_g", "w_mq",
         "w_mkv", "w_mo", "ffn_norm_g", "w_up", "conv_w", "conv_b", "w_down", "final_norm_g")


class _Exchange:
    def __init__(self, w):
        self.w = w
        xi, yi, ci = _place()
        self.core = ci
        self.chip = 2 * xi + yi
        self.core_idx = jnp.reshape(ci, (1,)).astype(jnp.int32)
        self.chip_idx = jnp.reshape(self.chip, (1,)).astype(jnp.int32)

    def slots(self, g):
        parts = []
        for name, shape, _ in GROUPS[g]:
            blk = self.w[name].reshape(shape)
            parts.append(lax.bitcast_convert_type(blk, CDT) if name == "conv_w" else blk.astype(CDT))
        rows = _rows_g(GROUPS[g])
        return lax.dynamic_update_slice(lax.empty((N_CHIP, rows, 128), CDT), _pack_rows(parts, rows)[None], (self.chip, 0, 0))

    def unpack(self, g, gathered):
        flat, full, off = gathered.reshape(N_CHIP, -1), {}, 0
        for (name, shape, axis), n in zip(GROUPS[g], _gather_sizes(GROUPS[g])):
            sh = flat[:, off:off + n]
            off += n
            if name == "conv_w":
                sh = lax.bitcast_convert_type(sh.reshape(N_CHIP, n // 2, 2), F32)
            full[name] = _from_shards(sh, shape, axis)
        return full

    def flat(self, g, gw):
        rows = _rows_f(GROUPS[g])
        flat = jnp.concatenate([_to_shards(gw[name], shape, axis) for name, shape, axis in GROUPS[g]], axis=1)
        return jnp.pad(flat, ((0, 0), (0, rows * 128 - flat.shape[1]))).reshape(N_CHIP, rows, 128)

    def pair_sums(self, g, flat, got=None):
        if got is None:
            got = _swap_halves(flat, "swap_halves_" + g)
        return _add_sibling(flat, got, self.core_idx, "add_sibling_" + g)

    def finish(self, g, pair, arrived):
        rows = _rows_f(GROUPS[g])
        mine = _add_chips(pair, arrived, self.chip_idx, "add_chips_" + g)
        whole = _join_halves(lax.dynamic_update_slice(lax.empty((rows, 128), F32), mine, (self.core * (rows // 2), 0)), "join_halves_" + g)
        shapes = [s for _, s, _ in GROUPS[g]]
        return {name: arr for (name, _, _), arr in zip(GROUPS[g], _unpack(whole.reshape(-1), _sizes(GROUPS[g]), shapes))}


def _step(x, mem, loss_target, w, m, v):
    ex = _Exchange(w)

    W = ex.unpack("a", _gather_weights(ex.slots("a")))
    gains = {name: w[name].reshape(1, -1) for name, _ in SMALL}

    loss, grad_x, gw, gs, (pair_b, arrived_b) = _local_step(x[0], mem[0], loss_target[0], W, gains, ex)

    grads = ex.finish("b", pair_b, arrived_b)
    pair_a, pair16_a = ex.pair_sums("a", ex.flat("a", gw))
    grads.update(ex.finish("a", pair_a, _scatter_chips(pair16_a)))
    small = jnp.concatenate([gs[name].reshape(-1) for name, _ in SMALL] + [loss[0, :1]])
    small = jnp.pad(small, (0, ROWS_S * 128 - P_SMALL)).reshape(ROWS_S, 128)
    small_parts = _gather_small(small).reshape(8, ROWS_S, 128)

    def flat_small(d):
        return _pack_rows([d[name] for name, _ in SMALL], ROWS_S)

    outs = {}
    for name, shape, _ in BIG:
        g = grads[name]
        res = _adamw(g, w[name], m[name], v[name], "adamw_" + name)
        for prefix, arr in zip(("grad_", "delta_", "new_m_", "new_v_"), (g, *res)):
            outs[prefix + name] = arr.reshape(w[name].shape)
    small_res = _adamw_small(small_parts, flat_small(w), flat_small(m), flat_small(v), "adamw_small")
    g_sm = small_res[0]
    for prefix, sm in zip(("grad_", "delta_", "new_m_", "new_v_"), small_res):
        for (name, n), arr in zip(SMALL, _unpack(sm.reshape(-1), [n for _, n in SMALL], [(n,) for _, n in SMALL])):
            outs[prefix + name] = arr.reshape(w[name].shape)
    total_loss = g_sm.reshape(-1)[P_SMALL - 1]
    return (total_loss, grad_x[None], *[outs[p + n] for p in ("grad_", "delta_", "new_m_", "new_v_") for n in NAMES])


def kernel(x, mem, attn_norm_g, w_in, b_forget, fox_out_g, sb_out_g, w_out, xattn_norm_g, mem_norm_g, w_mq, w_mkv, w_mo, ffn_norm_g, w_up, conv_w, conv_b, w_down, final_norm_g, loss_target, m_attn_norm_g, m_w_in, m_b_forget, m_fox_out_g, m_sb_out_g, m_w_out, m_xattn_norm_g, m_mem_norm_g, m_w_mq, m_w_mkv, m_w_mo, m_ffn_norm_g, m_w_up, m_conv_w, m_conv_b, m_w_down, m_final_norm_g, v_attn_norm_g, v_w_in, v_b_forget, v_fox_out_g, v_sb_out_g, v_w_out, v_xattn_norm_g, v_mem_norm_g, v_w_mq, v_w_mkv, v_w_mo, v_ffn_norm_g, v_w_up, v_conv_w, v_conv_b, v_w_down, v_final_norm_g):
    given = dict(locals())
    w = {n: given[n] for n in NAMES}
    m = {n: given["m_" + n] for n in NAMES}
    v = {n: given["v_" + n] for n in NAMES}
    return _step(x, mem, loss_target, w, m, v)
```

```python
import functools

import numpy as np
import jax
import jax.numpy as jnp
from jax import lax
from jax.experimental import pallas as pl
from jax.experimental.pallas import tpu as pltpu

F32 = jnp.float32
CDT = jnp.bfloat16
MESH = pl.DeviceIdType.MESH

D = 1024
HD = 64
NH = 8
GW = NH * HD
NQKV = 6 * GW
IN_COLS = NQKV + NH
IN_PAD = NQKV + 256
NMH = 4
MHD = D // NMH
DFF = 2816
EPS = 1e-6
ATT_SCALE = HD ** -0.5
MEM_SCALE = MHD ** -0.5
NEG = -1e30

LR, B1, B2, AEPS, WD, STEP = 0.001, 0.9, 0.999, 1e-08, 0.01, 10
BC1 = 1.0 - B1 ** STEP
BC2 = 1.0 - B2 ** STEP

ATT_TILES = {"fox_fwd": (1024, 2048), "fox_bwd": (1024, 1024), "sb_fwd": (1024, 1024), "sb_bwd": (1024, 1024)}
W_SB = 256
VMEM_LIMIT = 52 * 2 ** 20

N_CHIP = 4
BIG = (("w_in", (D, IN_COLS // N_CHIP), 1), ("w_out", (D // N_CHIP, D), 0), ("w_mq", (D // N_CHIP, D), 0),
       ("w_mkv", (D, 2 * D // N_CHIP), 1), ("w_mo", (D // N_CHIP, D), 0), ("w_up", (D, 2 * DFF // N_CHIP), 1),
       ("conv_w", (3, 2 * DFF // N_CHIP), 1), ("w_down", (DFF // N_CHIP, D), 0))
GROUPS = {"a": BIG[:1], "b": BIG[1:]}
ADAM_ROWS = 1536


def _sizes(group):
    return tuple(int(np.prod(s)) for _, s, _ in group)


def _gather_sizes(group):
    return tuple(2 * n if name == "conv_w" else n for (name, _, _), n in zip(group, _sizes(group)))


def _rows_g(group):
    return -(-sum(_gather_sizes(group)) // 4096) * 32


def _rows_f(group):
    return -(-sum(_sizes(group)) // 65536) * 512
SMALL = (("attn_norm_g", 1024), ("b_forget", 8), ("fox_out_g", 512), ("sb_out_g", 512), ("xattn_norm_g", 1024),
         ("mem_norm_g", 1024), ("ffn_norm_g", 1024), ("conv_b", 2 * DFF), ("final_norm_g", 1024))
P_SMALL = sum(n for _, n in SMALL) + 1
ROWS_S = -(-P_SMALL // 1024) * 8


def _params(sem=None, vmem=VMEM_LIMIT):
    return pltpu.CompilerParams(dimension_semantics=sem, vmem_limit_bytes=vmem)


def _tile(n, pref, mult):
    t = (min(pref, n) // mult) * mult
    while t >= mult:
        if n % t == 0:
            return t
        t -= mult
    return n


def _dot(a, b):
    return jnp.dot(a, b, preferred_element_type=F32)


def _dot_nt(a, b):
    return lax.dot_general(a, b, (((1,), (1,)), ((), ())), preferred_element_type=F32)


def _dot_tn(a, b):
    return lax.dot_general(a, b, (((0,), (0,)), ((), ())), preferred_element_type=F32)


def _split3(x):
    h1 = x.astype(CDT)
    r1 = x - h1.astype(F32)
    h2 = r1.astype(CDT)
    h3 = (r1 - h2.astype(F32)).astype(CDT)
    return h1, h2, h3


def _rms_bwd(dh, x, g):
    r = lax.rsqrt(jnp.mean(x * x, axis=-1, keepdims=True) + EPS)
    xn = x * r
    dg = jnp.sum(dh * xn, axis=0, keepdims=True)
    dhg = dh * g
    dx = r * (dhg - xn * jnp.mean(dhg * xn, axis=-1, keepdims=True))
    return dx, dg


def _mm_nn(a, b, out_dtype, name, *, tm=1024, tn=512, residual=None, halves=False):
    M, K = a.shape
    N = b.shape[1]
    tm = _tile(M, tm, 16)
    tn = _tile(N // 2 if halves else N, tn, 128)
    nj = N // tn

    def body(*refs):
        a_ref, b_ref = refs[0], refs[1]
        o_ref = refs[-1]
        acc = _dot(a_ref[...].astype(CDT), b_ref[...].astype(CDT))
        if residual is not None:
            acc = acc + refs[2][...]
        o_ref[...] = acc.astype(o_ref.dtype)

    in_specs = [pl.BlockSpec((tm, K), lambda i, j: (i, 0)), pl.BlockSpec((K, tn), lambda i, j: (0, j))]
    ops = [a, b]
    if residual is not None:
        in_specs.append(pl.BlockSpec((tm, tn), lambda i, j: (i, j)))
        ops.append(residual)
    if halves:
        njh = nj // 2
        out_shape = jax.ShapeDtypeStruct((2, M, N // 2), out_dtype)
        out_spec = pl.BlockSpec((None, tm, tn), lambda i, j: (j // njh, i, j % njh))
    else:
        out_shape = jax.ShapeDtypeStruct((M, N), out_dtype)
        out_spec = pl.BlockSpec((tm, tn), lambda i, j: (i, j))
    return pl.pallas_call(body, name=name, grid=(M // tm, nj), in_specs=in_specs, out_specs=out_spec,
                          out_shape=out_shape, compiler_params=_params(("parallel", "parallel")))(*ops)


def _mm_tn(a, b, name, *, tka=512, tn=1024, ts=512, b_halves=False):
    S, Ka = a.shape
    N = 2 * b.shape[2] if b_halves else b.shape[1]
    tka = _tile(Ka, tka, 128)
    tn = _tile(N // 2 if b_halves else N, tn, 128)
    ts = _tile(S, ts, 16)
    nn = N // tn

    def body(a_ref, b_ref, o_ref):
        @pl.when(pl.program_id(2) == 0)
        def _():
            o_ref[...] = jnp.zeros_like(o_ref)
        o_ref[...] += _dot_tn(a_ref[...].astype(CDT), b_ref[...].astype(CDT))

    if b_halves:
        nnh = nn // 2
        b_spec = pl.BlockSpec((None, ts, tn), lambda i, j, s: (j // nnh, s, j % nnh))
    else:
        b_spec = pl.BlockSpec((ts, tn), lambda i, j, s: (s, j))
    return pl.pallas_call(
        body, name=name, grid=(Ka // tka, nn, S // ts),
        in_specs=[pl.BlockSpec((ts, tka), lambda i, j, s: (s, i)), b_spec],
        out_specs=pl.BlockSpec((tka, tn), lambda i, j, s: (i, j)),
        out_shape=jax.ShapeDtypeStruct((Ka, N), F32),
        compiler_params=_params(("parallel", "parallel", "arbitrary")))(a, b)


def _mm_nt(a, b, name, *, tm=512, tn=None, tk=None, a_halves=False, out_dtype=F32,
           epilogue=None, extra=(), extra_specs=(), out_shape=None, out_specs=None, scatter=None):
    if a_halves:
        M, K = a.shape[1], 2 * a.shape[2]
    else:
        M, K = a.shape
    N = b.shape[0]
    tm = _tile(M, tm, 16)
    tn = N if (epilogue is not None or tn is None) else _tile(N, tn, 128)
    tk = K if tk is None else _tile(K // 2 if a_halves else K, tk, 128)
    nk = K // tk
    n_extra = len(extra)
    grid = (M // tm, N // tn, nk)

    def body(*refs):
        if scatter is not None:
            *refs, send_sems, recv_sems = refs
            h_ref, recv_ref = refs[2 + n_extra], refs[-2]
            refs = (*refs[:2 + n_extra], *refs[3 + n_extra:-2], refs[-1])
            at = [pl.program_id(d) for d in range(3)]
            _scatter_steps(h_ref, recv_ref, send_sems, recv_sems,
                           first=(at[0] == 0) & (at[1] == 0) & (at[2] == 0),
                           last=(at[0] == grid[0] - 1) & (at[1] == grid[1] - 1) & (at[2] == grid[2] - 1))
        a_ref, b_ref = refs[0], refs[1]
        extra_refs = refs[2:2 + n_extra]
        out_refs = refs[2 + n_extra:-1]
        acc_ref = refs[-1]
        k = pl.program_id(2)

        @pl.when(k == 0)
        def _():
            acc_ref[...] = jnp.zeros_like(acc_ref)
        acc_ref[...] += _dot_nt(a_ref[...].astype(CDT), b_ref[...].astype(CDT))

        @pl.when(k == nk - 1)
        def _():
            if epilogue is None:
                out_refs[0][...] = acc_ref[...].astype(out_refs[0].dtype)
            else:
                epilogue(acc_ref[...], pl.program_id(0), extra_refs, out_refs)

    if a_halves:
        nkh = nk // 2
        a_spec = pl.BlockSpec((None, tm, tk), lambda i, j, k: (k // nkh, i, k % nkh))
    else:
        a_spec = pl.BlockSpec((tm, tk), lambda i, j, k: (i, k))
    if epilogue is None:
        out_shape = jax.ShapeDtypeStruct((M, N), out_dtype)
        out_specs = pl.BlockSpec((tm, tn), lambda i, j, k: (i, j))
        sem = ("parallel", "parallel", "arbitrary")
    else:
        sem = ("arbitrary", "arbitrary", "arbitrary")
    in_specs = [a_spec, pl.BlockSpec((tn, tk), lambda i, j, k: (j, k)), *extra_specs]
    scratch = [pltpu.VMEM((tm, tn), F32)]
    ops = [a, b, *extra]
    if scatter is not None:
        in_specs, ops, scratch = in_specs + [ANY], ops + [scatter], scratch + list(SCATTER_SEMS)
        out_specs = (*out_specs, ANY)
        out_shape = (*out_shape, jax.ShapeDtypeStruct((3,) + scatter.shape[1:], scatter.dtype))
    return pl.pallas_call(body, name=name, grid=grid, in_specs=in_specs, out_specs=out_specs, out_shape=out_shape,
                          scratch_shapes=scratch, compiler_params=_params(sem))(*ops)


def _mm_nt_rmsbwd(a, b, x, g, dres, name, *, tm=512, tk=None, a_halves=False, scatter=None):
    M = x.shape[0]
    tm = _tile(M, tm, 16)

    def epilogue(acc, i, extra_refs, out_refs):
        x_ref, g_ref, r_ref = extra_refs
        dx_ref, dg_ref = out_refs
        dx, dg = _rms_bwd(acc, x_ref[...], g_ref[...])
        dx_ref[...] = r_ref[...] + dx

        @pl.when(i == 0)
        def _():
            dg_ref[...] = jnp.zeros_like(dg_ref)
        dg_ref[...] += dg

    row = pl.BlockSpec((tm, D), lambda i, j, k: (i, 0))
    vec = pl.BlockSpec((1, D), lambda i, j, k: (0, 0))
    return _mm_nt(a, b, name, tm=tm, tk=tk, a_halves=a_halves, epilogue=epilogue,
                  extra=(x, g, dres), extra_specs=(row, vec, row),
                  out_shape=(jax.ShapeDtypeStruct((M, D), F32), jax.ShapeDtypeStruct((1, D), F32)),
                  out_specs=(row, vec), scatter=scatter)


def _rms_cast(x, g, name, *, tm=512):
    M, W = x.shape
    tm = _tile(M, tm, 16)

    def body(x_ref, g_ref, o_ref):
        xf = x_ref[...]
        r = lax.rsqrt(jnp.mean(xf * xf, axis=-1, keepdims=True) + EPS)
        o_ref[...] = (xf * r * g_ref[...]).astype(o_ref.dtype)

    return pl.pallas_call(body, name=name, grid=(M // tm,),
                          in_specs=[pl.BlockSpec((tm, W), lambda i: (i, 0)), pl.BlockSpec((1, W), lambda i: (0, 0))],
                          out_specs=pl.BlockSpec((tm, W), lambda i: (i, 0)),
                          out_shape=jax.ShapeDtypeStruct((M, W), CDT),
                          compiler_params=_params(("parallel",)))(x, g)


def _tri(n, lower):
    r = lax.broadcasted_iota(jnp.int32, (n, n), 0)
    c = lax.broadcasted_iota(jnp.int32, (n, n), 1)
    return (c <= r if lower else c >= r).astype(CDT)


def _gate_fwd(fl, b, name, *, tm=512):
    S = fl.shape[0]
    tm = _tile(S, tm, 16)

    def body(f_ref, b_ref, c_ref, carry):
        @pl.when(pl.program_id(0) == 0)
        def _():
            carry[...] = jnp.zeros_like(carry)
        z = f_ref[...] + b_ref[...]
        lf = jnp.minimum(z, 0.0) - jnp.log(1.0 + jnp.exp(-jnp.abs(z)))
        tri = _tri(tm, True)
        cum = sum(_dot(tri, p) for p in _split3(lf)) + carry[...]
        c_ref[...] = cum
        carry[...] = cum[tm - 1:tm, :]

    return pl.pallas_call(body, name=name, grid=(S // tm,),
                          in_specs=[pl.BlockSpec((tm, 128), lambda i: (i, 0)), pl.BlockSpec((1, 128), lambda i: (0, 0))],
                          out_specs=pl.BlockSpec((tm, 128), lambda i: (i, 0)),
                          out_shape=jax.ShapeDtypeStruct((S, 128), F32),
                          scratch_shapes=[pltpu.VMEM((1, 128), F32)],
                          compiler_params=_params(("arbitrary",)))(fl, b)


def _gate_bwd(dck, dcq, fl, b, name, *, tm=512):
    S = fl.shape[0]
    tm = _tile(S, tm, 16)
    nb = S // tm

    def body(dck_ref, dcq_ref, f_ref, b_ref, df_ref, db_ref, carry):
        @pl.when(pl.program_id(0) == 0)
        def _():
            carry[...] = jnp.zeros_like(carry)
            db_ref[...] = jnp.zeros_like(db_ref)
        lane = lax.broadcasted_iota(jnp.int32, (1, 128), 1)
        dc = dck_ref[...]
        for h in range(NH):
            dc = dc + dcq_ref[h] * (lane == h).astype(F32)
        tri = _tri(tm, False)
        suf = sum(_dot(tri, p) for p in _split3(dc)) + carry[...]
        carry[...] = suf[0:1, :]
        df = suf * jax.nn.sigmoid(-(f_ref[...] + b_ref[...]))
        df_ref[...] = df
        db_ref[...] += jnp.sum(df, axis=0, keepdims=True)

    rev = pl.BlockSpec((tm, 128), lambda i: (nb - 1 - i, 0))
    cols = pl.BlockSpec((NH, tm, 1), lambda i: (0, nb - 1 - i, 0))
    vec = pl.BlockSpec((1, 128), lambda i: (0, 0))
    return pl.pallas_call(body, name=name, grid=(nb,), in_specs=[rev, cols, rev, vec], out_specs=(rev, vec),
                          out_shape=(jax.ShapeDtypeStruct((S, 128), F32), jax.ShapeDtypeStruct((1, 128), F32)),
                          scratch_shapes=[pltpu.VMEM((1, 128), F32)],
                          compiler_params=_params(("arbitrary",)))(dck, dcq, fl, b)


def _group_rows(o_ref):
    return jnp.concatenate([o_ref[h] for h in range(NH)], axis=1)


def _out_proj(fo, so, gf, gs, w_out, x0, name, *, tm=512):
    S = fo.shape[1]
    tm = _tile(S, tm, 16)

    def body(fo_ref, so_ref, gf_ref, gs_ref, w_ref, x_ref, x1_ref, mx_ref):
        for ref, g_ref, lo in ((fo_ref, gf_ref, 0), (so_ref, gs_ref, GW)):
            o = _group_rows(ref)
            r = lax.rsqrt(jnp.mean(o * o, axis=-1, keepdims=True) + EPS)
            mx_ref[:, lo:lo + GW] = (o * r * g_ref[...]).astype(CDT)
        x1_ref[...] = x_ref[...] + _dot(mx_ref[...], w_ref[...])

    half = pl.BlockSpec((NH, tm, HD), lambda i: (0, i, 0))
    gvec = pl.BlockSpec((1, GW), lambda i: (0, 0))
    row = pl.BlockSpec((tm, D), lambda i: (i, 0))
    return pl.pallas_call(body, name=name, grid=(S // tm,),
                          in_specs=[half, half, gvec, gvec, pl.BlockSpec((D, D), lambda i: (0, 0)), row],
                          out_specs=(row, row),
                          out_shape=(jax.ShapeDtypeStruct((S, D), F32), jax.ShapeDtypeStruct((S, D), CDT)),
                          compiler_params=_params(("parallel",)))(fo, so, gf, gs, w_out, x0)


def _out_proj_bwd(dx1, w_out, fo, so, gf, gs, name, *, tm=512):
    S = fo.shape[1]
    tm = _tile(S, tm, 16)

    def epilogue(acc, i, extra_refs, out_refs):
        fo_ref, so_ref, gf_ref, gs_ref = extra_refs
        dfo_ref, dso_ref, dgf_ref, dgs_ref = out_refs

        @pl.when(i == 0)
        def _():
            dgf_ref[...] = jnp.zeros_like(dgf_ref)
            dgs_ref[...] = jnp.zeros_like(dgs_ref)
        for lo, o_ref, g_ref, do_ref, dg_ref in ((0, fo_ref, gf_ref, dfo_ref, dgf_ref), (GW, so_ref, gs_ref, dso_ref, dgs_ref)):
            dx, dg = _rms_bwd(acc[:, lo:lo + GW], _group_rows(o_ref), g_ref[...])
            for h in range(NH):
                do_ref[h] = dx[:, h * HD:(h + 1) * HD].astype(do_ref.dtype)
            dg_ref[...] += dg

    half = pl.BlockSpec((NH, tm, HD), lambda i, j, k: (0, i, 0))
    gvec = pl.BlockSpec((1, GW), lambda i, j, k: (0, 0))
    return _mm_nt(dx1, w_out, name, tm=tm, epilogue=epilogue, extra=(fo, so, gf, gs),
                  extra_specs=(half, half, gvec, gvec),
                  out_shape=(jax.ShapeDtypeStruct((NH, S, HD), CDT), jax.ShapeDtypeStruct((NH, S, HD), CDT),
                             jax.ShapeDtypeStruct((1, GW), F32), jax.ShapeDtypeStruct((1, GW), F32)),
                  out_specs=(half, half, gvec, gvec))


def _loss_bwd(x3, tgt, g, name, *, tm=512):
    S = x3.shape[0]
    tm = _tile(S, tm, 16)

    def body(x_ref, t_ref, g_ref, dx_ref, loss_ref, dg_ref):
        @pl.when(pl.program_id(0) == 0)
        def _():
            loss_ref[...] = jnp.zeros_like(loss_ref)
            dg_ref[...] = jnp.zeros_like(dg_ref)
        x = x_ref[...]
        gv = g_ref[...]
        r = lax.rsqrt(jnp.mean(x * x, axis=-1, keepdims=True) + EPS)
        xn = x * r
        err = xn * gv - t_ref[...]
        loss_ref[...] += jnp.full(loss_ref.shape, 0.5 * jnp.sum(jnp.mean(err * err, axis=-1, keepdims=True)), F32)
        dy = err * (1.0 / D)
        dg_ref[...] += jnp.sum(dy * xn, axis=0, keepdims=True)
        dyg = dy * gv
        dx_ref[...] = r * (dyg - xn * jnp.mean(dyg * xn, axis=-1, keepdims=True))

    row = pl.BlockSpec((tm, D), lambda i: (i, 0))
    vec = pl.BlockSpec((1, D), lambda i: (0, 0))
    dx3, loss, dg = pl.pallas_call(
        body, name=name, grid=(S // tm,), in_specs=[row, row, vec],
        out_specs=(row, pl.BlockSpec((1, 128), lambda i: (0, 0)), vec),
        out_shape=(jax.ShapeDtypeStruct((S, D), F32), jax.ShapeDtypeStruct((1, 128), F32), jax.ShapeDtypeStruct((1, D), F32)),
        compiler_params=_params(("arbitrary",)))(x3, tgt, g)
    return loss, dx3, dg


MASKED, FIRST, LAST = 1, 2, 4


def _att_tiles(name, S):
    tq, tk = ATT_TILES[name]
    return min(tq, S), min(tk, S)


def _pairs(S, tq, tk, descending=True):
    assert tk % tq == 0 and S % tk == 0
    qi, kj, fl = [], [], []
    for i in range(S // tq):
        last = ((i + 1) * tq - 1) // tk
        order = list(range(last, -1, -1) if descending else range(last + 1))
        for pos, kb in enumerate(order):
            qi.append(i)
            kj.append(kb)
            fl.append((MASKED if (kb + 1) * tk - 1 > i * tq else 0) | (FIRST if pos == 0 else 0) | (LAST if pos == last else 0))
    return tuple(jnp.asarray(np.asarray(a, np.int32)) for a in (qi, kj, fl))


def _head_blk(rows, by_key, head0, width=HD):
    if by_key:
        return pl.BlockSpec((1, rows, width), lambda h, n, qi, kj, fl: (h + head0, kj[n], 0))
    return pl.BlockSpec((1, rows, width), lambda h, n, qi, kj, fl: (h + head0, qi[n], 0))


def _att_specs(tq, tk, width=HD):
    qblk = pl.BlockSpec((1, tq, width), lambda h, n, qi, kj, fl: (h, qi[n], 0))
    kblk = pl.BlockSpec((1, tk, width), lambda h, n, qi, kj, fl: (h, kj[n], 0))
    qcol = pl.BlockSpec((1, tq, 1), lambda h, n, qi, kj, fl: (h, qi[n], 0))
    return qblk, kblk, qcol


def _causal(tq, w, ahead, strict):
    diff = lax.broadcasted_iota(jnp.int32, (tq, w), 1) - lax.broadcasted_iota(jnp.int32, (tq, w), 0)
    return diff < ahead if strict else diff <= ahead


def _masked_or_not(flags, step):
    pl.when(flags % 2 == 1)(functools.partial(step, True))
    pl.when(flags % 2 == 0)(functools.partial(step, False))


FOX_DEPTH = 2 * HD


def _fox_operands(qkv, cum, name, *, tm=512):
    S = qkv.shape[1]
    tm = _tile(S, tm, 16)

    def body(q_ref, k_ref, c_ref, q2_ref, k2_ref):
        lane = lax.broadcasted_iota(jnp.int32, (1, HD), 1)
        c = c_ref[...]
        for h in range(NH):
            pieces = [p.astype(F32) for p in _split3(c[:, h:h + 1])]
            qa = sum(jnp.where(lane == 2 * n, pieces[n], 0.0) for n in range(3)) + jnp.where((lane < 6) & (lane % 2 == 1), 1.0, 0.0)
            ka = sum(jnp.where(lane == 2 * n + 1, -pieces[n], 0.0) for n in range(3)) + jnp.where((lane < 6) & (lane % 2 == 0), 1.0, 0.0)
            q2_ref[h] = jnp.concatenate([qa.astype(CDT), q_ref[h] * ATT_SCALE], axis=1)
            k2_ref[h] = jnp.concatenate([ka.astype(CDT), k_ref[h]], axis=1)

    wide = pl.BlockSpec((NH, tm, FOX_DEPTH), lambda i: (0, i, 0))
    shp = jax.ShapeDtypeStruct((NH, S, FOX_DEPTH), CDT)
    return pl.pallas_call(body, name=name, grid=(S // tm,),
                          in_specs=[pl.BlockSpec((NH, tm, HD), lambda i: (0, i, 0)), pl.BlockSpec((NH, tm, HD), lambda i: (1, i, 0)),
                                    pl.BlockSpec((tm, 128), lambda i: (i, 0))],
                          out_specs=(wide, wide), out_shape=(shp, shp), compiler_params=_params(("parallel",)))(qkv, qkv, cum)


def _fox_fwd(q2, k2, v, name, slots=None, v0=0):
    S = q2.shape[1]
    tq, tk = _att_tiles("fox_fwd", S)
    qi, kj, fl = _pairs(S, tq, tk)
    qblk, kblk, qcol = _att_specs(tq, tk)
    q2blk, k2blk, _ = _att_specs(tq, tk, FOX_DEPTH)
    npairs = int(qi.shape[0])

    def body(qi_ref, kj_ref, fl_ref, q2_ref, k2_ref, v_ref, *rest):
        if slots is None:
            o_ref, lse_ref, m_s, l_s, acc_s = rest
        else:
            _, o_ref, lse_ref, slots_ref, m_s, l_s, acc_s, send_sems, recv_sems = rest
        h, n = pl.program_id(0), pl.program_id(1)
        i, kb, flags = qi_ref[n], kj_ref[n], fl_ref[n]
        if slots is not None:
            _gather_steps(slots_ref, send_sems, recv_sems, first=(h == 0) & (n == 0), middle=(h == NH // 2) & (n == 0),
                          last=(h == NH - 1) & (n == npairs - 1))

        @pl.when(flags & FIRST != 0)
        def _():
            m_s[...] = jnp.full_like(m_s, NEG)
            l_s[...] = jnp.zeros_like(l_s)
            acc_s[...] = jnp.zeros_like(acc_s)

        def step(masked):
            s = _dot_nt(q2_ref[0], k2_ref[0])
            if masked:
                s = jnp.where(_causal(tq, tk, i * tq - kb * tk, False), s, NEG)
            m_new = jnp.maximum(m_s[...], jnp.max(s, axis=-1, keepdims=True))
            alpha = jnp.exp(m_s[...] - m_new)
            p = jnp.exp(s - m_new)
            l_s[...] = alpha * l_s[...] + jnp.sum(p, axis=-1, keepdims=True)
            acc_s[...] = alpha * acc_s[...] + _dot(p.astype(CDT), v_ref[0])
            m_s[...] = m_new

        _masked_or_not(flags, step)

        @pl.when(flags & LAST != 0)
        def _():
            o_ref[0] = acc_s[...] / l_s[...]
            lse_ref[0] = m_s[...] + jnp.log(l_s[...])

    scratch = [pltpu.VMEM((tq, 1), F32), pltpu.VMEM((tq, 1), F32), pltpu.VMEM((tq, HD), F32)]
    out_shape = (jax.ShapeDtypeStruct((NH, S, HD), F32), jax.ShapeDtypeStruct((NH, S, 1), F32))
    in_specs = [q2blk, k2blk, _head_blk(tk, True, v0)]
    if slots is None:
        grid_spec = pltpu.PrefetchScalarGridSpec(num_scalar_prefetch=3, grid=(NH, npairs), in_specs=in_specs,
                                                 out_specs=(qblk, qcol), scratch_shapes=scratch)
        o, lse = pl.pallas_call(body, name=name, grid_spec=grid_spec, out_shape=out_shape,
                                compiler_params=_params(("parallel", "arbitrary")))(qi, kj, fl, q2, k2, v)
        return o, lse, None
    grid_spec = pltpu.PrefetchScalarGridSpec(num_scalar_prefetch=3, grid=(NH, npairs), in_specs=in_specs + [ANY],
                                             out_specs=(qblk, qcol, ANY), scratch_shapes=scratch + list(GATHER_SEMS))
    return pl.pallas_call(body, name=name, grid_spec=grid_spec, out_shape=(*out_shape, jax.ShapeDtypeStruct(slots.shape, slots.dtype)),
                          input_output_aliases={6: 2},
                          compiler_params=_params(("arbitrary", "arbitrary")))(qi, kj, fl, q2, k2, v, slots)


def _fox_bwd(q, k, v, q2, k2, o, do, lse, name, h0=(0, 0, 0), swap=None):
    S = q.shape[1]
    tq, tk = _att_tiles("fox_bwd", S)
    qi, kj, fl = _pairs(S, tq, tk)
    qblk, kblk, qcol = _att_specs(tq, tk)
    npairs = int(qi.shape[0])

    def body(qi_ref, kj_ref, fl_ref, q_ref, k_ref, v_ref, q2_ref, k2_ref, o_ref, do_ref, lse_ref, *rest):
        if swap is None:
            dq_ref, dk_ref, dv_ref, dck_ref, dcq_ref, dq_s, dl_s, dcq_s = rest
        else:
            g_ref, dq_ref, dk_ref, dv_ref, dck_ref, dcq_ref, got_ref, dq_s, dl_s, dcq_s, send_sem, recv_sem = rest
        n = pl.program_id(1)
        i, kb, flags = qi_ref[n], kj_ref[n], fl_ref[n]
        if swap is not None:
            h = pl.program_id(0)
            _swap_steps(g_ref, got_ref, send_sem, recv_sem, first=(h == 0) & (n == 0), last=(h == NH - 1) & (n == npairs - 1))

        @pl.when(n == 0)
        def _():
            dk_ref[...] = jnp.zeros_like(dk_ref)
            dv_ref[...] = jnp.zeros_like(dv_ref)
            dck_ref[...] = jnp.zeros_like(dck_ref)

        @pl.when(flags & FIRST != 0)
        def _():
            dq_s[...] = jnp.zeros_like(dq_s)
            dcq_s[...] = jnp.zeros_like(dcq_s)
            dl_s[...] = jnp.sum(do_ref[0].astype(F32) * o_ref[0], axis=-1, keepdims=True)

        def step(masked):
            qs = q_ref[0] * ATT_SCALE
            do = do_ref[0]
            p = jnp.exp(_dot_nt(q2_ref[0], k2_ref[0]) - lse_ref[0])
            if masked:
                p = jnp.where(_causal(tq, tk, i * tq - kb * tk, False), p, 0.0)
            ds = p * (_dot_nt(do, v_ref[0]) - dl_s[...])
            dsb = ds.astype(CDT)
            dq_s[...] += _dot(dsb, k_ref[0])
            rows = pl.ds(pl.multiple_of(kb * tk, tk), tk)
            dk_ref[0, rows, :] += _dot_tn(dsb, qs)
            dv_ref[0, rows, :] += _dot_tn(p.astype(CDT), do)
            dck_ref[0, :, rows] += -jnp.sum(ds, axis=0, keepdims=True)
            dcq_s[...] += jnp.sum(ds, axis=-1, keepdims=True)

        _masked_or_not(flags, step)

        @pl.when(flags & LAST != 0)
        def _():
            dq_ref[0] = (dq_s[...] * ATT_SCALE).astype(dq_ref.dtype)
            dcq_ref[0] = dcq_s[...]

    whole = pl.BlockSpec((1, S, HD), lambda h, n, qi, kj, fl: (h, 0, 0))
    q2blk, k2blk, _ = _att_specs(tq, tk, FOX_DEPTH)
    in_specs = [_head_blk(tq, False, h0[0]), _head_blk(tk, True, h0[1]), _head_blk(tk, True, h0[2]), q2blk, k2blk, qblk, qblk, qcol]
    out_specs = (qblk, whole, whole, pl.BlockSpec((1, 1, S), lambda h, n, qi, kj, fl: (h, 0, 0)), qcol)
    out_shape = (jax.ShapeDtypeStruct((NH, S, HD), CDT), jax.ShapeDtypeStruct((NH, S, HD), F32), jax.ShapeDtypeStruct((NH, S, HD), F32),
                 jax.ShapeDtypeStruct((NH, 1, S), F32), jax.ShapeDtypeStruct((NH, S, 1), F32))
    scratch = [pltpu.VMEM((tq, HD), F32), pltpu.VMEM((tq, 1), F32), pltpu.VMEM((tq, 1), F32)]
    if swap is None:
        grid_spec = pltpu.PrefetchScalarGridSpec(num_scalar_prefetch=3, grid=(NH, npairs), in_specs=in_specs,
                                                 out_specs=out_specs, scratch_shapes=scratch)
        return (*pl.pallas_call(body, name=name, grid_spec=grid_spec, out_shape=out_shape,
                                compiler_params=_params(("parallel", "arbitrary")))(qi, kj, fl, q, k, v, q2, k2, o, do, lse), None)
    grid_spec = pltpu.PrefetchScalarGridSpec(num_scalar_prefetch=3, grid=(NH, npairs), in_specs=in_specs + [ANY],
                                             out_specs=(*out_specs, ANY), scratch_shapes=scratch + list(SWAP_SEMS))
    got_shape = jax.ShapeDtypeStruct((swap.shape[0], swap.shape[1] // 2, 128), swap.dtype)
    return pl.pallas_call(body, name=name, grid_spec=grid_spec, out_shape=(*out_shape, got_shape),
                          compiler_params=_params(("arbitrary", "arbitrary")))(qi, kj, fl, q, k, v, q2, k2, o, do, lse, swap)


LOG2E = 1.4426950408889634


def _proj_qkv(h1, w_qkv, name, *, tm=1024):
    S, K = h1.shape
    tm = _tile(S, tm, 16)
    SQ, SK = 3, 4

    def heads(t):
        return [t[:, h * HD:(h + 1) * HD] for h in range(NH)]

    def body(a_ref, b_ref, o_ref, q2_ref, k2_ref):
        j = pl.program_id(1)
        ob = _dot(a_ref[...], b_ref[...]).astype(CDT)
        for h, t in enumerate(heads(ob)):
            o_ref[h] = t

        @pl.when(j == SQ)
        def _():
            qf = ob.astype(F32) * (ATT_SCALE * LOG2E)
            hi = qf.astype(CDT)
            lo = (qf - hi.astype(F32)).astype(CDT)
            for h, (th, tl) in enumerate(zip(heads(hi), heads(lo))):
                q2_ref[h] = jnp.concatenate([th, tl], axis=1)

        @pl.when(j == SK)
        def _():
            for h, t in enumerate(heads(ob)):
                k2_ref[h] = jnp.concatenate([t, t], axis=1)

    wide = pl.BlockSpec((NH, tm, 2 * HD), lambda i, j: (0, i, 0))
    return pl.pallas_call(
        body, name=name, grid=(S // tm, 6),
        in_specs=[pl.BlockSpec((tm, K), lambda i, j: (i, 0)), pl.BlockSpec((K, GW), lambda i, j: (0, j))],
        out_specs=(pl.BlockSpec((NH, tm, HD), lambda i, j: (j, i, 0)), wide, wide),
        out_shape=(jax.ShapeDtypeStruct((6 * NH, S, HD), CDT), jax.ShapeDtypeStruct((NH, S, 2 * HD), CDT),
                   jax.ShapeDtypeStruct((NH, S, 2 * HD), CDT)),
        compiler_params=_params(("parallel", "arbitrary")))(h1, w_qkv)


def _sb_softplus2(q2, k2sub, mask):
    z2 = _dot_nt(q2, k2sub)
    sp2 = jnp.maximum(z2, 0.0) + jnp.log2(1.0 + jnp.exp2(-jnp.abs(z2)))
    return z2, sp2 if mask is None else jnp.where(mask, sp2, 0.0)


def _strict_tri(n, upper, value):
    r = lax.broadcasted_iota(jnp.int32, (n, n), 0)
    c = lax.broadcasted_iota(jnp.int32, (n, n), 1)
    return jnp.where(r < c if upper else r > c, value, 0.0).astype(CDT)


def _sb_fwd(q2, k2, v, name, v0=0):
    S = q2.shape[1]
    tq, tk = _att_tiles("sb_fwd", S)
    W = min(W_SB, tk)
    qi, kj, fl = _pairs(S, tq, tk)
    qblk, kblk, qcol = _att_specs(tq, tk)
    q2blk, k2blk, _ = _att_specs(tq, tk, 2 * HD)

    def body(qi_ref, kj_ref, fl_ref, q_ref, k_ref, v_ref, o_ref, lt_ref, run_s, acc_s):
        n = pl.program_id(1)
        i, kb, flags = qi_ref[n], kj_ref[n], fl_ref[n]

        @pl.when(flags & FIRST != 0)
        def _():
            run_s[...] = jnp.zeros_like(run_s)
            acc_s[...] = jnp.zeros_like(acc_s)

        def step(masked):
            neg_later = _strict_tri(W, False, -1.0)
            run = run_s[...]
            acc = acc_s[...]
            for sub in range(tk // W - 1, -1, -1):
                cols = slice(sub * W, (sub + 1) * W)
                mask = _causal(tq, W, i * tq - kb * tk - sub * W, True) if masked else None
                z2, sp2 = _sb_softplus2(q_ref[0], k_ref[0, cols, :], mask)
                excl = _dot(sp2.astype(CDT), neg_later)
                a = jnp.exp2((z2 - sp2) + (excl + run))
                if masked:
                    a = jnp.where(mask, a, 0.0)
                acc = acc + _dot(a.astype(CDT), v_ref[0, cols, :])
                run = run + (excl[:, 0:1] - sp2[:, 0:1])
            run_s[...] = run
            acc_s[...] = acc

        _masked_or_not(flags, step)

        @pl.when(flags & LAST != 0)
        def _():
            o_ref[0] = acc_s[...]
            lt_ref[0] = run_s[...]

    grid_spec = pltpu.PrefetchScalarGridSpec(
        num_scalar_prefetch=3, grid=(NH, int(qi.shape[0])), in_specs=[q2blk, k2blk, _head_blk(tk, True, v0)], out_specs=(qblk, qcol),
        scratch_shapes=[pltpu.VMEM((tq, 1), F32), pltpu.VMEM((tq, HD), F32)])
    return pl.pallas_call(body, name=name, grid_spec=grid_spec,
                          out_shape=(jax.ShapeDtypeStruct((NH, S, HD), F32), jax.ShapeDtypeStruct((NH, S, 1), F32)),
                          compiler_params=_params(("parallel", "arbitrary")))(qi, kj, fl, q2, k2, v)


def _sb_bwd(q, q2, k2, v, do, lt, name, scatter=None, q0=0, v0=0):
    S = q.shape[1]
    tq, tk = _att_tiles("sb_bwd", S)
    W = min(W_SB, tk)
    qi, kj, fl = _pairs(S, tq, tk, descending=False)
    qblk, kblk, qcol = _att_specs(tq, tk)
    q2blk, k2blk, _ = _att_specs(tq, tk, 2 * HD)
    npairs = int(qi.shape[0])

    def body(qi_ref, kj_ref, fl_ref, q_ref, q2_ref, k2_ref, v_ref, do_ref, lt_ref, *rest):
        if scatter is None:
            dq_ref, dk_ref, dv_ref, left_s, gsum_s, dq_s = rest
        else:
            h_ref, dq_ref, dk_ref, dv_ref, recv_ref, left_s, gsum_s, dq_s, send_sems, recv_sems = rest
        n = pl.program_id(1)
        i, kb, flags = qi_ref[n], kj_ref[n], fl_ref[n]
        if scatter is not None:
            h = pl.program_id(0)
            _scatter_steps(h_ref, recv_ref, send_sems, recv_sems, first=(h == 0) & (n == 0), last=(h == NH - 1) & (n == npairs - 1))

        @pl.when(n == 0)
        def _():
            dk_ref[...] = jnp.zeros_like(dk_ref)
            dv_ref[...] = jnp.zeros_like(dv_ref)

        @pl.when(flags & FIRST != 0)
        def _():
            left_s[...] = lt_ref[0]
            gsum_s[...] = jnp.zeros_like(gsum_s)
            dq_s[...] = jnp.zeros_like(dq_s)

        def step(masked):
            qs = q_ref[0] * ATT_SCALE
            do = do_ref[0]
            neg_later = _strict_tri(W, False, -1.0)
            earlier = _strict_tri(W, True, 1.0)
            left = left_s[...]
            gsum = gsum_s[...]
            dq = dq_s[...]
            for sub in range(tk // W):
                cols = slice(sub * W, (sub + 1) * W)
                mask = _causal(tq, W, i * tq - kb * tk - sub * W, True) if masked else None
                ksub = k2_ref[0, cols, 0:HD]
                z2, sp2 = _sb_softplus2(q2_ref[0], k2_ref[0, cols, :], mask)
                excl = _dot(sp2.astype(CDT), neg_later)
                left = left - (excl[:, 0:1] - sp2[:, 0:1])
                t1 = z2 - sp2
                sig = jnp.exp2(t1)
                a = jnp.exp2(t1 + (excl + left))
                if masked:
                    a = jnp.where(mask, a, 0.0)
                dl = _dot_nt(do, v_ref[0, cols, :]) * a
                before = _dot(dl.astype(CDT), earlier)
                dz = dl - sig * (dl + (before + gsum))
                if masked:
                    dz = jnp.where(mask, dz, 0.0)
                dzb = dz.astype(CDT)
                dq = dq + _dot(dzb, ksub)
                rows = pl.ds(pl.multiple_of(kb * tk + sub * W, W), W)
                dk_ref[0, rows, :] += _dot_tn(dzb, qs)
                dv_ref[0, rows, :] += _dot_tn(a.astype(CDT), do)
                gsum = gsum + (before[:, W - 1:W] + dl[:, W - 1:W])
            left_s[...] = left
            gsum_s[...] = gsum
            dq_s[...] = dq

        _masked_or_not(flags, step)

        @pl.when(flags & LAST != 0)
        def _():
            dq_ref[0] = (dq_s[...] * ATT_SCALE).astype(dq_ref.dtype)

    whole = pl.BlockSpec((1, S, HD), lambda h, n, qi, kj, fl: (h, 0, 0))
    in_specs = [_head_blk(tq, False, q0), q2blk, k2blk, _head_blk(tk, True, v0), qblk, qcol]
    out_specs = (qblk, whole, whole)
    out_shape = (jax.ShapeDtypeStruct((NH, S, HD), CDT), jax.ShapeDtypeStruct((NH, S, HD), F32), jax.ShapeDtypeStruct((NH, S, HD), F32))
    scratch = [pltpu.VMEM((tq, 1), F32), pltpu.VMEM((tq, 1), F32), pltpu.VMEM((tq, HD), F32)]
    if scatter is None:
        grid_spec = pltpu.PrefetchScalarGridSpec(num_scalar_prefetch=3, grid=(NH, npairs), in_specs=in_specs,
                                                 out_specs=out_specs, scratch_shapes=scratch)
        return (*pl.pallas_call(body, name=name, grid_spec=grid_spec, out_shape=out_shape,
                                compiler_params=_params(("parallel", "arbitrary")))(qi, kj, fl, q, q2, k2, v, do, lt), None)
    grid_spec = pltpu.PrefetchScalarGridSpec(num_scalar_prefetch=3, grid=(NH, npairs), in_specs=in_specs + [ANY],
                                             out_specs=(*out_specs, ANY), scratch_shapes=scratch + list(SCATTER_SEMS))
    recv_shape = jax.ShapeDtypeStruct((3,) + scatter.shape[1:], scatter.dtype)
    return pl.pallas_call(body, name=name, grid_spec=grid_spec, out_shape=(*out_shape, recv_shape),
                          compiler_params=_params(("arbitrary", "arbitrary")))(qi, kj, fl, q, q2, k2, v, do, lt, scatter)


def _mem_probs(q_ref, kv_ref, h):
    cols = slice(h * MHD, (h + 1) * MHD)
    s = _dot_nt(q_ref[:, cols], kv_ref[:, cols]) * MEM_SCALE
    e = jnp.exp(s - jnp.max(s, axis=-1, keepdims=True))
    return e / jnp.sum(e, axis=-1, keepdims=True)


def _xattn_fwd(q, kv, w_mo, x1, name, *, tm=512):
    S = q.shape[0]
    tm = _tile(S, tm, 16)
    nm = kv.shape[0]

    def body(q_ref, kv_ref, w_ref, x_ref, x2_ref, o_ref):
        for h in range(NMH):
            p = _mem_probs(q_ref, kv_ref, h)
            o_ref[:, h * MHD:(h + 1) * MHD] = _dot(p.astype(CDT), kv_ref[:, D + h * MHD:D + (h + 1) * MHD]).astype(CDT)
        x2_ref[...] = x_ref[...] + _dot(o_ref[...], w_ref[...])

    row = pl.BlockSpec((tm, D), lambda i: (i, 0))
    return pl.pallas_call(body, name=name, grid=(S // tm,),
                          in_specs=[row, pl.BlockSpec((nm, 2 * D), lambda i: (0, 0)), pl.BlockSpec((D, D), lambda i: (0, 0)), row],
                          out_specs=(row, row),
                          out_shape=(jax.ShapeDtypeStruct((S, D), F32), jax.ShapeDtypeStruct((S, D), CDT)),
                          compiler_params=_params(("parallel",)))(q, kv, w_mo, x1)


def _xattn_bwd(q, kv, do, name, *, tm=512):
    S = q.shape[0]
    tm = _tile(S, tm, 16)
    nm = kv.shape[0]

    def body(q_ref, kv_ref, do_ref, dq_ref, dkv_ref):
        @pl.when(pl.program_id(0) == 0)
        def _():
            dkv_ref[...] = jnp.zeros_like(dkv_ref)
        for h in range(NMH):
            cols = slice(h * MHD, (h + 1) * MHD)
            vcols = slice(D + h * MHD, D + (h + 1) * MHD)
            p = _mem_probs(q_ref, kv_ref, h)
            doh = do_ref[:, cols]
            dp = _dot_nt(doh, kv_ref[:, vcols])
            ds = (p * (dp - jnp.sum(p * dp, axis=-1, keepdims=True)) * MEM_SCALE).astype(CDT)
            dq_ref[:, cols] = _dot(ds, kv_ref[:, cols]).astype(CDT)
            dkv_ref[:, cols] += _dot_tn(ds, q_ref[:, cols])
            dkv_ref[:, vcols] += _dot_tn(p.astype(CDT), doh)

    row = pl.BlockSpec((tm, D), lambda i: (i, 0))
    kvs = pl.BlockSpec((nm, 2 * D), lambda i: (0, 0))
    return pl.pallas_call(body, name=name, grid=(S // tm,), in_specs=[row, kvs, row], out_specs=(row, kvs),
                          out_shape=(jax.ShapeDtypeStruct((S, D), CDT), jax.ShapeDtypeStruct((nm, 2 * D), F32)),
                          compiler_params=_params(("arbitrary",)))(q, kv, do)


HALO = 16
SLAB = 8


def _shift_down(u, prev, s):
    rolled = pltpu.roll(u, s, 0)
    top = rolled[0:SLAB]
    r = lax.broadcasted_iota(jnp.int32, top.shape, 0)
    for t in range(s):
        top = jnp.where(r == t, prev[HALO - s + t:HALO - s + t + 1, :], top)
    return jnp.concatenate([top, rolled[SLAB:]], axis=0)


def _shift_up(u, nxt, s):
    n = u.shape[0]
    rolled = pltpu.roll(u, n - s, 0)
    bottom = rolled[n - SLAB:]
    r = lax.broadcasted_iota(jnp.int32, bottom.shape, 0)
    for t in range(s):
        bottom = jnp.where(r == SLAB - s + t, nxt[t:t + 1, :], bottom)
    return jnp.concatenate([rolled[:n - SLAB], bottom], axis=0)


def _conv_taps(u_ref, h_ref, first):
    u = u_ref[...].astype(F32)
    prev = jnp.where(first, 0.0, h_ref[...].astype(F32))
    out = []
    for half in range(2):
        out.append((u[half], _shift_down(u[half], prev[half], 1), _shift_down(u[half], prev[half], 2)))
    return out


def _conv_specs(tm, tn, nsb):
    blk = pl.BlockSpec((2, tm, tn), lambda j, i: (0, i, j))
    prev = pl.BlockSpec((2, HALO, tn), lambda j, i: (0, jnp.maximum(i * (tm // HALO) - 1, 0), j))
    nxt = pl.BlockSpec((2, HALO, tn), lambda j, i: (0, jnp.minimum((i + 1) * (tm // HALO), nsb - 1), j))
    w = pl.BlockSpec((2, 3, tn), lambda j, i: (0, 0, j))
    b = pl.BlockSpec((2, 1, tn), lambda j, i: (0, 0, j))
    return blk, prev, nxt, w, b


def _conv_apply(taps, w_ref, b_ref):
    ys = []
    for half in range(2):
        u, u1, u2 = taps[half]
        w = w_ref[half]
        ys.append(b_ref[half] + u2 * w[0:1, :] + u1 * w[1:2, :] + u * w[2:3, :])
    return ys


def _conv_act(u0, cw, cb, name, *, tm=2048, tn=256):
    _, S, F = u0.shape
    tm = _tile(S, tm, HALO)
    tn = _tile(F, tn, 128)
    blk, prev, _, w, b = _conv_specs(tm, tn, S // HALO)

    def body(u_ref, h_ref, w_ref, b_ref, a_ref):
        yg, yv = _conv_apply(_conv_taps(u_ref, h_ref, pl.program_id(1) == 0), w_ref, b_ref)
        a_ref[...] = (yg * jax.nn.sigmoid(yg) * yv).astype(a_ref.dtype)

    return pl.pallas_call(body, name=name, grid=(F // tn, S // tm), in_specs=[blk, prev, w, b],
                          out_specs=pl.BlockSpec((tm, tn), lambda j, i: (i, j)),
                          out_shape=jax.ShapeDtypeStruct((S, F), CDT),
                          compiler_params=_params(("parallel", "parallel")))(u0, u0, cw, cb)


def _conv_act_bwd(u0, da, cw, cb, name, *, tm=2048, tn=256):
    _, S, F = u0.shape
    tm = _tile(S, tm, HALO)
    tn = _tile(F, tn, 128)
    blk, prev, _, w, b = _conv_specs(tm, tn, S // HALO)

    def body(u_ref, h_ref, da_ref, w_ref, b_ref, du_ref, dwb_ref):
        @pl.when(pl.program_id(1) == 0)
        def _():
            dwb_ref[...] = jnp.zeros_like(dwb_ref)
        taps = _conv_taps(u_ref, h_ref, pl.program_id(1) == 0)
        yg, yv = _conv_apply(taps, w_ref, b_ref)
        sg = jax.nn.sigmoid(yg)
        da = da_ref[...].astype(F32)
        dus = (da * yv * sg * (1.0 + yg * (1.0 - sg)), da * yg * sg)
        for half in range(2):
            du = dus[half]
            du_ref[half] = du.astype(du_ref.dtype)
            u, u1, u2 = taps[half]
            for row, term in enumerate((du * u2, du * u1, du * u, du)):
                dwb_ref[half, row:row + 1, :] += jnp.sum(term, axis=0, keepdims=True)

    return pl.pallas_call(body, name=name, grid=(F // tn, S // tm),
                          in_specs=[blk, prev, pl.BlockSpec((tm, tn), lambda j, i: (i, j)), w, b],
                          out_specs=(blk, pl.BlockSpec((2, 4, tn), lambda j, i: (0, 0, j))),
                          out_shape=(jax.ShapeDtypeStruct((2, S, F), CDT), jax.ShapeDtypeStruct((2, 4, F), F32)),
                          compiler_params=_params(("parallel", "arbitrary")))(u0, u0, da, cw, cb)


def _conv_bwd_input(du, cw, name, *, tm=2048, tn=256):
    _, S, F = du.shape
    tm = _tile(S, tm, HALO)
    tn = _tile(F, tn, 128)
    blk, _, nxt, w, _ = _conv_specs(tm, tn, S // HALO)
    ni = S // tm

    def body(d_ref, h_ref, w_ref, o_ref):
        d = d_ref[...].astype(F32)
        nx = jnp.where(pl.program_id(1) == ni - 1, 0.0, h_ref[...].astype(F32))
        for half in range(2):
            wv = w_ref[half]
            y = d[half] * wv[2:3, :] + _shift_up(d[half], nx[half], 1) * wv[1:2, :] + _shift_up(d[half], nx[half], 2) * wv[0:1, :]
            o_ref[half] = y.astype(o_ref.dtype)

    return pl.pallas_call(body, name=name, grid=(F // tn, ni), in_specs=[blk, nxt, w], out_specs=blk,
                          out_shape=jax.ShapeDtypeStruct((2, S, F), CDT),
                          compiler_params=_params(("parallel", "parallel")))(du, du, cw)


ANY = pl.BlockSpec(memory_space=pl.ANY)


def _place():
    return lax.axis_index("x"), lax.axis_index("y"), lax.axis_index("c")


def _other_chips(x, y):
    return ((1 - x, y), (x, 1 - y), (1 - x, 1 - y))


def _when(pred, fn):
    if pred is True:
        fn()
    else:
        pl.when(pred)(fn)


GATHER_SEMS = (pltpu.SemaphoreType.DMA((6,)), pltpu.SemaphoreType.DMA((6,)))
SCATTER_SEMS = (pltpu.SemaphoreType.DMA((3,)), pltpu.SemaphoreType.DMA((3,)))


def _gather_steps(out_ref, send_sems, recv_sems, first=True, middle=True, last=True):
    half = out_ref.shape[1] // 2
    x, y, c = _place()
    chips = _other_chips(x, y)

    def part(chip, pc):
        return out_ref.at[2 * chip[0] + chip[1], pl.ds(pl.multiple_of(pc * half, 16), half), :]

    def copy(k, chip, pc, to):
        return pltpu.make_async_remote_copy(src_ref=part(chip, pc), dst_ref=part(chip, pc),
                                            send_sem=send_sems.at[k], recv_sem=recv_sems.at[k],
                                            device_id=to, device_id_type=MESH)

    def send_mine():
        for j, chip in enumerate(chips):
            copy(j, (x, y), c, (*chip, c)).start()

    def pass_on():
        for j, chip in enumerate(chips):
            copy(j, chip, c, (x, y, c)).wait_recv()
            copy(3 + j, chip, c, (x, y, 1 - c)).start()

    def finish():
        for j, chip in enumerate(chips):
            copy(3 + j, chip, 1 - c, (x, y, c)).wait_recv()
        for j, chip in enumerate(chips):
            copy(j, (x, y), c, (*chip, c)).wait_send()
            copy(3 + j, chip, c, (x, y, 1 - c)).wait_send()

    _when(first, send_mine)
    _when(middle, pass_on)
    _when(last, finish)


def _gather_weights(buf):
    def body(buf_ref, out_ref, send_sems, recv_sems):
        del buf_ref
        _gather_steps(out_ref, send_sems, recv_sems)

    return pl.pallas_call(body, name="gather_weights", in_specs=[ANY], out_specs=ANY,
                          out_shape=jax.ShapeDtypeStruct(buf.shape, buf.dtype), input_output_aliases={0: 0},
                          scratch_shapes=list(GATHER_SEMS))(buf)


def _gather_small(v):
    m = v.shape[0]

    def body(v_ref, out_ref, send_sems, recv_sems, local_sem):
        x, y, c = _place()
        me, sibling = (x, y, c), (x, y, 1 - c)
        chips = _other_chips(x, y)

        def rows(px, py, pc):
            return out_ref.at[pl.ds((4 * px + 2 * py + pc) * m, m), :]

        def copy(k, block, to, src=None):
            return pltpu.make_async_remote_copy(src_ref=rows(*block) if src is None else src, dst_ref=rows(*block),
                                                send_sem=send_sems.at[k], recv_sem=recv_sems.at[k],
                                                device_id=to, device_id_type=MESH)

        mine = pltpu.make_async_copy(v_ref, rows(*me), local_sem)
        mine.start()
        first = [copy(0, me, sibling, src=v_ref)]
        first += [copy(1 + j, me, (*chip, c), src=v_ref) for j, chip in enumerate(chips)]
        for cp in first:
            cp.start()
        passed = [copy(4 + j, (*chip, c), sibling) for j, chip in enumerate(chips)]
        for j, chip in enumerate(chips):
            copy(1 + j, (*chip, c), me).wait_recv()
            passed[j].start()
        copy(0, sibling, me).wait_recv()
        for j, chip in enumerate(chips):
            copy(4 + j, (*chip, 1 - c), me).wait_recv()
        for cp in first + passed:
            cp.wait_send()
        mine.wait()

    vm = pl.BlockSpec(memory_space=pltpu.VMEM)
    return pl.pallas_call(body, name="gather_small", in_specs=[vm], out_specs=vm,
                          out_shape=jax.ShapeDtypeStruct((8 * m, 128), v.dtype),
                          scratch_shapes=[pltpu.SemaphoreType.DMA((7,)), pltpu.SemaphoreType.DMA((7,)), pltpu.SemaphoreType.DMA])(v)


SWAP_SEMS = (pltpu.SemaphoreType.DMA, pltpu.SemaphoreType.DMA)


def _swap_steps(g_ref, out_ref, send_sem, recv_sem, first=True, last=True):
    half = out_ref.shape[1]
    x, y, c = _place()

    def copy():
        src = g_ref.at[:, pl.ds(pl.multiple_of((1 - c) * half, 8), half), :]
        return pltpu.make_async_remote_copy(src_ref=src, dst_ref=out_ref, send_sem=send_sem, recv_sem=recv_sem,
                                            device_id=(x, y, 1 - c), device_id_type=MESH)

    _when(first, lambda: copy().start())
    _when(last, lambda: copy().wait())


def _swap_halves(g, name):
    n, rows, _ = g.shape

    def body(g_ref, out_ref, send_sem, recv_sem):
        _swap_steps(g_ref, out_ref, send_sem, recv_sem)

    return pl.pallas_call(body, name=name, in_specs=[ANY], out_specs=ANY,
                          out_shape=jax.ShapeDtypeStruct((n, rows // 2, 128), g.dtype),
                          scratch_shapes=list(SWAP_SEMS))(g)


def _scatter_steps(h_ref, out_ref, send_sems, recv_sems, first=True, last=True):
    x, y, c = _place()

    def copies():
        return [pltpu.make_async_remote_copy(src_ref=h_ref.at[2 * chip[0] + chip[1]], dst_ref=out_ref.at[j],
                                             send_sem=send_sems.at[j], recv_sem=recv_sems.at[j],
                                             device_id=(*chip, c), device_id_type=MESH)
                for j, chip in enumerate(_other_chips(x, y))]

    def start():
        for cp in copies():
            cp.start()

    def finish():
        for cp in copies():
            cp.wait()

    _when(first, start)
    _when(last, finish)


def _join_halves(buf, name):
    half = buf.shape[0] // 2

    def body(buf_ref, out_ref, send_sem, recv_sem):
        del buf_ref
        x, y, c = _place()
        mine = out_ref.at[pl.ds(pl.multiple_of(c * half, 8), half), :]
        other = out_ref.at[pl.ds(pl.multiple_of((1 - c) * half, 8), half), :]
        cp = pltpu.make_async_remote_copy(src_ref=mine, dst_ref=mine, send_sem=send_sem, recv_sem=recv_sem,
                                          device_id=(x, y, 1 - c), device_id_type=MESH)
        cp.start()
        cp.wait_send()
        pltpu.make_async_remote_copy(src_ref=other, dst_ref=other, send_sem=send_sem, recv_sem=recv_sem,
                                     device_id=(x, y, 1 - c), device_id_type=MESH).wait_recv()

    return pl.pallas_call(body, name=name, in_specs=[ANY], out_specs=ANY,
                          out_shape=jax.ShapeDtypeStruct(buf.shape, buf.dtype), input_output_aliases={0: 0},
                          scratch_shapes=[pltpu.SemaphoreType.DMA, pltpu.SemaphoreType.DMA])(buf)


def _add_sibling(g, recv, c_idx, name):
    n, rows, _ = g.shape
    half = rows // 2
    tr = _tile(half, ADAM_ROWS, 16)
    nb = half // tr

    def body(c_ref, g_ref, r_ref, o_ref, ob_ref):
        s = g_ref[...] + r_ref[...]
        o_ref[...] = s
        ob_ref[...] = s.astype(CDT)

    out = pl.BlockSpec((None, tr, 128), lambda k, i, c: (k, i, 0))
    grid_spec = pltpu.PrefetchScalarGridSpec(
        num_scalar_prefetch=1, grid=(n, nb),
        in_specs=[pl.BlockSpec((None, tr, 128), lambda k, i, c: (k, c[0] * nb + i, 0)), out],
        out_specs=(out, out))
    return pl.pallas_call(body, name=name, grid_spec=grid_spec,
                          out_shape=(jax.ShapeDtypeStruct((n, half, 128), F32), jax.ShapeDtypeStruct((n, half, 128), CDT)),
                          compiler_params=_params(("parallel", "parallel")))(c_idx, g, recv)


def _add_chips(hsum, recv, chip_idx, name):
    n, half, _ = hsum.shape
    tr = _tile(half, ADAM_ROWS, 16)

    def body(k_ref, h_ref, r_ref, o_ref):
        o_ref[...] = ((h_ref[...] + r_ref[0].astype(F32)) + r_ref[1].astype(F32)) + r_ref[2].astype(F32)

    grid_spec = pltpu.PrefetchScalarGridSpec(
        num_scalar_prefetch=1, grid=(half // tr,),
        in_specs=[pl.BlockSpec((None, tr, 128), lambda i, k: (k[0], i, 0)),
                  pl.BlockSpec((3, tr, 128), lambda i, k: (0, i, 0))],
        out_specs=pl.BlockSpec((tr, 128), lambda i, k: (i, 0)))
    return pl.pallas_call(body, name=name, grid_spec=grid_spec, out_shape=jax.ShapeDtypeStruct((half, 128), F32),
                          compiler_params=_params(("parallel",)))(chip_idx, hsum, recv)


def _adamw_math(g, w, m, v):
    m2 = B1 * m + (1.0 - B1) * g
    v2 = B2 * v + (1.0 - B2) * (g * g)
    delta = -LR * ((m2 / BC1) / (jnp.sqrt(v2 / BC2) + AEPS) + WD * w)
    return delta, m2, v2


def _adamw(g, w, m, v, name):
    rows, cols = g.shape
    tr = _tile(rows, max(8, (ADAM_ROWS * 128 // cols) // 8 * 8), 8)

    def body(g_ref, w_ref, m_ref, v_ref, d_ref, m2_ref, v2_ref):
        d_ref[...], m2_ref[...], v2_ref[...] = _adamw_math(g_ref[...], w_ref[...], m_ref[...], v_ref[...])

    blk = pl.BlockSpec((None, tr, cols), lambda i: (0, i, 0))
    shp = jax.ShapeDtypeStruct((1, rows, cols), F32)
    return pl.pallas_call(body, name=name, grid=(rows // tr,), in_specs=[pl.BlockSpec((tr, cols), lambda i: (i, 0))] + [blk] * 3,
                          out_specs=(blk,) * 3, out_shape=(shp,) * 3, compiler_params=_params(("parallel",)))(g, w, m, v)


def _adamw_small(parts, w, m, v, name):
    rows = w.shape[0]

    def body(p_ref, w_ref, m_ref, v_ref, g_ref, d_ref, m2_ref, v2_ref):
        g = p_ref[0]
        for k in range(1, 8):
            g = g + p_ref[k]
        g_ref[...] = g
        d_ref[...], m2_ref[...], v2_ref[...] = _adamw_math(g, w_ref[...], m_ref[...], v_ref[...])

    shp = jax.ShapeDtypeStruct((rows, 128), F32)
    return pl.pallas_call(body, name=name, out_shape=(shp,) * 4)(parts, w, m, v)


def _pack_rows(parts, rows):
    flat = jnp.concatenate([p.reshape(-1) for p in parts])
    return jnp.pad(flat, (0, rows * 128 - flat.shape[0])).reshape(rows, 128)


def _unpack(flat, sizes, shapes):
    out, off = [], 0
    for n, s in zip(sizes, shapes):
        out.append(flat[off:off + n].reshape(s))
        off += n
    return out


def _to_shards(full, shard_shape, axis):
    if axis == 0:
        return full.reshape(N_CHIP, -1)
    r, cs = shard_shape
    return full.reshape(r, N_CHIP, cs).transpose(1, 0, 2).reshape(N_CHIP, -1)


def _from_shards(sh, shard_shape, axis):
    r, cs = shard_shape
    if axis == 0:
        return sh.reshape(N_CHIP * r, cs)
    return sh.reshape(N_CHIP, r, cs).transpose(1, 0, 2).reshape(r, N_CHIP * cs)


def _local_step(x0, mem, tgt, W, gains, ex=None):
    S = x0.shape[0]
    w_in = jnp.pad(W["w_in"], ((0, 0), (0, IN_PAD - IN_COLS)))
    b_f = jnp.pad(gains["b_forget"], ((0, 0), (0, 128 - NH)))

    h1 = _rms_cast(x0, gains["attn_norm_g"], "norm_attn")
    qkv, sq2, sk2 = _proj_qkv(h1, w_in[:, :NQKV], "proj_qkv")
    fox, sb_q, sb_v = (0, NH, 2 * NH), 3 * NH, 5 * NH
    fl = _mm_nn(h1, w_in[:, NQKV:NQKV + 128], F32, "proj_gate")
    cum = _gate_fwd(fl, b_f, "gate_cumsum")
    fq2, fk2 = _fox_operands(qkv, cum, "fox_operands")
    fo_h, lse, gathered = _fox_fwd(fq2, fk2, qkv, "fox_fwd", slots=None if ex is None else ex.slots("b"), v0=fox[2])
    if ex is not None:
        W = {**W, **ex.unpack("b", gathered)}
    cw = W["conv_w"].reshape(3, 2, DFF).transpose(1, 0, 2)
    cb = gains["conv_b"].reshape(2, 1, DFF)
    so_h, s_lt = _sb_fwd(sq2, sk2, qkv, "sb_fwd", v0=sb_v)
    x1, mixed = _out_proj(fo_h, so_h, gains["fox_out_g"], gains["sb_out_g"], W["w_out"], x0, "out_proj")

    h2 = _rms_cast(x1, gains["xattn_norm_g"], "norm_xattn")
    mn = _rms_cast(mem, gains["mem_norm_g"], "norm_mem")
    mq = _mm_nn(h2, W["w_mq"], CDT, "proj_mq")
    kv = _mm_nn(mn, W["w_mkv"], CDT, "proj_mkv")
    x2, mo = _xattn_fwd(mq, kv, W["w_mo"], x1, "xattn_fwd")

    h3 = _rms_cast(x2, gains["ffn_norm_g"], "norm_ffn")
    u0 = _mm_nn(h3, W["w_up"], CDT, "ffn_up", tm=512, tn=DFF, halves=True)
    act = _conv_act(u0, cw, cb, "conv_act")
    x3 = _mm_nn(act, W["w_down"], F32, "ffn_down", tm=512, residual=x2)
    loss, dx3, dg_final = _loss_bwd(x3, tgt, gains["final_norm_g"].reshape(1, D), "loss")

    gw, gs = {}, {"final_norm_g": dg_final}
    da = _mm_nt(dx3, W["w_down"], "ffn_down_dx", tn=DFF, out_dtype=CDT)
    gw["w_down"] = _mm_tn(act, dx3, "ffn_down_dw", tka=DFF)
    du, dwb = _conv_act_bwd(u0, da, cw, cb, "conv_act_bwd")
    gw["conv_w"] = dwb[:, :3].transpose(1, 0, 2).reshape(3, 2 * DFF)
    gs["conv_b"] = dwb[:, 3].reshape(1, 2 * DFF)
    du0 = _conv_bwd_input(du, cw, "conv_bwd_input")
    gw["w_up"] = _mm_tn(h3, du0, "ffn_up_dw", tn=DFF, b_halves=True)
    dx2, gs["ffn_norm_g"] = _mm_nt_rmsbwd(du0, W["w_up"], x2, gains["ffn_norm_g"], dx3, "ffn_up_dx", tk=DFF, a_halves=True)

    dmo = _mm_nt(dx2, W["w_mo"], "mo_dx", tn=512, out_dtype=CDT)
    gw["w_mo"] = _mm_tn(mo, dx2, "mo_dw")
    dmq, dkv = _xattn_bwd(mq, kv, dmo, "xattn_bwd")
    gw["w_mq"] = _mm_tn(h2, dmq, "mq_dw")
    dx1, gs["xattn_norm_g"] = _mm_nt_rmsbwd(dmq, W["w_mq"], x1, gains["xattn_norm_g"], dx2, "mq_dx")
    gw["w_mkv"] = _mm_tn(mn, dkv, "mkv_dw")
    _, gs["mem_norm_g"] = _mm_nt_rmsbwd(dkv, W["w_mkv"], mem, gains["mem_norm_g"], jnp.zeros_like(mem), "mkv_dx")

    gw["w_out"] = _mm_tn(mixed, dx1, "out_dw")
    dfo_h, dso_h, gs["fox_out_g"], gs["sb_out_g"] = _out_proj_bwd(dx1, W["w_out"], fo_h, so_h, gains["fox_out_g"], gains["sb_out_g"], "out_dx")
    flat = None if ex is None else ex.flat("b", gw)
    dfq, dfk, dfv, dck, dcq, got = _fox_bwd(qkv, qkv, qkv, fq2, fk2, fo_h, dfo_h, lse, "fox_bwd", h0=fox, swap=flat)
    pair, pair16 = (None, None) if ex is None else ex.pair_sums("b", flat, got)
    dsq, dsk, dsv, arrived = _sb_bwd(qkv, sq2, sk2, qkv, dso_h, s_lt, "sb_bwd", scatter=pair16, q0=sb_q, v0=sb_v)
    dfl, db = _gate_bwd(jnp.pad(dck[:, 0, :].T, ((0, 0), (0, 128 - NH))), dcq, fl, b_f, "gate_bwd")
    gs["b_forget"] = db[:, :NH]
    dqkv = jnp.concatenate([dfq, dfk.astype(CDT), dfv.astype(CDT), dsq, dsk.astype(CDT), dsv.astype(CDT)], axis=0)
    dproj = jnp.concatenate([dqkv.transpose(1, 0, 2).reshape(S, NQKV), dfl.astype(CDT),
                             jnp.zeros((S, IN_PAD - NQKV - 128), CDT)], axis=1)
    gw["w_in"] = _mm_tn(h1, dproj, "in_dw", tn=IN_PAD)[:, :IN_COLS]
    if ex is None:
        dx0, gs["attn_norm_g"] = _mm_nt_rmsbwd(dproj, w_in, x0, gains["attn_norm_g"], dx1, "in_dx", tk=IN_PAD)
        return loss, dx0, gw, gs, None
    pair_a, pair16_a = ex.pair_sums("a", ex.flat("a", gw))
    dx0, gs["attn_norm_g"], arrived_a = _mm_nt_rmsbwd(dproj, w_in, x0, gains["attn_norm_g"], dx1, "in_dx", tk=IN_PAD, scatter=pair16_a)
    return loss, dx0, gw, gs, {"a": (pair_a, arrived_a), "b": (pair, arrived)}


NAMES = ("attn_norm_g", "w_in", "b_forget", "fox_out_g", "sb_out_g", "w_out", "xattn_norm_g", "mem_norm_g", "w_mq",
         "w_mkv", "w_mo", "ffn_norm_g", "w_up", "conv_w", "conv_b", "w_down", "final_norm_g")


class _Exchange:
    def __init__(self, w):
        self.w = w
        xi, yi, ci = _place()
        self.core = ci
        self.chip = 2 * xi + yi
        self.core_idx = jnp.reshape(ci, (1,)).astype(jnp.int32)
        self.chip_idx = jnp.reshape(self.chip, (1,)).astype(jnp.int32)

    def slots(self, g):
        parts = []
        for name, shape, _ in GROUPS[g]:
            blk = self.w[name].reshape(shape)
            parts.append(lax.bitcast_convert_type(blk, CDT) if name == "conv_w" else blk.astype(CDT))
        rows = _rows_g(GROUPS[g])
        return lax.dynamic_update_slice(lax.empty((N_CHIP, rows, 128), CDT), _pack_rows(parts, rows)[None], (self.chip, 0, 0))

    def unpack(self, g, gathered):
        flat, full, off = gathered.reshape(N_CHIP, -1), {}, 0
        for (name, shape, axis), n in zip(GROUPS[g], _gather_sizes(GROUPS[g])):
            sh = flat[:, off:off + n]
            off += n
            if name == "conv_w":
                sh = lax.bitcast_convert_type(sh.reshape(N_CHIP, n // 2, 2), F32)
            full[name] = _from_shards(sh, shape, axis)
        return full

    def flat(self, g, gw):
        rows = _rows_f(GROUPS[g])
        flat = jnp.concatenate([_to_shards(gw[name], shape, axis) for name, shape, axis in GROUPS[g]], axis=1)
        return jnp.pad(flat, ((0, 0), (0, rows * 128 - flat.shape[1]))).reshape(N_CHIP, rows, 128)

    def pair_sums(self, g, flat, got=None):
        if got is None:
            got = _swap_halves(flat, "swap_halves_" + g)
        return _add_sibling(flat, got, self.core_idx, "add_sibling_" + g)

    def finish(self, g, pair, arrived):
        rows = _rows_f(GROUPS[g])
        mine = _add_chips(pair, arrived, self.chip_idx, "add_chips_" + g)
        whole = _join_halves(lax.dynamic_update_slice(lax.empty((rows, 128), F32), mine, (self.core * (rows // 2), 0)), "join_halves_" + g)
        shapes = [s for _, s, _ in GROUPS[g]]
        return {name: arr for (name, _, _), arr in zip(GROUPS[g], _unpack(whole.reshape(-1), _sizes(GROUPS[g]), shapes))}


def _step(x, mem, loss_target, w, m, v):
    ex = _Exchange(w)

    W = ex.unpack("a", _gather_weights(ex.slots("a")))
    gains = {name: w[name].reshape(1, -1) for name, _ in SMALL}

    loss, grad_x, gw, gs, reduced = _local_step(x[0], mem[0], loss_target[0], W, gains, ex)

    grads = {**ex.finish("b", *reduced["b"]), **ex.finish("a", *reduced["a"])}
    small = jnp.concatenate([gs[name].reshape(-1) for name, _ in SMALL] + [loss[0, :1]])
    small = jnp.pad(small, (0, ROWS_S * 128 - P_SMALL)).reshape(ROWS_S, 128)
    small_parts = _gather_small(small).reshape(8, ROWS_S, 128)

    def flat_small(d):
        return _pack_rows([d[name] for name, _ in SMALL], ROWS_S)

    outs = {}
    for name, shape, _ in BIG:
        g = grads[name]
        res = _adamw(g, w[name], m[name], v[name], "adamw_" + name)
        for prefix, arr in zip(("grad_", "delta_", "new_m_", "new_v_"), (g, *res)):
            outs[prefix + name] = arr.reshape(w[name].shape)
    small_res = _adamw_small(small_parts, flat_small(w), flat_small(m), flat_small(v), "adamw_small")
    g_sm = small_res[0]
    for prefix, sm in zip(("grad_", "delta_", "new_m_", "new_v_"), small_res):
        for (name, n), arr in zip(SMALL, _unpack(sm.reshape(-1), [n for _, n in SMALL], [(n,) for _, n in SMALL])):
            outs[prefix + name] = arr.reshape(w[name].shape)
    total_loss = g_sm.reshape(-1)[P_SMALL - 1]
    return (total_loss, grad_x[None], *[outs[p + n] for p in ("grad_", "delta_", "new_m_", "new_v_") for n in NAMES])


def kernel(x, mem, attn_norm_g, w_in, b_forget, fox_out_g, sb_out_g, w_out, xattn_norm_g, mem_norm_g, w_mq, w_mkv, w_mo, ffn_norm_g, w_up, conv_w, conv_b, w_down, final_norm_g, loss_target, m_attn_norm_g, m_w_in, m_b_forget, m_fox_out_g, m_sb_out_g, m_w_out, m_xattn_norm_g, m_mem_norm_g, m_w_mq, m_w_mkv, m_w_mo, m_ffn_norm_g, m_w_up, m_conv_w, m_conv_b, m_w_down, m_final_norm_g, v_attn_norm_g, v_w_in, v_b_forget, v_fox_out_g, v_sb_out_g, v_w_out, v_xattn_norm_g, v_mem_norm_g, v_w_mq, v_w_mkv, v_w_mo, v_ffn_norm_g, v_w_up, v_conv_w, v_conv_b, v_w_down, v_final_norm_g):
    given = dict(locals())
    w = {n: given[n] for n in NAMES}
    m = {n: given["m_" + n] for n in NAMES}
    v = {n: given["v_" + n] for n in NAMES}
    return _step(x, mem, loss_target, w, m, v)
```

```python
import functools

import numpy as np
import jax
import jax.numpy as jnp
from jax import lax
from jax.experimental import pallas as pl
from jax.experimental.pallas import tpu as pltpu

F32 = jnp.float32
CDT = jnp.bfloat16
MESH = pl.DeviceIdType.MESH

D = 1024
HD = 64
NH = 8
GW = NH * HD
NQKV = 6 * GW
IN_COLS = NQKV + NH
IN_PAD = NQKV + 256
NMH = 4
MHD = D // NMH
DFF = 2816
EPS = 1e-6
ATT_SCALE = HD ** -0.5
MEM_SCALE = MHD ** -0.5
NEG = -1e30

LR, B1, B2, AEPS, WD, STEP = 0.001, 0.9, 0.999, 1e-08, 0.01, 10
BC1 = 1.0 - B1 ** STEP
BC2 = 1.0 - B2 ** STEP

ATT_TILES = {"fox_fwd": (1024, 2048), "fox_bwd": (1024, 1024), "sb_fwd": (1024, 1024), "sb_bwd": (1024, 1024)}
W_SB = 256
VMEM_LIMIT = 52 * 2 ** 20

N_CHIP = 4
BIG = (("w_in", (D, IN_COLS // N_CHIP), 1), ("w_out", (D // N_CHIP, D), 0), ("w_mq", (D // N_CHIP, D), 0),
       ("w_mkv", (D, 2 * D // N_CHIP), 1), ("w_mo", (D // N_CHIP, D), 0), ("w_up", (D, 2 * DFF // N_CHIP), 1),
       ("conv_w", (3, 2 * DFF // N_CHIP), 1), ("w_down", (DFF // N_CHIP, D), 0))
GROUPS = {"a": BIG[:1], "b": BIG[1:]}
ADAM_ROWS = 1536


def _sizes(group):
    return tuple(int(np.prod(s)) for _, s, _ in group)


def _gather_sizes(group):
    return tuple(2 * n if name == "conv_w" else n for (name, _, _), n in zip(group, _sizes(group)))


def _rows_g(group):
    return -(-sum(_gather_sizes(group)) // 4096) * 32


def _rows_f(group):
    return -(-sum(_sizes(group)) // 65536) * 512
SMALL = (("attn_norm_g", 1024), ("b_forget", 8), ("fox_out_g", 512), ("sb_out_g", 512), ("xattn_norm_g", 1024),
         ("mem_norm_g", 1024), ("ffn_norm_g", 1024), ("conv_b", 2 * DFF), ("final_norm_g", 1024))
P_SMALL = sum(n for _, n in SMALL) + 1
ROWS_S = -(-P_SMALL // 1024) * 8


def _params(sem=None, vmem=VMEM_LIMIT):
    return pltpu.CompilerParams(dimension_semantics=sem, vmem_limit_bytes=vmem)


def _tile(n, pref, mult):
    t = (min(pref, n) // mult) * mult
    while t >= mult:
        if n % t == 0:
            return t
        t -= mult
    return n


def _dot(a, b):
    return jnp.dot(a, b, preferred_element_type=F32)


def _dot_nt(a, b):
    return lax.dot_general(a, b, (((1,), (1,)), ((), ())), preferred_element_type=F32)


def _dot_tn(a, b):
    return lax.dot_general(a, b, (((0,), (0,)), ((), ())), preferred_element_type=F32)


def _split3(x):
    h1 = x.astype(CDT)
    r1 = x - h1.astype(F32)
    h2 = r1.astype(CDT)
    h3 = (r1 - h2.astype(F32)).astype(CDT)
    return h1, h2, h3


def _rms_bwd(dh, x, g):
    r = lax.rsqrt(jnp.mean(x * x, axis=-1, keepdims=True) + EPS)
    xn = x * r
    dg = jnp.sum(dh * xn, axis=0, keepdims=True)
    dhg = dh * g
    dx = r * (dhg - xn * jnp.mean(dhg * xn, axis=-1, keepdims=True))
    return dx, dg


def _mm_nn(a, b, out_dtype, name, *, tm=1024, tn=512, residual=None, halves=False):
    M, K = a.shape
    N = b.shape[1]
    tm = _tile(M, tm, 16)
    tn = _tile(N // 2 if halves else N, tn, 128)
    nj = N // tn

    def body(*refs):
        a_ref, b_ref = refs[0], refs[1]
        o_ref = refs[-1]
        acc = _dot(a_ref[...].astype(CDT), b_ref[...].astype(CDT))
        if residual is not None:
            acc = acc + refs[2][...]
        o_ref[...] = acc.astype(o_ref.dtype)

    in_specs = [pl.BlockSpec((tm, K), lambda i, j: (i, 0)), pl.BlockSpec((K, tn), lambda i, j: (0, j))]
    ops = [a, b]
    if residual is not None:
        in_specs.append(pl.BlockSpec((tm, tn), lambda i, j: (i, j)))
        ops.append(residual)
    if halves:
        njh = nj // 2
        out_shape = jax.ShapeDtypeStruct((2, M, N // 2), out_dtype)
        out_spec = pl.BlockSpec((None, tm, tn), lambda i, j: (j // njh, i, j % njh))
    else:
        out_shape = jax.ShapeDtypeStruct((M, N), out_dtype)
        out_spec = pl.BlockSpec((tm, tn), lambda i, j: (i, j))
    return pl.pallas_call(body, name=name, grid=(M // tm, nj), in_specs=in_specs, out_specs=out_spec,
                          out_shape=out_shape, compiler_params=_params(("parallel", "parallel")))(*ops)


def _mm_tn(a, b, name, *, tka=512, tn=1024, ts=512, b_halves=False):
    S, Ka = a.shape
    N = 2 * b.shape[2] if b_halves else b.shape[1]
    tka = _tile(Ka, tka, 128)
    tn = _tile(N // 2 if b_halves else N, tn, 128)
    ts = _tile(S, ts, 16)
    nn = N // tn

    def body(a_ref, b_ref, o_ref):
        @pl.when(pl.program_id(2) == 0)
        def _():
            o_ref[...] = jnp.zeros_like(o_ref)
        o_ref[...] += _dot_tn(a_ref[...].astype(CDT), b_ref[...].astype(CDT))

    if b_halves:
        nnh = nn // 2
        b_spec = pl.BlockSpec((None, ts, tn), lambda i, j, s: (j // nnh, s, j % nnh))
    else:
        b_spec = pl.BlockSpec((ts, tn), lambda i, j, s: (s, j))
    return pl.pallas_call(
        body, name=name, grid=(Ka // tka, nn, S // ts),
        in_specs=[pl.BlockSpec((ts, tka), lambda i, j, s: (s, i)), b_spec],
        out_specs=pl.BlockSpec((tka, tn), lambda i, j, s: (i, j)),
        out_shape=jax.ShapeDtypeStruct((Ka, N), F32),
        compiler_params=_params(("parallel", "parallel", "arbitrary")))(a, b)


def _mm_nt(a, b, name, *, tm=512, tn=None, tk=None, a_halves=False, out_dtype=F32,
           epilogue=None, extra=(), extra_specs=(), out_shape=None, out_specs=None, scatter=None):
    if a_halves:
        M, K = a.shape[1], 2 * a.shape[2]
    else:
        M, K = a.shape
    N = b.shape[0]
    tm = _tile(M, tm, 16)
    tn = N if (epilogue is not None or tn is None) else _tile(N, tn, 128)
    tk = K if tk is None else _tile(K // 2 if a_halves else K, tk, 128)
    nk = K // tk
    n_extra = len(extra)
    grid = (M // tm, N // tn, nk)

    def body(*refs):
        if scatter is not None:
            *refs, send_sems, recv_sems = refs
            h_ref, recv_ref = refs[2 + n_extra], refs[-2]
            refs = (*refs[:2 + n_extra], *refs[3 + n_extra:-2], refs[-1])
            at = [pl.program_id(d) for d in range(3)]
            _scatter_steps(h_ref, recv_ref, send_sems, recv_sems,
                           first=(at[0] == 0) & (at[1] == 0) & (at[2] == 0),
                           last=(at[0] == grid[0] - 1) & (at[1] == grid[1] - 1) & (at[2] == grid[2] - 1))
        a_ref, b_ref = refs[0], refs[1]
        extra_refs = refs[2:2 + n_extra]
        out_refs = refs[2 + n_extra:-1]
        acc_ref = refs[-1]
        k = pl.program_id(2)

        @pl.when(k == 0)
        def _():
            acc_ref[...] = jnp.zeros_like(acc_ref)
        acc_ref[...] += _dot_nt(a_ref[...].astype(CDT), b_ref[...].astype(CDT))

        @pl.when(k == nk - 1)
        def _():
            if epilogue is None:
                out_refs[0][...] = acc_ref[...].astype(out_refs[0].dtype)
            else:
                epilogue(acc_ref[...], pl.program_id(0), extra_refs, out_refs)

    if a_halves:
        nkh = nk // 2
        a_spec = pl.BlockSpec((None, tm, tk), lambda i, j, k: (k // nkh, i, k % nkh))
    else:
        a_spec = pl.BlockSpec((tm, tk), lambda i, j, k: (i, k))
    if epilogue is None:
        out_shape = jax.ShapeDtypeStruct((M, N), out_dtype)
        out_specs = pl.BlockSpec((tm, tn), lambda i, j, k: (i, j))
        sem = ("parallel", "parallel", "arbitrary")
    else:
        sem = ("arbitrary", "arbitrary", "arbitrary")
    in_specs = [a_spec, pl.BlockSpec((tn, tk), lambda i, j, k: (j, k)), *extra_specs]
    scratch = [pltpu.VMEM((tm, tn), F32)]
    ops = [a, b, *extra]
    if scatter is not None:
        in_specs, ops, scratch = in_specs + [ANY], ops + [scatter], scratch + list(SCATTER_SEMS)
        out_specs = (*out_specs, ANY)
        out_shape = (*out_shape, jax.ShapeDtypeStruct((3,) + scatter.shape[1:], scatter.dtype))
    return pl.pallas_call(body, name=name, grid=grid, in_specs=in_specs, out_specs=out_specs, out_shape=out_shape,
                          scratch_shapes=scratch, compiler_params=_params(sem))(*ops)


def _mm_nt_rmsbwd(a, b, x, g, dres, name, *, tm=512, tk=None, a_halves=False, scatter=None):
    M = x.shape[0]
    tm = _tile(M, tm, 16)

    def epilogue(acc, i, extra_refs, out_refs):
        x_ref, g_ref, r_ref = extra_refs
        dx_ref, dg_ref = out_refs
        dx, dg = _rms_bwd(acc, x_ref[...], g_ref[...])
        dx_ref[...] = r_ref[...] + dx

        @pl.when(i == 0)
        def _():
            dg_ref[...] = jnp.zeros_like(dg_ref)
        dg_ref[...] += dg

    row = pl.BlockSpec((tm, D), lambda i, j, k: (i, 0))
    vec = pl.BlockSpec((1, D), lambda i, j, k: (0, 0))
    return _mm_nt(a, b, name, tm=tm, tk=tk, a_halves=a_halves, epilogue=epilogue,
                  extra=(x, g, dres), extra_specs=(row, vec, row),
                  out_shape=(jax.ShapeDtypeStruct((M, D), F32), jax.ShapeDtypeStruct((1, D), F32)),
                  out_specs=(row, vec), scatter=scatter)


def _rms_cast(x, g, name, *, tm=512, slots=None):
    M, W = x.shape
    tm = _tile(M, tm, 16)
    nb = M // tm

    def body(x_ref, g_ref, *rest):
        o_ref = rest[0] if slots is None else rest[1]
        if slots is not None:
            i = pl.program_id(0)
            _gather_steps(rest[2], rest[3], rest[4], first=i == 0, middle=i == nb // 2, last=i == nb - 1)
        xf = x_ref[...]
        r = lax.rsqrt(jnp.mean(xf * xf, axis=-1, keepdims=True) + EPS)
        o_ref[...] = (xf * r * g_ref[...]).astype(o_ref.dtype)

    blk = pl.BlockSpec((tm, W), lambda i: (i, 0))
    in_specs = [blk, pl.BlockSpec((1, W), lambda i: (0, 0))]
    shp = jax.ShapeDtypeStruct((M, W), CDT)
    if slots is None:
        return pl.pallas_call(body, name=name, grid=(nb,), in_specs=in_specs, out_specs=blk, out_shape=shp,
                              compiler_params=_params(("parallel",)))(x, g)
    return pl.pallas_call(body, name=name, grid=(nb,), in_specs=in_specs + [ANY], out_specs=(blk, ANY),
                          out_shape=(shp, jax.ShapeDtypeStruct(slots.shape, slots.dtype)), input_output_aliases={2: 1},
                          scratch_shapes=list(GATHER_SEMS), compiler_params=_params(("arbitrary",)))(x, g, slots)


def _tri(n, lower):
    r = lax.broadcasted_iota(jnp.int32, (n, n), 0)
    c = lax.broadcasted_iota(jnp.int32, (n, n), 1)
    return (c <= r if lower else c >= r).astype(CDT)


def _gate_fwd(fl, b, name, *, tm=512):
    S = fl.shape[0]
    tm = _tile(S, tm, 16)

    def body(f_ref, b_ref, c_ref, carry):
        @pl.when(pl.program_id(0) == 0)
        def _():
            carry[...] = jnp.zeros_like(carry)
        z = f_ref[...] + b_ref[...]
        lf = jnp.minimum(z, 0.0) - jnp.log(1.0 + jnp.exp(-jnp.abs(z)))
        tri = _tri(tm, True)
        cum = sum(_dot(tri, p) for p in _split3(lf)) + carry[...]
        c_ref[...] = cum
        carry[...] = cum[tm - 1:tm, :]

    return pl.pallas_call(body, name=name, grid=(S // tm,),
                          in_specs=[pl.BlockSpec((tm, 128), lambda i: (i, 0)), pl.BlockSpec((1, 128), lambda i: (0, 0))],
                          out_specs=pl.BlockSpec((tm, 128), lambda i: (i, 0)),
                          out_shape=jax.ShapeDtypeStruct((S, 128), F32),
                          scratch_shapes=[pltpu.VMEM((1, 128), F32)],
                          compiler_params=_params(("arbitrary",)))(fl, b)


def _gate_bwd(dck, dcq, fl, b, name, *, tm=512):
    S = fl.shape[0]
    tm = _tile(S, tm, 16)
    nb = S // tm

    def body(dck_ref, dcq_ref, f_ref, b_ref, df_ref, db_ref, carry):
        @pl.when(pl.program_id(0) == 0)
        def _():
            carry[...] = jnp.zeros_like(carry)
            db_ref[...] = jnp.zeros_like(db_ref)
        lane = lax.broadcasted_iota(jnp.int32, (1, 128), 1)
        dc = dck_ref[...]
        for h in range(NH):
            dc = dc + dcq_ref[h] * (lane == h).astype(F32)
        tri = _tri(tm, False)
        suf = sum(_dot(tri, p) for p in _split3(dc)) + carry[...]
        carry[...] = suf[0:1, :]
        df = suf * jax.nn.sigmoid(-(f_ref[...] + b_ref[...]))
        df_ref[...] = df
        db_ref[...] += jnp.sum(df, axis=0, keepdims=True)

    rev = pl.BlockSpec((tm, 128), lambda i: (nb - 1 - i, 0))
    cols = pl.BlockSpec((NH, tm, 1), lambda i: (0, nb - 1 - i, 0))
    vec = pl.BlockSpec((1, 128), lambda i: (0, 0))
    return pl.pallas_call(body, name=name, grid=(nb,), in_specs=[rev, cols, rev, vec], out_specs=(rev, vec),
                          out_shape=(jax.ShapeDtypeStruct((S, 128), F32), jax.ShapeDtypeStruct((1, 128), F32)),
                          scratch_shapes=[pltpu.VMEM((1, 128), F32)],
                          compiler_params=_params(("arbitrary",)))(dck, dcq, fl, b)


def _group_rows(o_ref):
    return jnp.concatenate([o_ref[h] for h in range(NH)], axis=1)


def _out_proj(fo, so, gf, gs, w_out, x0, name, *, tm=512):
    S = fo.shape[1]
    tm = _tile(S, tm, 16)

    def body(fo_ref, so_ref, gf_ref, gs_ref, w_ref, x_ref, x1_ref, mx_ref):
        for ref, g_ref, lo in ((fo_ref, gf_ref, 0), (so_ref, gs_ref, GW)):
            o = _group_rows(ref)
            r = lax.rsqrt(jnp.mean(o * o, axis=-1, keepdims=True) + EPS)
            mx_ref[:, lo:lo + GW] = (o * r * g_ref[...]).astype(CDT)
        x1_ref[...] = x_ref[...] + _dot(mx_ref[...], w_ref[...])

    half = pl.BlockSpec((NH, tm, HD), lambda i: (0, i, 0))
    gvec = pl.BlockSpec((1, GW), lambda i: (0, 0))
    row = pl.BlockSpec((tm, D), lambda i: (i, 0))
    return pl.pallas_call(body, name=name, grid=(S // tm,),
                          in_specs=[half, half, gvec, gvec, pl.BlockSpec((D, D), lambda i: (0, 0)), row],
                          out_specs=(row, row),
                          out_shape=(jax.ShapeDtypeStruct((S, D), F32), jax.ShapeDtypeStruct((S, D), CDT)),
                          compiler_params=_params(("parallel",)))(fo, so, gf, gs, w_out, x0)


def _out_proj_bwd(dx1, w_out, fo, so, gf, gs, name, *, tm=512):
    S = fo.shape[1]
    tm = _tile(S, tm, 16)

    def epilogue(acc, i, extra_refs, out_refs):
        fo_ref, so_ref, gf_ref, gs_ref = extra_refs
        dfo_ref, dso_ref, dgf_ref, dgs_ref = out_refs

        @pl.when(i == 0)
        def _():
            dgf_ref[...] = jnp.zeros_like(dgf_ref)
            dgs_ref[...] = jnp.zeros_like(dgs_ref)
        for lo, o_ref, g_ref, do_ref, dg_ref in ((0, fo_ref, gf_ref, dfo_ref, dgf_ref), (GW, so_ref, gs_ref, dso_ref, dgs_ref)):
            dx, dg = _rms_bwd(acc[:, lo:lo + GW], _group_rows(o_ref), g_ref[...])
            for h in range(NH):
                do_ref[h] = dx[:, h * HD:(h + 1) * HD].astype(do_ref.dtype)
            dg_ref[...] += dg

    half = pl.BlockSpec((NH, tm, HD), lambda i, j, k: (0, i, 0))
    gvec = pl.BlockSpec((1, GW), lambda i, j, k: (0, 0))
    return _mm_nt(dx1, w_out, name, tm=tm, epilogue=epilogue, extra=(fo, so, gf, gs),
                  extra_specs=(half, half, gvec, gvec),
                  out_shape=(jax.ShapeDtypeStruct((NH, S, HD), CDT), jax.ShapeDtypeStruct((NH, S, HD), CDT),
                             jax.ShapeDtypeStruct((1, GW), F32), jax.ShapeDtypeStruct((1, GW), F32)),
                  out_specs=(half, half, gvec, gvec))


def _loss_bwd(x3, tgt, g, name, *, tm=512):
    S = x3.shape[0]
    tm = _tile(S, tm, 16)

    def body(x_ref, t_ref, g_ref, dx_ref, loss_ref, dg_ref):
        @pl.when(pl.program_id(0) == 0)
        def _():
            loss_ref[...] = jnp.zeros_like(loss_ref)
            dg_ref[...] = jnp.zeros_like(dg_ref)
        x = x_ref[...]
        gv = g_ref[...]
        r = lax.rsqrt(jnp.mean(x * x, axis=-1, keepdims=True) + EPS)
        xn = x * r
        err = xn * gv - t_ref[...]
        loss_ref[...] += jnp.full(loss_ref.shape, 0.5 * jnp.sum(jnp.mean(err * err, axis=-1, keepdims=True)), F32)
        dy = err * (1.0 / D)
        dg_ref[...] += jnp.sum(dy * xn, axis=0, keepdims=True)
        dyg = dy * gv
        dx_ref[...] = r * (dyg - xn * jnp.mean(dyg * xn, axis=-1, keepdims=True))

    row = pl.BlockSpec((tm, D), lambda i: (i, 0))
    vec = pl.BlockSpec((1, D), lambda i: (0, 0))
    dx3, loss, dg = pl.pallas_call(
        body, name=name, grid=(S // tm,), in_specs=[row, row, vec],
        out_specs=(row, pl.BlockSpec((1, 128), lambda i: (0, 0)), vec),
        out_shape=(jax.ShapeDtypeStruct((S, D), F32), jax.ShapeDtypeStruct((1, 128), F32), jax.ShapeDtypeStruct((1, D), F32)),
        compiler_params=_params(("arbitrary",)))(x3, tgt, g)
    return loss, dx3, dg


MASKED, FIRST, LAST = 1, 2, 4


def _att_tiles(name, S):
    tq, tk = ATT_TILES[name]
    return min(tq, S), min(tk, S)


def _pairs(S, tq, tk, descending=True):
    assert tk % tq == 0 and S % tk == 0
    qi, kj, fl = [], [], []
    for i in range(S // tq):
        last = ((i + 1) * tq - 1) // tk
        order = list(range(last, -1, -1) if descending else range(last + 1))
        for pos, kb in enumerate(order):
            qi.append(i)
            kj.append(kb)
            fl.append((MASKED if (kb + 1) * tk - 1 > i * tq else 0) | (FIRST if pos == 0 else 0) | (LAST if pos == last else 0))
    return tuple(jnp.asarray(np.asarray(a, np.int32)) for a in (qi, kj, fl))


def _head_blk(rows, by_key, head0, width=HD):
    if by_key:
        return pl.BlockSpec((1, rows, width), lambda h, n, qi, kj, fl: (h + head0, kj[n], 0))
    return pl.BlockSpec((1, rows, width), lambda h, n, qi, kj, fl: (h + head0, qi[n], 0))


def _att_specs(tq, tk, width=HD):
    qblk = pl.BlockSpec((1, tq, width), lambda h, n, qi, kj, fl: (h, qi[n], 0))
    kblk = pl.BlockSpec((1, tk, width), lambda h, n, qi, kj, fl: (h, kj[n], 0))
    qcol = pl.BlockSpec((1, tq, 1), lambda h, n, qi, kj, fl: (h, qi[n], 0))
    return qblk, kblk, qcol


def _causal(tq, w, ahead, strict):
    diff = lax.broadcasted_iota(jnp.int32, (tq, w), 1) - lax.broadcasted_iota(jnp.int32, (tq, w), 0)
    return diff < ahead if strict else diff <= ahead


def _masked_or_not(flags, step):
    pl.when(flags % 2 == 1)(functools.partial(step, True))
    pl.when(flags % 2 == 0)(functools.partial(step, False))


FOX_DEPTH = 2 * HD


def _fox_operands(qkv, cum, name, *, tm=512):
    S = qkv.shape[1]
    tm = _tile(S, tm, 16)

    def body(q_ref, k_ref, c_ref, q2_ref, k2_ref):
        lane = lax.broadcasted_iota(jnp.int32, (1, HD), 1)
        c = c_ref[...]
        for h in range(NH):
            pieces = [p.astype(F32) for p in _split3(c[:, h:h + 1])]
            qa = sum(jnp.where(lane == 2 * n, pieces[n], 0.0) for n in range(3)) + jnp.where((lane < 6) & (lane % 2 == 1), 1.0, 0.0)
            ka = sum(jnp.where(lane == 2 * n + 1, -pieces[n], 0.0) for n in range(3)) + jnp.where((lane < 6) & (lane % 2 == 0), 1.0, 0.0)
            q2_ref[h] = jnp.concatenate([qa.astype(CDT), q_ref[h] * ATT_SCALE], axis=1)
            k2_ref[h] = jnp.concatenate([ka.astype(CDT), k_ref[h]], axis=1)

    wide = pl.BlockSpec((NH, tm, FOX_DEPTH), lambda i: (0, i, 0))
    shp = jax.ShapeDtypeStruct((NH, S, FOX_DEPTH), CDT)
    return pl.pallas_call(body, name=name, grid=(S // tm,),
                          in_specs=[pl.BlockSpec((NH, tm, HD), lambda i: (0, i, 0)), pl.BlockSpec((NH, tm, HD), lambda i: (1, i, 0)),
                                    pl.BlockSpec((tm, 128), lambda i: (i, 0))],
                          out_specs=(wide, wide), out_shape=(shp, shp), compiler_params=_params(("parallel",)))(qkv, qkv, cum)


def _fox_fwd(q2, k2, v, name, slots=None, v0=0):
    S = q2.shape[1]
    tq, tk = _att_tiles("fox_fwd", S)
    qi, kj, fl = _pairs(S, tq, tk)
    qblk, kblk, qcol = _att_specs(tq, tk)
    q2blk, k2blk, _ = _att_specs(tq, tk, FOX_DEPTH)
    npairs = int(qi.shape[0])

    def body(qi_ref, kj_ref, fl_ref, q2_ref, k2_ref, v_ref, *rest):
        if slots is None:
            o_ref, lse_ref, m_s, l_s, acc_s = rest
        else:
            _, o_ref, lse_ref, slots_ref, m_s, l_s, acc_s, send_sems, recv_sems = rest
        h, n = pl.program_id(0), pl.program_id(1)
        i, kb, flags = qi_ref[n], kj_ref[n], fl_ref[n]
        if slots is not None:
            _gather_steps(slots_ref, send_sems, recv_sems, first=(h == 0) & (n == 0), middle=(h == NH // 2) & (n == 0),
                          last=(h == NH - 1) & (n == npairs - 1))

        @pl.when(flags & FIRST != 0)
        def _():
            m_s[...] = jnp.full_like(m_s, NEG)
            l_s[...] = jnp.zeros_like(l_s)
            acc_s[...] = jnp.zeros_like(acc_s)

        def step(masked):
            s = _dot_nt(q2_ref[0], k2_ref[0])
            if masked:
                s = jnp.where(_causal(tq, tk, i * tq - kb * tk, False), s, NEG)
            m_new = jnp.maximum(m_s[...], jnp.max(s, axis=-1, keepdims=True))
            alpha = jnp.exp(m_s[...] - m_new)
            p = jnp.exp(s - m_new)
            l_s[...] = alpha * l_s[...] + jnp.sum(p, axis=-1, keepdims=True)
            acc_s[...] = alpha * acc_s[...] + _dot(p.astype(CDT), v_ref[0])
            m_s[...] = m_new

        _masked_or_not(flags, step)

        @pl.when(flags & LAST != 0)
        def _():
            o_ref[0] = acc_s[...] / l_s[...]
            lse_ref[0] = m_s[...] + jnp.log(l_s[...])

    scratch = [pltpu.VMEM((tq, 1), F32), pltpu.VMEM((tq, 1), F32), pltpu.VMEM((tq, HD), F32)]
    out_shape = (jax.ShapeDtypeStruct((NH, S, HD), F32), jax.ShapeDtypeStruct((NH, S, 1), F32))
    in_specs = [q2blk, k2blk, _head_blk(tk, True, v0)]
    if slots is None:
        grid_spec = pltpu.PrefetchScalarGridSpec(num_scalar_prefetch=3, grid=(NH, npairs), in_specs=in_specs,
                                                 out_specs=(qblk, qcol), scratch_shapes=scratch)
        o, lse = pl.pallas_call(body, name=name, grid_spec=grid_spec, out_shape=out_shape,
                                compiler_params=_params(("parallel", "arbitrary")))(qi, kj, fl, q2, k2, v)
        return o, lse, None
    grid_spec = pltpu.PrefetchScalarGridSpec(num_scalar_prefetch=3, grid=(NH, npairs), in_specs=in_specs + [ANY],
                                             out_specs=(qblk, qcol, ANY), scratch_shapes=scratch + list(GATHER_SEMS))
    return pl.pallas_call(body, name=name, grid_spec=grid_spec, out_shape=(*out_shape, jax.ShapeDtypeStruct(slots.shape, slots.dtype)),
                          input_output_aliases={6: 2},
                          compiler_params=_params(("arbitrary", "arbitrary")))(qi, kj, fl, q2, k2, v, slots)


def _fox_bwd(q, k, v, q2, k2, o, do, lse, name, h0=(0, 0, 0), swap=None):
    S = q.shape[1]
    tq, tk = _att_tiles("fox_bwd", S)
    qi, kj, fl = _pairs(S, tq, tk)
    qblk, kblk, qcol = _att_specs(tq, tk)
    npairs = int(qi.shape[0])

    def body(qi_ref, kj_ref, fl_ref, q_ref, k_ref, v_ref, q2_ref, k2_ref, o_ref, do_ref, lse_ref, *rest):
        if swap is None:
            dq_ref, dk_ref, dv_ref, dck_ref, dcq_ref, dq_s, dl_s, dcq_s = rest
        else:
            g_ref, dq_ref, dk_ref, dv_ref, dck_ref, dcq_ref, got_ref, dq_s, dl_s, dcq_s, send_sem, recv_sem = rest
        n = pl.program_id(1)
        i, kb, flags = qi_ref[n], kj_ref[n], fl_ref[n]
        if swap is not None:
            h = pl.program_id(0)
            _swap_steps(g_ref, got_ref, send_sem, recv_sem, first=(h == 0) & (n == 0), last=(h == NH - 1) & (n == npairs - 1))

        @pl.when(n == 0)
        def _():
            dk_ref[...] = jnp.zeros_like(dk_ref)
            dv_ref[...] = jnp.zeros_like(dv_ref)
            dck_ref[...] = jnp.zeros_like(dck_ref)

        @pl.when(flags & FIRST != 0)
        def _():
            dq_s[...] = jnp.zeros_like(dq_s)
            dcq_s[...] = jnp.zeros_like(dcq_s)
            dl_s[...] = jnp.sum(do_ref[0].astype(F32) * o_ref[0], axis=-1, keepdims=True)

        def step(masked):
            qs = q_ref[0] * ATT_SCALE
            do = do_ref[0]
            p = jnp.exp(_dot_nt(q2_ref[0], k2_ref[0]) - lse_ref[0])
            if masked:
                p = jnp.where(_causal(tq, tk, i * tq - kb * tk, False), p, 0.0)
            ds = p * (_dot_nt(do, v_ref[0]) - dl_s[...])
            dsb = ds.astype(CDT)
            dq_s[...] += _dot(dsb, k_ref[0])
            rows = pl.ds(pl.multiple_of(kb * tk, tk), tk)
            dk_ref[0, rows, :] += _dot_tn(dsb, qs)
            dv_ref[0, rows, :] += _dot_tn(p.astype(CDT), do)
            dck_ref[0, :, rows] += -jnp.sum(ds, axis=0, keepdims=True)
            dcq_s[...] += jnp.sum(ds, axis=-1, keepdims=True)

        _masked_or_not(flags, step)

        @pl.when(flags & LAST != 0)
        def _():
            dq_ref[0] = (dq_s[...] * ATT_SCALE).astype(dq_ref.dtype)
            dcq_ref[0] = dcq_s[...]

    whole = pl.BlockSpec((1, S, HD), lambda h, n, qi, kj, fl: (h, 0, 0))
    q2blk, k2blk, _ = _att_specs(tq, tk, FOX_DEPTH)
    in_specs = [_head_blk(tq, False, h0[0]), _head_blk(tk, True, h0[1]), _head_blk(tk, True, h0[2]), q2blk, k2blk, qblk, qblk, qcol]
    out_specs = (qblk, whole, whole, pl.BlockSpec((1, 1, S), lambda h, n, qi, kj, fl: (h, 0, 0)), qcol)
    out_shape = (jax.ShapeDtypeStruct((NH, S, HD), CDT), jax.ShapeDtypeStruct((NH, S, HD), F32), jax.ShapeDtypeStruct((NH, S, HD), F32),
                 jax.ShapeDtypeStruct((NH, 1, S), F32), jax.ShapeDtypeStruct((NH, S, 1), F32))
    scratch = [pltpu.VMEM((tq, HD), F32), pltpu.VMEM((tq, 1), F32), pltpu.VMEM((tq, 1), F32)]
    if swap is None:
        grid_spec = pltpu.PrefetchScalarGridSpec(num_scalar_prefetch=3, grid=(NH, npairs), in_specs=in_specs,
                                                 out_specs=out_specs, scratch_shapes=scratch)
        return (*pl.pallas_call(body, name=name, grid_spec=grid_spec, out_shape=out_shape,
                                compiler_params=_params(("parallel", "arbitrary")))(qi, kj, fl, q, k, v, q2, k2, o, do, lse), None)
    grid_spec = pltpu.PrefetchScalarGridSpec(num_scalar_prefetch=3, grid=(NH, npairs), in_specs=in_specs + [ANY],
                                             out_specs=(*out_specs, ANY), scratch_shapes=scratch + list(SWAP_SEMS))
    got_shape = jax.ShapeDtypeStruct((swap.shape[0], swap.shape[1] // 2, 128), swap.dtype)
    return pl.pallas_call(body, name=name, grid_spec=grid_spec, out_shape=(*out_shape, got_shape),
                          compiler_params=_params(("arbitrary", "arbitrary")))(qi, kj, fl, q, k, v, q2, k2, o, do, lse, swap)


LOG2E = 1.4426950408889634


def _proj_qkv(h1, w_qkv, name, *, tm=1024):
    S, K = h1.shape
    tm = _tile(S, tm, 16)
    SQ, SK = 3, 4

    def heads(t):
        return [t[:, h * HD:(h + 1) * HD] for h in range(NH)]

    def body(a_ref, b_ref, o_ref, q2_ref, k2_ref):
        j = pl.program_id(1)
        ob = _dot(a_ref[...], b_ref[...]).astype(CDT)
        for h, t in enumerate(heads(ob)):
            o_ref[h] = t

        @pl.when(j == SQ)
        def _():
            qf = ob.astype(F32) * (ATT_SCALE * LOG2E)
            hi = qf.astype(CDT)
            lo = (qf - hi.astype(F32)).astype(CDT)
            for h, (th, tl) in enumerate(zip(heads(hi), heads(lo))):
                q2_ref[h] = jnp.concatenate([th, tl], axis=1)

        @pl.when(j == SK)
        def _():
            for h, t in enumerate(heads(ob)):
                k2_ref[h] = jnp.concatenate([t, t], axis=1)

    wide = pl.BlockSpec((NH, tm, 2 * HD), lambda i, j: (0, i, 0))
    return pl.pallas_call(
        body, name=name, grid=(S // tm, 6),
        in_specs=[pl.BlockSpec((tm, K), lambda i, j: (i, 0)), pl.BlockSpec((K, GW), lambda i, j: (0, j))],
        out_specs=(pl.BlockSpec((NH, tm, HD), lambda i, j: (j, i, 0)), wide, wide),
        out_shape=(jax.ShapeDtypeStruct((6 * NH, S, HD), CDT), jax.ShapeDtypeStruct((NH, S, 2 * HD), CDT),
                   jax.ShapeDtypeStruct((NH, S, 2 * HD), CDT)),
        compiler_params=_params(("parallel", "arbitrary")))(h1, w_qkv)


def _sb_softplus2(q2, k2sub, mask):
    z2 = _dot_nt(q2, k2sub)
    sp2 = jnp.maximum(z2, 0.0) + jnp.log2(1.0 + jnp.exp2(-jnp.abs(z2)))
    return z2, sp2 if mask is None else jnp.where(mask, sp2, 0.0)


def _strict_tri(n, upper, value):
    r = lax.broadcasted_iota(jnp.int32, (n, n), 0)
    c = lax.broadcasted_iota(jnp.int32, (n, n), 1)
    return jnp.where(r < c if upper else r > c, value, 0.0).astype(CDT)


def _sb_fwd(q2, k2, v, name, v0=0):
    S = q2.shape[1]
    tq, tk = _att_tiles("sb_fwd", S)
    W = min(W_SB, tk)
    qi, kj, fl = _pairs(S, tq, tk)
    qblk, kblk, qcol = _att_specs(tq, tk)
    q2blk, k2blk, _ = _att_specs(tq, tk, 2 * HD)

    def body(qi_ref, kj_ref, fl_ref, q_ref, k_ref, v_ref, o_ref, lt_ref, run_s, acc_s):
        n = pl.program_id(1)
        i, kb, flags = qi_ref[n], kj_ref[n], fl_ref[n]

        @pl.when(flags & FIRST != 0)
        def _():
            run_s[...] = jnp.zeros_like(run_s)
            acc_s[...] = jnp.zeros_like(acc_s)

        def step(masked):
            neg_later = _strict_tri(W, False, -1.0)
            run = run_s[...]
            acc = acc_s[...]
            for sub in range(tk // W - 1, -1, -1):
                cols = slice(sub * W, (sub + 1) * W)
                mask = _causal(tq, W, i * tq - kb * tk - sub * W, True) if masked else None
                z2, sp2 = _sb_softplus2(q_ref[0], k_ref[0, cols, :], mask)
                excl = _dot(sp2.astype(CDT), neg_later)
                a = jnp.exp2((z2 - sp2) + (excl + run))
                if masked:
                    a = jnp.where(mask, a, 0.0)
                acc = acc + _dot(a.astype(CDT), v_ref[0, cols, :])
                run = run + (excl[:, 0:1] - sp2[:, 0:1])
            run_s[...] = run
            acc_s[...] = acc

        _masked_or_not(flags, step)

        @pl.when(flags & LAST != 0)
        def _():
            o_ref[0] = acc_s[...]
            lt_ref[0] = run_s[...]

    grid_spec = pltpu.PrefetchScalarGridSpec(
        num_scalar_prefetch=3, grid=(NH, int(qi.shape[0])), in_specs=[q2blk, k2blk, _head_blk(tk, True, v0)], out_specs=(qblk, qcol),
        scratch_shapes=[pltpu.VMEM((tq, 1), F32), pltpu.VMEM((tq, HD), F32)])
    return pl.pallas_call(body, name=name, grid_spec=grid_spec,
                          out_shape=(jax.ShapeDtypeStruct((NH, S, HD), F32), jax.ShapeDtypeStruct((NH, S, 1), F32)),
                          compiler_params=_params(("parallel", "arbitrary")))(qi, kj, fl, q2, k2, v)


def _sb_bwd(q, q2, k2, v, do, lt, name, scatter=None, q0=0, v0=0):
    S = q.shape[1]
    tq, tk = _att_tiles("sb_bwd", S)
    W = min(W_SB, tk)
    qi, kj, fl = _pairs(S, tq, tk, descending=False)
    qblk, kblk, qcol = _att_specs(tq, tk)
    q2blk, k2blk, _ = _att_specs(tq, tk, 2 * HD)
    npairs = int(qi.shape[0])

    def body(qi_ref, kj_ref, fl_ref, q_ref, q2_ref, k2_ref, v_ref, do_ref, lt_ref, *rest):
        if scatter is None:
            dq_ref, dk_ref, dv_ref, left_s, gsum_s, dq_s = rest
        else:
            h_ref, dq_ref, dk_ref, dv_ref, recv_ref, left_s, gsum_s, dq_s, send_sems, recv_sems = rest
        n = pl.program_id(1)
        i, kb, flags = qi_ref[n], kj_ref[n], fl_ref[n]
        if scatter is not None:
            h = pl.program_id(0)
            _scatter_steps(h_ref, recv_ref, send_sems, recv_sems, first=(h == 0) & (n == 0), last=(h == NH - 1) & (n == npairs - 1))

        @pl.when(n == 0)
        def _():
            dk_ref[...] = jnp.zeros_like(dk_ref)
            dv_ref[...] = jnp.zeros_like(dv_ref)

        @pl.when(flags & FIRST != 0)
        def _():
            left_s[...] = lt_ref[0]
            gsum_s[...] = jnp.zeros_like(gsum_s)
            dq_s[...] = jnp.zeros_like(dq_s)

        def step(masked):
            qs = q_ref[0] * ATT_SCALE
            do = do_ref[0]
            neg_later = _strict_tri(W, False, -1.0)
            earlier = _strict_tri(W, True, 1.0)
            left = left_s[...]
            gsum = gsum_s[...]
            dq = dq_s[...]
            for sub in range(tk // W):
                cols = slice(sub * W, (sub + 1) * W)
                mask = _causal(tq, W, i * tq - kb * tk - sub * W, True) if masked else None
                ksub = k2_ref[0, cols, 0:HD]
                z2, sp2 = _sb_softplus2(q2_ref[0], k2_ref[0, cols, :], mask)
                excl = _dot(sp2.astype(CDT), neg_later)
                left = left - (excl[:, 0:1] - sp2[:, 0:1])
                t1 = z2 - sp2
                sig = jnp.exp2(t1)
                a = jnp.exp2(t1 + (excl + left))
                if masked:
                    a = jnp.where(mask, a, 0.0)
                dl = _dot_nt(do, v_ref[0, cols, :]) * a
                before = _dot(dl.astype(CDT), earlier)
                dz = dl - sig * (dl + (before + gsum))
                if masked:
                    dz = jnp.where(mask, dz, 0.0)
                dzb = dz.astype(CDT)
                dq = dq + _dot(dzb, ksub)
                rows = pl.ds(pl.multiple_of(kb * tk + sub * W, W), W)
                dk_ref[0, rows, :] += _dot_tn(dzb, qs)
                dv_ref[0, rows, :] += _dot_tn(a.astype(CDT), do)
                gsum = gsum + (before[:, W - 1:W] + dl[:, W - 1:W])
            left_s[...] = left
            gsum_s[...] = gsum
            dq_s[...] = dq

        _masked_or_not(flags, step)

        @pl.when(flags & LAST != 0)
        def _():
            dq_ref[0] = (dq_s[...] * ATT_SCALE).astype(dq_ref.dtype)

    whole = pl.BlockSpec((1, S, HD), lambda h, n, qi, kj, fl: (h, 0, 0))
    in_specs = [_head_blk(tq, False, q0), q2blk, k2blk, _head_blk(tk, True, v0), qblk, qcol]
    out_specs = (qblk, whole, whole)
    out_shape = (jax.ShapeDtypeStruct((NH, S, HD), CDT), jax.ShapeDtypeStruct((NH, S, HD), F32), jax.ShapeDtypeStruct((NH, S, HD), F32))
    scratch = [pltpu.VMEM((tq, 1), F32), pltpu.VMEM((tq, 1), F32), pltpu.VMEM((tq, HD), F32)]
    if scatter is None:
        grid_spec = pltpu.PrefetchScalarGridSpec(num_scalar_prefetch=3, grid=(NH, npairs), in_specs=in_specs,
                                                 out_specs=out_specs, scratch_shapes=scratch)
        return (*pl.pallas_call(body, name=name, grid_spec=grid_spec, out_shape=out_shape,
                                compiler_params=_params(("parallel", "arbitrary")))(qi, kj, fl, q, q2, k2, v, do, lt), None)
    grid_spec = pltpu.PrefetchScalarGridSpec(num_scalar_prefetch=3, grid=(NH, npairs), in_specs=in_specs + [ANY],
                                             out_specs=(*out_specs, ANY), scratch_shapes=scratch + list(SCATTER_SEMS))
    recv_shape = jax.ShapeDtypeStruct((3,) + scatter.shape[1:], scatter.dtype)
    return pl.pallas_call(body, name=name, grid_spec=grid_spec, out_shape=(*out_shape, recv_shape),
                          compiler_params=_params(("arbitrary", "arbitrary")))(qi, kj, fl, q, q2, k2, v, do, lt, scatter)


def _mem_probs(q_ref, kv_ref, h):
    cols = slice(h * MHD, (h + 1) * MHD)
    s = _dot_nt(q_ref[:, cols], kv_ref[:, cols]) * MEM_SCALE
    e = jnp.exp(s - jnp.max(s, axis=-1, keepdims=True))
    return e / jnp.sum(e, axis=-1, keepdims=True)


def _xattn_fwd(q, kv, w_mo, x1, name, *, tm=512):
    S = q.shape[0]
    tm = _tile(S, tm, 16)
    nm = kv.shape[0]

    def body(q_ref, kv_ref, w_ref, x_ref, x2_ref, o_ref):
        for h in range(NMH):
            p = _mem_probs(q_ref, kv_ref, h)
            o_ref[:, h * MHD:(h + 1) * MHD] = _dot(p.astype(CDT), kv_ref[:, D + h * MHD:D + (h + 1) * MHD]).astype(CDT)
        x2_ref[...] = x_ref[...] + _dot(o_ref[...], w_ref[...])

    row = pl.BlockSpec((tm, D), lambda i: (i, 0))
    return pl.pallas_call(body, name=name, grid=(S // tm,),
                          in_specs=[row, pl.BlockSpec((nm, 2 * D), lambda i: (0, 0)), pl.BlockSpec((D, D), lambda i: (0, 0)), row],
                          out_specs=(row, row),
                          out_shape=(jax.ShapeDtypeStruct((S, D), F32), jax.ShapeDtypeStruct((S, D), CDT)),
                          compiler_params=_params(("parallel",)))(q, kv, w_mo, x1)


def _xattn_bwd(q, kv, do, name, *, tm=512):
    S = q.shape[0]
    tm = _tile(S, tm, 16)
    nm = kv.shape[0]

    def body(q_ref, kv_ref, do_ref, dq_ref, dkv_ref):
        @pl.when(pl.program_id(0) == 0)
        def _():
            dkv_ref[...] = jnp.zeros_like(dkv_ref)
        for h in range(NMH):
            cols = slice(h * MHD, (h + 1) * MHD)
            vcols = slice(D + h * MHD, D + (h + 1) * MHD)
            p = _mem_probs(q_ref, kv_ref, h)
            doh = do_ref[:, cols]
            dp = _dot_nt(doh, kv_ref[:, vcols])
            ds = (p * (dp - jnp.sum(p * dp, axis=-1, keepdims=True)) * MEM_SCALE).astype(CDT)
            dq_ref[:, cols] = _dot(ds, kv_ref[:, cols]).astype(CDT)
            dkv_ref[:, cols] += _dot_tn(ds, q_ref[:, cols])
            dkv_ref[:, vcols] += _dot_tn(p.astype(CDT), doh)

    row = pl.BlockSpec((tm, D), lambda i: (i, 0))
    kvs = pl.BlockSpec((nm, 2 * D), lambda i: (0, 0))
    return pl.pallas_call(body, name=name, grid=(S // tm,), in_specs=[row, kvs, row], out_specs=(row, kvs),
                          out_shape=(jax.ShapeDtypeStruct((S, D), CDT), jax.ShapeDtypeStruct((nm, 2 * D), F32)),
                          compiler_params=_params(("arbitrary",)))(q, kv, do)


HALO = 16
SLAB = 8


def _shift_down(u, prev, s):
    rolled = pltpu.roll(u, s, 0)
    top = rolled[0:SLAB]
    r = lax.broadcasted_iota(jnp.int32, top.shape, 0)
    for t in range(s):
        top = jnp.where(r == t, prev[HALO - s + t:HALO - s + t + 1, :], top)
    return jnp.concatenate([top, rolled[SLAB:]], axis=0)


def _shift_up(u, nxt, s):
    n = u.shape[0]
    rolled = pltpu.roll(u, n - s, 0)
    bottom = rolled[n - SLAB:]
    r = lax.broadcasted_iota(jnp.int32, bottom.shape, 0)
    for t in range(s):
        bottom = jnp.where(r == SLAB - s + t, nxt[t:t + 1, :], bottom)
    return jnp.concatenate([rolled[:n - SLAB], bottom], axis=0)


def _conv_taps(u_ref, h_ref, first):
    u = u_ref[...].astype(F32)
    prev = jnp.where(first, 0.0, h_ref[...].astype(F32))
    out = []
    for half in range(2):
        out.append((u[half], _shift_down(u[half], prev[half], 1), _shift_down(u[half], prev[half], 2)))
    return out


def _conv_specs(tm, tn, nsb):
    blk = pl.BlockSpec((2, tm, tn), lambda j, i: (0, i, j))
    prev = pl.BlockSpec((2, HALO, tn), lambda j, i: (0, jnp.maximum(i * (tm // HALO) - 1, 0), j))
    nxt = pl.BlockSpec((2, HALO, tn), lambda j, i: (0, jnp.minimum((i + 1) * (tm // HALO), nsb - 1), j))
    w = pl.BlockSpec((2, 3, tn), lambda j, i: (0, 0, j))
    b = pl.BlockSpec((2, 1, tn), lambda j, i: (0, 0, j))
    return blk, prev, nxt, w, b


def _conv_apply(taps, w_ref, b_ref):
    ys = []
    for half in range(2):
        u, u1, u2 = taps[half]
        w = w_ref[half]
        ys.append(b_ref[half] + u2 * w[0:1, :] + u1 * w[1:2, :] + u * w[2:3, :])
    return ys


def _conv_act(u0, cw, cb, name, *, tm=2048, tn=256):
    _, S, F = u0.shape
    tm = _tile(S, tm, HALO)
    tn = _tile(F, tn, 128)
    blk, prev, _, w, b = _conv_specs(tm, tn, S // HALO)

    def body(u_ref, h_ref, w_ref, b_ref, a_ref):
        yg, yv = _conv_apply(_conv_taps(u_ref, h_ref, pl.program_id(1) == 0), w_ref, b_ref)
        a_ref[...] = (yg * jax.nn.sigmoid(yg) * yv).astype(a_ref.dtype)

    return pl.pallas_call(body, name=name, grid=(F // tn, S // tm), in_specs=[blk, prev, w, b],
                          out_specs=pl.BlockSpec((tm, tn), lambda j, i: (i, j)),
                          out_shape=jax.ShapeDtypeStruct((S, F), CDT),
                          compiler_params=_params(("parallel", "parallel")))(u0, u0, cw, cb)


def _conv_act_bwd(u0, da, cw, cb, name, *, tm=2048, tn=256):
    _, S, F = u0.shape
    tm = _tile(S, tm, HALO)
    tn = _tile(F, tn, 128)
    blk, prev, _, w, b = _conv_specs(tm, tn, S // HALO)

    def body(u_ref, h_ref, da_ref, w_ref, b_ref, du_ref, dwb_ref):
        @pl.when(pl.program_id(1) == 0)
        def _():
            dwb_ref[...] = jnp.zeros_like(dwb_ref)
        taps = _conv_taps(u_ref, h_ref, pl.program_id(1) == 0)
        yg, yv = _conv_apply(taps, w_ref, b_ref)
        sg = jax.nn.sigmoid(yg)
        da = da_ref[...].astype(F32)
        dus = (da * yv * sg * (1.0 + yg * (1.0 - sg)), da * yg * sg)
        for half in range(2):
            du = dus[half]
            du_ref[half] = du.astype(du_ref.dtype)
            u, u1, u2 = taps[half]
            for row, term in enumerate((du * u2, du * u1, du * u, du)):
                dwb_ref[half, row:row + 1, :] += jnp.sum(term, axis=0, keepdims=True)

    return pl.pallas_call(body, name=name, grid=(F // tn, S // tm),
                          in_specs=[blk, prev, pl.BlockSpec((tm, tn), lambda j, i: (i, j)), w, b],
                          out_specs=(blk, pl.BlockSpec((2, 4, tn), lambda j, i: (0, 0, j))),
                          out_shape=(jax.ShapeDtypeStruct((2, S, F), CDT), jax.ShapeDtypeStruct((2, 4, F), F32)),
                          compiler_params=_params(("parallel", "arbitrary")))(u0, u0, da, cw, cb)


def _conv_bwd_input(du, cw, name, *, tm=2048, tn=256):
    _, S, F = du.shape
    tm = _tile(S, tm, HALO)
    tn = _tile(F, tn, 128)
    blk, _, nxt, w, _ = _conv_specs(tm, tn, S // HALO)
    ni = S // tm

    def body(d_ref, h_ref, w_ref, o_ref):
        d = d_ref[...].astype(F32)
        nx = jnp.where(pl.program_id(1) == ni - 1, 0.0, h_ref[...].astype(F32))
        for half in range(2):
            wv = w_ref[half]
            y = d[half] * wv[2:3, :] + _shift_up(d[half], nx[half], 1) * wv[1:2, :] + _shift_up(d[half], nx[half], 2) * wv[0:1, :]
            o_ref[half] = y.astype(o_ref.dtype)

    return pl.pallas_call(body, name=name, grid=(F // tn, ni), in_specs=[blk, nxt, w], out_specs=blk,
                          out_shape=jax.ShapeDtypeStruct((2, S, F), CDT),
                          compiler_params=_params(("parallel", "parallel")))(du, du, cw)


ANY = pl.BlockSpec(memory_space=pl.ANY)


def _place():
    return lax.axis_index("x"), lax.axis_index("y"), lax.axis_index("c")


def _other_chips(x, y):
    return ((1 - x, y), (x, 1 - y), (1 - x, 1 - y))


def _when(pred, fn):
    if pred is True:
        fn()
    else:
        pl.when(pred)(fn)


GATHER_SEMS = (pltpu.SemaphoreType.DMA((6,)), pltpu.SemaphoreType.DMA((6,)))
SCATTER_SEMS = (pltpu.SemaphoreType.DMA((3,)), pltpu.SemaphoreType.DMA((3,)))


def _gather_steps(out_ref, send_sems, recv_sems, first=True, middle=True, last=True):
    half = out_ref.shape[1] // 2
    x, y, c = _place()
    chips = _other_chips(x, y)

    def part(chip, pc):
        return out_ref.at[2 * chip[0] + chip[1], pl.ds(pl.multiple_of(pc * half, 16), half), :]

    def copy(k, chip, pc, to):
        return pltpu.make_async_remote_copy(src_ref=part(chip, pc), dst_ref=part(chip, pc),
                                            send_sem=send_sems.at[k], recv_sem=recv_sems.at[k],
                                            device_id=to, device_id_type=MESH)

    def send_mine():
        for j, chip in enumerate(chips):
            copy(j, (x, y), c, (*chip, c)).start()

    def pass_on():
        for j, chip in enumerate(chips):
            copy(j, chip, c, (x, y, c)).wait_recv()
            copy(3 + j, chip, c, (x, y, 1 - c)).start()

    def finish():
        for j, chip in enumerate(chips):
            copy(3 + j, chip, 1 - c, (x, y, c)).wait_recv()
        for j, chip in enumerate(chips):
            copy(j, (x, y), c, (*chip, c)).wait_send()
            copy(3 + j, chip, c, (x, y, 1 - c)).wait_send()

    _when(first, send_mine)
    _when(middle, pass_on)
    _when(last, finish)


def _gather_small(v):
    m = v.shape[0]

    def body(v_ref, out_ref, send_sems, recv_sems, local_sem):
        x, y, c = _place()
        me, sibling = (x, y, c), (x, y, 1 - c)
        chips = _other_chips(x, y)

        def rows(px, py, pc):
            return out_ref.at[pl.ds((4 * px + 2 * py + pc) * m, m), :]

        def copy(k, block, to, src=None):
            return pltpu.make_async_remote_copy(src_ref=rows(*block) if src is None else src, dst_ref=rows(*block),
                                                send_sem=send_sems.at[k], recv_sem=recv_sems.at[k],
                                                device_id=to, device_id_type=MESH)

        mine = pltpu.make_async_copy(v_ref, rows(*me), local_sem)
        mine.start()
        first = [copy(0, me, sibling, src=v_ref)]
        first += [copy(1 + j, me, (*chip, c), src=v_ref) for j, chip in enumerate(chips)]
        for cp in first:
            cp.start()
        passed = [copy(4 + j, (*chip, c), sibling) for j, chip in enumerate(chips)]
        for j, chip in enumerate(chips):
            copy(1 + j, (*chip, c), me).wait_recv()
            passed[j].start()
        copy(0, sibling, me).wait_recv()
        for j, chip in enumerate(chips):
            copy(4 + j, (*chip, 1 - c), me).wait_recv()
        for cp in first + passed:
            cp.wait_send()
        mine.wait()

    vm = pl.BlockSpec(memory_space=pltpu.VMEM)
    return pl.pallas_call(body, name="gather_small", in_specs=[vm], out_specs=vm,
                          out_shape=jax.ShapeDtypeStruct((8 * m, 128), v.dtype),
                          scratch_shapes=[pltpu.SemaphoreType.DMA((7,)), pltpu.SemaphoreType.DMA((7,)), pltpu.SemaphoreType.DMA])(v)


SWAP_SEMS = (pltpu.SemaphoreType.DMA, pltpu.SemaphoreType.DMA)


def _swap_steps(g_ref, out_ref, send_sem, recv_sem, first=True, last=True):
    half = out_ref.shape[1]
    x, y, c = _place()

    def copy():
        src = g_ref.at[:, pl.ds(pl.multiple_of((1 - c) * half, 8), half), :]
        return pltpu.make_async_remote_copy(src_ref=src, dst_ref=out_ref, send_sem=send_sem, recv_sem=recv_sem,
                                            device_id=(x, y, 1 - c), device_id_type=MESH)

    _when(first, lambda: copy().start())
    _when(last, lambda: copy().wait())


def _swap_halves(g, name):
    n, rows, _ = g.shape

    def body(g_ref, out_ref, send_sem, recv_sem):
        _swap_steps(g_ref, out_ref, send_sem, recv_sem)

    return pl.pallas_call(body, name=name, in_specs=[ANY], out_specs=ANY,
                          out_shape=jax.ShapeDtypeStruct((n, rows // 2, 128), g.dtype),
                          scratch_shapes=list(SWAP_SEMS))(g)


def _scatter_steps(h_ref, out_ref, send_sems, recv_sems, first=True, last=True):
    x, y, c = _place()

    def copies():
        return [pltpu.make_async_remote_copy(src_ref=h_ref.at[2 * chip[0] + chip[1]], dst_ref=out_ref.at[j],
                                             send_sem=send_sems.at[j], recv_sem=recv_sems.at[j],
                                             device_id=(*chip, c), device_id_type=MESH)
                for j, chip in enumerate(_other_chips(x, y))]

    def start():
        for cp in copies():
            cp.start()

    def finish():
        for cp in copies():
            cp.wait()

    _when(first, start)
    _when(last, finish)


def _join_halves(buf, name):
    half = buf.shape[0] // 2

    def body(buf_ref, out_ref, send_sem, recv_sem):
        del buf_ref
        x, y, c = _place()
        mine = out_ref.at[pl.ds(pl.multiple_of(c * half, 8), half), :]
        other = out_ref.at[pl.ds(pl.multiple_of((1 - c) * half, 8), half), :]
        cp = pltpu.make_async_remote_copy(src_ref=mine, dst_ref=mine, send_sem=send_sem, recv_sem=recv_sem,
                                          device_id=(x, y, 1 - c), device_id_type=MESH)
        cp.start()
        cp.wait_send()
        pltpu.make_async_remote_copy(src_ref=other, dst_ref=other, send_sem=send_sem, recv_sem=recv_sem,
                                     device_id=(x, y, 1 - c), device_id_type=MESH).wait_recv()

    return pl.pallas_call(body, name=name, in_specs=[ANY], out_specs=ANY,
                          out_shape=jax.ShapeDtypeStruct(buf.shape, buf.dtype), input_output_aliases={0: 0},
                          scratch_shapes=[pltpu.SemaphoreType.DMA, pltpu.SemaphoreType.DMA])(buf)


def _add_sibling(g, recv, c_idx, name):
    n, rows, _ = g.shape
    half = rows // 2
    tr = _tile(half, ADAM_ROWS, 16)
    nb = half // tr

    def body(c_ref, g_ref, r_ref, o_ref, ob_ref):
        s = g_ref[...] + r_ref[...]
        o_ref[...] = s
        ob_ref[...] = s.astype(CDT)

    out = pl.BlockSpec((None, tr, 128), lambda k, i, c: (k, i, 0))
    grid_spec = pltpu.PrefetchScalarGridSpec(
        num_scalar_prefetch=1, grid=(n, nb),
        in_specs=[pl.BlockSpec((None, tr, 128), lambda k, i, c: (k, c[0] * nb + i, 0)), out],
        out_specs=(out, out))
    return pl.pallas_call(body, name=name, grid_spec=grid_spec,
                          out_shape=(jax.ShapeDtypeStruct((n, half, 128), F32), jax.ShapeDtypeStruct((n, half, 128), CDT)),
                          compiler_params=_params(("parallel", "parallel")))(c_idx, g, recv)


def _add_chips(hsum, recv, chip_idx, name):
    n, half, _ = hsum.shape
    tr = _tile(half, ADAM_ROWS, 16)

    def body(k_ref, h_ref, r_ref, o_ref):
        o_ref[...] = ((h_ref[...] + r_ref[0].astype(F32)) + r_ref[1].astype(F32)) + r_ref[2].astype(F32)

    grid_spec = pltpu.PrefetchScalarGridSpec(
        num_scalar_prefetch=1, grid=(half // tr,),
        in_specs=[pl.BlockSpec((None, tr, 128), lambda i, k: (k[0], i, 0)),
                  pl.BlockSpec((3, tr, 128), lambda i, k: (0, i, 0))],
        out_specs=pl.BlockSpec((tr, 128), lambda i, k: (i, 0)))
    return pl.pallas_call(body, name=name, grid_spec=grid_spec, out_shape=jax.ShapeDtypeStruct((half, 128), F32),
                          compiler_params=_params(("parallel",)))(chip_idx, hsum, recv)


def _adamw_math(g, w, m, v):
    m2 = B1 * m + (1.0 - B1) * g
    v2 = B2 * v + (1.0 - B2) * (g * g)
    delta = -LR * ((m2 / BC1) / (jnp.sqrt(v2 / BC2) + AEPS) + WD * w)
    return delta, m2, v2


def _adamw(g, w, m, v, name):
    rows, cols = g.shape
    tr = _tile(rows, max(8, (ADAM_ROWS * 128 // cols) // 8 * 8), 8)

    def body(g_ref, w_ref, m_ref, v_ref, d_ref, m2_ref, v2_ref):
        d_ref[...], m2_ref[...], v2_ref[...] = _adamw_math(g_ref[...], w_ref[...], m_ref[...], v_ref[...])

    blk = pl.BlockSpec((None, tr, cols), lambda i: (0, i, 0))
    shp = jax.ShapeDtypeStruct((1, rows, cols), F32)
    return pl.pallas_call(body, name=name, grid=(rows // tr,), in_specs=[pl.BlockSpec((tr, cols), lambda i: (i, 0))] + [blk] * 3,
                          out_specs=(blk,) * 3, out_shape=(shp,) * 3, compiler_params=_params(("parallel",)))(g, w, m, v)


def _adamw_small(parts, w, m, v, name):
    rows = w.shape[0]

    def body(p_ref, w_ref, m_ref, v_ref, g_ref, d_ref, m2_ref, v2_ref):
        g = p_ref[0]
        for k in range(1, 8):
            g = g + p_ref[k]
        g_ref[...] = g
        d_ref[...], m2_ref[...], v2_ref[...] = _adamw_math(g, w_ref[...], m_ref[...], v_ref[...])

    shp = jax.ShapeDtypeStruct((rows, 128), F32)
    return pl.pallas_call(body, name=name, out_shape=(shp,) * 4)(parts, w, m, v)


def _pack_rows(parts, rows):
    flat = jnp.concatenate([p.reshape(-1) for p in parts])
    return jnp.pad(flat, (0, rows * 128 - flat.shape[0])).reshape(rows, 128)


def _unpack(flat, sizes, shapes):
    out, off = [], 0
    for n, s in zip(sizes, shapes):
        out.append(flat[off:off + n].reshape(s))
        off += n
    return out


def _to_shards(full, shard_shape, axis):
    if axis == 0:
        return full.reshape(N_CHIP, -1)
    r, cs = shard_shape
    return full.reshape(r, N_CHIP, cs).transpose(1, 0, 2).reshape(N_CHIP, -1)


def _from_shards(sh, shard_shape, axis):
    r, cs = shard_shape
    if axis == 0:
        return sh.reshape(N_CHIP * r, cs)
    return sh.reshape(N_CHIP, r, cs).transpose(1, 0, 2).reshape(r, N_CHIP * cs)


def _local_step(x0, mem, tgt, W, gains, ex=None):
    S = x0.shape[0]
    b_f = jnp.pad(gains["b_forget"], ((0, 0), (0, 128 - NH)))

    if ex is None:
        h1 = _rms_cast(x0, gains["attn_norm_g"], "norm_attn")
    else:
        h1, gathered = _rms_cast(x0, gains["attn_norm_g"], "norm_attn", slots=ex.slots("a"))
        W = ex.unpack("a", gathered)
    w_in = jnp.pad(W["w_in"], ((0, 0), (0, IN_PAD - IN_COLS)))
    qkv, sq2, sk2 = _proj_qkv(h1, w_in[:, :NQKV], "proj_qkv")
    fox, sb_q, sb_v = (0, NH, 2 * NH), 3 * NH, 5 * NH
    fl = _mm_nn(h1, w_in[:, NQKV:NQKV + 128], F32, "proj_gate")
    cum = _gate_fwd(fl, b_f, "gate_cumsum")
    fq2, fk2 = _fox_operands(qkv, cum, "fox_operands")
    fo_h, lse, gathered = _fox_fwd(fq2, fk2, qkv, "fox_fwd", slots=None if ex is None else ex.slots("b"), v0=fox[2])
    if ex is not None:
        W = {**W, **ex.unpack("b", gathered)}
    cw = W["conv_w"].reshape(3, 2, DFF).transpose(1, 0, 2)
    cb = gains["conv_b"].reshape(2, 1, DFF)
    so_h, s_lt = _sb_fwd(sq2, sk2, qkv, "sb_fwd", v0=sb_v)
    x1, mixed = _out_proj(fo_h, so_h, gains["fox_out_g"], gains["sb_out_g"], W["w_out"], x0, "out_proj")

    h2 = _rms_cast(x1, gains["xattn_norm_g"], "norm_xattn")
    mn = _rms_cast(mem, gains["mem_norm_g"], "norm_mem")
    mq = _mm_nn(h2, W["w_mq"], CDT, "proj_mq")
    kv = _mm_nn(mn, W["w_mkv"], CDT, "proj_mkv")
    x2, mo = _xattn_fwd(mq, kv, W["w_mo"], x1, "xattn_fwd")

    h3 = _rms_cast(x2, gains["ffn_norm_g"], "norm_ffn")
    u0 = _mm_nn(h3, W["w_up"], CDT, "ffn_up", tm=512, tn=DFF, halves=True)
    act = _conv_act(u0, cw, cb, "conv_act")
    x3 = _mm_nn(act, W["w_down"], F32, "ffn_down", tm=512, residual=x2)
    loss, dx3, dg_final = _loss_bwd(x3, tgt, gains["final_norm_g"].reshape(1, D), "loss")

    gw, gs = {}, {"final_norm_g": dg_final}
    da = _mm_nt(dx3, W["w_down"], "ffn_down_dx", tn=DFF, out_dtype=CDT)
    gw["w_down"] = _mm_tn(act, dx3, "ffn_down_dw", tka=DFF)
    du, dwb = _conv_act_bwd(u0, da, cw, cb, "conv_act_bwd")
    gw["conv_w"] = dwb[:, :3].transpose(1, 0, 2).reshape(3, 2 * DFF)
    gs["conv_b"] = dwb[:, 3].reshape(1, 2 * DFF)
    du0 = _conv_bwd_input(du, cw, "conv_bwd_input")
    gw["w_up"] = _mm_tn(h3, du0, "ffn_up_dw", tn=DFF, b_halves=True)
    dx2, gs["ffn_norm_g"] = _mm_nt_rmsbwd(du0, W["w_up"], x2, gains["ffn_norm_g"], dx3, "ffn_up_dx", tk=DFF, a_halves=True)

    dmo = _mm_nt(dx2, W["w_mo"], "mo_dx", tn=512, out_dtype=CDT)
    gw["w_mo"] = _mm_tn(mo, dx2, "mo_dw")
    dmq, dkv = _xattn_bwd(mq, kv, dmo, "xattn_bwd")
    gw["w_mq"] = _mm_tn(h2, dmq, "mq_dw")
    dx1, gs["xattn_norm_g"] = _mm_nt_rmsbwd(dmq, W["w_mq"], x1, gains["xattn_norm_g"], dx2, "mq_dx")
    gw["w_mkv"] = _mm_tn(mn, dkv, "mkv_dw")
    _, gs["mem_norm_g"] = _mm_nt_rmsbwd(dkv, W["w_mkv"], mem, gains["mem_norm_g"], jnp.zeros_like(mem), "mkv_dx")

    gw["w_out"] = _mm_tn(mixed, dx1, "out_dw")
    dfo_h, dso_h, gs["fox_out_g"], gs["sb_out_g"] = _out_proj_bwd(dx1, W["w_out"], fo_h, so_h, gains["fox_out_g"], gains["sb_out_g"], "out_dx")
    flat = None if ex is None else ex.flat("b", gw)
    dfq, dfk, dfv, dck, dcq, got = _fox_bwd(qkv, qkv, qkv, fq2, fk2, fo_h, dfo_h, lse, "fox_bwd", h0=fox, swap=flat)
    pair, pair16 = (None, None) if ex is None else ex.pair_sums("b", flat, got)
    dsq, dsk, dsv, arrived = _sb_bwd(qkv, sq2, sk2, qkv, dso_h, s_lt, "sb_bwd", scatter=pair16, q0=sb_q, v0=sb_v)
    dfl, db = _gate_bwd(jnp.pad(dck[:, 0, :].T, ((0, 0), (0, 128 - NH))), dcq, fl, b_f, "gate_bwd")
    gs["b_forget"] = db[:, :NH]
    dqkv = jnp.concatenate([dfq, dfk.astype(CDT), dfv.astype(CDT), dsq, dsk.astype(CDT), dsv.astype(CDT)], axis=0)
    dproj = jnp.concatenate([dqkv.transpose(1, 0, 2).reshape(S, NQKV), dfl.astype(CDT),
                             jnp.zeros((S, IN_PAD - NQKV - 128), CDT)], axis=1)
    gw["w_in"] = _mm_tn(h1, dproj, "in_dw", tn=IN_PAD)[:, :IN_COLS]
    if ex is None:
        dx0, gs["attn_norm_g"] = _mm_nt_rmsbwd(dproj, w_in, x0, gains["attn_norm_g"], dx1, "in_dx", tk=IN_PAD)
        return loss, dx0, gw, gs, None
    pair_a, pair16_a = ex.pair_sums("a", ex.flat("a", gw))
    dx0, gs["attn_norm_g"], arrived_a = _mm_nt_rmsbwd(dproj, w_in, x0, gains["attn_norm_g"], dx1, "in_dx", tk=IN_PAD, scatter=pair16_a)
    return loss, dx0, gw, gs, {"a": (pair_a, arrived_a), "b": (pair, arrived)}


NAMES = ("attn_norm_g", "w_in", "b_forget", "fox_out_g", "sb_out_g", "w_out", "xattn_norm_g", "mem_norm_g", "w_mq",
         "w_mkv", "w_mo", "ffn_norm_g", "w_up", "conv_w", "conv_b", "w_down", "final_norm_g")


class _Exchange:
    def __init__(self, w):
        self.w = w
        xi, yi, ci = _place()
        self.core = ci
        self.chip = 2 * xi + yi
        self.core_idx = jnp.reshape(ci, (1,)).astype(jnp.int32)
        self.chip_idx = jnp.reshape(self.chip, (1,)).astype(jnp.int32)

    def slots(self, g):
        parts = []
        for name, shape, _ in GROUPS[g]:
            blk = self.w[name].reshape(shape)
            parts.append(lax.bitcast_convert_type(blk, CDT) if name == "conv_w" else blk.astype(CDT))
        rows = _rows_g(GROUPS[g])
        return lax.dynamic_update_slice(lax.empty((N_CHIP, rows, 128), CDT), _pack_rows(parts, rows)[None], (self.chip, 0, 0))

    def unpack(self, g, gathered):
        flat, full, off = gathered.reshape(N_CHIP, -1), {}, 0
        for (name, shape, axis), n in zip(GROUPS[g], _gather_sizes(GROUPS[g])):
            sh = flat[:, off:off + n]
            off += n
            if name == "conv_w":
                sh = lax.bitcast_convert_type(sh.reshape(N_CHIP, n // 2, 2), F32)
            full[name] = _from_shards(sh, shape, axis)
        return full

    def flat(self, g, gw):
        rows = _rows_f(GROUPS[g])
        flat = jnp.concatenate([_to_shards(gw[name], shape, axis) for name, shape, axis in GROUPS[g]], axis=1)
        return jnp.pad(flat, ((0, 0), (0, rows * 128 - flat.shape[1]))).reshape(N_CHIP, rows, 128)

    def pair_sums(self, g, flat, got=None):
        if got is None:
            got = _swap_halves(flat, "swap_halves_" + g)
        return _add_sibling(flat, got, self.core_idx, "add_sibling_" + g)

    def finish(self, g, pair, arrived):
        rows = _rows_f(GROUPS[g])
        mine = _add_chips(pair, arrived, self.chip_idx, "add_chips_" + g)
        whole = _join_halves(lax.dynamic_update_slice(lax.empty((rows, 128), F32), mine, (self.core * (rows // 2), 0)), "join_halves_" + g)
        shapes = [s for _, s, _ in GROUPS[g]]
        return {name: arr for (name, _, _), arr in zip(GROUPS[g], _unpack(whole.reshape(-1), _sizes(GROUPS[g]), shapes))}


def _step(x, mem, loss_target, w, m, v):
    ex = _Exchange(w)

    gains = {name: w[name].reshape(1, -1) for name, _ in SMALL}
    loss, grad_x, gw, gs, reduced = _local_step(x[0], mem[0], loss_target[0], {}, gains, ex)

    grads = {**ex.finish("b", *reduced["b"]), **ex.finish("a", *reduced["a"])}
    small = jnp.concatenate([gs[name].reshape(-1) for name, _ in SMALL] + [loss[0, :1]])
    small = jnp.pad(small, (0, ROWS_S * 128 - P_SMALL)).reshape(ROWS_S, 128)
    small_parts = _gather_small(small).reshape(8, ROWS_S, 128)

    def flat_small(d):
        return _pack_rows([d[name] for name, _ in SMALL], ROWS_S)

    outs = {}
    for name, shape, _ in BIG:
        g = grads[name]
        res = _adamw(g, w[name], m[name], v[name], "adamw_" + name)
        for prefix, arr in zip(("grad_", "delta_", "new_m_", "new_v_"), (g, *res)):
            outs[prefix + name] = arr.reshape(w[name].shape)
    small_res = _adamw_small(small_parts, flat_small(w), flat_small(m), flat_small(v), "adamw_small")
    g_sm = small_res[0]
    for prefix, sm in zip(("grad_", "delta_", "new_m_", "new_v_"), small_res):
        for (name, n), arr in zip(SMALL, _unpack(sm.reshape(-1), [n for _, n in SMALL], [(n,) for _, n in SMALL])):
            outs[prefix + name] = arr.reshape(w[name].shape)
    total_loss = g_sm.reshape(-1)[P_SMALL - 1]
    return (total_loss, grad_x[None], *[outs[p + n] for p in ("grad_", "delta_", "new_m_", "new_v_") for n in NAMES])


def kernel(x, mem, attn_norm_g, w_in, b_forget, fox_out_g, sb_out_g, w_out, xattn_norm_g, mem_norm_g, w_mq, w_mkv, w_mo, ffn_norm_g, w_up, conv_w, conv_b, w_down, final_norm_g, loss_target, m_attn_norm_g, m_w_in, m_b_forget, m_fox_out_g, m_sb_out_g, m_w_out, m_xattn_norm_g, m_mem_norm_g, m_w_mq, m_w_mkv, m_w_mo, m_ffn_norm_g, m_w_up, m_conv_w, m_conv_b, m_w_down, m_final_norm_g, v_attn_norm_g, v_w_in, v_b_forget, v_fox_out_g, v_sb_out_g, v_w_out, v_xattn_norm_g, v_mem_norm_g, v_w_mq, v_w_mkv, v_w_mo, v_ffn_norm_g, v_w_up, v_conv_w, v_conv_b, v_w_down, v_final_norm_g):
    given = dict(locals())
    w = {n: given[n] for n in NAMES}
    m = {n: given["m_" + n] for n in NAMES}
    v = {n: given["v_" + n] for n in NAMES}
    return _step(x, mem, loss_target, w, m, v)
```

```python
import functools

import numpy as np
import jax
import jax.numpy as jnp
from jax import lax
from jax.experimental import pallas as pl
from jax.experimental.pallas import tpu as pltpu

F32 = jnp.float32
CDT = jnp.bfloat16
MESH = pl.DeviceIdType.MESH

D = 1024
HD = 64
NH = 8
GW = NH * HD
NQKV = 6 * GW
IN_COLS = NQKV + NH
IN_PAD = NQKV + 256
NMH = 4
MHD = D // NMH
DFF = 2816
EPS = 1e-6
ATT_SCALE = HD ** -0.5
MEM_SCALE = MHD ** -0.5
NEG = -1e30

LR, B1, B2, AEPS, WD, STEP = 0.001, 0.9, 0.999, 1e-08, 0.01, 10
BC1 = 1.0 - B1 ** STEP
BC2 = 1.0 - B2 ** STEP

ATT_TILES = {"fox_fwd": (1024, 2048), "fox_bwd": (1024, 1024), "sb_fwd": (1024, 1024), "sb_bwd": (1024, 1024)}
W_SB = 256
VMEM_LIMIT = 52 * 2 ** 20

N_CHIP = 4
BIG = (("w_in", (D, IN_COLS // N_CHIP), 1), ("w_out", (D // N_CHIP, D), 0), ("w_mq", (D // N_CHIP, D), 0),
       ("w_mkv", (D, 2 * D // N_CHIP), 1), ("w_mo", (D // N_CHIP, D), 0), ("w_up", (D, 2 * DFF // N_CHIP), 1),
       ("conv_w", (3, 2 * DFF // N_CHIP), 1), ("w_down", (DFF // N_CHIP, D), 0))
GROUPS = {"a": BIG[:1], "b": BIG[1:]}
ADAM_ROWS = 1536


def _sizes(group):
    return tuple(int(np.prod(s)) for _, s, _ in group)


def _gather_sizes(group):
    return tuple(2 * n if name == "conv_w" else n for (name, _, _), n in zip(group, _sizes(group)))


def _rows_g(group):
    return -(-sum(_gather_sizes(group)) // 4096) * 32


def _rows_f(group):
    return -(-sum(_sizes(group)) // 65536) * 512
SMALL = (("attn_norm_g", 1024), ("b_forget", 8), ("fox_out_g", 512), ("sb_out_g", 512), ("xattn_norm_g", 1024),
         ("mem_norm_g", 1024), ("ffn_norm_g", 1024), ("conv_b", 2 * DFF), ("final_norm_g", 1024))
P_SMALL = sum(n for _, n in SMALL) + 1
ROWS_S = -(-P_SMALL // 1024) * 8


def _params(sem=None, vmem=VMEM_LIMIT):
    return pltpu.CompilerParams(dimension_semantics=sem, vmem_limit_bytes=vmem)


def _tile(n, pref, mult):
    t = (min(pref, n) // mult) * mult
    while t >= mult:
        if n % t == 0:
            return t
        t -= mult
    return n


def _dot(a, b):
    return jnp.dot(a, b, preferred_element_type=F32)


def _dot_nt(a, b):
    return lax.dot_general(a, b, (((1,), (1,)), ((), ())), preferred_element_type=F32)


def _dot_tn(a, b):
    return lax.dot_general(a, b, (((0,), (0,)), ((), ())), preferred_element_type=F32)


def _split3(x):
    h1 = x.astype(CDT)
    r1 = x - h1.astype(F32)
    h2 = r1.astype(CDT)
    h3 = (r1 - h2.astype(F32)).astype(CDT)
    return h1, h2, h3


def _rms_bwd(dh, x, g):
    r = lax.rsqrt(jnp.mean(x * x, axis=-1, keepdims=True) + EPS)
    xn = x * r
    dg = jnp.sum(dh * xn, axis=0, keepdims=True)
    dhg = dh * g
    dx = r * (dhg - xn * jnp.mean(dhg * xn, axis=-1, keepdims=True))
    return dx, dg


def _mm_nn(a, b, out_dtype, name, *, tm=1024, tn=512, residual=None, halves=False):
    M, K = a.shape
    N = b.shape[1]
    tm = _tile(M, tm, 16)
    tn = _tile(N // 2 if halves else N, tn, 128)
    nj = N // tn

    def body(*refs):
        a_ref, b_ref = refs[0], refs[1]
        o_ref = refs[-1]
        acc = _dot(a_ref[...].astype(CDT), b_ref[...].astype(CDT))
        if residual is not None:
            acc = acc + refs[2][...]
        o_ref[...] = acc.astype(o_ref.dtype)

    in_specs = [pl.BlockSpec((tm, K), lambda i, j: (i, 0)), pl.BlockSpec((K, tn), lambda i, j: (0, j))]
    ops = [a, b]
    if residual is not None:
        in_specs.append(pl.BlockSpec((tm, tn), lambda i, j: (i, j)))
        ops.append(residual)
    if halves:
        njh = nj // 2
        out_shape = jax.ShapeDtypeStruct((2, M, N // 2), out_dtype)
        out_spec = pl.BlockSpec((None, tm, tn), lambda i, j: (j // njh, i, j % njh))
    else:
        out_shape = jax.ShapeDtypeStruct((M, N), out_dtype)
        out_spec = pl.BlockSpec((tm, tn), lambda i, j: (i, j))
    return pl.pallas_call(body, name=name, grid=(M // tm, nj), in_specs=in_specs, out_specs=out_spec,
                          out_shape=out_shape, compiler_params=_params(("parallel", "parallel")))(*ops)


def _mm_tn(a, b, name, *, tka=512, tn=1024, ts=512, b_halves=False):
    S, Ka = a.shape
    N = 2 * b.shape[2] if b_halves else b.shape[1]
    tka = _tile(Ka, tka, 128)
    tn = _tile(N // 2 if b_halves else N, tn, 128)
    ts = _tile(S, ts, 16)
    nn = N // tn

    def body(a_ref, b_ref, o_ref):
        @pl.when(pl.program_id(2) == 0)
        def _():
            o_ref[...] = jnp.zeros_like(o_ref)
        o_ref[...] += _dot_tn(a_ref[...].astype(CDT), b_ref[...].astype(CDT))

    if b_halves:
        nnh = nn // 2
        b_spec = pl.BlockSpec((None, ts, tn), lambda i, j, s: (j // nnh, s, j % nnh))
    else:
        b_spec = pl.BlockSpec((ts, tn), lambda i, j, s: (s, j))
    return pl.pallas_call(
        body, name=name, grid=(Ka // tka, nn, S // ts),
        in_specs=[pl.BlockSpec((ts, tka), lambda i, j, s: (s, i)), b_spec],
        out_specs=pl.BlockSpec((tka, tn), lambda i, j, s: (i, j)),
        out_shape=jax.ShapeDtypeStruct((Ka, N), F32),
        compiler_params=_params(("parallel", "parallel", "arbitrary")))(a, b)


def _mm_nt(a, b, name, *, tm=512, tn=None, tk=None, a_halves=False, out_dtype=F32,
           epilogue=None, extra=(), extra_specs=(), out_shape=None, out_specs=None, scatter=None):
    if a_halves:
        M, K = a.shape[1], 2 * a.shape[2]
    else:
        M, K = a.shape
    N = b.shape[0]
    tm = _tile(M, tm, 16)
    tn = N if (epilogue is not None or tn is None) else _tile(N, tn, 128)
    tk = K if tk is None else _tile(K // 2 if a_halves else K, tk, 128)
    nk = K // tk
    n_extra = len(extra)
    grid = (M // tm, N // tn, nk)

    def body(*refs):
        if scatter is not None:
            *refs, send_sems, recv_sems = refs
            h_ref, recv_ref = refs[2 + n_extra], refs[-2]
            refs = (*refs[:2 + n_extra], *refs[3 + n_extra:-2], refs[-1])
            at = [pl.program_id(d) for d in range(3)]
            _scatter_steps(h_ref, recv_ref, send_sems, recv_sems,
                           first=(at[0] == 0) & (at[1] == 0) & (at[2] == 0),
                           last=(at[0] == grid[0] - 1) & (at[1] == grid[1] - 1) & (at[2] == grid[2] - 1))
        a_ref, b_ref = refs[0], refs[1]
        extra_refs = refs[2:2 + n_extra]
        out_refs = refs[2 + n_extra:-1]
        acc_ref = refs[-1]
        k = pl.program_id(2)

        @pl.when(k == 0)
        def _():
            acc_ref[...] = jnp.zeros_like(acc_ref)
        acc_ref[...] += _dot_nt(a_ref[...].astype(CDT), b_ref[...].astype(CDT))

        @pl.when(k == nk - 1)
        def _():
            if epilogue is None:
                out_refs[0][...] = acc_ref[...].astype(out_refs[0].dtype)
            else:
                epilogue(acc_ref[...], pl.program_id(0), extra_refs, out_refs)

    if a_halves:
        nkh = nk // 2
        a_spec = pl.BlockSpec((None, tm, tk), lambda i, j, k: (k // nkh, i, k % nkh))
    else:
        a_spec = pl.BlockSpec((tm, tk), lambda i, j, k: (i, k))
    if epilogue is None:
        out_shape = jax.ShapeDtypeStruct((M, N), out_dtype)
        out_specs = pl.BlockSpec((tm, tn), lambda i, j, k: (i, j))
        sem = ("parallel", "parallel", "arbitrary")
    else:
        sem = ("arbitrary", "arbitrary", "arbitrary")
    in_specs = [a_spec, pl.BlockSpec((tn, tk), lambda i, j, k: (j, k)), *extra_specs]
    scratch = [pltpu.VMEM((tm, tn), F32)]
    ops = [a, b, *extra]
    if scatter is not None:
        in_specs, ops, scratch = in_specs + [ANY], ops + [scatter], scratch + list(SCATTER_SEMS)
        out_specs = (*out_specs, ANY)
        out_shape = (*out_shape, jax.ShapeDtypeStruct((3,) + scatter.shape[1:], scatter.dtype))
    return pl.pallas_call(body, name=name, grid=grid, in_specs=in_specs, out_specs=out_specs, out_shape=out_shape,
                          scratch_shapes=scratch, compiler_params=_params(sem))(*ops)


def _mm_nt_rmsbwd(a, b, x, g, dres, name, *, tm=512, tk=None, a_halves=False, scatter=None):
    M = x.shape[0]
    tm = _tile(M, tm, 16)

    def epilogue(acc, i, extra_refs, out_refs):
        x_ref, g_ref, r_ref = extra_refs
        dx_ref, dg_ref = out_refs
        dx, dg = _rms_bwd(acc, x_ref[...], g_ref[...])
        dx_ref[...] = r_ref[...] + dx

        @pl.when(i == 0)
        def _():
            dg_ref[...] = jnp.zeros_like(dg_ref)
        dg_ref[...] += dg

    row = pl.BlockSpec((tm, D), lambda i, j, k: (i, 0))
    vec = pl.BlockSpec((1, D), lambda i, j, k: (0, 0))
    return _mm_nt(a, b, name, tm=tm, tk=tk, a_halves=a_halves, epilogue=epilogue,
                  extra=(x, g, dres), extra_specs=(row, vec, row),
                  out_shape=(jax.ShapeDtypeStruct((M, D), F32), jax.ShapeDtypeStruct((1, D), F32)),
                  out_specs=(row, vec), scatter=scatter)


def _rms_cast(x, g, name, *, tm=512):
    M, W = x.shape
    tm = _tile(M, tm, 16)

    def body(x_ref, g_ref, o_ref):
        xf = x_ref[...]
        r = lax.rsqrt(jnp.mean(xf * xf, axis=-1, keepdims=True) + EPS)
        o_ref[...] = (xf * r * g_ref[...]).astype(o_ref.dtype)

    return pl.pallas_call(body, name=name, grid=(M // tm,),
                          in_specs=[pl.BlockSpec((tm, W), lambda i: (i, 0)), pl.BlockSpec((1, W), lambda i: (0, 0))],
                          out_specs=pl.BlockSpec((tm, W), lambda i: (i, 0)),
                          out_shape=jax.ShapeDtypeStruct((M, W), CDT),
                          compiler_params=_params(("parallel",)))(x, g)


def _tri(n, lower):
    r = lax.broadcasted_iota(jnp.int32, (n, n), 0)
    c = lax.broadcasted_iota(jnp.int32, (n, n), 1)
    return (c <= r if lower else c >= r).astype(CDT)


def _gate_fwd(fl, b, name, *, tm=512):
    S = fl.shape[0]
    tm = _tile(S, tm, 16)

    def body(f_ref, b_ref, c_ref, carry):
        @pl.when(pl.program_id(0) == 0)
        def _():
            carry[...] = jnp.zeros_like(carry)
        z = f_ref[...] + b_ref[...]
        lf = jnp.minimum(z, 0.0) - jnp.log(1.0 + jnp.exp(-jnp.abs(z)))
        tri = _tri(tm, True)
        cum = sum(_dot(tri, p) for p in _split3(lf)) + carry[...]
        c_ref[...] = cum
        carry[...] = cum[tm - 1:tm, :]

    return pl.pallas_call(body, name=name, grid=(S // tm,),
                          in_specs=[pl.BlockSpec((tm, 128), lambda i: (i, 0)), pl.BlockSpec((1, 128), lambda i: (0, 0))],
                          out_specs=pl.BlockSpec((tm, 128), lambda i: (i, 0)),
                          out_shape=jax.ShapeDtypeStruct((S, 128), F32),
                          scratch_shapes=[pltpu.VMEM((1, 128), F32)],
                          compiler_params=_params(("arbitrary",)))(fl, b)


def _gate_bwd(dck, dcq, fl, b, name, *, tm=512):
    S = fl.shape[0]
    tm = _tile(S, tm, 16)
    nb = S // tm

    def body(dck_ref, dcq_ref, f_ref, b_ref, df_ref, db_ref, carry):
        @pl.when(pl.program_id(0) == 0)
        def _():
            carry[...] = jnp.zeros_like(carry)
            db_ref[...] = jnp.zeros_like(db_ref)
        lane = lax.broadcasted_iota(jnp.int32, (1, 128), 1)
        dc = dck_ref[...]
        for h in range(NH):
            dc = dc + dcq_ref[h] * (lane == h).astype(F32)
        tri = _tri(tm, False)
        suf = sum(_dot(tri, p) for p in _split3(dc)) + carry[...]
        carry[...] = suf[0:1, :]
        df = suf * jax.nn.sigmoid(-(f_ref[...] + b_ref[...]))
        df_ref[...] = df
        db_ref[...] += jnp.sum(df, axis=0, keepdims=True)

    rev = pl.BlockSpec((tm, 128), lambda i: (nb - 1 - i, 0))
    cols = pl.BlockSpec((NH, tm, 1), lambda i: (0, nb - 1 - i, 0))
    vec = pl.BlockSpec((1, 128), lambda i: (0, 0))
    return pl.pallas_call(body, name=name, grid=(nb,), in_specs=[rev, cols, rev, vec], out_specs=(rev, vec),
                          out_shape=(jax.ShapeDtypeStruct((S, 128), F32), jax.ShapeDtypeStruct((1, 128), F32)),
                          scratch_shapes=[pltpu.VMEM((1, 128), F32)],
                          compiler_params=_params(("arbitrary",)))(dck, dcq, fl, b)


def _group_rows(o_ref):
    return jnp.concatenate([o_ref[h] for h in range(NH)], axis=1)


def _out_proj(fo, so, gf, gs, w_out, x0, name, *, tm=512):
    S = fo.shape[1]
    tm = _tile(S, tm, 16)

    def body(fo_ref, so_ref, gf_ref, gs_ref, w_ref, x_ref, x1_ref, mx_ref):
        for ref, g_ref, lo in ((fo_ref, gf_ref, 0), (so_ref, gs_ref, GW)):
            o = _group_rows(ref)
            r = lax.rsqrt(jnp.mean(o * o, axis=-1, keepdims=True) + EPS)
            mx_ref[:, lo:lo + GW] = (o * r * g_ref[...]).astype(CDT)
        x1_ref[...] = x_ref[...] + _dot(mx_ref[...], w_ref[...])

    half = pl.BlockSpec((NH, tm, HD), lambda i: (0, i, 0))
    gvec = pl.BlockSpec((1, GW), lambda i: (0, 0))
    row = pl.BlockSpec((tm, D), lambda i: (i, 0))
    return pl.pallas_call(body, name=name, grid=(S // tm,),
                          in_specs=[half, half, gvec, gvec, pl.BlockSpec((D, D), lambda i: (0, 0)), row],
                          out_specs=(row, row),
                          out_shape=(jax.ShapeDtypeStruct((S, D), F32), jax.ShapeDtypeStruct((S, D), CDT)),
                          compiler_params=_params(("parallel",)))(fo, so, gf, gs, w_out, x0)


def _out_proj_bwd(dx1, w_out, fo, so, gf, gs, name, *, tm=512):
    S = fo.shape[1]
    tm = _tile(S, tm, 16)

    def epilogue(acc, i, extra_refs, out_refs):
        fo_ref, so_ref, gf_ref, gs_ref = extra_refs
        dfo_ref, dso_ref, dgf_ref, dgs_ref = out_refs

        @pl.when(i == 0)
        def _():
            dgf_ref[...] = jnp.zeros_like(dgf_ref)
            dgs_ref[...] = jnp.zeros_like(dgs_ref)
        for lo, o_ref, g_ref, do_ref, dg_ref in ((0, fo_ref, gf_ref, dfo_ref, dgf_ref), (GW, so_ref, gs_ref, dso_ref, dgs_ref)):
            dx, dg = _rms_bwd(acc[:, lo:lo + GW], _group_rows(o_ref), g_ref[...])
            for h in range(NH):
                do_ref[h] = dx[:, h * HD:(h + 1) * HD].astype(do_ref.dtype)
            dg_ref[...] += dg

    half = pl.BlockSpec((NH, tm, HD), lambda i, j, k: (0, i, 0))
    gvec = pl.BlockSpec((1, GW), lambda i, j, k: (0, 0))
    return _mm_nt(dx1, w_out, name, tm=tm, epilogue=epilogue, extra=(fo, so, gf, gs),
                  extra_specs=(half, half, gvec, gvec),
                  out_shape=(jax.ShapeDtypeStruct((NH, S, HD), CDT), jax.ShapeDtypeStruct((NH, S, HD), CDT),
                             jax.ShapeDtypeStruct((1, GW), F32), jax.ShapeDtypeStruct((1, GW), F32)),
                  out_specs=(half, half, gvec, gvec))


def _loss_bwd(x3, tgt, g, name, *, tm=512):
    S = x3.shape[0]
    tm = _tile(S, tm, 16)

    def body(x_ref, t_ref, g_ref, dx_ref, loss_ref, dg_ref):
        @pl.when(pl.program_id(0) == 0)
        def _():
            loss_ref[...] = jnp.zeros_like(loss_ref)
            dg_ref[...] = jnp.zeros_like(dg_ref)
        x = x_ref[...]
        gv = g_ref[...]
        r = lax.rsqrt(jnp.mean(x * x, axis=-1, keepdims=True) + EPS)
        xn = x * r
        err = xn * gv - t_ref[...]
        loss_ref[...] += jnp.full(loss_ref.shape, 0.5 * jnp.sum(jnp.mean(err * err, axis=-1, keepdims=True)), F32)
        dy = err * (1.0 / D)
        dg_ref[...] += jnp.sum(dy * xn, axis=0, keepdims=True)
        dyg = dy * gv
        dx_ref[...] = r * (dyg - xn * jnp.mean(dyg * xn, axis=-1, keepdims=True))

    row = pl.BlockSpec((tm, D), lambda i: (i, 0))
    vec = pl.BlockSpec((1, D), lambda i: (0, 0))
    dx3, loss, dg = pl.pallas_call(
        body, name=name, grid=(S // tm,), in_specs=[row, row, vec],
        out_specs=(row, pl.BlockSpec((1, 128), lambda i: (0, 0)), vec),
        out_shape=(jax.ShapeDtypeStruct((S, D), F32), jax.ShapeDtypeStruct((1, 128), F32), jax.ShapeDtypeStruct((1, D), F32)),
        compiler_params=_params(("arbitrary",)))(x3, tgt, g)
    return loss, dx3, dg


MASKED, FIRST, LAST = 1, 2, 4


def _att_tiles(name, S):
    tq, tk = ATT_TILES[name]
    return min(tq, S), min(tk, S)


def _pairs(S, tq, tk, descending=True):
    assert tk % tq == 0 and S % tk == 0
    qi, kj, fl = [], [], []
    for i in range(S // tq):
        last = ((i + 1) * tq - 1) // tk
        order = list(range(last, -1, -1) if descending else range(last + 1))
        for pos, kb in enumerate(order):
            qi.append(i)
            kj.append(kb)
            fl.append((MASKED if (kb + 1) * tk - 1 > i * tq else 0) | (FIRST if pos == 0 else 0) | (LAST if pos == last else 0))
    return tuple(jnp.asarray(np.asarray(a, np.int32)) for a in (qi, kj, fl))


def _head_blk(rows, by_key, head0, width=HD):
    if by_key:
        return pl.BlockSpec((1, rows, width), lambda h, n, qi, kj, fl: (h + head0, kj[n], 0))
    return pl.BlockSpec((1, rows, width), lambda h, n, qi, kj, fl: (h + head0, qi[n], 0))


def _att_specs(tq, tk, width=HD):
    qblk = pl.BlockSpec((1, tq, width), lambda h, n, qi, kj, fl: (h, qi[n], 0))
    kblk = pl.BlockSpec((1, tk, width), lambda h, n, qi, kj, fl: (h, kj[n], 0))
    qcol = pl.BlockSpec((1, tq, 1), lambda h, n, qi, kj, fl: (h, qi[n], 0))
    return qblk, kblk, qcol


def _causal(tq, w, ahead, strict):
    diff = lax.broadcasted_iota(jnp.int32, (tq, w), 1) - lax.broadcasted_iota(jnp.int32, (tq, w), 0)
    return diff < ahead if strict else diff <= ahead


def _masked_or_not(flags, step):
    pl.when(flags % 2 == 1)(functools.partial(step, True))
    pl.when(flags % 2 == 0)(functools.partial(step, False))


FOX_DEPTH = 2 * HD


def _fox_operands(qkv, cum, name, *, tm=512):
    S = qkv.shape[1]
    tm = _tile(S, tm, 16)

    def body(q_ref, k_ref, c_ref, q2_ref, k2_ref):
        lane = lax.broadcasted_iota(jnp.int32, (1, HD), 1)
        c = c_ref[...]
        for h in range(NH):
            pieces = [p.astype(F32) for p in _split3(c[:, h:h + 1])]
            qa = sum(jnp.where(lane == 2 * n, pieces[n], 0.0) for n in range(3)) + jnp.where((lane < 6) & (lane % 2 == 1), 1.0, 0.0)
            ka = sum(jnp.where(lane == 2 * n + 1, -pieces[n], 0.0) for n in range(3)) + jnp.where((lane < 6) & (lane % 2 == 0), 1.0, 0.0)
            q2_ref[h] = jnp.concatenate([qa.astype(CDT), q_ref[h] * ATT_SCALE], axis=1)
            k2_ref[h] = jnp.concatenate([ka.astype(CDT), k_ref[h]], axis=1)

    wide = pl.BlockSpec((NH, tm, FOX_DEPTH), lambda i: (0, i, 0))
    shp = jax.ShapeDtypeStruct((NH, S, FOX_DEPTH), CDT)
    return pl.pallas_call(body, name=name, grid=(S // tm,),
                          in_specs=[pl.BlockSpec((NH, tm, HD), lambda i: (0, i, 0)), pl.BlockSpec((NH, tm, HD), lambda i: (1, i, 0)),
                                    pl.BlockSpec((tm, 128), lambda i: (i, 0))],
                          out_specs=(wide, wide), out_shape=(shp, shp), compiler_params=_params(("parallel",)))(qkv, qkv, cum)


def _fox_fwd(q2, k2, v, name, slots=None, v0=0):
    S = q2.shape[1]
    tq, tk = _att_tiles("fox_fwd", S)
    qi, kj, fl = _pairs(S, tq, tk)
    qblk, kblk, qcol = _att_specs(tq, tk)
    q2blk, k2blk, _ = _att_specs(tq, tk, FOX_DEPTH)
    npairs = int(qi.shape[0])

    def body(qi_ref, kj_ref, fl_ref, q2_ref, k2_ref, v_ref, *rest):
        if slots is None:
            o_ref, lse_ref, m_s, l_s, acc_s = rest
        else:
            _, o_ref, lse_ref, slots_ref, m_s, l_s, acc_s, send_sems, recv_sems = rest
        h, n = pl.program_id(0), pl.program_id(1)
        i, kb, flags = qi_ref[n], kj_ref[n], fl_ref[n]
        if slots is not None:
            _gather_steps(slots_ref, send_sems, recv_sems, first=(h == 0) & (n == 0), middle=(h == NH // 2) & (n == 0),
                          last=(h == NH - 1) & (n == npairs - 1))

        @pl.when(flags & FIRST != 0)
        def _():
            m_s[...] = jnp.full_like(m_s, NEG)
            l_s[...] = jnp.zeros_like(l_s)
            acc_s[...] = jnp.zeros_like(acc_s)

        def step(masked):
            s = _dot_nt(q2_ref[0], k2_ref[0])
            if masked:
                s = jnp.where(_causal(tq, tk, i * tq - kb * tk, False), s, NEG)
            m_new = jnp.maximum(m_s[...], jnp.max(s, axis=-1, keepdims=True))
            alpha = jnp.exp(m_s[...] - m_new)
            p = jnp.exp(s - m_new)
            l_s[...] = alpha * l_s[...] + jnp.sum(p, axis=-1, keepdims=True)
            acc_s[...] = alpha * acc_s[...] + _dot(p.astype(CDT), v_ref[0])
            m_s[...] = m_new

        _masked_or_not(flags, step)

        @pl.when(flags & LAST != 0)
        def _():
            o_ref[0] = acc_s[...] / l_s[...]
            lse_ref[0] = m_s[...] + jnp.log(l_s[...])

    scratch = [pltpu.VMEM((tq, 1), F32), pltpu.VMEM((tq, 1), F32), pltpu.VMEM((tq, HD), F32)]
    out_shape = (jax.ShapeDtypeStruct((NH, S, HD), F32), jax.ShapeDtypeStruct((NH, S, 1), F32))
    in_specs = [q2blk, k2blk, _head_blk(tk, True, v0)]
    if slots is None:
        grid_spec = pltpu.PrefetchScalarGridSpec(num_scalar_prefetch=3, grid=(NH, npairs), in_specs=in_specs,
                                                 out_specs=(qblk, qcol), scratch_shapes=scratch)
        o, lse = pl.pallas_call(body, name=name, grid_spec=grid_spec, out_shape=out_shape,
                                compiler_params=_params(("parallel", "arbitrary")))(qi, kj, fl, q2, k2, v)
        return o, lse, None
    grid_spec = pltpu.PrefetchScalarGridSpec(num_scalar_prefetch=3, grid=(NH, npairs), in_specs=in_specs + [ANY],
                                             out_specs=(qblk, qcol, ANY), scratch_shapes=scratch + list(GATHER_SEMS))
    return pl.pallas_call(body, name=name, grid_spec=grid_spec, out_shape=(*out_shape, jax.ShapeDtypeStruct(slots.shape, slots.dtype)),
                          input_output_aliases={6: 2},
                          compiler_params=_params(("arbitrary", "arbitrary")))(qi, kj, fl, q2, k2, v, slots)


def _fox_bwd(q, k, v, q2, k2, o, do, lse, name, h0=(0, 0, 0), swap=None):
    S = q.shape[1]
    tq, tk = _att_tiles("fox_bwd", S)
    qi, kj, fl = _pairs(S, tq, tk)
    qblk, kblk, qcol = _att_specs(tq, tk)
    npairs = int(qi.shape[0])

    def body(qi_ref, kj_ref, fl_ref, q_ref, k_ref, v_ref, q2_ref, k2_ref, o_ref, do_ref, lse_ref, *rest):
        if swap is None:
            dq_ref, dk_ref, dv_ref, dck_ref, dcq_ref, dq_s, dl_s, dcq_s = rest
        else:
            g_ref, dq_ref, dk_ref, dv_ref, dck_ref, dcq_ref, got_ref, dq_s, dl_s, dcq_s, send_sem, recv_sem = rest
        n = pl.program_id(1)
        i, kb, flags = qi_ref[n], kj_ref[n], fl_ref[n]
        if swap is not None:
            h = pl.program_id(0)
            _swap_steps(g_ref, got_ref, send_sem, recv_sem, first=(h == 0) & (n == 0), last=(h == NH - 1) & (n == npairs - 1))

        @pl.when(n == 0)
        def _():
            dk_ref[...] = jnp.zeros_like(dk_ref)
            dv_ref[...] = jnp.zeros_like(dv_ref)
            dck_ref[...] = jnp.zeros_like(dck_ref)

        @pl.when(flags & FIRST != 0)
        def _():
            dq_s[...] = jnp.zeros_like(dq_s)
            dcq_s[...] = jnp.zeros_like(dcq_s)
            delta = jnp.sum(do_ref[0].astype(F32) * o_ref[0], axis=-1, keepdims=True)
            lane = lax.broadcasted_iota(jnp.int32, (1, HD), 1)
            aug = sum(jnp.where(lane == n, piece.astype(F32), 0.0) for n, piece in enumerate(_split3(delta)))
            dl_s[...] = jnp.concatenate([do_ref[0], aug.astype(CDT)], axis=1)

        def step(masked):
            qs = q_ref[0] * ATT_SCALE
            do = do_ref[0]
            p = jnp.exp(_dot_nt(q2_ref[0], k2_ref[0]) - lse_ref[0])
            if masked:
                p = jnp.where(_causal(tq, tk, i * tq - kb * tk, False), p, 0.0)
            ds = p * _dot_nt(dl_s[...], v_ref[0])
            dsb = ds.astype(CDT)
            dq_s[...] += _dot(dsb, k_ref[0])
            rows = pl.ds(pl.multiple_of(kb * tk, tk), tk)
            dk_ref[0, rows, :] += _dot_tn(dsb, qs)
            dv_ref[0, rows, :] += _dot_tn(p.astype(CDT), do)
            dck_ref[0, :, rows] += -jnp.sum(ds, axis=0, keepdims=True)
            dcq_s[...] += jnp.sum(ds, axis=-1, keepdims=True)

        _masked_or_not(flags, step)

        @pl.when(flags & LAST != 0)
        def _():
            dq_ref[0] = (dq_s[...] * ATT_SCALE).astype(dq_ref.dtype)
            dcq_ref[0] = dcq_s[...]

    whole = pl.BlockSpec((1, S, HD), lambda h, n, qi, kj, fl: (h, 0, 0))
    q2blk, k2blk, _ = _att_specs(tq, tk, FOX_DEPTH)
    in_specs = [_head_blk(tq, False, h0[0]), _head_blk(tk, True, h0[1]), _head_blk(tk, True, h0[2], 2 * HD), q2blk, k2blk, qblk, qblk, qcol]
    out_specs = (qblk, whole, whole, pl.BlockSpec((1, 1, S), lambda h, n, qi, kj, fl: (h, 0, 0)), qcol)
    out_shape = (jax.ShapeDtypeStruct((NH, S, HD), CDT), jax.ShapeDtypeStruct((NH, S, HD), F32), jax.ShapeDtypeStruct((NH, S, HD), F32),
                 jax.ShapeDtypeStruct((NH, 1, S), F32), jax.ShapeDtypeStruct((NH, S, 1), F32))
    scratch = [pltpu.VMEM((tq, HD), F32), pltpu.VMEM((tq, 2 * HD), CDT), pltpu.VMEM((tq, 1), F32)]
    if swap is None:
        grid_spec = pltpu.PrefetchScalarGridSpec(num_scalar_prefetch=3, grid=(NH, npairs), in_specs=in_specs,
                                                 out_specs=out_specs, scratch_shapes=scratch)
        return (*pl.pallas_call(body, name=name, grid_spec=grid_spec, out_shape=out_shape,
                                compiler_params=_params(("parallel", "arbitrary")))(qi, kj, fl, q, k, v, q2, k2, o, do, lse), None)
    grid_spec = pltpu.PrefetchScalarGridSpec(num_scalar_prefetch=3, grid=(NH, npairs), in_specs=in_specs + [ANY],
                                             out_specs=(*out_specs, ANY), scratch_shapes=scratch + list(SWAP_SEMS))
    got_shape = jax.ShapeDtypeStruct((swap.shape[0], swap.shape[1] // 2, 128), swap.dtype)
    return pl.pallas_call(body, name=name, grid_spec=grid_spec, out_shape=(*out_shape, got_shape),
                          compiler_params=_params(("arbitrary", "arbitrary")))(qi, kj, fl, q, k, v, q2, k2, o, do, lse, swap)


LOG2E = 1.4426950408889634


def _proj_qkv(h1, w_qkv, name, *, tm=1024):
    S, K = h1.shape
    tm = _tile(S, tm, 16)
    SQ, SK = 3, 4

    def heads(t):
        return [t[:, h * HD:(h + 1) * HD] for h in range(NH)]

    def body(a_ref, b_ref, o_ref, q2_ref, k2_ref):
        j = pl.program_id(1)
        ob = _dot(a_ref[...], b_ref[...]).astype(CDT)
        for h, t in enumerate(heads(ob)):
            o_ref[h] = t

        @pl.when(j == SQ)
        def _():
            qf = ob.astype(F32) * (ATT_SCALE * LOG2E)
            hi = qf.astype(CDT)
            lo = (qf - hi.astype(F32)).astype(CDT)
            for h, (th, tl) in enumerate(zip(heads(hi), heads(lo))):
                q2_ref[h] = jnp.concatenate([th, tl], axis=1)

        @pl.when(j == SK)
        def _():
            for h, t in enumerate(heads(ob)):
                k2_ref[h] = jnp.concatenate([t, t], axis=1)

    wide = pl.BlockSpec((NH, tm, 2 * HD), lambda i, j: (0, i, 0))
    return pl.pallas_call(
        body, name=name, grid=(S // tm, 6),
        in_specs=[pl.BlockSpec((tm, K), lambda i, j: (i, 0)), pl.BlockSpec((K, GW), lambda i, j: (0, j))],
        out_specs=(pl.BlockSpec((NH, tm, HD), lambda i, j: (j, i, 0)), wide, wide),
        out_shape=(jax.ShapeDtypeStruct((6 * NH, S, HD), CDT), jax.ShapeDtypeStruct((NH, S, 2 * HD), CDT),
                   jax.ShapeDtypeStruct((NH, S, 2 * HD), CDT)),
        compiler_params=_params(("parallel", "arbitrary")))(h1, w_qkv)


def _sb_softplus2(q2, k2sub, mask):
    z2 = _dot_nt(q2, k2sub)
    sp2 = jnp.maximum(z2, 0.0) + jnp.log2(1.0 + jnp.exp2(-jnp.abs(z2)))
    return z2, sp2 if mask is None else jnp.where(mask, sp2, 0.0)


def _strict_tri(n, upper, value):
    r = lax.broadcasted_iota(jnp.int32, (n, n), 0)
    c = lax.broadcasted_iota(jnp.int32, (n, n), 1)
    return jnp.where(r < c if upper else r > c, value, 0.0).astype(CDT)


def _sb_fwd(q2, k2, v, name, v0=0):
    S = q2.shape[1]
    tq, tk = _att_tiles("sb_fwd", S)
    W = min(W_SB, tk)
    qi, kj, fl = _pairs(S, tq, tk)
    qblk, kblk, qcol = _att_specs(tq, tk)
    q2blk, k2blk, _ = _att_specs(tq, tk, 2 * HD)

    def body(qi_ref, kj_ref, fl_ref, q_ref, k_ref, v_ref, o_ref, lt_ref, run_s, acc_s):
        n = pl.program_id(1)
        i, kb, flags = qi_ref[n], kj_ref[n], fl_ref[n]

        @pl.when(flags & FIRST != 0)
        def _():
            run_s[...] = jnp.zeros_like(run_s)
            acc_s[...] = jnp.zeros_like(acc_s)

        def step(masked):
            neg_later = _strict_tri(W, False, -1.0)
            run = run_s[...]
            acc = acc_s[...]
            for sub in range(tk // W - 1, -1, -1):
                cols = slice(sub * W, (sub + 1) * W)
                mask = _causal(tq, W, i * tq - kb * tk - sub * W, True) if masked else None
                z2, sp2 = _sb_softplus2(q_ref[0], k_ref[0, cols, :], mask)
                excl = _dot(sp2.astype(CDT), neg_later)
                a = jnp.exp2((z2 - sp2) + (excl + run))
                if masked:
                    a = jnp.where(mask, a, 0.0)
                acc = acc + _dot(a.astype(CDT), v_ref[0, cols, :])
                run = run + (excl[:, 0:1] - sp2[:, 0:1])
            run_s[...] = run
            acc_s[...] = acc

        _masked_or_not(flags, step)

        @pl.when(flags & LAST != 0)
        def _():
            o_ref[0] = acc_s[...]
            lt_ref[0] = run_s[...]

    grid_spec = pltpu.PrefetchScalarGridSpec(
        num_scalar_prefetch=3, grid=(NH, int(qi.shape[0])), in_specs=[q2blk, k2blk, _head_blk(tk, True, v0)], out_specs=(qblk, qcol),
        scratch_shapes=[pltpu.VMEM((tq, 1), F32), pltpu.VMEM((tq, HD), F32)])
    return pl.pallas_call(body, name=name, grid_spec=grid_spec,
                          out_shape=(jax.ShapeDtypeStruct((NH, S, HD), F32), jax.ShapeDtypeStruct((NH, S, 1), F32)),
                          compiler_params=_params(("parallel", "arbitrary")))(qi, kj, fl, q2, k2, v)


def _sb_bwd(q, q2, k2, v, do, lt, name, scatter=None, q0=0, v0=0):
    S = q.shape[1]
    tq, tk = _att_tiles("sb_bwd", S)
    W = min(W_SB, tk)
    qi, kj, fl = _pairs(S, tq, tk, descending=False)
    qblk, kblk, qcol = _att_specs(tq, tk)
    q2blk, k2blk, _ = _att_specs(tq, tk, 2 * HD)
    npairs = int(qi.shape[0])

    def body(qi_ref, kj_ref, fl_ref, q_ref, q2_ref, k2_ref, v_ref, do_ref, lt_ref, *rest):
        if scatter is None:
            dq_ref, dk_ref, dv_ref, left_s, gsum_s, dq_s = rest
        else:
            h_ref, dq_ref, dk_ref, dv_ref, recv_ref, left_s, gsum_s, dq_s, send_sems, recv_sems = rest
        n = pl.program_id(1)
        i, kb, flags = qi_ref[n], kj_ref[n], fl_ref[n]
        if scatter is not None:
            h = pl.program_id(0)
            _scatter_steps(h_ref, recv_ref, send_sems, recv_sems, first=(h == 0) & (n == 0), last=(h == NH - 1) & (n == npairs - 1))

        @pl.when(n == 0)
        def _():
            dk_ref[...] = jnp.zeros_like(dk_ref)
            dv_ref[...] = jnp.zeros_like(dv_ref)

        @pl.when(flags & FIRST != 0)
        def _():
            left_s[...] = lt_ref[0]
            gsum_s[...] = jnp.zeros_like(gsum_s)
            dq_s[...] = jnp.zeros_like(dq_s)

        def step(masked):
            qs = q_ref[0] * ATT_SCALE
            do = do_ref[0]
            neg_later = _strict_tri(W, False, -1.0)
            earlier = _strict_tri(W, True, 1.0)
            left = left_s[...]
            gsum = gsum_s[...]
            dq = dq_s[...]
            for sub in range(tk // W):
                cols = slice(sub * W, (sub + 1) * W)
                mask = _causal(tq, W, i * tq - kb * tk - sub * W, True) if masked else None
                ksub = k2_ref[0, cols, 0:HD]
                z2, sp2 = _sb_softplus2(q2_ref[0], k2_ref[0, cols, :], mask)
                excl = _dot(sp2.astype(CDT), neg_later)
                left = left - (excl[:, 0:1] - sp2[:, 0:1])
                t1 = z2 - sp2
                sig = jnp.exp2(t1)
                a = jnp.exp2(t1 + (excl + left))
                if masked:
                    a = jnp.where(mask, a, 0.0)
                dl = _dot_nt(do, v_ref[0, cols, :]) * a
                before = _dot(dl.astype(CDT), earlier)
                dz = dl - sig * (dl + (before + gsum))
                if masked:
                    dz = jnp.where(mask, dz, 0.0)
                dzb = dz.astype(CDT)
                dq = dq + _dot(dzb, ksub)
                rows = pl.ds(pl.multiple_of(kb * tk + sub * W, W), W)
                dk_ref[0, rows, :] += _dot_tn(dzb, qs)
                dv_ref[0, rows, :] += _dot_tn(a.astype(CDT), do)
                gsum = gsum + (before[:, W - 1:W] + dl[:, W - 1:W])
            left_s[...] = left
            gsum_s[...] = gsum
            dq_s[...] = dq

        _masked_or_not(flags, step)

        @pl.when(flags & LAST != 0)
        def _():
            dq_ref[0] = (dq_s[...] * ATT_SCALE).astype(dq_ref.dtype)

    whole = pl.BlockSpec((1, S, HD), lambda h, n, qi, kj, fl: (h, 0, 0))
    in_specs = [_head_blk(tq, False, q0), q2blk, k2blk, _head_blk(tk, True, v0), qblk, qcol]
    out_specs = (qblk, whole, whole)
    out_shape = (jax.ShapeDtypeStruct((NH, S, HD), CDT), jax.ShapeDtypeStruct((NH, S, HD), F32), jax.ShapeDtypeStruct((NH, S, HD), F32))
    scratch = [pltpu.VMEM((tq, 1), F32), pltpu.VMEM((tq, 1), F32), pltpu.VMEM((tq, HD), F32)]
    if scatter is None:
        grid_spec = pltpu.PrefetchScalarGridSpec(num_scalar_prefetch=3, grid=(NH, npairs), in_specs=in_specs,
                                                 out_specs=out_specs, scratch_shapes=scratch)
        return (*pl.pallas_call(body, name=name, grid_spec=grid_spec, out_shape=out_shape,
                                compiler_params=_params(("parallel", "arbitrary")))(qi, kj, fl, q, q2, k2, v, do, lt), None)
    grid_spec = pltpu.PrefetchScalarGridSpec(num_scalar_prefetch=3, grid=(NH, npairs), in_specs=in_specs + [ANY],
                                             out_specs=(*out_specs, ANY), scratch_shapes=scratch + list(SCATTER_SEMS))
    recv_shape = jax.ShapeDtypeStruct((3,) + scatter.shape[1:], scatter.dtype)
    return pl.pallas_call(body, name=name, grid_spec=grid_spec, out_shape=(*out_shape, recv_shape),
                          compiler_params=_params(("arbitrary", "arbitrary")))(qi, kj, fl, q, q2, k2, v, do, lt, scatter)


def _mem_probs(q_ref, kv_ref, h):
    cols = slice(h * MHD, (h + 1) * MHD)
    s = _dot_nt(q_ref[:, cols], kv_ref[:, cols]) * MEM_SCALE
    e = jnp.exp(s - jnp.max(s, axis=-1, keepdims=True))
    return e / jnp.sum(e, axis=-1, keepdims=True)


def _xattn_fwd(q, kv, w_mo, x1, name, *, tm=512):
    S = q.shape[0]
    tm = _tile(S, tm, 16)
    nm = kv.shape[0]

    def body(q_ref, kv_ref, w_ref, x_ref, x2_ref, o_ref):
        for h in range(NMH):
            p = _mem_probs(q_ref, kv_ref, h)
            o_ref[:, h * MHD:(h + 1) * MHD] = _dot(p.astype(CDT), kv_ref[:, D + h * MHD:D + (h + 1) * MHD]).astype(CDT)
        x2_ref[...] = x_ref[...] + _dot(o_ref[...], w_ref[...])

    row = pl.BlockSpec((tm, D), lambda i: (i, 0))
    return pl.pallas_call(body, name=name, grid=(S // tm,),
                          in_specs=[row, pl.BlockSpec((nm, 2 * D), lambda i: (0, 0)), pl.BlockSpec((D, D), lambda i: (0, 0)), row],
                          out_specs=(row, row),
                          out_shape=(jax.ShapeDtypeStruct((S, D), F32), jax.ShapeDtypeStruct((S, D), CDT)),
                          compiler_params=_params(("parallel",)))(q, kv, w_mo, x1)


def _xattn_bwd(q, kv, do, name, *, tm=512):
    S = q.shape[0]
    tm = _tile(S, tm, 16)
    nm = kv.shape[0]

    def body(q_ref, kv_ref, do_ref, dq_ref, dkv_ref):
        @pl.when(pl.program_id(0) == 0)
        def _():
            dkv_ref[...] = jnp.zeros_like(dkv_ref)
        for h in range(NMH):
            cols = slice(h * MHD, (h + 1) * MHD)
            vcols = slice(D + h * MHD, D + (h + 1) * MHD)
            p = _mem_probs(q_ref, kv_ref, h)
            doh = do_ref[:, cols]
            dp = _dot_nt(doh, kv_ref[:, vcols])
            ds = (p * (dp - jnp.sum(p * dp, axis=-1, keepdims=True)) * MEM_SCALE).astype(CDT)
            dq_ref[:, cols] = _dot(ds, kv_ref[:, cols]).astype(CDT)
            dkv_ref[:, cols] += _dot_tn(ds, q_ref[:, cols])
            dkv_ref[:, vcols] += _dot_tn(p.astype(CDT), doh)

    row = pl.BlockSpec((tm, D), lambda i: (i, 0))
    kvs = pl.BlockSpec((nm, 2 * D), lambda i: (0, 0))
    return pl.pallas_call(body, name=name, grid=(S // tm,), in_specs=[row, kvs, row], out_specs=(row, kvs),
                          out_shape=(jax.ShapeDtypeStruct((S, D), CDT), jax.ShapeDtypeStruct((nm, 2 * D), F32)),
                          compiler_params=_params(("arbitrary",)))(q, kv, do)


HALO = 16
SLAB = 8


def _shift_down(u, prev, s):
    rolled = pltpu.roll(u, s, 0)
    top = rolled[0:SLAB]
    r = lax.broadcasted_iota(jnp.int32, top.shape, 0)
    for t in range(s):
        top = jnp.where(r == t, prev[HALO - s + t:HALO - s + t + 1, :], top)
    return jnp.concatenate([top, rolled[SLAB:]], axis=0)


def _shift_up(u, nxt, s):
    n = u.shape[0]
    rolled = pltpu.roll(u, n - s, 0)
    bottom = rolled[n - SLAB:]
    r = lax.broadcasted_iota(jnp.int32, bottom.shape, 0)
    for t in range(s):
        bottom = jnp.where(r == SLAB - s + t, nxt[t:t + 1, :], bottom)
    return jnp.concatenate([rolled[:n - SLAB], bottom], axis=0)


def _conv_taps(u_ref, h_ref, first):
    u = u_ref[...].astype(F32)
    prev = jnp.where(first, 0.0, h_ref[...].astype(F32))
    out = []
    for half in range(2):
        out.append((u[half], _shift_down(u[half], prev[half], 1), _shift_down(u[half], prev[half], 2)))
    return out


def _conv_specs(tm, tn, nsb):
    blk = pl.BlockSpec((2, tm, tn), lambda j, i: (0, i, j))
    prev = pl.BlockSpec((2, HALO, tn), lambda j, i: (0, jnp.maximum(i * (tm // HALO) - 1, 0), j))
    nxt = pl.BlockSpec((2, HALO, tn), lambda j, i: (0, jnp.minimum((i + 1) * (tm // HALO), nsb - 1), j))
    w = pl.BlockSpec((2, 3, tn), lambda j, i: (0, 0, j))
    b = pl.BlockSpec((2, 1, tn), lambda j, i: (0, 0, j))
    return blk, prev, nxt, w, b


def _conv_apply(taps, w_ref, b_ref):
    ys = []
    for half in range(2):
        u, u1, u2 = taps[half]
        w = w_ref[half]
        ys.append(b_ref[half] + u2 * w[0:1, :] + u1 * w[1:2, :] + u * w[2:3, :])
    return ys


def _conv_act(u0, cw, cb, name, *, tm=2048, tn=256):
    _, S, F = u0.shape
    tm = _tile(S, tm, HALO)
    tn = _tile(F, tn, 128)
    blk, prev, _, w, b = _conv_specs(tm, tn, S // HALO)

    def body(u_ref, h_ref, w_ref, b_ref, a_ref):
        yg, yv = _conv_apply(_conv_taps(u_ref, h_ref, pl.program_id(1) == 0), w_ref, b_ref)
        a_ref[...] = (yg * jax.nn.sigmoid(yg) * yv).astype(a_ref.dtype)

    return pl.pallas_call(body, name=name, grid=(F // tn, S // tm), in_specs=[blk, prev, w, b],
                          out_specs=pl.BlockSpec((tm, tn), lambda j, i: (i, j)),
                          out_shape=jax.ShapeDtypeStruct((S, F), CDT),
                          compiler_params=_params(("parallel", "parallel")))(u0, u0, cw, cb)


def _conv_act_bwd(u0, da, cw, cb, name, *, tm=2048, tn=256):
    _, S, F = u0.shape
    tm = _tile(S, tm, HALO)
    tn = _tile(F, tn, 128)
    blk, prev, _, w, b = _conv_specs(tm, tn, S // HALO)

    def body(u_ref, h_ref, da_ref, w_ref, b_ref, du_ref, dwb_ref):
        @pl.when(pl.program_id(1) == 0)
        def _():
            dwb_ref[...] = jnp.zeros_like(dwb_ref)
        taps = _conv_taps(u_ref, h_ref, pl.program_id(1) == 0)
        yg, yv = _conv_apply(taps, w_ref, b_ref)
        sg = jax.nn.sigmoid(yg)
        da = da_ref[...].astype(F32)
        dus = (da * yv * sg * (1.0 + yg * (1.0 - sg)), da * yg * sg)
        for half in range(2):
            du = dus[half]
            du_ref[half] = du.astype(du_ref.dtype)
            u, u1, u2 = taps[half]
            for row, term in enumerate((du * u2, du * u1, du * u, du)):
                dwb_ref[half, row:row + 1, :] += jnp.sum(term, axis=0, keepdims=True)

    return pl.pallas_call(body, name=name, grid=(F // tn, S // tm),
                          in_specs=[blk, prev, pl.BlockSpec((tm, tn), lambda j, i: (i, j)), w, b],
                          out_specs=(blk, pl.BlockSpec((2, 4, tn), lambda j, i: (0, 0, j))),
                          out_shape=(jax.ShapeDtypeStruct((2, S, F), CDT), jax.ShapeDtypeStruct((2, 4, F), F32)),
                          compiler_params=_params(("parallel", "arbitrary")))(u0, u0, da, cw, cb)


def _conv_bwd_input(du, cw, name, *, tm=2048, tn=256):
    _, S, F = du.shape
    tm = _tile(S, tm, HALO)
    tn = _tile(F, tn, 128)
    blk, _, nxt, w, _ = _conv_specs(tm, tn, S // HALO)
    ni = S // tm

    def body(d_ref, h_ref, w_ref, o_ref):
        d = d_ref[...].astype(F32)
        nx = jnp.where(pl.program_id(1) == ni - 1, 0.0, h_ref[...].astype(F32))
        for half in range(2):
            wv = w_ref[half]
            y = d[half] * wv[2:3, :] + _shift_up(d[half], nx[half], 1) * wv[1:2, :] + _shift_up(d[half], nx[half], 2) * wv[0:1, :]
            o_ref[half] = y.astype(o_ref.dtype)

    return pl.pallas_call(body, name=name, grid=(F // tn, ni), in_specs=[blk, nxt, w], out_specs=blk,
                          out_shape=jax.ShapeDtypeStruct((2, S, F), CDT),
                          compiler_params=_params(("parallel", "parallel")))(du, du, cw)


ANY = pl.BlockSpec(memory_space=pl.ANY)


def _place():
    return lax.axis_index("x"), lax.axis_index("y"), lax.axis_index("c")


def _other_chips(x, y):
    return ((1 - x, y), (x, 1 - y), (1 - x, 1 - y))


def _when(pred, fn):
    if pred is True:
        fn()
    else:
        pl.when(pred)(fn)


GATHER_SEMS = (pltpu.SemaphoreType.DMA((6,)), pltpu.SemaphoreType.DMA((6,)))
SCATTER_SEMS = (pltpu.SemaphoreType.DMA((3,)), pltpu.SemaphoreType.DMA((3,)))


def _gather_steps(out_ref, send_sems, recv_sems, first=True, middle=True, last=True):
    half = out_ref.shape[1] // 2
    x, y, c = _place()
    chips = _other_chips(x, y)

    def part(chip, pc):
        return out_ref.at[2 * chip[0] + chip[1], pl.ds(pl.multiple_of(pc * half, 16), half), :]

    def copy(k, chip, pc, to):
        return pltpu.make_async_remote_copy(src_ref=part(chip, pc), dst_ref=part(chip, pc),
                                            send_sem=send_sems.at[k], recv_sem=recv_sems.at[k],
                                            device_id=to, device_id_type=MESH)

    def send_mine():
        for j, chip in enumerate(chips):
            copy(j, (x, y), c, (*chip, c)).start()

    def pass_on():
        for j, chip in enumerate(chips):
            copy(j, chip, c, (x, y, c)).wait_recv()
            copy(3 + j, chip, c, (x, y, 1 - c)).start()

    def finish():
        for j, chip in enumerate(chips):
            copy(3 + j, chip, 1 - c, (x, y, c)).wait_recv()
        for j, chip in enumerate(chips):
            copy(j, (x, y), c, (*chip, c)).wait_send()
            copy(3 + j, chip, c, (x, y, 1 - c)).wait_send()

    _when(first, send_mine)
    _when(middle, pass_on)
    _when(last, finish)


def _gather_weights(buf):
    def body(buf_ref, out_ref, send_sems, recv_sems):
        del buf_ref
        _gather_steps(out_ref, send_sems, recv_sems)

    return pl.pallas_call(body, name="gather_weights", in_specs=[ANY], out_specs=ANY,
                          out_shape=jax.ShapeDtypeStruct(buf.shape, buf.dtype), input_output_aliases={0: 0},
                          scratch_shapes=list(GATHER_SEMS))(buf)


def _gather_small(v):
    m = v.shape[0]

    def body(v_ref, out_ref, send_sems, recv_sems, local_sem):
        x, y, c = _place()
        me, sibling = (x, y, c), (x, y, 1 - c)
        chips = _other_chips(x, y)

        def rows(px, py, pc):
            return out_ref.at[pl.ds((4 * px + 2 * py + pc) * m, m), :]

        def copy(k, block, to, src=None):
            return pltpu.make_async_remote_copy(src_ref=rows(*block) if src is None else src, dst_ref=rows(*block),
                                                send_sem=send_sems.at[k], recv_sem=recv_sems.at[k],
                                                device_id=to, device_id_type=MESH)

        mine = pltpu.make_async_copy(v_ref, rows(*me), local_sem)
        mine.start()
        first = [copy(0, me, sibling, src=v_ref)]
        first += [copy(1 + j, me, (*chip, c), src=v_ref) for j, chip in enumerate(chips)]
        for cp in first:
            cp.start()
        passed = [copy(4 + j, (*chip, c), sibling) for j, chip in enumerate(chips)]
        for j, chip in enumerate(chips):
            copy(1 + j, (*chip, c), me).wait_recv()
            passed[j].start()
        copy(0, sibling, me).wait_recv()
        for j, chip in enumerate(chips):
            copy(4 + j, (*chip, 1 - c), me).wait_recv()
        for cp in first + passed:
            cp.wait_send()
        mine.wait()

    vm = pl.BlockSpec(memory_space=pltpu.VMEM)
    return pl.pallas_call(body, name="gather_small", in_specs=[vm], out_specs=vm,
                          out_shape=jax.ShapeDtypeStruct((8 * m, 128), v.dtype),
                          scratch_shapes=[pltpu.SemaphoreType.DMA((7,)), pltpu.SemaphoreType.DMA((7,)), pltpu.SemaphoreType.DMA])(v)


SWAP_SEMS = (pltpu.SemaphoreType.DMA, pltpu.SemaphoreType.DMA)


def _swap_steps(g_ref, out_ref, send_sem, recv_sem, first=True, last=True):
    half = out_ref.shape[1]
    x, y, c = _place()

    def copy():
        src = g_ref.at[:, pl.ds(pl.multiple_of((1 - c) * half, 8), half), :]
        return pltpu.make_async_remote_copy(src_ref=src, dst_ref=out_ref, send_sem=send_sem, recv_sem=recv_sem,
                                            device_id=(x, y, 1 - c), device_id_type=MESH)

    _when(first, lambda: copy().start())
    _when(last, lambda: copy().wait())


def _swap_halves(g, name):
    n, rows, _ = g.shape

    def body(g_ref, out_ref, send_sem, recv_sem):
        _swap_steps(g_ref, out_ref, send_sem, recv_sem)

    return pl.pallas_call(body, name=name, in_specs=[ANY], out_specs=ANY,
                          out_shape=jax.ShapeDtypeStruct((n, rows // 2, 128), g.dtype),
                          scratch_shapes=list(SWAP_SEMS))(g)


def _scatter_steps(h_ref, out_ref, send_sems, recv_sems, first=True, last=True):
    x, y, c = _place()

    def copies():
        return [pltpu.make_async_remote_copy(src_ref=h_ref.at[2 * chip[0] + chip[1]], dst_ref=out_ref.at[j],
                                             send_sem=send_sems.at[j], recv_sem=recv_sems.at[j],
                                             device_id=(*chip, c), device_id_type=MESH)
                for j, chip in enumerate(_other_chips(x, y))]

    def start():
        for cp in copies():
            cp.start()

    def finish():
        for cp in copies():
            cp.wait()

    _when(first, start)
    _when(last, finish)


def _join_halves(buf, name):
    half = buf.shape[0] // 2

    def body(buf_ref, out_ref, send_sem, recv_sem):
        del buf_ref
        x, y, c = _place()
        mine = out_ref.at[pl.ds(pl.multiple_of(c * half, 8), half), :]
        other = out_ref.at[pl.ds(pl.multiple_of((1 - c) * half, 8), half), :]
        cp = pltpu.make_async_remote_copy(src_ref=mine, dst_ref=mine, send_sem=send_sem, recv_sem=recv_sem,
                                          device_id=(x, y, 1 - c), device_id_type=MESH)
        cp.start()
        cp.wait_send()
        pltpu.make_async_remote_copy(src_ref=other, dst_ref=other, send_sem=send_sem, recv_sem=recv_sem,
                                     device_id=(x, y, 1 - c), device_id_type=MESH).wait_recv()

    return pl.pallas_call(body, name=name, in_specs=[ANY], out_specs=ANY,
                          out_shape=jax.ShapeDtypeStruct(buf.shape, buf.dtype), input_output_aliases={0: 0},
                          scratch_shapes=[pltpu.SemaphoreType.DMA, pltpu.SemaphoreType.DMA])(buf)


def _add_sibling(g, recv, c_idx, name):
    n, rows, _ = g.shape
    half = rows // 2
    tr = _tile(half, ADAM_ROWS, 16)
    nb = half // tr

    def body(c_ref, g_ref, r_ref, o_ref, ob_ref):
        s = g_ref[...] + r_ref[...]
        o_ref[...] = s
        ob_ref[...] = s.astype(CDT)

    out = pl.BlockSpec((None, tr, 128), lambda k, i, c: (k, i, 0))
    grid_spec = pltpu.PrefetchScalarGridSpec(
        num_scalar_prefetch=1, grid=(n, nb),
        in_specs=[pl.BlockSpec((None, tr, 128), lambda k, i, c: (k, c[0] * nb + i, 0)), out],
        out_specs=(out, out))
    return pl.pallas_call(body, name=name, grid_spec=grid_spec,
                          out_shape=(jax.ShapeDtypeStruct((n, half, 128), F32), jax.ShapeDtypeStruct((n, half, 128), CDT)),
                          compiler_params=_params(("parallel", "parallel")))(c_idx, g, recv)


def _add_chips(hsum, recv, chip_idx, name):
    n, half, _ = hsum.shape
    tr = _tile(half, ADAM_ROWS, 16)

    def body(k_ref, h_ref, r_ref, o_ref):
        o_ref[...] = ((h_ref[...] + r_ref[0].astype(F32)) + r_ref[1].astype(F32)) + r_ref[2].astype(F32)

    grid_spec = pltpu.PrefetchScalarGridSpec(
        num_scalar_prefetch=1, grid=(half // tr,),
        in_specs=[pl.BlockSpec((None, tr, 128), lambda i, k: (k[0], i, 0)),
                  pl.BlockSpec((3, tr, 128), lambda i, k: (0, i, 0))],
        out_specs=pl.BlockSpec((tr, 128), lambda i, k: (i, 0)))
    return pl.pallas_call(body, name=name, grid_spec=grid_spec, out_shape=jax.ShapeDtypeStruct((half, 128), F32),
                          compiler_params=_params(("parallel",)))(chip_idx, hsum, recv)


def _adamw_math(g, w, m, v):
    m2 = B1 * m + (1.0 - B1) * g
    v2 = B2 * v + (1.0 - B2) * (g * g)
    delta = -LR * ((m2 / BC1) / (jnp.sqrt(v2 / BC2) + AEPS) + WD * w)
    return delta, m2, v2


def _adamw(g, w, m, v, name):
    rows, cols = g.shape
    tr = _tile(rows, max(8, (ADAM_ROWS * 128 // cols) // 8 * 8), 8)

    def body(g_ref, w_ref, m_ref, v_ref, d_ref, m2_ref, v2_ref):
        d_ref[...], m2_ref[...], v2_ref[...] = _adamw_math(g_ref[...], w_ref[...], m_ref[...], v_ref[...])

    blk = pl.BlockSpec((None, tr, cols), lambda i: (0, i, 0))
    shp = jax.ShapeDtypeStruct((1, rows, cols), F32)
    return pl.pallas_call(body, name=name, grid=(rows // tr,), in_specs=[pl.BlockSpec((tr, cols), lambda i: (i, 0))] + [blk] * 3,
                          out_specs=(blk,) * 3, out_shape=(shp,) * 3, compiler_params=_params(("parallel",)))(g, w, m, v)


def _adamw_small(parts, w, m, v, name):
    rows = w.shape[0]

    def body(p_ref, w_ref, m_ref, v_ref, g_ref, d_ref, m2_ref, v2_ref):
        g = p_ref[0]
        for k in range(1, 8):
            g = g + p_ref[k]
        g_ref[...] = g
        d_ref[...], m2_ref[...], v2_ref[...] = _adamw_math(g, w_ref[...], m_ref[...], v_ref[...])

    shp = jax.ShapeDtypeStruct((rows, 128), F32)
    return pl.pallas_call(body, name=name, out_shape=(shp,) * 4)(parts, w, m, v)


def _pack_rows(parts, rows):
    flat = jnp.concatenate([p.reshape(-1) for p in parts])
    return jnp.pad(flat, (0, rows * 128 - flat.shape[0])).reshape(rows, 128)


def _unpack(flat, sizes, shapes):
    out, off = [], 0
    for n, s in zip(sizes, shapes):
        out.append(flat[off:off + n].reshape(s))
        off += n
    return out


def _to_shards(full, shard_shape, axis):
    if axis == 0:
        return full.reshape(N_CHIP, -1)
    r, cs = shard_shape
    return full.reshape(r, N_CHIP, cs).transpose(1, 0, 2).reshape(N_CHIP, -1)


def _from_shards(sh, shard_shape, axis):
    r, cs = shard_shape
    if axis == 0:
        return sh.reshape(N_CHIP * r, cs)
    return sh.reshape(N_CHIP, r, cs).transpose(1, 0, 2).reshape(r, N_CHIP * cs)


def _local_step(x0, mem, tgt, W, gains, ex=None):
    S = x0.shape[0]
    w_in = jnp.pad(W["w_in"], ((0, 0), (0, IN_PAD - IN_COLS)))
    b_f = jnp.pad(gains["b_forget"], ((0, 0), (0, 128 - NH)))

    h1 = _rms_cast(x0, gains["attn_norm_g"], "norm_attn")
    qkv, sq2, sk2 = _proj_qkv(h1, w_in[:, :NQKV], "proj_qkv")
    fox, sb_q, sb_v = (0, NH, 2 * NH), 3 * NH, 5 * NH
    fl = _mm_nn(h1, w_in[:, NQKV:NQKV + 128], F32, "proj_gate")
    cum = _gate_fwd(fl, b_f, "gate_cumsum")
    fq2, fk2 = _fox_operands(qkv, cum, "fox_operands")
    fo_h, lse, gathered = _fox_fwd(fq2, fk2, qkv, "fox_fwd", slots=None if ex is None else ex.slots("b"), v0=fox[2])
    if ex is not None:
        W = {**W, **ex.unpack("b", gathered)}
    cw = W["conv_w"].reshape(3, 2, DFF).transpose(1, 0, 2)
    cb = gains["conv_b"].reshape(2, 1, DFF)
    so_h, s_lt = _sb_fwd(sq2, sk2, qkv, "sb_fwd", v0=sb_v)
    x1, mixed = _out_proj(fo_h, so_h, gains["fox_out_g"], gains["sb_out_g"], W["w_out"], x0, "out_proj")

    h2 = _rms_cast(x1, gains["xattn_norm_g"], "norm_xattn")
    mn = _rms_cast(mem, gains["mem_norm_g"], "norm_mem")
    mq = _mm_nn(h2, W["w_mq"], CDT, "proj_mq")
    kv = _mm_nn(mn, W["w_mkv"], CDT, "proj_mkv")
    x2, mo = _xattn_fwd(mq, kv, W["w_mo"], x1, "xattn_fwd")

    h3 = _rms_cast(x2, gains["ffn_norm_g"], "norm_ffn")
    u0 = _mm_nn(h3, W["w_up"], CDT, "ffn_up", tm=512, tn=DFF, halves=True)
    act = _conv_act(u0, cw, cb, "conv_act")
    x3 = _mm_nn(act, W["w_down"], F32, "ffn_down", tm=512, residual=x2)
    loss, dx3, dg_final = _loss_bwd(x3, tgt, gains["final_norm_g"].reshape(1, D), "loss")

    gw, gs = {}, {"final_norm_g": dg_final}
    da = _mm_nt(dx3, W["w_down"], "ffn_down_dx", tn=DFF, out_dtype=CDT)
    gw["w_down"] = _mm_tn(act, dx3, "ffn_down_dw", tka=DFF)
    du, dwb = _conv_act_bwd(u0, da, cw, cb, "conv_act_bwd")
    gw["conv_w"] = dwb[:, :3].transpose(1, 0, 2).reshape(3, 2 * DFF)
    gs["conv_b"] = dwb[:, 3].reshape(1, 2 * DFF)
    du0 = _conv_bwd_input(du, cw, "conv_bwd_input")
    gw["w_up"] = _mm_tn(h3, du0, "ffn_up_dw", tn=DFF, b_halves=True)
    dx2, gs["ffn_norm_g"] = _mm_nt_rmsbwd(du0, W["w_up"], x2, gains["ffn_norm_g"], dx3, "ffn_up_dx", tk=DFF, a_halves=True)

    dmo = _mm_nt(dx2, W["w_mo"], "mo_dx", tn=512, out_dtype=CDT)
    gw["w_mo"] = _mm_tn(mo, dx2, "mo_dw")
    dmq, dkv = _xattn_bwd(mq, kv, dmo, "xattn_bwd")
    gw["w_mq"] = _mm_tn(h2, dmq, "mq_dw")
    dx1, gs["xattn_norm_g"] = _mm_nt_rmsbwd(dmq, W["w_mq"], x1, gains["xattn_norm_g"], dx2, "mq_dx")
    gw["w_mkv"] = _mm_tn(mn, dkv, "mkv_dw")
    _, gs["mem_norm_g"] = _mm_nt_rmsbwd(dkv, W["w_mkv"], mem, gains["mem_norm_g"], jnp.zeros_like(mem), "mkv_dx")

    gw["w_out"] = _mm_tn(mixed, dx1, "out_dw")
    dfo_h, dso_h, gs["fox_out_g"], gs["sb_out_g"] = _out_proj_bwd(dx1, W["w_out"], fo_h, so_h, gains["fox_out_g"], gains["sb_out_g"], "out_dx")
    flat = None if ex is None else ex.flat("b", gw)
    minus_ones = jnp.broadcast_to(jnp.where(jnp.arange(HD) < 3, -1.0, 0.0).astype(CDT), (NH, S, HD))
    fv2 = jnp.concatenate([qkv[fox[2]:fox[2] + NH], minus_ones], axis=-1)
    dfq, dfk, dfv, dck, dcq, got = _fox_bwd(qkv, qkv, fv2, fq2, fk2, fo_h, dfo_h, lse, "fox_bwd", h0=(fox[0], fox[1], 0), swap=flat)
    pair, pair16 = (None, None) if ex is None else ex.pair_sums("b", flat, got)
    dsq, dsk, dsv, arrived = _sb_bwd(qkv, sq2, sk2, qkv, dso_h, s_lt, "sb_bwd", scatter=pair16, q0=sb_q, v0=sb_v)
    dfl, db = _gate_bwd(jnp.pad(dck[:, 0, :].T, ((0, 0), (0, 128 - NH))), dcq, fl, b_f, "gate_bwd")
    gs["b_forget"] = db[:, :NH]
    dqkv = jnp.concatenate([dfq, dfk.astype(CDT), dfv.astype(CDT), dsq, dsk.astype(CDT), dsv.astype(CDT)], axis=0)
    dproj = jnp.concatenate([dqkv.transpose(1, 0, 2).reshape(S, NQKV), dfl.astype(CDT),
                             jnp.zeros((S, IN_PAD - NQKV - 128), CDT)], axis=1)
    gw["w_in"] = _mm_tn(h1, dproj, "in_dw", tn=IN_PAD)[:, :IN_COLS]
    if ex is None:
        dx0, gs["attn_norm_g"] = _mm_nt_rmsbwd(dproj, w_in, x0, gains["attn_norm_g"], dx1, "in_dx", tk=IN_PAD)
        return loss, dx0, gw, gs, None
    pair_a, pair16_a = ex.pair_sums("a", ex.flat("a", gw))
    dx0, gs["attn_norm_g"], arrived_a = _mm_nt_rmsbwd(dproj, w_in, x0, gains["attn_norm_g"], dx1, "in_dx", tk=IN_PAD, scatter=pair16_a)
    return loss, dx0, gw, gs, {"a": (pair_a, arrived_a), "b": (pair, arrived)}


NAMES = ("attn_norm_g", "w_in", "b_forget", "fox_out_g", "sb_out_g", "w_out", "xattn_norm_g", "mem_norm_g", "w_mq",
         "w_mkv", "w_mo", "ffn_norm_g", "w_up", "conv_w", "conv_b", "w_down", "final_norm_g")


class _Exchange:
    def __init__(self, w):
        self.w = w
        xi, yi, ci = _place()
        self.core = ci
        self.chip = 2 * xi + yi
        self.core_idx = jnp.reshape(ci, (1,)).astype(jnp.int32)
        self.chip_idx = jnp.reshape(self.chip, (1,)).astype(jnp.int32)

    def slots(self, g):
        parts = []
        for name, shape, _ in GROUPS[g]:
            blk = self.w[name].reshape(shape)
            parts.append(lax.bitcast_convert_type(blk, CDT) if name == "conv_w" else blk.astype(CDT))
        rows = _rows_g(GROUPS[g])
        return lax.dynamic_update_slice(lax.empty((N_CHIP, rows, 128), CDT), _pack_rows(parts, rows)[None], (self.chip, 0, 0))

    def unpack(self, g, gathered):
        flat, full, off = gathered.reshape(N_CHIP, -1), {}, 0
        for (name, shape, axis), n in zip(GROUPS[g], _gather_sizes(GROUPS[g])):
            sh = flat[:, off:off + n]
            off += n
            if name == "conv_w":
                sh = lax.bitcast_convert_type(sh.reshape(N_CHIP, n // 2, 2), F32)
            full[name] = _from_shards(sh, shape, axis)
        return full

    def flat(self, g, gw):
        rows = _rows_f(GROUPS[g])
        flat = jnp.concatenate([_to_shards(gw[name], shape, axis) for name, shape, axis in GROUPS[g]], axis=1)
        return jnp.pad(flat, ((0, 0), (0, rows * 128 - flat.shape[1]))).reshape(N_CHIP, rows, 128)

    def pair_sums(self, g, flat, got=None):
        if got is None:
            got = _swap_halves(flat, "swap_halves_" + g)
        return _add_sibling(flat, got, self.core_idx, "add_sibling_" + g)

    def finish(self, g, pair, arrived):
        rows = _rows_f(GROUPS[g])
        mine = _add_chips(pair, arrived, self.chip_idx, "add_chips_" + g)
        whole = _join_halves(lax.dynamic_update_slice(lax.empty((rows, 128), F32), mine, (self.core * (rows // 2), 0)), "join_halves_" + g)
        shapes = [s for _, s, _ in GROUPS[g]]
        return {name: arr for (name, _, _), arr in zip(GROUPS[g], _unpack(whole.reshape(-1), _sizes(GROUPS[g]), shapes))}


def _step(x, mem, loss_target, w, m, v):
    ex = _Exchange(w)

    W = ex.unpack("a", _gather_weights(ex.slots("a")))
    gains = {name: w[name].reshape(1, -1) for name, _ in SMALL}

    loss, grad_x, gw, gs, reduced = _local_step(x[0], mem[0], loss_target[0], W, gains, ex)

    grads = {**ex.finish("b", *reduced["b"]), **ex.finish("a", *reduced["a"])}
    small = jnp.concatenate([gs[name].reshape(-1) for name, _ in SMALL] + [loss[0, :1]])
    small = jnp.pad(small, (0, ROWS_S * 128 - P_SMALL)).reshape(ROWS_S, 128)
    small_parts = _gather_small(small).reshape(8, ROWS_S, 128)

    def flat_small(d):
        return _pack_rows([d[name] for name, _ in SMALL], ROWS_S)

    outs = {}
    for name, shape, _ in BIG:
        g = grads[name]
        res = _adamw(g, w[name], m[name], v[name], "adamw_" + name)
        for prefix, arr in zip(("grad_", "delta_", "new_m_", "new_v_"), (g, *res)):
            outs[prefix + name] = arr.reshape(w[name].shape)
    small_res = _adamw_small(small_parts, flat_small(w), flat_small(m), flat_small(v), "adamw_small")
    g_sm = small_res[0]
    for prefix, sm in zip(("grad_", "delta_", "new_m_", "new_v_"), small_res):
        for (name, n), arr in zip(SMALL, _unpack(sm.reshape(-1), [n for _, n in SMALL], [(n,) for _, n in SMALL])):
            outs[prefix + name] = arr.reshape(w[name].shape)
    total_loss = g_sm.reshape(-1)[P_SMALL - 1]
    return (total_loss, grad_x[None], *[outs[p + n] for p in ("grad_", "delta_", "new_m_", "new_v_") for n in NAMES])


def kernel(x, mem, attn_norm_g, w_in, b_forget, fox_out_g, sb_out_g, w_out, xattn_norm_g, mem_norm_g, w_mq, w_mkv, w_mo, ffn_norm_g, w_up, conv_w, conv_b, w_down, final_norm_g, loss_target, m_attn_norm_g, m_w_in, m_b_forget, m_fox_out_g, m_sb_out_g, m_w_out, m_xattn_norm_g, m_mem_norm_g, m_w_mq, m_w_mkv, m_w_mo, m_ffn_norm_g, m_w_up, m_conv_w, m_conv_b, m_w_down, m_final_norm_g, v_attn_norm_g, v_w_in, v_b_forget, v_fox_out_g, v_sb_out_g, v_w_out, v_xattn_norm_g, v_mem_norm_g, v_w_mq, v_w_mkv, v_w_mo, v_ffn_norm_g, v_w_up, v_conv_w, v_conv_b, v_w_down, v_final_norm_g):
    given = dict(locals())
    w = {n: given[n] for n in NAMES}
    m = {n: given["m_" + n] for n in NAMES}
    v = {n: given["v_" + n] for n in NAMES}
    return _step(x, mem, loss_target, w, m, v)
```
